```python
import jax, jax.numpy as jnp
from jax import lax
import numpy as np

D_MODEL = 2048
BATCH = 8
SEQ = 4096
DEPTH = 1

HEAD_DIM = 128
HEADS_PER_GROUP = 4
DILATED_GROUPS = ((128, 1), (512, 4), (2048, 16))
N_ATT_GROUPS = 3
N_ATT_HEADS = N_ATT_GROUPS * HEADS_PER_GROUP
ATT_WIDTH = N_ATT_HEADS * HEAD_DIM
ATT_OUT_WIDTH = HEADS_PER_GROUP * HEAD_DIM
ROPE_DIM = HEAD_DIM // 4
ROPE_THETA = 500000.0
ATT_BLOCK = 128

SG_CHUNK = 128
SG_GROUPS = 12
SG_GROUP_DIM = 128
SG_WIDTH = SG_GROUPS * SG_GROUP_DIM

D_FF = 5632

NORM_EPS = 1e-6
LN_EPS = 1e-5
IN_WIDTH = 3 * ATT_WIDTH + 2 * SG_WIDTH + 2 * D_MODEL

kernel_name = "dilated_attn_gmlp_gated_macaron_layer"


def rmsnorm(x, g):
    x32 = x.astype(jnp.float32)
    r = x32 * lax.rsqrt(jnp.mean(x32 * x32, axis=-1, keepdims=True) + NORM_EPS)
    return (r * g.astype(jnp.float32)).astype(x.dtype)


def layernorm(x, g, b):
    x32 = x.astype(jnp.float32)
    mu = jnp.mean(x32, axis=-1, keepdims=True)
    var = jnp.mean(jnp.square(x32 - mu), axis=-1, keepdims=True)
    y = (x32 - mu) * lax.rsqrt(var + LN_EPS)
    return (y * g.astype(jnp.float32) + b.astype(jnp.float32)).astype(x.dtype)


def swiglu(x, w_gate, w_up, w_down):
    return (jax.nn.silu(x @ w_gate) * (x @ w_up)) @ w_down


def partial_rope(t, pos):
    half = ROPE_DIM // 2
    inv_freq = ROPE_THETA ** (-jnp.arange(0, ROPE_DIM, 2, dtype=jnp.float32) / ROPE_DIM)
    ang = pos.astype(jnp.float32)[:, None] * inv_freq[None, :]
    ang = ang.reshape((1, t.shape[1]) + (1,) * (t.ndim - 3) + (half,))
    cos = jnp.cos(ang).astype(t.dtype)
    sin = jnp.sin(ang).astype(t.dtype)
    x1 = t[..., :half]
    x2 = t[..., half:ROPE_DIM]
    return jnp.concatenate([x1 * cos - x2 * sin, x2 * cos + x1 * sin, t[..., ROPE_DIM:]], axis=-1)


def dilated_window_attention(q, k, v, window, dilation):
    b, s, h, hd = q.shape
    L = s // dilation
    n_blk = -(-L // ATT_BLOCK)
    Lp = n_blk * ATT_BLOCK
    reach = window // dilation

    def by_residue(t):
        t = t.reshape(b, L, dilation, h, hd).transpose(0, 3, 2, 1, 4)
        t = jnp.pad(t, ((0, 0), (0, 0), (0, 0), (0, Lp - L), (0, 0)))
        return t.reshape(b, h, dilation, n_blk, ATT_BLOCK, hd)

    qb, kb, vb = by_residue(q), by_residue(k), by_residue(v)

    def with_prev(t):
        prev = jnp.pad(t[:, :, :, :-1], ((0, 0), (0, 0), (0, 0), (1, 0), (0, 0), (0, 0)))
        return jnp.concatenate([prev, t], axis=4)

    kw, vw = with_prev(kb), with_prev(vb)
    scores = jnp.einsum('bhrnqd,bhrnkd->bhrnqk', qb, kw).astype(jnp.float32) * (hd ** -0.5)
    qi = jnp.arange(ATT_BLOCK)[None, :, None]
    kj = jnp.arange(2 * ATT_BLOCK)[None, None, :]
    blk = jnp.arange(n_blk)[:, None, None]
    diff = qi + ATT_BLOCK - kj
    key_idx = blk * ATT_BLOCK - ATT_BLOCK + kj
    valid = (diff >= 0) & (diff <= reach) & (key_idx >= 0)
    scores = jnp.where(valid[None, None, None], scores, -jnp.inf)
    m = jnp.max(scores, axis=-1, keepdims=True)
    p = jnp.exp(scores - m)
    l = jnp.sum(p, axis=-1, keepdims=True)
    o = jnp.einsum('bhrnqk,bhrnkd->bhrnqd', p.astype(vw.dtype), vw).astype(jnp.float32) / l

    o = o.reshape(b, h, dilation, Lp, hd)[:, :, :, :L].transpose(0, 3, 2, 1, 4).reshape(b, s, h, hd)

    def stat_back(t):
        t = t.reshape(b, h, dilation, Lp)[:, :, :, :L]
        return t.transpose(0, 3, 2, 1).reshape(b, s, h)

    return o, stat_back(m), stat_back(l)


def hybrid_mixer(h, w_in, sg_ln_g, sg_ln_b, sg_w, sg_b, w_att_out, w_sg_out, w_out):
    b, s, _ = h.shape
    proj = h @ w_in
    splits = np.cumsum([ATT_WIDTH, ATT_WIDTH, ATT_WIDTH, SG_WIDTH, SG_WIDTH, D_MODEL]).tolist()
    q, k, v, u, vs, g_att, g_sg = jnp.split(proj, splits, axis=-1)

    pos = jnp.arange(s)
    q = partial_rope(q.reshape(b, s, N_ATT_GROUPS, HEADS_PER_GROUP, HEAD_DIM), pos)
    k = partial_rope(k.reshape(b, s, N_ATT_GROUPS, HEADS_PER_GROUP, HEAD_DIM), pos)
    v = v.reshape(b, s, N_ATT_GROUPS, HEADS_PER_GROUP, HEAD_DIM)
    outs, maxes, dens = [], [], []
    for gi, (window, dilation) in enumerate(DILATED_GROUPS):
        o_g, m_g, l_g = dilated_window_attention(q[:, :, gi], k[:, :, gi], v[:, :, gi], window, dilation)
        outs.append(o_g)
        maxes.append(m_g)
        dens.append(l_g)
    o_all = jnp.stack(outs)
    m_all = jnp.stack(maxes)
    l_all = jnp.stack(dens)
    w_den = l_all * jnp.exp(m_all - jnp.max(m_all, axis=0, keepdims=True))
    o_att = jnp.sum(w_den[..., None] * o_all, axis=0) / jnp.sum(w_den, axis=0)[..., None]
    y_att = o_att.astype(h.dtype).reshape(b, s, ATT_OUT_WIDTH) @ w_att_out

    u = jax.nn.gelu(u, approximate=False)
    vs = layernorm(jax.nn.gelu(vs, approximate=False), sg_ln_g, sg_ln_b)
    vc = vs.reshape(b, s // SG_CHUNK, SG_CHUNK, SG_GROUPS, SG_GROUP_DIM)
    causal = jnp.tril(jnp.ones((SG_CHUNK, SG_CHUNK), dtype=sg_w.dtype))
    w_sp = sg_w * causal[None]
    spatial = jnp.einsum('gts,bcsgd->bctgd', w_sp, vc) + sg_b.T[None, None, :, :, None]
    y_sg = (u * spatial.reshape(b, s, SG_WIDTH)) @ w_sg_out

    merged = jax.nn.sigmoid(g_att) * y_att + jax.nn.sigmoid(g_sg) * y_sg
    return merged @ w_out


def _fwd_setup_inputs(seed: int = 0) -> dict:
    key = jax.random.key(seed)
    ks = jax.random.split(key, 24)
    f32 = jnp.float32

    def nrm(k, shape, fan_in):
        return jax.random.normal(k, shape, f32) * (fan_in ** -0.5)

    def gain(k, shape):
        return 1.0 + 0.05 * jax.random.normal(k, shape, f32)

    return {
        "x": jax.random.normal(ks[0], (BATCH, SEQ, D_MODEL), f32),
        "ffn1_norm": gain(ks[1], (DEPTH, D_MODEL)),
        "ffn1_w_gate": nrm(ks[2], (DEPTH, D_MODEL, D_FF), D_MODEL),
        "ffn1_w_up": nrm(ks[3], (DEPTH, D_MODEL, D_FF), D_MODEL),
        "ffn1_w_down": nrm(ks[4], (DEPTH, D_FF, D_MODEL), D_FF),
        "mix_norm": gain(ks[5], (DEPTH, D_MODEL)),
        "w_in": nrm(ks[6], (DEPTH, D_MODEL, IN_WIDTH), D_MODEL),
        "sg_ln_g": gain(ks[7], (DEPTH, SG_WIDTH)),
        "sg_ln_b": 0.02 * jax.random.normal(ks[8], (DEPTH, SG_WIDTH), f32),
        "sg_w": nrm(ks[9], (DEPTH, SG_GROUPS, SG_CHUNK, SG_CHUNK), SG_CHUNK),
        "sg_b": gain(ks[10], (DEPTH, SG_GROUPS, SG_CHUNK)),
        "w_att_out": nrm(ks[11], (DEPTH, ATT_OUT_WIDTH, D_MODEL), ATT_OUT_WIDTH),
        "w_sg_out": nrm(ks[12], (DEPTH, SG_WIDTH, D_MODEL), SG_WIDTH),
        "w_out": nrm(ks[13], (DEPTH, D_MODEL, D_MODEL), D_MODEL),
        "ffn2_norm": gain(ks[14], (DEPTH, D_MODEL)),
        "ffn2_w_gate": nrm(ks[15], (DEPTH, D_MODEL, D_FF), D_MODEL),
        "ffn2_w_up": nrm(ks[16], (DEPTH, D_MODEL, D_FF), D_MODEL),
        "ffn2_w_down": nrm(ks[17], (DEPTH, D_FF, D_MODEL), D_FF),
        "final_norm": gain(ks[18], (D_MODEL,)),
    }


def _fwd_reference(x, ffn1_norm, ffn1_w_gate, ffn1_w_up, ffn1_w_down, mix_norm, w_in, sg_ln_g, sg_ln_b,
              sg_w, sg_b, w_att_out, w_sg_out, w_out, ffn2_norm, ffn2_w_gate, ffn2_w_up, ffn2_w_down,
              final_norm):
    for i in range(DEPTH):
        x = x + 0.5 * swiglu(rmsnorm(x, ffn1_norm[i]), ffn1_w_gate[i], ffn1_w_up[i], ffn1_w_down[i])
        x = x + hybrid_mixer(rmsnorm(x, mix_norm[i]), w_in[i], sg_ln_g[i], sg_ln_b[i], sg_w[i], sg_b[i],
                             w_att_out[i], w_sg_out[i], w_out[i])
        x = x + 0.5 * swiglu(rmsnorm(x, ffn2_norm[i]), ffn2_w_gate[i], ffn2_w_up[i], ffn2_w_down[i])
    return rmsnorm(x, final_norm)


import jax as _jax
import jax.numpy as _jnp

TWIN_FORMAT = 'train_step'
FWD_PARAMS = ['x', 'ffn1_norm', 'ffn1_w_gate', 'ffn1_w_up', 'ffn1_w_down', 'mix_norm', 'w_in', 'sg_ln_g', 'sg_ln_b', 'sg_w', 'sg_b', 'w_att_out', 'w_sg_out', 'w_out', 'ffn2_norm', 'ffn2_w_gate', 'ffn2_w_up', 'ffn2_w_down', 'final_norm']
TWIN_WEIGHTS = ['ffn1_norm', 'ffn1_w_gate', 'ffn1_w_up', 'ffn1_w_down', 'mix_norm', 'w_in', 'sg_ln_g', 'sg_ln_b', 'sg_w', 'sg_b', 'w_att_out', 'w_sg_out', 'w_out', 'ffn2_norm', 'ffn2_w_gate', 'ffn2_w_up', 'ffn2_w_down', 'final_norm']
TWIN_DIFF_INPUT = 'x'
TWIN_INPUTS = ['x', 'ffn1_norm', 'ffn1_w_gate', 'ffn1_w_up', 'ffn1_w_down', 'mix_norm', 'w_in', 'sg_ln_g', 'sg_ln_b', 'sg_w', 'sg_b', 'w_att_out', 'w_sg_out', 'w_out', 'ffn2_norm', 'ffn2_w_gate', 'ffn2_w_up', 'ffn2_w_down', 'final_norm', 'loss_target', 'm_ffn1_norm', 'm_ffn1_w_gate', 'm_ffn1_w_up', 'm_ffn1_w_down', 'm_mix_norm', 'm_w_in', 'm_sg_ln_g', 'm_sg_ln_b', 'm_sg_w', 'm_sg_b', 'm_w_att_out', 'm_w_sg_out', 'm_w_out', 'm_ffn2_norm', 'm_ffn2_w_gate', 'm_ffn2_w_up', 'm_ffn2_w_down', 'm_final_norm', 'v_ffn1_norm', 'v_ffn1_w_gate', 'v_ffn1_w_up', 'v_ffn1_w_down', 'v_mix_norm', 'v_w_in', 'v_sg_ln_g', 'v_sg_ln_b', 'v_sg_w', 'v_sg_b', 'v_w_att_out', 'v_w_sg_out', 'v_w_out', 'v_ffn2_norm', 'v_ffn2_w_gate', 'v_ffn2_w_up', 'v_ffn2_w_down', 'v_final_norm']
TWIN_OUTPUTS = ['loss', 'grad_x', 'grad_ffn1_norm', 'grad_ffn1_w_gate', 'grad_ffn1_w_up', 'grad_ffn1_w_down', 'grad_mix_norm', 'grad_w_in', 'grad_sg_ln_g', 'grad_sg_ln_b', 'grad_sg_w', 'grad_sg_b', 'grad_w_att_out', 'grad_w_sg_out', 'grad_w_out', 'grad_ffn2_norm', 'grad_ffn2_w_gate', 'grad_ffn2_w_up', 'grad_ffn2_w_down', 'grad_final_norm', 'delta_ffn1_norm', 'delta_ffn1_w_gate', 'delta_ffn1_w_up', 'delta_ffn1_w_down', 'delta_mix_norm', 'delta_w_in', 'delta_sg_ln_g', 'delta_sg_ln_b', 'delta_sg_w', 'delta_sg_b', 'delta_w_att_out', 'delta_w_sg_out', 'delta_w_out', 'delta_ffn2_norm', 'delta_ffn2_w_gate', 'delta_ffn2_w_up', 'delta_ffn2_w_down', 'delta_final_norm', 'new_m_ffn1_norm', 'new_m_ffn1_w_gate', 'new_m_ffn1_w_up', 'new_m_ffn1_w_down', 'new_m_mix_norm', 'new_m_w_in', 'new_m_sg_ln_g', 'new_m_sg_ln_b', 'new_m_sg_w', 'new_m_sg_b', 'new_m_w_att_out', 'new_m_w_sg_out', 'new_m_w_out', 'new_m_ffn2_norm', 'new_m_ffn2_w_gate', 'new_m_ffn2_w_up', 'new_m_ffn2_w_down', 'new_m_final_norm', 'new_v_ffn1_norm', 'new_v_ffn1_w_gate', 'new_v_ffn1_w_up', 'new_v_ffn1_w_down', 'new_v_mix_norm', 'new_v_w_in', 'new_v_sg_ln_g', 'new_v_sg_ln_b', 'new_v_sg_w', 'new_v_sg_b', 'new_v_w_att_out', 'new_v_w_sg_out', 'new_v_w_out', 'new_v_ffn2_norm', 'new_v_ffn2_w_gate', 'new_v_ffn2_w_up', 'new_v_ffn2_w_down', 'new_v_final_norm']
TWIN_LEAF_KINDS = {'loss': 'loss', 'grad_x': 'grad_x', 'grad_ffn1_norm': 'grad_w', 'grad_ffn1_w_gate': 'grad_w', 'grad_ffn1_w_up': 'grad_w', 'grad_ffn1_w_down': 'grad_w', 'grad_mix_norm': 'grad_w', 'grad_w_in': 'grad_w', 'grad_sg_ln_g': 'grad_w', 'grad_sg_ln_b': 'grad_w', 'grad_sg_w': 'grad_w', 'grad_sg_b': 'grad_w', 'grad_w_att_out': 'grad_w', 'grad_w_sg_out': 'grad_w', 'grad_w_out': 'grad_w', 'grad_ffn2_norm': 'grad_w', 'grad_ffn2_w_gate': 'grad_w', 'grad_ffn2_w_up': 'grad_w', 'grad_ffn2_w_down': 'grad_w', 'grad_final_norm': 'grad_w', 'delta_ffn1_norm': 'delta_w', 'delta_ffn1_w_gate': 'delta_w', 'delta_ffn1_w_up': 'delta_w', 'delta_ffn1_w_down': 'delta_w', 'delta_mix_norm': 'delta_w', 'delta_w_in': 'delta_w', 'delta_sg_ln_g': 'delta_w', 'delta_sg_ln_b': 'delta_w', 'delta_sg_w': 'delta_w', 'delta_sg_b': 'delta_w', 'delta_w_att_out': 'delta_w', 'delta_w_sg_out': 'delta_w', 'delta_w_out': 'delta_w', 'delta_ffn2_norm': 'delta_w', 'delta_ffn2_w_gate': 'delta_w', 'delta_ffn2_w_up': 'delta_w', 'delta_ffn2_w_down': 'delta_w', 'delta_final_norm': 'delta_w', 'new_m_ffn1_norm': 'new_m', 'new_m_ffn1_w_gate': 'new_m', 'new_m_ffn1_w_up': 'new_m', 'new_m_ffn1_w_down': 'new_m', 'new_m_mix_norm': 'new_m', 'new_m_w_in': 'new_m', 'new_m_sg_ln_g': 'new_m', 'new_m_sg_ln_b': 'new_m', 'new_m_sg_w': 'new_m', 'new_m_sg_b': 'new_m', 'new_m_w_att_out': 'new_m', 'new_m_w_sg_out': 'new_m', 'new_m_w_out': 'new_m', 'new_m_ffn2_norm': 'new_m', 'new_m_ffn2_w_gate': 'new_m', 'new_m_ffn2_w_up': 'new_m', 'new_m_ffn2_w_down': 'new_m', 'new_m_final_norm': 'new_m', 'new_v_ffn1_norm': 'new_v', 'new_v_ffn1_w_gate': 'new_v', 'new_v_ffn1_w_up': 'new_v', 'new_v_ffn1_w_down': 'new_v', 'new_v_mix_norm': 'new_v', 'new_v_w_in': 'new_v', 'new_v_sg_ln_g': 'new_v', 'new_v_sg_ln_b': 'new_v', 'new_v_sg_w': 'new_v', 'new_v_sg_b': 'new_v', 'new_v_w_att_out': 'new_v', 'new_v_w_sg_out': 'new_v', 'new_v_w_out': 'new_v', 'new_v_ffn2_norm': 'new_v', 'new_v_ffn2_w_gate': 'new_v', 'new_v_ffn2_w_up': 'new_v', 'new_v_ffn2_w_down': 'new_v', 'new_v_final_norm': 'new_v'}


def _forward(args):
    return _fwd_reference(*[args[k] for k in FWD_PARAMS])


def _output_shape():
    def fwd():
        inp = _fwd_setup_inputs(0)
        return _fwd_reference(*[inp[k] for k in FWD_PARAMS])
    out = _jax.eval_shape(fwd)
    return out.shape, out.dtype

N_MICROBATCH = 1
ADAM_LR = 0.001
ADAM_B1 = 0.9
ADAM_B2 = 0.999
ADAM_EPS = 1e-08
ADAM_WD = 0.01
ADAM_STEP = 10
PER_EXAMPLE_BATCH_AXIS = {'x': 0, 'loss_target': 0}
SHARED_INPUTS = []
_WEIGHT_DTYPES = {'ffn1_norm': _jnp.float32, 'ffn1_w_gate': _jnp.float32, 'ffn1_w_up': _jnp.float32, 'ffn1_w_down': _jnp.float32, 'mix_norm': _jnp.float32, 'w_in': _jnp.float32, 'sg_ln_g': _jnp.float32, 'sg_ln_b': _jnp.float32, 'sg_w': _jnp.float32, 'sg_b': _jnp.float32, 'w_att_out': _jnp.float32, 'w_sg_out': _jnp.float32, 'w_out': _jnp.float32, 'ffn2_norm': _jnp.float32, 'ffn2_w_gate': _jnp.float32, 'ffn2_w_up': _jnp.float32, 'ffn2_w_down': _jnp.float32, 'final_norm': _jnp.float32}
MOMENT_SCALE = {'ffn1_norm': 4.083898e-02, 'ffn1_w_gate': 1.729848e-02, 'ffn1_w_up': 1.675222e-02, 'ffn1_w_down': 2.778053e-02, 'mix_norm': 4.771933e-02, 'w_in': 1.969919e-02, 'sg_ln_g': 2.332696e-02, 'sg_ln_b': 2.334773e-02, 'sg_w': 2.340320e-02, 'sg_b': 3.387150e-02, 'w_att_out': 8.263968e-03, 'w_sg_out': 3.958017e-02, 'w_out': 3.963028e-02, 'ffn2_norm': 3.351464e-02, 'ffn2_w_gate': 1.419163e-02, 'ffn2_w_up': 1.383012e-02, 'ffn2_w_down': 2.293234e-02, 'final_norm': 1.603615e+01}


def _to_microbatches(a, axis):
    t = _jnp.moveaxis(a, axis, 0)
    t = t.reshape((N_MICROBATCH, t.shape[0] // N_MICROBATCH) + t.shape[1:])
    return _jnp.moveaxis(t, 1, axis + 1)


def setup_inputs(seed: int = 0) -> dict:
    inp = _fwd_setup_inputs(seed)
    key = _jax.random.fold_in(_jax.random.key(seed), 7919)
    shape, _ = _output_shape()
    out = dict(inp)
    out["loss_target"] = _jax.random.normal(_jax.random.fold_in(key, 0), shape, _jnp.float32)
    for i, name in enumerate(TWIN_WEIGHTS):
        w = inp[name].astype(_jnp.float32)
        if MOMENT_SCALE is None:
            s = _jnp.sqrt(_jnp.mean(_jnp.square(w)) + 1e-30)
        else:
            s = MOMENT_SCALE[name]
        km, kv = _jax.random.split(_jax.random.fold_in(key, i + 1))
        out[name] = w
        out["m_" + name] = s * _jax.random.normal(km, w.shape, _jnp.float32)
        out["v_" + name] = (s * s) * _jax.random.uniform(kv, w.shape, _jnp.float32, 0.5, 1.5)
    if N_MICROBATCH > 1:
        for name, axis in PER_EXAMPLE_BATCH_AXIS.items():
            out[name] = _to_microbatches(out[name], axis)
    return {'x': out['x'], 'ffn1_norm': out['ffn1_norm'], 'ffn1_w_gate': out['ffn1_w_gate'], 'ffn1_w_up': out['ffn1_w_up'], 'ffn1_w_down': out['ffn1_w_down'], 'mix_norm': out['mix_norm'], 'w_in': out['w_in'], 'sg_ln_g': out['sg_ln_g'], 'sg_ln_b': out['sg_ln_b'], 'sg_w': out['sg_w'], 'sg_b': out['sg_b'], 'w_att_out': out['w_att_out'], 'w_sg_out': out['w_sg_out'], 'w_out': out['w_out'], 'ffn2_norm': out['ffn2_norm'], 'ffn2_w_gate': out['ffn2_w_gate'], 'ffn2_w_up': out['ffn2_w_up'], 'ffn2_w_down': out['ffn2_w_down'], 'final_norm': out['final_norm'], 'loss_target': out['loss_target'], 'm_ffn1_norm': out['m_ffn1_norm'], 'm_ffn1_w_gate': out['m_ffn1_w_gate'], 'm_ffn1_w_up': out['m_ffn1_w_up'], 'm_ffn1_w_down': out['m_ffn1_w_down'], 'm_mix_norm': out['m_mix_norm'], 'm_w_in': out['m_w_in'], 'm_sg_ln_g': out['m_sg_ln_g'], 'm_sg_ln_b': out['m_sg_ln_b'], 'm_sg_w': out['m_sg_w'], 'm_sg_b': out['m_sg_b'], 'm_w_att_out': out['m_w_att_out'], 'm_w_sg_out': out['m_w_sg_out'], 'm_w_out': out['m_w_out'], 'm_ffn2_norm': out['m_ffn2_norm'], 'm_ffn2_w_gate': out['m_ffn2_w_gate'], 'm_ffn2_w_up': out['m_ffn2_w_up'], 'm_ffn2_w_down': out['m_ffn2_w_down'], 'm_final_norm': out['m_final_norm'], 'v_ffn1_norm': out['v_ffn1_norm'], 'v_ffn1_w_gate': out['v_ffn1_w_gate'], 'v_ffn1_w_up': out['v_ffn1_w_up'], 'v_ffn1_w_down': out['v_ffn1_w_down'], 'v_mix_norm': out['v_mix_norm'], 'v_w_in': out['v_w_in'], 'v_sg_ln_g': out['v_sg_ln_g'], 'v_sg_ln_b': out['v_sg_ln_b'], 'v_sg_w': out['v_sg_w'], 'v_sg_b': out['v_sg_b'], 'v_w_att_out': out['v_w_att_out'], 'v_w_sg_out': out['v_w_sg_out'], 'v_w_out': out['v_w_out'], 'v_ffn2_norm': out['v_ffn2_norm'], 'v_ffn2_w_gate': out['v_ffn2_w_gate'], 'v_ffn2_w_up': out['v_ffn2_w_up'], 'v_ffn2_w_down': out['v_ffn2_w_down'], 'v_final_norm': out['v_final_norm']}


def _loss(weights, diff, rest, loss_target):
    with _jax.named_scope("forward"):
        args = {**rest, TWIN_DIFF_INPUT: diff, **{k: w.astype(_WEIGHT_DTYPES[k]) for k, w in weights.items()}}
        y = _forward(args)
    with _jax.named_scope("loss_head"):
        err = _jnp.square(y.astype(_jnp.float32) - loss_target)
        return 0.5 * _jnp.sum(_jnp.mean(err, axis=-1)) if err.ndim else 0.5 * err


def _adamw(w, g, m, v):
    m = ADAM_B1 * m + (1.0 - ADAM_B1) * g
    v = ADAM_B2 * v + (1.0 - ADAM_B2) * _jnp.square(g)
    m_hat = m / (1.0 - ADAM_B1 ** ADAM_STEP)
    v_hat = v / (1.0 - ADAM_B2 ** ADAM_STEP)
    delta = -ADAM_LR * (m_hat / (_jnp.sqrt(v_hat) + ADAM_EPS) + ADAM_WD * w)
    return delta, m, v


def reference(x, ffn1_norm, ffn1_w_gate, ffn1_w_up, ffn1_w_down, mix_norm, w_in, sg_ln_g, sg_ln_b, sg_w, sg_b, w_att_out, w_sg_out, w_out, ffn2_norm, ffn2_w_gate, ffn2_w_up, ffn2_w_down, final_norm, loss_target, m_ffn1_norm, m_ffn1_w_gate, m_ffn1_w_up, m_ffn1_w_down, m_mix_norm, m_w_in, m_sg_ln_g, m_sg_ln_b, m_sg_w, m_sg_b, m_w_att_out, m_w_sg_out, m_w_out, m_ffn2_norm, m_ffn2_w_gate, m_ffn2_w_up, m_ffn2_w_down, m_final_norm, v_ffn1_norm, v_ffn1_w_gate, v_ffn1_w_up, v_ffn1_w_down, v_mix_norm, v_w_in, v_sg_ln_g, v_sg_ln_b, v_sg_w, v_sg_b, v_w_att_out, v_w_sg_out, v_w_out, v_ffn2_norm, v_ffn2_w_gate, v_ffn2_w_up, v_ffn2_w_down, v_final_norm):
    given = dict(x=x, ffn1_norm=ffn1_norm, ffn1_w_gate=ffn1_w_gate, ffn1_w_up=ffn1_w_up, ffn1_w_down=ffn1_w_down, mix_norm=mix_norm, w_in=w_in, sg_ln_g=sg_ln_g, sg_ln_b=sg_ln_b, sg_w=sg_w, sg_b=sg_b, w_att_out=w_att_out, w_sg_out=w_sg_out, w_out=w_out, ffn2_norm=ffn2_norm, ffn2_w_gate=ffn2_w_gate, ffn2_w_up=ffn2_w_up, ffn2_w_down=ffn2_w_down, final_norm=final_norm, loss_target=loss_target, m_ffn1_norm=m_ffn1_norm, m_ffn1_w_gate=m_ffn1_w_gate, m_ffn1_w_up=m_ffn1_w_up, m_ffn1_w_down=m_ffn1_w_down, m_mix_norm=m_mix_norm, m_w_in=m_w_in, m_sg_ln_g=m_sg_ln_g, m_sg_ln_b=m_sg_ln_b, m_sg_w=m_sg_w, m_sg_b=m_sg_b, m_w_att_out=m_w_att_out, m_w_sg_out=m_w_sg_out, m_w_out=m_w_out, m_ffn2_norm=m_ffn2_norm, m_ffn2_w_gate=m_ffn2_w_gate, m_ffn2_w_up=m_ffn2_w_up, m_ffn2_w_down=m_ffn2_w_down, m_final_norm=m_final_norm, v_ffn1_norm=v_ffn1_norm, v_ffn1_w_gate=v_ffn1_w_gate, v_ffn1_w_up=v_ffn1_w_up, v_ffn1_w_down=v_ffn1_w_down, v_mix_norm=v_mix_norm, v_w_in=v_w_in, v_sg_ln_g=v_sg_ln_g, v_sg_ln_b=v_sg_ln_b, v_sg_w=v_sg_w, v_sg_b=v_sg_b, v_w_att_out=v_w_att_out, v_w_sg_out=v_w_sg_out, v_w_out=v_w_out, v_ffn2_norm=v_ffn2_norm, v_ffn2_w_gate=v_ffn2_w_gate, v_ffn2_w_up=v_ffn2_w_up, v_ffn2_w_down=v_ffn2_w_down, v_final_norm=v_final_norm)
    weights = {n: given[n] for n in TWIN_WEIGHTS}
    shared = {n: given[n] for n in SHARED_INPUTS}
    per_example = {n: given[n] for n in ['x']}
    grad_fn = _jax.value_and_grad(_loss, argnums=(0, 1))

    def one_microbatch(ex, loss_target):
        ex = dict(ex)
        diff = ex.pop(TWIN_DIFF_INPUT)
        return grad_fn(weights, diff, {**shared, **ex}, loss_target)

    if N_MICROBATCH == 1:
        loss, (grad_w, grad_x) = one_microbatch(per_example, given["loss_target"])
    else:
        def body(carry, xs):
            loss_sum, grad_sum = carry
            l_k, (gw_k, gx_k) = one_microbatch(xs[0], xs[1])
            with _jax.named_scope("update"):
                return (loss_sum + l_k, _jax.tree.map(_jnp.add, grad_sum, gw_k)), gx_k

        init = (_jnp.zeros((), _jnp.float32), _jax.tree.map(_jnp.zeros_like, weights))
        (loss, grad_w), grad_x = _jax.lax.scan(body, init, (per_example, given["loss_target"]))
    with _jax.named_scope("update"):
        delta_w, new_m, new_v = {}, {}, {}
        for n in TWIN_WEIGHTS:
            delta_w[n], new_m[n], new_v[n] = _adamw(weights[n], grad_w[n], given["m_" + n], given["v_" + n])
    return (loss, grad_x, *[grad_w[n] for n in TWIN_WEIGHTS], *[delta_w[n] for n in TWIN_WEIGHTS],
            *[new_m[n] for n in TWIN_WEIGHTS], *[new_v[n] for n in TWIN_WEIGHTS])
```

```python
import functools
import math

import jax
import jax.numpy as jnp
from jax import lax
from jax.experimental import pallas as pl
from jax.experimental.pallas import tpu as pltpu

BF = jnp.bfloat16
F32 = jnp.float32
MESH = pl.DeviceIdType.MESH
N_DEV = 8

HEAD_DIM = 128
HEADS_PER_GROUP = 4
GROUP_W = HEADS_PER_GROUP * HEAD_DIM
DILATIONS = (1, 4, 16)
ATT_W = len(DILATIONS) * GROUP_W
SG_W = 1536
SG_GROUPS = 12
BLK = 128
ROPE_DIM = 32
ROPE_THETA = 500000.0
NORM_EPS = 1e-6
LN_EPS = 1e-5
Q_OFF, K_OFF, V_OFF, U_OFF, VS_OFF, GA_OFF = 0, ATT_W, 2 * ATT_W, 3 * ATT_W, 3 * ATT_W + SG_W, 3 * ATT_W + 2 * SG_W

ADAM_LR, ADAM_B1, ADAM_B2, ADAM_EPS, ADAM_WD, ADAM_STEP = 0.001, 0.9, 0.999, 1e-08, 0.01, 10

VMEM_LIMIT = 56 * 1024 * 1024
NEG = -1e30


def _cp(*sem):
    return pltpu.CompilerParams(dimension_semantics=sem, vmem_limit_bytes=VMEM_LIMIT)


def _tile(n, pref):
    t = min(n, pref)
    while n % t:
        t //= 2
    return t


def _nt(a, b):
    return lax.dot_general(a, b, (((1,), (1,)), ((), ())), preferred_element_type=F32)


def _tn(a, b):
    return lax.dot_general(a, b, (((0,), (0,)), ((), ())), preferred_element_type=F32)


def _nn(a, b):
    return jnp.dot(a, b, preferred_element_type=F32)


def _gelu(x):
    return 0.5 * x * (1.0 + lax.erf(x * (2.0 ** -0.5)))


def _gelu_grad(x):
    return 0.5 * (1.0 + lax.erf(x * (2.0 ** -0.5))) + x * jnp.exp(-0.5 * x * x) * (1.0 / math.sqrt(2.0 * math.pi))


def _rms_fwd(x, g, name):
    S, D = x.shape
    tm = _tile(S, 512)

    def body(x_ref, g_ref, o_ref):
        xv = x_ref[...]
        r = lax.rsqrt(jnp.mean(xv * xv, axis=-1, keepdims=True) + NORM_EPS)
        o_ref[...] = (xv * r * g_ref[...]).astype(BF)

    return pl.pallas_call(
        body, grid=(S // tm,), name=name,
        in_specs=[pl.BlockSpec((tm, D), lambda i: (i, 0)), pl.BlockSpec((1, D), lambda i: (0, 0))],
        out_specs=pl.BlockSpec((tm, D), lambda i: (i, 0)),
        out_shape=jax.ShapeDtypeStruct((S, D), BF), compiler_params=_cp("arbitrary"),
    )(x, g)


def _ffn_up(h, wg, wu, name):
    S, D = h.shape
    nb, _, Fb = wg.shape
    tm = _tile(S, 512)

    def body(h_ref, wg_ref, wu_ref, g_ref, u_ref, a_ref):
        hv = h_ref[...]
        g = _nn(hv, wg_ref[0])
        u = _nn(hv, wu_ref[0])
        g_ref[0] = g.astype(BF)
        u_ref[0] = u.astype(BF)
        a_ref[0] = (g * jax.nn.sigmoid(g) * u).astype(BF)

    act = pl.BlockSpec((1, tm, Fb), lambda j, i: (j, i, 0))
    w = pl.BlockSpec((1, D, Fb), lambda j, i: (j, 0, 0))
    shp = jax.ShapeDtypeStruct((nb, S, Fb), BF)
    return pl.pallas_call(
        body, grid=(nb, S // tm), name=name,
        in_specs=[pl.BlockSpec((tm, D), lambda j, i: (i, 0)), w, w],
        out_specs=[act, act, act], out_shape=[shp, shp, shp], compiler_params=_cp("arbitrary", "arbitrary"),
    )(h, wg, wu)


def _ffn_down_norm(a, wd, x, gn, name):
    nb, S, Fb = a.shape
    D = wd.shape[2]
    tm = _tile(S, 256)

    def body(a_ref, wd_ref, x_ref, gn_ref, xo_ref, hn_ref, acc_ref):
        j = pl.program_id(1)

        @pl.when(j == 0)
        def _():
            acc_ref[...] = jnp.zeros_like(acc_ref)

        acc_ref[...] += _nn(a_ref[0], wd_ref[0])

        @pl.when(j == nb - 1)
        def _():
            xo = x_ref[...] + 0.5 * acc_ref[...]
            r = lax.rsqrt(jnp.mean(xo * xo, axis=-1, keepdims=True) + NORM_EPS)
            xo_ref[...] = xo
            hn_ref[...] = (xo * r * gn_ref[...]).astype(BF)

    row = pl.BlockSpec((tm, D), lambda i, j: (i, 0))
    return pl.pallas_call(
        body, grid=(S // tm, nb), name=name,
        in_specs=[pl.BlockSpec((1, tm, Fb), lambda i, j: (j, i, 0)), pl.BlockSpec((1, Fb, D), lambda i, j: (j, 0, 0)), row,
                  pl.BlockSpec((1, D), lambda i, j: (0, 0))],
        out_specs=[row, row], out_shape=[jax.ShapeDtypeStruct((S, D), F32), jax.ShapeDtypeStruct((S, D), BF)],
        scratch_shapes=[pltpu.VMEM((tm, D), F32)], compiler_params=_cp("arbitrary", "arbitrary"),
    )(a, wd, x, gn)


def _ffn_down_loss(a, wd, x, gf, tgt, name):
    nb, S, Fb = a.shape
    D = wd.shape[2]
    tm = _tile(S, 256)

    def body(a_ref, wd_ref, x_ref, gf_ref, t_ref, dx_ref, dxb_ref, dgf_ref, loss_ref, acc_ref):
        i, j = pl.program_id(0), pl.program_id(1)

        @pl.when(j == 0)
        def _():
            acc_ref[...] = jnp.zeros_like(acc_ref)

        acc_ref[...] += _nn(a_ref[0], wd_ref[0])

        @pl.when((j == nb - 1) & (i == 0))
        def _():
            dgf_ref[...] = jnp.zeros_like(dgf_ref)
            loss_ref[...] = jnp.zeros_like(loss_ref)

        @pl.when(j == nb - 1)
        def _():
            xo = x_ref[...] + 0.5 * acc_ref[...]
            r = lax.rsqrt(jnp.mean(xo * xo, axis=-1, keepdims=True) + NORM_EPS)
            xh = xo * r
            gf = gf_ref[...]
            e = xh * gf - t_ref[...]
            loss_ref[...] += jnp.sum(jnp.mean(e * e, axis=-1, keepdims=True), axis=0, keepdims=True) * 0.5
            dy = e * (1.0 / D)
            dgf_ref[...] += jnp.sum(dy * xh, axis=0, keepdims=True)
            dxh = dy * gf
            dx = r * (dxh - xh * jnp.mean(dxh * xh, axis=-1, keepdims=True))
            dx_ref[...] = dx
            dxb_ref[...] = (0.5 * dx).astype(BF)

    row = pl.BlockSpec((tm, D), lambda i, j: (i, 0))
    vec = pl.BlockSpec((1, D), lambda i, j: (0, 0))
    return pl.pallas_call(
        body, grid=(S // tm, nb), name=name,
        in_specs=[pl.BlockSpec((1, tm, Fb), lambda i, j: (j, i, 0)), pl.BlockSpec((1, Fb, D), lambda i, j: (j, 0, 0)), row, vec, row],
        out_specs=[row, row, vec, pl.BlockSpec((1, 128), lambda i, j: (0, 0))],
        out_shape=[jax.ShapeDtypeStruct((S, D), F32), jax.ShapeDtypeStruct((S, D), BF), jax.ShapeDtypeStruct((1, D), F32),
                   jax.ShapeDtypeStruct((1, 128), F32)],
        scratch_shapes=[pltpu.VMEM((tm, D), F32)], compiler_params=_cp("arbitrary", "arbitrary"),
    )(a, wd, x, gf, tgt)


def _ffn_bwd_act(dyb, wd, g, u, name):
    S, D = dyb.shape
    nb, Fb, _ = wd.shape
    tm = _tile(S, 512)

    def body(dy_ref, wd_ref, g_ref, u_ref, dg_ref, du_ref):
        da = _nt(dy_ref[...], wd_ref[0])
        gv = g_ref[0].astype(F32)
        uv = u_ref[0].astype(F32)
        sg = jax.nn.sigmoid(gv)
        du_ref[0] = (da * gv * sg).astype(BF)
        dg_ref[0] = (da * uv * sg * (1.0 + gv * (1.0 - sg))).astype(BF)

    act = pl.BlockSpec((1, tm, Fb), lambda j, i: (j, i, 0))
    shp = jax.ShapeDtypeStruct((nb, S, Fb), BF)
    return pl.pallas_call(
        body, grid=(nb, S // tm), name=name,
        in_specs=[pl.BlockSpec((tm, D), lambda j, i: (i, 0)), pl.BlockSpec((1, Fb, D), lambda j, i: (j, 0, 0)), act, act],
        out_specs=[act, act], out_shape=[shp, shp], compiler_params=_cp("arbitrary", "arbitrary"),
    )(dyb, wd, g, u)


def _ffn_dwd(a, dyb, name):
    nb, S, Fb = a.shape
    D = dyb.shape[1]
    ts = _tile(S, 512)
    ns = S // ts

    def body(a_ref, dy_ref, o_ref, acc_ref):
        s = pl.program_id(1)

        @pl.when(s == 0)
        def _():
            acc_ref[...] = jnp.zeros_like(acc_ref)

        acc_ref[...] += _tn(a_ref[0], dy_ref[...])

        @pl.when(s == ns - 1)
        def _():
            o_ref[0] = acc_ref[...].astype(BF)

    return pl.pallas_call(
        body, grid=(nb, ns), name=name,
        in_specs=[pl.BlockSpec((1, ts, Fb), lambda j, s: (j, s, 0)), pl.BlockSpec((ts, D), lambda j, s: (s, 0))],
        out_specs=pl.BlockSpec((1, Fb, D), lambda j, s: (j, 0, 0)), out_shape=jax.ShapeDtypeStruct((nb, Fb, D), BF),
        scratch_shapes=[pltpu.VMEM((Fb, D), F32)], compiler_params=_cp("arbitrary", "arbitrary"),
    )(a, dyb)


def _ffn_dwgu(h, dg, du, name):
    S, D = h.shape
    nb, _, Fb = dg.shape
    ts = _tile(S, 512)
    ns = S // ts

    def body(h_ref, dg_ref, du_ref, og_ref, ou_ref, accg_ref, accu_ref):
        s = pl.program_id(1)

        @pl.when(s == 0)
        def _():
            accg_ref[...] = jnp.zeros_like(accg_ref)
            accu_ref[...] = jnp.zeros_like(accu_ref)

        hv = h_ref[...]
        accg_ref[...] += _tn(hv, dg_ref[0])
        accu_ref[...] += _tn(hv, du_ref[0])

        @pl.when(s == ns - 1)
        def _():
            og_ref[0] = accg_ref[...].astype(BF)
            ou_ref[0] = accu_ref[...].astype(BF)

    act = pl.BlockSpec((1, ts, Fb), lambda j, s: (j, s, 0))
    out = pl.BlockSpec((1, D, Fb), lambda j, s: (j, 0, 0))
    shp = jax.ShapeDtypeStruct((nb, D, Fb), BF)
    return pl.pallas_call(
        body, grid=(nb, ns), name=name,
        in_specs=[pl.BlockSpec((ts, D), lambda j, s: (s, 0)), act, act],
        out_specs=[out, out], out_shape=[shp, shp],
        scratch_shapes=[pltpu.VMEM((D, Fb), F32), pltpu.VMEM((D, Fb), F32)], compiler_params=_cp("arbitrary", "arbitrary"),
    )(h, dg, du)


def _dh_rms_bwd(pairs, blocked, tk, x, gn, dxo, out_scale, name):
    S, D = x.shape
    nk = pairs[0][0].shape[0] if blocked else pairs[0][0].shape[1] // tk
    tm = _tile(S, 256)
    npair = len(pairs)

    def body(*refs):
        ins = refs[: 2 * npair]
        x_ref, gn_ref, dxo_ref, dx_ref, dxb_ref, dgn_ref, acc_ref = refs[2 * npair:]
        i, k = pl.program_id(0), pl.program_id(1)

        @pl.when(k == 0)
        def _():
            acc_ref[...] = jnp.zeros_like(acc_ref)

        for p in range(npair):
            l_ref, r_ref = ins[2 * p], ins[2 * p + 1]
            if blocked:
                acc_ref[...] += _nt(l_ref[0], r_ref[0])
            else:
                acc_ref[...] += _nt(l_ref[...], r_ref[...])

        @pl.when((k == nk - 1) & (i == 0))
        def _():
            dgn_ref[...] = jnp.zeros_like(dgn_ref)

        @pl.when(k == nk - 1)
        def _():
            xv = x_ref[...]
            r = lax.rsqrt(jnp.mean(xv * xv, axis=-1, keepdims=True) + NORM_EPS)
            xh = xv * r
            dh = acc_ref[...]
            dgn_ref[...] += jnp.sum(dh * xh, axis=0, keepdims=True)
            dxh = dh * gn_ref[...]
            dx = dxo_ref[...] + r * (dxh - xh * jnp.mean(dxh * xh, axis=-1, keepdims=True))
            dx_ref[...] = dx
            dxb_ref[...] = (out_scale * dx).astype(BF)

    if blocked:
        lspec = pl.BlockSpec((1, tm, tk), lambda i, k: (k, i, 0))
        rspec = pl.BlockSpec((1, D, tk), lambda i, k: (k, 0, 0))
    else:
        lspec = pl.BlockSpec((tm, tk), lambda i, k: (i, k))
        rspec = pl.BlockSpec((D, tk), lambda i, k: (0, k))
    row = pl.BlockSpec((tm, D), lambda i, k: (i, 0))
    vec = pl.BlockSpec((1, D), lambda i, k: (0, 0))
    flat = [t for pr in pairs for t in pr]
    return pl.pallas_call(
        body, grid=(S // tm, nk), name=name,
        in_specs=[lspec, rspec] * npair + [row, vec, row],
        out_specs=[row, row, vec],
        out_shape=[jax.ShapeDtypeStruct((S, D), F32), jax.ShapeDtypeStruct((S, D), BF), jax.ShapeDtypeStruct((1, D), F32)],
        scratch_shapes=[pltpu.VMEM((tm, D), F32)], compiler_params=_cp("arbitrary", "arbitrary"),
    )(*flat, x, gn, dxo)


def _mm_nn(a, b, tm, tn, name):
    M, K = a.shape
    N = b.shape[1]

    def body(a_ref, b_ref, o_ref):
        o_ref[...] = _nn(a_ref[...], b_ref[...]).astype(BF)

    return pl.pallas_call(
        body, grid=(N // tn, M // tm), name=name,
        in_specs=[pl.BlockSpec((tm, K), lambda n, i: (i, 0)), pl.BlockSpec((K, tn), lambda n, i: (0, n))],
        out_specs=pl.BlockSpec((tm, tn), lambda n, i: (i, n)), out_shape=jax.ShapeDtypeStruct((M, N), BF),
        compiler_params=_cp("arbitrary", "arbitrary"),
    )(a, b)


def _mm_tn(a, b, tm, tn, ts, blocked, name):
    S, M = a.shape
    N = b.shape[1]
    ns = S // ts

    def body(a_ref, b_ref, o_ref, acc_ref):
        s = pl.program_id(2)

        @pl.when(s == 0)
        def _():
            acc_ref[...] = jnp.zeros_like(acc_ref)

        acc_ref[...] += _tn(a_ref[...], b_ref[...])

        @pl.when(s == ns - 1)
        def _():
            if blocked:
                o_ref[0] = acc_ref[...].astype(BF)
            else:
                o_ref[...] = acc_ref[...].astype(BF)

    if blocked:
        ospec = pl.BlockSpec((1, tm, tn), lambda i, n, s: (n, i, 0))
        oshape = jax.ShapeDtypeStruct((N // tn, M, tn), BF)
    else:
        ospec = pl.BlockSpec((tm, tn), lambda i, n, s: (i, n))
        oshape = jax.ShapeDtypeStruct((M, N), BF)
    return pl.pallas_call(
        body, grid=(M // tm, N // tn, ns), name=name,
        in_specs=[pl.BlockSpec((ts, tm), lambda i, n, s: (s, i)), pl.BlockSpec((ts, tn), lambda i, n, s: (s, n))],
        out_specs=ospec, out_shape=oshape, scratch_shapes=[pltpu.VMEM((tm, tn), F32)],
        compiler_params=_cp("arbitrary", "arbitrary", "arbitrary"),
    )(a, b)


def _rope_tables(S):
    half = ROPE_DIM // 2
    inv_freq = ROPE_THETA ** (-jnp.arange(0, ROPE_DIM, 2, dtype=F32) / ROPE_DIM)
    ang = jnp.arange(S, dtype=F32)[:, None] * inv_freq[None, :]
    cos, sin = jnp.cos(ang), jnp.sin(ang)
    zeros = jnp.zeros((S, HEAD_DIM - ROPE_DIM), F32)
    c = jnp.concatenate([cos, cos, jnp.ones((S, HEAD_DIM - ROPE_DIM), F32)], axis=1)
    sm = jnp.concatenate([-sin, jnp.zeros((S, half), F32), zeros], axis=1)
    sp = jnp.concatenate([jnp.zeros((S, half), F32), sin, zeros], axis=1)
    return c, sm, sp


def _rope(t, c, sm, sp):
    return t * c + pltpu.roll(t, HEAD_DIM - ROPE_DIM // 2, 1) * sm + pltpu.roll(t, ROPE_DIM // 2, 1) * sp


def _rope_t(dy, c, sm, sp):
    return dy * c + pltpu.roll(dy * sm, ROPE_DIM // 2, 1) + pltpu.roll(dy * sp, HEAD_DIM - ROPE_DIM // 2, 1)


def _att_mask(i):
    qi = lax.broadcasted_iota(jnp.int32, (BLK, 2 * BLK), 0)
    kj = lax.broadcasted_iota(jnp.int32, (BLK, 2 * BLK), 1)
    diff = qi + BLK - kj
    first_key = jnp.where(i > 0, 0, BLK)
    return (diff >= 0) & (diff <= BLK) & (kj >= first_key)


def _att_specs(S, P, gi, d):
    L = S // d
    pcols = P // HEAD_DIM

    def sect(off):
        base = (off + gi * GROUP_W) // HEAD_DIM
        return pl.BlockSpec((L, HEAD_DIM), lambda r, hh: (0, r * pcols + base + hh))

    tab = pl.BlockSpec((L, HEAD_DIM), lambda r, hh: (0, r))
    head = pl.BlockSpec((L, HEAD_DIM), lambda r, hh: (0, r * HEADS_PER_GROUP + hh))
    return L, sect, tab, head


def _att_fwd(proj, tabs, gi, d, name):
    S, P = proj.shape
    L, sect, tab, head = _att_specs(S, P, gi, d)
    nblk = L // BLK
    scale = HEAD_DIM ** -0.5
    pv = proj.reshape(L, d * P)
    tv = [t.reshape(L, d * HEAD_DIM) for t in tabs]

    def body(q_ref, k_ref, v_ref, c_ref, sm_ref, sp_ref, o_ref, lse_ref, qr, kp, vp):
        c, sm, sp = c_ref[...], sm_ref[...], sp_ref[...]
        qr[...] = _rope(q_ref[...].astype(F32), c, sm, sp).astype(BF)
        kp[pl.ds(0, BLK), :] = jnp.zeros((BLK, HEAD_DIM), BF)
        vp[pl.ds(0, BLK), :] = jnp.zeros((BLK, HEAD_DIM), BF)
        kp[pl.ds(BLK, L), :] = _rope(k_ref[...].astype(F32), c, sm, sp).astype(BF)
        vp[pl.ds(BLK, L), :] = v_ref[...]

        def blk(i, carry):
            r0 = pl.multiple_of(i * BLK, BLK)
            s = _nt(qr[pl.ds(r0, BLK), :], kp[pl.ds(r0, 2 * BLK), :]) * scale
            s = jnp.where(_att_mask(i), s, NEG)
            m = jnp.max(s, axis=-1, keepdims=True)
            p = jnp.exp(s - m)
            l = jnp.sum(p, axis=-1, keepdims=True)
            o_ref[pl.ds(r0, BLK), :] = _nn(p.astype(BF), vp[pl.ds(r0, 2 * BLK), :]) / l
            lse_ref[pl.ds(r0, BLK), :] = jnp.broadcast_to(m + jnp.log(l), (BLK, HEAD_DIM))
            return carry

        lax.fori_loop(0, nblk, blk, 0)

    shp = jax.ShapeDtypeStruct((L, d * GROUP_W), F32)
    o, lse = pl.pallas_call(
        body, grid=(d, HEADS_PER_GROUP), name=name,
        in_specs=[sect(Q_OFF), sect(K_OFF), sect(V_OFF), tab, tab, tab],
        out_specs=[head, head], out_shape=[shp, shp],
        scratch_shapes=[pltpu.VMEM((L, HEAD_DIM), BF), pltpu.VMEM((L + BLK, HEAD_DIM), BF), pltpu.VMEM((L + BLK, HEAD_DIM), BF)],
        compiler_params=_cp("arbitrary", "arbitrary"),
    )(pv, pv, pv, *tv)
    return o.reshape(S, GROUP_W), lse.reshape(S, GROUP_W)


def _att_combine(os, lses, name):
    S = os[0].shape[0]
    tm = _tile(S, 512)

    def body(o0, o1, o2, l0, l1, l2, oa_ref, lse_ref):
        a, b, c = l0[...], l1[...], l2[...]
        mx = jnp.maximum(jnp.maximum(a, b), c)
        wa, wb, wc = jnp.exp(a - mx), jnp.exp(b - mx), jnp.exp(c - mx)
        den = wa + wb + wc
        oa_ref[...] = ((wa * o0[...] + wb * o1[...] + wc * o2[...]) / den).astype(BF)
        lse_ref[...] = mx + jnp.log(den)

    row = pl.BlockSpec((tm, GROUP_W), lambda i: (i, 0))
    return pl.pallas_call(
        body, grid=(S // tm,), name=name, in_specs=[row] * 6, out_specs=[row, row],
        out_shape=[jax.ShapeDtypeStruct((S, GROUP_W), BF), jax.ShapeDtypeStruct((S, GROUP_W), F32)],
        compiler_params=_cp("arbitrary"),
    )(*os, *lses)


def _att_bwd(proj, tabs, do, lse, dvec, gi, d, name):
    S, P = proj.shape
    L, sect, tab, head = _att_specs(S, P, gi, d)
    nblk = L // BLK
    scale = HEAD_DIM ** -0.5
    pv = proj.reshape(L, d * P)
    tv = [t.reshape(L, d * HEAD_DIM) for t in tabs]
    hv = [t.reshape(L, d * GROUP_W) for t in (do, lse, dvec)]

    def body(q_ref, k_ref, v_ref, c_ref, sm_ref, sp_ref, do_ref, lse_ref, dv_ref, dq_out, dk_out, dv_out, qr, kp, vp, dkp, dvp):
        c, sm, sp = c_ref[...], sm_ref[...], sp_ref[...]
        qr[...] = _rope(q_ref[...].astype(F32), c, sm, sp).astype(BF)
        kp[pl.ds(0, BLK), :] = jnp.zeros((BLK, HEAD_DIM), BF)
        vp[pl.ds(0, BLK), :] = jnp.zeros((BLK, HEAD_DIM), BF)
        kp[pl.ds(BLK, L), :] = _rope(k_ref[...].astype(F32), c, sm, sp).astype(BF)
        vp[pl.ds(BLK, L), :] = v_ref[...]
        dkp[...] = jnp.zeros_like(dkp)
        dvp[...] = jnp.zeros_like(dvp)

        def blk(i, carry):
            r0 = pl.multiple_of(i * BLK, BLK)
            rows, win = pl.ds(r0, BLK), pl.ds(r0, 2 * BLK)
            q, kw, vw, dob = qr[rows, :], kp[win, :], vp[win, :], do_ref[rows, :]
            s = jnp.where(_att_mask(i), _nt(q, kw) * scale, NEG)
            p = jnp.exp(s - lse_ref[rows, :][:, :1])
            ds = p * (_nt(dob, vw) - dv_ref[rows, :][:, :1]) * scale
            dsb = ds.astype(BF)
            dq_out[rows, :] = _rope_t(_nn(dsb, kw), c_ref[rows, :], sm_ref[rows, :], sp_ref[rows, :]).astype(BF)
            dkp[win, :] += _tn(dsb, q)
            dvp[win, :] += _tn(p.astype(BF), dob)
            return carry

        lax.fori_loop(0, nblk, blk, 0)
        dk_out[...] = _rope_t(dkp[pl.ds(BLK, L), :], c, sm, sp).astype(BF)
        dv_out[...] = dvp[pl.ds(BLK, L), :].astype(BF)

    shp = jax.ShapeDtypeStruct((L, d * GROUP_W), BF)
    outs = pl.pallas_call(
        body, grid=(d, HEADS_PER_GROUP), name=name,
        in_specs=[sect(Q_OFF), sect(K_OFF), sect(V_OFF), tab, tab, tab, head, head, head],
        out_specs=[head, head, head], out_shape=[shp, shp, shp],
        scratch_shapes=[pltpu.VMEM((L, HEAD_DIM), BF), pltpu.VMEM((L + BLK, HEAD_DIM), BF), pltpu.VMEM((L + BLK, HEAD_DIM), BF),
                        pltpu.VMEM((L + BLK, HEAD_DIM), F32), pltpu.VMEM((L + BLK, HEAD_DIM), F32)],
        compiler_params=_cp("arbitrary", "arbitrary"),
    )(pv, pv, pv, *tv, *hv)
    return [t.reshape(S, GROUP_W) for t in outs]


def _sg_parts(u_ref, vs_ref, g_ref, b_ref):
    uv = u_ref[...].astype(F32)
    vv = vs_ref[...].astype(F32)
    vg = _gelu(vv)
    mu = jnp.mean(vg, axis=-1, keepdims=True)
    vc = vg - mu
    rs = lax.rsqrt(jnp.mean(vc * vc, axis=-1, keepdims=True) + LN_EPS)
    y = vc * rs
    return uv, vv, rs, y, y * g_ref[...] + b_ref[...]


def _sg_wmask():
    t = lax.broadcasted_iota(jnp.int32, (BLK, BLK), 0)
    s = lax.broadcasted_iota(jnp.int32, (BLK, BLK), 1)
    return s <= t


def _sg_fwd(proj, sgw, sgbT, lng, lnb, name):
    S, P = proj.shape

    def body(u_ref, vs_ref, w_ref, bt_ref, g_ref, b_ref, z_ref):
        uv, _, _, _, vln = _sg_parts(u_ref, vs_ref, g_ref, b_ref)
        ug = _gelu(uv)
        vb = vln.astype(BF)
        mask = _sg_wmask()
        bt = bt_ref[...]
        for g in range(SG_GROUPS):
            cols = slice(g * BLK, (g + 1) * BLK)
            w = jnp.where(mask, w_ref[g], 0.0).astype(BF)
            sp = _nn(w, vb[:, cols]) + bt[:, g:g + 1]
            z_ref[:, cols] = (ug[:, cols] * sp).astype(BF)

    tile = lambda off: pl.BlockSpec((BLK, SG_W), lambda i: (i, off // SG_W))
    full = lambda shape: pl.BlockSpec(shape, lambda i: (0,) * len(shape))
    return pl.pallas_call(
        body, grid=(S // BLK,), name=name,
        in_specs=[tile(U_OFF), tile(VS_OFF), full((SG_GROUPS, BLK, BLK)), full((BLK, BLK)), full((1, SG_W)), full((1, SG_W))],
        out_specs=pl.BlockSpec((BLK, SG_W), lambda i: (i, 0)), out_shape=jax.ShapeDtypeStruct((S, SG_W), BF),
        compiler_params=_cp("arbitrary"),
    )(proj, proj, sgw, sgbT, lng, lnb)


def _sg_bwd(proj, dz, sgw, sgbT, lng, lnb, name):
    S, P = proj.shape

    def body(u_ref, vs_ref, dz_ref, w_ref, bt_ref, g_ref, b_ref, du_ref, dvs_ref, dw_ref, dbt_ref, dg_ref, db_ref, dvln):
        @pl.when(pl.program_id(0) == 0)
        def _():
            dw_ref[...] = jnp.zeros_like(dw_ref)
            dbt_ref[...] = jnp.zeros_like(dbt_ref)
            dg_ref[...] = jnp.zeros_like(dg_ref)
            db_ref[...] = jnp.zeros_like(db_ref)

        uv, vv, rs, y, vln = _sg_parts(u_ref, vs_ref, g_ref, b_ref)
        ug = _gelu(uv)
        vb = vln.astype(BF)
        dzv = dz_ref[...].astype(F32)
        dsp = dzv * ug
        dspb = dsp.astype(BF)
        mask = _sg_wmask()
        bt = bt_ref[...]
        lane = lax.broadcasted_iota(jnp.int32, (BLK, BLK), 1)
        dbt = jnp.zeros((BLK, BLK), F32)
        for g in range(SG_GROUPS):
            cols = slice(g * BLK, (g + 1) * BLK)
            w = jnp.where(mask, w_ref[g], 0.0).astype(BF)
            sp = _nn(w, vb[:, cols]) + bt[:, g:g + 1]
            du_ref[:, cols] = (dzv[:, cols] * sp * _gelu_grad(uv[:, cols])).astype(BF)
            dw_ref[g] += jnp.where(mask, _nt(dspb[:, cols], vb[:, cols]), 0.0)
            dbt = dbt + jnp.where(lane == g, jnp.sum(dsp[:, cols], axis=-1, keepdims=True), 0.0)
            dvln[:, cols] = _tn(w, dspb[:, cols])
        dbt_ref[...] += dbt
        dvl = dvln[...]
        dg_ref[...] += jnp.sum(dvl * y, axis=0, keepdims=True)
        db_ref[...] += jnp.sum(dvl, axis=0, keepdims=True)
        dy = dvl * g_ref[...]
        dvg = rs * (dy - jnp.mean(dy, axis=-1, keepdims=True) - y * jnp.mean(dy * y, axis=-1, keepdims=True))
        dvs_ref[...] = (dvg * _gelu_grad(vv)).astype(BF)

    tile = lambda off: pl.BlockSpec((BLK, SG_W), lambda i: (i, off // SG_W))
    full = lambda shape: pl.BlockSpec(shape, lambda i: (0,) * len(shape))
    row = pl.BlockSpec((BLK, SG_W), lambda i: (i, 0))
    return pl.pallas_call(
        body, grid=(S // BLK,), name=name,
        in_specs=[tile(U_OFF), tile(VS_OFF), row, full((SG_GROUPS, BLK, BLK)), full((BLK, BLK)), full((1, SG_W)), full((1, SG_W))],
        out_specs=[row, row, full((SG_GROUPS, BLK, BLK)), full((BLK, BLK)), full((1, SG_W)), full((1, SG_W))],
        out_shape=[jax.ShapeDtypeStruct((S, SG_W), BF), jax.ShapeDtypeStruct((S, SG_W), BF),
                   jax.ShapeDtypeStruct((SG_GROUPS, BLK, BLK), F32), jax.ShapeDtypeStruct((BLK, BLK), F32),
                   jax.ShapeDtypeStruct((1, SG_W), F32), jax.ShapeDtypeStruct((1, SG_W), F32)],
        scratch_shapes=[pltpu.VMEM((BLK, SG_W), F32)], compiler_params=_cp("arbitrary"),
    )(proj, proj, dz, sgw, sgbT, lng, lnb)


def _gate_merge(oatt, z, watt, wsg, proj, name):
    S = oatt.shape[0]
    nb, _, Db = watt.shape
    D = nb * Db
    tm = _tile(S, 512)
    ga, gs = GA_OFF // Db, (GA_OFF + D) // Db

    def body(oa_ref, z_ref, wa_ref, ws_ref, ga_ref, gs_ref, ya_ref, ys_ref, mg_ref):
        ya = _nn(oa_ref[...], wa_ref[0])
        ys = _nn(z_ref[...], ws_ref[0])
        ya_ref[...] = ya.astype(BF)
        ys_ref[...] = ys.astype(BF)
        mg_ref[...] = (jax.nn.sigmoid(ga_ref[...].astype(F32)) * ya + jax.nn.sigmoid(gs_ref[...].astype(F32)) * ys).astype(BF)

    out = pl.BlockSpec((tm, Db), lambda j, i: (i, j))
    shp = jax.ShapeDtypeStruct((S, D), BF)
    return pl.pallas_call(
        body, grid=(nb, S // tm), name=name,
        in_specs=[pl.BlockSpec((tm, GROUP_W), lambda j, i: (i, 0)), pl.BlockSpec((tm, SG_W), lambda j, i: (i, 0)),
                  pl.BlockSpec((1, GROUP_W, Db), lambda j, i: (j, 0, 0)), pl.BlockSpec((1, SG_W, Db), lambda j, i: (j, 0, 0)),
                  pl.BlockSpec((tm, Db), lambda j, i: (i, ga + j)), pl.BlockSpec((tm, Db), lambda j, i: (i, gs + j))],
        out_specs=[out, out, out], out_shape=[shp, shp, shp], compiler_params=_cp("arbitrary", "arbitrary"),
    )(oatt, z, watt, wsg, proj, proj)


def _mix_out(merged, wout, x, gn, name):
    S, D = x.shape
    tm = _tile(S, 256)

    def body(m_ref, w_ref, x_ref, gn_ref, xo_ref, hn_ref):
        xo = x_ref[...] + _nn(m_ref[...], w_ref[...])
        r = lax.rsqrt(jnp.mean(xo * xo, axis=-1, keepdims=True) + NORM_EPS)
        xo_ref[...] = xo
        hn_ref[...] = (xo * r * gn_ref[...]).astype(BF)

    row = pl.BlockSpec((tm, D), lambda i: (i, 0))
    return pl.pallas_call(
        body, grid=(S // tm,), name=name,
        in_specs=[row, pl.BlockSpec((D, D), lambda i: (0, 0)), row, pl.BlockSpec((1, D), lambda i: (0, 0))],
        out_specs=[row, row], out_shape=[jax.ShapeDtypeStruct((S, D), F32), jax.ShapeDtypeStruct((S, D), BF)],
        compiler_params=_cp("arbitrary"),
    )(merged, wout, x, gn)


def _mix_bwd_gate(dmix, wout, ya, ys, proj, name):
    S, D = dmix.shape
    tm, tn = _tile(S, 512), 512
    ga, gs = GA_OFF // tn, (GA_OFF + D) // tn

    def body(dm_ref, w_ref, ya_ref, ys_ref, ga_ref, gs_ref, dya_ref, dys_ref, dga_ref, dgs_ref):
        dm = _nt(dm_ref[...], w_ref[...])
        sa = jax.nn.sigmoid(ga_ref[...].astype(F32))
        ss = jax.nn.sigmoid(gs_ref[...].astype(F32))
        dya_ref[...] = (dm * sa).astype(BF)
        dys_ref[...] = (dm * ss).astype(BF)
        dga_ref[...] = (dm * ya_ref[...].astype(F32) * sa * (1.0 - sa)).astype(BF)
        dgs_ref[...] = (dm * ys_ref[...].astype(F32) * ss * (1.0 - ss)).astype(BF)

    out = pl.BlockSpec((tm, tn), lambda i, n: (i, n))
    shp = jax.ShapeDtypeStruct((S, D), BF)
    return pl.pallas_call(
        body, grid=(S // tm, D // tn), name=name,
        in_specs=[pl.BlockSpec((tm, D), lambda i, n: (i, 0)), pl.BlockSpec((tn, D), lambda i, n: (n, 0)), out, out,
                  pl.BlockSpec((tm, tn), lambda i, n: (i, ga + n)), pl.BlockSpec((tm, tn), lambda i, n: (i, gs + n))],
        out_specs=[out] * 4, out_shape=[shp] * 4, compiler_params=_cp("arbitrary", "arbitrary"),
    )(dmix, wout, ya, ys, proj, proj)


def _att_sg_dout(dya, dys, watt, wsg, oatt, name):
    S, D = dya.shape
    nb, _, Db = watt.shape
    tm = _tile(S, 512)

    def body(dya_ref, dys_ref, wa_ref, ws_ref, oa_ref, do_ref, dz_ref, dvec_ref, acca, accs):
        j = pl.program_id(1)

        @pl.when(j == 0)
        def _():
            acca[...] = jnp.zeros_like(acca)
            accs[...] = jnp.zeros_like(accs)

        acca[...] += _nt(dya_ref[...], wa_ref[0])
        accs[...] += _nt(dys_ref[...], ws_ref[0])

        @pl.when(j == nb - 1)
        def _():
            dov = acca[...]
            do_ref[...] = dov.astype(BF)
            dz_ref[...] = accs[...].astype(BF)
            prod = dov * oa_ref[...].astype(F32)
            for hh in range(HEADS_PER_GROUP):
                cols = slice(hh * HEAD_DIM, (hh + 1) * HEAD_DIM)
                dvec_ref[:, cols] = jnp.broadcast_to(jnp.sum(prod[:, cols], axis=-1, keepdims=True), (tm, HEAD_DIM))

    blk = pl.BlockSpec((tm, Db), lambda i, j: (i, j))
    att = pl.BlockSpec((tm, GROUP_W), lambda i, j: (i, 0))
    return pl.pallas_call(
        body, grid=(S // tm, nb), name=name,
        in_specs=[blk, blk, pl.BlockSpec((1, GROUP_W, Db), lambda i, j: (j, 0, 0)), pl.BlockSpec((1, SG_W, Db), lambda i, j: (j, 0, 0)), att],
        out_specs=[att, pl.BlockSpec((tm, SG_W), lambda i, j: (i, 0)), att],
        out_shape=[jax.ShapeDtypeStruct((S, GROUP_W), BF), jax.ShapeDtypeStruct((S, SG_W), BF), jax.ShapeDtypeStruct((S, GROUP_W), F32)],
        scratch_shapes=[pltpu.VMEM((tm, GROUP_W), F32), pltpu.VMEM((tm, SG_W), F32)], compiler_params=_cp("arbitrary", "arbitrary"),
    )(dya, dys, watt, wsg, oatt)


def _place():
    x, y, c = lax.axis_index("x"), lax.axis_index("y"), lax.axis_index("c")
    return x, y, c, 4 * x + 2 * y + c


def _all_gather(shards, name):
    nw = len(shards)

    def body(*refs):
        ins, outs = refs[:nw], refs[nw:2 * nw]
        send, recv, loc = refs[2 * nw:]
        x, y, c, me = _place()
        sib = (x, y, 1 - c)
        chips = [(1 - x, y), (x, 1 - y), (1 - x, 1 - y)]

        def lin(px, py, pc):
            return 4 * px + 2 * py + pc

        def copy(k, s, block, to, src=None):
            dst = outs[k].at[lin(*block)]
            return pltpu.make_async_remote_copy(src_ref=dst if src is None else src, dst_ref=dst, send_sem=send.at[k, s],
                                                recv_sem=recv.at[k, s], device_id=to, device_id_type=MESH)

        local, remote = [], []
        for k in range(nw):
            mine = pltpu.make_async_copy(ins[k], outs[k].at[me], loc.at[k])
            mine.start()
            local.append(mine)
            first = [copy(k, 0, (x, y, c), sib, src=ins[k])]
            first += [copy(k, 1 + j, (x, y, c), (*chip, c), src=ins[k]) for j, chip in enumerate(chips)]
            for cp in first:
                cp.start()
            remote += first
        for k in range(nw):
            for j, chip in enumerate(chips):
                copy(k, 1 + j, (*chip, c), (x, y, c)).wait_recv()
                fwd = copy(k, 4 + j, (*chip, c), sib)
                fwd.start()
                remote.append(fwd)
        for k in range(nw):
            copy(k, 0, (x, y, 1 - c), (x, y, c)).wait_recv()
            for j, chip in enumerate(chips):
                copy(k, 4 + j, (*chip, 1 - c), (x, y, c)).wait_recv()
        for cp in remote:
            cp.wait_send()
        for cp in local:
            cp.wait()

    anyspec = pl.BlockSpec(memory_space=pl.ANY)
    return pl.pallas_call(
        body, name=name, in_specs=[anyspec] * nw, out_specs=[anyspec] * nw,
        out_shape=[jax.ShapeDtypeStruct((N_DEV,) + s.shape, s.dtype) for s in shards],
        scratch_shapes=[pltpu.SemaphoreType.DMA((nw, 7)), pltpu.SemaphoreType.DMA((nw, 7)), pltpu.SemaphoreType.DMA((nw,))],
    )(*shards)


def _reduce_scatter_exchange(parts, name):
    nw = len(parts)

    def body(*refs):
        ins, outs = refs[:nw], refs[nw:2 * nw]
        send, recv, loc = refs[2 * nw:]
        x, y, c, me = _place()
        started = []
        for k in range(nw):
            mine = pltpu.make_async_copy(ins[k].at[me], outs[k].at[me], loc.at[k])
            mine.start()
            started.append((mine, None))
            for r in range(1, N_DEV):
                px = 1 - x if r & 4 else x
                py = 1 - y if r & 2 else y
                pc = 1 - c if r & 1 else c
                peer = 4 * px + 2 * py + pc
                out_cp = pltpu.make_async_remote_copy(src_ref=ins[k].at[peer], dst_ref=outs[k].at[me], send_sem=send.at[k, r - 1],
                                                      recv_sem=recv.at[k, r - 1], device_id=(px, py, pc), device_id_type=MESH)
                out_cp.start()
                in_cp = pltpu.make_async_remote_copy(src_ref=ins[k].at[peer], dst_ref=outs[k].at[peer], send_sem=send.at[k, r - 1],
                                                     recv_sem=recv.at[k, r - 1], device_id=(px, py, pc), device_id_type=MESH)
                started.append((out_cp, in_cp))
        for out_cp, in_cp in started:
            if in_cp is None:
                out_cp.wait()
            else:
                in_cp.wait_recv()
                out_cp.wait_send()

    anyspec = pl.BlockSpec(memory_space=pl.ANY)
    return pl.pallas_call(
        body, name=name, in_specs=[anyspec] * nw, out_specs=[anyspec] * nw,
        out_shape=[jax.ShapeDtypeStruct(p.shape, p.dtype) for p in parts],
        scratch_shapes=[pltpu.SemaphoreType.DMA((nw, 7)), pltpu.SemaphoreType.DMA((nw, 7)), pltpu.SemaphoreType.DMA((nw,))],
    )(*parts)


def _small_allreduce(pack, name):
    R = pack.shape[0]

    def body(p_ref, o_ref, gath, send, recv):
        x, y, c, me = _place()
        gath[me] = p_ref[...]
        copies = []
        for r in range(1, N_DEV):
            px = 1 - x if r & 4 else x
            py = 1 - y if r & 2 else y
            pc = 1 - c if r & 1 else c
            peer = 4 * px + 2 * py + pc
            cp = pltpu.make_async_remote_copy(src_ref=p_ref, dst_ref=gath.at[me], send_sem=send.at[r - 1], recv_sem=recv.at[r - 1],
                                              device_id=(px, py, pc), device_id_type=MESH)
            cp.start()
            copies.append((cp, pltpu.make_async_remote_copy(src_ref=p_ref, dst_ref=gath.at[peer], send_sem=send.at[r - 1],
                                                            recv_sem=recv.at[r - 1], device_id=(px, py, pc), device_id_type=MESH)))
        for out_cp, in_cp in copies:
            in_cp.wait_recv()
            out_cp.wait_send()
        acc = gath[0]
        for s in range(1, N_DEV):
            acc = acc + gath[s]
        o_ref[...] = acc

    vm = pl.BlockSpec(memory_space=pltpu.VMEM)
    return pl.pallas_call(
        body, name=name, in_specs=[vm], out_specs=vm, out_shape=jax.ShapeDtypeStruct(pack.shape, F32),
        scratch_shapes=[pltpu.VMEM((N_DEV, R, 128), F32), pltpu.SemaphoreType.DMA((7,)), pltpu.SemaphoreType.DMA((7,))],
        compiler_params=pltpu.CompilerParams(vmem_limit_bytes=VMEM_LIMIT),
    )(pack)


def _adamw(parts, w, m, v, name):
    ns, R, C = parts.shape
    tr = R
    while tr * C > 262144 and tr % 32 == 0:
        tr //= 2
    c1 = 1.0 - ADAM_B1 ** ADAM_STEP
    c2 = 1.0 - ADAM_B2 ** ADAM_STEP

    def body(p_ref, w_ref, m_ref, v_ref, g_ref, d_ref, nm_ref, nv_ref):
        g = p_ref[0].astype(F32)
        for s in range(1, ns):
            g = g + p_ref[s].astype(F32)
        mn = ADAM_B1 * m_ref[...] + (1.0 - ADAM_B1) * g
        vn = ADAM_B2 * v_ref[...] + (1.0 - ADAM_B2) * (g * g)
        g_ref[...] = g
        nm_ref[...] = mn
        nv_ref[...] = vn
        d_ref[...] = -ADAM_LR * ((mn / c1) / (jnp.sqrt(vn / c2) + ADAM_EPS) + ADAM_WD * w_ref[...])

    row = pl.BlockSpec((tr, C), lambda i: (i, 0))
    shp = jax.ShapeDtypeStruct((R, C), F32)
    return pl.pallas_call(
        body, grid=(R // tr,), name=name,
        in_specs=[pl.BlockSpec((ns, tr, C), lambda i: (0, i, 0)), row, row, row],
        out_specs=[row] * 4, out_shape=[shp] * 4, compiler_params=_cp("arbitrary"),
    )(parts, w, m, v)


def _pad_rows(a, rows):
    return jnp.pad(a, ((0, rows - a.shape[0]), (0, 0)))


def kernel(x, ffn1_norm, ffn1_w_gate, ffn1_w_up, ffn1_w_down, mix_norm, w_in, sg_ln_g, sg_ln_b, sg_w, sg_b, w_att_out, w_sg_out, w_out, ffn2_norm, ffn2_w_gate, ffn2_w_up, ffn2_w_down, final_norm, loss_target, m_ffn1_norm, m_ffn1_w_gate, m_ffn1_w_up, m_ffn1_w_down, m_mix_norm, m_w_in, m_sg_ln_g, m_sg_ln_b, m_sg_w, m_sg_b, m_w_att_out, m_w_sg_out, m_w_out, m_ffn2_norm, m_ffn2_w_gate, m_ffn2_w_up, m_ffn2_w_down, m_final_norm, v_ffn1_norm, v_ffn1_w_gate, v_ffn1_w_up, v_ffn1_w_down, v_mix_norm, v_w_in, v_sg_ln_g, v_sg_ln_b, v_sg_w, v_sg_b, v_w_att_out, v_w_sg_out, v_w_out, v_ffn2_norm, v_ffn2_w_gate, v_ffn2_w_up, v_ffn2_w_down, v_final_norm):
    S, D = x.shape[1], x.shape[2]
    Pb = w_in.shape[2]
    P = N_DEV * Pb
    assert P == GA_OFF + 2 * D and D % (N_DEV * 128) == 0 and S % (BLK * DILATIONS[-1]) == 0
    xs, tgt = x[0], loss_target[0]

    sharded = dict(ffn1_w_gate=ffn1_w_gate, ffn1_w_up=ffn1_w_up, ffn1_w_down=ffn1_w_down, w_in=w_in, w_att_out=w_att_out,
                   w_sg_out=w_sg_out, w_out=w_out, ffn2_w_gate=ffn2_w_gate, ffn2_w_up=ffn2_w_up, ffn2_w_down=ffn2_w_down)
    names = list(sharded)
    full = dict(zip(names, _all_gather([sharded[n][0].astype(BF) for n in names], "gather_weights")))
    wg1, wu1, wd1 = full["ffn1_w_gate"], full["ffn1_w_up"], full["ffn1_w_down"]
    wg2, wu2, wd2 = full["ffn2_w_gate"], full["ffn2_w_up"], full["ffn2_w_down"]
    watt, wsg = full["w_att_out"], full["w_sg_out"]
    win = full["w_in"].transpose(1, 0, 2).reshape(D, P)
    wout = full["w_out"].reshape(D, D)

    h1 = _rms_fwd(xs, ffn1_norm, "rms1")
    g1, u1, a1 = _ffn_up(h1, wg1, wu1, "ffn1_up")
    x1, h2 = _ffn_down_norm(a1, wd1, xs, mix_norm, "ffn1_down")
    proj = _mm_nn(h2, win, _tile(S, 1024), 512, "proj")
    tabs = _rope_tables(S)
    os, lses = [], []
    for gi, d in enumerate(DILATIONS):
        o, l = _att_fwd(proj, tabs, gi, d, f"att_fwd{gi}")
        os.append(o)
        lses.append(l)
    oatt, lse = _att_combine(os, lses, "att_combine")
    sgw = sg_w[0]
    sgbT = jnp.pad(sg_b[0].T, ((0, 0), (0, BLK - SG_GROUPS)))
    z = _sg_fwd(proj, sgw, sgbT, sg_ln_g, sg_ln_b, "sg_fwd")
    ya, ys, merged = _gate_merge(oatt, z, watt, wsg, proj, "gate_merge")
    x2, h3 = _mix_out(merged, wout, x1, ffn2_norm, "mix_out")
    g3, u3, a3 = _ffn_up(h3, wg2, wu2, "ffn2_up")
    dx3, dyb3, d_final, loss_part = _ffn_down_loss(a3, wd2, x2, final_norm.reshape(1, D), tgt, "ffn2_down_loss")

    dg3, du3 = _ffn_bwd_act(dyb3, wd2, g3, u3, "ffn2_bwd_act")
    dwd2 = _ffn_dwd(a3, dyb3, "ffn2_dwd")
    dwg2, dwu2 = _ffn_dwgu(h3, dg3, du3, "ffn2_dwgu")
    Fb = wg2.shape[2]
    dx2, dmixb, d_ffn2n = _dh_rms_bwd([(dg3, wg2), (du3, wu2)], True, Fb, x2, ffn2_norm, dx3, 1.0, "ffn2_dh")

    dya, dys, dga, dgs = _mix_bwd_gate(dmixb, wout, ya, ys, proj, "mix_bwd_gate")
    dwout = _mm_tn(merged, dmixb, _tile(D, 1024), _tile(D, 1024), _tile(S, 1024), False, "dw_out")
    Db = watt.shape[2]
    do, dz, dvec = _att_sg_dout(dya, dys, watt, wsg, oatt, "att_sg_dout")
    dwatt = _mm_tn(oatt, dya, GROUP_W, Db, _tile(S, 1024), True, "dw_att")
    dwsg = _mm_tn(z, dys, SG_W, Db, _tile(S, 1024), True, "dw_sg")
    du, dvs, d_sgw, d_sgbT, d_lng, d_lnb = _sg_bwd(proj, dz, sgw, sgbT, sg_ln_g, sg_ln_b, "sg_bwd")
    dqs, dks, dvs_att = [], [], []
    for gi, d in enumerate(DILATIONS):
        dq, dk, dv = _att_bwd(proj, tabs, do, lse, dvec, gi, d, f"att_bwd{gi}")
        dqs.append(dq)
        dks.append(dk)
        dvs_att.append(dv)
    dproj = jnp.concatenate(dqs + dks + dvs_att + [du, dvs, dga, dgs], axis=1)
    dx1, dyb1, d_mixn = _dh_rms_bwd([(dproj, win)], False, 512, x1, mix_norm, dx2, 0.5, "proj_dh")
    dwin = _mm_tn(h2, dproj, D, 512, _tile(S, 1024), False, "dw_in")
    dwin = dwin.reshape(D, N_DEV, Pb).transpose(1, 0, 2)

    dg1, du1 = _ffn_bwd_act(dyb1, wd1, g1, u1, "ffn1_bwd_act")
    dwd1 = _ffn_dwd(a1, dyb1, "ffn1_dwd")
    dwg1, dwu1 = _ffn_dwgu(h1, dg1, du1, "ffn1_dwgu")
    dx0, _, d_ffn1n = _dh_rms_bwd([(dg1, wg1), (du1, wu1)], True, Fb, xs, ffn1_norm, dx1, 1.0, "ffn1_dh")

    partial = dict(ffn1_w_gate=dwg1, ffn1_w_up=dwu1, ffn1_w_down=dwd1, w_in=dwin, w_att_out=dwatt, w_sg_out=dwsg,
                   w_out=dwout.reshape(N_DEV, D // N_DEV, D), ffn2_w_gate=dwg2, ffn2_w_up=dwu2, ffn2_w_down=dwd2)
    got = dict(zip(names, _reduce_scatter_exchange([partial[n] for n in names], "exchange_grads")))
    moments = dict(ffn1_w_gate=(m_ffn1_w_gate, v_ffn1_w_gate), ffn1_w_up=(m_ffn1_w_up, v_ffn1_w_up),
                   ffn1_w_down=(m_ffn1_w_down, v_ffn1_w_down), w_in=(m_w_in, v_w_in), w_att_out=(m_w_att_out, v_w_att_out),
                   w_sg_out=(m_w_sg_out, v_w_sg_out), w_out=(m_w_out, v_w_out), ffn2_w_gate=(m_ffn2_w_gate, v_ffn2_w_gate),
                   ffn2_w_up=(m_ffn2_w_up, v_ffn2_w_up), ffn2_w_down=(m_ffn2_w_down, v_ffn2_w_down))
    res = {}
    for n in names:
        mm, vv = moments[n]
        outs = _adamw(got[n], sharded[n][0], mm[0], vv[0], "adamw_" + n)
        res[n] = [o[None] for o in outs]

    rows = lambda a: a.reshape(-1, 128)
    small = [("sg_w", rows(d_sgw), sg_w, m_sg_w, v_sg_w), ("ffn1_norm", rows(d_ffn1n), ffn1_norm, m_ffn1_norm, v_ffn1_norm),
             ("mix_norm", rows(d_mixn), mix_norm, m_mix_norm, v_mix_norm), ("ffn2_norm", rows(d_ffn2n), ffn2_norm, m_ffn2_norm, v_ffn2_norm),
             ("final_norm", rows(d_final), final_norm, m_final_norm, v_final_norm), ("sg_ln_g", rows(d_lng), sg_ln_g, m_sg_ln_g, v_sg_ln_g),
             ("sg_ln_b", rows(d_lnb), sg_ln_b, m_sg_ln_b, v_sg_ln_b), ("sg_b", d_sgbT[:, :SG_GROUPS].T, sg_b, m_sg_b, v_sg_b)]
    pad8 = lambda a: _pad_rows(a, -(-a.shape[0] // 8) * 8)
    gpack = jnp.concatenate([pad8(g) for _, g, _, _, _ in small] + [pad8(loss_part)], axis=0)
    gsum = _small_allreduce(gpack, "allreduce_small")
    zero8 = jnp.zeros((8, 128), F32)
    wpack = jnp.concatenate([pad8(rows(w)) for _, _, w, _, _ in small] + [zero8], axis=0)
    mpack = jnp.concatenate([pad8(rows(m)) for _, _, _, m, _ in small] + [zero8], axis=0)
    vpack = jnp.concatenate([pad8(rows(v)) for _, _, _, _, v in small] + [zero8], axis=0)
    packs = _adamw(gsum[None], wpack, mpack, vpack, "adamw_small")
    off = 0
    for n, g, w, _, _ in small:
        r = g.shape[0]
        res[n] = [p[off:off + r].reshape(w.shape) for p in packs]
        off += -(-r // 8) * 8
    loss = gsum[off, 0]

    order = ["ffn1_norm", "ffn1_w_gate", "ffn1_w_up", "ffn1_w_down", "mix_norm", "w_in", "sg_ln_g", "sg_ln_b", "sg_w", "sg_b",
             "w_att_out", "w_sg_out", "w_out", "ffn2_norm", "ffn2_w_gate", "ffn2_w_up", "ffn2_w_down", "final_norm"]
    return (loss, dx0[None], *[res[n][0] for n in order], *[res[n][1] for n in order], *[res[n][2] for n in order],
            *[res[n][3] for n in order])
```

```python
import math

import jax
import jax.numpy as jnp
from jax import lax
from jax.experimental import pallas as pl
from jax.experimental.pallas import tpu as pltpu

BF = jnp.bfloat16
F32 = jnp.float32
MESH = pl.DeviceIdType.MESH
N_DEV = 8
N_CHIP = 4

HEAD_DIM = 128
HEADS_PER_GROUP = 4
GROUP_W = HEADS_PER_GROUP * HEAD_DIM
DILATIONS = (1, 4, 16)
ATT_W = len(DILATIONS) * GROUP_W
SG_W = 1536
SG_GROUPS = 12
BLK = 128
ROPE_DIM = 32
ROPE_THETA = 500000.0
NORM_EPS = 1e-6
LN_EPS = 1e-5
Q_OFF, K_OFF, V_OFF, U_OFF, VS_OFF, GA_OFF = 0, ATT_W, 2 * ATT_W, 3 * ATT_W, 3 * ATT_W + SG_W, 3 * ATT_W + 2 * SG_W

ADAM_LR, ADAM_B1, ADAM_B2, ADAM_EPS, ADAM_WD, ADAM_STEP = 0.001, 0.9, 0.999, 1e-08, 0.01, 10

VMEM_LIMIT = 56 * 1024 * 1024
NEG = -1e30
ANY = pl.BlockSpec(memory_space=pl.ANY)


def _tile(n, pref):
    t = min(n, pref)
    while n % t:
        t //= 2
    return t


def _nt(a, b):
    return lax.dot_general(a, b, (((1,), (1,)), ((), ())), preferred_element_type=F32)


def _tn(a, b):
    return lax.dot_general(a, b, (((0,), (0,)), ((), ())), preferred_element_type=F32)


def _nn(a, b):
    return jnp.dot(a, b, preferred_element_type=F32)


def _gelu(x):
    return 0.5 * x * (1.0 + lax.erf(x * (2.0 ** -0.5)))


def _gelu_grad(x):
    return 0.5 * (1.0 + lax.erf(x * (2.0 ** -0.5))) + x * jnp.exp(-0.5 * x * x) * (1.0 / math.sqrt(2.0 * math.pi))


def _place():
    x, y, c = lax.axis_index("x"), lax.axis_index("y"), lax.axis_index("c")
    return x, y, c


def _flip(v, bit):
    return 1 - v if bit else v


class _Gather:
    def __init__(self, shards):
        self.arrays = list(shards)
        nw = len(shards)
        self.out_shape = [jax.ShapeDtypeStruct((N_DEV,) + s.shape, s.dtype) for s in shards]
        self.scratch = [pltpu.SemaphoreType.DMA((nw, 7)), pltpu.SemaphoreType.DMA((nw, 7)), pltpu.SemaphoreType.DMA((nw,))]

    def _parts(self, ins, outs, sems):
        x, y, c = _place()
        send, recv, loc = sems
        chips = [(1 - x, y), (x, 1 - y), (1 - x, 1 - y)]

        def copy(k, s, block, to, src=None):
            dst = outs[k].at[4 * block[0] + 2 * block[1] + block[2]]
            return pltpu.make_async_remote_copy(src_ref=dst if src is None else src, dst_ref=dst, send_sem=send.at[k, s],
                                                recv_sem=recv.at[k, s], device_id=to, device_id_type=MESH)

        def first(k):
            return [copy(k, 0, (x, y, c), (x, y, 1 - c), src=ins[k])] + [
                copy(k, 1 + j, (x, y, c), (*chip, c), src=ins[k]) for j, chip in enumerate(chips)]

        def local(k):
            return pltpu.make_async_copy(ins[k], outs[k].at[4 * x + 2 * y + c], loc.at[k])

        return x, y, c, chips, copy, first, local

    def start(self, ins, outs, sems):
        _, _, _, _, _, first, local = self._parts(ins, outs, sems)
        for k in range(len(ins)):
            local(k).start()
            for cp in first(k):
                cp.start()

    def mid(self, ins, outs, sems):
        x, y, c, chips, copy, _, _ = self._parts(ins, outs, sems)
        for k in range(len(ins)):
            for j, chip in enumerate(chips):
                copy(k, 1 + j, (*chip, c), (x, y, c)).wait_recv()
                copy(k, 4 + j, (*chip, c), (x, y, 1 - c)).start()

    def finish(self, ins, outs, sems):
        x, y, c, chips, copy, first, local = self._parts(ins, outs, sems)
        for k in range(len(ins)):
            copy(k, 0, (x, y, 1 - c), (x, y, c)).wait_recv()
            for j, chip in enumerate(chips):
                copy(k, 4 + j, (*chip, 1 - c), (x, y, c)).wait_recv()
        for k in range(len(ins)):
            for cp in first(k):
                cp.wait_send()
            for j, chip in enumerate(chips):
                copy(k, 4 + j, (*chip, c), (x, y, 1 - c)).wait_send()
            local(k).wait()


class _Swap:
    def __init__(self, parts):
        self.arrays = list(parts)
        nw = len(parts)
        self.out_shape = [jax.ShapeDtypeStruct((N_CHIP,) + p.shape[1:], p.dtype) for p in parts]
        self.scratch = [pltpu.SemaphoreType.DMA((nw, N_CHIP)), pltpu.SemaphoreType.DMA((nw, N_CHIP))]

    def _copy(self, ins, outs, sems, k, q):
        x, y, c = _place()
        return pltpu.make_async_remote_copy(src_ref=ins[k].at[2 * q + 1 - c], dst_ref=outs[k].at[q], send_sem=sems[0].at[k, q],
                                            recv_sem=sems[1].at[k, q], device_id=(x, y, 1 - c), device_id_type=MESH)

    def start(self, ins, outs, sems):
        for k in range(len(ins)):
            for q in range(N_CHIP):
                self._copy(ins, outs, sems, k, q).start()

    def mid(self, ins, outs, sems):
        pass

    def finish(self, ins, outs, sems):
        for k in range(len(ins)):
            for q in range(N_CHIP):
                self._copy(ins, outs, sems, k, q).wait()


class _Ici:
    def __init__(self, sums):
        self.arrays = list(sums)
        nw = len(sums)
        self.out_shape = [jax.ShapeDtypeStruct(s.shape, s.dtype) for s in sums]
        self.scratch = [pltpu.SemaphoreType.DMA((nw, 3)), pltpu.SemaphoreType.DMA((nw, 3)), pltpu.SemaphoreType.DMA((nw,))]

    def _copies(self, ins, outs, sems, k):
        x, y, c = _place()
        myq = 2 * x + y
        out = []
        for r in range(1, N_CHIP):
            px, py = _flip(x, r & 2), _flip(y, r & 1)
            pq = 2 * px + py
            mk = lambda dst: pltpu.make_async_remote_copy(src_ref=ins[k].at[pq], dst_ref=dst, send_sem=sems[0].at[k, r - 1],
                                                          recv_sem=sems[1].at[k, r - 1], device_id=(px, py, c), device_id_type=MESH)
            out.append((mk(outs[k].at[myq]), mk(outs[k].at[pq])))
        return out, pltpu.make_async_copy(ins[k].at[myq], outs[k].at[myq], sems[2].at[k])

    def start(self, ins, outs, sems):
        for k in range(len(ins)):
            remote, local = self._copies(ins, outs, sems, k)
            local.start()
            for snd, _ in remote:
                snd.start()

    def mid(self, ins, outs, sems):
        pass

    def finish(self, ins, outs, sems):
        for k in range(len(ins)):
            remote, local = self._copies(ins, outs, sems, k)
            for snd, rcv in remote:
                rcv.wait_recv()
                snd.wait_send()
            local.wait()


def _call(body, *, grid, in_specs, out_specs, out_shape, name, args, scratch=(), comm=()):
    comm = list(comm)
    n_in, n_out, n_scr = len(in_specs), len(out_specs), len(scratch)
    total = math.prod(grid) if grid else 1

    def wrapped(*refs):
        p = n_in
        cin = []
        for cm in comm:
            cin.append(refs[p:p + len(cm.arrays)])
            p += len(cm.arrays)
        own_out = refs[p:p + n_out]
        p += n_out
        cout = []
        for cm in comm:
            cout.append(refs[p:p + len(cm.arrays)])
            p += len(cm.arrays)
        own_scr = refs[p:p + n_scr]
        p += n_scr
        csem = []
        for cm in comm:
            csem.append(refs[p:p + len(cm.scratch)])
            p += len(cm.scratch)
        step = 0
        for axis, g in enumerate(grid):
            step = step * g + pl.program_id(axis)

        def at(when, what):
            if total == 1:
                what()
            else:
                pl.when(step == when)(what)

        def starts():
            for cm, i, o, s in zip(comm, cin, cout, csem):
                cm.start(i, o, s)

        def mids():
            for cm, i, o, s in zip(comm, cin, cout, csem):
                cm.mid(i, o, s)

        def finishes():
            for cm, i, o, s in zip(comm, cin, cout, csem):
                cm.finish(i, o, s)

        if comm:
            at(0, starts)
        if body is not None:
            body(*refs[:n_in], *own_out, *own_scr)
        if comm:
            at(total // 2, mids)
            at(total - 1, finishes)

    kw = dict(grid=tuple(grid)) if grid else {}
    outs = pl.pallas_call(
        wrapped, name=name, **kw,
        in_specs=list(in_specs) + [ANY for cm in comm for _ in cm.arrays],
        out_specs=list(out_specs) + [ANY for cm in comm for _ in cm.arrays],
        out_shape=list(out_shape) + [s for cm in comm for s in cm.out_shape],
        scratch_shapes=list(scratch) + [s for cm in comm for s in cm.scratch],
        compiler_params=pltpu.CompilerParams(dimension_semantics=("arbitrary",) * len(grid), vmem_limit_bytes=VMEM_LIMIT),
    )(*args, *[a for cm in comm for a in cm.arrays])
    own, p, per = list(outs[:n_out]), n_out, []
    for cm in comm:
        per.append(list(outs[p:p + len(cm.arrays)]))
        p += len(cm.arrays)
    return own, per


def _comm_only(cm, name):
    return _call(None, grid=(), in_specs=[], out_specs=[], out_shape=[], name=name, args=[], comm=[cm])[1][0]


def _rms_fwd(x, g, name):
    S, D = x.shape
    tm = _tile(S, 512)

    def body(x_ref, g_ref, o_ref):
        xv = x_ref[...]
        r = lax.rsqrt(jnp.mean(xv * xv, axis=-1, keepdims=True) + NORM_EPS)
        o_ref[...] = (xv * r * g_ref[...]).astype(BF)

    return _call(body, grid=(S // tm,), name=name, args=[x, g],
                 in_specs=[pl.BlockSpec((tm, D), lambda i: (i, 0)), pl.BlockSpec((1, D), lambda i: (0, 0))],
                 out_specs=[pl.BlockSpec((tm, D), lambda i: (i, 0))], out_shape=[jax.ShapeDtypeStruct((S, D), BF)])[0][0]


def _ffn_up(h, wg, wu, name, comm=()):
    S, D = h.shape
    nb, _, Fb = wg.shape
    tm = _tile(S, 512)

    def body(h_ref, wg_ref, wu_ref, g_ref, u_ref, a_ref):
        hv = h_ref[...]
        g = _nn(hv, wg_ref[0])
        u = _nn(hv, wu_ref[0])
        g_ref[0] = g.astype(BF)
        u_ref[0] = u.astype(BF)
        a_ref[0] = (g * jax.nn.sigmoid(g) * u).astype(BF)

    act = pl.BlockSpec((1, tm, Fb), lambda j, i: (j, i, 0))
    w = pl.BlockSpec((1, D, Fb), lambda j, i: (j, 0, 0))
    shp = jax.ShapeDtypeStruct((nb, S, Fb), BF)
    return _call(body, grid=(nb, S // tm), name=name, args=[h, wg, wu], comm=comm,
                 in_specs=[pl.BlockSpec((tm, D), lambda j, i: (i, 0)), w, w], out_specs=[act, act, act], out_shape=[shp, shp, shp])


def _ffn_down_norm(a, wd, x, gn, name, comm=()):
    nb, S, Fb = a.shape
    D = wd.shape[2]
    tm = _tile(S, 256)

    def body(a_ref, wd_ref, x_ref, gn_ref, xo_ref, hn_ref, acc_ref):
        j = pl.program_id(1)

        @pl.when(j == 0)
        def _():
            acc_ref[...] = jnp.zeros_like(acc_ref)

        acc_ref[...] += _nn(a_ref[0], wd_ref[0])

        @pl.when(j == nb - 1)
        def _():
            xo = x_ref[...] + 0.5 * acc_ref[...]
            r = lax.rsqrt(jnp.mean(xo * xo, axis=-1, keepdims=True) + NORM_EPS)
            xo_ref[...] = xo
            hn_ref[...] = (xo * r * gn_ref[...]).astype(BF)

    row = pl.BlockSpec((tm, D), lambda i, j: (i, 0))
    return _call(body, grid=(S // tm, nb), name=name, args=[a, wd, x, gn], comm=comm,
                 in_specs=[pl.BlockSpec((1, tm, Fb), lambda i, j: (j, i, 0)), pl.BlockSpec((1, Fb, D), lambda i, j: (j, 0, 0)), row,
                           pl.BlockSpec((1, D), lambda i, j: (0, 0))],
                 out_specs=[row, row], out_shape=[jax.ShapeDtypeStruct((S, D), F32), jax.ShapeDtypeStruct((S, D), BF)],
                 scratch=[pltpu.VMEM((tm, D), F32)])


def _ffn_down_loss(a, wd, x, gf, tgt, name):
    nb, S, Fb = a.shape
    D = wd.shape[2]
    tm = _tile(S, 256)

    def body(a_ref, wd_ref, x_ref, gf_ref, t_ref, dx_ref, dxb_ref, dgf_ref, loss_ref, acc_ref):
        i, j = pl.program_id(0), pl.program_id(1)

        @pl.when(j == 0)
        def _():
            acc_ref[...] = jnp.zeros_like(acc_ref)

        acc_ref[...] += _nn(a_ref[0], wd_ref[0])

        @pl.when((j == nb - 1) & (i == 0))
        def _():
            dgf_ref[...] = jnp.zeros_like(dgf_ref)
            loss_ref[...] = jnp.zeros_like(loss_ref)

        @pl.when(j == nb - 1)
        def _():
            xo = x_ref[...] + 0.5 * acc_ref[...]
            r = lax.rsqrt(jnp.mean(xo * xo, axis=-1, keepdims=True) + NORM_EPS)
            xh = xo * r
            gf = gf_ref[...]
            e = xh * gf - t_ref[...]
            loss_ref[...] += jnp.sum(jnp.mean(e * e, axis=-1, keepdims=True), axis=0, keepdims=True) * 0.5
            dy = e * (1.0 / D)
            dgf_ref[...] += jnp.sum(dy * xh, axis=0, keepdims=True)
            dxh = dy * gf
            dx = r * (dxh - xh * jnp.mean(dxh * xh, axis=-1, keepdims=True))
            dx_ref[...] = dx
            dxb_ref[...] = (0.5 * dx).astype(BF)

    row = pl.BlockSpec((tm, D), lambda i, j: (i, 0))
    vec = pl.BlockSpec((1, D), lambda i, j: (0, 0))
    return _call(body, grid=(S // tm, nb), name=name, args=[a, wd, x, gf, tgt],
                 in_specs=[pl.BlockSpec((1, tm, Fb), lambda i, j: (j, i, 0)), pl.BlockSpec((1, Fb, D), lambda i, j: (j, 0, 0)), row, vec, row],
                 out_specs=[row, row, vec, pl.BlockSpec((1, 128), lambda i, j: (0, 0))],
                 out_shape=[jax.ShapeDtypeStruct((S, D), F32), jax.ShapeDtypeStruct((S, D), BF), jax.ShapeDtypeStruct((1, D), F32),
                            jax.ShapeDtypeStruct((1, 128), F32)],
                 scratch=[pltpu.VMEM((tm, D), F32)])[0]


def _ffn_bwd_act(dyb, wd, g, u, name, comm=()):
    S, D = dyb.shape
    nb, Fb, _ = wd.shape
    tm = _tile(S, 512)

    def body(dy_ref, wd_ref, g_ref, u_ref, dg_ref, du_ref):
        da = _nt(dy_ref[...], wd_ref[0])
        gv = g_ref[0].astype(F32)
        uv = u_ref[0].astype(F32)
        sg = jax.nn.sigmoid(gv)
        du_ref[0] = (da * gv * sg).astype(BF)
        dg_ref[0] = (da * uv * sg * (1.0 + gv * (1.0 - sg))).astype(BF)

    act = pl.BlockSpec((1, tm, Fb), lambda j, i: (j, i, 0))
    shp = jax.ShapeDtypeStruct((nb, S, Fb), BF)
    return _call(body, grid=(nb, S // tm), name=name, args=[dyb, wd, g, u], comm=comm,
                 in_specs=[pl.BlockSpec((tm, D), lambda j, i: (i, 0)), pl.BlockSpec((1, Fb, D), lambda j, i: (j, 0, 0)), act, act],
                 out_specs=[act, act], out_shape=[shp, shp])


def _ffn_dwd(a, dyb, name, comm=()):
    nb, S, Fb = a.shape
    D = dyb.shape[1]
    ts = _tile(S, 512)
    ns = S // ts

    def body(a_ref, dy_ref, o_ref, acc_ref):
        s = pl.program_id(1)

        @pl.when(s == 0)
        def _():
            acc_ref[...] = jnp.zeros_like(acc_ref)

        acc_ref[...] += _tn(a_ref[0], dy_ref[...])

        @pl.when(s == ns - 1)
        def _():
            o_ref[0] = acc_ref[...].astype(BF)

    return _call(body, grid=(nb, ns), name=name, args=[a, dyb], comm=comm,
                 in_specs=[pl.BlockSpec((1, ts, Fb), lambda j, s: (j, s, 0)), pl.BlockSpec((ts, D), lambda j, s: (s, 0))],
                 out_specs=[pl.BlockSpec((1, Fb, D), lambda j, s: (j, 0, 0))], out_shape=[jax.ShapeDtypeStruct((nb, Fb, D), BF)],
                 scratch=[pltpu.VMEM((Fb, D), F32)])


def _ffn_dwgu(h, dg, du, name, comm=()):
    S, D = h.shape
    nb, _, Fb = dg.shape
    ts = _tile(S, 512)
    ns = S // ts

    def body(h_ref, dg_ref, du_ref, og_ref, ou_ref, accg_ref, accu_ref):
        s = pl.program_id(1)

        @pl.when(s == 0)
        def _():
            accg_ref[...] = jnp.zeros_like(accg_ref)
            accu_ref[...] = jnp.zeros_like(accu_ref)

        hv = h_ref[...]
        accg_ref[...] += _tn(hv, dg_ref[0])
        accu_ref[...] += _tn(hv, du_ref[0])

        @pl.when(s == ns - 1)
        def _():
            og_ref[0] = accg_ref[...].astype(BF)
            ou_ref[0] = accu_ref[...].astype(BF)

    act = pl.BlockSpec((1, ts, Fb), lambda j, s: (j, s, 0))
    out = pl.BlockSpec((1, D, Fb), lambda j, s: (j, 0, 0))
    shp = jax.ShapeDtypeStruct((nb, D, Fb), BF)
    return _call(body, grid=(nb, ns), name=name, args=[h, dg, du], comm=comm,
                 in_specs=[pl.BlockSpec((ts, D), lambda j, s: (s, 0)), act, act], out_specs=[out, out], out_shape=[shp, shp],
                 scratch=[pltpu.VMEM((D, Fb), F32), pltpu.VMEM((D, Fb), F32)])


def _dh_rms_bwd(pairs, blocked, tk, x, gn, dxo, out_scale, name, comm=()):
    S, D = x.shape
    nk = pairs[0][0].shape[0] if blocked else pairs[0][0].shape[1] // tk
    tm = _tile(S, 256)
    npair = len(pairs)

    def body(*refs):
        ins = refs[: 2 * npair]
        x_ref, gn_ref, dxo_ref, dx_ref, dxb_ref, dgn_ref, acc_ref = refs[2 * npair:]
        i, k = pl.program_id(0), pl.program_id(1)

        @pl.when(k == 0)
        def _():
            acc_ref[...] = jnp.zeros_like(acc_ref)

        for p in range(npair):
            l_ref, r_ref = ins[2 * p], ins[2 * p + 1]
            if blocked:
                acc_ref[...] += _nt(l_ref[0], r_ref[0])
            else:
                acc_ref[...] += _nt(l_ref[...], r_ref[...])

        @pl.when((k == nk - 1) & (i == 0))
        def _():
            dgn_ref[...] = jnp.zeros_like(dgn_ref)

        @pl.when(k == nk - 1)
        def _():
            xv = x_ref[...]
            r = lax.rsqrt(jnp.mean(xv * xv, axis=-1, keepdims=True) + NORM_EPS)
            xh = xv * r
            dh = acc_ref[...]
            dgn_ref[...] += jnp.sum(dh * xh, axis=0, keepdims=True)
            dxh = dh * gn_ref[...]
            dx = dxo_ref[...] + r * (dxh - xh * jnp.mean(dxh * xh, axis=-1, keepdims=True))
            dx_ref[...] = dx
            dxb_ref[...] = (out_scale * dx).astype(BF)

    if blocked:
        lspec = pl.BlockSpec((1, tm, tk), lambda i, k: (k, i, 0))
        rspec = pl.BlockSpec((1, D, tk), lambda i, k: (k, 0, 0))
    else:
        lspec = pl.BlockSpec((tm, tk), lambda i, k: (i, k))
        rspec = pl.BlockSpec((D, tk), lambda i, k: (0, k))
    row = pl.BlockSpec((tm, D), lambda i, k: (i, 0))
    vec = pl.BlockSpec((1, D), lambda i, k: (0, 0))
    flat = [t for pr in pairs for t in pr]
    return _call(body, grid=(S // tm, nk), name=name, args=[*flat, x, gn, dxo], comm=comm,
                 in_specs=[lspec, rspec] * npair + [row, vec, row], out_specs=[row, row, vec],
                 out_shape=[jax.ShapeDtypeStruct((S, D), F32), jax.ShapeDtypeStruct((S, D), BF), jax.ShapeDtypeStruct((1, D), F32)],
                 scratch=[pltpu.VMEM((tm, D), F32)])


def _mm_nn(a, b, tm, tn, name, comm=()):
    M, K = a.shape
    N = b.shape[1]

    def body(a_ref, b_ref, o_ref):
        o_ref[...] = _nn(a_ref[...], b_ref[...]).astype(BF)

    return _call(body, grid=(N // tn, M // tm), name=name, args=[a, b], comm=comm,
                 in_specs=[pl.BlockSpec((tm, K), lambda n, i: (i, 0)), pl.BlockSpec((K, tn), lambda n, i: (0, n))],
                 out_specs=[pl.BlockSpec((tm, tn), lambda n, i: (i, n))], out_shape=[jax.ShapeDtypeStruct((M, N), BF)])


def _mm_tn(a, b, tm, tn, ts, blocked, name, comm=()):
    S, M = a.shape
    N = b.shape[1]
    ns = S // ts

    def body(a_ref, b_ref, o_ref, acc_ref):
        s = pl.program_id(2)

        @pl.when(s == 0)
        def _():
            acc_ref[...] = jnp.zeros_like(acc_ref)

        acc_ref[...] += _tn(a_ref[...], b_ref[...])

        @pl.when(s == ns - 1)
        def _():
            if blocked:
                o_ref[0] = acc_ref[...].astype(BF)
            else:
                o_ref[...] = acc_ref[...].astype(BF)

    if blocked:
        ospec = pl.BlockSpec((1, tm, tn), lambda i, n, s: (n, i, 0))
        oshape = jax.ShapeDtypeStruct((N // tn, M, tn), BF)
    else:
        ospec = pl.BlockSpec((tm, tn), lambda i, n, s: (i, n))
        oshape = jax.ShapeDtypeStruct((M, N), BF)
    return _call(body, grid=(M // tm, N // tn, ns), name=name, args=[a, b], comm=comm,
                 in_specs=[pl.BlockSpec((ts, tm), lambda i, n, s: (s, i)), pl.BlockSpec((ts, tn), lambda i, n, s: (s, n))],
                 out_specs=[ospec], out_shape=[oshape], scratch=[pltpu.VMEM((tm, tn), F32)])


def _rope_tables(S):
    half = ROPE_DIM // 2
    inv_freq = ROPE_THETA ** (-jnp.arange(0, ROPE_DIM, 2, dtype=F32) / ROPE_DIM)
    ang = jnp.arange(S, dtype=F32)[:, None] * inv_freq[None, :]
    cos, sin = jnp.cos(ang), jnp.sin(ang)
    zeros = jnp.zeros((S, HEAD_DIM - ROPE_DIM), F32)
    c = jnp.concatenate([cos, cos, jnp.ones((S, HEAD_DIM - ROPE_DIM), F32)], axis=1)
    sm = jnp.concatenate([-sin, jnp.zeros((S, half), F32), zeros], axis=1)
    sp = jnp.concatenate([jnp.zeros((S, half), F32), sin, zeros], axis=1)
    return c, sm, sp


def _rope(t, c, sm, sp):
    return t * c + pltpu.roll(t, HEAD_DIM - ROPE_DIM // 2, 1) * sm + pltpu.roll(t, ROPE_DIM // 2, 1) * sp


def _rope_t(dy, c, sm, sp):
    return dy * c + pltpu.roll(dy * sm, ROPE_DIM // 2, 1) + pltpu.roll(dy * sp, HEAD_DIM - ROPE_DIM // 2, 1)


def _att_mask(i):
    qi = lax.broadcasted_iota(jnp.int32, (BLK, 2 * BLK), 0)
    kj = lax.broadcasted_iota(jnp.int32, (BLK, 2 * BLK), 1)
    diff = qi + BLK - kj
    first_key = jnp.where(i > 0, 0, BLK)
    return (diff >= 0) & (diff <= BLK) & (kj >= first_key)


def _att_specs(S, P, gi, d):
    L = S // d
    pcols = P // HEAD_DIM

    def sect(off):
        base = (off + gi * GROUP_W) // HEAD_DIM
        return pl.BlockSpec((L, HEAD_DIM), lambda r, hh: (0, r * pcols + base + hh))

    tab = pl.BlockSpec((L, HEAD_DIM), lambda r, hh: (0, r))
    head = pl.BlockSpec((L, HEAD_DIM), lambda r, hh: (0, r * HEADS_PER_GROUP + hh))
    return L, sect, tab, head


def _att_fwd(proj, tabs, gi, d, name):
    S, P = proj.shape
    L, sect, tab, head = _att_specs(S, P, gi, d)
    nblk = L // BLK
    scale = HEAD_DIM ** -0.5
    pv = proj.reshape(L, d * P)
    tv = [t.reshape(L, d * HEAD_DIM) for t in tabs]

    def body(q_ref, k_ref, v_ref, c_ref, sm_ref, sp_ref, o_ref, lse_ref, qr, kp, vp):
        c, sm, sp = c_ref[...], sm_ref[...], sp_ref[...]
        qr[...] = _rope(q_ref[...].astype(F32), c, sm, sp).astype(BF)
        kp[pl.ds(0, BLK), :] = jnp.zeros((BLK, HEAD_DIM), BF)
        vp[pl.ds(0, BLK), :] = jnp.zeros((BLK, HEAD_DIM), BF)
        kp[pl.ds(BLK, L), :] = _rope(k_ref[...].astype(F32), c, sm, sp).astype(BF)
        vp[pl.ds(BLK, L), :] = v_ref[...]

        def blk(i, carry):
            r0 = pl.multiple_of(i * BLK, BLK)
            s = _nt(qr[pl.ds(r0, BLK), :], kp[pl.ds(r0, 2 * BLK), :]) * scale
            s = jnp.where(_att_mask(i), s, NEG)
            m = jnp.max(s, axis=-1, keepdims=True)
            p = jnp.exp(s - m)
            l = jnp.sum(p, axis=-1, keepdims=True)
            o_ref[pl.ds(r0, BLK), :] = _nn(p.astype(BF), vp[pl.ds(r0, 2 * BLK), :]) / l
            lse_ref[pl.ds(r0, BLK), :] = jnp.broadcast_to(m + jnp.log(l), (BLK, HEAD_DIM))
            return carry

        lax.fori_loop(0, nblk, blk, 0)

    shp = jax.ShapeDtypeStruct((L, d * GROUP_W), F32)
    o, lse = _call(body, grid=(d, HEADS_PER_GROUP), name=name, args=[pv, pv, pv, *tv],
                   in_specs=[sect(Q_OFF), sect(K_OFF), sect(V_OFF), tab, tab, tab], out_specs=[head, head], out_shape=[shp, shp],
                   scratch=[pltpu.VMEM((L, HEAD_DIM), BF), pltpu.VMEM((L + BLK, HEAD_DIM), BF), pltpu.VMEM((L + BLK, HEAD_DIM), BF)])[0]
    return o.reshape(S, GROUP_W), lse.reshape(S, GROUP_W)


def _att_combine(os, lses, name):
    S = os[0].shape[0]
    tm = _tile(S, 512)

    def body(o0, o1, o2, l0, l1, l2, oa_ref, lse_ref):
        a, b, c = l0[...], l1[...], l2[...]
        mx = jnp.maximum(jnp.maximum(a, b), c)
        wa, wb, wc = jnp.exp(a - mx), jnp.exp(b - mx), jnp.exp(c - mx)
        den = wa + wb + wc
        oa_ref[...] = ((wa * o0[...] + wb * o1[...] + wc * o2[...]) / den).astype(BF)
        lse_ref[...] = mx + jnp.log(den)

    row = pl.BlockSpec((tm, GROUP_W), lambda i: (i, 0))
    return _call(body, grid=(S // tm,), name=name, args=[*os, *lses], in_specs=[row] * 6, out_specs=[row, row],
                 out_shape=[jax.ShapeDtypeStruct((S, GROUP_W), BF), jax.ShapeDtypeStruct((S, GROUP_W), F32)])[0]


def _att_bwd(proj, tabs, do, lse, dvec, gi, d, name):
    S, P = proj.shape
    L, sect, tab, head = _att_specs(S, P, gi, d)
    nblk = L // BLK
    scale = HEAD_DIM ** -0.5
    pv = proj.reshape(L, d * P)
    tv = [t.reshape(L, d * HEAD_DIM) for t in tabs]
    hv = [t.reshape(L, d * GROUP_W) for t in (do, lse, dvec)]

    def body(q_ref, k_ref, v_ref, c_ref, sm_ref, sp_ref, do_ref, lse_ref, dv_ref, dq_out, dk_out, dv_out, qr, kp, vp, dkp, dvp):
        c, sm, sp = c_ref[...], sm_ref[...], sp_ref[...]
        qr[...] = _rope(q_ref[...].astype(F32), c, sm, sp).astype(BF)
        kp[pl.ds(0, BLK), :] = jnp.zeros((BLK, HEAD_DIM), BF)
        vp[pl.ds(0, BLK), :] = jnp.zeros((BLK, HEAD_DIM), BF)
        kp[pl.ds(BLK, L), :] = _rope(k_ref[...].astype(F32), c, sm, sp).astype(BF)
        vp[pl.ds(BLK, L), :] = v_ref[...]
        dkp[...] = jnp.zeros_like(dkp)
        dvp[...] = jnp.zeros_like(dvp)

        def blk(i, carry):
            r0 = pl.multiple_of(i * BLK, BLK)
            rows, win = pl.ds(r0, BLK), pl.ds(r0, 2 * BLK)
            q, kw, vw, dob = qr[rows, :], kp[win, :], vp[win, :], do_ref[rows, :]
            s = jnp.where(_att_mask(i), _nt(q, kw) * scale, NEG)
            p = jnp.exp(s - lse_ref[rows, :][:, :1])
            ds = p * (_nt(dob, vw) - dv_ref[rows, :][:, :1]) * scale
            dsb = ds.astype(BF)
            dq_out[rows, :] = _rope_t(_nn(dsb, kw), c_ref[rows, :], sm_ref[rows, :], sp_ref[rows, :]).astype(BF)
            dkp[win, :] += _tn(dsb, q)
            dvp[win, :] += _tn(p.astype(BF), dob)
            return carry

        lax.fori_loop(0, nblk, blk, 0)
        dk_out[...] = _rope_t(dkp[pl.ds(BLK, L), :], c, sm, sp).astype(BF)
        dv_out[...] = dvp[pl.ds(BLK, L), :].astype(BF)

    shp = jax.ShapeDtypeStruct((L, d * GROUP_W), BF)
    outs = _call(body, grid=(d, HEADS_PER_GROUP), name=name, args=[pv, pv, pv, *tv, *hv],
                 in_specs=[sect(Q_OFF), sect(K_OFF), sect(V_OFF), tab, tab, tab, head, head, head],
                 out_specs=[head, head, head], out_shape=[shp, shp, shp],
                 scratch=[pltpu.VMEM((L, HEAD_DIM), BF), pltpu.VMEM((L + BLK, HEAD_DIM), BF), pltpu.VMEM((L + BLK, HEAD_DIM), BF),
                          pltpu.VMEM((L + BLK, HEAD_DIM), F32), pltpu.VMEM((L + BLK, HEAD_DIM), F32)])[0]
    return [t.reshape(S, GROUP_W) for t in outs]


def _sg_parts(u_ref, vs_ref, g_ref, b_ref):
    uv = u_ref[...].astype(F32)
    vv = vs_ref[...].astype(F32)
    vg = _gelu(vv)
    mu = jnp.mean(vg, axis=-1, keepdims=True)
    vc = vg - mu
    rs = lax.rsqrt(jnp.mean(vc * vc, axis=-1, keepdims=True) + LN_EPS)
    y = vc * rs
    return uv, vv, rs, y, y * g_ref[...] + b_ref[...]


def _sg_wmask():
    t = lax.broadcasted_iota(jnp.int32, (BLK, BLK), 0)
    s = lax.broadcasted_iota(jnp.int32, (BLK, BLK), 1)
    return s <= t


def _sg_fwd(proj, sgw, sgbT, lng, lnb, name):
    S, P = proj.shape

    def body(u_ref, vs_ref, w_ref, bt_ref, g_ref, b_ref, z_ref):
        uv, _, _, _, vln = _sg_parts(u_ref, vs_ref, g_ref, b_ref)
        ug = _gelu(uv)
        vb = vln.astype(BF)
        mask = _sg_wmask()
        bt = bt_ref[...]
        for g in range(SG_GROUPS):
            cols = slice(g * BLK, (g + 1) * BLK)
            w = jnp.where(mask, w_ref[g], 0.0).astype(BF)
            sp = _nn(w, vb[:, cols]) + bt[:, g:g + 1]
            z_ref[:, cols] = (ug[:, cols] * sp).astype(BF)

    tile = lambda off: pl.BlockSpec((BLK, SG_W), lambda i: (i, off // SG_W))
    full = lambda shape: pl.BlockSpec(shape, lambda i: (0,) * len(shape))
    return _call(body, grid=(S // BLK,), name=name, args=[proj, proj, sgw, sgbT, lng, lnb],
                 in_specs=[tile(U_OFF), tile(VS_OFF), full((SG_GROUPS, BLK, BLK)), full((BLK, BLK)), full((1, SG_W)), full((1, SG_W))],
                 out_specs=[pl.BlockSpec((BLK, SG_W), lambda i: (i, 0))], out_shape=[jax.ShapeDtypeStruct((S, SG_W), BF)])[0][0]


def _sg_bwd(proj, dz, sgw, sgbT, lng, lnb, name):
    S, P = proj.shape

    def body(u_ref, vs_ref, dz_ref, w_ref, bt_ref, g_ref, b_ref, du_ref, dvs_ref, dw_ref, dbt_ref, dg_ref, db_ref, dvln):
        @pl.when(pl.program_id(0) == 0)
        def _():
            dw_ref[...] = jnp.zeros_like(dw_ref)
            dbt_ref[...] = jnp.zeros_like(dbt_ref)
            dg_ref[...] = jnp.zeros_like(dg_ref)
            db_ref[...] = jnp.zeros_like(db_ref)

        uv, vv, rs, y, vln = _sg_parts(u_ref, vs_ref, g_ref, b_ref)
        ug = _gelu(uv)
        vb = vln.astype(BF)
        dzv = dz_ref[...].astype(F32)
        dsp = dzv * ug
        dspb = dsp.astype(BF)
        mask = _sg_wmask()
        bt = bt_ref[...]
        lane = lax.broadcasted_iota(jnp.int32, (BLK, BLK), 1)
        dbt = jnp.zeros((BLK, BLK), F32)
        for g in range(SG_GROUPS):
            cols = slice(g * BLK, (g + 1) * BLK)
            w = jnp.where(mask, w_ref[g], 0.0).astype(BF)
            sp = _nn(w, vb[:, cols]) + bt[:, g:g + 1]
            du_ref[:, cols] = (dzv[:, cols] * sp * _gelu_grad(uv[:, cols])).astype(BF)
            dw_ref[g] += jnp.where(mask, _nt(dspb[:, cols], vb[:, cols]), 0.0)
            dbt = dbt + jnp.where(lane == g, jnp.sum(dsp[:, cols], axis=-1, keepdims=True), 0.0)
            dvln[:, cols] = _tn(w, dspb[:, cols])
        dbt_ref[...] += dbt
        dvl = dvln[...]
        dg_ref[...] += jnp.sum(dvl * y, axis=0, keepdims=True)
        db_ref[...] += jnp.sum(dvl, axis=0, keepdims=True)
        dy = dvl * g_ref[...]
        dvg = rs * (dy - jnp.mean(dy, axis=-1, keepdims=True) - y * jnp.mean(dy * y, axis=-1, keepdims=True))
        dvs_ref[...] = (dvg * _gelu_grad(vv)).astype(BF)

    tile = lambda off: pl.BlockSpec((BLK, SG_W), lambda i: (i, off // SG_W))
    full = lambda shape: pl.BlockSpec(shape, lambda i: (0,) * len(shape))
    row = pl.BlockSpec((BLK, SG_W), lambda i: (i, 0))
    return _call(body, grid=(S // BLK,), name=name, args=[proj, proj, dz, sgw, sgbT, lng, lnb],
                 in_specs=[tile(U_OFF), tile(VS_OFF), row, full((SG_GROUPS, BLK, BLK)), full((BLK, BLK)), full((1, SG_W)), full((1, SG_W))],
                 out_specs=[row, row, full((SG_GROUPS, BLK, BLK)), full((BLK, BLK)), full((1, SG_W)), full((1, SG_W))],
                 out_shape=[jax.ShapeDtypeStruct((S, SG_W), BF), jax.ShapeDtypeStruct((S, SG_W), BF),
                            jax.ShapeDtypeStruct((SG_GROUPS, BLK, BLK), F32), jax.ShapeDtypeStruct((BLK, BLK), F32),
                            jax.ShapeDtypeStruct((1, SG_W), F32), jax.ShapeDtypeStruct((1, SG_W), F32)],
                 scratch=[pltpu.VMEM((BLK, SG_W), F32)])[0]


def _gate_merge(oatt, z, watt, wsg, proj, name, comm=()):
    S = oatt.shape[0]
    nb, _, Db = watt.shape
    D = nb * Db
    tm = _tile(S, 512)
    ga, gs = GA_OFF // Db, (GA_OFF + D) // Db

    def body(oa_ref, z_ref, wa_ref, ws_ref, ga_ref, gs_ref, ya_ref, ys_ref, mg_ref):
        ya = _nn(oa_ref[...], wa_ref[0])
        ys = _nn(z_ref[...], ws_ref[0])
        ya_ref[...] = ya.astype(BF)
        ys_ref[...] = ys.astype(BF)
        mg_ref[...] = (jax.nn.sigmoid(ga_ref[...].astype(F32)) * ya + jax.nn.sigmoid(gs_ref[...].astype(F32)) * ys).astype(BF)

    out = pl.BlockSpec((tm, Db), lambda j, i: (i, j))
    shp = jax.ShapeDtypeStruct((S, D), BF)
    return _call(body, grid=(nb, S // tm), name=name, args=[oatt, z, watt, wsg, proj, proj], comm=comm,
                 in_specs=[pl.BlockSpec((tm, GROUP_W), lambda j, i: (i, 0)), pl.BlockSpec((tm, SG_W), lambda j, i: (i, 0)),
                           pl.BlockSpec((1, GROUP_W, Db), lambda j, i: (j, 0, 0)), pl.BlockSpec((1, SG_W, Db), lambda j, i: (j, 0, 0)),
                           pl.BlockSpec((tm, Db), lambda j, i: (i, ga + j)), pl.BlockSpec((tm, Db), lambda j, i: (i, gs + j))],
                 out_specs=[out, out, out], out_shape=[shp, shp, shp])


def _mix_out(merged, wout, x, gn, name):
    S, D = x.shape
    tm = _tile(S, 256)

    def body(m_ref, w_ref, x_ref, gn_ref, xo_ref, hn_ref):
        xo = x_ref[...] + _nn(m_ref[...], w_ref[...])
        r = lax.rsqrt(jnp.mean(xo * xo, axis=-1, keepdims=True) + NORM_EPS)
        xo_ref[...] = xo
        hn_ref[...] = (xo * r * gn_ref[...]).astype(BF)

    row = pl.BlockSpec((tm, D), lambda i: (i, 0))
    return _call(body, grid=(S // tm,), name=name, args=[merged, wout, x, gn],
                 in_specs=[row, pl.BlockSpec((D, D), lambda i: (0, 0)), row, pl.BlockSpec((1, D), lambda i: (0, 0))],
                 out_specs=[row, row], out_shape=[jax.ShapeDtypeStruct((S, D), F32), jax.ShapeDtypeStruct((S, D), BF)])[0]


def _mix_bwd_gate(dmix, wout, ya, ys, proj, name):
    S, D = dmix.shape
    tm, tn = _tile(S, 512), 512
    ga, gs = GA_OFF // tn, (GA_OFF + D) // tn

    def body(dm_ref, w_ref, ya_ref, ys_ref, ga_ref, gs_ref, dya_ref, dys_ref, dga_ref, dgs_ref):
        dm = _nt(dm_ref[...], w_ref[...])
        sa = jax.nn.sigmoid(ga_ref[...].astype(F32))
        ss = jax.nn.sigmoid(gs_ref[...].astype(F32))
        dya_ref[...] = (dm * sa).astype(BF)
        dys_ref[...] = (dm * ss).astype(BF)
        dga_ref[...] = (dm * ya_ref[...].astype(F32) * sa * (1.0 - sa)).astype(BF)
        dgs_ref[...] = (dm * ys_ref[...].astype(F32) * ss * (1.0 - ss)).astype(BF)

    out = pl.BlockSpec((tm, tn), lambda i, n: (i, n))
    shp = jax.ShapeDtypeStruct((S, D), BF)
    return _call(body, grid=(S // tm, D // tn), name=name, args=[dmix, wout, ya, ys, proj, proj],
                 in_specs=[pl.BlockSpec((tm, D), lambda i, n: (i, 0)), pl.BlockSpec((tn, D), lambda i, n: (n, 0)), out, out,
                           pl.BlockSpec((tm, tn), lambda i, n: (i, ga + n)), pl.BlockSpec((tm, tn), lambda i, n: (i, gs + n))],
                 out_specs=[out] * 4, out_shape=[shp] * 4)[0]


def _att_sg_dout(dya, dys, watt, wsg, oatt, name):
    S, D = dya.shape
    nb, _, Db = watt.shape
    tm = _tile(S, 512)

    def body(dya_ref, dys_ref, wa_ref, ws_ref, oa_ref, do_ref, dz_ref, dvec_ref, acca, accs):
        j = pl.program_id(1)

        @pl.when(j == 0)
        def _():
            acca[...] = jnp.zeros_like(acca)
            accs[...] = jnp.zeros_like(accs)

        acca[...] += _nt(dya_ref[...], wa_ref[0])
        accs[...] += _nt(dys_ref[...], ws_ref[0])

        @pl.when(j == nb - 1)
        def _():
            dov = acca[...]
            do_ref[...] = dov.astype(BF)
            dz_ref[...] = accs[...].astype(BF)
            prod = dov * oa_ref[...].astype(F32)
            for hh in range(HEADS_PER_GROUP):
                cols = slice(hh * HEAD_DIM, (hh + 1) * HEAD_DIM)
                dvec_ref[:, cols] = jnp.broadcast_to(jnp.sum(prod[:, cols], axis=-1, keepdims=True), (tm, HEAD_DIM))

    blk = pl.BlockSpec((tm, Db), lambda i, j: (i, j))
    att = pl.BlockSpec((tm, GROUP_W), lambda i, j: (i, 0))
    return _call(body, grid=(S // tm, nb), name=name, args=[dya, dys, watt, wsg, oatt],
                 in_specs=[blk, blk, pl.BlockSpec((1, GROUP_W, Db), lambda i, j: (j, 0, 0)), pl.BlockSpec((1, SG_W, Db), lambda i, j: (j, 0, 0)), att],
                 out_specs=[att, pl.BlockSpec((tm, SG_W), lambda i, j: (i, 0)), att],
                 out_shape=[jax.ShapeDtypeStruct((S, GROUP_W), BF), jax.ShapeDtypeStruct((S, SG_W), BF), jax.ShapeDtypeStruct((S, GROUP_W), F32)],
                 scratch=[pltpu.VMEM((tm, GROUP_W), F32), pltpu.VMEM((tm, SG_W), F32)])[0]


def _small_allreduce(pack, name):
    R = pack.shape[0]

    def body(p_ref, o_ref, gath, send, recv):
        x, y, c = _place()
        me = 4 * x + 2 * y + c
        gath[me] = p_ref[...]
        copies = []
        for r in range(1, N_DEV):
            px, py, pc = _flip(x, r & 4), _flip(y, r & 2), _flip(c, r & 1)
            peer = 4 * px + 2 * py + pc
            mk = lambda dst: pltpu.make_async_remote_copy(src_ref=p_ref, dst_ref=dst, send_sem=send.at[r - 1], recv_sem=recv.at[r - 1],
                                                          device_id=(px, py, pc), device_id_type=MESH)
            snd = mk(gath.at[me])
            snd.start()
            copies.append((snd, mk(gath.at[peer])))
        for snd, rcv in copies:
            rcv.wait_recv()
            snd.wait_send()
        acc = gath[0]
        for s in range(1, N_DEV):
            acc = acc + gath[s]
        o_ref[...] = acc

    vm = pl.BlockSpec(memory_space=pltpu.VMEM)
    return pl.pallas_call(
        body, name=name, in_specs=[vm], out_specs=vm, out_shape=jax.ShapeDtypeStruct(pack.shape, F32),
        scratch_shapes=[pltpu.VMEM((N_DEV, R, 128), F32), pltpu.SemaphoreType.DMA((7,)), pltpu.SemaphoreType.DMA((7,))],
        compiler_params=pltpu.CompilerParams(vmem_limit_bytes=VMEM_LIMIT),
    )(pack)


def _row_tile(R, C):
    tr = R
    while tr * C > 262144 and tr % 32 == 0:
        tr //= 2
    return tr


def _pair_add(parts, other, name):
    _, R, C = parts.shape
    tr = _row_tile(R, C)

    def body(p_ref, o_ref, s_ref):
        c = lax.axis_index("c")
        s_ref[0] = (p_ref[0, c].astype(F32) + o_ref[0].astype(F32)).astype(BF)

    return _call(body, grid=(N_CHIP, R // tr), name=name, args=[parts.reshape(N_CHIP, 2, R, C), other],
                 in_specs=[pl.BlockSpec((1, 2, tr, C), lambda q, i: (q, 0, i, 0)), pl.BlockSpec((1, tr, C), lambda q, i: (q, i, 0))],
                 out_specs=[pl.BlockSpec((1, tr, C), lambda q, i: (q, i, 0))], out_shape=[jax.ShapeDtypeStruct((N_CHIP, R, C), BF)])[0][0]


def _adamw(parts, w, m, v, name):
    ns, R, C = parts.shape
    tr = _row_tile(R, C)
    c1 = 1.0 - ADAM_B1 ** ADAM_STEP
    c2 = 1.0 - ADAM_B2 ** ADAM_STEP

    def body(p_ref, w_ref, m_ref, v_ref, g_ref, d_ref, nm_ref, nv_ref):
        g = p_ref[0].astype(F32)
        for s in range(1, ns):
            g = g + p_ref[s].astype(F32)
        mn = ADAM_B1 * m_ref[...] + (1.0 - ADAM_B1) * g
        vn = ADAM_B2 * v_ref[...] + (1.0 - ADAM_B2) * (g * g)
        g_ref[...] = g
        nm_ref[...] = mn
        nv_ref[...] = vn
        d_ref[...] = -ADAM_LR * ((mn / c1) / (jnp.sqrt(vn / c2) + ADAM_EPS) + ADAM_WD * w_ref[...])

    row = pl.BlockSpec((tr, C), lambda i: (i, 0))
    shp = jax.ShapeDtypeStruct((R, C), F32)
    return _call(body, grid=(R // tr,), name=name, args=[parts, w, m, v],
                 in_specs=[pl.BlockSpec((ns, tr, C), lambda i: (0, i, 0)), row, row, row], out_specs=[row] * 4, out_shape=[shp] * 4)[0]


def _pad_rows(a, rows):
    return jnp.pad(a, ((0, rows - a.shape[0]), (0, 0)))


def kernel(x, ffn1_norm, ffn1_w_gate, ffn1_w_up, ffn1_w_down, mix_norm, w_in, sg_ln_g, sg_ln_b, sg_w, sg_b, w_att_out, w_sg_out, w_out, ffn2_norm, ffn2_w_gate, ffn2_w_up, ffn2_w_down, final_norm, loss_target, m_ffn1_norm, m_ffn1_w_gate, m_ffn1_w_up, m_ffn1_w_down, m_mix_norm, m_w_in, m_sg_ln_g, m_sg_ln_b, m_sg_w, m_sg_b, m_w_att_out, m_w_sg_out, m_w_out, m_ffn2_norm, m_ffn2_w_gate, m_ffn2_w_up, m_ffn2_w_down, m_final_norm, v_ffn1_norm, v_ffn1_w_gate, v_ffn1_w_up, v_ffn1_w_down, v_mix_norm, v_w_in, v_sg_ln_g, v_sg_ln_b, v_sg_w, v_sg_b, v_w_att_out, v_w_sg_out, v_w_out, v_ffn2_norm, v_ffn2_w_gate, v_ffn2_w_up, v_ffn2_w_down, v_final_norm):
    S, D = x.shape[1], x.shape[2]
    Pb = w_in.shape[2]
    P = N_DEV * Pb
    assert P == GA_OFF + 2 * D and D % (N_DEV * 128) == 0 and S % (BLK * DILATIONS[-1]) == 0
    xs, tgt = x[0], loss_target[0]

    sharded = dict(ffn1_w_gate=ffn1_w_gate, ffn1_w_up=ffn1_w_up, ffn1_w_down=ffn1_w_down, w_in=w_in, w_att_out=w_att_out,
                   w_sg_out=w_sg_out, w_out=w_out, ffn2_w_gate=ffn2_w_gate, ffn2_w_up=ffn2_w_up, ffn2_w_down=ffn2_w_down)
    sb = {n: w[0].astype(BF) for n, w in sharded.items()}

    wg1, wu1 = _comm_only(_Gather([sb["ffn1_w_gate"], sb["ffn1_w_up"]]), "gather_ffn1")
    h1 = _rms_fwd(xs, ffn1_norm, "rms1")
    (g1, u1, a1), ((wd1, watt, wsg, wout8),) = _ffn_up(
        h1, wg1, wu1, "ffn1_up", comm=[_Gather([sb["ffn1_w_down"], sb["w_att_out"], sb["w_sg_out"], sb["w_out"]])])
    (x1, h2), ((win8,),) = _ffn_down_norm(a1, wd1, xs, mix_norm, "ffn1_down", comm=[_Gather([sb["w_in"]])])
    win = win8.transpose(1, 0, 2).reshape(D, P)
    wout = wout8.reshape(D, D)
    (proj,), ((wg2, wu2),) = _mm_nn(h2, win, _tile(S, 1024), 512, "proj", comm=[_Gather([sb["ffn2_w_gate"], sb["ffn2_w_up"]])])
    tabs = _rope_tables(S)
    os, lses = [], []
    for gi, d in enumerate(DILATIONS):
        o, l = _att_fwd(proj, tabs, gi, d, f"att_fwd{gi}")
        os.append(o)
        lses.append(l)
    oatt, lse = _att_combine(os, lses, "att_combine")
    sgw = sg_w[0]
    sgbT = jnp.pad(sg_b[0].T, ((0, 0), (0, BLK - SG_GROUPS)))
    z = _sg_fwd(proj, sgw, sgbT, sg_ln_g, sg_ln_b, "sg_fwd")
    (ya, ys, merged), ((wd2,),) = _gate_merge(oatt, z, watt, wsg, proj, "gate_merge", comm=[_Gather([sb["ffn2_w_down"]])])
    x2, h3 = _mix_out(merged, wout, x1, ffn2_norm, "mix_out")
    (g3, u3, a3), _ = _ffn_up(h3, wg2, wu2, "ffn2_up")
    dx3, dyb3, d_final, loss_part = _ffn_down_loss(a3, wd2, x2, final_norm.reshape(1, D), tgt, "ffn2_down_loss")

    Fb = wg2.shape[2]
    Db = watt.shape[2]
    (dg3, du3), _ = _ffn_bwd_act(dyb3, wd2, g3, u3, "ffn2_bwd_act")
    (dwd2,), _ = _ffn_dwd(a3, dyb3, "ffn2_dwd")
    (dwg2, dwu2), _ = _ffn_dwgu(h3, dg3, du3, "ffn2_dwgu")
    ffn2_parts = [dwd2, dwg2, dwu2]
    (dx2, dmixb, d_ffn2n), (ffn2_other,) = _dh_rms_bwd([(dg3, wg2), (du3, wu2)], True, Fb, x2, ffn2_norm, dx3, 1.0, "ffn2_dh",
                                                     comm=[_Swap(ffn2_parts)])
    ffn2_sums = [_pair_add(p, o, f"pair_ffn2_{i}") for i, (p, o) in enumerate(zip(ffn2_parts, ffn2_other))]

    dya, dys, dga, dgs = _mix_bwd_gate(dmixb, wout, ya, ys, proj, "mix_bwd_gate")
    (dwout,), _ = _mm_tn(merged, dmixb, _tile(D, 1024), _tile(D, 1024), _tile(S, 1024), False, "dw_out")
    do, dz, dvec = _att_sg_dout(dya, dys, watt, wsg, oatt, "att_sg_dout")
    (dwatt,), _ = _mm_tn(oatt, dya, GROUP_W, Db, _tile(S, 1024), True, "dw_att")
    (dwsg,), _ = _mm_tn(z, dys, SG_W, Db, _tile(S, 1024), True, "dw_sg")
    mix_parts = [dwout.reshape(N_DEV, D // N_DEV, D), dwatt, dwsg]
    du, dvs, d_sgw, d_sgbT, d_lng, d_lnb = _sg_bwd(proj, dz, sgw, sgbT, sg_ln_g, sg_ln_b, "sg_bwd")
    dqs, dks, dvs_att = [], [], []
    for gi, d in enumerate(DILATIONS):
        dq, dk, dv = _att_bwd(proj, tabs, do, lse, dvec, gi, d, f"att_bwd{gi}")
        dqs.append(dq)
        dks.append(dk)
        dvs_att.append(dv)
    dproj = jnp.concatenate(dqs + dks + dvs_att + [du, dvs, dga, dgs], axis=1)
    (dx1, dyb1, d_mixn), (ffn2_got, mix_other) = _dh_rms_bwd([(dproj, win)], False, 512, x1, mix_norm, dx2, 0.5, "proj_dh",
                                                           comm=[_Ici(ffn2_sums), _Swap(mix_parts)])
    mix_sums = [_pair_add(p, o, f"pair_mix_{i}") for i, (p, o) in enumerate(zip(mix_parts, mix_other))]
    (dwin,), (mix_got,) = _mm_tn(h2, dproj, D, 512, _tile(S, 1024), False, "dw_in", comm=[_Ici(mix_sums)])
    dwin = dwin.reshape(D, N_DEV, Pb).transpose(1, 0, 2)

    (dg1, du1), ((win_other,),) = _ffn_bwd_act(dyb1, wd1, g1, u1, "ffn1_bwd_act", comm=[_Swap([dwin])])
    win_sum = _pair_add(dwin, win_other, "pair_win")
    (dwg1, dwu1), ((win_got,),) = _ffn_dwgu(h1, dg1, du1, "ffn1_dwgu", comm=[_Ici([win_sum])])
    (dwd1,), (gu_other,) = _ffn_dwd(a1, dyb1, "ffn1_dwd", comm=[_Swap([dwg1, dwu1])])
    gu_sums = [_pair_add(p, o, f"pair_gu1_{i}") for i, (p, o) in enumerate(zip([dwg1, dwu1], gu_other))]
    (dx0, _, d_ffn1n), (gu_got, (wd1_other,)) = _dh_rms_bwd([(dg1, wg1), (du1, wu1)], True, Fb, xs, ffn1_norm, dx1, 1.0, "ffn1_dh",
                                                           comm=[_Ici(gu_sums), _Swap([dwd1])])
    wd1_sum = _pair_add(dwd1, wd1_other, "pair_wd1")
    (wd1_got,) = _comm_only(_Ici([wd1_sum]), "exchange_wd1")

    got = dict(ffn2_w_down=ffn2_got[0], ffn2_w_gate=ffn2_got[1], ffn2_w_up=ffn2_got[2], w_out=mix_got[0], w_att_out=mix_got[1],
               w_sg_out=mix_got[2], w_in=win_got, ffn1_w_gate=gu_got[0], ffn1_w_up=gu_got[1], ffn1_w_down=wd1_got)
    moments = dict(ffn1_w_gate=(m_ffn1_w_gate, v_ffn1_w_gate), ffn1_w_up=(m_ffn1_w_up, v_ffn1_w_up),
                   ffn1_w_down=(m_ffn1_w_down, v_ffn1_w_down), w_in=(m_w_in, v_w_in), w_att_out=(m_w_att_out, v_w_att_out),
                   w_sg_out=(m_w_sg_out, v_w_sg_out), w_out=(m_w_out, v_w_out), ffn2_w_gate=(m_ffn2_w_gate, v_ffn2_w_gate),
                   ffn2_w_up=(m_ffn2_w_up, v_ffn2_w_up), ffn2_w_down=(m_ffn2_w_down, v_ffn2_w_down))
    res = {}
    for n in sharded:
        mm, vv = moments[n]
        outs = _adamw(got[n], sharded[n][0], mm[0], vv[0], "adamw_" + n)
        res[n] = [o[None] for o in outs]

    rows = lambda a: a.reshape(-1, 128)
    small = [("sg_w", rows(d_sgw), sg_w, m_sg_w, v_sg_w), ("ffn1_norm", rows(d_ffn1n), ffn1_norm, m_ffn1_norm, v_ffn1_norm),
             ("mix_norm", rows(d_mixn), mix_norm, m_mix_norm, v_mix_norm), ("ffn2_norm", rows(d_ffn2n), ffn2_norm, m_ffn2_norm, v_ffn2_norm),
             ("final_norm", rows(d_final), final_norm, m_final_norm, v_final_norm), ("sg_ln_g", rows(d_lng), sg_ln_g, m_sg_ln_g, v_sg_ln_g),
             ("sg_ln_b", rows(d_lnb), sg_ln_b, m_sg_ln_b, v_sg_ln_b), ("sg_b", d_sgbT[:, :SG_GROUPS].T, sg_b, m_sg_b, v_sg_b)]
    pad8 = lambda a: _pad_rows(a, -(-a.shape[0] // 8) * 8)
    gpack = jnp.concatenate([pad8(g) for _, g, _, _, _ in small] + [pad8(loss_part)], axis=0)
    gsum = _small_allreduce(gpack, "allreduce_small")
    zero8 = jnp.zeros((8, 128), F32)
    wpack = jnp.concatenate([pad8(rows(w)) for _, _, w, _, _ in small] + [zero8], axis=0)
    mpack = jnp.concatenate([pad8(rows(m)) for _, _, _, m, _ in small] + [zero8], axis=0)
    vpack = jnp.concatenate([pad8(rows(v)) for _, _, _, _, v in small] + [zero8], axis=0)
    packs = _adamw(gsum[None], wpack, mpack, vpack, "adamw_small")
    off = 0
    for n, g, w, _, _ in small:
        r = g.shape[0]
        res[n] = [p[off:off + r].reshape(w.shape) for p in packs]
        off += -(-r // 8) * 8
    loss = gsum[off, 0]

    order = ["ffn1_norm", "ffn1_w_gate", "ffn1_w_up", "ffn1_w_down", "mix_norm", "w_in", "sg_ln_g", "sg_ln_b", "sg_w", "sg_b",
             "w_att_out", "w_sg_out", "w_out", "ffn2_norm", "ffn2_w_gate", "ffn2_w_up", "ffn2_w_down", "final_norm"]
    return (loss, dx0[None], *[res[n][0] for n in order], *[res[n][1] for n in order], *[res[n][2] for n in order],
            *[res[n][3] for n in order])
```

```python
import math

import jax
import jax.numpy as jnp
from jax import lax
from jax.experimental import pallas as pl
from jax.experimental.pallas import tpu as pltpu

BF = jnp.bfloat16
F32 = jnp.float32
MESH = pl.DeviceIdType.MESH
N_DEV = 8
N_CHIP = 4

HEAD_DIM = 128
HEADS_PER_GROUP = 4
GROUP_W = HEADS_PER_GROUP * HEAD_DIM
DILATIONS = (1, 4, 16)
ATT_W = len(DILATIONS) * GROUP_W
SG_W = 1536
SG_GROUPS = 12
BLK = 128
ROPE_DIM = 32
ROPE_THETA = 500000.0
NORM_EPS = 1e-6
LN_EPS = 1e-5
Q_OFF, K_OFF, V_OFF, U_OFF, VS_OFF, GA_OFF = 0, ATT_W, 2 * ATT_W, 3 * ATT_W, 3 * ATT_W + SG_W, 3 * ATT_W + 2 * SG_W

ADAM_LR, ADAM_B1, ADAM_B2, ADAM_EPS, ADAM_WD, ADAM_STEP = 0.001, 0.9, 0.999, 1e-08, 0.01, 10

VMEM_LIMIT = 56 * 1024 * 1024
NEG = -1e30
ANY = pl.BlockSpec(memory_space=pl.ANY)
EPI_ROWS = 128
R_U, R_VS, R_GA = 0, SG_W, 2 * SG_W


def _once(shape, index_map):
    return pl.BlockSpec(shape, index_map, pipeline_mode=pl.Buffered(1))


def _tile(n, pref):
    t = min(n, pref)
    while n % t:
        t //= 2
    return t


def _nt(a, b):
    return lax.dot_general(a, b, (((1,), (1,)), ((), ())), preferred_element_type=F32)


def _tn(a, b):
    return lax.dot_general(a, b, (((0,), (0,)), ((), ())), preferred_element_type=F32)


def _nn(a, b):
    return jnp.dot(a, b, preferred_element_type=F32)


def _gelu(x):
    return 0.5 * x * (1.0 + lax.erf(x * (2.0 ** -0.5)))


def _gelu_grad(x):
    return 0.5 * (1.0 + lax.erf(x * (2.0 ** -0.5))) + x * jnp.exp(-0.5 * x * x) * (1.0 / math.sqrt(2.0 * math.pi))


def _place():
    x, y, c = lax.axis_index("x"), lax.axis_index("y"), lax.axis_index("c")
    return x, y, c


def _flip(v, bit):
    return 1 - v if bit else v


class _Gather:
    def __init__(self, shards):
        self.arrays = list(shards)
        nw = len(shards)
        self.out_shape = [jax.ShapeDtypeStruct((N_DEV,) + s.shape, s.dtype) for s in shards]
        self.scratch = [pltpu.SemaphoreType.DMA((nw, 7)), pltpu.SemaphoreType.DMA((nw, 7)), pltpu.SemaphoreType.DMA((nw,))]

    def _parts(self, ins, outs, sems):
        x, y, c = _place()
        send, recv, loc = sems
        chips = [(1 - x, y), (x, 1 - y), (1 - x, 1 - y)]

        def copy(k, s, block, to, src=None):
            dst = outs[k].at[4 * block[0] + 2 * block[1] + block[2]]
            return pltpu.make_async_remote_copy(src_ref=dst if src is None else src, dst_ref=dst, send_sem=send.at[k, s],
                                                recv_sem=recv.at[k, s], device_id=to, device_id_type=MESH)

        def first(k):
            return [copy(k, 0, (x, y, c), (x, y, 1 - c), src=ins[k])] + [
                copy(k, 1 + j, (x, y, c), (*chip, c), src=ins[k]) for j, chip in enumerate(chips)]

        def local(k):
            return pltpu.make_async_copy(ins[k], outs[k].at[4 * x + 2 * y + c], loc.at[k])

        return x, y, c, chips, copy, first, local

    def start(self, ins, outs, sems):
        _, _, _, _, _, first, local = self._parts(ins, outs, sems)
        for k in range(len(ins)):
            local(k).start()
            for cp in first(k):
                cp.start()

    def mid(self, ins, outs, sems):
        x, y, c, chips, copy, _, _ = self._parts(ins, outs, sems)
        for k in range(len(ins)):
            for j, chip in enumerate(chips):
                copy(k, 1 + j, (*chip, c), (x, y, c)).wait_recv()
                copy(k, 4 + j, (*chip, c), (x, y, 1 - c)).start()

    def finish(self, ins, outs, sems):
        x, y, c, chips, copy, first, local = self._parts(ins, outs, sems)
        for k in range(len(ins)):
            copy(k, 0, (x, y, 1 - c), (x, y, c)).wait_recv()
            for j, chip in enumerate(chips):
                copy(k, 4 + j, (*chip, 1 - c), (x, y, c)).wait_recv()
        for k in range(len(ins)):
            for cp in first(k):
                cp.wait_send()
            for j, chip in enumerate(chips):
                copy(k, 4 + j, (*chip, c), (x, y, 1 - c)).wait_send()
            local(k).wait()


class _Swap:
    def __init__(self, parts):
        self.arrays = list(parts)
        nw = len(parts)
        self.out_shape = [jax.ShapeDtypeStruct((N_CHIP,) + p.shape[1:], p.dtype) for p in parts]
        self.scratch = [pltpu.SemaphoreType.DMA((nw, N_CHIP)), pltpu.SemaphoreType.DMA((nw, N_CHIP))]

    def _copy(self, ins, outs, sems, k, q):
        x, y, c = _place()
        return pltpu.make_async_remote_copy(src_ref=ins[k].at[2 * q + 1 - c], dst_ref=outs[k].at[q], send_sem=sems[0].at[k, q],
                                            recv_sem=sems[1].at[k, q], device_id=(x, y, 1 - c), device_id_type=MESH)

    def start(self, ins, outs, sems):
        for k in range(len(ins)):
            for q in range(N_CHIP):
                self._copy(ins, outs, sems, k, q).start()

    def mid(self, ins, outs, sems):
        pass

    def finish(self, ins, outs, sems):
        for k in range(len(ins)):
            for q in range(N_CHIP):
                self._copy(ins, outs, sems, k, q).wait()


class _Ici:
    def __init__(self, sums):
        self.arrays = list(sums)
        nw = len(sums)
        self.out_shape = [jax.ShapeDtypeStruct(s.shape, s.dtype) for s in sums]
        self.scratch = [pltpu.SemaphoreType.DMA((nw, 3)), pltpu.SemaphoreType.DMA((nw, 3)), pltpu.SemaphoreType.DMA((nw,))]

    def _copies(self, ins, outs, sems, k):
        x, y, c = _place()
        myq = 2 * x + y
        out = []
        for r in range(1, N_CHIP):
            px, py = _flip(x, r & 2), _flip(y, r & 1)
            pq = 2 * px + py
            mk = lambda dst: pltpu.make_async_remote_copy(src_ref=ins[k].at[pq], dst_ref=dst, send_sem=sems[0].at[k, r - 1],
                                                          recv_sem=sems[1].at[k, r - 1], device_id=(px, py, c), device_id_type=MESH)
            out.append((mk(outs[k].at[myq]), mk(outs[k].at[pq])))
        return out, pltpu.make_async_copy(ins[k].at[myq], outs[k].at[myq], sems[2].at[k])

    def start(self, ins, outs, sems):
        for k in range(len(ins)):
            remote, local = self._copies(ins, outs, sems, k)
            local.start()
            for snd, _ in remote:
                snd.start()

    def mid(self, ins, outs, sems):
        pass

    def finish(self, ins, outs, sems):
        for k in range(len(ins)):
            remote, local = self._copies(ins, outs, sems, k)
            for snd, rcv in remote:
                rcv.wait_recv()
                snd.wait_send()
            local.wait()


def _call(body, *, grid, in_specs, out_specs, out_shape, name, args, scratch=(), comm=()):
    comm = list(comm)
    n_in, n_out, n_scr = len(in_specs), len(out_specs), len(scratch)
    total = math.prod(grid) if grid else 1

    def wrapped(*refs):
        p = n_in
        cin = []
        for cm in comm:
            cin.append(refs[p:p + len(cm.arrays)])
            p += len(cm.arrays)
        own_out = refs[p:p + n_out]
        p += n_out
        cout = []
        for cm in comm:
            cout.append(refs[p:p + len(cm.arrays)])
            p += len(cm.arrays)
        own_scr = refs[p:p + n_scr]
        p += n_scr
        csem = []
        for cm in comm:
            csem.append(refs[p:p + len(cm.scratch)])
            p += len(cm.scratch)
        step = 0
        for axis, g in enumerate(grid):
            step = step * g + pl.program_id(axis)

        def at(when, what):
            if total == 1:
                what()
            else:
                pl.when(step == when)(what)

        def starts():
            for cm, i, o, s in zip(comm, cin, cout, csem):
                cm.start(i, o, s)

        def mids():
            for cm, i, o, s in zip(comm, cin, cout, csem):
                cm.mid(i, o, s)

        def finishes():
            for cm, i, o, s in zip(comm, cin, cout, csem):
                cm.finish(i, o, s)

        if comm:
            at(0, starts)
        if body is not None:
            body(*refs[:n_in], *own_out, *own_scr)
        if comm:
            at(total - 1, mids)
            at(total - 1, finishes)

    kw = dict(grid=tuple(grid)) if grid else {}
    outs = pl.pallas_call(
        wrapped, name=name, **kw,
        in_specs=list(in_specs) + [ANY for cm in comm for _ in cm.arrays],
        out_specs=list(out_specs) + [ANY for cm in comm for _ in cm.arrays],
        out_shape=list(out_shape) + [s for cm in comm for s in cm.out_shape],
        scratch_shapes=list(scratch) + [s for cm in comm for s in cm.scratch],
        compiler_params=pltpu.CompilerParams(dimension_semantics=("arbitrary",) * len(grid), vmem_limit_bytes=VMEM_LIMIT),
    )(*args, *[a for cm in comm for a in cm.arrays])
    own, p, per = list(outs[:n_out]), n_out, []
    for cm in comm:
        per.append(list(outs[p:p + len(cm.arrays)]))
        p += len(cm.arrays)
    return own, per


def _comm_only(cm, name):
    return _call(None, grid=(), in_specs=[], out_specs=[], out_shape=[], name=name, args=[], comm=[cm])[1][0]


def _rms_fwd(x, g, name):
    S, D = x.shape
    tm = _tile(S, 512)

    def body(x_ref, g_ref, o_ref):
        xv = x_ref[...]
        r = lax.rsqrt(jnp.mean(xv * xv, axis=-1, keepdims=True) + NORM_EPS)
        o_ref[...] = (xv * r * g_ref[...]).astype(BF)

    return _call(body, grid=(S // tm,), name=name, args=[x, g],
                 in_specs=[pl.BlockSpec((tm, D), lambda i: (i, 0)), pl.BlockSpec((1, D), lambda i: (0, 0))],
                 out_specs=[pl.BlockSpec((tm, D), lambda i: (i, 0))], out_shape=[jax.ShapeDtypeStruct((S, D), BF)])[0][0]


def _ffn_up(h, wg, wu, name, comm=()):
    S, D = h.shape
    nb, _, Fb = wg.shape
    tm = _tile(S, 512)

    def body(h_ref, wg_ref, wu_ref, g_ref, u_ref, a_ref):
        hv = h_ref[...]
        g = _nn(hv, wg_ref[0])
        u = _nn(hv, wu_ref[0])
        g_ref[0] = g.astype(BF)
        u_ref[0] = u.astype(BF)
        a_ref[0] = (g * jax.nn.sigmoid(g) * u).astype(BF)

    act = pl.BlockSpec((1, tm, Fb), lambda j, i: (j, i, 0))
    w = pl.BlockSpec((1, D, Fb), lambda j, i: (j, 0, 0))
    shp = jax.ShapeDtypeStruct((nb, S, Fb), BF)
    return _call(body, grid=(nb, S // tm), name=name, args=[h, wg, wu], comm=comm,
                 in_specs=[pl.BlockSpec((tm, D), lambda j, i: (i, 0)), w, w], out_specs=[act, act, act], out_shape=[shp, shp, shp])


def _ffn_down_norm(a, wd, x, gn, name, comm=()):
    nb, S, Fb = a.shape
    D = wd.shape[2]
    tm = _tile(S, 512)

    def body(a_ref, wd_ref, x_ref, gn_ref, xo_ref, hn_ref, acc_ref):
        j = pl.program_id(1)

        @pl.when(j == 0)
        def _():
            acc_ref[...] = jnp.zeros_like(acc_ref)

        acc_ref[...] += _nn(a_ref[0], wd_ref[0])

        @pl.when(j == nb - 1)
        def _():
            def chunk(t, carry):
                rows = pl.ds(pl.multiple_of(t * EPI_ROWS, EPI_ROWS), EPI_ROWS)
                xo = x_ref[rows, :] + 0.5 * acc_ref[rows, :]
                r = lax.rsqrt(jnp.mean(xo * xo, axis=-1, keepdims=True) + NORM_EPS)
                xo_ref[rows, :] = xo
                hn_ref[rows, :] = (xo * r * gn_ref[...]).astype(BF)
                return carry

            lax.fori_loop(0, tm // EPI_ROWS, chunk, 0)

    row = pl.BlockSpec((tm, D), lambda i, j: (i, 0))
    return _call(body, grid=(S // tm, nb), name=name, args=[a, wd, x, gn], comm=comm,
                 in_specs=[pl.BlockSpec((1, tm, Fb), lambda i, j: (j, i, 0)), pl.BlockSpec((1, Fb, D), lambda i, j: (j, 0, 0)),
                           _once((tm, D), lambda i, j: (i, 0)), pl.BlockSpec((1, D), lambda i, j: (0, 0))],
                 out_specs=[row, row], out_shape=[jax.ShapeDtypeStruct((S, D), F32), jax.ShapeDtypeStruct((S, D), BF)],
                 scratch=[pltpu.VMEM((tm, D), F32)])


def _ffn_down_loss(a, wd, x, gf, tgt, name):
    nb, S, Fb = a.shape
    D = wd.shape[2]
    tm = _tile(S, 512)

    def body(a_ref, wd_ref, x_ref, gf_ref, t_ref, dx_ref, dxb_ref, dgf_ref, loss_ref, acc_ref):
        i, j = pl.program_id(0), pl.program_id(1)

        @pl.when(j == 0)
        def _():
            acc_ref[...] = jnp.zeros_like(acc_ref)

        acc_ref[...] += _nn(a_ref[0], wd_ref[0])

        @pl.when((j == nb - 1) & (i == 0))
        def _():
            dgf_ref[...] = jnp.zeros_like(dgf_ref)
            loss_ref[...] = jnp.zeros_like(loss_ref)

        @pl.when(j == nb - 1)
        def _():
            def chunk(t, carry):
                rows = pl.ds(pl.multiple_of(t * EPI_ROWS, EPI_ROWS), EPI_ROWS)
                xo = x_ref[rows, :] + 0.5 * acc_ref[rows, :]
                r = lax.rsqrt(jnp.mean(xo * xo, axis=-1, keepdims=True) + NORM_EPS)
                xh = xo * r
                gf = gf_ref[...]
                e = xh * gf - t_ref[rows, :]
                loss_ref[...] += jnp.sum(jnp.mean(e * e, axis=-1, keepdims=True), axis=0, keepdims=True) * 0.5
                dy = e * (1.0 / D)
                dgf_ref[...] += jnp.sum(dy * xh, axis=0, keepdims=True)
                dxh = dy * gf
                dx = r * (dxh - xh * jnp.mean(dxh * xh, axis=-1, keepdims=True))
                dx_ref[rows, :] = dx
                dxb_ref[rows, :] = (0.5 * dx).astype(BF)
                return carry

            lax.fori_loop(0, tm // EPI_ROWS, chunk, 0)

    row = pl.BlockSpec((tm, D), lambda i, j: (i, 0))
    once = _once((tm, D), lambda i, j: (i, 0))
    vec = pl.BlockSpec((1, D), lambda i, j: (0, 0))
    return _call(body, grid=(S // tm, nb), name=name, args=[a, wd, x, gf, tgt],
                 in_specs=[pl.BlockSpec((1, tm, Fb), lambda i, j: (j, i, 0)), pl.BlockSpec((1, Fb, D), lambda i, j: (j, 0, 0)), once, vec, once],
                 out_specs=[row, row, vec, pl.BlockSpec((1, 128), lambda i, j: (0, 0))],
                 out_shape=[jax.ShapeDtypeStruct((S, D), F32), jax.ShapeDtypeStruct((S, D), BF), jax.ShapeDtypeStruct((1, D), F32),
                            jax.ShapeDtypeStruct((1, 128), F32)],
                 scratch=[pltpu.VMEM((tm, D), F32)])[0]


def _ffn_bwd_act(dyb, wd, g, u, name, comm=()):
    S, D = dyb.shape
    nb, Fb, _ = wd.shape
    tm = _tile(S, 512)

    def body(dy_ref, wd_ref, g_ref, u_ref, dg_ref, du_ref):
        da = _nt(dy_ref[...], wd_ref[0])
        gv = g_ref[0].astype(F32)
        uv = u_ref[0].astype(F32)
        sg = jax.nn.sigmoid(gv)
        du_ref[0] = (da * gv * sg).astype(BF)
        dg_ref[0] = (da * uv * sg * (1.0 + gv * (1.0 - sg))).astype(BF)

    act = pl.BlockSpec((1, tm, Fb), lambda j, i: (j, i, 0))
    shp = jax.ShapeDtypeStruct((nb, S, Fb), BF)
    return _call(body, grid=(nb, S // tm), name=name, args=[dyb, wd, g, u], comm=comm,
                 in_specs=[pl.BlockSpec((tm, D), lambda j, i: (i, 0)), pl.BlockSpec((1, Fb, D), lambda j, i: (j, 0, 0)), act, act],
                 out_specs=[act, act], out_shape=[shp, shp])


def _ffn_dwd(a, dyb, name, comm=()):
    nb, S, Fb = a.shape
    D = dyb.shape[1]
    ts = _tile(S, 512)
    ns = S // ts

    def body(a_ref, dy_ref, o_ref, acc_ref):
        s = pl.program_id(1)

        @pl.when(s == 0)
        def _():
            acc_ref[...] = jnp.zeros_like(acc_ref)

        acc_ref[...] += _tn(a_ref[0], dy_ref[...])

        @pl.when(s == ns - 1)
        def _():
            o_ref[0] = acc_ref[...].astype(BF)

    return _call(body, grid=(nb, ns), name=name, args=[a, dyb], comm=comm,
                 in_specs=[pl.BlockSpec((1, ts, Fb), lambda j, s: (j, s, 0)), pl.BlockSpec((ts, D), lambda j, s: (s, 0))],
                 out_specs=[pl.BlockSpec((1, Fb, D), lambda j, s: (j, 0, 0))], out_shape=[jax.ShapeDtypeStruct((nb, Fb, D), BF)],
                 scratch=[pltpu.VMEM((Fb, D), F32)])


def _ffn_dwgu(h, dg, du, name, comm=()):
    S, D = h.shape
    nb, _, Fb = dg.shape
    ts = _tile(S, 512)
    ns = S // ts

    def body(h_ref, dg_ref, du_ref, og_ref, ou_ref, accg_ref, accu_ref):
        s = pl.program_id(1)

        @pl.when(s == 0)
        def _():
            accg_ref[...] = jnp.zeros_like(accg_ref)
            accu_ref[...] = jnp.zeros_like(accu_ref)

        hv = h_ref[...]
        accg_ref[...] += _tn(hv, dg_ref[0])
        accu_ref[...] += _tn(hv, du_ref[0])

        @pl.when(s == ns - 1)
        def _():
            og_ref[0] = accg_ref[...].astype(BF)
            ou_ref[0] = accu_ref[...].astype(BF)

    act = pl.BlockSpec((1, ts, Fb), lambda j, s: (j, s, 0))
    out = pl.BlockSpec((1, D, Fb), lambda j, s: (j, 0, 0))
    shp = jax.ShapeDtypeStruct((nb, D, Fb), BF)
    return _call(body, grid=(nb, ns), name=name, args=[h, dg, du], comm=comm,
                 in_specs=[pl.BlockSpec((ts, D), lambda j, s: (s, 0)), act, act], out_specs=[out, out], out_shape=[shp, shp],
                 scratch=[pltpu.VMEM((D, Fb), F32), pltpu.VMEM((D, Fb), F32)])


def _dh_rms_bwd(pairs, blocked, tk, x, gn, dxo, out_scale, name, comm=()):
    S, D = x.shape
    nk = pairs[0][0].shape[0] if blocked else pairs[0][0].shape[1] // tk
    tm = _tile(S, 512)
    npair = len(pairs)

    def body(*refs):
        ins = refs[: 2 * npair]
        x_ref, gn_ref, dxo_ref, dx_ref, dxb_ref, dgn_ref, acc_ref = refs[2 * npair:]
        i, k = pl.program_id(0), pl.program_id(1)

        @pl.when(k == 0)
        def _():
            acc_ref[...] = jnp.zeros_like(acc_ref)

        for p in range(npair):
            l_ref, r_ref = ins[2 * p], ins[2 * p + 1]
            if blocked:
                acc_ref[...] += _nt(l_ref[0], r_ref[0])
            else:
                acc_ref[...] += _nt(l_ref[...], r_ref[...])

        @pl.when((k == nk - 1) & (i == 0))
        def _():
            dgn_ref[...] = jnp.zeros_like(dgn_ref)

        @pl.when(k == nk - 1)
        def _():
            def chunk(t, carry):
                rows = pl.ds(pl.multiple_of(t * EPI_ROWS, EPI_ROWS), EPI_ROWS)
                xv = x_ref[rows, :]
                r = lax.rsqrt(jnp.mean(xv * xv, axis=-1, keepdims=True) + NORM_EPS)
                xh = xv * r
                dh = acc_ref[rows, :]
                dgn_ref[...] += jnp.sum(dh * xh, axis=0, keepdims=True)
                dxh = dh * gn_ref[...]
                dx = dxo_ref[rows, :] + r * (dxh - xh * jnp.mean(dxh * xh, axis=-1, keepdims=True))
                dx_ref[rows, :] = dx
                dxb_ref[rows, :] = (out_scale * dx).astype(BF)
                return carry

            lax.fori_loop(0, tm // EPI_ROWS, chunk, 0)

    if blocked:
        lspec = pl.BlockSpec((1, tm, tk), lambda i, k: (k, i, 0))
        rspec = pl.BlockSpec((1, D, tk), lambda i, k: (k, 0, 0))
    else:
        lspec = pl.BlockSpec((tm, tk), lambda i, k: (i, k))
        rspec = pl.BlockSpec((D, tk), lambda i, k: (0, k))
    row = pl.BlockSpec((tm, D), lambda i, k: (i, 0))
    once = _once((tm, D), lambda i, k: (i, 0))
    vec = pl.BlockSpec((1, D), lambda i, k: (0, 0))
    flat = [t for pr in pairs for t in pr]
    return _call(body, grid=(S // tm, nk), name=name, args=[*flat, x, gn, dxo], comm=comm,
                 in_specs=[lspec, rspec] * npair + [once, vec, once], out_specs=[row, row, vec],
                 out_shape=[jax.ShapeDtypeStruct((S, D), F32), jax.ShapeDtypeStruct((S, D), BF), jax.ShapeDtypeStruct((1, D), F32)],
                 scratch=[pltpu.VMEM((tm, D), F32)])


def _mm_nn(a, b, tm, tn, col0, col1, dtype, name, comm=()):
    M, K = a.shape
    n0, nn = col0 // tn, (col1 - col0) // tn

    def body(a_ref, b_ref, o_ref):
        o_ref[...] = _nn(a_ref[...], b_ref[...]).astype(dtype)

    return _call(body, grid=(nn, M // tm), name=name, args=[a, b], comm=comm,
                 in_specs=[pl.BlockSpec((tm, K), lambda n, i: (i, 0)), pl.BlockSpec((K, tn), lambda n, i: (0, n0 + n))],
                 out_specs=[pl.BlockSpec((tm, tn), lambda n, i: (i, n))], out_shape=[jax.ShapeDtypeStruct((M, nn * tn), dtype)])


def _mm_tn(a, b, tm, tn, ts, blocked, name, comm=()):
    S, M = a.shape
    N = b.shape[1]
    ns = S // ts

    def body(a_ref, b_ref, o_ref, acc_ref):
        s = pl.program_id(2)

        @pl.when(s == 0)
        def _():
            acc_ref[...] = jnp.zeros_like(acc_ref)

        acc_ref[...] += _tn(a_ref[...], b_ref[...])

        @pl.when(s == ns - 1)
        def _():
            if blocked:
                o_ref[0] = acc_ref[...].astype(BF)
            else:
                o_ref[...] = acc_ref[...].astype(BF)

    if blocked:
        ospec = pl.BlockSpec((1, tm, tn), lambda i, n, s: (n, i, 0))
        oshape = jax.ShapeDtypeStruct((N // tn, M, tn), BF)
    else:
        ospec = pl.BlockSpec((tm, tn), lambda i, n, s: (i, n))
        oshape = jax.ShapeDtypeStruct((M, N), BF)
    return _call(body, grid=(M // tm, N // tn, ns), name=name, args=[a, b], comm=comm,
                 in_specs=[pl.BlockSpec((ts, tm), lambda i, n, s: (s, i)), pl.BlockSpec((ts, tn), lambda i, n, s: (s, n))],
                 out_specs=[ospec], out_shape=[oshape], scratch=[pltpu.VMEM((tm, tn), F32)])


def _rope_tables(S):
    half = ROPE_DIM // 2
    inv_freq = ROPE_THETA ** (-jnp.arange(0, ROPE_DIM, 2, dtype=F32) / ROPE_DIM)
    ang = jnp.arange(S, dtype=F32)[:, None] * inv_freq[None, :]
    cos, sin = jnp.cos(ang), jnp.sin(ang)
    zeros = jnp.zeros((S, HEAD_DIM - ROPE_DIM), F32)
    c = jnp.concatenate([cos, cos, jnp.ones((S, HEAD_DIM - ROPE_DIM), F32)], axis=1)
    sm = jnp.concatenate([-sin, jnp.zeros((S, half), F32), zeros], axis=1)
    sp = jnp.concatenate([jnp.zeros((S, half), F32), sin, zeros], axis=1)
    return c, sm, sp


def _rope(t, c, sm, sp):
    return t * c + pltpu.roll(t, HEAD_DIM - ROPE_DIM // 2, 1) * sm + pltpu.roll(t, ROPE_DIM // 2, 1) * sp


def _rope_t(dy, c, sm, sp):
    return dy * c + pltpu.roll(dy * sm, ROPE_DIM // 2, 1) + pltpu.roll(dy * sp, HEAD_DIM - ROPE_DIM // 2, 1)


def _att_mask(i):
    qi = lax.broadcasted_iota(jnp.int32, (BLK, 2 * BLK), 0)
    kj = lax.broadcasted_iota(jnp.int32, (BLK, 2 * BLK), 1)
    diff = qi + BLK - kj
    first_key = jnp.where(i > 0, 0, BLK)
    return (diff >= 0) & (diff <= BLK) & (kj >= first_key)


def _res_rows(r, i, n, d):
    if d == 1:
        return pl.ds(pl.multiple_of(i * n, n), n)
    return pl.ds(r + i * (n * d), n, stride=d)


def _att_specs(S, gi):
    def sect(off):
        base = (off + gi * GROUP_W) // HEAD_DIM
        return _once((S, HEAD_DIM), lambda hh, r: (0, base + hh))

    tab = pl.BlockSpec((S, HEAD_DIM), lambda hh, r: (0, 0))
    head = pl.BlockSpec((S, HEAD_DIM), lambda hh, r: (0, hh))
    return sect, tab, head


def _att_fwd(qkv, tabs, gi, d, name):
    S = qkv.shape[0]
    L = S // d
    sect, tab, head = _att_specs(S, gi)
    nblk = L // BLK
    scale = HEAD_DIM ** -0.5

    def body(q_ref, k_ref, v_ref, c_ref, sm_ref, sp_ref, o_ref, lse_ref, qr, kp, vp):
        r = pl.program_id(1)
        res = _res_rows(r, 0, L, d)
        c, sm, sp = c_ref[res, :], sm_ref[res, :], sp_ref[res, :]
        qr[...] = _rope(q_ref[res, :], c, sm, sp).astype(BF)
        kp[pl.ds(0, BLK), :] = jnp.zeros((BLK, HEAD_DIM), BF)
        vp[pl.ds(0, BLK), :] = jnp.zeros((BLK, HEAD_DIM), BF)
        kp[pl.ds(BLK, L), :] = _rope(k_ref[res, :], c, sm, sp).astype(BF)
        vp[pl.ds(BLK, L), :] = v_ref[res, :].astype(BF)

        def blk(i, carry):
            r0 = pl.multiple_of(i * BLK, BLK)
            s = _nt(qr[pl.ds(r0, BLK), :], kp[pl.ds(r0, 2 * BLK), :]) * scale
            s = jnp.where(_att_mask(i), s, NEG)
            m = jnp.max(s, axis=-1, keepdims=True)
            p = jnp.exp(s - m)
            l = jnp.sum(p, axis=-1, keepdims=True)
            out = _res_rows(r, i, BLK, d)
            o_ref[out, :] = _nn(p.astype(BF), vp[pl.ds(r0, 2 * BLK), :]) / l
            lse_ref[out, :] = jnp.broadcast_to(m + jnp.log(l), (BLK, HEAD_DIM))
            return carry

        lax.fori_loop(0, nblk, blk, 0)

    shp = jax.ShapeDtypeStruct((S, GROUP_W), F32)
    return _call(body, grid=(HEADS_PER_GROUP, d), name=name, args=[qkv, qkv, qkv, *tabs],
                 in_specs=[sect(Q_OFF), sect(K_OFF), sect(V_OFF), tab, tab, tab], out_specs=[head, head], out_shape=[shp, shp],
                 scratch=[pltpu.VMEM((L, HEAD_DIM), BF), pltpu.VMEM((L + BLK, HEAD_DIM), BF), pltpu.VMEM((L + BLK, HEAD_DIM), BF)])[0]


def _att_combine(os, lses, name):
    S = os[0].shape[0]
    tm = _tile(S, 512)

    def body(o0, o1, o2, l0, l1, l2, oa_ref, lse_ref):
        a, b, c = l0[...], l1[...], l2[...]
        mx = jnp.maximum(jnp.maximum(a, b), c)
        wa, wb, wc = jnp.exp(a - mx), jnp.exp(b - mx), jnp.exp(c - mx)
        den = wa + wb + wc
        oa_ref[...] = ((wa * o0[...] + wb * o1[...] + wc * o2[...]) / den).astype(BF)
        lse_ref[...] = mx + jnp.log(den)

    row = pl.BlockSpec((tm, GROUP_W), lambda i: (i, 0))
    return _call(body, grid=(S // tm,), name=name, args=[*os, *lses], in_specs=[row] * 6, out_specs=[row, row],
                 out_shape=[jax.ShapeDtypeStruct((S, GROUP_W), BF), jax.ShapeDtypeStruct((S, GROUP_W), F32)])[0]


def _att_bwd(qkv, tabs, do, lse, dvec, gi, d, name):
    S = qkv.shape[0]
    L = S // d
    sect, tab, head = _att_specs(S, gi)
    stat = _once((S, HEAD_DIM), lambda hh, r: (0, hh))
    nblk = L // BLK
    scale = HEAD_DIM ** -0.5

    def body(q_ref, k_ref, v_ref, c_ref, sm_ref, sp_ref, do_ref, lse_ref, dv_ref, dq_out, dk_out, dv_out, qr, kp, vp, dkp, dvp):
        r = pl.program_id(1)
        res = _res_rows(r, 0, L, d)
        c, sm, sp = c_ref[res, :], sm_ref[res, :], sp_ref[res, :]
        qr[...] = _rope(q_ref[res, :], c, sm, sp).astype(BF)
        kp[pl.ds(0, BLK), :] = jnp.zeros((BLK, HEAD_DIM), BF)
        vp[pl.ds(0, BLK), :] = jnp.zeros((BLK, HEAD_DIM), BF)
        kp[pl.ds(BLK, L), :] = _rope(k_ref[res, :], c, sm, sp).astype(BF)
        vp[pl.ds(BLK, L), :] = v_ref[res, :].astype(BF)
        dkp[...] = jnp.zeros_like(dkp)
        dvp[...] = jnp.zeros_like(dvp)

        def blk(i, carry):
            r0 = pl.multiple_of(i * BLK, BLK)
            rows, win, pos = pl.ds(r0, BLK), pl.ds(r0, 2 * BLK), _res_rows(r, i, BLK, d)
            q, kw, vw, dob = qr[rows, :], kp[win, :], vp[win, :], do_ref[pos, :].astype(BF)
            s = jnp.where(_att_mask(i), _nt(q, kw) * scale, NEG)
            p = jnp.exp(s - lse_ref[pos, :][:, :1])
            ds = p * (_nt(dob, vw) - dv_ref[pos, :][:, :1]) * scale
            dsb = ds.astype(BF)
            dq_out[pos, :] = _rope_t(_nn(dsb, kw), c_ref[pos, :], sm_ref[pos, :], sp_ref[pos, :])
            dkp[win, :] += _tn(dsb, q)
            dvp[win, :] += _tn(p.astype(BF), dob)
            return carry

        lax.fori_loop(0, nblk, blk, 0)
        dk_out[res, :] = _rope_t(dkp[pl.ds(BLK, L), :], c, sm, sp)
        dv_out[res, :] = dvp[pl.ds(BLK, L), :]

    shp = jax.ShapeDtypeStruct((S, GROUP_W), F32)
    return _call(body, grid=(HEADS_PER_GROUP, d), name=name, args=[qkv, qkv, qkv, *tabs, do, lse, dvec],
                 in_specs=[sect(Q_OFF), sect(K_OFF), sect(V_OFF), tab, tab, tab, stat, stat, stat],
                 out_specs=[head, head, head], out_shape=[shp, shp, shp],
                 scratch=[pltpu.VMEM((L, HEAD_DIM), BF), pltpu.VMEM((L + BLK, HEAD_DIM), BF), pltpu.VMEM((L + BLK, HEAD_DIM), BF),
                          pltpu.VMEM((L + BLK, HEAD_DIM), F32), pltpu.VMEM((L + BLK, HEAD_DIM), F32)])[0]


def _sg_parts(u_ref, vs_ref, g_ref, b_ref):
    uv = u_ref[...].astype(F32)
    vv = vs_ref[...].astype(F32)
    vg = _gelu(vv)
    mu = jnp.mean(vg, axis=-1, keepdims=True)
    vc = vg - mu
    rs = lax.rsqrt(jnp.mean(vc * vc, axis=-1, keepdims=True) + LN_EPS)
    y = vc * rs
    return uv, vv, rs, y, y * g_ref[...] + b_ref[...]


def _sg_wmask():
    t = lax.broadcasted_iota(jnp.int32, (BLK, BLK), 0)
    s = lax.broadcasted_iota(jnp.int32, (BLK, BLK), 1)
    return s <= t


def _sg_fwd(proj, sgw, sgbT, lng, lnb, name):
    S, P = proj.shape

    def body(u_ref, vs_ref, w_ref, bt_ref, g_ref, b_ref, z_ref):
        uv, _, _, _, vln = _sg_parts(u_ref, vs_ref, g_ref, b_ref)
        ug = _gelu(uv)
        vb = vln.astype(BF)
        mask = _sg_wmask()
        bt = bt_ref[...]
        for g in range(SG_GROUPS):
            cols = slice(g * BLK, (g + 1) * BLK)
            w = jnp.where(mask, w_ref[g], 0.0).astype(BF)
            sp = _nn(w, vb[:, cols]) + bt[:, g:g + 1]
            z_ref[:, cols] = (ug[:, cols] * sp).astype(BF)

    tile = lambda off: pl.BlockSpec((BLK, SG_W), lambda i: (i, off // SG_W))
    full = lambda shape: pl.BlockSpec(shape, lambda i: (0,) * len(shape))
    return _call(body, grid=(S // BLK,), name=name, args=[proj, proj, sgw, sgbT, lng, lnb],
                 in_specs=[tile(R_U), tile(R_VS), full((SG_GROUPS, BLK, BLK)), full((BLK, BLK)), full((1, SG_W)), full((1, SG_W))],
                 out_specs=[pl.BlockSpec((BLK, SG_W), lambda i: (i, 0))], out_shape=[jax.ShapeDtypeStruct((S, SG_W), BF)])[0][0]


def _sg_bwd(proj, dz, sgw, sgbT, lng, lnb, name):
    S, P = proj.shape

    def body(u_ref, vs_ref, dz_ref, w_ref, bt_ref, g_ref, b_ref, du_ref, dvs_ref, dw_ref, dbt_ref, dg_ref, db_ref, dvln):
        @pl.when(pl.program_id(0) == 0)
        def _():
            dw_ref[...] = jnp.zeros_like(dw_ref)
            dbt_ref[...] = jnp.zeros_like(dbt_ref)
            dg_ref[...] = jnp.zeros_like(dg_ref)
            db_ref[...] = jnp.zeros_like(db_ref)

        uv, vv, rs, y, vln = _sg_parts(u_ref, vs_ref, g_ref, b_ref)
        ug = _gelu(uv)
        vb = vln.astype(BF)
        dzv = dz_ref[...].astype(F32)
        dsp = dzv * ug
        dspb = dsp.astype(BF)
        mask = _sg_wmask()
        bt = bt_ref[...]
        lane = lax.broadcasted_iota(jnp.int32, (BLK, BLK), 1)
        dbt = jnp.zeros((BLK, BLK), F32)
        for g in range(SG_GROUPS):
            cols = slice(g * BLK, (g + 1) * BLK)
            w = jnp.where(mask, w_ref[g], 0.0).astype(BF)
            sp = _nn(w, vb[:, cols]) + bt[:, g:g + 1]
            du_ref[:, cols] = (dzv[:, cols] * sp * _gelu_grad(uv[:, cols])).astype(BF)
            dw_ref[g] += jnp.where(mask, _nt(dspb[:, cols], vb[:, cols]), 0.0)
            dbt = dbt + jnp.where(lane == g, jnp.sum(dsp[:, cols], axis=-1, keepdims=True), 0.0)
            dvln[:, cols] = _tn(w, dspb[:, cols])
        dbt_ref[...] += dbt
        dvl = dvln[...]
        dg_ref[...] += jnp.sum(dvl * y, axis=0, keepdims=True)
        db_ref[...] += jnp.sum(dvl, axis=0, keepdims=True)
        dy = dvl * g_ref[...]
        dvg = rs * (dy - jnp.mean(dy, axis=-1, keepdims=True) - y * jnp.mean(dy * y, axis=-1, keepdims=True))
        dvs_ref[...] = (dvg * _gelu_grad(vv)).astype(BF)

    tile = lambda off: pl.BlockSpec((BLK, SG_W), lambda i: (i, off // SG_W))
    full = lambda shape: pl.BlockSpec(shape, lambda i: (0,) * len(shape))
    row = pl.BlockSpec((BLK, SG_W), lambda i: (i, 0))
    return _call(body, grid=(S // BLK,), name=name, args=[proj, proj, dz, sgw, sgbT, lng, lnb],
                 in_specs=[tile(R_U), tile(R_VS), row, full((SG_GROUPS, BLK, BLK)), full((BLK, BLK)), full((1, SG_W)), full((1, SG_W))],
                 out_specs=[row, row, full((SG_GROUPS, BLK, BLK)), full((BLK, BLK)), full((1, SG_W)), full((1, SG_W))],
                 out_shape=[jax.ShapeDtypeStruct((S, SG_W), BF), jax.ShapeDtypeStruct((S, SG_W), BF),
                            jax.ShapeDtypeStruct((SG_GROUPS, BLK, BLK), F32), jax.ShapeDtypeStruct((BLK, BLK), F32),
                            jax.ShapeDtypeStruct((1, SG_W), F32), jax.ShapeDtypeStruct((1, SG_W), F32)],
                 scratch=[pltpu.VMEM((BLK, SG_W), F32)])[0]


def _gate_merge(oatt, z, watt, wsg, proj, name, comm=()):
    S = oatt.shape[0]
    nb, _, Db = watt.shape
    D = nb * Db
    tm = _tile(S, 512)
    ga, gs = R_GA // Db, (R_GA + D) // Db

    def body(oa_ref, z_ref, wa_ref, ws_ref, ga_ref, gs_ref, ya_ref, ys_ref, mg_ref):
        ya = _nn(oa_ref[...], wa_ref[0])
        ys = _nn(z_ref[...], ws_ref[0])
        ya_ref[...] = ya.astype(BF)
        ys_ref[...] = ys.astype(BF)
        mg_ref[...] = (jax.nn.sigmoid(ga_ref[...].astype(F32)) * ya + jax.nn.sigmoid(gs_ref[...].astype(F32)) * ys).astype(BF)

    out = pl.BlockSpec((tm, Db), lambda j, i: (i, j))
    shp = jax.ShapeDtypeStruct((S, D), BF)
    return _call(body, grid=(nb, S // tm), name=name, args=[oatt, z, watt, wsg, proj, proj], comm=comm,
                 in_specs=[pl.BlockSpec((tm, GROUP_W), lambda j, i: (i, 0)), pl.BlockSpec((tm, SG_W), lambda j, i: (i, 0)),
                           pl.BlockSpec((1, GROUP_W, Db), lambda j, i: (j, 0, 0)), pl.BlockSpec((1, SG_W, Db), lambda j, i: (j, 0, 0)),
                           pl.BlockSpec((tm, Db), lambda j, i: (i, ga + j)), pl.BlockSpec((tm, Db), lambda j, i: (i, gs + j))],
                 out_specs=[out, out, out], out_shape=[shp, shp, shp])


def _mix_out(merged, wout, x, gn, name):
    S, D = x.shape
    tm = _tile(S, 256)

    def body(m_ref, w_ref, x_ref, gn_ref, xo_ref, hn_ref):
        xo = x_ref[...] + _nn(m_ref[...], w_ref[...])
        r = lax.rsqrt(jnp.mean(xo * xo, axis=-1, keepdims=True) + NORM_EPS)
        xo_ref[...] = xo
        hn_ref[...] = (xo * r * gn_ref[...]).astype(BF)

    row = pl.BlockSpec((tm, D), lambda i: (i, 0))
    return _call(body, grid=(S // tm,), name=name, args=[merged, wout, x, gn],
                 in_specs=[row, pl.BlockSpec((D, D), lambda i: (0, 0)), row, pl.BlockSpec((1, D), lambda i: (0, 0))],
                 out_specs=[row, row], out_shape=[jax.ShapeDtypeStruct((S, D), F32), jax.ShapeDtypeStruct((S, D), BF)])[0]


def _mix_bwd_gate(dmix, wout, ya, ys, proj, name):
    S, D = dmix.shape
    tm, tn = _tile(S, 512), 512
    ga, gs = R_GA // tn, (R_GA + D) // tn

    def body(dm_ref, w_ref, ya_ref, ys_ref, ga_ref, gs_ref, dya_ref, dys_ref, dga_ref, dgs_ref):
        dm = _nt(dm_ref[...], w_ref[...])
        sa = jax.nn.sigmoid(ga_ref[...].astype(F32))
        ss = jax.nn.sigmoid(gs_ref[...].astype(F32))
        dya_ref[...] = (dm * sa).astype(BF)
        dys_ref[...] = (dm * ss).astype(BF)
        dga_ref[...] = (dm * ya_ref[...].astype(F32) * sa * (1.0 - sa)).astype(BF)
        dgs_ref[...] = (dm * ys_ref[...].astype(F32) * ss * (1.0 - ss)).astype(BF)

    out = pl.BlockSpec((tm, tn), lambda i, n: (i, n))
    shp = jax.ShapeDtypeStruct((S, D), BF)
    return _call(body, grid=(S // tm, D // tn), name=name, args=[dmix, wout, ya, ys, proj, proj],
                 in_specs=[pl.BlockSpec((tm, D), lambda i, n: (i, 0)), pl.BlockSpec((tn, D), lambda i, n: (n, 0)), out, out,
                           pl.BlockSpec((tm, tn), lambda i, n: (i, ga + n)), pl.BlockSpec((tm, tn), lambda i, n: (i, gs + n))],
                 out_specs=[out] * 4, out_shape=[shp] * 4)[0]


def _att_sg_dout(dya, dys, watt, wsg, oatt, name):
    S, D = dya.shape
    nb, _, Db = watt.shape
    tm = _tile(S, 512)

    def body(dya_ref, dys_ref, wa_ref, ws_ref, oa_ref, do_ref, dz_ref, dvec_ref, acca, accs):
        j = pl.program_id(1)

        @pl.when(j == 0)
        def _():
            acca[...] = jnp.zeros_like(acca)
            accs[...] = jnp.zeros_like(accs)

        acca[...] += _nt(dya_ref[...], wa_ref[0])
        accs[...] += _nt(dys_ref[...], ws_ref[0])

        @pl.when(j == nb - 1)
        def _():
            dov = acca[...]
            do_ref[...] = dov
            dz_ref[...] = accs[...].astype(BF)
            prod = dov * oa_ref[...].astype(F32)
            for hh in range(HEADS_PER_GROUP):
                cols = slice(hh * HEAD_DIM, (hh + 1) * HEAD_DIM)
                dvec_ref[:, cols] = jnp.broadcast_to(jnp.sum(prod[:, cols], axis=-1, keepdims=True), (tm, HEAD_DIM))

    blk = pl.BlockSpec((tm, Db), lambda i, j: (i, j))
    att = pl.BlockSpec((tm, GROUP_W), lambda i, j: (i, 0))
    return _call(body, grid=(S // tm, nb), name=name, args=[dya, dys, watt, wsg, oatt],
                 in_specs=[blk, blk, pl.BlockSpec((1, GROUP_W, Db), lambda i, j: (j, 0, 0)), pl.BlockSpec((1, SG_W, Db), lambda i, j: (j, 0, 0)), att],
                 out_specs=[att, pl.BlockSpec((tm, SG_W), lambda i, j: (i, 0)), att],
                 out_shape=[jax.ShapeDtypeStruct((S, GROUP_W), F32), jax.ShapeDtypeStruct((S, SG_W), BF), jax.ShapeDtypeStruct((S, GROUP_W), F32)],
                 scratch=[pltpu.VMEM((tm, GROUP_W), F32), pltpu.VMEM((tm, SG_W), F32)])[0]


def _small_allreduce(pack, name):
    R = pack.shape[0]

    def body(p_ref, o_ref, gath, send, recv):
        x, y, c = _place()
        me = 4 * x + 2 * y + c
        gath[me] = p_ref[...]
        copies = []
        for r in range(1, N_DEV):
            px, py, pc = _flip(x, r & 4), _flip(y, r & 2), _flip(c, r & 1)
            peer = 4 * px + 2 * py + pc
            mk = lambda dst: pltpu.make_async_remote_copy(src_ref=p_ref, dst_ref=dst, send_sem=send.at[r - 1], recv_sem=recv.at[r - 1],
                                                          device_id=(px, py, pc), device_id_type=MESH)
            snd = mk(gath.at[me])
            snd.start()
            copies.append((snd, mk(gath.at[peer])))
        for snd, rcv in copies:
            rcv.wait_recv()
            snd.wait_send()
        acc = gath[0]
        for s in range(1, N_DEV):
            acc = acc + gath[s]
        o_ref[...] = acc

    vm = pl.BlockSpec(memory_space=pltpu.VMEM)
    return pl.pallas_call(
        body, name=name, in_specs=[vm], out_specs=vm, out_shape=jax.ShapeDtypeStruct(pack.shape, F32),
        scratch_shapes=[pltpu.VMEM((N_DEV, R, 128), F32), pltpu.SemaphoreType.DMA((7,)), pltpu.SemaphoreType.DMA((7,))],
        compiler_params=pltpu.CompilerParams(vmem_limit_bytes=VMEM_LIMIT),
    )(pack)


def _row_tile(R, C):
    tr = R
    while tr * C > 262144 and tr % 32 == 0:
        tr //= 2
    return tr


def _pair_add(parts, other, name):
    _, R, C = parts.shape
    tr = _row_tile(R, C)

    def body(c_ref, p_ref, o_ref, s_ref):
        s_ref[0] = (p_ref[0].astype(F32) + o_ref[0].astype(F32)).astype(BF)

    core = lax.axis_index("c").astype(jnp.int32).reshape(1)
    return pl.pallas_call(
        body, name=name,
        grid_spec=pltpu.PrefetchScalarGridSpec(
            num_scalar_prefetch=1, grid=(N_CHIP, R // tr),
            in_specs=[pl.BlockSpec((1, tr, C), lambda q, i, c: (2 * q + c[0], i, 0)), pl.BlockSpec((1, tr, C), lambda q, i, c: (q, i, 0))],
            out_specs=pl.BlockSpec((1, tr, C), lambda q, i, c: (q, i, 0))),
        out_shape=jax.ShapeDtypeStruct((N_CHIP, R, C), BF),
        compiler_params=pltpu.CompilerParams(dimension_semantics=("arbitrary", "arbitrary"), vmem_limit_bytes=VMEM_LIMIT),
    )(core, parts, other)


def _adamw(parts, w, m, v, name):
    ns, R, C = parts.shape
    tr = _row_tile(R, C)
    c1 = 1.0 - ADAM_B1 ** ADAM_STEP
    c2 = 1.0 - ADAM_B2 ** ADAM_STEP

    def body(p_ref, w_ref, m_ref, v_ref, g_ref, d_ref, nm_ref, nv_ref):
        g = p_ref[0].astype(F32)
        for s in range(1, ns):
            g = g + p_ref[s].astype(F32)
        mn = ADAM_B1 * m_ref[...] + (1.0 - ADAM_B1) * g
        vn = ADAM_B2 * v_ref[...] + (1.0 - ADAM_B2) * (g * g)
        g_ref[...] = g
        nm_ref[...] = mn
        nv_ref[...] = vn
        d_ref[...] = -ADAM_LR * ((mn / c1) / (jnp.sqrt(vn / c2) + ADAM_EPS) + ADAM_WD * w_ref[...])

    row = pl.BlockSpec((tr, C), lambda i: (i, 0))
    shp = jax.ShapeDtypeStruct((R, C), F32)
    return _call(body, grid=(R // tr,), name=name, args=[parts, w, m, v],
                 in_specs=[pl.BlockSpec((ns, tr, C), lambda i: (0, i, 0)), row, row, row], out_specs=[row] * 4, out_shape=[shp] * 4)[0]


def _pad_rows(a, rows):
    return jnp.pad(a, ((0, rows - a.shape[0]), (0, 0)))


def kernel(x, ffn1_norm, ffn1_w_gate, ffn1_w_up, ffn1_w_down, mix_norm, w_in, sg_ln_g, sg_ln_b, sg_w, sg_b, w_att_out, w_sg_out, w_out, ffn2_norm, ffn2_w_gate, ffn2_w_up, ffn2_w_down, final_norm, loss_target, m_ffn1_norm, m_ffn1_w_gate, m_ffn1_w_up, m_ffn1_w_down, m_mix_norm, m_w_in, m_sg_ln_g, m_sg_ln_b, m_sg_w, m_sg_b, m_w_att_out, m_w_sg_out, m_w_out, m_ffn2_norm, m_ffn2_w_gate, m_ffn2_w_up, m_ffn2_w_down, m_final_norm, v_ffn1_norm, v_ffn1_w_gate, v_ffn1_w_up, v_ffn1_w_down, v_mix_norm, v_w_in, v_sg_ln_g, v_sg_ln_b, v_sg_w, v_sg_b, v_w_att_out, v_w_sg_out, v_w_out, v_ffn2_norm, v_ffn2_w_gate, v_ffn2_w_up, v_ffn2_w_down, v_final_norm):
    S, D = x.shape[1], x.shape[2]
    Pb = w_in.shape[2]
    P = N_DEV * Pb
    assert P == GA_OFF + 2 * D and D % (N_DEV * 128) == 0 and S % (BLK * DILATIONS[-1]) == 0
    xs, tgt = x[0], loss_target[0]

    sharded = dict(ffn1_w_gate=ffn1_w_gate, ffn1_w_up=ffn1_w_up, ffn1_w_down=ffn1_w_down, w_in=w_in, w_att_out=w_att_out,
                   w_sg_out=w_sg_out, w_out=w_out, ffn2_w_gate=ffn2_w_gate, ffn2_w_up=ffn2_w_up, ffn2_w_down=ffn2_w_down)
    sb = {n: w[0].astype(BF) for n, w in sharded.items()}

    wg1, wu1 = _comm_only(_Gather([sb["ffn1_w_gate"], sb["ffn1_w_up"]]), "gather_ffn1")
    h1 = _rms_fwd(xs, ffn1_norm, "rms1")
    win_top, win_bot = sb["w_in"][: D // 2], sb["w_in"][D // 2:]
    (g1, u1, a1), ((wd1, win8a),) = _ffn_up(h1, wg1, wu1, "ffn1_up", comm=[_Gather([sb["ffn1_w_down"], win_top])])
    (x1, h2), ((win8b, watt, wsg, wout8),) = _ffn_down_norm(
        a1, wd1, xs, mix_norm, "ffn1_down", comm=[_Gather([win_bot, sb["w_att_out"], sb["w_sg_out"], sb["w_out"]])])
    win = jnp.concatenate([w8.transpose(1, 0, 2).reshape(D // 2, P) for w8 in (win8a, win8b)], axis=0)
    wout = wout8.reshape(D, D)
    tm_proj = _tile(S, 1024)
    (qkv,), ((wg2,),) = _mm_nn(h2, win, tm_proj, 512, 0, U_OFF, F32, "proj_qkv", comm=[_Gather([sb["ffn2_w_gate"]])])
    (rest,), ((wu2,),) = _mm_nn(h2, win, tm_proj, 512, U_OFF, P, BF, "proj_rest", comm=[_Gather([sb["ffn2_w_up"]])])
    tabs = _rope_tables(S)
    os, lses = [], []
    for gi, d in enumerate(DILATIONS):
        o, l = _att_fwd(qkv, tabs, gi, d, f"att_fwd{gi}")
        os.append(o)
        lses.append(l)
    oatt, lse = _att_combine(os, lses, "att_combine")
    sgw = sg_w[0]
    sgbT = jnp.pad(sg_b[0].T, ((0, 0), (0, BLK - SG_GROUPS)))
    z = _sg_fwd(rest, sgw, sgbT, sg_ln_g, sg_ln_b, "sg_fwd")
    (ya, ys, merged), _ = _gate_merge(oatt, z, watt, wsg, rest, "gate_merge")
    x2, h3 = _mix_out(merged, wout, x1, ffn2_norm, "mix_out")
    (g3, u3, a3), ((wd2,),) = _ffn_up(h3, wg2, wu2, "ffn2_up", comm=[_Gather([sb["ffn2_w_down"]])])
    dx3, dyb3, d_final, loss_part = _ffn_down_loss(a3, wd2, x2, final_norm.reshape(1, D), tgt, "ffn2_down_loss")

    Fb = wg2.shape[2]
    Db = watt.shape[2]
    (dg3, du3), _ = _ffn_bwd_act(dyb3, wd2, g3, u3, "ffn2_bwd_act")
    (dwd2,), _ = _ffn_dwd(a3, dyb3, "ffn2_dwd")
    (dwg2, dwu2), _ = _ffn_dwgu(h3, dg3, du3, "ffn2_dwgu")
    ffn2_parts = [dwd2, dwg2, dwu2]
    (dx2, dmixb, d_ffn2n), (ffn2_other,) = _dh_rms_bwd([(dg3, wg2), (du3, wu2)], True, Fb, x2, ffn2_norm, dx3, 1.0, "ffn2_dh",
                                                     comm=[_Swap(ffn2_parts)])
    ffn2_sums = [_pair_add(p, o, f"pair_ffn2_{i}") for i, (p, o) in enumerate(zip(ffn2_parts, ffn2_other))]

    dya, dys, dga, dgs = _mix_bwd_gate(dmixb, wout, ya, ys, rest, "mix_bwd_gate")
    (dwout,), _ = _mm_tn(merged, dmixb, _tile(D, 1024), _tile(D, 1024), _tile(S, 1024), False, "dw_out")
    do, dz, dvec = _att_sg_dout(dya, dys, watt, wsg, oatt, "att_sg_dout")
    (dwatt,), _ = _mm_tn(oatt, dya, GROUP_W, Db, _tile(S, 1024), True, "dw_att")
    (dwsg,), _ = _mm_tn(z, dys, SG_W, Db, _tile(S, 1024), True, "dw_sg")
    mix_parts = [dwout.reshape(N_DEV, D // N_DEV, D), dwatt, dwsg]
    du, dvs, d_sgw, d_sgbT, d_lng, d_lnb = _sg_bwd(rest, dz, sgw, sgbT, sg_ln_g, sg_ln_b, "sg_bwd")
    dqs, dks, dvs_att = [], [], []
    for gi, d in enumerate(DILATIONS):
        dq, dk, dv = _att_bwd(qkv, tabs, do, lse, dvec, gi, d, f"att_bwd{gi}")
        dqs.append(dq)
        dks.append(dk)
        dvs_att.append(dv)
    dproj = jnp.concatenate([t.astype(BF) for t in dqs + dks + dvs_att] + [du, dvs, dga, dgs], axis=1)
    (dx1, dyb1, d_mixn), (ffn2_got, mix_other) = _dh_rms_bwd([(dproj, win)], False, 512, x1, mix_norm, dx2, 0.5, "proj_dh",
                                                           comm=[_Ici(ffn2_sums), _Swap(mix_parts)])
    mix_sums = [_pair_add(p, o, f"pair_mix_{i}") for i, (p, o) in enumerate(zip(mix_parts, mix_other))]
    (dwin,), (mix_got,) = _mm_tn(h2, dproj, D, 512, _tile(S, 1024), False, "dw_in", comm=[_Ici(mix_sums)])
    dwin = dwin.reshape(D, N_DEV, Pb).transpose(1, 0, 2)

    (dg1, du1), ((win_other,),) = _ffn_bwd_act(dyb1, wd1, g1, u1, "ffn1_bwd_act", comm=[_Swap([dwin])])
    win_sum = _pair_add(dwin, win_other, "pair_win")
    (dwg1, dwu1), ((win_got,),) = _ffn_dwgu(h1, dg1, du1, "ffn1_dwgu", comm=[_Ici([win_sum])])
    (dwd1,), (gu_other,) = _ffn_dwd(a1, dyb1, "ffn1_dwd", comm=[_Swap([dwg1, dwu1])])
    gu_sums = [_pair_add(p, o, f"pair_gu1_{i}") for i, (p, o) in enumerate(zip([dwg1, dwu1], gu_other))]
    (dx0, _, d_ffn1n), (gu_got, (wd1_other,)) = _dh_rms_bwd([(dg1, wg1), (du1, wu1)], True, Fb, xs, ffn1_norm, dx1, 1.0, "ffn1_dh",
                                                           comm=[_Ici(gu_sums), _Swap([dwd1])])
    wd1_sum = _pair_add(dwd1, wd1_other, "pair_wd1")
    (wd1_got,) = _comm_only(_Ici([wd1_sum]), "exchange_wd1")

    got = dict(ffn2_w_down=ffn2_got[0], ffn2_w_gate=ffn2_got[1], ffn2_w_up=ffn2_got[2], w_out=mix_got[0], w_att_out=mix_got[1],
               w_sg_out=mix_got[2], w_in=win_got, ffn1_w_gate=gu_got[0], ffn1_w_up=gu_got[1], ffn1_w_down=wd1_got)
    moments = dict(ffn1_w_gate=(m_ffn1_w_gate, v_ffn1_w_gate), ffn1_w_up=(m_ffn1_w_up, v_ffn1_w_up),
                   ffn1_w_down=(m_ffn1_w_down, v_ffn1_w_down), w_in=(m_w_in, v_w_in), w_att_out=(m_w_att_out, v_w_att_out),
                   w_sg_out=(m_w_sg_out, v_w_sg_out), w_out=(m_w_out, v_w_out), ffn2_w_gate=(m_ffn2_w_gate, v_ffn2_w_gate),
                   ffn2_w_up=(m_ffn2_w_up, v_ffn2_w_up), ffn2_w_down=(m_ffn2_w_down, v_ffn2_w_down))
    res = {}
    for n in sharded:
        mm, vv = moments[n]
        outs = _adamw(got[n], sharded[n][0], mm[0], vv[0], "adamw_" + n)
        res[n] = [o[None] for o in outs]

    rows = lambda a: a.reshape(-1, 128)
    small = [("sg_w", rows(d_sgw), sg_w, m_sg_w, v_sg_w), ("ffn1_norm", rows(d_ffn1n), ffn1_norm, m_ffn1_norm, v_ffn1_norm),
             ("mix_norm", rows(d_mixn), mix_norm, m_mix_norm, v_mix_norm), ("ffn2_norm", rows(d_ffn2n), ffn2_norm, m_ffn2_norm, v_ffn2_norm),
             ("final_norm", rows(d_final), final_norm, m_final_norm, v_final_norm), ("sg_ln_g", rows(d_lng), sg_ln_g, m_sg_ln_g, v_sg_ln_g),
             ("sg_ln_b", rows(d_lnb), sg_ln_b, m_sg_ln_b, v_sg_ln_b), ("sg_b", d_sgbT[:, :SG_GROUPS].T, sg_b, m_sg_b, v_sg_b)]
    pad8 = lambda a: _pad_rows(a, -(-a.shape[0] // 8) * 8)
    gpack = jnp.concatenate([pad8(g) for _, g, _, _, _ in small] + [pad8(loss_part)], axis=0)
    gsum = _small_allreduce(gpack, "allreduce_small")
    zero8 = jnp.zeros((8, 128), F32)
    wpack = jnp.concatenate([pad8(rows(w)) for _, _, w, _, _ in small] + [zero8], axis=0)
    mpack = jnp.concatenate([pad8(rows(m)) for _, _, _, m, _ in small] + [zero8], axis=0)
    vpack = jnp.concatenate([pad8(rows(v)) for _, _, _, _, v in small] + [zero8], axis=0)
    packs = _adamw(gsum[None], wpack, mpack, vpack, "adamw_small")
    off = 0
    for n, g, w, _, _ in small:
        r = g.shape[0]
        res[n] = [p[off:off + r].reshape(w.shape) for p in packs]
        off += -(-r // 8) * 8
    loss = gsum[off, 0]

    order = ["ffn1_norm", "ffn1_w_gate", "ffn1_w_up", "ffn1_w_down", "mix_norm", "w_in", "sg_ln_g", "sg_ln_b", "sg_w", "sg_b",
             "w_att_out", "w_sg_out", "w_out", "ffn2_norm", "ffn2_w_gate", "ffn2_w_up", "ffn2_w_down", "final_norm"]
    return (loss, dx0[None], *[res[n][0] for n in order], *[res[n][1] for n in order], *[res[n][2] for n in order],
            *[res[n][3] for n in order])
```

```python
import math

import jax
import jax.numpy as jnp
from jax import lax
from jax.experimental import pallas as pl
from jax.experimental.pallas import tpu as pltpu

BF = jnp.bfloat16
F32 = jnp.float32
MESH = pl.DeviceIdType.MESH
N_DEV = 8
N_CHIP = 4

HEAD_DIM = 128
HEADS_PER_GROUP = 4
GROUP_W = HEADS_PER_GROUP * HEAD_DIM
DILATIONS = (1, 4, 16)
ATT_W = len(DILATIONS) * GROUP_W
SG_W = 1536
SG_GROUPS = 12
BLK = 128
ROPE_DIM = 32
ROPE_THETA = 500000.0
NORM_EPS = 1e-6
LN_EPS = 1e-5
Q_OFF, K_OFF, V_OFF, U_OFF, VS_OFF, GA_OFF = 0, ATT_W, 2 * ATT_W, 3 * ATT_W, 3 * ATT_W + SG_W, 3 * ATT_W + 2 * SG_W

ADAM_LR, ADAM_B1, ADAM_B2, ADAM_EPS, ADAM_WD, ADAM_STEP = 0.001, 0.9, 0.999, 1e-08, 0.01, 10

VMEM_LIMIT = 56 * 1024 * 1024
NEG = -1e30
ANY = pl.BlockSpec(memory_space=pl.ANY)
EPI_ROWS = 128
ACC_COLS = 512
FFN_PAIR = 2
PROJ_TK = 1536
R_U, R_VS, R_GA = 0, SG_W, 2 * SG_W


def _once(shape, index_map):
    return pl.BlockSpec(shape, index_map, pipeline_mode=pl.Buffered(1))


def _tile(n, pref):
    t = min(n, pref)
    while n % t:
        t //= 2
    return t


def _nt(a, b):
    return lax.dot_general(a, b, (((1,), (1,)), ((), ())), preferred_element_type=F32)


def _tn(a, b):
    return lax.dot_general(a, b, (((0,), (0,)), ((), ())), preferred_element_type=F32)


def _nn(a, b):
    return jnp.dot(a, b, preferred_element_type=F32)


def _acc_dots(acc_ref, terms):
    n = acc_ref.shape[1]
    width = min(n, ACC_COLS)
    for c0 in range(0, n, width):
        cols = slice(c0, c0 + width)
        tot = None
        for lhs, rhs in terms:
            part = _nn(lhs, rhs(cols))
            tot = part if tot is None else tot + part
        acc_ref[:, cols] += tot


def _gelu(x):
    return 0.5 * x * (1.0 + lax.erf(x * (2.0 ** -0.5)))


def _gelu_grad(x):
    return 0.5 * (1.0 + lax.erf(x * (2.0 ** -0.5))) + x * jnp.exp(-0.5 * x * x) * (1.0 / math.sqrt(2.0 * math.pi))


def _place():
    x, y, c = lax.axis_index("x"), lax.axis_index("y"), lax.axis_index("c")
    return x, y, c


def _flip(v, bit):
    return 1 - v if bit else v


class _Gather:
    def __init__(self, shards):
        self.arrays = list(shards)
        nw = len(shards)
        self.out_shape = [jax.ShapeDtypeStruct((N_DEV,) + s.shape, s.dtype) for s in shards]
        self.scratch = [pltpu.SemaphoreType.DMA((nw, 7)), pltpu.SemaphoreType.DMA((nw, 7)), pltpu.SemaphoreType.DMA((nw,))]

    def _parts(self, ins, outs, sems):
        x, y, c = _place()
        send, recv, loc = sems
        chips = [(1 - x, y), (x, 1 - y), (1 - x, 1 - y)]

        def copy(k, s, block, to, src=None):
            dst = outs[k].at[4 * block[0] + 2 * block[1] + block[2]]
            return pltpu.make_async_remote_copy(src_ref=dst if src is None else src, dst_ref=dst, send_sem=send.at[k, s],
                                                recv_sem=recv.at[k, s], device_id=to, device_id_type=MESH)

        def first(k):
            return [copy(k, 0, (x, y, c), (x, y, 1 - c), src=ins[k])] + [
                copy(k, 1 + j, (x, y, c), (*chip, c), src=ins[k]) for j, chip in enumerate(chips)]

        def local(k):
            return pltpu.make_async_copy(ins[k], outs[k].at[4 * x + 2 * y + c], loc.at[k])

        return x, y, c, chips, copy, first, local

    def start(self, ins, outs, sems):
        _, _, _, _, _, first, local = self._parts(ins, outs, sems)
        for k in range(len(ins)):
            local(k).start()
            for cp in first(k):
                cp.start()

    def mid(self, ins, outs, sems):
        x, y, c, chips, copy, _, _ = self._parts(ins, outs, sems)
        for k in range(len(ins)):
            for j, chip in enumerate(chips):
                copy(k, 1 + j, (*chip, c), (x, y, c)).wait_recv()
                copy(k, 4 + j, (*chip, c), (x, y, 1 - c)).start()

    def finish(self, ins, outs, sems):
        x, y, c, chips, copy, first, local = self._parts(ins, outs, sems)
        for k in range(len(ins)):
            copy(k, 0, (x, y, 1 - c), (x, y, c)).wait_recv()
            for j, chip in enumerate(chips):
                copy(k, 4 + j, (*chip, 1 - c), (x, y, c)).wait_recv()
        for k in range(len(ins)):
            for cp in first(k):
                cp.wait_send()
            for j, chip in enumerate(chips):
                copy(k, 4 + j, (*chip, c), (x, y, 1 - c)).wait_send()
            local(k).wait()


class _Swap:
    def __init__(self, parts):
        self.arrays = list(parts)
        nw = len(parts)
        self.out_shape = [jax.ShapeDtypeStruct((N_CHIP,) + p.shape[1:], p.dtype) for p in parts]
        self.scratch = [pltpu.SemaphoreType.DMA((nw, N_CHIP)), pltpu.SemaphoreType.DMA((nw, N_CHIP))]

    def _copy(self, ins, outs, sems, k, q):
        x, y, c = _place()
        return pltpu.make_async_remote_copy(src_ref=ins[k].at[2 * q + 1 - c], dst_ref=outs[k].at[q], send_sem=sems[0].at[k, q],
                                            recv_sem=sems[1].at[k, q], device_id=(x, y, 1 - c), device_id_type=MESH)

    def start(self, ins, outs, sems):
        for k in range(len(ins)):
            for q in range(N_CHIP):
                self._copy(ins, outs, sems, k, q).start()

    def mid(self, ins, outs, sems):
        pass

    def finish(self, ins, outs, sems):
        for k in range(len(ins)):
            for q in range(N_CHIP):
                self._copy(ins, outs, sems, k, q).wait()


class _Ici:
    def __init__(self, sums):
        self.arrays = list(sums)
        nw = len(sums)
        self.out_shape = [jax.ShapeDtypeStruct(s.shape, s.dtype) for s in sums]
        self.scratch = [pltpu.SemaphoreType.DMA((nw, 3)), pltpu.SemaphoreType.DMA((nw, 3)), pltpu.SemaphoreType.DMA((nw,))]

    def _copies(self, ins, outs, sems, k):
        x, y, c = _place()
        myq = 2 * x + y
        out = []
        for r in range(1, N_CHIP):
            px, py = _flip(x, r & 2), _flip(y, r & 1)
            pq = 2 * px + py
            mk = lambda dst: pltpu.make_async_remote_copy(src_ref=ins[k].at[pq], dst_ref=dst, send_sem=sems[0].at[k, r - 1],
                                                          recv_sem=sems[1].at[k, r - 1], device_id=(px, py, c), device_id_type=MESH)
            out.append((mk(outs[k].at[myq]), mk(outs[k].at[pq])))
        return out, pltpu.make_async_copy(ins[k].at[myq], outs[k].at[myq], sems[2].at[k])

    def start(self, ins, outs, sems):
        for k in range(len(ins)):
            remote, local = self._copies(ins, outs, sems, k)
            local.start()
            for snd, _ in remote:
                snd.start()

    def mid(self, ins, outs, sems):
        pass

    def finish(self, ins, outs, sems):
        for k in range(len(ins)):
            remote, local = self._copies(ins, outs, sems, k)
            for snd, rcv in remote:
                rcv.wait_recv()
                snd.wait_send()
            local.wait()


def _call(body, *, grid, in_specs, out_specs, out_shape, name, args, scratch=(), comm=()):
    comm = list(comm)
    n_in, n_out, n_scr = len(in_specs), len(out_specs), len(scratch)
    total = math.prod(grid) if grid else 1

    def wrapped(*refs):
        p = n_in
        cin = []
        for cm in comm:
            cin.append(refs[p:p + len(cm.arrays)])
            p += len(cm.arrays)
        own_out = refs[p:p + n_out]
        p += n_out
        cout = []
        for cm in comm:
            cout.append(refs[p:p + len(cm.arrays)])
            p += len(cm.arrays)
        own_scr = refs[p:p + n_scr]
        p += n_scr
        csem = []
        for cm in comm:
            csem.append(refs[p:p + len(cm.scratch)])
            p += len(cm.scratch)
        step = 0
        for axis, g in enumerate(grid):
            step = step * g + pl.program_id(axis)

        def at(when, what):
            if total == 1:
                what()
            else:
                pl.when(step == when)(what)

        def starts():
            for cm, i, o, s in zip(comm, cin, cout, csem):
                cm.start(i, o, s)

        def mids():
            for cm, i, o, s in zip(comm, cin, cout, csem):
                cm.mid(i, o, s)

        def finishes():
            for cm, i, o, s in zip(comm, cin, cout, csem):
                cm.finish(i, o, s)

        if comm:
            at(0, starts)
        if body is not None:
            body(*refs[:n_in], *own_out, *own_scr)
        if comm:
            at(total - 1, mids)
            at(total - 1, finishes)

    kw = dict(grid=tuple(grid)) if grid else {}
    outs = pl.pallas_call(
        wrapped, name=name, **kw,
        in_specs=list(in_specs) + [ANY for cm in comm for _ in cm.arrays],
        out_specs=list(out_specs) + [ANY for cm in comm for _ in cm.arrays],
        out_shape=list(out_shape) + [s for cm in comm for s in cm.out_shape],
        scratch_shapes=list(scratch) + [s for cm in comm for s in cm.scratch],
        compiler_params=pltpu.CompilerParams(dimension_semantics=("arbitrary",) * len(grid), vmem_limit_bytes=VMEM_LIMIT),
    )(*args, *[a for cm in comm for a in cm.arrays])
    own, p, per = list(outs[:n_out]), n_out, []
    for cm in comm:
        per.append(list(outs[p:p + len(cm.arrays)]))
        p += len(cm.arrays)
    return own, per


def _comm_only(cm, name):
    return _call(None, grid=(), in_specs=[], out_specs=[], out_shape=[], name=name, args=[], comm=[cm])[1][0]


def _rms_fwd(x, g, name):
    S, D = x.shape
    tm = _tile(S, 512)

    def body(x_ref, g_ref, o_ref):
        xv = x_ref[...]
        r = lax.rsqrt(jnp.mean(xv * xv, axis=-1, keepdims=True) + NORM_EPS)
        o_ref[...] = (xv * r * g_ref[...]).astype(BF)

    return _call(body, grid=(S // tm,), name=name, args=[x, g],
                 in_specs=[pl.BlockSpec((tm, D), lambda i: (i, 0)), pl.BlockSpec((1, D), lambda i: (0, 0))],
                 out_specs=[pl.BlockSpec((tm, D), lambda i: (i, 0))], out_shape=[jax.ShapeDtypeStruct((S, D), BF)])[0][0]


def _ffn_up(h, wg, wu, name, comm=()):
    S, D = h.shape
    nb, _, Fb = wg.shape
    tm = _tile(S, 512)

    def body(h_ref, wg_ref, wu_ref, g_ref, u_ref, a_ref):
        hv = h_ref[...]
        g = _nn(hv, wg_ref[0])
        u = _nn(hv, wu_ref[0])
        g_ref[0] = g.astype(BF)
        u_ref[0] = u.astype(BF)
        a_ref[0] = (g * jax.nn.sigmoid(g) * u).astype(BF)

    act = pl.BlockSpec((1, tm, Fb), lambda j, i: (j, i, 0))
    w = pl.BlockSpec((1, D, Fb), lambda j, i: (j, 0, 0))
    shp = jax.ShapeDtypeStruct((nb, S, Fb), BF)
    return _call(body, grid=(nb, S // tm), name=name, args=[h, wg, wu], comm=comm,
                 in_specs=[pl.BlockSpec((tm, D), lambda j, i: (i, 0)), w, w], out_specs=[act, act, act], out_shape=[shp, shp, shp])


def _ffn_down_norm(a, wd, x, gn, name, comm=()):
    nb, S, Fb = a.shape
    D = wd.shape[2]
    tm = _tile(S, 512)

    nj = nb // FFN_PAIR

    def body(a_ref, wd_ref, x_ref, gn_ref, xo_ref, hn_ref, acc_ref):
        j = pl.program_id(1)

        @pl.when(j == 0)
        def _():
            acc_ref[...] = jnp.zeros_like(acc_ref)

        _acc_dots(acc_ref, [(a_ref[b], lambda cols, b=b: wd_ref[b, :, cols]) for b in range(FFN_PAIR)])

        @pl.when(j == nj - 1)
        def _():
            def chunk(t, carry):
                rows = pl.ds(pl.multiple_of(t * EPI_ROWS, EPI_ROWS), EPI_ROWS)
                xo = x_ref[rows, :] + 0.5 * acc_ref[rows, :]
                r = lax.rsqrt(jnp.mean(xo * xo, axis=-1, keepdims=True) + NORM_EPS)
                xo_ref[rows, :] = xo
                hn_ref[rows, :] = (xo * r * gn_ref[...]).astype(BF)
                return carry

            lax.fori_loop(0, tm // EPI_ROWS, chunk, 0)

    row = pl.BlockSpec((tm, D), lambda i, j: (i, 0))
    return _call(body, grid=(S // tm, nj), name=name, args=[a, wd, x, gn], comm=comm,
                 in_specs=[pl.BlockSpec((FFN_PAIR, tm, Fb), lambda i, j: (j, i, 0)), pl.BlockSpec((FFN_PAIR, Fb, D), lambda i, j: (j, 0, 0)),
                           _once((tm, D), lambda i, j: (i, 0)), pl.BlockSpec((1, D), lambda i, j: (0, 0))],
                 out_specs=[row, row], out_shape=[jax.ShapeDtypeStruct((S, D), F32), jax.ShapeDtypeStruct((S, D), BF)],
                 scratch=[pltpu.VMEM((tm, D), F32)])


def _ffn_down_loss(a, wd, x, gf, tgt, name):
    nb, S, Fb = a.shape
    D = wd.shape[2]
    tm = _tile(S, 512)

    nj = nb // FFN_PAIR

    def body(a_ref, wd_ref, x_ref, gf_ref, t_ref, dx_ref, dxb_ref, dgf_ref, loss_ref, acc_ref):
        i, j = pl.program_id(0), pl.program_id(1)

        @pl.when(j == 0)
        def _():
            acc_ref[...] = jnp.zeros_like(acc_ref)

        _acc_dots(acc_ref, [(a_ref[b], lambda cols, b=b: wd_ref[b, :, cols]) for b in range(FFN_PAIR)])

        @pl.when((j == nj - 1) & (i == 0))
        def _():
            dgf_ref[...] = jnp.zeros_like(dgf_ref)
            loss_ref[...] = jnp.zeros_like(loss_ref)

        @pl.when(j == nj - 1)
        def _():
            def chunk(t, carry):
                rows = pl.ds(pl.multiple_of(t * EPI_ROWS, EPI_ROWS), EPI_ROWS)
                xo = x_ref[rows, :] + 0.5 * acc_ref[rows, :]
                r = lax.rsqrt(jnp.mean(xo * xo, axis=-1, keepdims=True) + NORM_EPS)
                xh = xo * r
                gf = gf_ref[...]
                e = xh * gf - t_ref[rows, :]
                loss_ref[...] += jnp.sum(jnp.mean(e * e, axis=-1, keepdims=True), axis=0, keepdims=True) * 0.5
                dy = e * (1.0 / D)
                dgf_ref[...] += jnp.sum(dy * xh, axis=0, keepdims=True)
                dxh = dy * gf
                dx = r * (dxh - xh * jnp.mean(dxh * xh, axis=-1, keepdims=True))
                dx_ref[rows, :] = dx
                dxb_ref[rows, :] = (0.5 * dx).astype(BF)
                return carry

            lax.fori_loop(0, tm // EPI_ROWS, chunk, 0)

    row = pl.BlockSpec((tm, D), lambda i, j: (i, 0))
    once = _once((tm, D), lambda i, j: (i, 0))
    vec = pl.BlockSpec((1, D), lambda i, j: (0, 0))
    return _call(body, grid=(S // tm, nj), name=name, args=[a, wd, x, gf, tgt],
                 in_specs=[pl.BlockSpec((FFN_PAIR, tm, Fb), lambda i, j: (j, i, 0)), pl.BlockSpec((FFN_PAIR, Fb, D), lambda i, j: (j, 0, 0)),
                           once, vec, once],
                 out_specs=[row, row, vec, pl.BlockSpec((1, 128), lambda i, j: (0, 0))],
                 out_shape=[jax.ShapeDtypeStruct((S, D), F32), jax.ShapeDtypeStruct((S, D), BF), jax.ShapeDtypeStruct((1, D), F32),
                            jax.ShapeDtypeStruct((1, 128), F32)],
                 scratch=[pltpu.VMEM((tm, D), F32)])[0]


def _ffn_bwd_act(dyb, wdT, g, u, name, comm=()):
    S, D = dyb.shape
    nb, _, Fb = wdT.shape
    tm = _tile(S, 512)

    def body(dy_ref, wd_ref, g_ref, u_ref, dg_ref, du_ref):
        da = _nn(dy_ref[...], wd_ref[0])
        gv = g_ref[0].astype(F32)
        uv = u_ref[0].astype(F32)
        sg = jax.nn.sigmoid(gv)
        du_ref[0] = (da * gv * sg).astype(BF)
        dg_ref[0] = (da * uv * sg * (1.0 + gv * (1.0 - sg))).astype(BF)

    act = pl.BlockSpec((1, tm, Fb), lambda j, i: (j, i, 0))
    shp = jax.ShapeDtypeStruct((nb, S, Fb), BF)
    return _call(body, grid=(nb, S // tm), name=name, args=[dyb, wdT, g, u], comm=comm,
                 in_specs=[pl.BlockSpec((tm, D), lambda j, i: (i, 0)), pl.BlockSpec((1, D, Fb), lambda j, i: (j, 0, 0)), act, act],
                 out_specs=[act, act], out_shape=[shp, shp])


def _ffn_dwd(a, dyb, name, comm=()):
    nb, S, Fb = a.shape
    D = dyb.shape[1]
    ts = _tile(S, 512)
    ns = S // ts

    def body(a_ref, dy_ref, o_ref, acc_ref):
        s = pl.program_id(1)

        @pl.when(s == 0)
        def _():
            acc_ref[...] = jnp.zeros_like(acc_ref)

        acc_ref[...] += _tn(a_ref[0], dy_ref[...])

        @pl.when(s == ns - 1)
        def _():
            o_ref[0] = acc_ref[...].astype(BF)

    return _call(body, grid=(nb, ns), name=name, args=[a, dyb], comm=comm,
                 in_specs=[pl.BlockSpec((1, ts, Fb), lambda j, s: (j, s, 0)), pl.BlockSpec((ts, D), lambda j, s: (s, 0))],
                 out_specs=[pl.BlockSpec((1, Fb, D), lambda j, s: (j, 0, 0))], out_shape=[jax.ShapeDtypeStruct((nb, Fb, D), BF)],
                 scratch=[pltpu.VMEM((Fb, D), F32)])


def _ffn_dwgu(h, dg, du, name, comm=()):
    S, D = h.shape
    nb, _, Fb = dg.shape
    ts = _tile(S, 512)
    ns = S // ts

    def body(h_ref, dg_ref, du_ref, og_ref, ou_ref, accg_ref, accu_ref):
        s = pl.program_id(1)

        @pl.when(s == 0)
        def _():
            accg_ref[...] = jnp.zeros_like(accg_ref)
            accu_ref[...] = jnp.zeros_like(accu_ref)

        hv = h_ref[...]
        accg_ref[...] += _tn(hv, dg_ref[0])
        accu_ref[...] += _tn(hv, du_ref[0])

        @pl.when(s == ns - 1)
        def _():
            og_ref[0] = accg_ref[...].astype(BF)
            ou_ref[0] = accu_ref[...].astype(BF)

    act = pl.BlockSpec((1, ts, Fb), lambda j, s: (j, s, 0))
    out = pl.BlockSpec((1, D, Fb), lambda j, s: (j, 0, 0))
    shp = jax.ShapeDtypeStruct((nb, D, Fb), BF)
    return _call(body, grid=(nb, ns), name=name, args=[h, dg, du], comm=comm,
                 in_specs=[pl.BlockSpec((ts, D), lambda j, s: (s, 0)), act, act], out_specs=[out, out], out_shape=[shp, shp],
                 scratch=[pltpu.VMEM((D, Fb), F32), pltpu.VMEM((D, Fb), F32)])


def _dh_rms_bwd(pairs, blocked, tk, x, gn, dxo, out_scale, name, comm=()):
    S, D = x.shape
    nk = pairs[0][0].shape[0] if blocked else pairs[0][0].shape[1] // tk
    tm = _tile(S, 512)
    npair = len(pairs)

    def body(*refs):
        ins = refs[: 2 * npair]
        x_ref, gn_ref, dxo_ref, dx_ref, dxb_ref, dgn_ref, acc_ref = refs[2 * npair:]
        i, k = pl.program_id(0), pl.program_id(1)

        @pl.when(k == 0)
        def _():
            acc_ref[...] = jnp.zeros_like(acc_ref)

        if blocked:
            terms = [(ins[2 * p][0], lambda cols, r=ins[2 * p + 1]: r[0, :, cols]) for p in range(npair)]
        else:
            terms = [(ins[2 * p][...], lambda cols, r=ins[2 * p + 1]: r[:, cols]) for p in range(npair)]
        _acc_dots(acc_ref, terms)

        @pl.when((k == nk - 1) & (i == 0))
        def _():
            dgn_ref[...] = jnp.zeros_like(dgn_ref)

        @pl.when(k == nk - 1)
        def _():
            def chunk(t, carry):
                rows = pl.ds(pl.multiple_of(t * EPI_ROWS, EPI_ROWS), EPI_ROWS)
                xv = x_ref[rows, :]
                r = lax.rsqrt(jnp.mean(xv * xv, axis=-1, keepdims=True) + NORM_EPS)
                xh = xv * r
                dh = acc_ref[rows, :]
                dgn_ref[...] += jnp.sum(dh * xh, axis=0, keepdims=True)
                dxh = dh * gn_ref[...]
                dx = dxo_ref[rows, :] + r * (dxh - xh * jnp.mean(dxh * xh, axis=-1, keepdims=True))
                dx_ref[rows, :] = dx
                dxb_ref[rows, :] = (out_scale * dx).astype(BF)
                return carry

            lax.fori_loop(0, tm // EPI_ROWS, chunk, 0)

    if blocked:
        lspec = pl.BlockSpec((1, tm, tk), lambda i, k: (k, i, 0))
        rspec = pl.BlockSpec((1, tk, D), lambda i, k: (k, 0, 0))
    else:
        lspec = pl.BlockSpec((tm, tk), lambda i, k: (i, k))
        rspec = pl.BlockSpec((tk, D), lambda i, k: (k, 0))
    row = pl.BlockSpec((tm, D), lambda i, k: (i, 0))
    once = _once((tm, D), lambda i, k: (i, 0))
    vec = pl.BlockSpec((1, D), lambda i, k: (0, 0))
    flat = [t for pr in pairs for t in pr]
    return _call(body, grid=(S // tm, nk), name=name, args=[*flat, x, gn, dxo], comm=comm,
                 in_specs=[lspec, rspec] * npair + [once, vec, once], out_specs=[row, row, vec],
                 out_shape=[jax.ShapeDtypeStruct((S, D), F32), jax.ShapeDtypeStruct((S, D), BF), jax.ShapeDtypeStruct((1, D), F32)],
                 scratch=[pltpu.VMEM((tm, D), F32)])


def _mm_nn(a, b, tm, tn, col0, col1, dtype, name, comm=()):
    M, K = a.shape
    n0, nn = col0 // tn, (col1 - col0) // tn

    def body(a_ref, b_ref, o_ref):
        o_ref[...] = _nn(a_ref[...], b_ref[...]).astype(dtype)

    return _call(body, grid=(nn, M // tm), name=name, args=[a, b], comm=comm,
                 in_specs=[pl.BlockSpec((tm, K), lambda n, i: (i, 0)), pl.BlockSpec((K, tn), lambda n, i: (0, n0 + n))],
                 out_specs=[pl.BlockSpec((tm, tn), lambda n, i: (i, n))], out_shape=[jax.ShapeDtypeStruct((M, nn * tn), dtype)])


def _mm_tn(a, b, tm, tn, ts, blocked, name, comm=(), ncols=None):
    S, M = a.shape
    N = b.shape[1] if ncols is None else ncols
    ns = S // ts

    def body(a_ref, b_ref, o_ref, acc_ref):
        s = pl.program_id(2)

        @pl.when(s == 0)
        def _():
            acc_ref[...] = jnp.zeros_like(acc_ref)

        acc_ref[...] += _tn(a_ref[...], b_ref[...])

        @pl.when(s == ns - 1)
        def _():
            if blocked:
                o_ref[0] = acc_ref[...].astype(BF)
            else:
                o_ref[...] = acc_ref[...].astype(BF)

    if blocked:
        ospec = pl.BlockSpec((1, tm, tn), lambda i, n, s: (n, i, 0))
        oshape = jax.ShapeDtypeStruct((N // tn, M, tn), BF)
    else:
        ospec = pl.BlockSpec((tm, tn), lambda i, n, s: (i, n))
        oshape = jax.ShapeDtypeStruct((M, N), BF)
    return _call(body, grid=(M // tm, N // tn, ns), name=name, args=[a, b], comm=comm,
                 in_specs=[pl.BlockSpec((ts, tm), lambda i, n, s: (s, i)), pl.BlockSpec((ts, tn), lambda i, n, s: (s, n))],
                 out_specs=[ospec], out_shape=[oshape], scratch=[pltpu.VMEM((tm, tn), F32)])


def _rope_tables(S):
    half = ROPE_DIM // 2
    inv_freq = ROPE_THETA ** (-jnp.arange(0, ROPE_DIM, 2, dtype=F32) / ROPE_DIM)
    ang = jnp.arange(S, dtype=F32)[:, None] * inv_freq[None, :]
    cos, sin = jnp.cos(ang), jnp.sin(ang)
    zeros = jnp.zeros((S, HEAD_DIM - ROPE_DIM), F32)
    c = jnp.concatenate([cos, cos, jnp.ones((S, HEAD_DIM - ROPE_DIM), F32)], axis=1)
    sm = jnp.concatenate([-sin, jnp.zeros((S, half), F32), zeros], axis=1)
    sp = jnp.concatenate([jnp.zeros((S, half), F32), sin, zeros], axis=1)
    return c, sm, sp


def _rope(t, c, sm, sp):
    return t * c + pltpu.roll(t, HEAD_DIM - ROPE_DIM // 2, 1) * sm + pltpu.roll(t, ROPE_DIM // 2, 1) * sp


def _rope_t(dy, c, sm, sp):
    return dy * c + pltpu.roll(dy * sm, ROPE_DIM // 2, 1) + pltpu.roll(dy * sp, HEAD_DIM - ROPE_DIM // 2, 1)


def _att_mask(i):
    qi = lax.broadcasted_iota(jnp.int32, (BLK, 2 * BLK), 0)
    kj = lax.broadcasted_iota(jnp.int32, (BLK, 2 * BLK), 1)
    diff = qi + BLK - kj
    first_key = jnp.where(i > 0, 0, BLK)
    return (diff >= 0) & (diff <= BLK) & (kj >= first_key)


def _res_rows(r, i, n, d):
    if d == 1:
        return pl.ds(pl.multiple_of(i * n, n), n)
    return pl.ds(r + i * (n * d), n, stride=d)


def _att_specs(S, gi):
    def sect(off):
        base = (off + gi * GROUP_W) // HEAD_DIM
        return _once((S, HEAD_DIM), lambda hh, r: (0, base + hh))

    tab = pl.BlockSpec((S, HEAD_DIM), lambda hh, r: (0, 0))
    head = pl.BlockSpec((S, HEAD_DIM), lambda hh, r: (0, hh))
    return sect, tab, head


def _att_fwd(qkv, tabs, gi, d, name):
    S = qkv.shape[0]
    L = S // d
    sect, tab, head = _att_specs(S, gi)
    nblk = L // BLK
    scale = HEAD_DIM ** -0.5

    def body(q_ref, k_ref, v_ref, c_ref, sm_ref, sp_ref, o_ref, lse_ref, qr, kp, vp):
        r = pl.program_id(1)
        res = _res_rows(r, 0, L, d)
        c, sm, sp = c_ref[res, :], sm_ref[res, :], sp_ref[res, :]
        qr[...] = _rope(q_ref[res, :], c, sm, sp).astype(BF)
        kp[pl.ds(0, BLK), :] = jnp.zeros((BLK, HEAD_DIM), BF)
        vp[pl.ds(0, BLK), :] = jnp.zeros((BLK, HEAD_DIM), BF)
        kp[pl.ds(BLK, L), :] = _rope(k_ref[res, :], c, sm, sp).astype(BF)
        vp[pl.ds(BLK, L), :] = v_ref[res, :].astype(BF)

        def blk(i, carry):
            r0 = pl.multiple_of(i * BLK, BLK)
            s = _nt(qr[pl.ds(r0, BLK), :], kp[pl.ds(r0, 2 * BLK), :]) * scale
            s = jnp.where(_att_mask(i), s, NEG)
            m = jnp.max(s, axis=-1, keepdims=True)
            p = jnp.exp(s - m)
            l = jnp.sum(p, axis=-1, keepdims=True)
            out = _res_rows(r, i, BLK, d)
            o_ref[out, :] = _nn(p.astype(BF), vp[pl.ds(r0, 2 * BLK), :]) / l
            lse_ref[out, :] = jnp.broadcast_to(m + jnp.log(l), (BLK, HEAD_DIM))
            return carry

        lax.fori_loop(0, nblk, blk, 0, unroll=2)

    shp = jax.ShapeDtypeStruct((S, GROUP_W), F32)
    return _call(body, grid=(HEADS_PER_GROUP, d), name=name, args=[qkv, qkv, qkv, *tabs],
                 in_specs=[sect(Q_OFF), sect(K_OFF), sect(V_OFF), tab, tab, tab], out_specs=[head, head], out_shape=[shp, shp],
                 scratch=[pltpu.VMEM((L, HEAD_DIM), BF), pltpu.VMEM((L + BLK, HEAD_DIM), BF), pltpu.VMEM((L + BLK, HEAD_DIM), BF)])[0]


def _att_combine(os, lses, name):
    S = os[0].shape[0]
    tm = _tile(S, 512)

    def body(o0, o1, o2, l0, l1, l2, oa_ref, lse_ref):
        a, b, c = l0[...], l1[...], l2[...]
        mx = jnp.maximum(jnp.maximum(a, b), c)
        wa, wb, wc = jnp.exp(a - mx), jnp.exp(b - mx), jnp.exp(c - mx)
        den = wa + wb + wc
        oa_ref[...] = ((wa * o0[...] + wb * o1[...] + wc * o2[...]) / den).astype(BF)
        lse_ref[...] = mx + jnp.log(den)

    row = pl.BlockSpec((tm, GROUP_W), lambda i: (i, 0))
    return _call(body, grid=(S // tm,), name=name, args=[*os, *lses], in_specs=[row] * 6, out_specs=[row, row],
                 out_shape=[jax.ShapeDtypeStruct((S, GROUP_W), BF), jax.ShapeDtypeStruct((S, GROUP_W), F32)])[0]


def _att_bwd(qkv, tabs, do, lse, dvec, gi, d, name):
    S = qkv.shape[0]
    L = S // d
    sect, tab, head = _att_specs(S, gi)
    stat = _once((S, HEAD_DIM), lambda hh, r: (0, hh))
    nblk = L // BLK
    scale = HEAD_DIM ** -0.5

    def body(q_ref, k_ref, v_ref, c_ref, sm_ref, sp_ref, do_ref, lse_ref, dv_ref, dq_out, dk_out, dv_out, qr, kp, vp, dkp, dvp):
        r = pl.program_id(1)
        res = _res_rows(r, 0, L, d)
        c, sm, sp = c_ref[res, :], sm_ref[res, :], sp_ref[res, :]
        qr[...] = _rope(q_ref[res, :], c, sm, sp).astype(BF)
        kp[pl.ds(0, BLK), :] = jnp.zeros((BLK, HEAD_DIM), BF)
        vp[pl.ds(0, BLK), :] = jnp.zeros((BLK, HEAD_DIM), BF)
        kp[pl.ds(BLK, L), :] = _rope(k_ref[res, :], c, sm, sp).astype(BF)
        vp[pl.ds(BLK, L), :] = v_ref[res, :].astype(BF)
        dkp[...] = jnp.zeros_like(dkp)
        dvp[...] = jnp.zeros_like(dvp)

        def blk(i, carry):
            r0 = pl.multiple_of(i * BLK, BLK)
            rows, win, pos = pl.ds(r0, BLK), pl.ds(r0, 2 * BLK), _res_rows(r, i, BLK, d)
            q, kw, vw, dob = qr[rows, :], kp[win, :], vp[win, :], do_ref[pos, :].astype(BF)
            s = jnp.where(_att_mask(i), _nt(q, kw) * scale, NEG)
            p = jnp.exp(s - lse_ref[pos, :][:, :1])
            ds = p * (_nt(dob, vw) - dv_ref[pos, :][:, :1]) * scale
            dsb = ds.astype(BF)
            dq_out[pos, :] = _rope_t(_nn(dsb, kw), c_ref[pos, :], sm_ref[pos, :], sp_ref[pos, :])
            dkp[win, :] += _tn(dsb, q)
            dvp[win, :] += _tn(p.astype(BF), dob)
            return carry

        lax.fori_loop(0, nblk, blk, 0, unroll=2)
        dk_out[res, :] = _rope_t(dkp[pl.ds(BLK, L), :], c, sm, sp)
        dv_out[res, :] = dvp[pl.ds(BLK, L), :]

    shp = jax.ShapeDtypeStruct((S, GROUP_W), F32)
    return _call(body, grid=(HEADS_PER_GROUP, d), name=name, args=[qkv, qkv, qkv, *tabs, do, lse, dvec],
                 in_specs=[sect(Q_OFF), sect(K_OFF), sect(V_OFF), tab, tab, tab, stat, stat, stat],
                 out_specs=[head, head, head], out_shape=[shp, shp, shp],
                 scratch=[pltpu.VMEM((L, HEAD_DIM), BF), pltpu.VMEM((L + BLK, HEAD_DIM), BF), pltpu.VMEM((L + BLK, HEAD_DIM), BF),
                          pltpu.VMEM((L + BLK, HEAD_DIM), F32), pltpu.VMEM((L + BLK, HEAD_DIM), F32)])[0]


def _sg_parts(u_ref, vs_ref, g_ref, b_ref):
    uv = u_ref[...].astype(F32)
    vv = vs_ref[...].astype(F32)
    vg = _gelu(vv)
    mu = jnp.mean(vg, axis=-1, keepdims=True)
    vc = vg - mu
    rs = lax.rsqrt(jnp.mean(vc * vc, axis=-1, keepdims=True) + LN_EPS)
    y = vc * rs
    return uv, vv, rs, y, y * g_ref[...] + b_ref[...]


def _sg_wmask():
    t = lax.broadcasted_iota(jnp.int32, (BLK, BLK), 0)
    s = lax.broadcasted_iota(jnp.int32, (BLK, BLK), 1)
    return s <= t


def _sg_fwd(proj, sgw, sgbT, lng, lnb, name):
    S, P = proj.shape

    def body(u_ref, vs_ref, w_ref, bt_ref, g_ref, b_ref, z_ref):
        uv, _, _, _, vln = _sg_parts(u_ref, vs_ref, g_ref, b_ref)
        ug = _gelu(uv)
        vb = vln.astype(BF)
        mask = _sg_wmask()
        bt = bt_ref[...]
        for g in range(SG_GROUPS):
            cols = slice(g * BLK, (g + 1) * BLK)
            w = jnp.where(mask, w_ref[g], 0.0).astype(BF)
            sp = _nn(w, vb[:, cols]) + bt[:, g:g + 1]
            z_ref[:, cols] = (ug[:, cols] * sp).astype(BF)

    tile = lambda off: pl.BlockSpec((BLK, SG_W), lambda i: (i, off // SG_W))
    full = lambda shape: pl.BlockSpec(shape, lambda i: (0,) * len(shape))
    return _call(body, grid=(S // BLK,), name=name, args=[proj, proj, sgw, sgbT, lng, lnb],
                 in_specs=[tile(R_U), tile(R_VS), full((SG_GROUPS, BLK, BLK)), full((BLK, BLK)), full((1, SG_W)), full((1, SG_W))],
                 out_specs=[pl.BlockSpec((BLK, SG_W), lambda i: (i, 0))], out_shape=[jax.ShapeDtypeStruct((S, SG_W), BF)])[0][0]


def _sg_bwd(proj, dz, sgw, sgbT, lng, lnb, name):
    S, P = proj.shape

    def body(u_ref, vs_ref, dz_ref, w_ref, bt_ref, g_ref, b_ref, du_ref, dvs_ref, dw_ref, dbt_ref, dg_ref, db_ref, dvln):
        @pl.when(pl.program_id(0) == 0)
        def _():
            dw_ref[...] = jnp.zeros_like(dw_ref)
            dbt_ref[...] = jnp.zeros_like(dbt_ref)
            dg_ref[...] = jnp.zeros_like(dg_ref)
            db_ref[...] = jnp.zeros_like(db_ref)

        uv, vv, rs, y, vln = _sg_parts(u_ref, vs_ref, g_ref, b_ref)
        ug = _gelu(uv)
        vb = vln.astype(BF)
        dzv = dz_ref[...].astype(F32)
        dsp = dzv * ug
        dspb = dsp.astype(BF)
        mask = _sg_wmask()
        bt = bt_ref[...]
        lane = lax.broadcasted_iota(jnp.int32, (BLK, BLK), 1)
        dbt = jnp.zeros((BLK, BLK), F32)
        for g in range(SG_GROUPS):
            cols = slice(g * BLK, (g + 1) * BLK)
            w = jnp.where(mask, w_ref[g], 0.0).astype(BF)
            sp = _nn(w, vb[:, cols]) + bt[:, g:g + 1]
            du_ref[:, cols] = (dzv[:, cols] * sp * _gelu_grad(uv[:, cols])).astype(BF)
            dw_ref[g] += jnp.where(mask, _nt(dspb[:, cols], vb[:, cols]), 0.0)
            dbt = dbt + jnp.where(lane == g, jnp.sum(dsp[:, cols], axis=-1, keepdims=True), 0.0)
            dvln[:, cols] = _tn(w, dspb[:, cols])
        dbt_ref[...] += dbt
        dvl = dvln[...]
        dg_ref[...] += jnp.sum(dvl * y, axis=0, keepdims=True)
        db_ref[...] += jnp.sum(dvl, axis=0, keepdims=True)
        dy = dvl * g_ref[...]
        dvg = rs * (dy - jnp.mean(dy, axis=-1, keepdims=True) - y * jnp.mean(dy * y, axis=-1, keepdims=True))
        dvs_ref[...] = (dvg * _gelu_grad(vv)).astype(BF)

    tile = lambda off: pl.BlockSpec((BLK, SG_W), lambda i: (i, off // SG_W))
    full = lambda shape: pl.BlockSpec(shape, lambda i: (0,) * len(shape))
    row = pl.BlockSpec((BLK, SG_W), lambda i: (i, 0))
    return _call(body, grid=(S // BLK,), name=name, args=[proj, proj, dz, sgw, sgbT, lng, lnb],
                 in_specs=[tile(R_U), tile(R_VS), row, full((SG_GROUPS, BLK, BLK)), full((BLK, BLK)), full((1, SG_W)), full((1, SG_W))],
                 out_specs=[row, row, full((SG_GROUPS, BLK, BLK)), full((BLK, BLK)), full((1, SG_W)), full((1, SG_W))],
                 out_shape=[jax.ShapeDtypeStruct((S, SG_W), BF), jax.ShapeDtypeStruct((S, SG_W), BF),
                            jax.ShapeDtypeStruct((SG_GROUPS, BLK, BLK), F32), jax.ShapeDtypeStruct((BLK, BLK), F32),
                            jax.ShapeDtypeStruct((1, SG_W), F32), jax.ShapeDtypeStruct((1, SG_W), F32)],
                 scratch=[pltpu.VMEM((BLK, SG_W), F32)])[0]


def _gate_merge(oatt, z, watt, wsg, proj, name, comm=()):
    S = oatt.shape[0]
    nb, _, Db = watt.shape
    D = nb * Db
    tm = _tile(S, 512)
    ga, gs = R_GA // Db, (R_GA + D) // Db

    def body(oa_ref, z_ref, wa_ref, ws_ref, ga_ref, gs_ref, ya_ref, ys_ref, mg_ref):
        ya = _nn(oa_ref[...], wa_ref[0])
        ys = _nn(z_ref[...], ws_ref[0])
        ya_ref[...] = ya.astype(BF)
        ys_ref[...] = ys.astype(BF)
        mg_ref[...] = (jax.nn.sigmoid(ga_ref[...].astype(F32)) * ya + jax.nn.sigmoid(gs_ref[...].astype(F32)) * ys).astype(BF)

    out = pl.BlockSpec((tm, Db), lambda j, i: (i, j))
    shp = jax.ShapeDtypeStruct((S, D), BF)
    return _call(body, grid=(nb, S // tm), name=name, args=[oatt, z, watt, wsg, proj, proj], comm=comm,
                 in_specs=[pl.BlockSpec((tm, GROUP_W), lambda j, i: (i, 0)), pl.BlockSpec((tm, SG_W), lambda j, i: (i, 0)),
                           pl.BlockSpec((1, GROUP_W, Db), lambda j, i: (j, 0, 0)), pl.BlockSpec((1, SG_W, Db), lambda j, i: (j, 0, 0)),
                           pl.BlockSpec((tm, Db), lambda j, i: (i, ga + j)), pl.BlockSpec((tm, Db), lambda j, i: (i, gs + j))],
                 out_specs=[out, out, out], out_shape=[shp, shp, shp])


def _mix_out(merged, wout, x, gn, name):
    S, D = x.shape
    tm = _tile(S, 256)

    def body(m_ref, w_ref, x_ref, gn_ref, xo_ref, hn_ref):
        xo = x_ref[...] + _nn(m_ref[...], w_ref[...])
        r = lax.rsqrt(jnp.mean(xo * xo, axis=-1, keepdims=True) + NORM_EPS)
        xo_ref[...] = xo
        hn_ref[...] = (xo * r * gn_ref[...]).astype(BF)

    row = pl.BlockSpec((tm, D), lambda i: (i, 0))
    return _call(body, grid=(S // tm,), name=name, args=[merged, wout, x, gn],
                 in_specs=[row, pl.BlockSpec((D, D), lambda i: (0, 0)), row, pl.BlockSpec((1, D), lambda i: (0, 0))],
                 out_specs=[row, row], out_shape=[jax.ShapeDtypeStruct((S, D), F32), jax.ShapeDtypeStruct((S, D), BF)])[0]


def _mix_bwd_gate(dmix, wout, ya, ys, proj, name):
    S, D = dmix.shape
    tm, tn = _tile(S, 512), 512
    ga, gs = R_GA // tn, (R_GA + D) // tn

    def body(dm_ref, w_ref, ya_ref, ys_ref, ga_ref, gs_ref, dya_ref, dys_ref, dga_ref, dgs_ref):
        dm = _nn(dm_ref[...], w_ref[...])
        sa = jax.nn.sigmoid(ga_ref[...].astype(F32))
        ss = jax.nn.sigmoid(gs_ref[...].astype(F32))
        dya_ref[...] = (dm * sa).astype(BF)
        dys_ref[...] = (dm * ss).astype(BF)
        dga_ref[...] = (dm * ya_ref[...].astype(F32) * sa * (1.0 - sa)).astype(BF)
        dgs_ref[...] = (dm * ys_ref[...].astype(F32) * ss * (1.0 - ss)).astype(BF)

    out = pl.BlockSpec((tm, tn), lambda i, n: (i, n))
    shp = jax.ShapeDtypeStruct((S, D), BF)
    return _call(body, grid=(S // tm, D // tn), name=name, args=[dmix, wout, ya, ys, proj, proj],
                 in_specs=[pl.BlockSpec((tm, D), lambda i, n: (i, 0)), pl.BlockSpec((D, tn), lambda i, n: (0, n)), out, out,
                           pl.BlockSpec((tm, tn), lambda i, n: (i, ga + n)), pl.BlockSpec((tm, tn), lambda i, n: (i, gs + n))],
                 out_specs=[out] * 4, out_shape=[shp] * 4)[0]


def _att_sg_dout(dya, dys, wattT, wsgT, oatt, name):
    S, D = dya.shape
    nb, Db, _ = wattT.shape
    tm = _tile(S, 512)

    def body(dya_ref, dys_ref, wa_ref, ws_ref, oa_ref, do_ref, dz_ref, dvec_ref, acca, accs):
        j = pl.program_id(1)

        @pl.when(j == 0)
        def _():
            acca[...] = jnp.zeros_like(acca)
            accs[...] = jnp.zeros_like(accs)

        acca[...] += _nn(dya_ref[...], wa_ref[0])
        accs[...] += _nn(dys_ref[...], ws_ref[0])

        @pl.when(j == nb - 1)
        def _():
            dov = acca[...]
            do_ref[...] = dov
            dz_ref[...] = accs[...].astype(BF)
            prod = dov * oa_ref[...].astype(F32)
            for hh in range(HEADS_PER_GROUP):
                cols = slice(hh * HEAD_DIM, (hh + 1) * HEAD_DIM)
                dvec_ref[:, cols] = jnp.broadcast_to(jnp.sum(prod[:, cols], axis=-1, keepdims=True), (tm, HEAD_DIM))

    blk = pl.BlockSpec((tm, Db), lambda i, j: (i, j))
    att = pl.BlockSpec((tm, GROUP_W), lambda i, j: (i, 0))
    return _call(body, grid=(S // tm, nb), name=name, args=[dya, dys, wattT, wsgT, oatt],
                 in_specs=[blk, blk, pl.BlockSpec((1, Db, GROUP_W), lambda i, j: (j, 0, 0)), pl.BlockSpec((1, Db, SG_W), lambda i, j: (j, 0, 0)), att],
                 out_specs=[att, pl.BlockSpec((tm, SG_W), lambda i, j: (i, 0)), att],
                 out_shape=[jax.ShapeDtypeStruct((S, GROUP_W), F32), jax.ShapeDtypeStruct((S, SG_W), BF), jax.ShapeDtypeStruct((S, GROUP_W), F32)],
                 scratch=[pltpu.VMEM((tm, GROUP_W), F32), pltpu.VMEM((tm, SG_W), F32)])[0]


def _small_allreduce(pack, name):
    R = pack.shape[0]

    def body(p_ref, o_ref, gath, send, recv):
        x, y, c = _place()
        me = 4 * x + 2 * y + c
        gath[me] = p_ref[...]
        copies = []
        for r in range(1, N_DEV):
            px, py, pc = _flip(x, r & 4), _flip(y, r & 2), _flip(c, r & 1)
            peer = 4 * px + 2 * py + pc
            mk = lambda dst: pltpu.make_async_remote_copy(src_ref=p_ref, dst_ref=dst, send_sem=send.at[r - 1], recv_sem=recv.at[r - 1],
                                                          device_id=(px, py, pc), device_id_type=MESH)
            snd = mk(gath.at[me])
            snd.start()
            copies.append((snd, mk(gath.at[peer])))
        for snd, rcv in copies:
            rcv.wait_recv()
            snd.wait_send()
        acc = gath[0]
        for s in range(1, N_DEV):
            acc = acc + gath[s]
        o_ref[...] = acc

    vm = pl.BlockSpec(memory_space=pltpu.VMEM)
    return pl.pallas_call(
        body, name=name, in_specs=[vm], out_specs=vm, out_shape=jax.ShapeDtypeStruct(pack.shape, F32),
        scratch_shapes=[pltpu.VMEM((N_DEV, R, 128), F32), pltpu.SemaphoreType.DMA((7,)), pltpu.SemaphoreType.DMA((7,))],
        compiler_params=pltpu.CompilerParams(vmem_limit_bytes=VMEM_LIMIT),
    )(pack)


def _row_tile(R, C):
    tr = R
    while tr * C > 262144 and tr % 32 == 0:
        tr //= 2
    return tr


def _pair_add(parts, other, name):
    _, R, C = parts.shape
    tr = _row_tile(R, C)

    def body(c_ref, p_ref, o_ref, s_ref):
        s_ref[0] = (p_ref[0].astype(F32) + o_ref[0].astype(F32)).astype(BF)

    core = lax.axis_index("c").astype(jnp.int32).reshape(1)
    return pl.pallas_call(
        body, name=name,
        grid_spec=pltpu.PrefetchScalarGridSpec(
            num_scalar_prefetch=1, grid=(N_CHIP, R // tr),
            in_specs=[pl.BlockSpec((1, tr, C), lambda q, i, c: (2 * q + c[0], i, 0)), pl.BlockSpec((1, tr, C), lambda q, i, c: (q, i, 0))],
            out_specs=pl.BlockSpec((1, tr, C), lambda q, i, c: (q, i, 0))),
        out_shape=jax.ShapeDtypeStruct((N_CHIP, R, C), BF),
        compiler_params=pltpu.CompilerParams(dimension_semantics=("arbitrary", "arbitrary"), vmem_limit_bytes=VMEM_LIMIT),
    )(core, parts, other)


def _adamw(parts, w, m, v, name):
    ns, R, C = parts.shape
    tr = _row_tile(R, C)
    c1 = 1.0 - ADAM_B1 ** ADAM_STEP
    c2 = 1.0 - ADAM_B2 ** ADAM_STEP

    def body(p_ref, w_ref, m_ref, v_ref, g_ref, d_ref, nm_ref, nv_ref):
        g = p_ref[0].astype(F32)
        for s in range(1, ns):
            g = g + p_ref[s].astype(F32)
        mn = ADAM_B1 * m_ref[...] + (1.0 - ADAM_B1) * g
        vn = ADAM_B2 * v_ref[...] + (1.0 - ADAM_B2) * (g * g)
        g_ref[...] = g
        nm_ref[...] = mn
        nv_ref[...] = vn
        d_ref[...] = -ADAM_LR * ((mn / c1) / (jnp.sqrt(vn / c2) + ADAM_EPS) + ADAM_WD * w_ref[...])

    row = pl.BlockSpec((tr, C), lambda i: (i, 0))
    shp = jax.ShapeDtypeStruct((R, C), F32)
    return _call(body, grid=(R // tr,), name=name, args=[parts, w, m, v],
                 in_specs=[pl.BlockSpec((ns, tr, C), lambda i: (0, i, 0)), row, row, row], out_specs=[row] * 4, out_shape=[shp] * 4)[0]


def _pad_rows(a, rows):
    return jnp.pad(a, ((0, rows - a.shape[0]), (0, 0)))


def kernel(x, ffn1_norm, ffn1_w_gate, ffn1_w_up, ffn1_w_down, mix_norm, w_in, sg_ln_g, sg_ln_b, sg_w, sg_b, w_att_out, w_sg_out, w_out, ffn2_norm, ffn2_w_gate, ffn2_w_up, ffn2_w_down, final_norm, loss_target, m_ffn1_norm, m_ffn1_w_gate, m_ffn1_w_up, m_ffn1_w_down, m_mix_norm, m_w_in, m_sg_ln_g, m_sg_ln_b, m_sg_w, m_sg_b, m_w_att_out, m_w_sg_out, m_w_out, m_ffn2_norm, m_ffn2_w_gate, m_ffn2_w_up, m_ffn2_w_down, m_final_norm, v_ffn1_norm, v_ffn1_w_gate, v_ffn1_w_up, v_ffn1_w_down, v_mix_norm, v_w_in, v_sg_ln_g, v_sg_ln_b, v_sg_w, v_sg_b, v_w_att_out, v_w_sg_out, v_w_out, v_ffn2_norm, v_ffn2_w_gate, v_ffn2_w_up, v_ffn2_w_down, v_final_norm):
    S, D = x.shape[1], x.shape[2]
    Pb = w_in.shape[2]
    P = N_DEV * Pb
    assert P == GA_OFF + 2 * D and D % (N_DEV * 128) == 0 and S % (BLK * DILATIONS[-1]) == 0
    xs, tgt = x[0], loss_target[0]

    sharded = dict(ffn1_w_gate=ffn1_w_gate, ffn1_w_up=ffn1_w_up, ffn1_w_down=ffn1_w_down, w_in=w_in, w_att_out=w_att_out,
                   w_sg_out=w_sg_out, w_out=w_out, ffn2_w_gate=ffn2_w_gate, ffn2_w_up=ffn2_w_up, ffn2_w_down=ffn2_w_down)
    sb = {n: w[0].astype(BF) for n, w in sharded.items()}

    wg1, wu1 = _comm_only(_Gather([sb["ffn1_w_gate"], sb["ffn1_w_up"]]), "gather_ffn1")
    h1 = _rms_fwd(xs, ffn1_norm, "rms1")
    win_top, win_bot = sb["w_in"][: D // 2], sb["w_in"][D // 2:]
    (g1, u1, a1), ((wd1, win8a),) = _ffn_up(h1, wg1, wu1, "ffn1_up", comm=[_Gather([sb["ffn1_w_down"], win_top])])
    (x1, h2), ((win8b, watt, wsg, wout8),) = _ffn_down_norm(
        a1, wd1, xs, mix_norm, "ffn1_down", comm=[_Gather([win_bot, sb["w_att_out"], sb["w_sg_out"], sb["w_out"]])])
    win = jnp.concatenate([w8.transpose(1, 0, 2).reshape(D // 2, P) for w8 in (win8a, win8b)], axis=0)
    wout = wout8.reshape(D, D)
    tm_proj = _tile(S, 1024)
    (qkv,), ((wg2,),) = _mm_nn(h2, win, tm_proj, 512, 0, U_OFF, F32, "proj_qkv", comm=[_Gather([sb["ffn2_w_gate"]])])
    (rest,), ((wu2,),) = _mm_nn(h2, win, tm_proj, 512, U_OFF, P, BF, "proj_rest", comm=[_Gather([sb["ffn2_w_up"]])])
    tabs = _rope_tables(S)
    os, lses = [], []
    for gi, d in enumerate(DILATIONS):
        o, l = _att_fwd(qkv, tabs, gi, d, f"att_fwd{gi}")
        os.append(o)
        lses.append(l)
    oatt, lse = _att_combine(os, lses, "att_combine")
    sgw = sg_w[0]
    sgbT = jnp.pad(sg_b[0].T, ((0, 0), (0, BLK - SG_GROUPS)))
    z = _sg_fwd(rest, sgw, sgbT, sg_ln_g, sg_ln_b, "sg_fwd")
    (ya, ys, merged), _ = _gate_merge(oatt, z, watt, wsg, rest, "gate_merge")
    x2, h3 = _mix_out(merged, wout, x1, ffn2_norm, "mix_out")
    (g3, u3, a3), ((wd2,),) = _ffn_up(h3, wg2, wu2, "ffn2_up", comm=[_Gather([sb["ffn2_w_down"]])])
    dx3, dyb3, d_final, loss_part = _ffn_down_loss(a3, wd2, x2, final_norm.reshape(1, D), tgt, "ffn2_down_loss")

    Fb = wg2.shape[2]
    Db = watt.shape[2]
    tr = lambda w8: w8.transpose(0, 2, 1)
    p_pad = -(-P // PROJ_TK) * PROJ_TK
    winT = jnp.concatenate([tr(w8).reshape(P, D // 2) for w8 in (win8a, win8b)], axis=1)
    winT = jnp.pad(winT, ((0, p_pad - P), (0, 0)))
    (dg3, du3), _ = _ffn_bwd_act(dyb3, tr(wd2), g3, u3, "ffn2_bwd_act")
    (dwd2,), _ = _ffn_dwd(a3, dyb3, "ffn2_dwd")
    (dwg2, dwu2), _ = _ffn_dwgu(h3, dg3, du3, "ffn2_dwgu")
    ffn2_parts = [dwd2, dwg2, dwu2]
    (dx2, dmixb, d_ffn2n), (ffn2_other,) = _dh_rms_bwd([(dg3, tr(wg2)), (du3, tr(wu2))], True, Fb, x2, ffn2_norm, dx3, 1.0, "ffn2_dh",
                                                     comm=[_Swap(ffn2_parts)])
    ffn2_sums = [_pair_add(p, o, f"pair_ffn2_{i}") for i, (p, o) in enumerate(zip(ffn2_parts, ffn2_other))]

    dya, dys, dga, dgs = _mix_bwd_gate(dmixb, wout.T, ya, ys, rest, "mix_bwd_gate")
    (dwout,), _ = _mm_tn(merged, dmixb, _tile(D, 1024), _tile(D, 1024), _tile(S, 1024), False, "dw_out")
    do, dz, dvec = _att_sg_dout(dya, dys, tr(watt), tr(wsg), oatt, "att_sg_dout")
    (dwatt,), _ = _mm_tn(oatt, dya, GROUP_W, Db, _tile(S, 1024), True, "dw_att")
    (dwsg,), _ = _mm_tn(z, dys, SG_W, Db, _tile(S, 1024), True, "dw_sg")
    mix_parts = [dwout.reshape(N_DEV, D // N_DEV, D), dwatt, dwsg]
    du, dvs, d_sgw, d_sgbT, d_lng, d_lnb = _sg_bwd(rest, dz, sgw, sgbT, sg_ln_g, sg_ln_b, "sg_bwd")
    dqs, dks, dvs_att = [], [], []
    for gi, d in enumerate(DILATIONS):
        dq, dk, dv = _att_bwd(qkv, tabs, do, lse, dvec, gi, d, f"att_bwd{gi}")
        dqs.append(dq)
        dks.append(dk)
        dvs_att.append(dv)
    dproj = jnp.concatenate([t.astype(BF) for t in dqs + dks + dvs_att] + [du, dvs, dga, dgs, jnp.zeros((S, p_pad - P), BF)], axis=1)
    (dx1, dyb1, d_mixn), (ffn2_got, mix_other) = _dh_rms_bwd([(dproj, winT)], False, PROJ_TK, x1, mix_norm, dx2, 0.5, "proj_dh",
                                                           comm=[_Ici(ffn2_sums), _Swap(mix_parts)])
    mix_sums = [_pair_add(p, o, f"pair_mix_{i}") for i, (p, o) in enumerate(zip(mix_parts, mix_other))]
    (dwin,), (mix_got,) = _mm_tn(h2, dproj, D, 512, _tile(S, 1024), False, "dw_in", comm=[_Ici(mix_sums)], ncols=P)
    dwin = dwin.reshape(D, N_DEV, Pb).transpose(1, 0, 2)

    (dwd1,), ((win_other,),) = _ffn_dwd(a1, dyb1, "ffn1_dwd", comm=[_Swap([dwin])])
    win_sum = _pair_add(dwin, win_other, "pair_win")
    (dg1, du1), ((win_got,), (wd1_other,)) = _ffn_bwd_act(dyb1, tr(wd1), g1, u1, "ffn1_bwd_act", comm=[_Ici([win_sum]), _Swap([dwd1])])
    wd1_sum = _pair_add(dwd1, wd1_other, "pair_wd1")
    (dwg1, dwu1), ((wd1_got,),) = _ffn_dwgu(h1, dg1, du1, "ffn1_dwgu", comm=[_Ici([wd1_sum])])
    gu_parts = [dwg1, dwu1]
    gu_other = _comm_only(_Swap(gu_parts), "swap_gu1")
    gu_sums = [_pair_add(p, o, f"pair_gu1_{i}") for i, (p, o) in enumerate(zip(gu_parts, gu_other))]
    (dx0, _, d_ffn1n), (gu_got,) = _dh_rms_bwd([(dg1, tr(wg1)), (du1, tr(wu1))], True, Fb, xs, ffn1_norm, dx1, 1.0, "ffn1_dh",
                                               comm=[_Ici(gu_sums)])

    got = dict(ffn2_w_down=ffn2_got[0], ffn2_w_gate=ffn2_got[1], ffn2_w_up=ffn2_got[2], w_out=mix_got[0], w_att_out=mix_got[1],
               w_sg_out=mix_got[2], w_in=win_got, ffn1_w_gate=gu_got[0], ffn1_w_up=gu_got[1], ffn1_w_down=wd1_got)
    moments = dict(ffn1_w_gate=(m_ffn1_w_gate, v_ffn1_w_gate), ffn1_w_up=(m_ffn1_w_up, v_ffn1_w_up),
                   ffn1_w_down=(m_ffn1_w_down, v_ffn1_w_down), w_in=(m_w_in, v_w_in), w_att_out=(m_w_att_out, v_w_att_out),
                   w_sg_out=(m_w_sg_out, v_w_sg_out), w_out=(m_w_out, v_w_out), ffn2_w_gate=(m_ffn2_w_gate, v_ffn2_w_gate),
                   ffn2_w_up=(m_ffn2_w_up, v_ffn2_w_up), ffn2_w_down=(m_ffn2_w_down, v_ffn2_w_down))
    res = {}
    for n in sharded:
        mm, vv = moments[n]
        outs = _adamw(got[n], sharded[n][0], mm[0], vv[0], "adamw_" + n)
        res[n] = [o[None] for o in outs]

    rows = lambda a: a.reshape(-1, 128)
    small = [("sg_w", rows(d_sgw), sg_w, m_sg_w, v_sg_w), ("ffn1_norm", rows(d_ffn1n), ffn1_norm, m_ffn1_norm, v_ffn1_norm),
             ("mix_norm", rows(d_mixn), mix_norm, m_mix_norm, v_mix_norm), ("ffn2_norm", rows(d_ffn2n), ffn2_norm, m_ffn2_norm, v_ffn2_norm),
             ("final_norm", rows(d_final), final_norm, m_final_norm, v_final_norm), ("sg_ln_g", rows(d_lng), sg_ln_g, m_sg_ln_g, v_sg_ln_g),
             ("sg_ln_b", rows(d_lnb), sg_ln_b, m_sg_ln_b, v_sg_ln_b), ("sg_b", d_sgbT[:, :SG_GROUPS].T, sg_b, m_sg_b, v_sg_b)]
    pad8 = lambda a: _pad_rows(a, -(-a.shape[0] // 8) * 8)
    gpack = jnp.concatenate([pad8(g) for _, g, _, _, _ in small] + [pad8(loss_part)], axis=0)
    gsum = _small_allreduce(gpack, "allreduce_small")
    zero8 = jnp.zeros((8, 128), F32)
    wpack = jnp.concatenate([pad8(rows(w)) for _, _, w, _, _ in small] + [zero8], axis=0)
    mpack = jnp.concatenate([pad8(rows(m)) for _, _, _, m, _ in small] + [zero8], axis=0)
    vpack = jnp.concatenate([pad8(rows(v)) for _, _, _, _, v in small] + [zero8], axis=0)
    packs = _adamw(gsum[None], wpack, mpack, vpack, "adamw_small")
    off = 0
    for n, g, w, _, _ in small:
        r = g.shape[0]
        res[n] = [p[off:off + r].reshape(w.shape) for p in packs]
        off += -(-r // 8) * 8
    loss = gsum[off, 0]

    order = ["ffn1_norm", "ffn1_w_gate", "ffn1_w_up", "ffn1_w_down", "mix_norm", "w_in", "sg_ln_g", "sg_ln_b", "sg_w", "sg_b",
             "w_att_out", "w_sg_out", "w_out", "ffn2_norm", "ffn2_w_gate", "ffn2_w_up", "ffn2_w_down", "final_norm"]
    return (loss, dx0[None], *[res[n][0] for n in order], *[res[n][1] for n in order], *[res[n][2] for n in order],
            *[res[n][3] for n in order])
```

```python
import math

import jax
import jax.numpy as jnp
from jax import lax
from jax.experimental import pallas as pl
from jax.experimental.pallas import tpu as pltpu

BF = jnp.bfloat16
F32 = jnp.float32
MESH = pl.DeviceIdType.MESH
N_DEV = 8
N_CHIP = 4

HEAD_DIM = 128
HEADS_PER_GROUP = 4
GROUP_W = HEADS_PER_GROUP * HEAD_DIM
DILATIONS = (1, 4, 16)
ATT_W = len(DILATIONS) * GROUP_W
SG_W = 1536
SG_GROUPS = 12
BLK = 128
ROPE_DIM = 32
ROPE_THETA = 500000.0
NORM_EPS = 1e-6
LN_EPS = 1e-5
Q_OFF, K_OFF, V_OFF, U_OFF, VS_OFF, GA_OFF = 0, ATT_W, 2 * ATT_W, 3 * ATT_W, 3 * ATT_W + SG_W, 3 * ATT_W + 2 * SG_W

ADAM_LR, ADAM_B1, ADAM_B2, ADAM_EPS, ADAM_WD, ADAM_STEP = 0.001, 0.9, 0.999, 1e-08, 0.01, 10

VMEM_LIMIT = 56 * 1024 * 1024
NEG = -1e30
ANY = pl.BlockSpec(memory_space=pl.ANY)
EPI_ROWS = 128
ACC_COLS = 512
FFN_PAIR = 2
PROJ_TK = 1536
R_U, R_VS, R_GA = 0, SG_W, 2 * SG_W


def _once(shape, index_map):
    return pl.BlockSpec(shape, index_map, pipeline_mode=pl.Buffered(1))


def _tile(n, pref):
    t = min(n, pref)
    while n % t:
        t //= 2
    return t


def _nt(a, b):
    return lax.dot_general(a, b, (((1,), (1,)), ((), ())), preferred_element_type=F32)


def _tn(a, b):
    return lax.dot_general(a, b, (((0,), (0,)), ((), ())), preferred_element_type=F32)


def _nn(a, b):
    return jnp.dot(a, b, preferred_element_type=F32)


def _acc_dots(acc_ref, terms, transposed_rhs=False):
    n = acc_ref.shape[1]
    width = min(n, ACC_COLS)
    for c0 in range(0, n, width):
        cols = slice(c0, c0 + width)
        tot = None
        for lhs, rhs in terms:
            part = _nt(lhs, rhs(cols)) if transposed_rhs else _nn(lhs, rhs(cols))
            tot = part if tot is None else tot + part
        acc_ref[:, cols] += tot


def _gelu(x):
    return 0.5 * x * (1.0 + lax.erf(x * (2.0 ** -0.5)))


def _gelu_grad(x):
    return 0.5 * (1.0 + lax.erf(x * (2.0 ** -0.5))) + x * jnp.exp(-0.5 * x * x) * (1.0 / math.sqrt(2.0 * math.pi))


def _place():
    x, y, c = lax.axis_index("x"), lax.axis_index("y"), lax.axis_index("c")
    return x, y, c


def _flip(v, bit):
    return 1 - v if bit else v


class _Gather:
    def __init__(self, shards, mid_frac=1.0):
        self.arrays = list(shards)
        self.mid_frac = mid_frac
        nw = len(shards)
        self.out_shape = [jax.ShapeDtypeStruct((N_DEV,) + s.shape, s.dtype) for s in shards]
        self.scratch = [pltpu.SemaphoreType.DMA((nw, 7)), pltpu.SemaphoreType.DMA((nw, 7)), pltpu.SemaphoreType.DMA((nw,))]

    def _parts(self, ins, outs, sems):
        x, y, c = _place()
        send, recv, loc = sems
        chips = [(1 - x, y), (x, 1 - y), (1 - x, 1 - y)]

        def copy(k, s, block, to, src=None):
            dst = outs[k].at[4 * block[0] + 2 * block[1] + block[2]]
            return pltpu.make_async_remote_copy(src_ref=dst if src is None else src, dst_ref=dst, send_sem=send.at[k, s],
                                                recv_sem=recv.at[k, s], device_id=to, device_id_type=MESH)

        def first(k):
            return [copy(k, 0, (x, y, c), (x, y, 1 - c), src=ins[k])] + [
                copy(k, 1 + j, (x, y, c), (*chip, c), src=ins[k]) for j, chip in enumerate(chips)]

        def local(k):
            return pltpu.make_async_copy(ins[k], outs[k].at[4 * x + 2 * y + c], loc.at[k])

        return x, y, c, chips, copy, first, local

    def start(self, ins, outs, sems):
        _, _, _, _, _, first, local = self._parts(ins, outs, sems)
        for k in range(len(ins)):
            local(k).start()
            for cp in first(k):
                cp.start()

    def mid(self, ins, outs, sems):
        x, y, c, chips, copy, _, _ = self._parts(ins, outs, sems)
        for k in range(len(ins)):
            for j, chip in enumerate(chips):
                copy(k, 1 + j, (*chip, c), (x, y, c)).wait_recv()
                copy(k, 4 + j, (*chip, c), (x, y, 1 - c)).start()

    def finish(self, ins, outs, sems):
        x, y, c, chips, copy, first, local = self._parts(ins, outs, sems)
        for k in range(len(ins)):
            copy(k, 0, (x, y, 1 - c), (x, y, c)).wait_recv()
            for j, chip in enumerate(chips):
                copy(k, 4 + j, (*chip, 1 - c), (x, y, c)).wait_recv()
        for k in range(len(ins)):
            for cp in first(k):
                cp.wait_send()
            for j, chip in enumerate(chips):
                copy(k, 4 + j, (*chip, c), (x, y, 1 - c)).wait_send()
            local(k).wait()


class _Swap:
    def __init__(self, parts):
        self.arrays = list(parts)
        nw = len(parts)
        self.out_shape = [jax.ShapeDtypeStruct((N_CHIP,) + p.shape[1:], p.dtype) for p in parts]
        self.scratch = [pltpu.SemaphoreType.DMA((nw, N_CHIP)), pltpu.SemaphoreType.DMA((nw, N_CHIP))]

    def _copy(self, ins, outs, sems, k, q):
        x, y, c = _place()
        return pltpu.make_async_remote_copy(src_ref=ins[k].at[2 * q + 1 - c], dst_ref=outs[k].at[q], send_sem=sems[0].at[k, q],
                                            recv_sem=sems[1].at[k, q], device_id=(x, y, 1 - c), device_id_type=MESH)

    mid_frac = None

    def start(self, ins, outs, sems):
        for k in range(len(ins)):
            for q in range(N_CHIP):
                self._copy(ins, outs, sems, k, q).start()

    def finish(self, ins, outs, sems):
        for k in range(len(ins)):
            for q in range(N_CHIP):
                self._copy(ins, outs, sems, k, q).wait()


class _Ici:
    mid_frac = None

    def __init__(self, sums):
        self.arrays = list(sums)
        nw = len(sums)
        self.out_shape = [jax.ShapeDtypeStruct(s.shape, s.dtype) for s in sums]
        self.scratch = [pltpu.SemaphoreType.DMA((nw, 3)), pltpu.SemaphoreType.DMA((nw, 3)), pltpu.SemaphoreType.DMA((nw,))]

    def _copies(self, ins, outs, sems, k):
        x, y, c = _place()
        myq = 2 * x + y
        out = []
        for r in range(1, N_CHIP):
            px, py = _flip(x, r & 2), _flip(y, r & 1)
            pq = 2 * px + py
            mk = lambda dst: pltpu.make_async_remote_copy(src_ref=ins[k].at[pq], dst_ref=dst, send_sem=sems[0].at[k, r - 1],
                                                          recv_sem=sems[1].at[k, r - 1], device_id=(px, py, c), device_id_type=MESH)
            out.append((mk(outs[k].at[myq]), mk(outs[k].at[pq])))
        return out, pltpu.make_async_copy(ins[k].at[myq], outs[k].at[myq], sems[2].at[k])

    def start(self, ins, outs, sems):
        for k in range(len(ins)):
            remote, local = self._copies(ins, outs, sems, k)
            local.start()
            for snd, _ in remote:
                snd.start()

    def finish(self, ins, outs, sems):
        for k in range(len(ins)):
            remote, local = self._copies(ins, outs, sems, k)
            for snd, rcv in remote:
                rcv.wait_recv()
                snd.wait_send()
            local.wait()


def _call(body, *, grid, in_specs, out_specs, out_shape, name, args, scratch=(), comm=()):
    comm = list(comm)
    n_in, n_out, n_scr = len(in_specs), len(out_specs), len(scratch)
    total = math.prod(grid) if grid else 1

    def wrapped(*refs):
        p = n_in
        cin = []
        for cm in comm:
            cin.append(refs[p:p + len(cm.arrays)])
            p += len(cm.arrays)
        own_out = refs[p:p + n_out]
        p += n_out
        cout = []
        for cm in comm:
            cout.append(refs[p:p + len(cm.arrays)])
            p += len(cm.arrays)
        own_scr = refs[p:p + n_scr]
        p += n_scr
        csem = []
        for cm in comm:
            csem.append(refs[p:p + len(cm.scratch)])
            p += len(cm.scratch)
        step = 0
        for axis, g in enumerate(grid):
            step = step * g + pl.program_id(axis)

        def at(when, what):
            if total == 1:
                what()
            else:
                pl.when(step == when)(what)

        def starts():
            for cm, i, o, s in zip(comm, cin, cout, csem):
                cm.start(i, o, s)

        def finishes():
            for cm, i, o, s in zip(comm, cin, cout, csem):
                cm.finish(i, o, s)

        if comm:
            at(0, starts)
        if body is not None:
            body(*refs[:n_in], *own_out, *own_scr)
        for cm, i, o, s in zip(comm, cin, cout, csem):
            if cm.mid_frac is not None:
                at(min(total - 1, int(total * cm.mid_frac)), lambda cm=cm, i=i, o=o, s=s: cm.mid(i, o, s))
        if comm:
            at(total - 1, finishes)

    kw = dict(grid=tuple(grid)) if grid else {}
    outs = pl.pallas_call(
        wrapped, name=name, **kw,
        in_specs=list(in_specs) + [ANY for cm in comm for _ in cm.arrays],
        out_specs=list(out_specs) + [ANY for cm in comm for _ in cm.arrays],
        out_shape=list(out_shape) + [s for cm in comm for s in cm.out_shape],
        scratch_shapes=list(scratch) + [s for cm in comm for s in cm.scratch],
        compiler_params=pltpu.CompilerParams(dimension_semantics=("arbitrary",) * len(grid), vmem_limit_bytes=VMEM_LIMIT),
    )(*args, *[a for cm in comm for a in cm.arrays])
    own, p, per = list(outs[:n_out]), n_out, []
    for cm in comm:
        per.append(list(outs[p:p + len(cm.arrays)]))
        p += len(cm.arrays)
    return own, per


def _comm_only(cm, name):
    return _call(None, grid=(), in_specs=[], out_specs=[], out_shape=[], name=name, args=[], comm=[cm])[1][0]


def _rms_fwd(x, g, name):
    S, D = x.shape
    tm = _tile(S, 512)

    def body(x_ref, g_ref, o_ref):
        xv = x_ref[...]
        r = lax.rsqrt(jnp.mean(xv * xv, axis=-1, keepdims=True) + NORM_EPS)
        o_ref[...] = (xv * r * g_ref[...]).astype(BF)

    return _call(body, grid=(S // tm,), name=name, args=[x, g],
                 in_specs=[pl.BlockSpec((tm, D), lambda i: (i, 0)), pl.BlockSpec((1, D), lambda i: (0, 0))],
                 out_specs=[pl.BlockSpec((tm, D), lambda i: (i, 0))], out_shape=[jax.ShapeDtypeStruct((S, D), BF)])[0][0]


def _ffn_up(h, wg, wu, name, comm=()):
    S, D = h.shape
    nb, _, Fb = wg.shape
    tm = _tile(S, 512)

    def body(h_ref, wg_ref, wu_ref, g_ref, u_ref, a_ref):
        hv = h_ref[...]
        g = _nn(hv, wg_ref[0])
        u = _nn(hv, wu_ref[0])
        g_ref[0] = g.astype(BF)
        u_ref[0] = u.astype(BF)
        a_ref[0] = (g * jax.nn.sigmoid(g) * u).astype(BF)

    act = pl.BlockSpec((1, tm, Fb), lambda j, i: (j, i, 0))
    w = pl.BlockSpec((1, D, Fb), lambda j, i: (j, 0, 0))
    shp = jax.ShapeDtypeStruct((nb, S, Fb), BF)
    return _call(body, grid=(nb, S // tm), name=name, args=[h, wg, wu], comm=comm,
                 in_specs=[pl.BlockSpec((tm, D), lambda j, i: (i, 0)), w, w], out_specs=[act, act, act], out_shape=[shp, shp, shp])


def _ffn_down_norm(a, wd, x, gn, name, comm=()):
    nb, S, Fb = a.shape
    D = wd.shape[2]
    tm = _tile(S, 512)

    nj = nb // FFN_PAIR

    def body(a_ref, wd_ref, x_ref, gn_ref, xo_ref, hn_ref, acc_ref):
        j = pl.program_id(1)

        @pl.when(j == 0)
        def _():
            acc_ref[...] = jnp.zeros_like(acc_ref)

        _acc_dots(acc_ref, [(a_ref[b], lambda cols, b=b: wd_ref[b, :, cols]) for b in range(FFN_PAIR)])

        @pl.when(j == nj - 1)
        def _():
            def chunk(t, carry):
                rows = pl.ds(pl.multiple_of(t * EPI_ROWS, EPI_ROWS), EPI_ROWS)
                xo = x_ref[rows, :] + 0.5 * acc_ref[rows, :]
                r = lax.rsqrt(jnp.mean(xo * xo, axis=-1, keepdims=True) + NORM_EPS)
                xo_ref[rows, :] = xo
                hn_ref[rows, :] = (xo * r * gn_ref[...]).astype(BF)
                return carry

            lax.fori_loop(0, tm // EPI_ROWS, chunk, 0)

    row = pl.BlockSpec((tm, D), lambda i, j: (i, 0))
    return _call(body, grid=(S // tm, nj), name=name, args=[a, wd, x, gn], comm=comm,
                 in_specs=[pl.BlockSpec((FFN_PAIR, tm, Fb), lambda i, j: (j, i, 0)), pl.BlockSpec((FFN_PAIR, Fb, D), lambda i, j: (j, 0, 0)),
                           _once((tm, D), lambda i, j: (i, 0)), pl.BlockSpec((1, D), lambda i, j: (0, 0))],
                 out_specs=[row, row], out_shape=[jax.ShapeDtypeStruct((S, D), F32), jax.ShapeDtypeStruct((S, D), BF)],
                 scratch=[pltpu.VMEM((tm, D), F32)])


def _ffn_down_loss(a, wd, x, gf, tgt, name):
    nb, S, Fb = a.shape
    D = wd.shape[2]
    tm = _tile(S, 512)

    nj = nb // FFN_PAIR

    def body(a_ref, wd_ref, x_ref, gf_ref, t_ref, dx_ref, dxb_ref, dgf_ref, loss_ref, acc_ref):
        i, j = pl.program_id(0), pl.program_id(1)

        @pl.when(j == 0)
        def _():
            acc_ref[...] = jnp.zeros_like(acc_ref)

        _acc_dots(acc_ref, [(a_ref[b], lambda cols, b=b: wd_ref[b, :, cols]) for b in range(FFN_PAIR)])

        @pl.when((j == nj - 1) & (i == 0))
        def _():
            dgf_ref[...] = jnp.zeros_like(dgf_ref)
            loss_ref[...] = jnp.zeros_like(loss_ref)

        @pl.when(j == nj - 1)
        def _():
            def chunk(t, carry):
                rows = pl.ds(pl.multiple_of(t * EPI_ROWS, EPI_ROWS), EPI_ROWS)
                xo = x_ref[rows, :] + 0.5 * acc_ref[rows, :]
                r = lax.rsqrt(jnp.mean(xo * xo, axis=-1, keepdims=True) + NORM_EPS)
                xh = xo * r
                gf = gf_ref[...]
                e = xh * gf - t_ref[rows, :]
                loss_ref[...] += jnp.sum(jnp.mean(e * e, axis=-1, keepdims=True), axis=0, keepdims=True) * 0.5
                dy = e * (1.0 / D)
                dgf_ref[...] += jnp.sum(dy * xh, axis=0, keepdims=True)
                dxh = dy * gf
                dx = r * (dxh - xh * jnp.mean(dxh * xh, axis=-1, keepdims=True))
                dx_ref[rows, :] = dx
                dxb_ref[rows, :] = (0.5 * dx).astype(BF)
                return carry

            lax.fori_loop(0, tm // EPI_ROWS, chunk, 0)

    row = pl.BlockSpec((tm, D), lambda i, j: (i, 0))
    once = _once((tm, D), lambda i, j: (i, 0))
    vec = pl.BlockSpec((1, D), lambda i, j: (0, 0))
    return _call(body, grid=(S // tm, nj), name=name, args=[a, wd, x, gf, tgt],
                 in_specs=[pl.BlockSpec((FFN_PAIR, tm, Fb), lambda i, j: (j, i, 0)), pl.BlockSpec((FFN_PAIR, Fb, D), lambda i, j: (j, 0, 0)),
                           once, vec, once],
                 out_specs=[row, row, vec, pl.BlockSpec((1, 128), lambda i, j: (0, 0))],
                 out_shape=[jax.ShapeDtypeStruct((S, D), F32), jax.ShapeDtypeStruct((S, D), BF), jax.ShapeDtypeStruct((1, D), F32),
                            jax.ShapeDtypeStruct((1, 128), F32)],
                 scratch=[pltpu.VMEM((tm, D), F32)])[0]


def _ffn_bwd_act(dyb, wd, g, u, name, comm=()):
    S, D = dyb.shape
    nb, Fb, _ = wd.shape
    tm = _tile(S, 512)

    def body(dy_ref, wd_ref, g_ref, u_ref, dg_ref, du_ref):
        da = _nt(dy_ref[...], wd_ref[0])
        gv = g_ref[0].astype(F32)
        uv = u_ref[0].astype(F32)
        sg = jax.nn.sigmoid(gv)
        du_ref[0] = (da * gv * sg).astype(BF)
        dg_ref[0] = (da * uv * sg * (1.0 + gv * (1.0 - sg))).astype(BF)

    act = pl.BlockSpec((1, tm, Fb), lambda j, i: (j, i, 0))
    shp = jax.ShapeDtypeStruct((nb, S, Fb), BF)
    return _call(body, grid=(nb, S // tm), name=name, args=[dyb, wd, g, u], comm=comm,
                 in_specs=[pl.BlockSpec((tm, D), lambda j, i: (i, 0)), pl.BlockSpec((1, Fb, D), lambda j, i: (j, 0, 0)), act, act],
                 out_specs=[act, act], out_shape=[shp, shp])


def _ffn_dwd(a, dyb, name, comm=()):
    nb, S, Fb = a.shape
    D = dyb.shape[1]
    ts = _tile(S, 512)
    ns = S // ts

    def body(a_ref, dy_ref, o_ref, acc_ref):
        s = pl.program_id(1)

        @pl.when(s == 0)
        def _():
            acc_ref[...] = jnp.zeros_like(acc_ref)

        acc_ref[...] += _tn(a_ref[0], dy_ref[...])

        @pl.when(s == ns - 1)
        def _():
            o_ref[0] = acc_ref[...].astype(BF)

    return _call(body, grid=(nb, ns), name=name, args=[a, dyb], comm=comm,
                 in_specs=[pl.BlockSpec((1, ts, Fb), lambda j, s: (j, s, 0)), pl.BlockSpec((ts, D), lambda j, s: (s, 0))],
                 out_specs=[pl.BlockSpec((1, Fb, D), lambda j, s: (j, 0, 0))], out_shape=[jax.ShapeDtypeStruct((nb, Fb, D), BF)],
                 scratch=[pltpu.VMEM((Fb, D), F32)])


def _ffn_dwgu(h, dg, du, name, comm=()):
    S, D = h.shape
    nb, _, Fb = dg.shape
    ts = _tile(S, 512)
    ns = S // ts

    def body(h_ref, dg_ref, du_ref, og_ref, ou_ref, accg_ref, accu_ref):
        s = pl.program_id(1)

        @pl.when(s == 0)
        def _():
            accg_ref[...] = jnp.zeros_like(accg_ref)
            accu_ref[...] = jnp.zeros_like(accu_ref)

        hv = h_ref[...]
        accg_ref[...] += _tn(hv, dg_ref[0])
        accu_ref[...] += _tn(hv, du_ref[0])

        @pl.when(s == ns - 1)
        def _():
            og_ref[0] = accg_ref[...].astype(BF)
            ou_ref[0] = accu_ref[...].astype(BF)

    act = pl.BlockSpec((1, ts, Fb), lambda j, s: (j, s, 0))
    out = pl.BlockSpec((1, D, Fb), lambda j, s: (j, 0, 0))
    shp = jax.ShapeDtypeStruct((nb, D, Fb), BF)
    return _call(body, grid=(nb, ns), name=name, args=[h, dg, du], comm=comm,
                 in_specs=[pl.BlockSpec((ts, D), lambda j, s: (s, 0)), act, act], out_specs=[out, out], out_shape=[shp, shp],
                 scratch=[pltpu.VMEM((D, Fb), F32), pltpu.VMEM((D, Fb), F32)])


def _dh_rms_bwd(pairs, blocked, tk, x, gn, dxo, out_scale, name, comm=()):
    S, D = x.shape
    nk = pairs[0][0].shape[0] if blocked else pairs[0][0].shape[1] // tk
    tm = _tile(S, 512)
    npair = len(pairs)

    def body(*refs):
        ins = refs[: 2 * npair]
        x_ref, gn_ref, dxo_ref, dx_ref, dxb_ref, dgn_ref, acc_ref = refs[2 * npair:]
        i, k = pl.program_id(0), pl.program_id(1)

        @pl.when(k == 0)
        def _():
            acc_ref[...] = jnp.zeros_like(acc_ref)

        if blocked:
            terms = [(ins[2 * p][0], lambda cols, r=ins[2 * p + 1]: r[0, cols, :]) for p in range(npair)]
        else:
            terms = [(ins[2 * p][...], lambda cols, r=ins[2 * p + 1]: r[:, cols]) for p in range(npair)]
        _acc_dots(acc_ref, terms, transposed_rhs=blocked)

        @pl.when((k == nk - 1) & (i == 0))
        def _():
            dgn_ref[...] = jnp.zeros_like(dgn_ref)

        @pl.when(k == nk - 1)
        def _():
            def chunk(t, carry):
                rows = pl.ds(pl.multiple_of(t * EPI_ROWS, EPI_ROWS), EPI_ROWS)
                xv = x_ref[rows, :]
                r = lax.rsqrt(jnp.mean(xv * xv, axis=-1, keepdims=True) + NORM_EPS)
                xh = xv * r
                dh = acc_ref[rows, :]
                dgn_ref[...] += jnp.sum(dh * xh, axis=0, keepdims=True)
                dxh = dh * gn_ref[...]
                dx = dxo_ref[rows, :] + r * (dxh - xh * jnp.mean(dxh * xh, axis=-1, keepdims=True))
                dx_ref[rows, :] = dx
                dxb_ref[rows, :] = (out_scale * dx).astype(BF)
                return carry

            lax.fori_loop(0, tm // EPI_ROWS, chunk, 0)

    if blocked:
        lspec = pl.BlockSpec((1, tm, tk), lambda i, k: (k, i, 0))
        rspec = pl.BlockSpec((1, D, tk), lambda i, k: (k, 0, 0))
    else:
        lspec = pl.BlockSpec((tm, tk), lambda i, k: (i, k))
        rspec = pl.BlockSpec((tk, D), lambda i, k: (k, 0))
    row = pl.BlockSpec((tm, D), lambda i, k: (i, 0))
    once = _once((tm, D), lambda i, k: (i, 0))
    vec = pl.BlockSpec((1, D), lambda i, k: (0, 0))
    flat = [t for pr in pairs for t in pr]
    return _call(body, grid=(S // tm, nk), name=name, args=[*flat, x, gn, dxo], comm=comm,
                 in_specs=[lspec, rspec] * npair + [once, vec, once], out_specs=[row, row, vec],
                 out_shape=[jax.ShapeDtypeStruct((S, D), F32), jax.ShapeDtypeStruct((S, D), BF), jax.ShapeDtypeStruct((1, D), F32)],
                 scratch=[pltpu.VMEM((tm, D), F32)])


def _mm_nn(a, b, tm, tn, col0, col1, dtype, name, comm=()):
    M, K = a.shape
    n0, nn = col0 // tn, (col1 - col0) // tn

    def body(a_ref, b_ref, o_ref):
        o_ref[...] = _nn(a_ref[...], b_ref[...]).astype(dtype)

    return _call(body, grid=(nn, M // tm), name=name, args=[a, b], comm=comm,
                 in_specs=[pl.BlockSpec((tm, K), lambda n, i: (i, 0)), pl.BlockSpec((K, tn), lambda n, i: (0, n0 + n))],
                 out_specs=[pl.BlockSpec((tm, tn), lambda n, i: (i, n))], out_shape=[jax.ShapeDtypeStruct((M, nn * tn), dtype)])


def _mm_tn(a, b, tm, tn, ts, blocked, name, comm=(), ncols=None):
    S, M = a.shape
    N = b.shape[1] if ncols is None else ncols
    ns = S // ts

    def body(a_ref, b_ref, o_ref, acc_ref):
        s = pl.program_id(2)

        @pl.when(s == 0)
        def _():
            acc_ref[...] = jnp.zeros_like(acc_ref)

        acc_ref[...] += _tn(a_ref[...], b_ref[...])

        @pl.when(s == ns - 1)
        def _():
            if blocked:
                o_ref[0] = acc_ref[...].astype(BF)
            else:
                o_ref[...] = acc_ref[...].astype(BF)

    if blocked:
        ospec = pl.BlockSpec((1, tm, tn), lambda i, n, s: (n, i, 0))
        oshape = jax.ShapeDtypeStruct((N // tn, M, tn), BF)
    else:
        ospec = pl.BlockSpec((tm, tn), lambda i, n, s: (i, n))
        oshape = jax.ShapeDtypeStruct((M, N), BF)
    return _call(body, grid=(M // tm, N // tn, ns), name=name, args=[a, b], comm=comm,
                 in_specs=[pl.BlockSpec((ts, tm), lambda i, n, s: (s, i)), pl.BlockSpec((ts, tn), lambda i, n, s: (s, n))],
                 out_specs=[ospec], out_shape=[oshape], scratch=[pltpu.VMEM((tm, tn), F32)])


def _rope_tables(S):
    half = ROPE_DIM // 2
    inv_freq = ROPE_THETA ** (-jnp.arange(0, ROPE_DIM, 2, dtype=F32) / ROPE_DIM)
    ang = jnp.arange(S, dtype=F32)[:, None] * inv_freq[None, :]
    cos, sin = jnp.cos(ang), jnp.sin(ang)
    zeros = jnp.zeros((S, HEAD_DIM - ROPE_DIM), F32)
    c = jnp.concatenate([cos, cos, jnp.ones((S, HEAD_DIM - ROPE_DIM), F32)], axis=1)
    sm = jnp.concatenate([-sin, jnp.zeros((S, half), F32), zeros], axis=1)
    sp = jnp.concatenate([jnp.zeros((S, half), F32), sin, zeros], axis=1)
    return c, sm, sp


def _rope(t, c, sm, sp):
    return t * c + pltpu.roll(t, HEAD_DIM - ROPE_DIM // 2, 1) * sm + pltpu.roll(t, ROPE_DIM // 2, 1) * sp


def _rope_t(dy, c, sm, sp):
    return dy * c + pltpu.roll(dy * sm, ROPE_DIM // 2, 1) + pltpu.roll(dy * sp, HEAD_DIM - ROPE_DIM // 2, 1)


def _att_mask(i):
    qi = lax.broadcasted_iota(jnp.int32, (BLK, 2 * BLK), 0)
    kj = lax.broadcasted_iota(jnp.int32, (BLK, 2 * BLK), 1)
    diff = qi + BLK - kj
    first_key = jnp.where(i > 0, 0, BLK)
    return (diff >= 0) & (diff <= BLK) & (kj >= first_key)


def _res_rows(r, i, n, d):
    if d == 1:
        return pl.ds(pl.multiple_of(i * n, n), n)
    return pl.ds(r + i * (n * d), n, stride=d)


def _att_specs(S, gi):
    def sect(off):
        base = (off + gi * GROUP_W) // HEAD_DIM
        return _once((S, HEAD_DIM), lambda hh, r: (0, base + hh))

    tab = pl.BlockSpec((S, HEAD_DIM), lambda hh, r: (0, 0))
    head = pl.BlockSpec((S, HEAD_DIM), lambda hh, r: (0, hh))
    return sect, tab, head


def _att_fwd(qkv, tabs, gi, d, name):
    S = qkv.shape[0]
    L = S // d
    sect, tab, head = _att_specs(S, gi)
    nblk = L // BLK
    scale = HEAD_DIM ** -0.5

    def body(q_ref, k_ref, v_ref, c_ref, sm_ref, sp_ref, o_ref, lse_ref, qr, kp, vp):
        r = pl.program_id(1)
        res = _res_rows(r, 0, L, d)
        c, sm, sp = c_ref[res, :], sm_ref[res, :], sp_ref[res, :]
        qr[...] = _rope(q_ref[res, :], c, sm, sp).astype(BF)
        kp[pl.ds(0, BLK), :] = jnp.zeros((BLK, HEAD_DIM), BF)
        vp[pl.ds(0, BLK), :] = jnp.zeros((BLK, HEAD_DIM), BF)
        kp[pl.ds(BLK, L), :] = _rope(k_ref[res, :], c, sm, sp).astype(BF)
        vp[pl.ds(BLK, L), :] = v_ref[res, :].astype(BF)

        def blk(i, carry):
            r0 = pl.multiple_of(i * BLK, BLK)
            s = _nt(qr[pl.ds(r0, BLK), :], kp[pl.ds(r0, 2 * BLK), :]) * scale
            s = jnp.where(_att_mask(i), s, NEG)
            m = jnp.max(s, axis=-1, keepdims=True)
            p = jnp.exp(s - m)
            l = jnp.sum(p, axis=-1, keepdims=True)
            out = _res_rows(r, i, BLK, d)
            o_ref[out, :] = _nn(p.astype(BF), vp[pl.ds(r0, 2 * BLK), :]) / l
            lse_ref[out, :] = jnp.broadcast_to(m + jnp.log(l), (BLK, HEAD_DIM))
            return carry

        lax.fori_loop(0, nblk, blk, 0, unroll=min(4, nblk))

    shp = jax.ShapeDtypeStruct((S, GROUP_W), F32)
    return _call(body, grid=(HEADS_PER_GROUP, d), name=name, args=[qkv, qkv, qkv, *tabs],
                 in_specs=[sect(Q_OFF), sect(K_OFF), sect(V_OFF), tab, tab, tab], out_specs=[head, head], out_shape=[shp, shp],
                 scratch=[pltpu.VMEM((L, HEAD_DIM), BF), pltpu.VMEM((L + BLK, HEAD_DIM), BF), pltpu.VMEM((L + BLK, HEAD_DIM), BF)])[0]


def _att_combine(os, lses, name):
    S = os[0].shape[0]
    tm = _tile(S, 512)

    def body(o0, o1, o2, l0, l1, l2, oa_ref, lse_ref):
        a, b, c = l0[...], l1[...], l2[...]
        mx = jnp.maximum(jnp.maximum(a, b), c)
        wa, wb, wc = jnp.exp(a - mx), jnp.exp(b - mx), jnp.exp(c - mx)
        den = wa + wb + wc
        oa_ref[...] = ((wa * o0[...] + wb * o1[...] + wc * o2[...]) / den).astype(BF)
        lse_ref[...] = mx + jnp.log(den)

    row = pl.BlockSpec((tm, GROUP_W), lambda i: (i, 0))
    return _call(body, grid=(S // tm,), name=name, args=[*os, *lses], in_specs=[row] * 6, out_specs=[row, row],
                 out_shape=[jax.ShapeDtypeStruct((S, GROUP_W), BF), jax.ShapeDtypeStruct((S, GROUP_W), F32)])[0]


def _att_bwd(qkv, tabs, do, lse, dvec, gi, d, name, comm=()):
    S = qkv.shape[0]
    L = S // d
    sect, tab, head = _att_specs(S, gi)
    stat = _once((S, HEAD_DIM), lambda hh, r: (0, hh))
    nblk = L // BLK
    scale = HEAD_DIM ** -0.5

    def body(q_ref, k_ref, v_ref, c_ref, sm_ref, sp_ref, do_ref, lse_ref, dv_ref, dq_out, dk_out, dv_out, qr, kp, vp, dkp, dvp):
        r = pl.program_id(1)
        res = _res_rows(r, 0, L, d)
        c, sm, sp = c_ref[res, :], sm_ref[res, :], sp_ref[res, :]
        qr[...] = _rope(q_ref[res, :], c, sm, sp).astype(BF)
        kp[pl.ds(0, BLK), :] = jnp.zeros((BLK, HEAD_DIM), BF)
        vp[pl.ds(0, BLK), :] = jnp.zeros((BLK, HEAD_DIM), BF)
        kp[pl.ds(BLK, L), :] = _rope(k_ref[res, :], c, sm, sp).astype(BF)
        vp[pl.ds(BLK, L), :] = v_ref[res, :].astype(BF)
        dkp[...] = jnp.zeros_like(dkp)
        dvp[...] = jnp.zeros_like(dvp)

        def blk(i, carry):
            r0 = pl.multiple_of(i * BLK, BLK)
            rows, win, pos = pl.ds(r0, BLK), pl.ds(r0, 2 * BLK), _res_rows(r, i, BLK, d)
            q, kw, vw, dob = qr[rows, :], kp[win, :], vp[win, :], do_ref[pos, :].astype(BF)
            s = jnp.where(_att_mask(i), _nt(q, kw) * scale, NEG)
            p = jnp.exp(s - lse_ref[pos, :][:, :1])
            ds = p * (_nt(dob, vw) - dv_ref[pos, :][:, :1]) * scale
            dsb = ds.astype(BF)
            dq_out[pos, :] = _rope_t(_nn(dsb, kw), c_ref[pos, :], sm_ref[pos, :], sp_ref[pos, :])
            dkp[win, :] += _tn(dsb, q)
            dvp[win, :] += _tn(p.astype(BF), dob)
            return carry

        lax.fori_loop(0, nblk, blk, 0, unroll=2)
        dk_out[res, :] = _rope_t(dkp[pl.ds(BLK, L), :], c, sm, sp)
        dv_out[res, :] = dvp[pl.ds(BLK, L), :]

    shp = jax.ShapeDtypeStruct((S, GROUP_W), F32)
    return _call(body, grid=(HEADS_PER_GROUP, d), name=name, args=[qkv, qkv, qkv, *tabs, do, lse, dvec], comm=comm,
                 in_specs=[sect(Q_OFF), sect(K_OFF), sect(V_OFF), tab, tab, tab, stat, stat, stat],
                 out_specs=[head, head, head], out_shape=[shp, shp, shp],
                 scratch=[pltpu.VMEM((L, HEAD_DIM), BF), pltpu.VMEM((L + BLK, HEAD_DIM), BF), pltpu.VMEM((L + BLK, HEAD_DIM), BF),
                          pltpu.VMEM((L + BLK, HEAD_DIM), F32), pltpu.VMEM((L + BLK, HEAD_DIM), F32)])


def _sg_parts(u_ref, vs_ref, g_ref, b_ref):
    uv = u_ref[...].astype(F32)
    vv = vs_ref[...].astype(F32)
    vg = _gelu(vv)
    mu = jnp.mean(vg, axis=-1, keepdims=True)
    vc = vg - mu
    rs = lax.rsqrt(jnp.mean(vc * vc, axis=-1, keepdims=True) + LN_EPS)
    y = vc * rs
    return uv, vv, rs, y, y * g_ref[...] + b_ref[...]


def _sg_wmask():
    t = lax.broadcasted_iota(jnp.int32, (BLK, BLK), 0)
    s = lax.broadcasted_iota(jnp.int32, (BLK, BLK), 1)
    return s <= t


def _sg_fwd(proj, sgw, sgbT, lng, lnb, name):
    S, P = proj.shape

    def body(u_ref, vs_ref, w_ref, bt_ref, g_ref, b_ref, z_ref):
        uv, _, _, _, vln = _sg_parts(u_ref, vs_ref, g_ref, b_ref)
        ug = _gelu(uv)
        vb = vln.astype(BF)
        mask = _sg_wmask()
        bt = bt_ref[...]
        for g in range(SG_GROUPS):
            cols = slice(g * BLK, (g + 1) * BLK)
            w = jnp.where(mask, w_ref[g], 0.0).astype(BF)
            sp = _nn(w, vb[:, cols]) + bt[:, g:g + 1]
            z_ref[:, cols] = (ug[:, cols] * sp).astype(BF)

    tile = lambda off: pl.BlockSpec((BLK, SG_W), lambda i: (i, off // SG_W))
    full = lambda shape: pl.BlockSpec(shape, lambda i: (0,) * len(shape))
    return _call(body, grid=(S // BLK,), name=name, args=[proj, proj, sgw, sgbT, lng, lnb],
                 in_specs=[tile(R_U), tile(R_VS), full((SG_GROUPS, BLK, BLK)), full((BLK, BLK)), full((1, SG_W)), full((1, SG_W))],
                 out_specs=[pl.BlockSpec((BLK, SG_W), lambda i: (i, 0))], out_shape=[jax.ShapeDtypeStruct((S, SG_W), BF)])[0][0]


def _sg_bwd(proj, dz, sgw, sgbT, lng, lnb, name):
    S, P = proj.shape

    def body(u_ref, vs_ref, dz_ref, w_ref, bt_ref, g_ref, b_ref, du_ref, dvs_ref, dw_ref, dbt_ref, dg_ref, db_ref, dvln):
        @pl.when(pl.program_id(0) == 0)
        def _():
            dw_ref[...] = jnp.zeros_like(dw_ref)
            dbt_ref[...] = jnp.zeros_like(dbt_ref)
            dg_ref[...] = jnp.zeros_like(dg_ref)
            db_ref[...] = jnp.zeros_like(db_ref)

        uv, vv, rs, y, vln = _sg_parts(u_ref, vs_ref, g_ref, b_ref)
        ug = _gelu(uv)
        vb = vln.astype(BF)
        dzv = dz_ref[...].astype(F32)
        dsp = dzv * ug
        dspb = dsp.astype(BF)
        mask = _sg_wmask()
        bt = bt_ref[...]
        lane = lax.broadcasted_iota(jnp.int32, (BLK, BLK), 1)
        dbt = jnp.zeros((BLK, BLK), F32)
        for g in range(SG_GROUPS):
            cols = slice(g * BLK, (g + 1) * BLK)
            w = jnp.where(mask, w_ref[g], 0.0).astype(BF)
            sp = _nn(w, vb[:, cols]) + bt[:, g:g + 1]
            du_ref[:, cols] = (dzv[:, cols] * sp * _gelu_grad(uv[:, cols])).astype(BF)
            dw_ref[g] += jnp.where(mask, _nt(dspb[:, cols], vb[:, cols]), 0.0)
            dbt = dbt + jnp.where(lane == g, jnp.sum(dsp[:, cols], axis=-1, keepdims=True), 0.0)
            dvln[:, cols] = _tn(w, dspb[:, cols])
        dbt_ref[...] += dbt
        dvl = dvln[...]
        dg_ref[...] += jnp.sum(dvl * y, axis=0, keepdims=True)
        db_ref[...] += jnp.sum(dvl, axis=0, keepdims=True)
        dy = dvl * g_ref[...]
        dvg = rs * (dy - jnp.mean(dy, axis=-1, keepdims=True) - y * jnp.mean(dy * y, axis=-1, keepdims=True))
        dvs_ref[...] = (dvg * _gelu_grad(vv)).astype(BF)

    tile = lambda off: pl.BlockSpec((BLK, SG_W), lambda i: (i, off // SG_W))
    full = lambda shape: pl.BlockSpec(shape, lambda i: (0,) * len(shape))
    row = pl.BlockSpec((BLK, SG_W), lambda i: (i, 0))
    return _call(body, grid=(S // BLK,), name=name, args=[proj, proj, dz, sgw, sgbT, lng, lnb],
                 in_specs=[tile(R_U), tile(R_VS), row, full((SG_GROUPS, BLK, BLK)), full((BLK, BLK)), full((1, SG_W)), full((1, SG_W))],
                 out_specs=[row, row, full((SG_GROUPS, BLK, BLK)), full((BLK, BLK)), full((1, SG_W)), full((1, SG_W))],
                 out_shape=[jax.ShapeDtypeStruct((S, SG_W), BF), jax.ShapeDtypeStruct((S, SG_W), BF),
                            jax.ShapeDtypeStruct((SG_GROUPS, BLK, BLK), F32), jax.ShapeDtypeStruct((BLK, BLK), F32),
                            jax.ShapeDtypeStruct((1, SG_W), F32), jax.ShapeDtypeStruct((1, SG_W), F32)],
                 scratch=[pltpu.VMEM((BLK, SG_W), F32)])[0]


def _gate_merge(oatt, z, watt, wsg, proj, name, comm=()):
    S = oatt.shape[0]
    nb, _, Db = watt.shape
    D = nb * Db
    tm = _tile(S, 512)
    ga, gs = R_GA // Db, (R_GA + D) // Db

    def body(oa_ref, z_ref, wa_ref, ws_ref, ga_ref, gs_ref, ya_ref, ys_ref, mg_ref):
        ya = _nn(oa_ref[...], wa_ref[0])
        ys = _nn(z_ref[...], ws_ref[0])
        ya_ref[...] = ya.astype(BF)
        ys_ref[...] = ys.astype(BF)
        mg_ref[...] = (jax.nn.sigmoid(ga_ref[...].astype(F32)) * ya + jax.nn.sigmoid(gs_ref[...].astype(F32)) * ys).astype(BF)

    out = pl.BlockSpec((tm, Db), lambda j, i: (i, j))
    shp = jax.ShapeDtypeStruct((S, D), BF)
    return _call(body, grid=(nb, S // tm), name=name, args=[oatt, z, watt, wsg, proj, proj], comm=comm,
                 in_specs=[pl.BlockSpec((tm, GROUP_W), lambda j, i: (i, 0)), pl.BlockSpec((tm, SG_W), lambda j, i: (i, 0)),
                           pl.BlockSpec((1, GROUP_W, Db), lambda j, i: (j, 0, 0)), pl.BlockSpec((1, SG_W, Db), lambda j, i: (j, 0, 0)),
                           pl.BlockSpec((tm, Db), lambda j, i: (i, ga + j)), pl.BlockSpec((tm, Db), lambda j, i: (i, gs + j))],
                 out_specs=[out, out, out], out_shape=[shp, shp, shp])


def _mix_out(merged, wout, x, gn, name):
    S, D = x.shape
    tm = _tile(S, 256)

    def body(m_ref, w_ref, x_ref, gn_ref, xo_ref, hn_ref):
        xo = x_ref[...] + _nn(m_ref[...], w_ref[...])
        r = lax.rsqrt(jnp.mean(xo * xo, axis=-1, keepdims=True) + NORM_EPS)
        xo_ref[...] = xo
        hn_ref[...] = (xo * r * gn_ref[...]).astype(BF)

    row = pl.BlockSpec((tm, D), lambda i: (i, 0))
    return _call(body, grid=(S // tm,), name=name, args=[merged, wout, x, gn],
                 in_specs=[row, pl.BlockSpec((D, D), lambda i: (0, 0)), row, pl.BlockSpec((1, D), lambda i: (0, 0))],
                 out_specs=[row, row], out_shape=[jax.ShapeDtypeStruct((S, D), F32), jax.ShapeDtypeStruct((S, D), BF)])[0]


def _mix_bwd_gate(dmix, wout, ya, ys, proj, name):
    S, D = dmix.shape
    tm, tn = _tile(S, 512), 512
    ga, gs = R_GA // tn, (R_GA + D) // tn

    def body(dm_ref, w_ref, ya_ref, ys_ref, ga_ref, gs_ref, dya_ref, dys_ref, dga_ref, dgs_ref):
        dm = _nt(dm_ref[...], w_ref[...])
        sa = jax.nn.sigmoid(ga_ref[...].astype(F32))
        ss = jax.nn.sigmoid(gs_ref[...].astype(F32))
        dya_ref[...] = (dm * sa).astype(BF)
        dys_ref[...] = (dm * ss).astype(BF)
        dga_ref[...] = (dm * ya_ref[...].astype(F32) * sa * (1.0 - sa)).astype(BF)
        dgs_ref[...] = (dm * ys_ref[...].astype(F32) * ss * (1.0 - ss)).astype(BF)

    out = pl.BlockSpec((tm, tn), lambda i, n: (i, n))
    shp = jax.ShapeDtypeStruct((S, D), BF)
    return _call(body, grid=(S // tm, D // tn), name=name, args=[dmix, wout, ya, ys, proj, proj],
                 in_specs=[pl.BlockSpec((tm, D), lambda i, n: (i, 0)), pl.BlockSpec((tn, D), lambda i, n: (n, 0)), out, out,
                           pl.BlockSpec((tm, tn), lambda i, n: (i, ga + n)), pl.BlockSpec((tm, tn), lambda i, n: (i, gs + n))],
                 out_specs=[out] * 4, out_shape=[shp] * 4)[0]


def _att_sg_dout(dya, dys, watt, wsg, oatt, name, comm=()):
    S, D = dya.shape
    nb, _, Db = watt.shape
    tm = _tile(S, 512)

    def body(dya_ref, dys_ref, wa_ref, ws_ref, oa_ref, do_ref, dz_ref, dvec_ref, acca, accs):
        j = pl.program_id(1)

        @pl.when(j == 0)
        def _():
            acca[...] = jnp.zeros_like(acca)
            accs[...] = jnp.zeros_like(accs)

        acca[...] += _nt(dya_ref[...], wa_ref[0])
        accs[...] += _nt(dys_ref[...], ws_ref[0])

        @pl.when(j == nb - 1)
        def _():
            dov = acca[...]
            do_ref[...] = dov
            dz_ref[...] = accs[...].astype(BF)
            prod = dov * oa_ref[...].astype(F32)
            for hh in range(HEADS_PER_GROUP):
                cols = slice(hh * HEAD_DIM, (hh + 1) * HEAD_DIM)
                dvec_ref[:, cols] = jnp.broadcast_to(jnp.sum(prod[:, cols], axis=-1, keepdims=True), (tm, HEAD_DIM))

    blk = pl.BlockSpec((tm, Db), lambda i, j: (i, j))
    att = pl.BlockSpec((tm, GROUP_W), lambda i, j: (i, 0))
    return _call(body, grid=(S // tm, nb), name=name, args=[dya, dys, watt, wsg, oatt], comm=comm,
                 in_specs=[blk, blk, pl.BlockSpec((1, GROUP_W, Db), lambda i, j: (j, 0, 0)), pl.BlockSpec((1, SG_W, Db), lambda i, j: (j, 0, 0)), att],
                 out_specs=[att, pl.BlockSpec((tm, SG_W), lambda i, j: (i, 0)), att],
                 out_shape=[jax.ShapeDtypeStruct((S, GROUP_W), F32), jax.ShapeDtypeStruct((S, SG_W), BF), jax.ShapeDtypeStruct((S, GROUP_W), F32)],
                 scratch=[pltpu.VMEM((tm, GROUP_W), F32), pltpu.VMEM((tm, SG_W), F32)])[0]


def _small_allreduce(pack, name):
    R = pack.shape[0]

    def body(p_ref, o_ref, gath, send, recv):
        x, y, c = _place()
        me = 4 * x + 2 * y + c
        gath[me] = p_ref[...]
        copies = []
        for r in range(1, N_DEV):
            px, py, pc = _flip(x, r & 4), _flip(y, r & 2), _flip(c, r & 1)
            peer = 4 * px + 2 * py + pc
            mk = lambda dst: pltpu.make_async_remote_copy(src_ref=p_ref, dst_ref=dst, send_sem=send.at[r - 1], recv_sem=recv.at[r - 1],
                                                          device_id=(px, py, pc), device_id_type=MESH)
            snd = mk(gath.at[me])
            snd.start()
            copies.append((snd, mk(gath.at[peer])))
        for snd, rcv in copies:
            rcv.wait_recv()
            snd.wait_send()
        acc = gath[0]
        for s in range(1, N_DEV):
            acc = acc + gath[s]
        o_ref[...] = acc

    vm = pl.BlockSpec(memory_space=pltpu.VMEM)
    return pl.pallas_call(
        body, name=name, in_specs=[vm], out_specs=vm, out_shape=jax.ShapeDtypeStruct(pack.shape, F32),
        scratch_shapes=[pltpu.VMEM((N_DEV, R, 128), F32), pltpu.SemaphoreType.DMA((7,)), pltpu.SemaphoreType.DMA((7,))],
        compiler_params=pltpu.CompilerParams(vmem_limit_bytes=VMEM_LIMIT),
    )(pack)


def _row_tile(R, C, elems=262144):
    tr = R
    while tr * C > elems and tr % 32 == 0:
        tr //= 2
    return tr


def _pair_add(parts, other, name):
    _, R, C = parts.shape
    tr = _row_tile(R, C, 1048576)

    def body(c_ref, p_ref, o_ref, s_ref):
        s_ref[0] = (p_ref[0].astype(F32) + o_ref[0].astype(F32)).astype(BF)

    core = lax.axis_index("c").astype(jnp.int32).reshape(1)
    return pl.pallas_call(
        body, name=name,
        grid_spec=pltpu.PrefetchScalarGridSpec(
            num_scalar_prefetch=1, grid=(N_CHIP, R // tr),
            in_specs=[pl.BlockSpec((1, tr, C), lambda q, i, c: (2 * q + c[0], i, 0)), pl.BlockSpec((1, tr, C), lambda q, i, c: (q, i, 0))],
            out_specs=pl.BlockSpec((1, tr, C), lambda q, i, c: (q, i, 0))),
        out_shape=jax.ShapeDtypeStruct((N_CHIP, R, C), BF),
        compiler_params=pltpu.CompilerParams(dimension_semantics=("arbitrary", "arbitrary"), vmem_limit_bytes=VMEM_LIMIT),
    )(core, parts, other)


def _adamw(parts, w, m, v, name):
    ns, R, C = parts.shape
    tr = _row_tile(R, C, 524288)
    c1 = 1.0 - ADAM_B1 ** ADAM_STEP
    c2 = 1.0 - ADAM_B2 ** ADAM_STEP

    def body(p_ref, w_ref, m_ref, v_ref, g_ref, d_ref, nm_ref, nv_ref):
        g = p_ref[0].astype(F32)
        for s in range(1, ns):
            g = g + p_ref[s].astype(F32)
        mn = ADAM_B1 * m_ref[...] + (1.0 - ADAM_B1) * g
        vn = ADAM_B2 * v_ref[...] + (1.0 - ADAM_B2) * (g * g)
        g_ref[...] = g
        nm_ref[...] = mn
        nv_ref[...] = vn
        d_ref[...] = -ADAM_LR * ((mn / c1) / (jnp.sqrt(vn / c2) + ADAM_EPS) + ADAM_WD * w_ref[...])

    row = pl.BlockSpec((tr, C), lambda i: (i, 0))
    shp = jax.ShapeDtypeStruct((R, C), F32)
    return _call(body, grid=(R // tr,), name=name, args=[parts, w, m, v],
                 in_specs=[pl.BlockSpec((ns, tr, C), lambda i: (0, i, 0)), row, row, row], out_specs=[row] * 4, out_shape=[shp] * 4)[0]


def _pad_rows(a, rows):
    return jnp.pad(a, ((0, rows - a.shape[0]), (0, 0)))


def kernel(x, ffn1_norm, ffn1_w_gate, ffn1_w_up, ffn1_w_down, mix_norm, w_in, sg_ln_g, sg_ln_b, sg_w, sg_b, w_att_out, w_sg_out, w_out, ffn2_norm, ffn2_w_gate, ffn2_w_up, ffn2_w_down, final_norm, loss_target, m_ffn1_norm, m_ffn1_w_gate, m_ffn1_w_up, m_ffn1_w_down, m_mix_norm, m_w_in, m_sg_ln_g, m_sg_ln_b, m_sg_w, m_sg_b, m_w_att_out, m_w_sg_out, m_w_out, m_ffn2_norm, m_ffn2_w_gate, m_ffn2_w_up, m_ffn2_w_down, m_final_norm, v_ffn1_norm, v_ffn1_w_gate, v_ffn1_w_up, v_ffn1_w_down, v_mix_norm, v_w_in, v_sg_ln_g, v_sg_ln_b, v_sg_w, v_sg_b, v_w_att_out, v_w_sg_out, v_w_out, v_ffn2_norm, v_ffn2_w_gate, v_ffn2_w_up, v_ffn2_w_down, v_final_norm):
    S, D = x.shape[1], x.shape[2]
    Pb = w_in.shape[2]
    P = N_DEV * Pb
    assert P == GA_OFF + 2 * D and D % (N_DEV * 128) == 0 and S % (BLK * DILATIONS[-1]) == 0
    xs, tgt = x[0], loss_target[0]

    sharded = dict(ffn1_w_gate=ffn1_w_gate, ffn1_w_up=ffn1_w_up, ffn1_w_down=ffn1_w_down, w_in=w_in, w_att_out=w_att_out,
                   w_sg_out=w_sg_out, w_out=w_out, ffn2_w_gate=ffn2_w_gate, ffn2_w_up=ffn2_w_up, ffn2_w_down=ffn2_w_down)
    sb = {n: w[0].astype(BF) for n, w in sharded.items()}

    wg1, wu1 = _comm_only(_Gather([sb["ffn1_w_gate"], sb["ffn1_w_up"]]), "gather_ffn1")
    h1 = _rms_fwd(xs, ffn1_norm, "rms1")
    win_top, win_bot = sb["w_in"][: D // 2], sb["w_in"][D // 2:]
    (g1, u1, a1), ((wd1, win8a),) = _ffn_up(h1, wg1, wu1, "ffn1_up", comm=[_Gather([sb["ffn1_w_down"], win_top])])
    (x1, h2), ((win8b, watt, wsg, wout8),) = _ffn_down_norm(
        a1, wd1, xs, mix_norm, "ffn1_down", comm=[_Gather([win_bot, sb["w_att_out"], sb["w_sg_out"], sb["w_out"]])])
    win = jnp.concatenate([w8.transpose(1, 0, 2).reshape(D // 2, P) for w8 in (win8a, win8b)], axis=0)
    wout = wout8.reshape(D, D)
    tm_proj = _tile(S, 1024)
    (qkv,), ((wg2,),) = _mm_nn(h2, win, tm_proj, 512, 0, U_OFF, F32, "proj_qkv", comm=[_Gather([sb["ffn2_w_gate"]])])
    (rest,), ((wu2,),) = _mm_nn(h2, win, tm_proj, 512, U_OFF, P, BF, "proj_rest", comm=[_Gather([sb["ffn2_w_up"]], mid_frac=0.9)])
    tabs = _rope_tables(S)
    os, lses = [], []
    for gi, d in enumerate(DILATIONS):
        o, l = _att_fwd(qkv, tabs, gi, d, f"att_fwd{gi}")
        os.append(o)
        lses.append(l)
    oatt, lse = _att_combine(os, lses, "att_combine")
    sgw = sg_w[0]
    sgbT = jnp.pad(sg_b[0].T, ((0, 0), (0, BLK - SG_GROUPS)))
    z = _sg_fwd(rest, sgw, sgbT, sg_ln_g, sg_ln_b, "sg_fwd")
    (ya, ys, merged), _ = _gate_merge(oatt, z, watt, wsg, rest, "gate_merge")
    x2, h3 = _mix_out(merged, wout, x1, ffn2_norm, "mix_out")
    (g3, u3, a3), ((wd2,),) = _ffn_up(h3, wg2, wu2, "ffn2_up", comm=[_Gather([sb["ffn2_w_down"]], mid_frac=0.7)])
    dx3, dyb3, d_final, loss_part = _ffn_down_loss(a3, wd2, x2, final_norm.reshape(1, D), tgt, "ffn2_down_loss")

    Fb = wg2.shape[2]
    Db = watt.shape[2]
    p_pad = -(-P // PROJ_TK) * PROJ_TK
    winT = jnp.concatenate([w8.transpose(0, 2, 1).reshape(P, D // 2) for w8 in (win8a, win8b)], axis=1)
    winT = jnp.pad(winT, ((0, p_pad - P), (0, 0)))
    (dg3, du3), _ = _ffn_bwd_act(dyb3, wd2, g3, u3, "ffn2_bwd_act")
    (dwd2,), _ = _ffn_dwd(a3, dyb3, "ffn2_dwd")
    (dwg2, dwu2), _ = _ffn_dwgu(h3, dg3, du3, "ffn2_dwgu")
    ffn2_parts = [dwd2, dwg2, dwu2]
    (dx2, dmixb, d_ffn2n), (ffn2_other,) = _dh_rms_bwd([(dg3, wg2), (du3, wu2)], True, Fb, x2, ffn2_norm, dx3, 1.0, "ffn2_dh",
                                                     comm=[_Swap(ffn2_parts)])
    ffn2_sums = [_pair_add(p, o, f"pair_ffn2_{i}") for i, (p, o) in enumerate(zip(ffn2_parts, ffn2_other))]

    dya, dys, dga, dgs = _mix_bwd_gate(dmixb, wout, ya, ys, rest, "mix_bwd_gate")
    (dwout,), _ = _mm_tn(merged, dmixb, _tile(D, 1024), _tile(D, 1024), _tile(S, 1024), False, "dw_out")
    do, dz, dvec = _att_sg_dout(dya, dys, watt, wsg, oatt, "att_sg_dout")
    (dwatt,), _ = _mm_tn(oatt, dya, GROUP_W, Db, _tile(S, 1024), True, "dw_att")
    (dwsg,), _ = _mm_tn(z, dys, SG_W, Db, _tile(S, 1024), True, "dw_sg")
    mix_parts = [dwout.reshape(N_DEV, D // N_DEV, D), dwatt, dwsg]
    du, dvs, d_sgw, d_sgbT, d_lng, d_lnb = _sg_bwd(rest, dz, sgw, sgbT, sg_ln_g, sg_ln_b, "sg_bwd")
    dqs, dks, dvs_att, ffn2_got = [], [], [], []
    for gi, d in enumerate(DILATIONS):
        ride = [_Ici([ffn2_sums[gi - 1]])] if gi else []
        (dq, dk, dv), got_here = _att_bwd(qkv, tabs, do, lse, dvec, gi, d, f"att_bwd{gi}", comm=ride)
        ffn2_got += [g[0] for g in got_here]
        dqs.append(dq)
        dks.append(dk)
        dvs_att.append(dv)
    dproj = jnp.concatenate([t.astype(BF) for t in dqs + dks + dvs_att] + [du, dvs, dga, dgs, jnp.zeros((S, p_pad - P), BF)], axis=1)
    (dx1, dyb1, d_mixn), ((ffn2_last,), mix_other) = _dh_rms_bwd([(dproj, winT)], False, PROJ_TK, x1, mix_norm, dx2, 0.5, "proj_dh",
                                                               comm=[_Ici([ffn2_sums[2]]), _Swap(mix_parts)])
    ffn2_got.append(ffn2_last)
    mix_sums = [_pair_add(p, o, f"pair_mix_{i}") for i, (p, o) in enumerate(zip(mix_parts, mix_other))]
    (dwd1,), (mix_got,) = _ffn_dwd(a1, dyb1, "ffn1_dwd", comm=[_Ici(mix_sums)])
    (dwin,), ((wd1_other,),) = _mm_tn(h2, dproj, D, 512, _tile(S, 1024), False, "dw_in", comm=[_Swap([dwd1])], ncols=P)
    dwin = dwin.reshape(D, N_DEV, Pb).transpose(1, 0, 2)
    wd1_sum = _pair_add(dwd1, wd1_other, "pair_wd1")
    (dg1, du1), ((wd1_got,), (win_other,)) = _ffn_bwd_act(dyb1, wd1, g1, u1, "ffn1_bwd_act", comm=[_Ici([wd1_sum]), _Swap([dwin])])
    win_sum = _pair_add(dwin, win_other, "pair_win")
    (dwg1, dwu1), ((win_got,),) = _ffn_dwgu(h1, dg1, du1, "ffn1_dwgu", comm=[_Ici([win_sum])])
    gu_parts = [dwg1, dwu1]
    gu_other = _comm_only(_Swap(gu_parts), "swap_gu1")
    gu_sums = [_pair_add(p, o, f"pair_gu1_{i}") for i, (p, o) in enumerate(zip(gu_parts, gu_other))]
    (dx0, _, d_ffn1n), (gu_got,) = _dh_rms_bwd([(dg1, wg1), (du1, wu1)], True, Fb, xs, ffn1_norm, dx1, 1.0, "ffn1_dh",
                                               comm=[_Ici(gu_sums)])

    got = dict(ffn2_w_down=ffn2_got[0], ffn2_w_gate=ffn2_got[1], ffn2_w_up=ffn2_got[2], w_out=mix_got[0], w_att_out=mix_got[1],
               w_sg_out=mix_got[2], w_in=win_got, ffn1_w_gate=gu_got[0], ffn1_w_up=gu_got[1], ffn1_w_down=wd1_got)
    moments = dict(ffn1_w_gate=(m_ffn1_w_gate, v_ffn1_w_gate), ffn1_w_up=(m_ffn1_w_up, v_ffn1_w_up),
                   ffn1_w_down=(m_ffn1_w_down, v_ffn1_w_down), w_in=(m_w_in, v_w_in), w_att_out=(m_w_att_out, v_w_att_out),
                   w_sg_out=(m_w_sg_out, v_w_sg_out), w_out=(m_w_out, v_w_out), ffn2_w_gate=(m_ffn2_w_gate, v_ffn2_w_gate),
                   ffn2_w_up=(m_ffn2_w_up, v_ffn2_w_up), ffn2_w_down=(m_ffn2_w_down, v_ffn2_w_down))
    res = {}
    for n in sharded:
        mm, vv = moments[n]
        outs = _adamw(got[n], sharded[n][0], mm[0], vv[0], "adamw_" + n)
        res[n] = [o[None] for o in outs]

    rows = lambda a: a.reshape(-1, 128)
    small = [("sg_w", rows(d_sgw), sg_w, m_sg_w, v_sg_w), ("ffn1_norm", rows(d_ffn1n), ffn1_norm, m_ffn1_norm, v_ffn1_norm),
             ("mix_norm", rows(d_mixn), mix_norm, m_mix_norm, v_mix_norm), ("ffn2_norm", rows(d_ffn2n), ffn2_norm, m_ffn2_norm, v_ffn2_norm),
             ("final_norm", rows(d_final), final_norm, m_final_norm, v_final_norm), ("sg_ln_g", rows(d_lng), sg_ln_g, m_sg_ln_g, v_sg_ln_g),
             ("sg_ln_b", rows(d_lnb), sg_ln_b, m_sg_ln_b, v_sg_ln_b), ("sg_b", d_sgbT[:, :SG_GROUPS].T, sg_b, m_sg_b, v_sg_b)]
    pad8 = lambda a: _pad_rows(a, -(-a.shape[0] // 8) * 8)
    gpack = jnp.concatenate([pad8(g) for _, g, _, _, _ in small] + [pad8(loss_part)], axis=0)
    gsum = _small_allreduce(gpack, "allreduce_small")
    zero8 = jnp.zeros((8, 128), F32)
    wpack = jnp.concatenate([pad8(rows(w)) for _, _, w, _, _ in small] + [zero8], axis=0)
    mpack = jnp.concatenate([pad8(rows(m)) for _, _, _, m, _ in small] + [zero8], axis=0)
    vpack = jnp.concatenate([pad8(rows(v)) for _, _, _, _, v in small] + [zero8], axis=0)
    packs = _adamw(gsum[None], wpack, mpack, vpack, "adamw_small")
    off = 0
    for n, g, w, _, _ in small:
        r = g.shape[0]
        res[n] = [p[off:off + r].reshape(w.shape) for p in packs]
        off += -(-r // 8) * 8
    loss = gsum[off, 0]

    order = ["ffn1_norm", "ffn1_w_gate", "ffn1_w_up", "ffn1_w_down", "mix_norm", "w_in", "sg_ln_g", "sg_ln_b", "sg_w", "sg_b",
             "w_att_out", "w_sg_out", "w_out", "ffn2_norm", "ffn2_w_gate", "ffn2_w_up", "ffn2_w_down", "final_norm"]
    return (loss, dx0[None], *[res[n][0] for n in order], *[res[n][1] for n in order], *[res[n][2] for n in order],
            *[res[n][3] for n in order])
```

```python
import math

import jax
import jax.numpy as jnp
from jax import lax
from jax.experimental import pallas as pl
from jax.experimental.pallas import tpu as pltpu

BF = jnp.bfloat16
F32 = jnp.float32
MESH = pl.DeviceIdType.MESH
N_DEV = 8
N_CHIP = 4

HEAD_DIM = 128
HEADS_PER_GROUP = 4
GROUP_W = HEADS_PER_GROUP * HEAD_DIM
DILATIONS = (1, 4, 16)
ATT_W = len(DILATIONS) * GROUP_W
SG_W = 1536
SG_GROUPS = 12
BLK = 128
ROPE_DIM = 32
ROPE_THETA = 500000.0
NORM_EPS = 1e-6
LN_EPS = 1e-5
Q_OFF, K_OFF, V_OFF, U_OFF, VS_OFF, GA_OFF = 0, ATT_W, 2 * ATT_W, 3 * ATT_W, 3 * ATT_W + SG_W, 3 * ATT_W + 2 * SG_W

ADAM_LR, ADAM_B1, ADAM_B2, ADAM_EPS, ADAM_WD, ADAM_STEP = 0.001, 0.9, 0.999, 1e-08, 0.01, 10

VMEM_LIMIT = 56 * 1024 * 1024
NEG = -1e30
ANY = pl.BlockSpec(memory_space=pl.ANY)
EPI_ROWS = 128
ACC_COLS = 512
FFN_PAIR = 2
PROJ_TK = 1536
R_U, R_VS, R_GA = 0, SG_W, 2 * SG_W


def _once(shape, index_map):
    return pl.BlockSpec(shape, index_map, pipeline_mode=pl.Buffered(1))


def _tile(n, pref):
    t = min(n, pref)
    while n % t:
        t //= 2
    return t


def _nt(a, b):
    return lax.dot_general(a, b, (((1,), (1,)), ((), ())), preferred_element_type=F32)


def _tn(a, b):
    return lax.dot_general(a, b, (((0,), (0,)), ((), ())), preferred_element_type=F32)


def _nn(a, b):
    return jnp.dot(a, b, preferred_element_type=F32)


def _acc_dots(acc_ref, terms, transposed_rhs=False):
    n = acc_ref.shape[1]
    width = min(n, ACC_COLS)
    for c0 in range(0, n, width):
        cols = slice(c0, c0 + width)
        tot = None
        for lhs, rhs in terms:
            part = _nt(lhs, rhs(cols)) if transposed_rhs else _nn(lhs, rhs(cols))
            tot = part if tot is None else tot + part
        acc_ref[:, cols] += tot


def _gelu(x):
    return 0.5 * x * (1.0 + lax.erf(x * (2.0 ** -0.5)))


def _gelu_grad(x):
    return 0.5 * (1.0 + lax.erf(x * (2.0 ** -0.5))) + x * jnp.exp(-0.5 * x * x) * (1.0 / math.sqrt(2.0 * math.pi))


def _place():
    x, y, c = lax.axis_index("x"), lax.axis_index("y"), lax.axis_index("c")
    return x, y, c


def _flip(v, bit):
    return 1 - v if bit else v


class _Gather:
    def __init__(self, shards, mid_frac=1.0):
        self.arrays = list(shards)
        self.mid_frac = mid_frac
        nw = len(shards)
        self.out_shape = [jax.ShapeDtypeStruct((N_DEV,) + s.shape, s.dtype) for s in shards]
        self.scratch = [pltpu.SemaphoreType.DMA((nw, 7)), pltpu.SemaphoreType.DMA((nw, 7)), pltpu.SemaphoreType.DMA((nw,))]

    def _parts(self, ins, outs, sems):
        x, y, c = _place()
        send, recv, loc = sems
        chips = [(1 - x, y), (x, 1 - y), (1 - x, 1 - y)]

        def copy(k, s, block, to, src=None):
            dst = outs[k].at[4 * block[0] + 2 * block[1] + block[2]]
            return pltpu.make_async_remote_copy(src_ref=dst if src is None else src, dst_ref=dst, send_sem=send.at[k, s],
                                                recv_sem=recv.at[k, s], device_id=to, device_id_type=MESH)

        def first(k):
            return [copy(k, 0, (x, y, c), (x, y, 1 - c), src=ins[k])] + [
                copy(k, 1 + j, (x, y, c), (*chip, c), src=ins[k]) for j, chip in enumerate(chips)]

        def local(k):
            return pltpu.make_async_copy(ins[k], outs[k].at[4 * x + 2 * y + c], loc.at[k])

        return x, y, c, chips, copy, first, local

    def start(self, ins, outs, sems):
        _, _, _, _, _, first, local = self._parts(ins, outs, sems)
        for k in range(len(ins)):
            local(k).start()
            for cp in first(k):
                cp.start()

    def mid(self, ins, outs, sems):
        x, y, c, chips, copy, _, _ = self._parts(ins, outs, sems)
        for k in range(len(ins)):
            for j, chip in enumerate(chips):
                copy(k, 1 + j, (*chip, c), (x, y, c)).wait_recv()
                copy(k, 4 + j, (*chip, c), (x, y, 1 - c)).start()

    def finish(self, ins, outs, sems):
        x, y, c, chips, copy, first, local = self._parts(ins, outs, sems)
        for k in range(len(ins)):
            copy(k, 0, (x, y, 1 - c), (x, y, c)).wait_recv()
            for j, chip in enumerate(chips):
                copy(k, 4 + j, (*chip, 1 - c), (x, y, c)).wait_recv()
        for k in range(len(ins)):
            for cp in first(k):
                cp.wait_send()
            for j, chip in enumerate(chips):
                copy(k, 4 + j, (*chip, c), (x, y, 1 - c)).wait_send()
            local(k).wait()


class _Swap:
    def __init__(self, parts):
        self.arrays = list(parts)
        nw = len(parts)
        self.out_shape = [jax.ShapeDtypeStruct((N_CHIP,) + p.shape[1:], p.dtype) for p in parts]
        self.scratch = [pltpu.SemaphoreType.DMA((nw, N_CHIP)), pltpu.SemaphoreType.DMA((nw, N_CHIP))]

    def _copy(self, ins, outs, sems, k, q):
        x, y, c = _place()
        return pltpu.make_async_remote_copy(src_ref=ins[k].at[2 * q + 1 - c], dst_ref=outs[k].at[q], send_sem=sems[0].at[k, q],
                                            recv_sem=sems[1].at[k, q], device_id=(x, y, 1 - c), device_id_type=MESH)

    mid_frac = None

    def start(self, ins, outs, sems):
        for k in range(len(ins)):
            for q in range(N_CHIP):
                self._copy(ins, outs, sems, k, q).start()

    def finish(self, ins, outs, sems):
        for k in range(len(ins)):
            for q in range(N_CHIP):
                self._copy(ins, outs, sems, k, q).wait()


class _Ici:
    mid_frac = None

    def __init__(self, sums):
        self.arrays = list(sums)
        nw = len(sums)
        self.out_shape = [jax.ShapeDtypeStruct(s.shape, s.dtype) for s in sums]
        self.scratch = [pltpu.SemaphoreType.DMA((nw, 3)), pltpu.SemaphoreType.DMA((nw, 3)), pltpu.SemaphoreType.DMA((nw,))]

    def _copies(self, ins, outs, sems, k):
        x, y, c = _place()
        myq = 2 * x + y
        out = []
        for r in range(1, N_CHIP):
            px, py = _flip(x, r & 2), _flip(y, r & 1)
            pq = 2 * px + py
            mk = lambda dst: pltpu.make_async_remote_copy(src_ref=ins[k].at[pq], dst_ref=dst, send_sem=sems[0].at[k, r - 1],
                                                          recv_sem=sems[1].at[k, r - 1], device_id=(px, py, c), device_id_type=MESH)
            out.append((mk(outs[k].at[myq]), mk(outs[k].at[pq])))
        return out, pltpu.make_async_copy(ins[k].at[myq], outs[k].at[myq], sems[2].at[k])

    def start(self, ins, outs, sems):
        for k in range(len(ins)):
            remote, local = self._copies(ins, outs, sems, k)
            local.start()
            for snd, _ in remote:
                snd.start()

    def finish(self, ins, outs, sems):
        for k in range(len(ins)):
            remote, local = self._copies(ins, outs, sems, k)
            for snd, rcv in remote:
                rcv.wait_recv()
                snd.wait_send()
            local.wait()


class _Spread:
    mid_frac = None

    def __init__(self, arrays):
        self.arrays = list(arrays)
        nw = len(arrays)
        self.out_shape = [jax.ShapeDtypeStruct((N_DEV,) + a.shape, a.dtype) for a in arrays]
        self.scratch = [pltpu.SemaphoreType.DMA((nw, 7)), pltpu.SemaphoreType.DMA((nw, 7)), pltpu.SemaphoreType.DMA((nw,))]

    def _copies(self, ins, outs, sems, k):
        x, y, c = _place()
        me = 4 * x + 2 * y + c
        out = []
        for r in range(1, N_DEV):
            px, py, pc = _flip(x, r & 4), _flip(y, r & 2), _flip(c, r & 1)
            peer = 4 * px + 2 * py + pc
            mk = lambda dst: pltpu.make_async_remote_copy(src_ref=ins[k], dst_ref=dst, send_sem=sems[0].at[k, r - 1],
                                                          recv_sem=sems[1].at[k, r - 1], device_id=(px, py, pc), device_id_type=MESH)
            out.append((mk(outs[k].at[me]), mk(outs[k].at[peer])))
        return out, pltpu.make_async_copy(ins[k], outs[k].at[me], sems[2].at[k])

    def start(self, ins, outs, sems):
        for k in range(len(ins)):
            remote, local = self._copies(ins, outs, sems, k)
            local.start()
            for snd, _ in remote:
                snd.start()

    def finish(self, ins, outs, sems):
        for k in range(len(ins)):
            remote, local = self._copies(ins, outs, sems, k)
            for snd, rcv in remote:
                rcv.wait_recv()
                snd.wait_send()
            local.wait()


def _call(body, *, grid, in_specs, out_specs, out_shape, name, args, scratch=(), comm=()):
    comm = list(comm)
    n_in, n_out, n_scr = len(in_specs), len(out_specs), len(scratch)
    total = math.prod(grid) if grid else 1

    def wrapped(*refs):
        p = n_in
        cin = []
        for cm in comm:
            cin.append(refs[p:p + len(cm.arrays)])
            p += len(cm.arrays)
        own_out = refs[p:p + n_out]
        p += n_out
        cout = []
        for cm in comm:
            cout.append(refs[p:p + len(cm.arrays)])
            p += len(cm.arrays)
        own_scr = refs[p:p + n_scr]
        p += n_scr
        csem = []
        for cm in comm:
            csem.append(refs[p:p + len(cm.scratch)])
            p += len(cm.scratch)
        step = 0
        for axis, g in enumerate(grid):
            step = step * g + pl.program_id(axis)

        def at(when, what):
            if total == 1:
                what()
            else:
                pl.when(step == when)(what)

        def starts():
            for cm, i, o, s in zip(comm, cin, cout, csem):
                cm.start(i, o, s)

        def finishes():
            for cm, i, o, s in zip(comm, cin, cout, csem):
                cm.finish(i, o, s)

        if comm:
            at(0, starts)
        if body is not None:
            body(*refs[:n_in], *own_out, *own_scr)
        for cm, i, o, s in zip(comm, cin, cout, csem):
            if cm.mid_frac is not None:
                at(min(total - 1, int(total * cm.mid_frac)), lambda cm=cm, i=i, o=o, s=s: cm.mid(i, o, s))
        if comm:
            at(total - 1, finishes)

    kw = dict(grid=tuple(grid)) if grid else {}
    outs = pl.pallas_call(
        wrapped, name=name, **kw,
        in_specs=list(in_specs) + [ANY for cm in comm for _ in cm.arrays],
        out_specs=list(out_specs) + [ANY for cm in comm for _ in cm.arrays],
        out_shape=list(out_shape) + [s for cm in comm for s in cm.out_shape],
        scratch_shapes=list(scratch) + [s for cm in comm for s in cm.scratch],
        compiler_params=pltpu.CompilerParams(dimension_semantics=("arbitrary",) * len(grid), vmem_limit_bytes=VMEM_LIMIT),
    )(*args, *[a for cm in comm for a in cm.arrays])
    own, p, per = list(outs[:n_out]), n_out, []
    for cm in comm:
        per.append(list(outs[p:p + len(cm.arrays)]))
        p += len(cm.arrays)
    return own, per


def _comm_only(cm, name):
    return _call(None, grid=(), in_specs=[], out_specs=[], out_shape=[], name=name, args=[], comm=[cm])[1][0]


def _rms_fwd(x, g, name):
    S, D = x.shape
    tm = _tile(S, 512)

    def body(x_ref, g_ref, o_ref):
        xv = x_ref[...]
        r = lax.rsqrt(jnp.mean(xv * xv, axis=-1, keepdims=True) + NORM_EPS)
        o_ref[...] = (xv * r * g_ref[...]).astype(BF)

    return _call(body, grid=(S // tm,), name=name, args=[x, g],
                 in_specs=[pl.BlockSpec((tm, D), lambda i: (i, 0)), pl.BlockSpec((1, D), lambda i: (0, 0))],
                 out_specs=[pl.BlockSpec((tm, D), lambda i: (i, 0))], out_shape=[jax.ShapeDtypeStruct((S, D), BF)])[0][0]


def _ffn_up(h, wg, wu, name, comm=()):
    S, D = h.shape
    nb, _, Fb = wg.shape
    tm = _tile(S, 512)

    def body(h_ref, wg_ref, wu_ref, g_ref, u_ref, a_ref):
        hv = h_ref[...]
        g = _nn(hv, wg_ref[0])
        u = _nn(hv, wu_ref[0])
        g_ref[0] = g.astype(BF)
        u_ref[0] = u.astype(BF)
        a_ref[0] = (g * jax.nn.sigmoid(g) * u).astype(BF)

    act = pl.BlockSpec((1, tm, Fb), lambda j, i: (j, i, 0))
    w = pl.BlockSpec((1, D, Fb), lambda j, i: (j, 0, 0))
    shp = jax.ShapeDtypeStruct((nb, S, Fb), BF)
    return _call(body, grid=(nb, S // tm), name=name, args=[h, wg, wu], comm=comm,
                 in_specs=[pl.BlockSpec((tm, D), lambda j, i: (i, 0)), w, w], out_specs=[act, act, act], out_shape=[shp, shp, shp])


def _ffn_down_norm(a, wd, x, gn, name, comm=()):
    nb, S, Fb = a.shape
    D = wd.shape[2]
    tm = _tile(S, 512)

    nj = nb // FFN_PAIR

    def body(a_ref, wd_ref, x_ref, gn_ref, xo_ref, hn_ref, acc_ref):
        j = pl.program_id(1)

        @pl.when(j == 0)
        def _():
            acc_ref[...] = jnp.zeros_like(acc_ref)

        _acc_dots(acc_ref, [(a_ref[b], lambda cols, b=b: wd_ref[b, :, cols]) for b in range(FFN_PAIR)])

        @pl.when(j == nj - 1)
        def _():
            def chunk(t, carry):
                rows = pl.ds(pl.multiple_of(t * EPI_ROWS, EPI_ROWS), EPI_ROWS)
                xo = x_ref[rows, :] + 0.5 * acc_ref[rows, :]
                r = lax.rsqrt(jnp.mean(xo * xo, axis=-1, keepdims=True) + NORM_EPS)
                xo_ref[rows, :] = xo
                hn_ref[rows, :] = (xo * r * gn_ref[...]).astype(BF)
                return carry

            lax.fori_loop(0, tm // EPI_ROWS, chunk, 0)

    row = pl.BlockSpec((tm, D), lambda i, j: (i, 0))
    return _call(body, grid=(S // tm, nj), name=name, args=[a, wd, x, gn], comm=comm,
                 in_specs=[pl.BlockSpec((FFN_PAIR, tm, Fb), lambda i, j: (j, i, 0)), pl.BlockSpec((FFN_PAIR, Fb, D), lambda i, j: (j, 0, 0)),
                           _once((tm, D), lambda i, j: (i, 0)), pl.BlockSpec((1, D), lambda i, j: (0, 0))],
                 out_specs=[row, row], out_shape=[jax.ShapeDtypeStruct((S, D), F32), jax.ShapeDtypeStruct((S, D), BF)],
                 scratch=[pltpu.VMEM((tm, D), F32)])


def _ffn_down_loss(a, wd, x, gf, tgt, name):
    nb, S, Fb = a.shape
    D = wd.shape[2]
    tm = _tile(S, 512)

    nj = nb // FFN_PAIR

    def body(a_ref, wd_ref, x_ref, gf_ref, t_ref, dx_ref, dxb_ref, dgf_ref, loss_ref, acc_ref):
        i, j = pl.program_id(0), pl.program_id(1)

        @pl.when(j == 0)
        def _():
            acc_ref[...] = jnp.zeros_like(acc_ref)

        _acc_dots(acc_ref, [(a_ref[b], lambda cols, b=b: wd_ref[b, :, cols]) for b in range(FFN_PAIR)])

        @pl.when((j == nj - 1) & (i == 0))
        def _():
            dgf_ref[...] = jnp.zeros_like(dgf_ref)
            loss_ref[...] = jnp.zeros_like(loss_ref)

        @pl.when(j == nj - 1)
        def _():
            def chunk(t, carry):
                rows = pl.ds(pl.multiple_of(t * EPI_ROWS, EPI_ROWS), EPI_ROWS)
                xo = x_ref[rows, :] + 0.5 * acc_ref[rows, :]
                r = lax.rsqrt(jnp.mean(xo * xo, axis=-1, keepdims=True) + NORM_EPS)
                xh = xo * r
                gf = gf_ref[...]
                e = xh * gf - t_ref[rows, :]
                loss_ref[...] += jnp.sum(jnp.mean(e * e, axis=-1, keepdims=True), axis=0, keepdims=True) * 0.5
                dy = e * (1.0 / D)
                dgf_ref[...] += jnp.sum(dy * xh, axis=0, keepdims=True)
                dxh = dy * gf
                dx = r * (dxh - xh * jnp.mean(dxh * xh, axis=-1, keepdims=True))
                dx_ref[rows, :] = dx
                dxb_ref[rows, :] = (0.5 * dx).astype(BF)
                return carry

            lax.fori_loop(0, tm // EPI_ROWS, chunk, 0)

    row = pl.BlockSpec((tm, D), lambda i, j: (i, 0))
    once = _once((tm, D), lambda i, j: (i, 0))
    vec = pl.BlockSpec((1, D), lambda i, j: (0, 0))
    return _call(body, grid=(S // tm, nj), name=name, args=[a, wd, x, gf, tgt],
                 in_specs=[pl.BlockSpec((FFN_PAIR, tm, Fb), lambda i, j: (j, i, 0)), pl.BlockSpec((FFN_PAIR, Fb, D), lambda i, j: (j, 0, 0)),
                           once, vec, once],
                 out_specs=[row, row, vec, pl.BlockSpec((1, 128), lambda i, j: (0, 0))],
                 out_shape=[jax.ShapeDtypeStruct((S, D), F32), jax.ShapeDtypeStruct((S, D), BF), jax.ShapeDtypeStruct((1, D), F32),
                            jax.ShapeDtypeStruct((1, 128), F32)],
                 scratch=[pltpu.VMEM((tm, D), F32)])[0]


def _ffn_bwd_act(dyb, wd, g, u, name, comm=()):
    S, D = dyb.shape
    nb, Fb, _ = wd.shape
    tm = _tile(S, 512)

    def body(dy_ref, wd_ref, g_ref, u_ref, dg_ref, du_ref):
        da = _nt(dy_ref[...], wd_ref[0])
        gv = g_ref[0].astype(F32)
        uv = u_ref[0].astype(F32)
        sg = jax.nn.sigmoid(gv)
        du_ref[0] = (da * gv * sg).astype(BF)
        dg_ref[0] = (da * uv * sg * (1.0 + gv * (1.0 - sg))).astype(BF)

    act = pl.BlockSpec((1, tm, Fb), lambda j, i: (j, i, 0))
    shp = jax.ShapeDtypeStruct((nb, S, Fb), BF)
    return _call(body, grid=(nb, S // tm), name=name, args=[dyb, wd, g, u], comm=comm,
                 in_specs=[pl.BlockSpec((tm, D), lambda j, i: (i, 0)), pl.BlockSpec((1, Fb, D), lambda j, i: (j, 0, 0)), act, act],
                 out_specs=[act, act], out_shape=[shp, shp])


def _ffn_dwd(a, dyb, name, comm=()):
    nb, S, Fb = a.shape
    D = dyb.shape[1]
    ts = _tile(S, 512)
    ns = S // ts

    def body(a_ref, dy_ref, o_ref, acc_ref):
        s = pl.program_id(1)

        @pl.when(s == 0)
        def _():
            acc_ref[...] = jnp.zeros_like(acc_ref)

        acc_ref[...] += _tn(a_ref[0], dy_ref[...])

        @pl.when(s == ns - 1)
        def _():
            o_ref[0] = acc_ref[...].astype(BF)

    return _call(body, grid=(nb, ns), name=name, args=[a, dyb], comm=comm,
                 in_specs=[pl.BlockSpec((1, ts, Fb), lambda j, s: (j, s, 0)), pl.BlockSpec((ts, D), lambda j, s: (s, 0))],
                 out_specs=[pl.BlockSpec((1, Fb, D), lambda j, s: (j, 0, 0))], out_shape=[jax.ShapeDtypeStruct((nb, Fb, D), BF)],
                 scratch=[pltpu.VMEM((Fb, D), F32)])


def _ffn_dwgu(h, dg, du, name, comm=()):
    S, D = h.shape
    nb, _, Fb = dg.shape
    ts = _tile(S, 512)
    ns = S // ts

    def body(h_ref, dg_ref, du_ref, og_ref, ou_ref, accg_ref, accu_ref):
        s = pl.program_id(1)

        @pl.when(s == 0)
        def _():
            accg_ref[...] = jnp.zeros_like(accg_ref)
            accu_ref[...] = jnp.zeros_like(accu_ref)

        hv = h_ref[...]
        accg_ref[...] += _tn(hv, dg_ref[0])
        accu_ref[...] += _tn(hv, du_ref[0])

        @pl.when(s == ns - 1)
        def _():
            og_ref[0] = accg_ref[...].astype(BF)
            ou_ref[0] = accu_ref[...].astype(BF)

    act = pl.BlockSpec((1, ts, Fb), lambda j, s: (j, s, 0))
    out = pl.BlockSpec((1, D, Fb), lambda j, s: (j, 0, 0))
    shp = jax.ShapeDtypeStruct((nb, D, Fb), BF)
    return _call(body, grid=(nb, ns), name=name, args=[h, dg, du], comm=comm,
                 in_specs=[pl.BlockSpec((ts, D), lambda j, s: (s, 0)), act, act], out_specs=[out, out], out_shape=[shp, shp],
                 scratch=[pltpu.VMEM((D, Fb), F32), pltpu.VMEM((D, Fb), F32)])


def _dh_rms_bwd(pairs, blocked, tk, x, gn, dxo, out_scale, name, comm=()):
    S, D = x.shape
    nk = pairs[0][0].shape[0] if blocked else pairs[0][0].shape[1] // tk
    tm = _tile(S, 512)
    npair = len(pairs)

    def body(*refs):
        ins = refs[: 2 * npair]
        x_ref, gn_ref, dxo_ref, dx_ref, dxb_ref, dgn_ref, acc_ref = refs[2 * npair:]
        i, k = pl.program_id(0), pl.program_id(1)

        @pl.when(k == 0)
        def _():
            acc_ref[...] = jnp.zeros_like(acc_ref)

        if blocked:
            terms = [(ins[2 * p][0], lambda cols, r=ins[2 * p + 1]: r[0, cols, :]) for p in range(npair)]
        else:
            terms = [(ins[2 * p][...], lambda cols, r=ins[2 * p + 1]: r[:, cols]) for p in range(npair)]
        _acc_dots(acc_ref, terms, transposed_rhs=blocked)

        @pl.when((k == nk - 1) & (i == 0))
        def _():
            dgn_ref[...] = jnp.zeros_like(dgn_ref)

        @pl.when(k == nk - 1)
        def _():
            def chunk(t, carry):
                rows = pl.ds(pl.multiple_of(t * EPI_ROWS, EPI_ROWS), EPI_ROWS)
                xv = x_ref[rows, :]
                r = lax.rsqrt(jnp.mean(xv * xv, axis=-1, keepdims=True) + NORM_EPS)
                xh = xv * r
                dh = acc_ref[rows, :]
                dgn_ref[...] += jnp.sum(dh * xh, axis=0, keepdims=True)
                dxh = dh * gn_ref[...]
                dx = dxo_ref[rows, :] + r * (dxh - xh * jnp.mean(dxh * xh, axis=-1, keepdims=True))
                dx_ref[rows, :] = dx
                dxb_ref[rows, :] = (out_scale * dx).astype(BF)
                return carry

            lax.fori_loop(0, tm // EPI_ROWS, chunk, 0)

    if blocked:
        lspec = pl.BlockSpec((1, tm, tk), lambda i, k: (k, i, 0))
        rspec = pl.BlockSpec((1, D, tk), lambda i, k: (k, 0, 0))
    else:
        lspec = pl.BlockSpec((tm, tk), lambda i, k: (i, k))
        rspec = pl.BlockSpec((tk, D), lambda i, k: (k, 0))
    row = pl.BlockSpec((tm, D), lambda i, k: (i, 0))
    once = _once((tm, D), lambda i, k: (i, 0))
    vec = pl.BlockSpec((1, D), lambda i, k: (0, 0))
    flat = [t for pr in pairs for t in pr]
    return _call(body, grid=(S // tm, nk), name=name, args=[*flat, x, gn, dxo], comm=comm,
                 in_specs=[lspec, rspec] * npair + [once, vec, once], out_specs=[row, row, vec],
                 out_shape=[jax.ShapeDtypeStruct((S, D), F32), jax.ShapeDtypeStruct((S, D), BF), jax.ShapeDtypeStruct((1, D), F32)],
                 scratch=[pltpu.VMEM((tm, D), F32)])


def _mm_nn(a, b, tm, tn, col0, col1, dtype, name, comm=()):
    M, K = a.shape
    n0, nn = col0 // tn, (col1 - col0) // tn

    def body(a_ref, b_ref, o_ref):
        o_ref[...] = _nn(a_ref[...], b_ref[...]).astype(dtype)

    return _call(body, grid=(nn, M // tm), name=name, args=[a, b], comm=comm,
                 in_specs=[pl.BlockSpec((tm, K), lambda n, i: (i, 0)), pl.BlockSpec((K, tn), lambda n, i: (0, n0 + n))],
                 out_specs=[pl.BlockSpec((tm, tn), lambda n, i: (i, n))], out_shape=[jax.ShapeDtypeStruct((M, nn * tn), dtype)])


def _mm_tn(a, b, tm, tn, ts, blocked, name, comm=(), ncols=None):
    S, M = a.shape
    N = b.shape[1] if ncols is None else ncols
    ns = S // ts

    def body(a_ref, b_ref, o_ref, acc_ref):
        s = pl.program_id(2)

        @pl.when(s == 0)
        def _():
            acc_ref[...] = jnp.zeros_like(acc_ref)

        acc_ref[...] += _tn(a_ref[...], b_ref[...])

        @pl.when(s == ns - 1)
        def _():
            if blocked:
                o_ref[0] = acc_ref[...].astype(BF)
            else:
                o_ref[...] = acc_ref[...].astype(BF)

    if blocked:
        ospec = pl.BlockSpec((1, tm, tn), lambda i, n, s: (n, i, 0))
        oshape = jax.ShapeDtypeStruct((N // tn, M, tn), BF)
    else:
        ospec = pl.BlockSpec((tm, tn), lambda i, n, s: (i, n))
        oshape = jax.ShapeDtypeStruct((M, N), BF)
    return _call(body, grid=(M // tm, N // tn, ns), name=name, args=[a, b], comm=comm,
                 in_specs=[pl.BlockSpec((ts, tm), lambda i, n, s: (s, i)), pl.BlockSpec((ts, tn), lambda i, n, s: (s, n))],
                 out_specs=[ospec], out_shape=[oshape], scratch=[pltpu.VMEM((tm, tn), F32)])


def _rope_tables(S):
    half = ROPE_DIM // 2
    inv_freq = ROPE_THETA ** (-jnp.arange(0, ROPE_DIM, 2, dtype=F32) / ROPE_DIM)
    ang = jnp.arange(S, dtype=F32)[:, None] * inv_freq[None, :]
    cos, sin = jnp.cos(ang), jnp.sin(ang)
    zeros = jnp.zeros((S, HEAD_DIM - ROPE_DIM), F32)
    c = jnp.concatenate([cos, cos, jnp.ones((S, HEAD_DIM - ROPE_DIM), F32)], axis=1)
    sm = jnp.concatenate([-sin, jnp.zeros((S, half), F32), zeros], axis=1)
    sp = jnp.concatenate([jnp.zeros((S, half), F32), sin, zeros], axis=1)
    return c, sm, sp


def _rope(t, c, sm, sp):
    return t * c + pltpu.roll(t, HEAD_DIM - ROPE_DIM // 2, 1) * sm + pltpu.roll(t, ROPE_DIM // 2, 1) * sp


def _rope_t(dy, c, sm, sp):
    return dy * c + pltpu.roll(dy * sm, ROPE_DIM // 2, 1) + pltpu.roll(dy * sp, HEAD_DIM - ROPE_DIM // 2, 1)


def _att_mask(i):
    qi = lax.broadcasted_iota(jnp.int32, (BLK, 2 * BLK), 0)
    kj = lax.broadcasted_iota(jnp.int32, (BLK, 2 * BLK), 1)
    diff = qi + BLK - kj
    first_key = jnp.where(i > 0, 0, BLK)
    return (diff >= 0) & (diff <= BLK) & (kj >= first_key)


def _res_rows(r, i, n, d):
    if d == 1:
        return pl.ds(pl.multiple_of(i * n, n), n)
    return pl.ds(r + i * (n * d), n, stride=d)


def _att_specs(S, gi):
    def sect(off):
        base = (off + gi * GROUP_W) // HEAD_DIM
        return _once((S, HEAD_DIM), lambda hh, r: (0, base + hh))

    tab = pl.BlockSpec((S, HEAD_DIM), lambda hh, r: (0, 0))
    head = pl.BlockSpec((S, HEAD_DIM), lambda hh, r: (0, hh))
    return sect, tab, head


def _att_fwd(qkv, tabs, gi, d, name, comm=()):
    S = qkv.shape[0]
    L = S // d
    sect, tab, head = _att_specs(S, gi)
    nblk = L // BLK
    scale = HEAD_DIM ** -0.5

    def body(q_ref, k_ref, v_ref, c_ref, sm_ref, sp_ref, o_ref, lse_ref, qr, kp, vp):
        r = pl.program_id(1)
        res = _res_rows(r, 0, L, d)
        c, sm, sp = c_ref[res, :], sm_ref[res, :], sp_ref[res, :]
        qr[...] = _rope(q_ref[res, :], c, sm, sp).astype(BF)
        kp[pl.ds(0, BLK), :] = jnp.zeros((BLK, HEAD_DIM), BF)
        vp[pl.ds(0, BLK), :] = jnp.zeros((BLK, HEAD_DIM), BF)
        kp[pl.ds(BLK, L), :] = _rope(k_ref[res, :], c, sm, sp).astype(BF)
        vp[pl.ds(BLK, L), :] = v_ref[res, :].astype(BF)

        def blk(i, carry):
            r0 = pl.multiple_of(i * BLK, BLK)
            s = _nt(qr[pl.ds(r0, BLK), :], kp[pl.ds(r0, 2 * BLK), :]) * scale
            s = jnp.where(_att_mask(i), s, NEG)
            m = jnp.max(s, axis=-1, keepdims=True)
            p = jnp.exp(s - m)
            l = jnp.sum(p, axis=-1, keepdims=True)
            out = _res_rows(r, i, BLK, d)
            o_ref[out, :] = _nn(p.astype(BF), vp[pl.ds(r0, 2 * BLK), :]) / l
            lse_ref[out, :] = jnp.broadcast_to(m + jnp.log(l), (BLK, HEAD_DIM))
            return carry

        lax.fori_loop(0, nblk, blk, 0, unroll=min(4, nblk))

    shp = jax.ShapeDtypeStruct((S, GROUP_W), F32)
    return _call(body, grid=(HEADS_PER_GROUP, d), name=name, args=[qkv, qkv, qkv, *tabs], comm=comm,
                 in_specs=[sect(Q_OFF), sect(K_OFF), sect(V_OFF), tab, tab, tab], out_specs=[head, head], out_shape=[shp, shp],
                 scratch=[pltpu.VMEM((L, HEAD_DIM), BF), pltpu.VMEM((L + BLK, HEAD_DIM), BF), pltpu.VMEM((L + BLK, HEAD_DIM), BF)])


def _att_combine(os, lses, name):
    S = os[0].shape[0]
    tm = _tile(S, 512)

    def body(o0, o1, o2, l0, l1, l2, oa_ref, lse_ref):
        a, b, c = l0[...], l1[...], l2[...]
        mx = jnp.maximum(jnp.maximum(a, b), c)
        wa, wb, wc = jnp.exp(a - mx), jnp.exp(b - mx), jnp.exp(c - mx)
        den = wa + wb + wc
        oa_ref[...] = ((wa * o0[...] + wb * o1[...] + wc * o2[...]) / den).astype(BF)
        lse_ref[...] = mx + jnp.log(den)

    row = pl.BlockSpec((tm, GROUP_W), lambda i: (i, 0))
    return _call(body, grid=(S // tm,), name=name, args=[*os, *lses], in_specs=[row] * 6, out_specs=[row, row],
                 out_shape=[jax.ShapeDtypeStruct((S, GROUP_W), BF), jax.ShapeDtypeStruct((S, GROUP_W), F32)])[0]


def _att_bwd(qkv, tabs, do, lse, dvec, gi, d, name, comm=()):
    S = qkv.shape[0]
    L = S // d
    sect, tab, head = _att_specs(S, gi)
    stat = _once((S, HEAD_DIM), lambda hh, r: (0, hh))
    nblk = L // BLK
    scale = HEAD_DIM ** -0.5

    def body(q_ref, k_ref, v_ref, c_ref, sm_ref, sp_ref, do_ref, lse_ref, dv_ref, dq_out, dk_out, dv_out, qr, kp, vp, dkp, dvp):
        r = pl.program_id(1)
        res = _res_rows(r, 0, L, d)
        c, sm, sp = c_ref[res, :], sm_ref[res, :], sp_ref[res, :]
        qr[...] = _rope(q_ref[res, :], c, sm, sp).astype(BF)
        kp[pl.ds(0, BLK), :] = jnp.zeros((BLK, HEAD_DIM), BF)
        vp[pl.ds(0, BLK), :] = jnp.zeros((BLK, HEAD_DIM), BF)
        kp[pl.ds(BLK, L), :] = _rope(k_ref[res, :], c, sm, sp).astype(BF)
        vp[pl.ds(BLK, L), :] = v_ref[res, :].astype(BF)
        dkp[...] = jnp.zeros_like(dkp)
        dvp[...] = jnp.zeros_like(dvp)

        def blk(i, carry):
            r0 = pl.multiple_of(i * BLK, BLK)
            rows, win, pos = pl.ds(r0, BLK), pl.ds(r0, 2 * BLK), _res_rows(r, i, BLK, d)
            q, kw, vw, dob = qr[rows, :], kp[win, :], vp[win, :], do_ref[pos, :].astype(BF)
            s = jnp.where(_att_mask(i), _nt(q, kw) * scale, NEG)
            p = jnp.exp(s - lse_ref[pos, :][:, :1])
            ds = p * (_nt(dob, vw) - dv_ref[pos, :][:, :1]) * scale
            dsb = ds.astype(BF)
            dq_out[pos, :] = _rope_t(_nn(dsb, kw), c_ref[pos, :], sm_ref[pos, :], sp_ref[pos, :])
            dkp[win, :] += _tn(dsb, q)
            dvp[win, :] += _tn(p.astype(BF), dob)
            return carry

        lax.fori_loop(0, nblk, blk, 0, unroll=2)
        dk_out[res, :] = _rope_t(dkp[pl.ds(BLK, L), :], c, sm, sp)
        dv_out[res, :] = dvp[pl.ds(BLK, L), :]

    shp = jax.ShapeDtypeStruct((S, GROUP_W), F32)
    return _call(body, grid=(HEADS_PER_GROUP, d), name=name, args=[qkv, qkv, qkv, *tabs, do, lse, dvec], comm=comm,
                 in_specs=[sect(Q_OFF), sect(K_OFF), sect(V_OFF), tab, tab, tab, stat, stat, stat],
                 out_specs=[head, head, head], out_shape=[shp, shp, shp],
                 scratch=[pltpu.VMEM((L, HEAD_DIM), BF), pltpu.VMEM((L + BLK, HEAD_DIM), BF), pltpu.VMEM((L + BLK, HEAD_DIM), BF),
                          pltpu.VMEM((L + BLK, HEAD_DIM), F32), pltpu.VMEM((L + BLK, HEAD_DIM), F32)])


def _sg_parts(u_ref, vs_ref, g_ref, b_ref):
    uv = u_ref[...].astype(F32)
    vv = vs_ref[...].astype(F32)
    vg = _gelu(vv)
    mu = jnp.mean(vg, axis=-1, keepdims=True)
    vc = vg - mu
    rs = lax.rsqrt(jnp.mean(vc * vc, axis=-1, keepdims=True) + LN_EPS)
    y = vc * rs
    return uv, vv, rs, y, y * g_ref[...] + b_ref[...]


def _sg_wmask():
    t = lax.broadcasted_iota(jnp.int32, (BLK, BLK), 0)
    s = lax.broadcasted_iota(jnp.int32, (BLK, BLK), 1)
    return s <= t


def _sg_fwd(proj, sgw, sgbT, lng, lnb, name):
    S, P = proj.shape

    def body(u_ref, vs_ref, w_ref, bt_ref, g_ref, b_ref, z_ref):
        uv, _, _, _, vln = _sg_parts(u_ref, vs_ref, g_ref, b_ref)
        ug = _gelu(uv)
        vb = vln.astype(BF)
        mask = _sg_wmask()
        bt = bt_ref[...]
        for g in range(SG_GROUPS):
            cols = slice(g * BLK, (g + 1) * BLK)
            w = jnp.where(mask, w_ref[g], 0.0).astype(BF)
            sp = _nn(w, vb[:, cols]) + bt[:, g:g + 1]
            z_ref[:, cols] = (ug[:, cols] * sp).astype(BF)

    tile = lambda off: pl.BlockSpec((BLK, SG_W), lambda i: (i, off // SG_W))
    full = lambda shape: pl.BlockSpec(shape, lambda i: (0,) * len(shape))
    return _call(body, grid=(S // BLK,), name=name, args=[proj, proj, sgw, sgbT, lng, lnb],
                 in_specs=[tile(R_U), tile(R_VS), full((SG_GROUPS, BLK, BLK)), full((BLK, BLK)), full((1, SG_W)), full((1, SG_W))],
                 out_specs=[pl.BlockSpec((BLK, SG_W), lambda i: (i, 0))], out_shape=[jax.ShapeDtypeStruct((S, SG_W), BF)])[0][0]


def _sg_bwd(proj, dz, sgw, sgbT, lng, lnb, name):
    S, P = proj.shape

    def body(u_ref, vs_ref, dz_ref, w_ref, bt_ref, g_ref, b_ref, du_ref, dvs_ref, dw_ref, dbt_ref, dg_ref, db_ref, dvln):
        @pl.when(pl.program_id(0) == 0)
        def _():
            dw_ref[...] = jnp.zeros_like(dw_ref)
            dbt_ref[...] = jnp.zeros_like(dbt_ref)
            dg_ref[...] = jnp.zeros_like(dg_ref)
            db_ref[...] = jnp.zeros_like(db_ref)

        uv, vv, rs, y, vln = _sg_parts(u_ref, vs_ref, g_ref, b_ref)
        ug = _gelu(uv)
        vb = vln.astype(BF)
        dzv = dz_ref[...].astype(F32)
        dsp = dzv * ug
        dspb = dsp.astype(BF)
        mask = _sg_wmask()
        bt = bt_ref[...]
        lane = lax.broadcasted_iota(jnp.int32, (BLK, BLK), 1)
        dbt = jnp.zeros((BLK, BLK), F32)
        for g in range(SG_GROUPS):
            cols = slice(g * BLK, (g + 1) * BLK)
            w = jnp.where(mask, w_ref[g], 0.0).astype(BF)
            sp = _nn(w, vb[:, cols]) + bt[:, g:g + 1]
            du_ref[:, cols] = (dzv[:, cols] * sp * _gelu_grad(uv[:, cols])).astype(BF)
            dw_ref[g] += jnp.where(mask, _nt(dspb[:, cols], vb[:, cols]), 0.0)
            dbt = dbt + jnp.where(lane == g, jnp.sum(dsp[:, cols], axis=-1, keepdims=True), 0.0)
            dvln[:, cols] = _tn(w, dspb[:, cols])
        dbt_ref[...] += dbt
        dvl = dvln[...]
        dg_ref[...] += jnp.sum(dvl * y, axis=0, keepdims=True)
        db_ref[...] += jnp.sum(dvl, axis=0, keepdims=True)
        dy = dvl * g_ref[...]
        dvg = rs * (dy - jnp.mean(dy, axis=-1, keepdims=True) - y * jnp.mean(dy * y, axis=-1, keepdims=True))
        dvs_ref[...] = (dvg * _gelu_grad(vv)).astype(BF)

    tile = lambda off: pl.BlockSpec((BLK, SG_W), lambda i: (i, off // SG_W))
    full = lambda shape: pl.BlockSpec(shape, lambda i: (0,) * len(shape))
    row = pl.BlockSpec((BLK, SG_W), lambda i: (i, 0))
    return _call(body, grid=(S // BLK,), name=name, args=[proj, proj, dz, sgw, sgbT, lng, lnb],
                 in_specs=[tile(R_U), tile(R_VS), row, full((SG_GROUPS, BLK, BLK)), full((BLK, BLK)), full((1, SG_W)), full((1, SG_W))],
                 out_specs=[row, row, full((SG_GROUPS, BLK, BLK)), full((BLK, BLK)), full((1, SG_W)), full((1, SG_W))],
                 out_shape=[jax.ShapeDtypeStruct((S, SG_W), BF), jax.ShapeDtypeStruct((S, SG_W), BF),
                            jax.ShapeDtypeStruct((SG_GROUPS, BLK, BLK), F32), jax.ShapeDtypeStruct((BLK, BLK), F32),
                            jax.ShapeDtypeStruct((1, SG_W), F32), jax.ShapeDtypeStruct((1, SG_W), F32)],
                 scratch=[pltpu.VMEM((BLK, SG_W), F32)])[0]


def _gate_merge(oatt, z, watt, wsg, proj, name, comm=()):
    S = oatt.shape[0]
    nb, _, Db = watt.shape
    D = nb * Db
    tm = _tile(S, 512)
    ga, gs = R_GA // Db, (R_GA + D) // Db

    def body(oa_ref, z_ref, wa_ref, ws_ref, ga_ref, gs_ref, ya_ref, ys_ref, mg_ref):
        ya = _nn(oa_ref[...], wa_ref[0])
        ys = _nn(z_ref[...], ws_ref[0])
        ya_ref[...] = ya.astype(BF)
        ys_ref[...] = ys.astype(BF)
        mg_ref[...] = (jax.nn.sigmoid(ga_ref[...].astype(F32)) * ya + jax.nn.sigmoid(gs_ref[...].astype(F32)) * ys).astype(BF)

    out = pl.BlockSpec((tm, Db), lambda j, i: (i, j))
    shp = jax.ShapeDtypeStruct((S, D), BF)
    return _call(body, grid=(nb, S // tm), name=name, args=[oatt, z, watt, wsg, proj, proj], comm=comm,
                 in_specs=[pl.BlockSpec((tm, GROUP_W), lambda j, i: (i, 0)), pl.BlockSpec((tm, SG_W), lambda j, i: (i, 0)),
                           pl.BlockSpec((1, GROUP_W, Db), lambda j, i: (j, 0, 0)), pl.BlockSpec((1, SG_W, Db), lambda j, i: (j, 0, 0)),
                           pl.BlockSpec((tm, Db), lambda j, i: (i, ga + j)), pl.BlockSpec((tm, Db), lambda j, i: (i, gs + j))],
                 out_specs=[out, out, out], out_shape=[shp, shp, shp])


def _mix_out(merged, wout, x, gn, name):
    S, D = x.shape
    tm = _tile(S, 256)

    def body(m_ref, w_ref, x_ref, gn_ref, xo_ref, hn_ref):
        xo = x_ref[...] + _nn(m_ref[...], w_ref[...])
        r = lax.rsqrt(jnp.mean(xo * xo, axis=-1, keepdims=True) + NORM_EPS)
        xo_ref[...] = xo
        hn_ref[...] = (xo * r * gn_ref[...]).astype(BF)

    row = pl.BlockSpec((tm, D), lambda i: (i, 0))
    return _call(body, grid=(S // tm,), name=name, args=[merged, wout, x, gn],
                 in_specs=[row, pl.BlockSpec((D, D), lambda i: (0, 0)), row, pl.BlockSpec((1, D), lambda i: (0, 0))],
                 out_specs=[row, row], out_shape=[jax.ShapeDtypeStruct((S, D), F32), jax.ShapeDtypeStruct((S, D), BF)])[0]


def _mix_bwd_gate(dmix, wout, ya, ys, proj, name):
    S, D = dmix.shape
    tm, tn = _tile(S, 512), 512
    ga, gs = R_GA // tn, (R_GA + D) // tn

    def body(dm_ref, w_ref, ya_ref, ys_ref, ga_ref, gs_ref, dya_ref, dys_ref, dga_ref, dgs_ref):
        dm = _nt(dm_ref[...], w_ref[...])
        sa = jax.nn.sigmoid(ga_ref[...].astype(F32))
        ss = jax.nn.sigmoid(gs_ref[...].astype(F32))
        dya_ref[...] = (dm * sa).astype(BF)
        dys_ref[...] = (dm * ss).astype(BF)
        dga_ref[...] = (dm * ya_ref[...].astype(F32) * sa * (1.0 - sa)).astype(BF)
        dgs_ref[...] = (dm * ys_ref[...].astype(F32) * ss * (1.0 - ss)).astype(BF)

    out = pl.BlockSpec((tm, tn), lambda i, n: (i, n))
    shp = jax.ShapeDtypeStruct((S, D), BF)
    return _call(body, grid=(S // tm, D // tn), name=name, args=[dmix, wout, ya, ys, proj, proj],
                 in_specs=[pl.BlockSpec((tm, D), lambda i, n: (i, 0)), pl.BlockSpec((tn, D), lambda i, n: (n, 0)), out, out,
                           pl.BlockSpec((tm, tn), lambda i, n: (i, ga + n)), pl.BlockSpec((tm, tn), lambda i, n: (i, gs + n))],
                 out_specs=[out] * 4, out_shape=[shp] * 4)[0]


def _att_sg_dout(dya, dys, watt, wsg, oatt, name, comm=()):
    S, D = dya.shape
    nb, _, Db = watt.shape
    tm = _tile(S, 512)

    def body(dya_ref, dys_ref, wa_ref, ws_ref, oa_ref, do_ref, dz_ref, dvec_ref, acca, accs):
        j = pl.program_id(1)

        @pl.when(j == 0)
        def _():
            acca[...] = jnp.zeros_like(acca)
            accs[...] = jnp.zeros_like(accs)

        acca[...] += _nt(dya_ref[...], wa_ref[0])
        accs[...] += _nt(dys_ref[...], ws_ref[0])

        @pl.when(j == nb - 1)
        def _():
            dov = acca[...]
            do_ref[...] = dov
            dz_ref[...] = accs[...].astype(BF)
            prod = dov * oa_ref[...].astype(F32)
            for hh in range(HEADS_PER_GROUP):
                cols = slice(hh * HEAD_DIM, (hh + 1) * HEAD_DIM)
                dvec_ref[:, cols] = jnp.broadcast_to(jnp.sum(prod[:, cols], axis=-1, keepdims=True), (tm, HEAD_DIM))

    blk = pl.BlockSpec((tm, Db), lambda i, j: (i, j))
    att = pl.BlockSpec((tm, GROUP_W), lambda i, j: (i, 0))
    return _call(body, grid=(S // tm, nb), name=name, args=[dya, dys, watt, wsg, oatt], comm=comm,
                 in_specs=[blk, blk, pl.BlockSpec((1, GROUP_W, Db), lambda i, j: (j, 0, 0)), pl.BlockSpec((1, SG_W, Db), lambda i, j: (j, 0, 0)), att],
                 out_specs=[att, pl.BlockSpec((tm, SG_W), lambda i, j: (i, 0)), att],
                 out_shape=[jax.ShapeDtypeStruct((S, GROUP_W), F32), jax.ShapeDtypeStruct((S, SG_W), BF), jax.ShapeDtypeStruct((S, GROUP_W), F32)],
                 scratch=[pltpu.VMEM((tm, GROUP_W), F32), pltpu.VMEM((tm, SG_W), F32)])[0]


def _small_allreduce(pack, name):
    R = pack.shape[0]

    def body(p_ref, o_ref, gath, send, recv):
        x, y, c = _place()
        me = 4 * x + 2 * y + c
        gath[me] = p_ref[...]
        copies = []
        for r in range(1, N_DEV):
            px, py, pc = _flip(x, r & 4), _flip(y, r & 2), _flip(c, r & 1)
            peer = 4 * px + 2 * py + pc
            mk = lambda dst: pltpu.make_async_remote_copy(src_ref=p_ref, dst_ref=dst, send_sem=send.at[r - 1], recv_sem=recv.at[r - 1],
                                                          device_id=(px, py, pc), device_id_type=MESH)
            snd = mk(gath.at[me])
            snd.start()
            copies.append((snd, mk(gath.at[peer])))
        for snd, rcv in copies:
            rcv.wait_recv()
            snd.wait_send()
        acc = gath[0]
        for s in range(1, N_DEV):
            acc = acc + gath[s]
        o_ref[...] = acc

    vm = pl.BlockSpec(memory_space=pltpu.VMEM)
    return pl.pallas_call(
        body, name=name, in_specs=[vm], out_specs=vm, out_shape=jax.ShapeDtypeStruct(pack.shape, F32),
        scratch_shapes=[pltpu.VMEM((N_DEV, R, 128), F32), pltpu.SemaphoreType.DMA((7,)), pltpu.SemaphoreType.DMA((7,))],
        compiler_params=pltpu.CompilerParams(vmem_limit_bytes=VMEM_LIMIT),
    )(pack)


def _row_tile(R, C, elems=262144):
    tr = R
    while tr * C > elems and tr % 32 == 0:
        tr //= 2
    return tr


def _pair_add(parts, other, name):
    _, R, C = parts.shape
    tr = _row_tile(R, C, 1048576)

    def body(c_ref, p_ref, o_ref, s_ref):
        s_ref[0] = (p_ref[0].astype(F32) + o_ref[0].astype(F32)).astype(BF)

    core = lax.axis_index("c").astype(jnp.int32).reshape(1)
    return pl.pallas_call(
        body, name=name,
        grid_spec=pltpu.PrefetchScalarGridSpec(
            num_scalar_prefetch=1, grid=(N_CHIP, R // tr),
            in_specs=[pl.BlockSpec((1, tr, C), lambda q, i, c: (2 * q + c[0], i, 0)), pl.BlockSpec((1, tr, C), lambda q, i, c: (q, i, 0))],
            out_specs=pl.BlockSpec((1, tr, C), lambda q, i, c: (q, i, 0))),
        out_shape=jax.ShapeDtypeStruct((N_CHIP, R, C), BF),
        compiler_params=pltpu.CompilerParams(dimension_semantics=("arbitrary", "arbitrary"), vmem_limit_bytes=VMEM_LIMIT),
    )(core, parts, other)


def _adamw(parts, w, m, v, name):
    ns, R, C = parts.shape
    tr = _row_tile(R, C, 524288)
    c1 = 1.0 - ADAM_B1 ** ADAM_STEP
    c2 = 1.0 - ADAM_B2 ** ADAM_STEP

    def body(p_ref, w_ref, m_ref, v_ref, g_ref, d_ref, nm_ref, nv_ref):
        g = p_ref[0].astype(F32)
        for s in range(1, ns):
            g = g + p_ref[s].astype(F32)
        mn = ADAM_B1 * m_ref[...] + (1.0 - ADAM_B1) * g
        vn = ADAM_B2 * v_ref[...] + (1.0 - ADAM_B2) * (g * g)
        g_ref[...] = g
        nm_ref[...] = mn
        nv_ref[...] = vn
        d_ref[...] = -ADAM_LR * ((mn / c1) / (jnp.sqrt(vn / c2) + ADAM_EPS) + ADAM_WD * w_ref[...])

    row = pl.BlockSpec((tr, C), lambda i: (i, 0))
    shp = jax.ShapeDtypeStruct((R, C), F32)
    return _call(body, grid=(R // tr,), name=name, args=[parts, w, m, v],
                 in_specs=[pl.BlockSpec((ns, tr, C), lambda i: (0, i, 0)), row, row, row], out_specs=[row] * 4, out_shape=[shp] * 4)[0]


def _pad_rows(a, rows):
    return jnp.pad(a, ((0, rows - a.shape[0]), (0, 0)))


def kernel(x, ffn1_norm, ffn1_w_gate, ffn1_w_up, ffn1_w_down, mix_norm, w_in, sg_ln_g, sg_ln_b, sg_w, sg_b, w_att_out, w_sg_out, w_out, ffn2_norm, ffn2_w_gate, ffn2_w_up, ffn2_w_down, final_norm, loss_target, m_ffn1_norm, m_ffn1_w_gate, m_ffn1_w_up, m_ffn1_w_down, m_mix_norm, m_w_in, m_sg_ln_g, m_sg_ln_b, m_sg_w, m_sg_b, m_w_att_out, m_w_sg_out, m_w_out, m_ffn2_norm, m_ffn2_w_gate, m_ffn2_w_up, m_ffn2_w_down, m_final_norm, v_ffn1_norm, v_ffn1_w_gate, v_ffn1_w_up, v_ffn1_w_down, v_mix_norm, v_w_in, v_sg_ln_g, v_sg_ln_b, v_sg_w, v_sg_b, v_w_att_out, v_w_sg_out, v_w_out, v_ffn2_norm, v_ffn2_w_gate, v_ffn2_w_up, v_ffn2_w_down, v_final_norm):
    S, D = x.shape[1], x.shape[2]
    Pb = w_in.shape[2]
    P = N_DEV * Pb
    assert P == GA_OFF + 2 * D and D % (N_DEV * 128) == 0 and S % (BLK * DILATIONS[-1]) == 0
    xs, tgt = x[0], loss_target[0]

    sharded = dict(ffn1_w_gate=ffn1_w_gate, ffn1_w_up=ffn1_w_up, ffn1_w_down=ffn1_w_down, w_in=w_in, w_att_out=w_att_out,
                   w_sg_out=w_sg_out, w_out=w_out, ffn2_w_gate=ffn2_w_gate, ffn2_w_up=ffn2_w_up, ffn2_w_down=ffn2_w_down)
    sb = {n: w[0].astype(BF) for n, w in sharded.items()}

    wg1, wu1 = _comm_only(_Gather([sb["ffn1_w_gate"], sb["ffn1_w_up"]]), "gather_ffn1")
    h1 = _rms_fwd(xs, ffn1_norm, "rms1")
    win_top, win_bot = sb["w_in"][: D // 2], sb["w_in"][D // 2:]
    (g1, u1, a1), ((wd1, win8a),) = _ffn_up(h1, wg1, wu1, "ffn1_up", comm=[_Gather([sb["ffn1_w_down"], win_top])])
    (x1, h2), ((win8b,),) = _ffn_down_norm(a1, wd1, xs, mix_norm, "ffn1_down", comm=[_Gather([win_bot], mid_frac=0.8)])
    win = jnp.concatenate([w8.transpose(1, 0, 2).reshape(D // 2, P) for w8 in (win8a, win8b)], axis=0)
    tm_proj = _tile(S, 1024)
    (qkv,), ((wg2,),) = _mm_nn(h2, win, tm_proj, 512, 0, U_OFF, F32, "proj_qkv", comm=[_Gather([sb["ffn2_w_gate"]])])
    (rest,), ((wu2,),) = _mm_nn(h2, win, tm_proj, 512, U_OFF, P, BF, "proj_rest", comm=[_Gather([sb["ffn2_w_up"]], mid_frac=0.9)])
    tabs = _rope_tables(S)
    rides = [[_Gather([sb["w_att_out"], sb["w_sg_out"]])], [_Gather([sb["w_out"]])], []]
    os, lses, late = [], [], []
    for gi, d in enumerate(DILATIONS):
        (o, l), got_here = _att_fwd(qkv, tabs, gi, d, f"att_fwd{gi}", comm=rides[gi])
        late += [w for g in got_here for w in g]
        os.append(o)
        lses.append(l)
    watt, wsg, wout8 = late
    wout = wout8.reshape(D, D)
    oatt, lse = _att_combine(os, lses, "att_combine")
    sgw = sg_w[0]
    sgbT = jnp.pad(sg_b[0].T, ((0, 0), (0, BLK - SG_GROUPS)))
    z = _sg_fwd(rest, sgw, sgbT, sg_ln_g, sg_ln_b, "sg_fwd")
    (ya, ys, merged), _ = _gate_merge(oatt, z, watt, wsg, rest, "gate_merge")
    x2, h3 = _mix_out(merged, wout, x1, ffn2_norm, "mix_out")
    (g3, u3, a3), ((wd2,),) = _ffn_up(h3, wg2, wu2, "ffn2_up", comm=[_Gather([sb["ffn2_w_down"]], mid_frac=0.7)])
    dx3, dyb3, d_final, loss_part = _ffn_down_loss(a3, wd2, x2, final_norm.reshape(1, D), tgt, "ffn2_down_loss")

    Fb = wg2.shape[2]
    Db = watt.shape[2]
    p_pad = -(-P // PROJ_TK) * PROJ_TK
    winT = jnp.concatenate([w8.transpose(0, 2, 1).reshape(P, D // 2) for w8 in (win8a, win8b)], axis=1)
    winT = jnp.pad(winT, ((0, p_pad - P), (0, 0)))
    (dg3, du3), _ = _ffn_bwd_act(dyb3, wd2, g3, u3, "ffn2_bwd_act")
    (dwd2,), _ = _ffn_dwd(a3, dyb3, "ffn2_dwd")
    (dwg2, dwu2), _ = _ffn_dwgu(h3, dg3, du3, "ffn2_dwgu")
    ffn2_parts = [dwd2, dwg2, dwu2]
    (dx2, dmixb, d_ffn2n), (ffn2_other,) = _dh_rms_bwd([(dg3, wg2), (du3, wu2)], True, Fb, x2, ffn2_norm, dx3, 1.0, "ffn2_dh",
                                                     comm=[_Swap(ffn2_parts)])
    ffn2_sums = [_pair_add(p, o, f"pair_ffn2_{i}") for i, (p, o) in enumerate(zip(ffn2_parts, ffn2_other))]

    dya, dys, dga, dgs = _mix_bwd_gate(dmixb, wout, ya, ys, rest, "mix_bwd_gate")
    (dwout,), _ = _mm_tn(merged, dmixb, _tile(D, 1024), _tile(D, 1024), _tile(S, 1024), False, "dw_out")
    do, dz, dvec = _att_sg_dout(dya, dys, watt, wsg, oatt, "att_sg_dout")
    (dwatt,), _ = _mm_tn(oatt, dya, GROUP_W, Db, _tile(S, 1024), True, "dw_att")
    (dwsg,), _ = _mm_tn(z, dys, SG_W, Db, _tile(S, 1024), True, "dw_sg")
    mix_parts = [dwout.reshape(N_DEV, D // N_DEV, D), dwatt, dwsg]
    du, dvs, d_sgw, d_sgbT, d_lng, d_lnb = _sg_bwd(rest, dz, sgw, sgbT, sg_ln_g, sg_ln_b, "sg_bwd")
    dqs, dks, dvs_att, ffn2_got = [], [], [], []
    for gi, d in enumerate(DILATIONS):
        ride = [_Ici([ffn2_sums[gi - 1]])] if gi else []
        (dq, dk, dv), got_here = _att_bwd(qkv, tabs, do, lse, dvec, gi, d, f"att_bwd{gi}", comm=ride)
        ffn2_got += [g[0] for g in got_here]
        dqs.append(dq)
        dks.append(dk)
        dvs_att.append(dv)
    dproj = jnp.concatenate([t.astype(BF) for t in dqs + dks + dvs_att] + [du, dvs, dga, dgs, jnp.zeros((S, p_pad - P), BF)], axis=1)
    (dx1, dyb1, d_mixn), ((ffn2_last,), mix_other) = _dh_rms_bwd([(dproj, winT)], False, PROJ_TK, x1, mix_norm, dx2, 0.5, "proj_dh",
                                                               comm=[_Ici([ffn2_sums[2]]), _Swap(mix_parts)])
    ffn2_got.append(ffn2_last)
    mix_sums = [_pair_add(p, o, f"pair_mix_{i}") for i, (p, o) in enumerate(zip(mix_parts, mix_other))]
    (dwd1,), (mix_got,) = _ffn_dwd(a1, dyb1, "ffn1_dwd", comm=[_Ici(mix_sums)])
    (dwin,), ((wd1_other,),) = _mm_tn(h2, dproj, D, 512, _tile(S, 2048), False, "dw_in", comm=[_Swap([dwd1])], ncols=P)
    dwin = dwin.reshape(D, N_DEV, Pb).transpose(1, 0, 2)
    wd1_sum = _pair_add(dwd1, wd1_other, "pair_wd1")
    (dg1, du1), ((wd1_got,), (win_other,)) = _ffn_bwd_act(dyb1, wd1, g1, u1, "ffn1_bwd_act", comm=[_Ici([wd1_sum]), _Swap([dwin])])
    win_sum = _pair_add(dwin, win_other, "pair_win")
    rows = lambda a: a.reshape(-1, 128)
    pad8 = lambda a: _pad_rows(a, -(-a.shape[0] // 8) * 8)
    small = [("sg_w", rows(d_sgw), sg_w, m_sg_w, v_sg_w), ("mix_norm", rows(d_mixn), mix_norm, m_mix_norm, v_mix_norm),
             ("ffn2_norm", rows(d_ffn2n), ffn2_norm, m_ffn2_norm, v_ffn2_norm), ("final_norm", rows(d_final), final_norm, m_final_norm, v_final_norm),
             ("sg_ln_g", rows(d_lng), sg_ln_g, m_sg_ln_g, v_sg_ln_g), ("sg_ln_b", rows(d_lnb), sg_ln_b, m_sg_ln_b, v_sg_ln_b),
             ("sg_b", d_sgbT[:, :SG_GROUPS].T, sg_b, m_sg_b, v_sg_b)]
    gpack = jnp.concatenate([pad8(g) for _, g, _, _, _ in small] + [pad8(loss_part)], axis=0)
    (dwg1, dwu1), ((win_got,), (gpacks,)) = _ffn_dwgu(h1, dg1, du1, "ffn1_dwgu", comm=[_Ici([win_sum]), _Spread([gpack])])
    gu_parts = [dwg1, dwu1]
    gu_other = _comm_only(_Swap(gu_parts), "swap_gu1")
    gu_sums = [_pair_add(p, o, f"pair_gu1_{i}") for i, (p, o) in enumerate(zip(gu_parts, gu_other))]
    (dx0, _, d_ffn1n), (gu_got,) = _dh_rms_bwd([(dg1, wg1), (du1, wu1)], True, Fb, xs, ffn1_norm, dx1, 1.0, "ffn1_dh",
                                               comm=[_Ici(gu_sums)])

    got = dict(ffn2_w_down=ffn2_got[0], ffn2_w_gate=ffn2_got[1], ffn2_w_up=ffn2_got[2], w_out=mix_got[0], w_att_out=mix_got[1],
               w_sg_out=mix_got[2], w_in=win_got, ffn1_w_gate=gu_got[0], ffn1_w_up=gu_got[1], ffn1_w_down=wd1_got)
    moments = dict(ffn1_w_gate=(m_ffn1_w_gate, v_ffn1_w_gate), ffn1_w_up=(m_ffn1_w_up, v_ffn1_w_up),
                   ffn1_w_down=(m_ffn1_w_down, v_ffn1_w_down), w_in=(m_w_in, v_w_in), w_att_out=(m_w_att_out, v_w_att_out),
                   w_sg_out=(m_w_sg_out, v_w_sg_out), w_out=(m_w_out, v_w_out), ffn2_w_gate=(m_ffn2_w_gate, v_ffn2_w_gate),
                   ffn2_w_up=(m_ffn2_w_up, v_ffn2_w_up), ffn2_w_down=(m_ffn2_w_down, v_ffn2_w_down))
    res = {}
    for n in sharded:
        mm, vv = moments[n]
        outs = _adamw(got[n], sharded[n][0], mm[0], vv[0], "adamw_" + n)
        res[n] = [o[None] for o in outs]

    zero8 = jnp.zeros((8, 128), F32)
    wpack = jnp.concatenate([pad8(rows(w)) for _, _, w, _, _ in small] + [zero8], axis=0)
    mpack = jnp.concatenate([pad8(rows(m)) for _, _, _, m, _ in small] + [zero8], axis=0)
    vpack = jnp.concatenate([pad8(rows(v)) for _, _, _, _, v in small] + [zero8], axis=0)
    packs = _adamw(gpacks, wpack, mpack, vpack, "adamw_small")
    off = 0
    for n, g, w, _, _ in small:
        r = g.shape[0]
        res[n] = [p[off:off + r].reshape(w.shape) for p in packs]
        off += -(-r // 8) * 8
    loss = packs[0][off, 0]
    g_first = _small_allreduce(rows(d_ffn1n), "allreduce_ffn1_norm")
    res["ffn1_norm"] = [p.reshape(ffn1_norm.shape) for p in
                        _adamw(g_first[None], rows(ffn1_norm), rows(m_ffn1_norm), rows(v_ffn1_norm), "adamw_ffn1_norm")]

    order = ["ffn1_norm", "ffn1_w_gate", "ffn1_w_up", "ffn1_w_down", "mix_norm", "w_in", "sg_ln_g", "sg_ln_b", "sg_w", "sg_b",
             "w_att_out", "w_sg_out", "w_out", "ffn2_norm", "ffn2_w_gate", "ffn2_w_up", "ffn2_w_down", "final_norm"]
    return (loss, dx0[None], *[res[n][0] for n in order], *[res[n][1] for n in order], *[res[n][2] for n in order],
            *[res[n][3] for n in order])
```

```python
import math

import jax
import jax.numpy as jnp
from jax import lax
from jax.experimental import pallas as pl
from jax.experimental.pallas import tpu as pltpu

BF = jnp.bfloat16
F32 = jnp.float32
MESH = pl.DeviceIdType.MESH
N_DEV = 8
N_CHIP = 4

HEAD_DIM = 128
HEADS_PER_GROUP = 4
GROUP_W = HEADS_PER_GROUP * HEAD_DIM
DILATIONS = (1, 4, 16)
ATT_W = len(DILATIONS) * GROUP_W
SG_W = 1536
SG_GROUPS = 12
BLK = 128
ROPE_DIM = 32
ROPE_THETA = 500000.0
NORM_EPS = 1e-6
LN_EPS = 1e-5
Q_OFF, K_OFF, V_OFF, U_OFF, VS_OFF, GA_OFF = 0, ATT_W, 2 * ATT_W, 3 * ATT_W, 3 * ATT_W + SG_W, 3 * ATT_W + 2 * SG_W

ADAM_LR, ADAM_B1, ADAM_B2, ADAM_EPS, ADAM_WD, ADAM_STEP = 0.001, 0.9, 0.999, 1e-08, 0.01, 10

VMEM_LIMIT = 56 * 1024 * 1024
NEG = -1e30
ANY = pl.BlockSpec(memory_space=pl.ANY)
EPI_ROWS = 128
ACC_COLS = 512
FFN_PAIR = 2
PROJ_TK = 1536
R_U, R_VS, R_GA = 0, SG_W, 2 * SG_W


def _once(shape, index_map):
    return pl.BlockSpec(shape, index_map, pipeline_mode=pl.Buffered(1))


def _tile(n, pref):
    t = min(n, pref)
    while n % t:
        t //= 2
    return t


def _nt(a, b):
    return lax.dot_general(a, b, (((1,), (1,)), ((), ())), preferred_element_type=F32)


def _tn(a, b):
    return lax.dot_general(a, b, (((0,), (0,)), ((), ())), preferred_element_type=F32)


def _nn(a, b):
    return jnp.dot(a, b, preferred_element_type=F32)


def _acc_dots(acc_ref, terms, transposed_rhs=False):
    n = acc_ref.shape[1]
    width = min(n, ACC_COLS)
    for c0 in range(0, n, width):
        cols = slice(c0, c0 + width)
        tot = None
        for lhs, rhs in terms:
            part = _nt(lhs, rhs(cols)) if transposed_rhs else _nn(lhs, rhs(cols))
            tot = part if tot is None else tot + part
        acc_ref[:, cols] += tot


def _gelu(x):
    return 0.5 * x * (1.0 + lax.erf(x * (2.0 ** -0.5)))


def _gelu_grad(x):
    return 0.5 * (1.0 + lax.erf(x * (2.0 ** -0.5))) + x * jnp.exp(-0.5 * x * x) * (1.0 / math.sqrt(2.0 * math.pi))


def _place():
    x, y, c = lax.axis_index("x"), lax.axis_index("y"), lax.axis_index("c")
    return x, y, c


def _flip(v, bit):
    return 1 - v if bit else v


class _Gather:
    def __init__(self, shards, mid_frac=1.0, relay_frac=0.5):
        self.arrays = list(shards)
        self.relay_frac = relay_frac
        self.mid_frac = mid_frac
        nw = len(shards)
        self.out_shape = [jax.ShapeDtypeStruct((N_DEV,) + s.shape, s.dtype) for s in shards]
        self.scratch = [pltpu.SemaphoreType.DMA((nw, 7)), pltpu.SemaphoreType.DMA((nw, 7)), pltpu.SemaphoreType.DMA((nw,))]

    def _parts(self, ins, outs, sems):
        x, y, c = _place()
        send, recv, loc = sems
        south = c == 0
        near = (jnp.where(south, x, 1 - x), jnp.where(south, 1 - y, y), c)
        far = (jnp.where(south, 1 - x, x), jnp.where(south, y, 1 - y), c)
        diag = (1 - x, 1 - y, c)

        def copy(k, s, block, to, src=None):
            dst = outs[k].at[4 * block[0] + 2 * block[1] + block[2]]
            return pltpu.make_async_remote_copy(src_ref=dst if src is None else src, dst_ref=dst, send_sem=send.at[k, s],
                                                recv_sem=recv.at[k, s], device_id=to, device_id_type=MESH)

        def first(k):
            me = (x, y, c)
            return [copy(k, 0, me, (x, y, 1 - c), src=ins[k]), copy(k, 1, me, (1 - x, y, c), src=ins[k]),
                    copy(k, 2, me, (x, 1 - y, c), src=ins[k])]

        def local(k):
            return pltpu.make_async_copy(ins[k], outs[k].at[4 * x + 2 * y + c], loc.at[k])

        return x, y, c, near, far, diag, copy, first, local

    def start(self, ins, outs, sems):
        *_, first, local = self._parts(ins, outs, sems)
        for k in range(len(ins)):
            local(k).start()
            for cp in first(k):
                cp.start()

    def relay(self, ins, outs, sems):
        x, y, c, near, far, _, copy, _, _ = self._parts(ins, outs, sems)
        for k in range(len(ins)):
            copy(k, 2 - c, near, (x, y, c)).wait_recv()
            copy(k, 3, near, far).start()
            copy(k, 5 - c, near, (x, y, 1 - c)).start()

    def mid(self, ins, outs, sems):
        x, y, c, _, far, diag, copy, _, _ = self._parts(ins, outs, sems)
        for k in range(len(ins)):
            copy(k, 1 + c, far, (x, y, c)).wait_recv()
            copy(k, 4 + c, far, (x, y, 1 - c)).start()
            copy(k, 3, diag, (x, y, c)).wait_recv()
            copy(k, 6, diag, (x, y, 1 - c)).start()

    def finish(self, ins, outs, sems):
        x, y, c, near, _, _, copy, first, local = self._parts(ins, outs, sems)
        sib = (x, y, 1 - c)
        for k in range(len(ins)):
            copy(k, 0, sib, (x, y, c)).wait_recv()
            copy(k, 4, (1 - x, y, 1 - c), (x, y, c)).wait_recv()
            copy(k, 5, (x, 1 - y, 1 - c), (x, y, c)).wait_recv()
            copy(k, 6, (1 - x, 1 - y, 1 - c), (x, y, c)).wait_recv()
        for k in range(len(ins)):
            for cp in first(k):
                cp.wait_send()
            for s in (3, 4, 5, 6):
                copy(k, s, near, sib).wait_send()
            local(k).wait()


class _Swap:
    def __init__(self, parts):
        self.arrays = list(parts)
        nw = len(parts)
        self.out_shape = [jax.ShapeDtypeStruct((N_CHIP,) + p.shape[1:], p.dtype) for p in parts]
        self.scratch = [pltpu.SemaphoreType.DMA((nw, N_CHIP)), pltpu.SemaphoreType.DMA((nw, N_CHIP))]

    def _copy(self, ins, outs, sems, k, q):
        x, y, c = _place()
        return pltpu.make_async_remote_copy(src_ref=ins[k].at[2 * q + 1 - c], dst_ref=outs[k].at[q], send_sem=sems[0].at[k, q],
                                            recv_sem=sems[1].at[k, q], device_id=(x, y, 1 - c), device_id_type=MESH)

    mid_frac = None

    def start(self, ins, outs, sems):
        for k in range(len(ins)):
            for q in range(N_CHIP):
                self._copy(ins, outs, sems, k, q).start()

    def finish(self, ins, outs, sems):
        for k in range(len(ins)):
            for q in range(N_CHIP):
                self._copy(ins, outs, sems, k, q).wait()


class _Ici:
    mid_frac = None

    def __init__(self, sums):
        self.arrays = list(sums)
        nw = len(sums)
        self.out_shape = [jax.ShapeDtypeStruct(s.shape, s.dtype) for s in sums]
        self.scratch = [pltpu.SemaphoreType.DMA((nw, 3)), pltpu.SemaphoreType.DMA((nw, 3)), pltpu.SemaphoreType.DMA((nw,))]

    def _copies(self, ins, outs, sems, k):
        x, y, c = _place()
        myq = 2 * x + y
        out = []
        for r in range(1, N_CHIP):
            px, py = _flip(x, r & 2), _flip(y, r & 1)
            pq = 2 * px + py
            mk = lambda dst: pltpu.make_async_remote_copy(src_ref=ins[k].at[pq], dst_ref=dst, send_sem=sems[0].at[k, r - 1],
                                                          recv_sem=sems[1].at[k, r - 1], device_id=(px, py, c), device_id_type=MESH)
            out.append((mk(outs[k].at[myq]), mk(outs[k].at[pq])))
        return out, pltpu.make_async_copy(ins[k].at[myq], outs[k].at[myq], sems[2].at[k])

    def start(self, ins, outs, sems):
        for k in range(len(ins)):
            remote, local = self._copies(ins, outs, sems, k)
            local.start()
            for snd, _ in remote:
                snd.start()

    def finish(self, ins, outs, sems):
        for k in range(len(ins)):
            remote, local = self._copies(ins, outs, sems, k)
            for snd, rcv in remote:
                rcv.wait_recv()
                snd.wait_send()
            local.wait()


class _Spread:
    mid_frac = None

    def __init__(self, arrays):
        self.arrays = list(arrays)
        nw = len(arrays)
        self.out_shape = [jax.ShapeDtypeStruct((N_DEV,) + a.shape, a.dtype) for a in arrays]
        self.scratch = [pltpu.SemaphoreType.DMA((nw, 7)), pltpu.SemaphoreType.DMA((nw, 7)), pltpu.SemaphoreType.DMA((nw,))]

    def _copies(self, ins, outs, sems, k):
        x, y, c = _place()
        me = 4 * x + 2 * y + c
        out = []
        for r in range(1, N_DEV):
            px, py, pc = _flip(x, r & 4), _flip(y, r & 2), _flip(c, r & 1)
            peer = 4 * px + 2 * py + pc
            mk = lambda dst: pltpu.make_async_remote_copy(src_ref=ins[k], dst_ref=dst, send_sem=sems[0].at[k, r - 1],
                                                          recv_sem=sems[1].at[k, r - 1], device_id=(px, py, pc), device_id_type=MESH)
            out.append((mk(outs[k].at[me]), mk(outs[k].at[peer])))
        return out, pltpu.make_async_copy(ins[k], outs[k].at[me], sems[2].at[k])

    def start(self, ins, outs, sems):
        for k in range(len(ins)):
            remote, local = self._copies(ins, outs, sems, k)
            local.start()
            for snd, _ in remote:
                snd.start()

    def finish(self, ins, outs, sems):
        for k in range(len(ins)):
            remote, local = self._copies(ins, outs, sems, k)
            for snd, rcv in remote:
                rcv.wait_recv()
                snd.wait_send()
            local.wait()


def _call(body, *, grid, in_specs, out_specs, out_shape, name, args, scratch=(), comm=()):
    comm = list(comm)
    n_in, n_out, n_scr = len(in_specs), len(out_specs), len(scratch)
    total = math.prod(grid) if grid else 1

    def wrapped(*refs):
        p = n_in
        cin = []
        for cm in comm:
            cin.append(refs[p:p + len(cm.arrays)])
            p += len(cm.arrays)
        own_out = refs[p:p + n_out]
        p += n_out
        cout = []
        for cm in comm:
            cout.append(refs[p:p + len(cm.arrays)])
            p += len(cm.arrays)
        own_scr = refs[p:p + n_scr]
        p += n_scr
        csem = []
        for cm in comm:
            csem.append(refs[p:p + len(cm.scratch)])
            p += len(cm.scratch)
        step = 0
        for axis, g in enumerate(grid):
            step = step * g + pl.program_id(axis)

        def at(when, what):
            if total == 1:
                what()
            else:
                pl.when(step == when)(what)

        def starts():
            for cm, i, o, s in zip(comm, cin, cout, csem):
                cm.start(i, o, s)

        def finishes():
            for cm, i, o, s in zip(comm, cin, cout, csem):
                cm.finish(i, o, s)

        if comm:
            at(0, starts)
        if body is not None:
            body(*refs[:n_in], *own_out, *own_scr)
        for cm, i, o, s in zip(comm, cin, cout, csem):
            if cm.mid_frac is not None:
                at(min(total - 1, int(total * cm.relay_frac)), lambda cm=cm, i=i, o=o, s=s: cm.relay(i, o, s))
                at(min(total - 1, int(total * cm.mid_frac)), lambda cm=cm, i=i, o=o, s=s: cm.mid(i, o, s))
        if comm:
            at(total - 1, finishes)

    kw = dict(grid=tuple(grid)) if grid else {}
    outs = pl.pallas_call(
        wrapped, name=name, **kw,
        in_specs=list(in_specs) + [ANY for cm in comm for _ in cm.arrays],
        out_specs=list(out_specs) + [ANY for cm in comm for _ in cm.arrays],
        out_shape=list(out_shape) + [s for cm in comm for s in cm.out_shape],
        scratch_shapes=list(scratch) + [s for cm in comm for s in cm.scratch],
        compiler_params=pltpu.CompilerParams(dimension_semantics=("arbitrary",) * len(grid), vmem_limit_bytes=VMEM_LIMIT),
    )(*args, *[a for cm in comm for a in cm.arrays])
    own, p, per = list(outs[:n_out]), n_out, []
    for cm in comm:
        per.append(list(outs[p:p + len(cm.arrays)]))
        p += len(cm.arrays)
    return own, per


def _comm_only(cm, name):
    return _call(None, grid=(), in_specs=[], out_specs=[], out_shape=[], name=name, args=[], comm=[cm])[1][0]


def _rms_fwd(x, g, name):
    S, D = x.shape
    tm = _tile(S, 512)

    def body(x_ref, g_ref, o_ref):
        xv = x_ref[...]
        r = lax.rsqrt(jnp.mean(xv * xv, axis=-1, keepdims=True) + NORM_EPS)
        o_ref[...] = (xv * r * g_ref[...]).astype(BF)

    return _call(body, grid=(S // tm,), name=name, args=[x, g],
                 in_specs=[pl.BlockSpec((tm, D), lambda i: (i, 0)), pl.BlockSpec((1, D), lambda i: (0, 0))],
                 out_specs=[pl.BlockSpec((tm, D), lambda i: (i, 0))], out_shape=[jax.ShapeDtypeStruct((S, D), BF)])[0][0]


def _ffn_up(h, wg, wu, name, comm=()):
    S, D = h.shape
    nb, _, Fb = wg.shape
    tm = _tile(S, 512)

    def body(h_ref, wg_ref, wu_ref, g_ref, u_ref, a_ref):
        hv = h_ref[...]
        g = _nn(hv, wg_ref[0])
        u = _nn(hv, wu_ref[0])
        g_ref[0] = g.astype(BF)
        u_ref[0] = u.astype(BF)
        a_ref[0] = (g * jax.nn.sigmoid(g) * u).astype(BF)

    act = pl.BlockSpec((1, tm, Fb), lambda j, i: (j, i, 0))
    w = pl.BlockSpec((1, D, Fb), lambda j, i: (j, 0, 0))
    shp = jax.ShapeDtypeStruct((nb, S, Fb), BF)
    return _call(body, grid=(nb, S // tm), name=name, args=[h, wg, wu], comm=comm,
                 in_specs=[pl.BlockSpec((tm, D), lambda j, i: (i, 0)), w, w], out_specs=[act, act, act], out_shape=[shp, shp, shp])


def _ffn_down_norm(a, wd, x, gn, name, comm=()):
    nb, S, Fb = a.shape
    D = wd.shape[2]
    tm = _tile(S, 512)

    nj = nb // FFN_PAIR

    def body(a_ref, wd_ref, x_ref, gn_ref, xo_ref, hn_ref, acc_ref):
        j = pl.program_id(1)

        @pl.when(j == 0)
        def _():
            acc_ref[...] = jnp.zeros_like(acc_ref)

        _acc_dots(acc_ref, [(a_ref[b], lambda cols, b=b: wd_ref[b, :, cols]) for b in range(FFN_PAIR)])

        @pl.when(j == nj - 1)
        def _():
            def chunk(t, carry):
                rows = pl.ds(pl.multiple_of(t * EPI_ROWS, EPI_ROWS), EPI_ROWS)
                xo = x_ref[rows, :] + 0.5 * acc_ref[rows, :]
                r = lax.rsqrt(jnp.mean(xo * xo, axis=-1, keepdims=True) + NORM_EPS)
                xo_ref[rows, :] = xo
                hn_ref[rows, :] = (xo * r * gn_ref[...]).astype(BF)
                return carry

            lax.fori_loop(0, tm // EPI_ROWS, chunk, 0)

    row = pl.BlockSpec((tm, D), lambda i, j: (i, 0))
    return _call(body, grid=(S // tm, nj), name=name, args=[a, wd, x, gn], comm=comm,
                 in_specs=[pl.BlockSpec((FFN_PAIR, tm, Fb), lambda i, j: (j, i, 0)), pl.BlockSpec((FFN_PAIR, Fb, D), lambda i, j: (j, 0, 0)),
                           _once((tm, D), lambda i, j: (i, 0)), pl.BlockSpec((1, D), lambda i, j: (0, 0))],
                 out_specs=[row, row], out_shape=[jax.ShapeDtypeStruct((S, D), F32), jax.ShapeDtypeStruct((S, D), BF)],
                 scratch=[pltpu.VMEM((tm, D), F32)])


def _ffn_down_loss(a, wd, x, gf, tgt, name):
    nb, S, Fb = a.shape
    D = wd.shape[2]
    tm = _tile(S, 512)

    nj = nb // FFN_PAIR

    def body(a_ref, wd_ref, x_ref, gf_ref, t_ref, dx_ref, dxb_ref, dgf_ref, loss_ref, acc_ref):
        i, j = pl.program_id(0), pl.program_id(1)

        @pl.when(j == 0)
        def _():
            acc_ref[...] = jnp.zeros_like(acc_ref)

        _acc_dots(acc_ref, [(a_ref[b], lambda cols, b=b: wd_ref[b, :, cols]) for b in range(FFN_PAIR)])

        @pl.when((j == nj - 1) & (i == 0))
        def _():
            dgf_ref[...] = jnp.zeros_like(dgf_ref)
            loss_ref[...] = jnp.zeros_like(loss_ref)

        @pl.when(j == nj - 1)
        def _():
            def chunk(t, carry):
                rows = pl.ds(pl.multiple_of(t * EPI_ROWS, EPI_ROWS), EPI_ROWS)
                xo = x_ref[rows, :] + 0.5 * acc_ref[rows, :]
                r = lax.rsqrt(jnp.mean(xo * xo, axis=-1, keepdims=True) + NORM_EPS)
                xh = xo * r
                gf = gf_ref[...]
                e = xh * gf - t_ref[rows, :]
                loss_ref[...] += jnp.sum(jnp.mean(e * e, axis=-1, keepdims=True), axis=0, keepdims=True) * 0.5
                dy = e * (1.0 / D)
                dgf_ref[...] += jnp.sum(dy * xh, axis=0, keepdims=True)
                dxh = dy * gf
                dx = r * (dxh - xh * jnp.mean(dxh * xh, axis=-1, keepdims=True))
                dx_ref[rows, :] = dx
                dxb_ref[rows, :] = (0.5 * dx).astype(BF)
                return carry

            lax.fori_loop(0, tm // EPI_ROWS, chunk, 0)

    row = pl.BlockSpec((tm, D), lambda i, j: (i, 0))
    once = _once((tm, D), lambda i, j: (i, 0))
    vec = pl.BlockSpec((1, D), lambda i, j: (0, 0))
    return _call(body, grid=(S // tm, nj), name=name, args=[a, wd, x, gf, tgt],
                 in_specs=[pl.BlockSpec((FFN_PAIR, tm, Fb), lambda i, j: (j, i, 0)), pl.BlockSpec((FFN_PAIR, Fb, D), lambda i, j: (j, 0, 0)),
                           once, vec, once],
                 out_specs=[row, row, vec, pl.BlockSpec((1, 128), lambda i, j: (0, 0))],
                 out_shape=[jax.ShapeDtypeStruct((S, D), F32), jax.ShapeDtypeStruct((S, D), BF), jax.ShapeDtypeStruct((1, D), F32),
                            jax.ShapeDtypeStruct((1, 128), F32)],
                 scratch=[pltpu.VMEM((tm, D), F32)])[0]


def _ffn_bwd_act(dyb, wd, g, u, name, comm=()):
    S, D = dyb.shape
    nb, Fb, _ = wd.shape
    tm = _tile(S, 512)

    def body(dy_ref, wd_ref, g_ref, u_ref, dg_ref, du_ref):
        da = _nt(dy_ref[...], wd_ref[0])
        gv = g_ref[0].astype(F32)
        uv = u_ref[0].astype(F32)
        sg = jax.nn.sigmoid(gv)
        du_ref[0] = (da * gv * sg).astype(BF)
        dg_ref[0] = (da * uv * sg * (1.0 + gv * (1.0 - sg))).astype(BF)

    act = pl.BlockSpec((1, tm, Fb), lambda j, i: (j, i, 0))
    shp = jax.ShapeDtypeStruct((nb, S, Fb), BF)
    return _call(body, grid=(nb, S // tm), name=name, args=[dyb, wd, g, u], comm=comm,
                 in_specs=[pl.BlockSpec((tm, D), lambda j, i: (i, 0)), pl.BlockSpec((1, Fb, D), lambda j, i: (j, 0, 0)), act, act],
                 out_specs=[act, act], out_shape=[shp, shp])


def _ffn_dwd(a, dyb, name, comm=()):
    nb, S, Fb = a.shape
    D = dyb.shape[1]
    ts = _tile(S, 512)
    ns = S // ts

    def body(a_ref, dy_ref, o_ref, acc_ref):
        s = pl.program_id(1)

        @pl.when(s == 0)
        def _():
            acc_ref[...] = jnp.zeros_like(acc_ref)

        acc_ref[...] += _tn(a_ref[0], dy_ref[...])

        @pl.when(s == ns - 1)
        def _():
            o_ref[0] = acc_ref[...].astype(BF)

    return _call(body, grid=(nb, ns), name=name, args=[a, dyb], comm=comm,
                 in_specs=[pl.BlockSpec((1, ts, Fb), lambda j, s: (j, s, 0)), pl.BlockSpec((ts, D), lambda j, s: (s, 0))],
                 out_specs=[pl.BlockSpec((1, Fb, D), lambda j, s: (j, 0, 0))], out_shape=[jax.ShapeDtypeStruct((nb, Fb, D), BF)],
                 scratch=[pltpu.VMEM((Fb, D), F32)])


def _ffn_dwgu(h, dg, du, name, comm=()):
    S, D = h.shape
    nb, _, Fb = dg.shape
    ts = _tile(S, 512)
    ns = S // ts

    def body(h_ref, dg_ref, du_ref, og_ref, ou_ref, accg_ref, accu_ref):
        s = pl.program_id(1)

        @pl.when(s == 0)
        def _():
            accg_ref[...] = jnp.zeros_like(accg_ref)
            accu_ref[...] = jnp.zeros_like(accu_ref)

        hv = h_ref[...]
        accg_ref[...] += _tn(hv, dg_ref[0])
        accu_ref[...] += _tn(hv, du_ref[0])

        @pl.when(s == ns - 1)
        def _():
            og_ref[0] = accg_ref[...].astype(BF)
            ou_ref[0] = accu_ref[...].astype(BF)

    act = pl.BlockSpec((1, ts, Fb), lambda j, s: (j, s, 0))
    out = pl.BlockSpec((1, D, Fb), lambda j, s: (j, 0, 0))
    shp = jax.ShapeDtypeStruct((nb, D, Fb), BF)
    return _call(body, grid=(nb, ns), name=name, args=[h, dg, du], comm=comm,
                 in_specs=[pl.BlockSpec((ts, D), lambda j, s: (s, 0)), act, act], out_specs=[out, out], out_shape=[shp, shp],
                 scratch=[pltpu.VMEM((D, Fb), F32), pltpu.VMEM((D, Fb), F32)])


def _dh_rms_bwd(pairs, blocked, tk, x, gn, dxo, out_scale, name, comm=()):
    S, D = x.shape
    nk = pairs[0][0].shape[0] if blocked else pairs[0][0].shape[1] // tk
    tm = _tile(S, 512)
    npair = len(pairs)

    def body(*refs):
        ins = refs[: 2 * npair]
        x_ref, gn_ref, dxo_ref, dx_ref, dxb_ref, dgn_ref, acc_ref = refs[2 * npair:]
        i, k = pl.program_id(0), pl.program_id(1)

        @pl.when(k == 0)
        def _():
            acc_ref[...] = jnp.zeros_like(acc_ref)

        if blocked:
            terms = [(ins[2 * p][0], lambda cols, r=ins[2 * p + 1]: r[0, cols, :]) for p in range(npair)]
        else:
            terms = [(ins[2 * p][...], lambda cols, r=ins[2 * p + 1]: r[:, cols]) for p in range(npair)]
        _acc_dots(acc_ref, terms, transposed_rhs=blocked)

        @pl.when((k == nk - 1) & (i == 0))
        def _():
            dgn_ref[...] = jnp.zeros_like(dgn_ref)

        @pl.when(k == nk - 1)
        def _():
            def chunk(t, carry):
                rows = pl.ds(pl.multiple_of(t * EPI_ROWS, EPI_ROWS), EPI_ROWS)
                xv = x_ref[rows, :]
                r = lax.rsqrt(jnp.mean(xv * xv, axis=-1, keepdims=True) + NORM_EPS)
                xh = xv * r
                dh = acc_ref[rows, :]
                dgn_ref[...] += jnp.sum(dh * xh, axis=0, keepdims=True)
                dxh = dh * gn_ref[...]
                dx = dxo_ref[rows, :] + r * (dxh - xh * jnp.mean(dxh * xh, axis=-1, keepdims=True))
                dx_ref[rows, :] = dx
                dxb_ref[rows, :] = (out_scale * dx).astype(BF)
                return carry

            lax.fori_loop(0, tm // EPI_ROWS, chunk, 0)

    if blocked:
        lspec = pl.BlockSpec((1, tm, tk), lambda i, k: (k, i, 0))
        rspec = pl.BlockSpec((1, D, tk), lambda i, k: (k, 0, 0))
    else:
        lspec = pl.BlockSpec((tm, tk), lambda i, k: (i, k))
        rspec = pl.BlockSpec((tk, D), lambda i, k: (k, 0))
    row = pl.BlockSpec((tm, D), lambda i, k: (i, 0))
    once = _once((tm, D), lambda i, k: (i, 0))
    vec = pl.BlockSpec((1, D), lambda i, k: (0, 0))
    flat = [t for pr in pairs for t in pr]
    return _call(body, grid=(S // tm, nk), name=name, args=[*flat, x, gn, dxo], comm=comm,
                 in_specs=[lspec, rspec] * npair + [once, vec, once], out_specs=[row, row, vec],
                 out_shape=[jax.ShapeDtypeStruct((S, D), F32), jax.ShapeDtypeStruct((S, D), BF), jax.ShapeDtypeStruct((1, D), F32)],
                 scratch=[pltpu.VMEM((tm, D), F32)])


def _mm_nn(a, b, tm, tn, col0, col1, dtype, name, comm=()):
    M, K = a.shape
    n0, nn = col0 // tn, (col1 - col0) // tn

    def body(a_ref, b_ref, o_ref):
        o_ref[...] = _nn(a_ref[...], b_ref[...]).astype(dtype)

    return _call(body, grid=(nn, M // tm), name=name, args=[a, b], comm=comm,
                 in_specs=[pl.BlockSpec((tm, K), lambda n, i: (i, 0)), pl.BlockSpec((K, tn), lambda n, i: (0, n0 + n))],
                 out_specs=[pl.BlockSpec((tm, tn), lambda n, i: (i, n))], out_shape=[jax.ShapeDtypeStruct((M, nn * tn), dtype)])


def _mm_tn(a, b, tm, tn, ts, blocked, name, comm=(), ncols=None):
    S, M = a.shape
    N = b.shape[1] if ncols is None else ncols
    ns = S // ts

    def body(a_ref, b_ref, o_ref, acc_ref):
        s = pl.program_id(2)

        @pl.when(s == 0)
        def _():
            acc_ref[...] = jnp.zeros_like(acc_ref)

        acc_ref[...] += _tn(a_ref[...], b_ref[...])

        @pl.when(s == ns - 1)
        def _():
            if blocked:
                o_ref[0] = acc_ref[...].astype(BF)
            else:
                o_ref[...] = acc_ref[...].astype(BF)

    if blocked:
        ospec = pl.BlockSpec((1, tm, tn), lambda i, n, s: (n, i, 0))
        oshape = jax.ShapeDtypeStruct((N // tn, M, tn), BF)
    else:
        ospec = pl.BlockSpec((tm, tn), lambda i, n, s: (i, n))
        oshape = jax.ShapeDtypeStruct((M, N), BF)
    return _call(body, grid=(M // tm, N // tn, ns), name=name, args=[a, b], comm=comm,
                 in_specs=[pl.BlockSpec((ts, tm), lambda i, n, s: (s, i)), pl.BlockSpec((ts, tn), lambda i, n, s: (s, n))],
                 out_specs=[ospec], out_shape=[oshape], scratch=[pltpu.VMEM((tm, tn), F32)])


def _rope_tables(S):
    half = ROPE_DIM // 2
    inv_freq = ROPE_THETA ** (-jnp.arange(0, ROPE_DIM, 2, dtype=F32) / ROPE_DIM)
    ang = jnp.arange(S, dtype=F32)[:, None] * inv_freq[None, :]
    cos, sin = jnp.cos(ang), jnp.sin(ang)
    zeros = jnp.zeros((S, HEAD_DIM - ROPE_DIM), F32)
    c = jnp.concatenate([cos, cos, jnp.ones((S, HEAD_DIM - ROPE_DIM), F32)], axis=1)
    sm = jnp.concatenate([-sin, jnp.zeros((S, half), F32), zeros], axis=1)
    sp = jnp.concatenate([jnp.zeros((S, half), F32), sin, zeros], axis=1)
    return c, sm, sp


def _rope(t, c, sm, sp):
    return t * c + pltpu.roll(t, HEAD_DIM - ROPE_DIM // 2, 1) * sm + pltpu.roll(t, ROPE_DIM // 2, 1) * sp


def _rope_t(dy, c, sm, sp):
    return dy * c + pltpu.roll(dy * sm, ROPE_DIM // 2, 1) + pltpu.roll(dy * sp, HEAD_DIM - ROPE_DIM // 2, 1)


def _att_mask(i):
    qi = lax.broadcasted_iota(jnp.int32, (BLK, 2 * BLK), 0)
    kj = lax.broadcasted_iota(jnp.int32, (BLK, 2 * BLK), 1)
    diff = qi + BLK - kj
    first_key = jnp.where(i > 0, 0, BLK)
    return (diff >= 0) & (diff <= BLK) & (kj >= first_key)


def _res_rows(r, i, n, d):
    if d == 1:
        return pl.ds(pl.multiple_of(i * n, n), n)
    return pl.ds(r + i * (n * d), n, stride=d)


def _att_specs(S, gi):
    def sect(off):
        base = (off + gi * GROUP_W) // HEAD_DIM
        return _once((S, HEAD_DIM), lambda hh, r: (0, base + hh))

    tab = pl.BlockSpec((S, HEAD_DIM), lambda hh, r: (0, 0))
    head = pl.BlockSpec((S, HEAD_DIM), lambda hh, r: (0, hh))
    return sect, tab, head


def _att_fwd(qkv, tabs, gi, d, name, comm=()):
    S = qkv.shape[0]
    L = S // d
    sect, tab, head = _att_specs(S, gi)
    nblk = L // BLK
    scale = HEAD_DIM ** -0.5

    def body(q_ref, k_ref, v_ref, c_ref, sm_ref, sp_ref, o_ref, lse_ref, qr, kp, vp):
        r = pl.program_id(1)
        res = _res_rows(r, 0, L, d)
        c, sm, sp = c_ref[res, :], sm_ref[res, :], sp_ref[res, :]
        qr[...] = _rope(q_ref[res, :], c, sm, sp).astype(BF)
        kp[pl.ds(0, BLK), :] = jnp.zeros((BLK, HEAD_DIM), BF)
        vp[pl.ds(0, BLK), :] = jnp.zeros((BLK, HEAD_DIM), BF)
        kp[pl.ds(BLK, L), :] = _rope(k_ref[res, :], c, sm, sp).astype(BF)
        vp[pl.ds(BLK, L), :] = v_ref[res, :].astype(BF)

        def blk(i, carry):
            r0 = pl.multiple_of(i * BLK, BLK)
            s = _nt(qr[pl.ds(r0, BLK), :], kp[pl.ds(r0, 2 * BLK), :]) * scale
            s = jnp.where(_att_mask(i), s, NEG)
            m = jnp.max(s, axis=-1, keepdims=True)
            p = jnp.exp(s - m)
            l = jnp.sum(p, axis=-1, keepdims=True)
            out = _res_rows(r, i, BLK, d)
            o_ref[out, :] = _nn(p.astype(BF), vp[pl.ds(r0, 2 * BLK), :]) / l
            lse_ref[out, :] = jnp.broadcast_to(m + jnp.log(l), (BLK, HEAD_DIM))
            return carry

        lax.fori_loop(0, nblk, blk, 0, unroll=min(4, nblk))

    shp = jax.ShapeDtypeStruct((S, GROUP_W), F32)
    return _call(body, grid=(HEADS_PER_GROUP, d), name=name, args=[qkv, qkv, qkv, *tabs], comm=comm,
                 in_specs=[sect(Q_OFF), sect(K_OFF), sect(V_OFF), tab, tab, tab], out_specs=[head, head], out_shape=[shp, shp],
                 scratch=[pltpu.VMEM((L, HEAD_DIM), BF), pltpu.VMEM((L + BLK, HEAD_DIM), BF), pltpu.VMEM((L + BLK, HEAD_DIM), BF)])


def _att_combine(os, lses, name):
    S = os[0].shape[0]
    tm = _tile(S, 512)

    def body(o0, o1, o2, l0, l1, l2, oa_ref, lse_ref):
        a, b, c = l0[...], l1[...], l2[...]
        mx = jnp.maximum(jnp.maximum(a, b), c)
        wa, wb, wc = jnp.exp(a - mx), jnp.exp(b - mx), jnp.exp(c - mx)
        den = wa + wb + wc
        oa_ref[...] = ((wa * o0[...] + wb * o1[...] + wc * o2[...]) / den).astype(BF)
        lse_ref[...] = mx + jnp.log(den)

    row = pl.BlockSpec((tm, GROUP_W), lambda i: (i, 0))
    return _call(body, grid=(S // tm,), name=name, args=[*os, *lses], in_specs=[row] * 6, out_specs=[row, row],
                 out_shape=[jax.ShapeDtypeStruct((S, GROUP_W), BF), jax.ShapeDtypeStruct((S, GROUP_W), F32)])[0]


def _att_bwd(qkv, tabs, do, lse, dvec, gi, d, name, comm=()):
    S = qkv.shape[0]
    L = S // d
    sect, tab, head = _att_specs(S, gi)
    stat = _once((S, HEAD_DIM), lambda hh, r: (0, hh))
    nblk = L // BLK
    scale = HEAD_DIM ** -0.5

    def body(q_ref, k_ref, v_ref, c_ref, sm_ref, sp_ref, do_ref, lse_ref, dv_ref, dq_out, dk_out, dv_out, qr, kp, vp, dkp, dvp):
        r = pl.program_id(1)
        res = _res_rows(r, 0, L, d)
        c, sm, sp = c_ref[res, :], sm_ref[res, :], sp_ref[res, :]
        qr[...] = _rope(q_ref[res, :], c, sm, sp).astype(BF)
        kp[pl.ds(0, BLK), :] = jnp.zeros((BLK, HEAD_DIM), BF)
        vp[pl.ds(0, BLK), :] = jnp.zeros((BLK, HEAD_DIM), BF)
        kp[pl.ds(BLK, L), :] = _rope(k_ref[res, :], c, sm, sp).astype(BF)
        vp[pl.ds(BLK, L), :] = v_ref[res, :].astype(BF)
        dkp[...] = jnp.zeros_like(dkp)
        dvp[...] = jnp.zeros_like(dvp)

        def blk(i, carry):
            r0 = pl.multiple_of(i * BLK, BLK)
            rows, win, pos = pl.ds(r0, BLK), pl.ds(r0, 2 * BLK), _res_rows(r, i, BLK, d)
            q, kw, vw, dob = qr[rows, :], kp[win, :], vp[win, :], do_ref[pos, :].astype(BF)
            s = jnp.where(_att_mask(i), _nt(q, kw) * scale, NEG)
            p = jnp.exp(s - lse_ref[pos, :][:, :1])
            ds = p * (_nt(dob, vw) - dv_ref[pos, :][:, :1]) * scale
            dsb = ds.astype(BF)
            dq_out[pos, :] = _rope_t(_nn(dsb, kw), c_ref[pos, :], sm_ref[pos, :], sp_ref[pos, :])
            dkp[win, :] += _tn(dsb, q)
            dvp[win, :] += _tn(p.astype(BF), dob)
            return carry

        lax.fori_loop(0, nblk, blk, 0, unroll=2)
        dk_out[res, :] = _rope_t(dkp[pl.ds(BLK, L), :], c, sm, sp)
        dv_out[res, :] = dvp[pl.ds(BLK, L), :]

    shp = jax.ShapeDtypeStruct((S, GROUP_W), F32)
    return _call(body, grid=(HEADS_PER_GROUP, d), name=name, args=[qkv, qkv, qkv, *tabs, do, lse, dvec], comm=comm,
                 in_specs=[sect(Q_OFF), sect(K_OFF), sect(V_OFF), tab, tab, tab, stat, stat, stat],
                 out_specs=[head, head, head], out_shape=[shp, shp, shp],
                 scratch=[pltpu.VMEM((L, HEAD_DIM), BF), pltpu.VMEM((L + BLK, HEAD_DIM), BF), pltpu.VMEM((L + BLK, HEAD_DIM), BF),
                          pltpu.VMEM((L + BLK, HEAD_DIM), F32), pltpu.VMEM((L + BLK, HEAD_DIM), F32)])


def _sg_parts(u_ref, vs_ref, g_ref, b_ref):
    uv = u_ref[...].astype(F32)
    vv = vs_ref[...].astype(F32)
    vg = _gelu(vv)
    mu = jnp.mean(vg, axis=-1, keepdims=True)
    vc = vg - mu
    rs = lax.rsqrt(jnp.mean(vc * vc, axis=-1, keepdims=True) + LN_EPS)
    y = vc * rs
    return uv, vv, rs, y, y * g_ref[...] + b_ref[...]


def _sg_wmask():
    t = lax.broadcasted_iota(jnp.int32, (BLK, BLK), 0)
    s = lax.broadcasted_iota(jnp.int32, (BLK, BLK), 1)
    return s <= t


def _sg_fwd(proj, sgw, sgbT, lng, lnb, name):
    S, P = proj.shape

    def body(u_ref, vs_ref, w_ref, bt_ref, g_ref, b_ref, z_ref):
        uv, _, _, _, vln = _sg_parts(u_ref, vs_ref, g_ref, b_ref)
        ug = _gelu(uv)
        vb = vln.astype(BF)
        mask = _sg_wmask()
        bt = bt_ref[...]
        for g in range(SG_GROUPS):
            cols = slice(g * BLK, (g + 1) * BLK)
            w = jnp.where(mask, w_ref[g], 0.0).astype(BF)
            sp = _nn(w, vb[:, cols]) + bt[:, g:g + 1]
            z_ref[:, cols] = (ug[:, cols] * sp).astype(BF)

    tile = lambda off: pl.BlockSpec((BLK, SG_W), lambda i: (i, off // SG_W))
    full = lambda shape: pl.BlockSpec(shape, lambda i: (0,) * len(shape))
    return _call(body, grid=(S // BLK,), name=name, args=[proj, proj, sgw, sgbT, lng, lnb],
                 in_specs=[tile(R_U), tile(R_VS), full((SG_GROUPS, BLK, BLK)), full((BLK, BLK)), full((1, SG_W)), full((1, SG_W))],
                 out_specs=[pl.BlockSpec((BLK, SG_W), lambda i: (i, 0))], out_shape=[jax.ShapeDtypeStruct((S, SG_W), BF)])[0][0]


def _sg_bwd(proj, dz, sgw, sgbT, lng, lnb, name):
    S, P = proj.shape

    def body(u_ref, vs_ref, dz_ref, w_ref, bt_ref, g_ref, b_ref, du_ref, dvs_ref, dw_ref, dbt_ref, dg_ref, db_ref, dvln):
        @pl.when(pl.program_id(0) == 0)
        def _():
            dw_ref[...] = jnp.zeros_like(dw_ref)
            dbt_ref[...] = jnp.zeros_like(dbt_ref)
            dg_ref[...] = jnp.zeros_like(dg_ref)
            db_ref[...] = jnp.zeros_like(db_ref)

        uv, vv, rs, y, vln = _sg_parts(u_ref, vs_ref, g_ref, b_ref)
        ug = _gelu(uv)
        vb = vln.astype(BF)
        dzv = dz_ref[...].astype(F32)
        dsp = dzv * ug
        dspb = dsp.astype(BF)
        mask = _sg_wmask()
        bt = bt_ref[...]
        lane = lax.broadcasted_iota(jnp.int32, (BLK, BLK), 1)
        dbt = jnp.zeros((BLK, BLK), F32)
        for g in range(SG_GROUPS):
            cols = slice(g * BLK, (g + 1) * BLK)
            w = jnp.where(mask, w_ref[g], 0.0).astype(BF)
            sp = _nn(w, vb[:, cols]) + bt[:, g:g + 1]
            du_ref[:, cols] = (dzv[:, cols] * sp * _gelu_grad(uv[:, cols])).astype(BF)
            dw_ref[g] += jnp.where(mask, _nt(dspb[:, cols], vb[:, cols]), 0.0)
            dbt = dbt + jnp.where(lane == g, jnp.sum(dsp[:, cols], axis=-1, keepdims=True), 0.0)
            dvln[:, cols] = _tn(w, dspb[:, cols])
        dbt_ref[...] += dbt
        dvl = dvln[...]
        dg_ref[...] += jnp.sum(dvl * y, axis=0, keepdims=True)
        db_ref[...] += jnp.sum(dvl, axis=0, keepdims=True)
        dy = dvl * g_ref[...]
        dvg = rs * (dy - jnp.mean(dy, axis=-1, keepdims=True) - y * jnp.mean(dy * y, axis=-1, keepdims=True))
        dvs_ref[...] = (dvg * _gelu_grad(vv)).astype(BF)

    tile = lambda off: pl.BlockSpec((BLK, SG_W), lambda i: (i, off // SG_W))
    full = lambda shape: pl.BlockSpec(shape, lambda i: (0,) * len(shape))
    row = pl.BlockSpec((BLK, SG_W), lambda i: (i, 0))
    return _call(body, grid=(S // BLK,), name=name, args=[proj, proj, dz, sgw, sgbT, lng, lnb],
                 in_specs=[tile(R_U), tile(R_VS), row, full((SG_GROUPS, BLK, BLK)), full((BLK, BLK)), full((1, SG_W)), full((1, SG_W))],
                 out_specs=[row, row, full((SG_GROUPS, BLK, BLK)), full((BLK, BLK)), full((1, SG_W)), full((1, SG_W))],
                 out_shape=[jax.ShapeDtypeStruct((S, SG_W), BF), jax.ShapeDtypeStruct((S, SG_W), BF),
                            jax.ShapeDtypeStruct((SG_GROUPS, BLK, BLK), F32), jax.ShapeDtypeStruct((BLK, BLK), F32),
                            jax.ShapeDtypeStruct((1, SG_W), F32), jax.ShapeDtypeStruct((1, SG_W), F32)],
                 scratch=[pltpu.VMEM((BLK, SG_W), F32)])[0]


def _gate_merge(oatt, z, watt, wsg, proj, name, comm=()):
    S = oatt.shape[0]
    nb, _, Db = watt.shape
    D = nb * Db
    tm = _tile(S, 512)
    ga, gs = R_GA // Db, (R_GA + D) // Db

    def body(oa_ref, z_ref, wa_ref, ws_ref, ga_ref, gs_ref, ya_ref, ys_ref, mg_ref):
        ya = _nn(oa_ref[...], wa_ref[0])
        ys = _nn(z_ref[...], ws_ref[0])
        ya_ref[...] = ya.astype(BF)
        ys_ref[...] = ys.astype(BF)
        mg_ref[...] = (jax.nn.sigmoid(ga_ref[...].astype(F32)) * ya + jax.nn.sigmoid(gs_ref[...].astype(F32)) * ys).astype(BF)

    out = pl.BlockSpec((tm, Db), lambda j, i: (i, j))
    shp = jax.ShapeDtypeStruct((S, D), BF)
    return _call(body, grid=(nb, S // tm), name=name, args=[oatt, z, watt, wsg, proj, proj], comm=comm,
                 in_specs=[pl.BlockSpec((tm, GROUP_W), lambda j, i: (i, 0)), pl.BlockSpec((tm, SG_W), lambda j, i: (i, 0)),
                           pl.BlockSpec((1, GROUP_W, Db), lambda j, i: (j, 0, 0)), pl.BlockSpec((1, SG_W, Db), lambda j, i: (j, 0, 0)),
                           pl.BlockSpec((tm, Db), lambda j, i: (i, ga + j)), pl.BlockSpec((tm, Db), lambda j, i: (i, gs + j))],
                 out_specs=[out, out, out], out_shape=[shp, shp, shp])


def _mix_out(merged, wout, x, gn, name):
    S, D = x.shape
    tm = _tile(S, 256)

    def body(m_ref, w_ref, x_ref, gn_ref, xo_ref, hn_ref):
        xo = x_ref[...] + _nn(m_ref[...], w_ref[...])
        r = lax.rsqrt(jnp.mean(xo * xo, axis=-1, keepdims=True) + NORM_EPS)
        xo_ref[...] = xo
        hn_ref[...] = (xo * r * gn_ref[...]).astype(BF)

    row = pl.BlockSpec((tm, D), lambda i: (i, 0))
    return _call(body, grid=(S // tm,), name=name, args=[merged, wout, x, gn],
                 in_specs=[row, pl.BlockSpec((D, D), lambda i: (0, 0)), row, pl.BlockSpec((1, D), lambda i: (0, 0))],
                 out_specs=[row, row], out_shape=[jax.ShapeDtypeStruct((S, D), F32), jax.ShapeDtypeStruct((S, D), BF)])[0]


def _mix_bwd_gate(dmix, wout, ya, ys, proj, name):
    S, D = dmix.shape
    tm, tn = _tile(S, 512), 512
    ga, gs = R_GA // tn, (R_GA + D) // tn

    def body(dm_ref, w_ref, ya_ref, ys_ref, ga_ref, gs_ref, dya_ref, dys_ref, dga_ref, dgs_ref):
        dm = _nt(dm_ref[...], w_ref[...])
        sa = jax.nn.sigmoid(ga_ref[...].astype(F32))
        ss = jax.nn.sigmoid(gs_ref[...].astype(F32))
        dya_ref[...] = (dm * sa).astype(BF)
        dys_ref[...] = (dm * ss).astype(BF)
        dga_ref[...] = (dm * ya_ref[...].astype(F32) * sa * (1.0 - sa)).astype(BF)
        dgs_ref[...] = (dm * ys_ref[...].astype(F32) * ss * (1.0 - ss)).astype(BF)

    out = pl.BlockSpec((tm, tn), lambda i, n: (i, n))
    shp = jax.ShapeDtypeStruct((S, D), BF)
    return _call(body, grid=(S // tm, D // tn), name=name, args=[dmix, wout, ya, ys, proj, proj],
                 in_specs=[pl.BlockSpec((tm, D), lambda i, n: (i, 0)), pl.BlockSpec((tn, D), lambda i, n: (n, 0)), out, out,
                           pl.BlockSpec((tm, tn), lambda i, n: (i, ga + n)), pl.BlockSpec((tm, tn), lambda i, n: (i, gs + n))],
                 out_specs=[out] * 4, out_shape=[shp] * 4)[0]


def _att_sg_dout(dya, dys, watt, wsg, oatt, name, comm=()):
    S, D = dya.shape
    nb, _, Db = watt.shape
    tm = _tile(S, 512)

    def body(dya_ref, dys_ref, wa_ref, ws_ref, oa_ref, do_ref, dz_ref, dvec_ref, acca, accs):
        j = pl.program_id(1)

        @pl.when(j == 0)
        def _():
            acca[...] = jnp.zeros_like(acca)
            accs[...] = jnp.zeros_like(accs)

        acca[...] += _nt(dya_ref[...], wa_ref[0])
        accs[...] += _nt(dys_ref[...], ws_ref[0])

        @pl.when(j == nb - 1)
        def _():
            dov = acca[...]
            do_ref[...] = dov
            dz_ref[...] = accs[...].astype(BF)
            prod = dov * oa_ref[...].astype(F32)
            for hh in range(HEADS_PER_GROUP):
                cols = slice(hh * HEAD_DIM, (hh + 1) * HEAD_DIM)
                dvec_ref[:, cols] = jnp.broadcast_to(jnp.sum(prod[:, cols], axis=-1, keepdims=True), (tm, HEAD_DIM))

    blk = pl.BlockSpec((tm, Db), lambda i, j: (i, j))
    att = pl.BlockSpec((tm, GROUP_W), lambda i, j: (i, 0))
    return _call(body, grid=(S // tm, nb), name=name, args=[dya, dys, watt, wsg, oatt], comm=comm,
                 in_specs=[blk, blk, pl.BlockSpec((1, GROUP_W, Db), lambda i, j: (j, 0, 0)), pl.BlockSpec((1, SG_W, Db), lambda i, j: (j, 0, 0)), att],
                 out_specs=[att, pl.BlockSpec((tm, SG_W), lambda i, j: (i, 0)), att],
                 out_shape=[jax.ShapeDtypeStruct((S, GROUP_W), F32), jax.ShapeDtypeStruct((S, SG_W), BF), jax.ShapeDtypeStruct((S, GROUP_W), F32)],
                 scratch=[pltpu.VMEM((tm, GROUP_W), F32), pltpu.VMEM((tm, SG_W), F32)])[0]


def _small_allreduce(pack, name):
    R = pack.shape[0]

    def body(p_ref, o_ref, gath, send, recv):
        x, y, c = _place()
        me = 4 * x + 2 * y + c
        gath[me] = p_ref[...]
        copies = []
        for r in range(1, N_DEV):
            px, py, pc = _flip(x, r & 4), _flip(y, r & 2), _flip(c, r & 1)
            peer = 4 * px + 2 * py + pc
            mk = lambda dst: pltpu.make_async_remote_copy(src_ref=p_ref, dst_ref=dst, send_sem=send.at[r - 1], recv_sem=recv.at[r - 1],
                                                          device_id=(px, py, pc), device_id_type=MESH)
            snd = mk(gath.at[me])
            snd.start()
            copies.append((snd, mk(gath.at[peer])))
        for snd, rcv in copies:
            rcv.wait_recv()
            snd.wait_send()
        acc = gath[0]
        for s in range(1, N_DEV):
            acc = acc + gath[s]
        o_ref[...] = acc

    vm = pl.BlockSpec(memory_space=pltpu.VMEM)
    return pl.pallas_call(
        body, name=name, in_specs=[vm], out_specs=vm, out_shape=jax.ShapeDtypeStruct(pack.shape, F32),
        scratch_shapes=[pltpu.VMEM((N_DEV, R, 128), F32), pltpu.SemaphoreType.DMA((7,)), pltpu.SemaphoreType.DMA((7,))],
        compiler_params=pltpu.CompilerParams(vmem_limit_bytes=VMEM_LIMIT),
    )(pack)


def _row_tile(R, C, elems=262144):
    tr = R
    while tr * C > elems and tr % 32 == 0:
        tr //= 2
    return tr


def _pair_add(parts, other, name):
    _, R, C = parts.shape
    tr = _row_tile(R, C, 1048576)

    def body(c_ref, p_ref, o_ref, s_ref):
        s_ref[0] = (p_ref[0].astype(F32) + o_ref[0].astype(F32)).astype(BF)

    core = lax.axis_index("c").astype(jnp.int32).reshape(1)
    return pl.pallas_call(
        body, name=name,
        grid_spec=pltpu.PrefetchScalarGridSpec(
            num_scalar_prefetch=1, grid=(N_CHIP, R // tr),
            in_specs=[pl.BlockSpec((1, tr, C), lambda q, i, c: (2 * q + c[0], i, 0)), pl.BlockSpec((1, tr, C), lambda q, i, c: (q, i, 0))],
            out_specs=pl.BlockSpec((1, tr, C), lambda q, i, c: (q, i, 0))),
        out_shape=jax.ShapeDtypeStruct((N_CHIP, R, C), BF),
        compiler_params=pltpu.CompilerParams(dimension_semantics=("arbitrary", "arbitrary"), vmem_limit_bytes=VMEM_LIMIT),
    )(core, parts, other)


def _adamw(parts, w, m, v, name):
    ns, R, C = parts.shape
    tr = _row_tile(R, C, 524288)
    c1 = 1.0 - ADAM_B1 ** ADAM_STEP
    c2 = 1.0 - ADAM_B2 ** ADAM_STEP

    def body(p_ref, w_ref, m_ref, v_ref, g_ref, d_ref, nm_ref, nv_ref):
        g = p_ref[0].astype(F32)
        for s in range(1, ns):
            g = g + p_ref[s].astype(F32)
        mn = ADAM_B1 * m_ref[...] + (1.0 - ADAM_B1) * g
        vn = ADAM_B2 * v_ref[...] + (1.0 - ADAM_B2) * (g * g)
        g_ref[...] = g
        nm_ref[...] = mn
        nv_ref[...] = vn
        d_ref[...] = -ADAM_LR * ((mn / c1) / (jnp.sqrt(vn / c2) + ADAM_EPS) + ADAM_WD * w_ref[...])

    row = pl.BlockSpec((tr, C), lambda i: (i, 0))
    shp = jax.ShapeDtypeStruct((R, C), F32)
    return _call(body, grid=(R // tr,), name=name, args=[parts, w, m, v],
                 in_specs=[pl.BlockSpec((ns, tr, C), lambda i: (0, i, 0)), row, row, row], out_specs=[row] * 4, out_shape=[shp] * 4)[0]


def _pad_rows(a, rows):
    return jnp.pad(a, ((0, rows - a.shape[0]), (0, 0)))


def kernel(x, ffn1_norm, ffn1_w_gate, ffn1_w_up, ffn1_w_down, mix_norm, w_in, sg_ln_g, sg_ln_b, sg_w, sg_b, w_att_out, w_sg_out, w_out, ffn2_norm, ffn2_w_gate, ffn2_w_up, ffn2_w_down, final_norm, loss_target, m_ffn1_norm, m_ffn1_w_gate, m_ffn1_w_up, m_ffn1_w_down, m_mix_norm, m_w_in, m_sg_ln_g, m_sg_ln_b, m_sg_w, m_sg_b, m_w_att_out, m_w_sg_out, m_w_out, m_ffn2_norm, m_ffn2_w_gate, m_ffn2_w_up, m_ffn2_w_down, m_final_norm, v_ffn1_norm, v_ffn1_w_gate, v_ffn1_w_up, v_ffn1_w_down, v_mix_norm, v_w_in, v_sg_ln_g, v_sg_ln_b, v_sg_w, v_sg_b, v_w_att_out, v_w_sg_out, v_w_out, v_ffn2_norm, v_ffn2_w_gate, v_ffn2_w_up, v_ffn2_w_down, v_final_norm):
    S, D = x.shape[1], x.shape[2]
    Pb = w_in.shape[2]
    P = N_DEV * Pb
    assert P == GA_OFF + 2 * D and D % (N_DEV * 128) == 0 and S % (BLK * DILATIONS[-1]) == 0
    xs, tgt = x[0], loss_target[0]

    sharded = dict(ffn1_w_gate=ffn1_w_gate, ffn1_w_up=ffn1_w_up, ffn1_w_down=ffn1_w_down, w_in=w_in, w_att_out=w_att_out,
                   w_sg_out=w_sg_out, w_out=w_out, ffn2_w_gate=ffn2_w_gate, ffn2_w_up=ffn2_w_up, ffn2_w_down=ffn2_w_down)
    sb = {n: w[0].astype(BF) for n, w in sharded.items()}

    wg1, wu1 = _comm_only(_Gather([sb["ffn1_w_gate"], sb["ffn1_w_up"]]), "gather_ffn1")
    h1 = _rms_fwd(xs, ffn1_norm, "rms1")
    win_top, win_bot = sb["w_in"][: D // 2], sb["w_in"][D // 2:]
    (g1, u1, a1), ((wd1, win8a),) = _ffn_up(h1, wg1, wu1, "ffn1_up", comm=[_Gather([sb["ffn1_w_down"], win_top], 1.0, 0.6)])
    (x1, h2), ((win8b,),) = _ffn_down_norm(a1, wd1, xs, mix_norm, "ffn1_down", comm=[_Gather([win_bot], 0.75, 0.45)])
    win = jnp.concatenate([w8.transpose(1, 0, 2).reshape(D // 2, P) for w8 in (win8a, win8b)], axis=0)
    tm_proj = _tile(S, 1024)
    (qkv,), ((wg2,),) = _mm_nn(h2, win, tm_proj, 512, 0, U_OFF, F32, "proj_qkv", comm=[_Gather([sb["ffn2_w_gate"]], 1.0, 0.7)])
    (rest,), ((wu2,),) = _mm_nn(h2, win, tm_proj, 512, U_OFF, P, BF, "proj_rest", comm=[_Gather([sb["ffn2_w_up"]], 0.75, 0.45)])
    tabs = _rope_tables(S)
    rides = [[_Gather([sb["w_att_out"], sb["w_sg_out"]], 0.9, 0.5)], [_Gather([sb["w_out"]], 0.85, 0.45)], []]
    os, lses, late = [], [], []
    for gi, d in enumerate(DILATIONS):
        (o, l), got_here = _att_fwd(qkv, tabs, gi, d, f"att_fwd{gi}", comm=rides[gi])
        late += [w for g in got_here for w in g]
        os.append(o)
        lses.append(l)
    watt, wsg, wout8 = late
    wout = wout8.reshape(D, D)
    oatt, lse = _att_combine(os, lses, "att_combine")
    sgw = sg_w[0]
    sgbT = jnp.pad(sg_b[0].T, ((0, 0), (0, BLK - SG_GROUPS)))
    z = _sg_fwd(rest, sgw, sgbT, sg_ln_g, sg_ln_b, "sg_fwd")
    (ya, ys, merged), _ = _gate_merge(oatt, z, watt, wsg, rest, "gate_merge")
    x2, h3 = _mix_out(merged, wout, x1, ffn2_norm, "mix_out")
    (g3, u3, a3), ((wd2,),) = _ffn_up(h3, wg2, wu2, "ffn2_up", comm=[_Gather([sb["ffn2_w_down"]], 0.6, 0.35)])
    dx3, dyb3, d_final, loss_part = _ffn_down_loss(a3, wd2, x2, final_norm.reshape(1, D), tgt, "ffn2_down_loss")

    Fb = wg2.shape[2]
    Db = watt.shape[2]
    p_pad = -(-P // PROJ_TK) * PROJ_TK
    winT = jnp.concatenate([w8.transpose(0, 2, 1).reshape(P, D // 2) for w8 in (win8a, win8b)], axis=1)
    winT = jnp.pad(winT, ((0, p_pad - P), (0, 0)))
    (dg3, du3), _ = _ffn_bwd_act(dyb3, wd2, g3, u3, "ffn2_bwd_act")
    (dwd2,), _ = _ffn_dwd(a3, dyb3, "ffn2_dwd")
    (dwg2, dwu2), _ = _ffn_dwgu(h3, dg3, du3, "ffn2_dwgu")
    ffn2_parts = [dwd2, dwg2, dwu2]
    (dx2, dmixb, d_ffn2n), (ffn2_other,) = _dh_rms_bwd([(dg3, wg2), (du3, wu2)], True, Fb, x2, ffn2_norm, dx3, 1.0, "ffn2_dh",
                                                     comm=[_Swap(ffn2_parts)])
    ffn2_sums = [_pair_add(p, o, f"pair_ffn2_{i}") for i, (p, o) in enumerate(zip(ffn2_parts, ffn2_other))]

    dya, dys, dga, dgs = _mix_bwd_gate(dmixb, wout, ya, ys, rest, "mix_bwd_gate")
    (dwout,), _ = _mm_tn(merged, dmixb, _tile(D, 1024), _tile(D, 1024), _tile(S, 1024), False, "dw_out")
    do, dz, dvec = _att_sg_dout(dya, dys, watt, wsg, oatt, "att_sg_dout")
    (dwatt,), _ = _mm_tn(oatt, dya, GROUP_W, Db, _tile(S, 1024), True, "dw_att")
    (dwsg,), _ = _mm_tn(z, dys, SG_W, Db, _tile(S, 1024), True, "dw_sg")
    mix_parts = [dwout.reshape(N_DEV, D // N_DEV, D), dwatt, dwsg]
    du, dvs, d_sgw, d_sgbT, d_lng, d_lnb = _sg_bwd(rest, dz, sgw, sgbT, sg_ln_g, sg_ln_b, "sg_bwd")
    dqs, dks, dvs_att, ffn2_got = [], [], [], []
    for gi, d in enumerate(DILATIONS):
        ride = [_Ici([ffn2_sums[gi - 1]])] if gi else []
        (dq, dk, dv), got_here = _att_bwd(qkv, tabs, do, lse, dvec, gi, d, f"att_bwd{gi}", comm=ride)
        ffn2_got += [g[0] for g in got_here]
        dqs.append(dq)
        dks.append(dk)
        dvs_att.append(dv)
    dproj = jnp.concatenate([t.astype(BF) for t in dqs + dks + dvs_att] + [du, dvs, dga, dgs, jnp.zeros((S, p_pad - P), BF)], axis=1)
    (dx1, dyb1, d_mixn), ((ffn2_last,), mix_other) = _dh_rms_bwd([(dproj, winT)], False, PROJ_TK, x1, mix_norm, dx2, 0.5, "proj_dh",
                                                               comm=[_Ici([ffn2_sums[2]]), _Swap(mix_parts)])
    ffn2_got.append(ffn2_last)
    mix_sums = [_pair_add(p, o, f"pair_mix_{i}") for i, (p, o) in enumerate(zip(mix_parts, mix_other))]
    (dwd1,), (mix_got,) = _ffn_dwd(a1, dyb1, "ffn1_dwd", comm=[_Ici(mix_sums)])
    rows = lambda a: a.reshape(-1, 128)
    pad8 = lambda a: _pad_rows(a, -(-a.shape[0] // 8) * 8)
    small = [("sg_w", rows(d_sgw), sg_w, m_sg_w, v_sg_w), ("mix_norm", rows(d_mixn), mix_norm, m_mix_norm, v_mix_norm),
             ("ffn2_norm", rows(d_ffn2n), ffn2_norm, m_ffn2_norm, v_ffn2_norm), ("final_norm", rows(d_final), final_norm, m_final_norm, v_final_norm),
             ("sg_ln_g", rows(d_lng), sg_ln_g, m_sg_ln_g, v_sg_ln_g), ("sg_ln_b", rows(d_lnb), sg_ln_b, m_sg_ln_b, v_sg_ln_b),
             ("sg_b", d_sgbT[:, :SG_GROUPS].T, sg_b, m_sg_b, v_sg_b)]
    gpack = jnp.concatenate([pad8(g) for _, g, _, _, _ in small] + [pad8(loss_part)], axis=0)
    (dwin,), ((wd1_other,), (gpacks,)) = _mm_tn(h2, dproj, D, 512, _tile(S, 2048), False, "dw_in", ncols=P,
                                              comm=[_Swap([dwd1]), _Spread([gpack])])
    dwin = dwin.reshape(D, N_DEV, Pb).transpose(1, 0, 2)
    wd1_sum = _pair_add(dwd1, wd1_other, "pair_wd1")
    (dg1, du1), ((wd1_got,), (win_other,)) = _ffn_bwd_act(dyb1, wd1, g1, u1, "ffn1_bwd_act", comm=[_Ici([wd1_sum]), _Swap([dwin])])
    win_sum = _pair_add(dwin, win_other, "pair_win")
    (dwg1, dwu1), ((win_got,),) = _ffn_dwgu(h1, dg1, du1, "ffn1_dwgu", comm=[_Ici([win_sum])])
    gu_parts = [dwg1, dwu1]
    gu_other = _comm_only(_Swap(gu_parts), "swap_gu1")
    gu_sums = [_pair_add(p, o, f"pair_gu1_{i}") for i, (p, o) in enumerate(zip(gu_parts, gu_other))]
    (dx0, _, d_ffn1n), (gu_got,) = _dh_rms_bwd([(dg1, wg1), (du1, wu1)], True, Fb, xs, ffn1_norm, dx1, 1.0, "ffn1_dh",
                                               comm=[_Ici(gu_sums)])

    got = dict(ffn2_w_down=ffn2_got[0], ffn2_w_gate=ffn2_got[1], ffn2_w_up=ffn2_got[2], w_out=mix_got[0], w_att_out=mix_got[1],
               w_sg_out=mix_got[2], w_in=win_got, ffn1_w_gate=gu_got[0], ffn1_w_up=gu_got[1], ffn1_w_down=wd1_got)
    moments = dict(ffn1_w_gate=(m_ffn1_w_gate, v_ffn1_w_gate), ffn1_w_up=(m_ffn1_w_up, v_ffn1_w_up),
                   ffn1_w_down=(m_ffn1_w_down, v_ffn1_w_down), w_in=(m_w_in, v_w_in), w_att_out=(m_w_att_out, v_w_att_out),
                   w_sg_out=(m_w_sg_out, v_w_sg_out), w_out=(m_w_out, v_w_out), ffn2_w_gate=(m_ffn2_w_gate, v_ffn2_w_gate),
                   ffn2_w_up=(m_ffn2_w_up, v_ffn2_w_up), ffn2_w_down=(m_ffn2_w_down, v_ffn2_w_down))
    res = {}
    for n in sharded:
        mm, vv = moments[n]
        outs = _adamw(got[n], sharded[n][0], mm[0], vv[0], "adamw_" + n)
        res[n] = [o[None] for o in outs]

    zero8 = jnp.zeros((8, 128), F32)
    wpack = jnp.concatenate([pad8(rows(w)) for _, _, w, _, _ in small] + [zero8], axis=0)
    mpack = jnp.concatenate([pad8(rows(m)) for _, _, _, m, _ in small] + [zero8], axis=0)
    vpack = jnp.concatenate([pad8(rows(v)) for _, _, _, _, v in small] + [zero8], axis=0)
    packs = _adamw(gpacks, wpack, mpack, vpack, "adamw_small")
    off = 0
    for n, g, w, _, _ in small:
        r = g.shape[0]
        res[n] = [p[off:off + r].reshape(w.shape) for p in packs]
        off += -(-r // 8) * 8
    loss = packs[0][off, 0]
    g_first = _small_allreduce(rows(d_ffn1n), "allreduce_ffn1_norm")
    res["ffn1_norm"] = [p.reshape(ffn1_norm.shape) for p in
                        _adamw(g_first[None], rows(ffn1_norm), rows(m_ffn1_norm), rows(v_ffn1_norm), "adamw_ffn1_norm")]

    order = ["ffn1_norm", "ffn1_w_gate", "ffn1_w_up", "ffn1_w_down", "mix_norm", "w_in", "sg_ln_g", "sg_ln_b", "sg_w", "sg_b",
             "w_att_out", "w_sg_out", "w_out", "ffn2_norm", "ffn2_w_gate", "ffn2_w_up", "ffn2_w_down", "final_norm"]
    return (loss, dx0[None], *[res[n][0] for n in order], *[res[n][1] for n in order], *[res[n][2] for n in order],
            *[res[n][3] for n in order])
```

```python
import math

import jax
import jax.numpy as jnp
from jax import lax
from jax.experimental import pallas as pl
from jax.experimental.pallas import tpu as pltpu

BF = jnp.bfloat16
F32 = jnp.float32
MESH = pl.DeviceIdType.MESH
N_DEV = 8
N_CHIP = 4

HEAD_DIM = 128
HEADS_PER_GROUP = 4
GROUP_W = HEADS_PER_GROUP * HEAD_DIM
DILATIONS = (1, 4, 16)
ATT_W = len(DILATIONS) * GROUP_W
SG_W = 1536
SG_GROUPS = 12
BLK = 128
ROPE_DIM = 32
ROPE_THETA = 500000.0
NORM_EPS = 1e-6
LN_EPS = 1e-5
Q_OFF, K_OFF, V_OFF, U_OFF, VS_OFF, GA_OFF = 0, ATT_W, 2 * ATT_W, 3 * ATT_W, 3 * ATT_W + SG_W, 3 * ATT_W + 2 * SG_W

ADAM_LR, ADAM_B1, ADAM_B2, ADAM_EPS, ADAM_WD, ADAM_STEP = 0.001, 0.9, 0.999, 1e-08, 0.01, 10

VMEM_LIMIT = 56 * 1024 * 1024
NEG = -1e30
ANY = pl.BlockSpec(memory_space=pl.ANY)
EPI_ROWS = 128
ACC_COLS = 512
FFN_PAIR = 2
PROJ_TK = 1536
R_U, R_VS, R_GA = 0, SG_W, 2 * SG_W


def _once(shape, index_map):
    return pl.BlockSpec(shape, index_map, pipeline_mode=pl.Buffered(1))


def _tile(n, pref):
    t = min(n, pref)
    while n % t:
        t //= 2
    return t


def _nt(a, b):
    return lax.dot_general(a, b, (((1,), (1,)), ((), ())), preferred_element_type=F32)


def _tn(a, b):
    return lax.dot_general(a, b, (((0,), (0,)), ((), ())), preferred_element_type=F32)


def _nn(a, b):
    return jnp.dot(a, b, preferred_element_type=F32)


def _acc_dots(acc_ref, terms, transposed_rhs=False):
    n = acc_ref.shape[1]
    width = min(n, ACC_COLS)
    for c0 in range(0, n, width):
        cols = slice(c0, c0 + width)
        tot = None
        for lhs, rhs in terms:
            part = _nt(lhs, rhs(cols)) if transposed_rhs else _nn(lhs, rhs(cols))
            tot = part if tot is None else tot + part
        acc_ref[:, cols] += tot


def _gelu(x):
    return 0.5 * x * (1.0 + lax.erf(x * (2.0 ** -0.5)))


def _gelu_grad(x):
    return 0.5 * (1.0 + lax.erf(x * (2.0 ** -0.5))) + x * jnp.exp(-0.5 * x * x) * (1.0 / math.sqrt(2.0 * math.pi))


def _place():
    x, y, c = lax.axis_index("x"), lax.axis_index("y"), lax.axis_index("c")
    return x, y, c


def _flip(v, bit):
    return 1 - v if bit else v


class _Gather:
    def __init__(self, shards, mid_frac=1.0, relay_frac=0.5):
        self.arrays = list(shards)
        self.relay_frac = relay_frac
        self.mid_frac = mid_frac
        nw = len(shards)
        self.out_shape = [jax.ShapeDtypeStruct((N_DEV,) + s.shape, s.dtype) for s in shards]
        self.scratch = [pltpu.SemaphoreType.DMA((nw, 7)), pltpu.SemaphoreType.DMA((nw, 7)), pltpu.SemaphoreType.DMA((nw,))]

    def _parts(self, ins, outs, sems):
        x, y, c = _place()
        send, recv, loc = sems
        south = c == 0
        near = (jnp.where(south, x, 1 - x), jnp.where(south, 1 - y, y), c)
        far = (jnp.where(south, 1 - x, x), jnp.where(south, y, 1 - y), c)
        diag = (1 - x, 1 - y, c)

        def copy(k, s, block, to, src=None):
            dst = outs[k].at[4 * block[0] + 2 * block[1] + block[2]]
            return pltpu.make_async_remote_copy(src_ref=dst if src is None else src, dst_ref=dst, send_sem=send.at[k, s],
                                                recv_sem=recv.at[k, s], device_id=to, device_id_type=MESH)

        def first(k):
            me = (x, y, c)
            return [copy(k, 0, me, (x, y, 1 - c), src=ins[k]), copy(k, 1, me, (1 - x, y, c), src=ins[k]),
                    copy(k, 2, me, (x, 1 - y, c), src=ins[k])]

        def local(k):
            return pltpu.make_async_copy(ins[k], outs[k].at[4 * x + 2 * y + c], loc.at[k])

        return x, y, c, near, far, diag, copy, first, local

    def start(self, ins, outs, sems):
        *_, first, local = self._parts(ins, outs, sems)
        for k in range(len(ins)):
            local(k).start()
            for cp in first(k):
                cp.start()

    def relay(self, ins, outs, sems):
        x, y, c, near, far, _, copy, _, _ = self._parts(ins, outs, sems)
        for k in range(len(ins)):
            copy(k, 2 - c, near, (x, y, c)).wait_recv()
            copy(k, 3, near, far).start()
            copy(k, 5 - c, near, (x, y, 1 - c)).start()

    def mid(self, ins, outs, sems):
        x, y, c, _, far, diag, copy, _, _ = self._parts(ins, outs, sems)
        for k in range(len(ins)):
            copy(k, 1 + c, far, (x, y, c)).wait_recv()
            copy(k, 4 + c, far, (x, y, 1 - c)).start()
            copy(k, 3, diag, (x, y, c)).wait_recv()
            copy(k, 6, diag, (x, y, 1 - c)).start()

    def finish(self, ins, outs, sems):
        x, y, c, near, _, _, copy, first, local = self._parts(ins, outs, sems)
        sib = (x, y, 1 - c)
        for k in range(len(ins)):
            copy(k, 0, sib, (x, y, c)).wait_recv()
            copy(k, 4, (1 - x, y, 1 - c), (x, y, c)).wait_recv()
            copy(k, 5, (x, 1 - y, 1 - c), (x, y, c)).wait_recv()
            copy(k, 6, (1 - x, 1 - y, 1 - c), (x, y, c)).wait_recv()
        for k in range(len(ins)):
            for cp in first(k):
                cp.wait_send()
            for s in (3, 4, 5, 6):
                copy(k, s, near, sib).wait_send()
            local(k).wait()


class _Swap:
    def __init__(self, parts):
        self.arrays = list(parts)
        nw = len(parts)
        self.out_shape = [jax.ShapeDtypeStruct((N_CHIP,) + p.shape[1:], p.dtype) for p in parts]
        self.scratch = [pltpu.SemaphoreType.DMA((nw, N_CHIP)), pltpu.SemaphoreType.DMA((nw, N_CHIP))]

    def _copy(self, ins, outs, sems, k, q):
        x, y, c = _place()
        return pltpu.make_async_remote_copy(src_ref=ins[k].at[2 * q + 1 - c], dst_ref=outs[k].at[q], send_sem=sems[0].at[k, q],
                                            recv_sem=sems[1].at[k, q], device_id=(x, y, 1 - c), device_id_type=MESH)

    mid_frac = None

    def start(self, ins, outs, sems):
        for k in range(len(ins)):
            for q in range(N_CHIP):
                self._copy(ins, outs, sems, k, q).start()

    def finish(self, ins, outs, sems):
        for k in range(len(ins)):
            for q in range(N_CHIP):
                self._copy(ins, outs, sems, k, q).wait()


class _Ici:
    mid_frac = None

    def __init__(self, sums):
        self.arrays = list(sums)
        nw = len(sums)
        self.out_shape = [jax.ShapeDtypeStruct(s.shape, s.dtype) for s in sums]
        self.scratch = [pltpu.SemaphoreType.DMA((nw, 3)), pltpu.SemaphoreType.DMA((nw, 3)), pltpu.SemaphoreType.DMA((nw,))]

    def _copies(self, ins, outs, sems, k):
        x, y, c = _place()
        myq = 2 * x + y
        out = []
        for r in range(1, N_CHIP):
            px, py = _flip(x, r & 2), _flip(y, r & 1)
            pq = 2 * px + py
            mk = lambda dst: pltpu.make_async_remote_copy(src_ref=ins[k].at[pq], dst_ref=dst, send_sem=sems[0].at[k, r - 1],
                                                          recv_sem=sems[1].at[k, r - 1], device_id=(px, py, c), device_id_type=MESH)
            out.append((mk(outs[k].at[myq]), mk(outs[k].at[pq])))
        return out, pltpu.make_async_copy(ins[k].at[myq], outs[k].at[myq], sems[2].at[k])

    def start(self, ins, outs, sems):
        for k in range(len(ins)):
            remote, local = self._copies(ins, outs, sems, k)
            local.start()
            for snd, _ in remote:
                snd.start()

    def finish(self, ins, outs, sems):
        for k in range(len(ins)):
            remote, local = self._copies(ins, outs, sems, k)
            for snd, rcv in remote:
                rcv.wait_recv()
                snd.wait_send()
            local.wait()


class _Spread:
    mid_frac = None

    def __init__(self, arrays):
        self.arrays = list(arrays)
        nw = len(arrays)
        self.out_shape = [jax.ShapeDtypeStruct((N_DEV,) + a.shape, a.dtype) for a in arrays]
        self.scratch = [pltpu.SemaphoreType.DMA((nw, 7)), pltpu.SemaphoreType.DMA((nw, 7)), pltpu.SemaphoreType.DMA((nw,))]

    def _copies(self, ins, outs, sems, k):
        x, y, c = _place()
        me = 4 * x + 2 * y + c
        out = []
        for r in range(1, N_DEV):
            px, py, pc = _flip(x, r & 4), _flip(y, r & 2), _flip(c, r & 1)
            peer = 4 * px + 2 * py + pc
            mk = lambda dst: pltpu.make_async_remote_copy(src_ref=ins[k], dst_ref=dst, send_sem=sems[0].at[k, r - 1],
                                                          recv_sem=sems[1].at[k, r - 1], device_id=(px, py, pc), device_id_type=MESH)
            out.append((mk(outs[k].at[me]), mk(outs[k].at[peer])))
        return out, pltpu.make_async_copy(ins[k], outs[k].at[me], sems[2].at[k])

    def start(self, ins, outs, sems):
        for k in range(len(ins)):
            remote, local = self._copies(ins, outs, sems, k)
            local.start()
            for snd, _ in remote:
                snd.start()

    def finish(self, ins, outs, sems):
        for k in range(len(ins)):
            remote, local = self._copies(ins, outs, sems, k)
            for snd, rcv in remote:
                rcv.wait_recv()
                snd.wait_send()
            local.wait()


def _call(body, *, grid, in_specs, out_specs, out_shape, name, args, scratch=(), comm=()):
    comm = list(comm)
    n_in, n_out, n_scr = len(in_specs), len(out_specs), len(scratch)
    total = math.prod(grid) if grid else 1

    def wrapped(*refs):
        p = n_in
        cin = []
        for cm in comm:
            cin.append(refs[p:p + len(cm.arrays)])
            p += len(cm.arrays)
        own_out = refs[p:p + n_out]
        p += n_out
        cout = []
        for cm in comm:
            cout.append(refs[p:p + len(cm.arrays)])
            p += len(cm.arrays)
        own_scr = refs[p:p + n_scr]
        p += n_scr
        csem = []
        for cm in comm:
            csem.append(refs[p:p + len(cm.scratch)])
            p += len(cm.scratch)
        step = 0
        for axis, g in enumerate(grid):
            step = step * g + pl.program_id(axis)

        def at(when, what):
            if total == 1:
                what()
            else:
                pl.when(step == when)(what)

        def starts():
            for cm, i, o, s in zip(comm, cin, cout, csem):
                cm.start(i, o, s)

        def finishes():
            for cm, i, o, s in zip(comm, cin, cout, csem):
                cm.finish(i, o, s)

        if comm:
            at(0, starts)
        if body is not None:
            body(*refs[:n_in], *own_out, *own_scr)
        for cm, i, o, s in zip(comm, cin, cout, csem):
            if cm.mid_frac is not None:
                at(min(total - 1, int(total * cm.relay_frac)), lambda cm=cm, i=i, o=o, s=s: cm.relay(i, o, s))
                at(min(total - 1, int(total * cm.mid_frac)), lambda cm=cm, i=i, o=o, s=s: cm.mid(i, o, s))
        if comm:
            at(total - 1, finishes)

    kw = dict(grid=tuple(grid)) if grid else {}
    outs = pl.pallas_call(
        wrapped, name=name, **kw,
        in_specs=list(in_specs) + [ANY for cm in comm for _ in cm.arrays],
        out_specs=list(out_specs) + [ANY for cm in comm for _ in cm.arrays],
        out_shape=list(out_shape) + [s for cm in comm for s in cm.out_shape],
        scratch_shapes=list(scratch) + [s for cm in comm for s in cm.scratch],
        compiler_params=pltpu.CompilerParams(dimension_semantics=("arbitrary",) * len(grid), vmem_limit_bytes=VMEM_LIMIT),
    )(*args, *[a for cm in comm for a in cm.arrays])
    own, p, per = list(outs[:n_out]), n_out, []
    for cm in comm:
        per.append(list(outs[p:p + len(cm.arrays)]))
        p += len(cm.arrays)
    return own, per


def _comm_only(cm, name):
    return _call(None, grid=(), in_specs=[], out_specs=[], out_shape=[], name=name, args=[], comm=[cm])[1][0]


def _rms_fwd(x, g, name):
    S, D = x.shape
    tm = _tile(S, 512)

    def body(x_ref, g_ref, o_ref):
        xv = x_ref[...]
        r = lax.rsqrt(jnp.mean(xv * xv, axis=-1, keepdims=True) + NORM_EPS)
        o_ref[...] = (xv * r * g_ref[...]).astype(BF)

    return _call(body, grid=(S // tm,), name=name, args=[x, g],
                 in_specs=[pl.BlockSpec((tm, D), lambda i: (i, 0)), pl.BlockSpec((1, D), lambda i: (0, 0))],
                 out_specs=[pl.BlockSpec((tm, D), lambda i: (i, 0))], out_shape=[jax.ShapeDtypeStruct((S, D), BF)])[0][0]


def _ffn_up(h, wg, wu, name, comm=()):
    S, D = h.shape
    nb, _, Fb = wg.shape
    tm = _tile(S, 512)

    def body(h_ref, wg_ref, wu_ref, g_ref, u_ref, a_ref):
        hv = h_ref[...]
        g = _nn(hv, wg_ref[0])
        u = _nn(hv, wu_ref[0])
        g_ref[0] = g.astype(BF)
        u_ref[0] = u.astype(BF)
        a_ref[0] = (g * jax.nn.sigmoid(g) * u).astype(BF)

    act = pl.BlockSpec((1, tm, Fb), lambda j, i: (j, i, 0))
    w = pl.BlockSpec((1, D, Fb), lambda j, i: (j, 0, 0))
    shp = jax.ShapeDtypeStruct((nb, S, Fb), BF)
    return _call(body, grid=(nb, S // tm), name=name, args=[h, wg, wu], comm=comm,
                 in_specs=[pl.BlockSpec((tm, D), lambda j, i: (i, 0)), w, w], out_specs=[act, act, act], out_shape=[shp, shp, shp])


def _ffn_gate(h, wg, name, comm=()):
    S, D = h.shape
    nb, _, Fb = wg.shape
    tm = _tile(S, 512)

    def body(h_ref, wg_ref, g_ref):
        g_ref[0] = _nn(h_ref[...], wg_ref[0]).astype(BF)

    act = pl.BlockSpec((1, tm, Fb), lambda j, i: (j, i, 0))
    return _call(body, grid=(nb, S // tm), name=name, args=[h, wg], comm=comm,
                 in_specs=[pl.BlockSpec((tm, D), lambda j, i: (i, 0)), pl.BlockSpec((1, D, Fb), lambda j, i: (j, 0, 0))],
                 out_specs=[act], out_shape=[jax.ShapeDtypeStruct((nb, S, Fb), BF)])


def _ffn_up_act(h, wu, g, name, comm=()):
    S, D = h.shape
    nb, _, Fb = wu.shape
    tm = _tile(S, 512)

    def body(h_ref, wu_ref, g_ref, u_ref, a_ref):
        u = _nn(h_ref[...], wu_ref[0])
        gv = g_ref[0].astype(F32)
        u_ref[0] = u.astype(BF)
        a_ref[0] = (gv * jax.nn.sigmoid(gv) * u).astype(BF)

    act = pl.BlockSpec((1, tm, Fb), lambda j, i: (j, i, 0))
    shp = jax.ShapeDtypeStruct((nb, S, Fb), BF)
    return _call(body, grid=(nb, S // tm), name=name, args=[h, wu, g], comm=comm,
                 in_specs=[pl.BlockSpec((tm, D), lambda j, i: (i, 0)), pl.BlockSpec((1, D, Fb), lambda j, i: (j, 0, 0)), act],
                 out_specs=[act, act], out_shape=[shp, shp])


def _ffn_down_norm(a, wd, x, gn, name, comm=()):
    nb, S, Fb = a.shape
    D = wd.shape[2]
    tm = _tile(S, 512)

    nj = nb // FFN_PAIR

    def body(a_ref, wd_ref, x_ref, gn_ref, xo_ref, hn_ref, acc_ref):
        j = pl.program_id(1)

        @pl.when(j == 0)
        def _():
            acc_ref[...] = jnp.zeros_like(acc_ref)

        _acc_dots(acc_ref, [(a_ref[b], lambda cols, b=b: wd_ref[b, :, cols]) for b in range(FFN_PAIR)])

        @pl.when(j == nj - 1)
        def _():
            def chunk(t, carry):
                rows = pl.ds(pl.multiple_of(t * EPI_ROWS, EPI_ROWS), EPI_ROWS)
                xo = x_ref[rows, :] + 0.5 * acc_ref[rows, :]
                r = lax.rsqrt(jnp.mean(xo * xo, axis=-1, keepdims=True) + NORM_EPS)
                xo_ref[rows, :] = xo
                hn_ref[rows, :] = (xo * r * gn_ref[...]).astype(BF)
                return carry

            lax.fori_loop(0, tm // EPI_ROWS, chunk, 0)

    row = pl.BlockSpec((tm, D), lambda i, j: (i, 0))
    return _call(body, grid=(S // tm, nj), name=name, args=[a, wd, x, gn], comm=comm,
                 in_specs=[pl.BlockSpec((FFN_PAIR, tm, Fb), lambda i, j: (j, i, 0)), pl.BlockSpec((FFN_PAIR, Fb, D), lambda i, j: (j, 0, 0)),
                           _once((tm, D), lambda i, j: (i, 0)), pl.BlockSpec((1, D), lambda i, j: (0, 0))],
                 out_specs=[row, row], out_shape=[jax.ShapeDtypeStruct((S, D), F32), jax.ShapeDtypeStruct((S, D), BF)],
                 scratch=[pltpu.VMEM((tm, D), F32)])


def _ffn_down_loss(a, wd, x, gf, tgt, name):
    nb, S, Fb = a.shape
    D = wd.shape[2]
    tm = _tile(S, 512)

    nj = nb // FFN_PAIR

    def body(a_ref, wd_ref, x_ref, gf_ref, t_ref, dx_ref, dxb_ref, dgf_ref, loss_ref, acc_ref):
        i, j = pl.program_id(0), pl.program_id(1)

        @pl.when(j == 0)
        def _():
            acc_ref[...] = jnp.zeros_like(acc_ref)

        _acc_dots(acc_ref, [(a_ref[b], lambda cols, b=b: wd_ref[b, :, cols]) for b in range(FFN_PAIR)])

        @pl.when((j == nj - 1) & (i == 0))
        def _():
            dgf_ref[...] = jnp.zeros_like(dgf_ref)
            loss_ref[...] = jnp.zeros_like(loss_ref)

        @pl.when(j == nj - 1)
        def _():
            def chunk(t, carry):
                rows = pl.ds(pl.multiple_of(t * EPI_ROWS, EPI_ROWS), EPI_ROWS)
                xo = x_ref[rows, :] + 0.5 * acc_ref[rows, :]
                r = lax.rsqrt(jnp.mean(xo * xo, axis=-1, keepdims=True) + NORM_EPS)
                xh = xo * r
                gf = gf_ref[...]
                e = xh * gf - t_ref[rows, :]
                loss_ref[...] += jnp.sum(jnp.mean(e * e, axis=-1, keepdims=True), axis=0, keepdims=True) * 0.5
                dy = e * (1.0 / D)
                dgf_ref[...] += jnp.sum(dy * xh, axis=0, keepdims=True)
                dxh = dy * gf
                dx = r * (dxh - xh * jnp.mean(dxh * xh, axis=-1, keepdims=True))
                dx_ref[rows, :] = dx
                dxb_ref[rows, :] = (0.5 * dx).astype(BF)
                return carry

            lax.fori_loop(0, tm // EPI_ROWS, chunk, 0)

    row = pl.BlockSpec((tm, D), lambda i, j: (i, 0))
    once = _once((tm, D), lambda i, j: (i, 0))
    vec = pl.BlockSpec((1, D), lambda i, j: (0, 0))
    return _call(body, grid=(S // tm, nj), name=name, args=[a, wd, x, gf, tgt],
                 in_specs=[pl.BlockSpec((FFN_PAIR, tm, Fb), lambda i, j: (j, i, 0)), pl.BlockSpec((FFN_PAIR, Fb, D), lambda i, j: (j, 0, 0)),
                           once, vec, once],
                 out_specs=[row, row, vec, pl.BlockSpec((1, 128), lambda i, j: (0, 0))],
                 out_shape=[jax.ShapeDtypeStruct((S, D), F32), jax.ShapeDtypeStruct((S, D), BF), jax.ShapeDtypeStruct((1, D), F32),
                            jax.ShapeDtypeStruct((1, 128), F32)],
                 scratch=[pltpu.VMEM((tm, D), F32)])[0]


def _ffn_bwd_act(dyb, wd, g, u, name, comm=()):
    S, D = dyb.shape
    nb, Fb, _ = wd.shape
    tm = _tile(S, 512)

    def body(dy_ref, wd_ref, g_ref, u_ref, dg_ref, du_ref):
        da = _nt(dy_ref[...], wd_ref[0])
        gv = g_ref[0].astype(F32)
        uv = u_ref[0].astype(F32)
        sg = jax.nn.sigmoid(gv)
        du_ref[0] = (da * gv * sg).astype(BF)
        dg_ref[0] = (da * uv * sg * (1.0 + gv * (1.0 - sg))).astype(BF)

    act = pl.BlockSpec((1, tm, Fb), lambda j, i: (j, i, 0))
    shp = jax.ShapeDtypeStruct((nb, S, Fb), BF)
    return _call(body, grid=(nb, S // tm), name=name, args=[dyb, wd, g, u], comm=comm,
                 in_specs=[pl.BlockSpec((tm, D), lambda j, i: (i, 0)), pl.BlockSpec((1, Fb, D), lambda j, i: (j, 0, 0)), act, act],
                 out_specs=[act, act], out_shape=[shp, shp])


def _ffn_dwd(a, dyb, name, comm=()):
    nb, S, Fb = a.shape
    D = dyb.shape[1]
    ts = _tile(S, 512)
    ns = S // ts

    def body(a_ref, dy_ref, o_ref, acc_ref):
        s = pl.program_id(1)

        @pl.when(s == 0)
        def _():
            acc_ref[...] = jnp.zeros_like(acc_ref)

        acc_ref[...] += _tn(a_ref[0], dy_ref[...])

        @pl.when(s == ns - 1)
        def _():
            o_ref[0] = acc_ref[...].astype(BF)

    return _call(body, grid=(nb, ns), name=name, args=[a, dyb], comm=comm,
                 in_specs=[pl.BlockSpec((1, ts, Fb), lambda j, s: (j, s, 0)), pl.BlockSpec((ts, D), lambda j, s: (s, 0))],
                 out_specs=[pl.BlockSpec((1, Fb, D), lambda j, s: (j, 0, 0))], out_shape=[jax.ShapeDtypeStruct((nb, Fb, D), BF)],
                 scratch=[pltpu.VMEM((Fb, D), F32)])


def _ffn_dwgu(h, dg, du, name, comm=()):
    S, D = h.shape
    nb, _, Fb = dg.shape
    ts = _tile(S, 512)
    ns = S // ts

    def body(h_ref, dg_ref, du_ref, og_ref, ou_ref, accg_ref, accu_ref):
        s = pl.program_id(1)

        @pl.when(s == 0)
        def _():
            accg_ref[...] = jnp.zeros_like(accg_ref)
            accu_ref[...] = jnp.zeros_like(accu_ref)

        hv = h_ref[...]
        accg_ref[...] += _tn(hv, dg_ref[0])
        accu_ref[...] += _tn(hv, du_ref[0])

        @pl.when(s == ns - 1)
        def _():
            og_ref[0] = accg_ref[...].astype(BF)
            ou_ref[0] = accu_ref[...].astype(BF)

    act = pl.BlockSpec((1, ts, Fb), lambda j, s: (j, s, 0))
    out = pl.BlockSpec((1, D, Fb), lambda j, s: (j, 0, 0))
    shp = jax.ShapeDtypeStruct((nb, D, Fb), BF)
    return _call(body, grid=(nb, ns), name=name, args=[h, dg, du], comm=comm,
                 in_specs=[pl.BlockSpec((ts, D), lambda j, s: (s, 0)), act, act], out_specs=[out, out], out_shape=[shp, shp],
                 scratch=[pltpu.VMEM((D, Fb), F32), pltpu.VMEM((D, Fb), F32)])


def _dh_rms_bwd(pairs, blocked, tk, x, gn, dxo, out_scale, name, comm=()):
    S, D = x.shape
    nk = pairs[0][0].shape[0] if blocked else pairs[0][0].shape[1] // tk
    tm = _tile(S, 512)
    npair = len(pairs)

    def body(*refs):
        ins = refs[: 2 * npair]
        x_ref, gn_ref, dxo_ref, dx_ref, dxb_ref, dgn_ref, acc_ref = refs[2 * npair:]
        i, k = pl.program_id(0), pl.program_id(1)

        @pl.when(k == 0)
        def _():
            acc_ref[...] = jnp.zeros_like(acc_ref)

        if blocked:
            terms = [(ins[2 * p][0], lambda cols, r=ins[2 * p + 1]: r[0, cols, :]) for p in range(npair)]
        else:
            terms = [(ins[2 * p][...], lambda cols, r=ins[2 * p + 1]: r[:, cols]) for p in range(npair)]
        _acc_dots(acc_ref, terms, transposed_rhs=blocked)

        @pl.when((k == nk - 1) & (i == 0))
        def _():
            dgn_ref[...] = jnp.zeros_like(dgn_ref)

        @pl.when(k == nk - 1)
        def _():
            def chunk(t, carry):
                rows = pl.ds(pl.multiple_of(t * EPI_ROWS, EPI_ROWS), EPI_ROWS)
                xv = x_ref[rows, :]
                r = lax.rsqrt(jnp.mean(xv * xv, axis=-1, keepdims=True) + NORM_EPS)
                xh = xv * r
                dh = acc_ref[rows, :]
                dgn_ref[...] += jnp.sum(dh * xh, axis=0, keepdims=True)
                dxh = dh * gn_ref[...]
                dx = dxo_ref[rows, :] + r * (dxh - xh * jnp.mean(dxh * xh, axis=-1, keepdims=True))
                dx_ref[rows, :] = dx
                dxb_ref[rows, :] = (out_scale * dx).astype(BF)
                return carry

            lax.fori_loop(0, tm // EPI_ROWS, chunk, 0)

    if blocked:
        lspec = pl.BlockSpec((1, tm, tk), lambda i, k: (k, i, 0))
        rspec = pl.BlockSpec((1, D, tk), lambda i, k: (k, 0, 0))
    else:
        lspec = pl.BlockSpec((tm, tk), lambda i, k: (i, k))
        rspec = pl.BlockSpec((tk, D), lambda i, k: (k, 0))
    row = pl.BlockSpec((tm, D), lambda i, k: (i, 0))
    once = _once((tm, D), lambda i, k: (i, 0))
    vec = pl.BlockSpec((1, D), lambda i, k: (0, 0))
    flat = [t for pr in pairs for t in pr]
    return _call(body, grid=(S // tm, nk), name=name, args=[*flat, x, gn, dxo], comm=comm,
                 in_specs=[lspec, rspec] * npair + [once, vec, once], out_specs=[row, row, vec],
                 out_shape=[jax.ShapeDtypeStruct((S, D), F32), jax.ShapeDtypeStruct((S, D), BF), jax.ShapeDtypeStruct((1, D), F32)],
                 scratch=[pltpu.VMEM((tm, D), F32)])


def _mm_nn(a, b, tm, tn, col0, col1, dtype, name, comm=()):
    M, K = a.shape
    n0, nn = col0 // tn, (col1 - col0) // tn

    def body(a_ref, b_ref, o_ref):
        o_ref[...] = _nn(a_ref[...], b_ref[...]).astype(dtype)

    return _call(body, grid=(nn, M // tm), name=name, args=[a, b], comm=comm,
                 in_specs=[pl.BlockSpec((tm, K), lambda n, i: (i, 0)), pl.BlockSpec((K, tn), lambda n, i: (0, n0 + n))],
                 out_specs=[pl.BlockSpec((tm, tn), lambda n, i: (i, n))], out_shape=[jax.ShapeDtypeStruct((M, nn * tn), dtype)])


def _mm_tn(a, b, tm, tn, ts, blocked, name, comm=(), ncols=None):
    S, M = a.shape
    N = b.shape[1] if ncols is None else ncols
    ns = S // ts

    def body(a_ref, b_ref, o_ref, acc_ref):
        s = pl.program_id(2)

        @pl.when(s == 0)
        def _():
            acc_ref[...] = jnp.zeros_like(acc_ref)

        acc_ref[...] += _tn(a_ref[...], b_ref[...])

        @pl.when(s == ns - 1)
        def _():
            if blocked:
                o_ref[0] = acc_ref[...].astype(BF)
            else:
                o_ref[...] = acc_ref[...].astype(BF)

    if blocked:
        ospec = pl.BlockSpec((1, tm, tn), lambda i, n, s: (n, i, 0))
        oshape = jax.ShapeDtypeStruct((N // tn, M, tn), BF)
    else:
        ospec = pl.BlockSpec((tm, tn), lambda i, n, s: (i, n))
        oshape = jax.ShapeDtypeStruct((M, N), BF)
    return _call(body, grid=(M // tm, N // tn, ns), name=name, args=[a, b], comm=comm,
                 in_specs=[pl.BlockSpec((ts, tm), lambda i, n, s: (s, i)), pl.BlockSpec((ts, tn), lambda i, n, s: (s, n))],
                 out_specs=[ospec], out_shape=[oshape], scratch=[pltpu.VMEM((tm, tn), F32)])


def _rope_tables(S):
    half = ROPE_DIM // 2
    inv_freq = ROPE_THETA ** (-jnp.arange(0, ROPE_DIM, 2, dtype=F32) / ROPE_DIM)
    ang = jnp.arange(S, dtype=F32)[:, None] * inv_freq[None, :]
    cos, sin = jnp.cos(ang), jnp.sin(ang)
    zeros = jnp.zeros((S, HEAD_DIM - ROPE_DIM), F32)
    c = jnp.concatenate([cos, cos, jnp.ones((S, HEAD_DIM - ROPE_DIM), F32)], axis=1)
    sm = jnp.concatenate([-sin, jnp.zeros((S, half), F32), zeros], axis=1)
    sp = jnp.concatenate([jnp.zeros((S, half), F32), sin, zeros], axis=1)
    return c, sm, sp


def _rope(t, c, sm, sp):
    return t * c + pltpu.roll(t, HEAD_DIM - ROPE_DIM // 2, 1) * sm + pltpu.roll(t, ROPE_DIM // 2, 1) * sp


def _rope_t(dy, c, sm, sp):
    return dy * c + pltpu.roll(dy * sm, ROPE_DIM // 2, 1) + pltpu.roll(dy * sp, HEAD_DIM - ROPE_DIM // 2, 1)


def _att_mask(i):
    qi = lax.broadcasted_iota(jnp.int32, (BLK, 2 * BLK), 0)
    kj = lax.broadcasted_iota(jnp.int32, (BLK, 2 * BLK), 1)
    diff = qi + BLK - kj
    first_key = jnp.where(i > 0, 0, BLK)
    return (diff >= 0) & (diff <= BLK) & (kj >= first_key)


def _res_rows(r, i, n, d):
    if d == 1:
        return pl.ds(pl.multiple_of(i * n, n), n)
    return pl.ds(r + i * (n * d), n, stride=d)


def _att_specs(S, gi):
    def sect(off):
        base = (off + gi * GROUP_W) // HEAD_DIM
        return _once((S, HEAD_DIM), lambda hh, r: (0, base + hh))

    tab = pl.BlockSpec((S, HEAD_DIM), lambda hh, r: (0, 0))
    head = pl.BlockSpec((S, HEAD_DIM), lambda hh, r: (0, hh))
    return sect, tab, head


def _att_fwd(qkv, tabs, gi, d, name, comm=()):
    S = qkv.shape[0]
    L = S // d
    sect, tab, head = _att_specs(S, gi)
    nblk = L // BLK
    scale = HEAD_DIM ** -0.5

    def body(q_ref, k_ref, v_ref, c_ref, sm_ref, sp_ref, o_ref, lse_ref, qr, kp, vp):
        r = pl.program_id(1)
        res = _res_rows(r, 0, L, d)
        c, sm, sp = c_ref[res, :], sm_ref[res, :], sp_ref[res, :]
        qr[...] = _rope(q_ref[res, :], c, sm, sp).astype(BF)
        kp[pl.ds(0, BLK), :] = jnp.zeros((BLK, HEAD_DIM), BF)
        vp[pl.ds(0, BLK), :] = jnp.zeros((BLK, HEAD_DIM), BF)
        kp[pl.ds(BLK, L), :] = _rope(k_ref[res, :], c, sm, sp).astype(BF)
        vp[pl.ds(BLK, L), :] = v_ref[res, :].astype(BF)

        def blk(i, carry):
            r0 = pl.multiple_of(i * BLK, BLK)
            s = _nt(qr[pl.ds(r0, BLK), :], kp[pl.ds(r0, 2 * BLK), :]) * scale
            s = jnp.where(_att_mask(i), s, NEG)
            m = jnp.max(s, axis=-1, keepdims=True)
            p = jnp.exp(s - m)
            l = jnp.sum(p, axis=-1, keepdims=True)
            out = _res_rows(r, i, BLK, d)
            o_ref[out, :] = _nn(p.astype(BF), vp[pl.ds(r0, 2 * BLK), :]) / l
            lse_ref[out, :] = jnp.broadcast_to(m + jnp.log(l), (BLK, HEAD_DIM))
            return carry

        lax.fori_loop(0, nblk, blk, 0, unroll=min(4, nblk))

    shp = jax.ShapeDtypeStruct((S, GROUP_W), F32)
    return _call(body, grid=(HEADS_PER_GROUP, d), name=name, args=[qkv, qkv, qkv, *tabs], comm=comm,
                 in_specs=[sect(Q_OFF), sect(K_OFF), sect(V_OFF), tab, tab, tab], out_specs=[head, head], out_shape=[shp, shp],
                 scratch=[pltpu.VMEM((L, HEAD_DIM), BF), pltpu.VMEM((L + BLK, HEAD_DIM), BF), pltpu.VMEM((L + BLK, HEAD_DIM), BF)])


def _att_combine(os, lses, name):
    S = os[0].shape[0]
    tm = _tile(S, 512)

    def body(o0, o1, o2, l0, l1, l2, oa_ref, lse_ref):
        a, b, c = l0[...], l1[...], l2[...]
        mx = jnp.maximum(jnp.maximum(a, b), c)
        wa, wb, wc = jnp.exp(a - mx), jnp.exp(b - mx), jnp.exp(c - mx)
        den = wa + wb + wc
        oa_ref[...] = ((wa * o0[...] + wb * o1[...] + wc * o2[...]) / den).astype(BF)
        lse_ref[...] = mx + jnp.log(den)

    row = pl.BlockSpec((tm, GROUP_W), lambda i: (i, 0))
    return _call(body, grid=(S // tm,), name=name, args=[*os, *lses], in_specs=[row] * 6, out_specs=[row, row],
                 out_shape=[jax.ShapeDtypeStruct((S, GROUP_W), BF), jax.ShapeDtypeStruct((S, GROUP_W), F32)])[0]


def _att_bwd(qkv, tabs, do, lse, dvec, gi, d, name, comm=()):
    S = qkv.shape[0]
    L = S // d
    sect, tab, head = _att_specs(S, gi)
    stat = _once((S, HEAD_DIM), lambda hh, r: (0, hh))
    nblk = L // BLK
    scale = HEAD_DIM ** -0.5

    def body(q_ref, k_ref, v_ref, c_ref, sm_ref, sp_ref, do_ref, lse_ref, dv_ref, dq_out, dk_out, dv_out, qr, kp, vp, dkp, dvp):
        r = pl.program_id(1)
        res = _res_rows(r, 0, L, d)
        c, sm, sp = c_ref[res, :], sm_ref[res, :], sp_ref[res, :]
        qr[...] = _rope(q_ref[res, :], c, sm, sp).astype(BF)
        kp[pl.ds(0, BLK), :] = jnp.zeros((BLK, HEAD_DIM), BF)
        vp[pl.ds(0, BLK), :] = jnp.zeros((BLK, HEAD_DIM), BF)
        kp[pl.ds(BLK, L), :] = _rope(k_ref[res, :], c, sm, sp).astype(BF)
        vp[pl.ds(BLK, L), :] = v_ref[res, :].astype(BF)
        dkp[...] = jnp.zeros_like(dkp)
        dvp[...] = jnp.zeros_like(dvp)

        def blk(i, carry):
            r0 = pl.multiple_of(i * BLK, BLK)
            rows, win, pos = pl.ds(r0, BLK), pl.ds(r0, 2 * BLK), _res_rows(r, i, BLK, d)
            q, kw, vw, dob = qr[rows, :], kp[win, :], vp[win, :], do_ref[pos, :].astype(BF)
            s = jnp.where(_att_mask(i), _nt(q, kw) * scale, NEG)
            p = jnp.exp(s - lse_ref[pos, :][:, :1])
            ds = p * (_nt(dob, vw) - dv_ref[pos, :][:, :1]) * scale
            dsb = ds.astype(BF)
            dq_out[pos, :] = _rope_t(_nn(dsb, kw), c_ref[pos, :], sm_ref[pos, :], sp_ref[pos, :])
            dkp[win, :] += _tn(dsb, q)
            dvp[win, :] += _tn(p.astype(BF), dob)
            return carry

        lax.fori_loop(0, nblk, blk, 0, unroll=2)
        dk_out[res, :] = _rope_t(dkp[pl.ds(BLK, L), :], c, sm, sp)
        dv_out[res, :] = dvp[pl.ds(BLK, L), :]

    shp = jax.ShapeDtypeStruct((S, GROUP_W), F32)
    return _call(body, grid=(HEADS_PER_GROUP, d), name=name, args=[qkv, qkv, qkv, *tabs, do, lse, dvec], comm=comm,
                 in_specs=[sect(Q_OFF), sect(K_OFF), sect(V_OFF), tab, tab, tab, stat, stat, stat],
                 out_specs=[head, head, head], out_shape=[shp, shp, shp],
                 scratch=[pltpu.VMEM((L, HEAD_DIM), BF), pltpu.VMEM((L + BLK, HEAD_DIM), BF), pltpu.VMEM((L + BLK, HEAD_DIM), BF),
                          pltpu.VMEM((L + BLK, HEAD_DIM), F32), pltpu.VMEM((L + BLK, HEAD_DIM), F32)])


def _sg_parts(u_ref, vs_ref, g_ref, b_ref):
    uv = u_ref[...].astype(F32)
    vv = vs_ref[...].astype(F32)
    vg = _gelu(vv)
    mu = jnp.mean(vg, axis=-1, keepdims=True)
    vc = vg - mu
    rs = lax.rsqrt(jnp.mean(vc * vc, axis=-1, keepdims=True) + LN_EPS)
    y = vc * rs
    return uv, vv, rs, y, y * g_ref[...] + b_ref[...]


def _sg_wmask():
    t = lax.broadcasted_iota(jnp.int32, (BLK, BLK), 0)
    s = lax.broadcasted_iota(jnp.int32, (BLK, BLK), 1)
    return s <= t


def _sg_fwd(proj, sgw, sgbT, lng, lnb, name):
    S, P = proj.shape

    def body(u_ref, vs_ref, w_ref, bt_ref, g_ref, b_ref, z_ref):
        uv, _, _, _, vln = _sg_parts(u_ref, vs_ref, g_ref, b_ref)
        ug = _gelu(uv)
        vb = vln.astype(BF)
        mask = _sg_wmask()
        bt = bt_ref[...]
        for g in range(SG_GROUPS):
            cols = slice(g * BLK, (g + 1) * BLK)
            w = jnp.where(mask, w_ref[g], 0.0).astype(BF)
            sp = _nn(w, vb[:, cols]) + bt[:, g:g + 1]
            z_ref[:, cols] = (ug[:, cols] * sp).astype(BF)

    tile = lambda off: pl.BlockSpec((BLK, SG_W), lambda i: (i, off // SG_W))
    full = lambda shape: pl.BlockSpec(shape, lambda i: (0,) * len(shape))
    return _call(body, grid=(S // BLK,), name=name, args=[proj, proj, sgw, sgbT, lng, lnb],
                 in_specs=[tile(R_U), tile(R_VS), full((SG_GROUPS, BLK, BLK)), full((BLK, BLK)), full((1, SG_W)), full((1, SG_W))],
                 out_specs=[pl.BlockSpec((BLK, SG_W), lambda i: (i, 0))], out_shape=[jax.ShapeDtypeStruct((S, SG_W), BF)])[0][0]


def _sg_bwd(proj, dz, sgw, sgbT, lng, lnb, name):
    S, P = proj.shape

    def body(u_ref, vs_ref, dz_ref, w_ref, bt_ref, g_ref, b_ref, du_ref, dvs_ref, dw_ref, dbt_ref, dg_ref, db_ref, dvln):
        @pl.when(pl.program_id(0) == 0)
        def _():
            dw_ref[...] = jnp.zeros_like(dw_ref)
            dbt_ref[...] = jnp.zeros_like(dbt_ref)
            dg_ref[...] = jnp.zeros_like(dg_ref)
            db_ref[...] = jnp.zeros_like(db_ref)

        uv, vv, rs, y, vln = _sg_parts(u_ref, vs_ref, g_ref, b_ref)
        ug = _gelu(uv)
        vb = vln.astype(BF)
        dzv = dz_ref[...].astype(F32)
        dsp = dzv * ug
        dspb = dsp.astype(BF)
        mask = _sg_wmask()
        bt = bt_ref[...]
        lane = lax.broadcasted_iota(jnp.int32, (BLK, BLK), 1)
        dbt = jnp.zeros((BLK, BLK), F32)
        for g in range(SG_GROUPS):
            cols = slice(g * BLK, (g + 1) * BLK)
            w = jnp.where(mask, w_ref[g], 0.0).astype(BF)
            sp = _nn(w, vb[:, cols]) + bt[:, g:g + 1]
            du_ref[:, cols] = (dzv[:, cols] * sp * _gelu_grad(uv[:, cols])).astype(BF)
            dw_ref[g] += jnp.where(mask, _nt(dspb[:, cols], vb[:, cols]), 0.0)
            dbt = dbt + jnp.where(lane == g, jnp.sum(dsp[:, cols], axis=-1, keepdims=True), 0.0)
            dvln[:, cols] = _tn(w, dspb[:, cols])
        dbt_ref[...] += dbt
        dvl = dvln[...]
        dg_ref[...] += jnp.sum(dvl * y, axis=0, keepdims=True)
        db_ref[...] += jnp.sum(dvl, axis=0, keepdims=True)
        dy = dvl * g_ref[...]
        dvg = rs * (dy - jnp.mean(dy, axis=-1, keepdims=True) - y * jnp.mean(dy * y, axis=-1, keepdims=True))
        dvs_ref[...] = (dvg * _gelu_grad(vv)).astype(BF)

    tile = lambda off: pl.BlockSpec((BLK, SG_W), lambda i: (i, off // SG_W))
    full = lambda shape: pl.BlockSpec(shape, lambda i: (0,) * len(shape))
    row = pl.BlockSpec((BLK, SG_W), lambda i: (i, 0))
    return _call(body, grid=(S // BLK,), name=name, args=[proj, proj, dz, sgw, sgbT, lng, lnb],
                 in_specs=[tile(R_U), tile(R_VS), row, full((SG_GROUPS, BLK, BLK)), full((BLK, BLK)), full((1, SG_W)), full((1, SG_W))],
                 out_specs=[row, row, full((SG_GROUPS, BLK, BLK)), full((BLK, BLK)), full((1, SG_W)), full((1, SG_W))],
                 out_shape=[jax.ShapeDtypeStruct((S, SG_W), BF), jax.ShapeDtypeStruct((S, SG_W), BF),
                            jax.ShapeDtypeStruct((SG_GROUPS, BLK, BLK), F32), jax.ShapeDtypeStruct((BLK, BLK), F32),
                            jax.ShapeDtypeStruct((1, SG_W), F32), jax.ShapeDtypeStruct((1, SG_W), F32)],
                 scratch=[pltpu.VMEM((BLK, SG_W), F32)])[0]


def _gate_merge(oatt, z, watt, wsg, proj, name, comm=()):
    S = oatt.shape[0]
    nb, _, Db = watt.shape
    D = nb * Db
    tm = _tile(S, 512)
    ga, gs = R_GA // Db, (R_GA + D) // Db

    def body(oa_ref, z_ref, wa_ref, ws_ref, ga_ref, gs_ref, ya_ref, ys_ref, mg_ref):
        ya = _nn(oa_ref[...], wa_ref[0])
        ys = _nn(z_ref[...], ws_ref[0])
        ya_ref[...] = ya.astype(BF)
        ys_ref[...] = ys.astype(BF)
        mg_ref[...] = (jax.nn.sigmoid(ga_ref[...].astype(F32)) * ya + jax.nn.sigmoid(gs_ref[...].astype(F32)) * ys).astype(BF)

    out = pl.BlockSpec((tm, Db), lambda j, i: (i, j))
    shp = jax.ShapeDtypeStruct((S, D), BF)
    return _call(body, grid=(nb, S // tm), name=name, args=[oatt, z, watt, wsg, proj, proj], comm=comm,
                 in_specs=[pl.BlockSpec((tm, GROUP_W), lambda j, i: (i, 0)), pl.BlockSpec((tm, SG_W), lambda j, i: (i, 0)),
                           pl.BlockSpec((1, GROUP_W, Db), lambda j, i: (j, 0, 0)), pl.BlockSpec((1, SG_W, Db), lambda j, i: (j, 0, 0)),
                           pl.BlockSpec((tm, Db), lambda j, i: (i, ga + j)), pl.BlockSpec((tm, Db), lambda j, i: (i, gs + j))],
                 out_specs=[out, out, out], out_shape=[shp, shp, shp])


def _mix_out(merged, wout, x, gn, name):
    S, D = x.shape
    tm = _tile(S, 256)

    def body(m_ref, w_ref, x_ref, gn_ref, xo_ref, hn_ref):
        xo = x_ref[...] + _nn(m_ref[...], w_ref[...])
        r = lax.rsqrt(jnp.mean(xo * xo, axis=-1, keepdims=True) + NORM_EPS)
        xo_ref[...] = xo
        hn_ref[...] = (xo * r * gn_ref[...]).astype(BF)

    row = pl.BlockSpec((tm, D), lambda i: (i, 0))
    return _call(body, grid=(S // tm,), name=name, args=[merged, wout, x, gn],
                 in_specs=[row, pl.BlockSpec((D, D), lambda i: (0, 0)), row, pl.BlockSpec((1, D), lambda i: (0, 0))],
                 out_specs=[row, row], out_shape=[jax.ShapeDtypeStruct((S, D), F32), jax.ShapeDtypeStruct((S, D), BF)])[0]


def _mix_bwd_gate(dmix, wout, ya, ys, proj, name):
    S, D = dmix.shape
    tm, tn = _tile(S, 512), 512
    ga, gs = R_GA // tn, (R_GA + D) // tn

    def body(dm_ref, w_ref, ya_ref, ys_ref, ga_ref, gs_ref, dya_ref, dys_ref, dga_ref, dgs_ref):
        dm = _nt(dm_ref[...], w_ref[...])
        sa = jax.nn.sigmoid(ga_ref[...].astype(F32))
        ss = jax.nn.sigmoid(gs_ref[...].astype(F32))
        dya_ref[...] = (dm * sa).astype(BF)
        dys_ref[...] = (dm * ss).astype(BF)
        dga_ref[...] = (dm * ya_ref[...].astype(F32) * sa * (1.0 - sa)).astype(BF)
        dgs_ref[...] = (dm * ys_ref[...].astype(F32) * ss * (1.0 - ss)).astype(BF)

    out = pl.BlockSpec((tm, tn), lambda i, n: (i, n))
    shp = jax.ShapeDtypeStruct((S, D), BF)
    return _call(body, grid=(S // tm, D // tn), name=name, args=[dmix, wout, ya, ys, proj, proj],
                 in_specs=[pl.BlockSpec((tm, D), lambda i, n: (i, 0)), pl.BlockSpec((tn, D), lambda i, n: (n, 0)), out, out,
                           pl.BlockSpec((tm, tn), lambda i, n: (i, ga + n)), pl.BlockSpec((tm, tn), lambda i, n: (i, gs + n))],
                 out_specs=[out] * 4, out_shape=[shp] * 4)[0]


def _att_sg_dout(dya, dys, watt, wsg, oatt, name, comm=()):
    S, D = dya.shape
    nb, _, Db = watt.shape
    tm = _tile(S, 512)

    def body(dya_ref, dys_ref, wa_ref, ws_ref, oa_ref, do_ref, dz_ref, dvec_ref, acca, accs):
        j = pl.program_id(1)

        @pl.when(j == 0)
        def _():
            acca[...] = jnp.zeros_like(acca)
            accs[...] = jnp.zeros_like(accs)

        acca[...] += _nt(dya_ref[...], wa_ref[0])
        accs[...] += _nt(dys_ref[...], ws_ref[0])

        @pl.when(j == nb - 1)
        def _():
            dov = acca[...]
            do_ref[...] = dov
            dz_ref[...] = accs[...].astype(BF)
            prod = dov * oa_ref[...].astype(F32)
            for hh in range(HEADS_PER_GROUP):
                cols = slice(hh * HEAD_DIM, (hh + 1) * HEAD_DIM)
                dvec_ref[:, cols] = jnp.broadcast_to(jnp.sum(prod[:, cols], axis=-1, keepdims=True), (tm, HEAD_DIM))

    blk = pl.BlockSpec((tm, Db), lambda i, j: (i, j))
    att = pl.BlockSpec((tm, GROUP_W), lambda i, j: (i, 0))
    return _call(body, grid=(S // tm, nb), name=name, args=[dya, dys, watt, wsg, oatt], comm=comm,
                 in_specs=[blk, blk, pl.BlockSpec((1, GROUP_W, Db), lambda i, j: (j, 0, 0)), pl.BlockSpec((1, SG_W, Db), lambda i, j: (j, 0, 0)), att],
                 out_specs=[att, pl.BlockSpec((tm, SG_W), lambda i, j: (i, 0)), att],
                 out_shape=[jax.ShapeDtypeStruct((S, GROUP_W), F32), jax.ShapeDtypeStruct((S, SG_W), BF), jax.ShapeDtypeStruct((S, GROUP_W), F32)],
                 scratch=[pltpu.VMEM((tm, GROUP_W), F32), pltpu.VMEM((tm, SG_W), F32)])[0]


def _small_allreduce(pack, name):
    R = pack.shape[0]

    def body(p_ref, o_ref, gath, send, recv):
        x, y, c = _place()
        me = 4 * x + 2 * y + c
        gath[me] = p_ref[...]
        copies = []
        for r in range(1, N_DEV):
            px, py, pc = _flip(x, r & 4), _flip(y, r & 2), _flip(c, r & 1)
            peer = 4 * px + 2 * py + pc
            mk = lambda dst: pltpu.make_async_remote_copy(src_ref=p_ref, dst_ref=dst, send_sem=send.at[r - 1], recv_sem=recv.at[r - 1],
                                                          device_id=(px, py, pc), device_id_type=MESH)
            snd = mk(gath.at[me])
            snd.start()
            copies.append((snd, mk(gath.at[peer])))
        for snd, rcv in copies:
            rcv.wait_recv()
            snd.wait_send()
        acc = gath[0]
        for s in range(1, N_DEV):
            acc = acc + gath[s]
        o_ref[...] = acc

    vm = pl.BlockSpec(memory_space=pltpu.VMEM)
    return pl.pallas_call(
        body, name=name, in_specs=[vm], out_specs=vm, out_shape=jax.ShapeDtypeStruct(pack.shape, F32),
        scratch_shapes=[pltpu.VMEM((N_DEV, R, 128), F32), pltpu.SemaphoreType.DMA((7,)), pltpu.SemaphoreType.DMA((7,))],
        compiler_params=pltpu.CompilerParams(vmem_limit_bytes=VMEM_LIMIT),
    )(pack)


def _row_tile(R, C, elems=262144):
    tr = R
    while tr * C > elems and tr % 32 == 0:
        tr //= 2
    return tr


def _pair_add(parts, other, name):
    _, R, C = parts.shape
    tr = _row_tile(R, C, 1048576)

    def body(c_ref, p_ref, o_ref, s_ref):
        s_ref[0] = (p_ref[0].astype(F32) + o_ref[0].astype(F32)).astype(BF)

    core = lax.axis_index("c").astype(jnp.int32).reshape(1)
    return pl.pallas_call(
        body, name=name,
        grid_spec=pltpu.PrefetchScalarGridSpec(
            num_scalar_prefetch=1, grid=(N_CHIP, R // tr),
            in_specs=[pl.BlockSpec((1, tr, C), lambda q, i, c: (2 * q + c[0], i, 0)), pl.BlockSpec((1, tr, C), lambda q, i, c: (q, i, 0))],
            out_specs=pl.BlockSpec((1, tr, C), lambda q, i, c: (q, i, 0))),
        out_shape=jax.ShapeDtypeStruct((N_CHIP, R, C), BF),
        compiler_params=pltpu.CompilerParams(dimension_semantics=("arbitrary", "arbitrary"), vmem_limit_bytes=VMEM_LIMIT),
    )(core, parts, other)


def _adamw(parts, w, m, v, name):
    ns, R, C = parts.shape
    tr = _row_tile(R, C, 524288)
    c1 = 1.0 - ADAM_B1 ** ADAM_STEP
    c2 = 1.0 - ADAM_B2 ** ADAM_STEP

    def body(p_ref, w_ref, m_ref, v_ref, g_ref, d_ref, nm_ref, nv_ref):
        g = p_ref[0].astype(F32)
        for s in range(1, ns):
            g = g + p_ref[s].astype(F32)
        mn = ADAM_B1 * m_ref[...] + (1.0 - ADAM_B1) * g
        vn = ADAM_B2 * v_ref[...] + (1.0 - ADAM_B2) * (g * g)
        g_ref[...] = g
        nm_ref[...] = mn
        nv_ref[...] = vn
        d_ref[...] = -ADAM_LR * ((mn / c1) / (jnp.sqrt(vn / c2) + ADAM_EPS) + ADAM_WD * w_ref[...])

    row = pl.BlockSpec((tr, C), lambda i: (i, 0))
    shp = jax.ShapeDtypeStruct((R, C), F32)
    return _call(body, grid=(R // tr,), name=name, args=[parts, w, m, v],
                 in_specs=[pl.BlockSpec((ns, tr, C), lambda i: (0, i, 0)), row, row, row], out_specs=[row] * 4, out_shape=[shp] * 4)[0]


def _pad_rows(a, rows):
    return jnp.pad(a, ((0, rows - a.shape[0]), (0, 0)))


def kernel(x, ffn1_norm, ffn1_w_gate, ffn1_w_up, ffn1_w_down, mix_norm, w_in, sg_ln_g, sg_ln_b, sg_w, sg_b, w_att_out, w_sg_out, w_out, ffn2_norm, ffn2_w_gate, ffn2_w_up, ffn2_w_down, final_norm, loss_target, m_ffn1_norm, m_ffn1_w_gate, m_ffn1_w_up, m_ffn1_w_down, m_mix_norm, m_w_in, m_sg_ln_g, m_sg_ln_b, m_sg_w, m_sg_b, m_w_att_out, m_w_sg_out, m_w_out, m_ffn2_norm, m_ffn2_w_gate, m_ffn2_w_up, m_ffn2_w_down, m_final_norm, v_ffn1_norm, v_ffn1_w_gate, v_ffn1_w_up, v_ffn1_w_down, v_mix_norm, v_w_in, v_sg_ln_g, v_sg_ln_b, v_sg_w, v_sg_b, v_w_att_out, v_w_sg_out, v_w_out, v_ffn2_norm, v_ffn2_w_gate, v_ffn2_w_up, v_ffn2_w_down, v_final_norm):
    S, D = x.shape[1], x.shape[2]
    Pb = w_in.shape[2]
    P = N_DEV * Pb
    assert P == GA_OFF + 2 * D and D % (N_DEV * 128) == 0 and S % (BLK * DILATIONS[-1]) == 0
    xs, tgt = x[0], loss_target[0]

    sharded = dict(ffn1_w_gate=ffn1_w_gate, ffn1_w_up=ffn1_w_up, ffn1_w_down=ffn1_w_down, w_in=w_in, w_att_out=w_att_out,
                   w_sg_out=w_sg_out, w_out=w_out, ffn2_w_gate=ffn2_w_gate, ffn2_w_up=ffn2_w_up, ffn2_w_down=ffn2_w_down)
    sb = {n: w[0].astype(BF) for n, w in sharded.items()}

    (wg1,) = _comm_only(_Gather([sb["ffn1_w_gate"]]), "gather_ffn1")
    h1 = _rms_fwd(xs, ffn1_norm, "rms1")
    win_top, win_bot = sb["w_in"][: D // 4], sb["w_in"][D // 4:]
    (g1,), ((wu1,),) = _ffn_gate(h1, wg1, "ffn1_gate", comm=[_Gather([sb["ffn1_w_up"]], 0.9, 0.55)])
    (u1, a1), ((wd1, win8a),) = _ffn_up_act(h1, wu1, g1, "ffn1_up_act", comm=[_Gather([sb["ffn1_w_down"], win_top], 1.0, 0.6)])
    (x1, h2), ((win8b,),) = _ffn_down_norm(a1, wd1, xs, mix_norm, "ffn1_down", comm=[_Gather([win_bot], 0.9, 0.55)])
    win = jnp.concatenate([w8.transpose(1, 0, 2).reshape(w8.shape[1], P) for w8 in (win8a, win8b)], axis=0)
    tm_proj = _tile(S, 1024)
    (qkv,), ((wg2,),) = _mm_nn(h2, win, tm_proj, 512, 0, U_OFF, F32, "proj_qkv", comm=[_Gather([sb["ffn2_w_gate"]], 1.0, 0.7)])
    (rest,), ((wu2,),) = _mm_nn(h2, win, tm_proj, 512, U_OFF, P, BF, "proj_rest", comm=[_Gather([sb["ffn2_w_up"]], 0.75, 0.45)])
    tabs = _rope_tables(S)
    rides = [[_Gather([sb["w_att_out"], sb["w_sg_out"]], 0.9, 0.5)], [_Gather([sb["w_out"]], 0.85, 0.45)], []]
    os, lses, late = [], [], []
    for gi, d in enumerate(DILATIONS):
        (o, l), got_here = _att_fwd(qkv, tabs, gi, d, f"att_fwd{gi}", comm=rides[gi])
        late += [w for g in got_here for w in g]
        os.append(o)
        lses.append(l)
    watt, wsg, wout8 = late
    wout = wout8.reshape(D, D)
    oatt, lse = _att_combine(os, lses, "att_combine")
    sgw = sg_w[0]
    sgbT = jnp.pad(sg_b[0].T, ((0, 0), (0, BLK - SG_GROUPS)))
    z = _sg_fwd(rest, sgw, sgbT, sg_ln_g, sg_ln_b, "sg_fwd")
    (ya, ys, merged), _ = _gate_merge(oatt, z, watt, wsg, rest, "gate_merge")
    x2, h3 = _mix_out(merged, wout, x1, ffn2_norm, "mix_out")
    (g3, u3, a3), ((wd2,),) = _ffn_up(h3, wg2, wu2, "ffn2_up", comm=[_Gather([sb["ffn2_w_down"]], 0.6, 0.35)])
    dx3, dyb3, d_final, loss_part = _ffn_down_loss(a3, wd2, x2, final_norm.reshape(1, D), tgt, "ffn2_down_loss")

    Fb = wg2.shape[2]
    Db = watt.shape[2]
    p_pad = -(-P // PROJ_TK) * PROJ_TK
    winT = jnp.concatenate([w8.transpose(0, 2, 1).reshape(P, w8.shape[1]) for w8 in (win8a, win8b)], axis=1)
    winT = jnp.pad(winT, ((0, p_pad - P), (0, 0)))
    (dg3, du3), _ = _ffn_bwd_act(dyb3, wd2, g3, u3, "ffn2_bwd_act")
    (dwd2,), _ = _ffn_dwd(a3, dyb3, "ffn2_dwd")
    (dwg2, dwu2), _ = _ffn_dwgu(h3, dg3, du3, "ffn2_dwgu")
    ffn2_parts = [dwd2, dwg2, dwu2]
    (dx2, dmixb, d_ffn2n), (ffn2_other,) = _dh_rms_bwd([(dg3, wg2), (du3, wu2)], True, Fb, x2, ffn2_norm, dx3, 1.0, "ffn2_dh",
                                                     comm=[_Swap(ffn2_parts)])
    ffn2_sums = [_pair_add(p, o, f"pair_ffn2_{i}") for i, (p, o) in enumerate(zip(ffn2_parts, ffn2_other))]

    dya, dys, dga, dgs = _mix_bwd_gate(dmixb, wout, ya, ys, rest, "mix_bwd_gate")
    (dwout,), _ = _mm_tn(merged, dmixb, _tile(D, 1024), _tile(D, 1024), _tile(S, 1024), False, "dw_out")
    do, dz, dvec = _att_sg_dout(dya, dys, watt, wsg, oatt, "att_sg_dout")
    (dwatt,), _ = _mm_tn(oatt, dya, GROUP_W, Db, _tile(S, 1024), True, "dw_att")
    (dwsg,), _ = _mm_tn(z, dys, SG_W, Db, _tile(S, 1024), True, "dw_sg")
    mix_parts = [dwout.reshape(N_DEV, D // N_DEV, D), dwatt, dwsg]
    du, dvs, d_sgw, d_sgbT, d_lng, d_lnb = _sg_bwd(rest, dz, sgw, sgbT, sg_ln_g, sg_ln_b, "sg_bwd")
    dqs, dks, dvs_att, ffn2_got = [], [], [], []
    for gi, d in enumerate(DILATIONS):
        ride = [_Ici([ffn2_sums[0]])] if gi == 2 else []
        (dq, dk, dv), got_here = _att_bwd(qkv, tabs, do, lse, dvec, gi, d, f"att_bwd{gi}", comm=ride)
        ffn2_got += [g[0] for g in got_here]
        dqs.append(dq)
        dks.append(dk)
        dvs_att.append(dv)
    dproj = jnp.concatenate([t.astype(BF) for t in dqs + dks + dvs_att] + [du, dvs, dga, dgs, jnp.zeros((S, p_pad - P), BF)], axis=1)
    (dx1, dyb1, d_mixn), (ffn2_rest, mix_other) = _dh_rms_bwd([(dproj, winT)], False, PROJ_TK, x1, mix_norm, dx2, 0.5, "proj_dh",
                                                            comm=[_Ici(ffn2_sums[1:]), _Swap(mix_parts)])
    ffn2_got += ffn2_rest
    mix_sums = [_pair_add(p, o, f"pair_mix_{i}") for i, (p, o) in enumerate(zip(mix_parts, mix_other))]
    (dwd1,), (mix_got,) = _ffn_dwd(a1, dyb1, "ffn1_dwd", comm=[_Ici(mix_sums)])
    rows = lambda a: a.reshape(-1, 128)
    pad8 = lambda a: _pad_rows(a, -(-a.shape[0] // 8) * 8)
    small = [("sg_w", rows(d_sgw), sg_w, m_sg_w, v_sg_w), ("mix_norm", rows(d_mixn), mix_norm, m_mix_norm, v_mix_norm),
             ("ffn2_norm", rows(d_ffn2n), ffn2_norm, m_ffn2_norm, v_ffn2_norm), ("final_norm", rows(d_final), final_norm, m_final_norm, v_final_norm),
             ("sg_ln_g", rows(d_lng), sg_ln_g, m_sg_ln_g, v_sg_ln_g), ("sg_ln_b", rows(d_lnb), sg_ln_b, m_sg_ln_b, v_sg_ln_b),
             ("sg_b", d_sgbT[:, :SG_GROUPS].T, sg_b, m_sg_b, v_sg_b)]
    gpack = jnp.concatenate([pad8(g) for _, g, _, _, _ in small] + [pad8(loss_part)], axis=0)
    (dwin,), ((wd1_other,), (gpacks,)) = _mm_tn(h2, dproj, D, 512, _tile(S, 2048), False, "dw_in", ncols=P,
                                              comm=[_Swap([dwd1]), _Spread([gpack])])
    dwin = dwin.reshape(D, N_DEV, Pb).transpose(1, 0, 2)
    wd1_sum = _pair_add(dwd1, wd1_other, "pair_wd1")
    (dg1, du1), ((wd1_got,), (win_other,)) = _ffn_bwd_act(dyb1, wd1, g1, u1, "ffn1_bwd_act", comm=[_Ici([wd1_sum]), _Swap([dwin])])
    win_sum = _pair_add(dwin, win_other, "pair_win")
    (dwg1, dwu1), ((win_got,),) = _ffn_dwgu(h1, dg1, du1, "ffn1_dwgu", comm=[_Ici([win_sum])])
    gu_parts = [dwg1, dwu1]
    gu_other = _comm_only(_Swap(gu_parts), "swap_gu1")
    gu_sums = [_pair_add(p, o, f"pair_gu1_{i}") for i, (p, o) in enumerate(zip(gu_parts, gu_other))]
    (dx0, _, d_ffn1n), (gu_got,) = _dh_rms_bwd([(dg1, wg1), (du1, wu1)], True, Fb, xs, ffn1_norm, dx1, 1.0, "ffn1_dh",
                                               comm=[_Ici(gu_sums)])

    got = dict(ffn2_w_down=ffn2_got[0], ffn2_w_gate=ffn2_got[1], ffn2_w_up=ffn2_got[2], w_out=mix_got[0], w_att_out=mix_got[1],
               w_sg_out=mix_got[2], w_in=win_got, ffn1_w_gate=gu_got[0], ffn1_w_up=gu_got[1], ffn1_w_down=wd1_got)
    moments = dict(ffn1_w_gate=(m_ffn1_w_gate, v_ffn1_w_gate), ffn1_w_up=(m_ffn1_w_up, v_ffn1_w_up),
                   ffn1_w_down=(m_ffn1_w_down, v_ffn1_w_down), w_in=(m_w_in, v_w_in), w_att_out=(m_w_att_out, v_w_att_out),
                   w_sg_out=(m_w_sg_out, v_w_sg_out), w_out=(m_w_out, v_w_out), ffn2_w_gate=(m_ffn2_w_gate, v_ffn2_w_gate),
                   ffn2_w_up=(m_ffn2_w_up, v_ffn2_w_up), ffn2_w_down=(m_ffn2_w_down, v_ffn2_w_down))
    res = {}
    for n in sharded:
        mm, vv = moments[n]
        outs = _adamw(got[n], sharded[n][0], mm[0], vv[0], "adamw_" + n)
        res[n] = [o[None] for o in outs]

    zero8 = jnp.zeros((8, 128), F32)
    wpack = jnp.concatenate([pad8(rows(w)) for _, _, w, _, _ in small] + [zero8], axis=0)
    mpack = jnp.concatenate([pad8(rows(m)) for _, _, _, m, _ in small] + [zero8], axis=0)
    vpack = jnp.concatenate([pad8(rows(v)) for _, _, _, _, v in small] + [zero8], axis=0)
    packs = _adamw(gpacks, wpack, mpack, vpack, "adamw_small")
    off = 0
    for n, g, w, _, _ in small:
        r = g.shape[0]
        res[n] = [p[off:off + r].reshape(w.shape) for p in packs]
        off += -(-r // 8) * 8
    loss = packs[0][off, 0]
    g_first = _small_allreduce(rows(d_ffn1n), "allreduce_ffn1_norm")
    res["ffn1_norm"] = [p.reshape(ffn1_norm.shape) for p in
                        _adamw(g_first[None], rows(ffn1_norm), rows(m_ffn1_norm), rows(v_ffn1_norm), "adamw_ffn1_norm")]

    order = ["ffn1_norm", "ffn1_w_gate", "ffn1_w_up", "ffn1_w_down", "mix_norm", "w_in", "sg_ln_g", "sg_ln_b", "sg_w", "sg_b",
             "w_att_out", "w_sg_out", "w_out", "ffn2_norm", "ffn2_w_gate", "ffn2_w_up", "ffn2_w_down", "final_norm"]
    return (loss, dx0[None], *[res[n][0] for n in order], *[res[n][1] for n in order], *[res[n][2] for n in order],
            *[res[n][3] for n in order])
```

```python
import math

import jax
import jax.numpy as jnp
from jax import lax
from jax.experimental import pallas as pl
from jax.experimental.pallas import tpu as pltpu

BF = jnp.bfloat16
F32 = jnp.float32
MESH = pl.DeviceIdType.MESH
N_DEV = 8
N_CHIP = 4

HEAD_DIM = 128
HEADS_PER_GROUP = 4
GROUP_W = HEADS_PER_GROUP * HEAD_DIM
DILATIONS = (1, 4, 16)
ATT_W = len(DILATIONS) * GROUP_W
SG_W = 1536
SG_GROUPS = 12
BLK = 128
ROPE_DIM = 32
ROPE_THETA = 500000.0
NORM_EPS = 1e-6
LN_EPS = 1e-5
Q_OFF, K_OFF, V_OFF, U_OFF, VS_OFF, GA_OFF = 0, ATT_W, 2 * ATT_W, 3 * ATT_W, 3 * ATT_W + SG_W, 3 * ATT_W + 2 * SG_W

ADAM_LR, ADAM_B1, ADAM_B2, ADAM_EPS, ADAM_WD, ADAM_STEP = 0.001, 0.9, 0.999, 1e-08, 0.01, 10

VMEM_LIMIT = 56 * 1024 * 1024
NEG = -1e30
ANY = pl.BlockSpec(memory_space=pl.ANY)
EPI_ROWS = 128
ACC_COLS = 512
FFN_PAIR = 2
PROJ_TK = 1536
R_U, R_VS, R_GA = 0, SG_W, 2 * SG_W


def _once(shape, index_map):
    return pl.BlockSpec(shape, index_map, pipeline_mode=pl.Buffered(1))


def _tile(n, pref):
    t = min(n, pref)
    while n % t:
        t //= 2
    return t


def _nt(a, b):
    return lax.dot_general(a, b, (((1,), (1,)), ((), ())), preferred_element_type=F32)


def _tn(a, b):
    return lax.dot_general(a, b, (((0,), (0,)), ((), ())), preferred_element_type=F32)


def _nn(a, b):
    return jnp.dot(a, b, preferred_element_type=F32)


def _acc_dots(acc_ref, terms, transposed_rhs=False):
    n = acc_ref.shape[1]
    width = min(n, ACC_COLS)
    for c0 in range(0, n, width):
        cols = slice(c0, c0 + width)
        tot = None
        for lhs, rhs in terms:
            part = _nt(lhs, rhs(cols)) if transposed_rhs else _nn(lhs, rhs(cols))
            tot = part if tot is None else tot + part
        acc_ref[:, cols] += tot


def _gelu(x):
    return 0.5 * x * (1.0 + lax.erf(x * (2.0 ** -0.5)))


def _gelu_grad(x):
    return 0.5 * (1.0 + lax.erf(x * (2.0 ** -0.5))) + x * jnp.exp(-0.5 * x * x) * (1.0 / math.sqrt(2.0 * math.pi))


def _place():
    x, y, c = lax.axis_index("x"), lax.axis_index("y"), lax.axis_index("c")
    return x, y, c


def _flip(v, bit):
    return 1 - v if bit else v


class _Gather:
    def __init__(self, shards, mid_frac=1.0, relay_frac=0.5):
        self.arrays = list(shards)
        self.relay_frac = relay_frac
        self.mid_frac = mid_frac
        nw = len(shards)
        self.out_shape = [jax.ShapeDtypeStruct((N_DEV,) + s.shape, s.dtype) for s in shards]
        self.scratch = [pltpu.SemaphoreType.DMA((nw, 7)), pltpu.SemaphoreType.DMA((nw, 7)), pltpu.SemaphoreType.DMA((nw,))]

    def _parts(self, ins, outs, sems):
        x, y, c = _place()
        send, recv, loc = sems
        south = c == 0
        near = (jnp.where(south, x, 1 - x), jnp.where(south, 1 - y, y), c)
        far = (jnp.where(south, 1 - x, x), jnp.where(south, y, 1 - y), c)
        diag = (1 - x, 1 - y, c)

        def copy(k, s, block, to, src=None):
            dst = outs[k].at[4 * block[0] + 2 * block[1] + block[2]]
            return pltpu.make_async_remote_copy(src_ref=dst if src is None else src, dst_ref=dst, send_sem=send.at[k, s],
                                                recv_sem=recv.at[k, s], device_id=to, device_id_type=MESH)

        def first(k):
            me = (x, y, c)
            return [copy(k, 0, me, (x, y, 1 - c), src=ins[k]), copy(k, 1, me, (1 - x, y, c), src=ins[k]),
                    copy(k, 2, me, (x, 1 - y, c), src=ins[k])]

        def local(k):
            return pltpu.make_async_copy(ins[k], outs[k].at[4 * x + 2 * y + c], loc.at[k])

        return x, y, c, near, far, diag, copy, first, local

    def start(self, ins, outs, sems):
        *_, first, local = self._parts(ins, outs, sems)
        for k in range(len(ins)):
            local(k).start()
            for cp in first(k):
                cp.start()

    def relay(self, ins, outs, sems):
        x, y, c, near, far, _, copy, _, _ = self._parts(ins, outs, sems)
        for k in range(len(ins)):
            copy(k, 2 - c, near, (x, y, c)).wait_recv()
            copy(k, 3, near, far).start()
            copy(k, 5 - c, near, (x, y, 1 - c)).start()

    def mid(self, ins, outs, sems):
        x, y, c, _, far, diag, copy, _, _ = self._parts(ins, outs, sems)
        for k in range(len(ins)):
            copy(k, 1 + c, far, (x, y, c)).wait_recv()
            copy(k, 4 + c, far, (x, y, 1 - c)).start()
            copy(k, 3, diag, (x, y, c)).wait_recv()
            copy(k, 6, diag, (x, y, 1 - c)).start()

    def finish(self, ins, outs, sems):
        x, y, c, near, _, _, copy, first, local = self._parts(ins, outs, sems)
        sib = (x, y, 1 - c)
        for k in range(len(ins)):
            copy(k, 0, sib, (x, y, c)).wait_recv()
            copy(k, 4, (1 - x, y, 1 - c), (x, y, c)).wait_recv()
            copy(k, 5, (x, 1 - y, 1 - c), (x, y, c)).wait_recv()
            copy(k, 6, (1 - x, 1 - y, 1 - c), (x, y, c)).wait_recv()
        for k in range(len(ins)):
            for cp in first(k):
                cp.wait_send()
            for s in (3, 4, 5, 6):
                copy(k, s, near, sib).wait_send()
            local(k).wait()


class _Swap:
    def __init__(self, parts):
        self.arrays = list(parts)
        nw = len(parts)
        self.out_shape = [jax.ShapeDtypeStruct((N_CHIP,) + p.shape[1:], p.dtype) for p in parts]
        self.scratch = [pltpu.SemaphoreType.DMA((nw, N_CHIP)), pltpu.SemaphoreType.DMA((nw, N_CHIP))]

    def _copy(self, ins, outs, sems, k, q):
        x, y, c = _place()
        return pltpu.make_async_remote_copy(src_ref=ins[k].at[2 * q + 1 - c], dst_ref=outs[k].at[q], send_sem=sems[0].at[k, q],
                                            recv_sem=sems[1].at[k, q], device_id=(x, y, 1 - c), device_id_type=MESH)

    mid_frac = None

    def start(self, ins, outs, sems):
        for k in range(len(ins)):
            for q in range(N_CHIP):
                self._copy(ins, outs, sems, k, q).start()

    def finish(self, ins, outs, sems):
        for k in range(len(ins)):
            for q in range(N_CHIP):
                self._copy(ins, outs, sems, k, q).wait()


class _Ici:
    mid_frac = None

    def __init__(self, sums):
        self.arrays = list(sums)
        nw = len(sums)
        self.out_shape = [jax.ShapeDtypeStruct(s.shape, s.dtype) for s in sums]
        self.scratch = [pltpu.SemaphoreType.DMA((nw, 3)), pltpu.SemaphoreType.DMA((nw, 3)), pltpu.SemaphoreType.DMA((nw,))]

    def _copies(self, ins, outs, sems, k):
        x, y, c = _place()
        myq = 2 * x + y
        out = []
        for r in range(1, N_CHIP):
            px, py = _flip(x, r & 2), _flip(y, r & 1)
            pq = 2 * px + py
            mk = lambda dst: pltpu.make_async_remote_copy(src_ref=ins[k].at[pq], dst_ref=dst, send_sem=sems[0].at[k, r - 1],
                                                          recv_sem=sems[1].at[k, r - 1], device_id=(px, py, c), device_id_type=MESH)
            out.append((mk(outs[k].at[myq]), mk(outs[k].at[pq])))
        return out, pltpu.make_async_copy(ins[k].at[myq], outs[k].at[myq], sems[2].at[k])

    def start(self, ins, outs, sems):
        for k in range(len(ins)):
            remote, local = self._copies(ins, outs, sems, k)
            local.start()
            for snd, _ in remote:
                snd.start()

    def finish(self, ins, outs, sems):
        for k in range(len(ins)):
            remote, local = self._copies(ins, outs, sems, k)
            for snd, rcv in remote:
                rcv.wait_recv()
                snd.wait_send()
            local.wait()


class _Spread:
    mid_frac = None

    def __init__(self, arrays):
        self.arrays = list(arrays)
        nw = len(arrays)
        self.out_shape = [jax.ShapeDtypeStruct((N_DEV,) + a.shape, a.dtype) for a in arrays]
        self.scratch = [pltpu.SemaphoreType.DMA((nw, 7)), pltpu.SemaphoreType.DMA((nw, 7)), pltpu.SemaphoreType.DMA((nw,))]

    def _copies(self, ins, outs, sems, k):
        x, y, c = _place()
        me = 4 * x + 2 * y + c
        out = []
        for r in range(1, N_DEV):
            px, py, pc = _flip(x, r & 4), _flip(y, r & 2), _flip(c, r & 1)
            peer = 4 * px + 2 * py + pc
            mk = lambda dst: pltpu.make_async_remote_copy(src_ref=ins[k], dst_ref=dst, send_sem=sems[0].at[k, r - 1],
                                                          recv_sem=sems[1].at[k, r - 1], device_id=(px, py, pc), device_id_type=MESH)
            out.append((mk(outs[k].at[me]), mk(outs[k].at[peer])))
        return out, pltpu.make_async_copy(ins[k], outs[k].at[me], sems[2].at[k])

    def start(self, ins, outs, sems):
        for k in range(len(ins)):
            remote, local = self._copies(ins, outs, sems, k)
            local.start()
            for snd, _ in remote:
                snd.start()

    def finish(self, ins, outs, sems):
        for k in range(len(ins)):
            remote, local = self._copies(ins, outs, sems, k)
            for snd, rcv in remote:
                rcv.wait_recv()
                snd.wait_send()
            local.wait()


def _call(body, *, grid, in_specs, out_specs, out_shape, name, args, scratch=(), comm=()):
    comm = list(comm)
    n_in, n_out, n_scr = len(in_specs), len(out_specs), len(scratch)
    total = math.prod(grid) if grid else 1

    def wrapped(*refs):
        p = n_in
        cin = []
        for cm in comm:
            cin.append(refs[p:p + len(cm.arrays)])
            p += len(cm.arrays)
        own_out = refs[p:p + n_out]
        p += n_out
        cout = []
        for cm in comm:
            cout.append(refs[p:p + len(cm.arrays)])
            p += len(cm.arrays)
        own_scr = refs[p:p + n_scr]
        p += n_scr
        csem = []
        for cm in comm:
            csem.append(refs[p:p + len(cm.scratch)])
            p += len(cm.scratch)
        step = 0
        for axis, g in enumerate(grid):
            step = step * g + pl.program_id(axis)

        def at(when, what):
            if total == 1:
                what()
            else:
                pl.when(step == when)(what)

        def starts():
            for cm, i, o, s in zip(comm, cin, cout, csem):
                cm.start(i, o, s)

        def finishes():
            for cm, i, o, s in zip(comm, cin, cout, csem):
                cm.finish(i, o, s)

        if comm:
            at(0, starts)
        if body is not None:
            body(*refs[:n_in], *own_out, *own_scr)
        for cm, i, o, s in zip(comm, cin, cout, csem):
            if cm.mid_frac is not None:
                at(min(total - 1, int(total * cm.relay_frac)), lambda cm=cm, i=i, o=o, s=s: cm.relay(i, o, s))
                at(min(total - 1, int(total * cm.mid_frac)), lambda cm=cm, i=i, o=o, s=s: cm.mid(i, o, s))
        if comm:
            at(total - 1, finishes)

    kw = dict(grid=tuple(grid)) if grid else {}
    outs = pl.pallas_call(
        wrapped, name=name, **kw,
        in_specs=list(in_specs) + [ANY for cm in comm for _ in cm.arrays],
        out_specs=list(out_specs) + [ANY for cm in comm for _ in cm.arrays],
        out_shape=list(out_shape) + [s for cm in comm for s in cm.out_shape],
        scratch_shapes=list(scratch) + [s for cm in comm for s in cm.scratch],
        compiler_params=pltpu.CompilerParams(dimension_semantics=("arbitrary",) * len(grid), vmem_limit_bytes=VMEM_LIMIT),
    )(*args, *[a for cm in comm for a in cm.arrays])
    own, p, per = list(outs[:n_out]), n_out, []
    for cm in comm:
        per.append(list(outs[p:p + len(cm.arrays)]))
        p += len(cm.arrays)
    return own, per


def _comm_only(cm, name):
    return _call(None, grid=(), in_specs=[], out_specs=[], out_shape=[], name=name, args=[], comm=[cm])[1][0]


def _rms_fwd(x, g, name):
    S, D = x.shape
    tm = _tile(S, 512)

    def body(x_ref, g_ref, o_ref):
        xv = x_ref[...]
        r = lax.rsqrt(jnp.mean(xv * xv, axis=-1, keepdims=True) + NORM_EPS)
        o_ref[...] = (xv * r * g_ref[...]).astype(BF)

    return _call(body, grid=(S // tm,), name=name, args=[x, g],
                 in_specs=[pl.BlockSpec((tm, D), lambda i: (i, 0)), pl.BlockSpec((1, D), lambda i: (0, 0))],
                 out_specs=[pl.BlockSpec((tm, D), lambda i: (i, 0))], out_shape=[jax.ShapeDtypeStruct((S, D), BF)])[0][0]


def _ffn_up(h, wg, wu, name, comm=()):
    S, D = h.shape
    nb, _, Fb = wg.shape
    tm = _tile(S, 512)

    def body(h_ref, wg_ref, wu_ref, g_ref, u_ref, a_ref):
        hv = h_ref[...]
        g = _nn(hv, wg_ref[0])
        u = _nn(hv, wu_ref[0])
        g_ref[0] = g.astype(BF)
        u_ref[0] = u.astype(BF)
        a_ref[0] = (g * jax.nn.sigmoid(g) * u).astype(BF)

    act = pl.BlockSpec((1, tm, Fb), lambda j, i: (j, i, 0))
    w = pl.BlockSpec((1, D, Fb), lambda j, i: (j, 0, 0))
    shp = jax.ShapeDtypeStruct((nb, S, Fb), BF)
    return _call(body, grid=(nb, S // tm), name=name, args=[h, wg, wu], comm=comm,
                 in_specs=[pl.BlockSpec((tm, D), lambda j, i: (i, 0)), w, w], out_specs=[act, act, act], out_shape=[shp, shp, shp])


def _ffn_gate(h, wg, name, comm=()):
    S, D = h.shape
    nb, _, Fb = wg.shape
    tm = _tile(S, 512)

    def body(h_ref, wg_ref, g_ref):
        g_ref[0] = _nn(h_ref[...], wg_ref[0]).astype(BF)

    act = pl.BlockSpec((1, tm, Fb), lambda j, i: (j, i, 0))
    return _call(body, grid=(nb, S // tm), name=name, args=[h, wg], comm=comm,
                 in_specs=[pl.BlockSpec((tm, D), lambda j, i: (i, 0)), pl.BlockSpec((1, D, Fb), lambda j, i: (j, 0, 0))],
                 out_specs=[act], out_shape=[jax.ShapeDtypeStruct((nb, S, Fb), BF)])


def _ffn_up_act(h, wu, g, name, comm=()):
    S, D = h.shape
    nb, _, Fb = wu.shape
    tm = _tile(S, 512)

    def body(h_ref, wu_ref, g_ref, u_ref, a_ref):
        u = _nn(h_ref[...], wu_ref[0])
        gv = g_ref[0].astype(F32)
        u_ref[0] = u.astype(BF)
        a_ref[0] = (gv * jax.nn.sigmoid(gv) * u).astype(BF)

    act = pl.BlockSpec((1, tm, Fb), lambda j, i: (j, i, 0))
    shp = jax.ShapeDtypeStruct((nb, S, Fb), BF)
    return _call(body, grid=(nb, S // tm), name=name, args=[h, wu, g], comm=comm,
                 in_specs=[pl.BlockSpec((tm, D), lambda j, i: (i, 0)), pl.BlockSpec((1, D, Fb), lambda j, i: (j, 0, 0)), act],
                 out_specs=[act, act], out_shape=[shp, shp])


def _ffn_down_norm(a, wd, x, gn, name, comm=()):
    nb, S, Fb = a.shape
    D = wd.shape[2]
    tm = _tile(S, 512)

    nj = nb // FFN_PAIR

    def body(a_ref, wd_ref, x_ref, gn_ref, xo_ref, hn_ref, acc_ref):
        j = pl.program_id(1)

        @pl.when(j == 0)
        def _():
            acc_ref[...] = jnp.zeros_like(acc_ref)

        _acc_dots(acc_ref, [(a_ref[b], lambda cols, b=b: wd_ref[b, :, cols]) for b in range(FFN_PAIR)])

        @pl.when(j == nj - 1)
        def _():
            def chunk(t, carry):
                rows = pl.ds(pl.multiple_of(t * EPI_ROWS, EPI_ROWS), EPI_ROWS)
                xo = x_ref[rows, :] + 0.5 * acc_ref[rows, :]
                r = lax.rsqrt(jnp.mean(xo * xo, axis=-1, keepdims=True) + NORM_EPS)
                xo_ref[rows, :] = xo
                hn_ref[rows, :] = (xo * r * gn_ref[...]).astype(BF)
                return carry

            lax.fori_loop(0, tm // EPI_ROWS, chunk, 0)

    row = pl.BlockSpec((tm, D), lambda i, j: (i, 0))
    return _call(body, grid=(S // tm, nj), name=name, args=[a, wd, x, gn], comm=comm,
                 in_specs=[pl.BlockSpec((FFN_PAIR, tm, Fb), lambda i, j: (j, i, 0)), pl.BlockSpec((FFN_PAIR, Fb, D), lambda i, j: (j, 0, 0)),
                           _once((tm, D), lambda i, j: (i, 0)), pl.BlockSpec((1, D), lambda i, j: (0, 0))],
                 out_specs=[row, row], out_shape=[jax.ShapeDtypeStruct((S, D), F32), jax.ShapeDtypeStruct((S, D), BF)],
                 scratch=[pltpu.VMEM((tm, D), F32)])


def _ffn_down_loss(a, wd, x, gf, tgt, name):
    nb, S, Fb = a.shape
    D = wd.shape[2]
    tm = _tile(S, 512)

    nj = nb // FFN_PAIR

    def body(a_ref, wd_ref, x_ref, gf_ref, t_ref, dx_ref, dxb_ref, dgf_ref, loss_ref, acc_ref):
        i, j = pl.program_id(0), pl.program_id(1)

        @pl.when(j == 0)
        def _():
            acc_ref[...] = jnp.zeros_like(acc_ref)

        _acc_dots(acc_ref, [(a_ref[b], lambda cols, b=b: wd_ref[b, :, cols]) for b in range(FFN_PAIR)])

        @pl.when((j == nj - 1) & (i == 0))
        def _():
            dgf_ref[...] = jnp.zeros_like(dgf_ref)
            loss_ref[...] = jnp.zeros_like(loss_ref)

        @pl.when(j == nj - 1)
        def _():
            def chunk(t, carry):
                rows = pl.ds(pl.multiple_of(t * EPI_ROWS, EPI_ROWS), EPI_ROWS)
                xo = x_ref[rows, :] + 0.5 * acc_ref[rows, :]
                r = lax.rsqrt(jnp.mean(xo * xo, axis=-1, keepdims=True) + NORM_EPS)
                xh = xo * r
                gf = gf_ref[...]
                e = xh * gf - t_ref[rows, :]
                loss_ref[...] += jnp.sum(jnp.mean(e * e, axis=-1, keepdims=True), axis=0, keepdims=True) * 0.5
                dy = e * (1.0 / D)
                dgf_ref[...] += jnp.sum(dy * xh, axis=0, keepdims=True)
                dxh = dy * gf
                dx = r * (dxh - xh * jnp.mean(dxh * xh, axis=-1, keepdims=True))
                dx_ref[rows, :] = dx
                dxb_ref[rows, :] = (0.5 * dx).astype(BF)
                return carry

            lax.fori_loop(0, tm // EPI_ROWS, chunk, 0)

    row = pl.BlockSpec((tm, D), lambda i, j: (i, 0))
    once = _once((tm, D), lambda i, j: (i, 0))
    vec = pl.BlockSpec((1, D), lambda i, j: (0, 0))
    return _call(body, grid=(S // tm, nj), name=name, args=[a, wd, x, gf, tgt],
                 in_specs=[pl.BlockSpec((FFN_PAIR, tm, Fb), lambda i, j: (j, i, 0)), pl.BlockSpec((FFN_PAIR, Fb, D), lambda i, j: (j, 0, 0)),
                           once, vec, once],
                 out_specs=[row, row, vec, pl.BlockSpec((1, 128), lambda i, j: (0, 0))],
                 out_shape=[jax.ShapeDtypeStruct((S, D), F32), jax.ShapeDtypeStruct((S, D), BF), jax.ShapeDtypeStruct((1, D), F32),
                            jax.ShapeDtypeStruct((1, 128), F32)],
                 scratch=[pltpu.VMEM((tm, D), F32)])[0]


def _ffn_bwd_act(dyb, wd, g, u, name, comm=()):
    S, D = dyb.shape
    nb, Fb, _ = wd.shape
    tm = _tile(S, 512)

    def body(dy_ref, wd_ref, g_ref, u_ref, dg_ref, du_ref):
        da = _nt(dy_ref[...], wd_ref[0])
        gv = g_ref[0].astype(F32)
        uv = u_ref[0].astype(F32)
        sg = jax.nn.sigmoid(gv)
        du_ref[0] = (da * gv * sg).astype(BF)
        dg_ref[0] = (da * uv * sg * (1.0 + gv * (1.0 - sg))).astype(BF)

    act = pl.BlockSpec((1, tm, Fb), lambda j, i: (j, i, 0))
    shp = jax.ShapeDtypeStruct((nb, S, Fb), BF)
    return _call(body, grid=(nb, S // tm), name=name, args=[dyb, wd, g, u], comm=comm,
                 in_specs=[pl.BlockSpec((tm, D), lambda j, i: (i, 0)), pl.BlockSpec((1, Fb, D), lambda j, i: (j, 0, 0)), act, act],
                 out_specs=[act, act], out_shape=[shp, shp])


def _ffn_dwd(a, dyb, name, comm=()):
    nb, S, Fb = a.shape
    D = dyb.shape[1]
    ts = _tile(S, 512)
    ns = S // ts

    def body(a_ref, dy_ref, o_ref, acc_ref):
        s = pl.program_id(1)

        @pl.when(s == 0)
        def _():
            acc_ref[...] = jnp.zeros_like(acc_ref)

        acc_ref[...] += _tn(a_ref[0], dy_ref[...])

        @pl.when(s == ns - 1)
        def _():
            o_ref[0] = acc_ref[...].astype(BF)

    return _call(body, grid=(nb, ns), name=name, args=[a, dyb], comm=comm,
                 in_specs=[pl.BlockSpec((1, ts, Fb), lambda j, s: (j, s, 0)), pl.BlockSpec((ts, D), lambda j, s: (s, 0))],
                 out_specs=[pl.BlockSpec((1, Fb, D), lambda j, s: (j, 0, 0))], out_shape=[jax.ShapeDtypeStruct((nb, Fb, D), BF)],
                 scratch=[pltpu.VMEM((Fb, D), F32)])


def _ffn_dwgu(h, dg, du, name, comm=()):
    S, D = h.shape
    nb, _, Fb = dg.shape
    ts = _tile(S, 512)
    ns = S // ts

    def body(h_ref, dg_ref, du_ref, og_ref, ou_ref, accg_ref, accu_ref):
        s = pl.program_id(1)

        @pl.when(s == 0)
        def _():
            accg_ref[...] = jnp.zeros_like(accg_ref)
            accu_ref[...] = jnp.zeros_like(accu_ref)

        hv = h_ref[...]
        accg_ref[...] += _tn(hv, dg_ref[0])
        accu_ref[...] += _tn(hv, du_ref[0])

        @pl.when(s == ns - 1)
        def _():
            og_ref[0] = accg_ref[...].astype(BF)
            ou_ref[0] = accu_ref[...].astype(BF)

    act = pl.BlockSpec((1, ts, Fb), lambda j, s: (j, s, 0))
    out = pl.BlockSpec((1, D, Fb), lambda j, s: (j, 0, 0))
    shp = jax.ShapeDtypeStruct((nb, D, Fb), BF)
    return _call(body, grid=(nb, ns), name=name, args=[h, dg, du], comm=comm,
                 in_specs=[pl.BlockSpec((ts, D), lambda j, s: (s, 0)), act, act], out_specs=[out, out], out_shape=[shp, shp],
                 scratch=[pltpu.VMEM((D, Fb), F32), pltpu.VMEM((D, Fb), F32)])


def _dh_rms_bwd(pairs, blocked, tk, x, gn, dxo, out_scale, name, comm=()):
    S, D = x.shape
    nk = pairs[0][0].shape[0] if blocked else pairs[0][0].shape[1] // tk
    tm = _tile(S, 512)
    npair = len(pairs)

    def body(*refs):
        ins = refs[: 2 * npair]
        x_ref, gn_ref, dxo_ref, dx_ref, dxb_ref, dgn_ref, acc_ref = refs[2 * npair:]
        i, k = pl.program_id(0), pl.program_id(1)

        @pl.when(k == 0)
        def _():
            acc_ref[...] = jnp.zeros_like(acc_ref)

        if blocked:
            terms = [(ins[2 * p][0], lambda cols, r=ins[2 * p + 1]: r[0, cols, :]) for p in range(npair)]
        else:
            terms = [(ins[2 * p][...], lambda cols, r=ins[2 * p + 1]: r[:, cols]) for p in range(npair)]
        _acc_dots(acc_ref, terms, transposed_rhs=blocked)

        @pl.when((k == nk - 1) & (i == 0))
        def _():
            dgn_ref[...] = jnp.zeros_like(dgn_ref)

        @pl.when(k == nk - 1)
        def _():
            def chunk(t, carry):
                rows = pl.ds(pl.multiple_of(t * EPI_ROWS, EPI_ROWS), EPI_ROWS)
                xv = x_ref[rows, :]
                r = lax.rsqrt(jnp.mean(xv * xv, axis=-1, keepdims=True) + NORM_EPS)
                xh = xv * r
                dh = acc_ref[rows, :]
                dgn_ref[...] += jnp.sum(dh * xh, axis=0, keepdims=True)
                dxh = dh * gn_ref[...]
                dx = dxo_ref[rows, :] + r * (dxh - xh * jnp.mean(dxh * xh, axis=-1, keepdims=True))
                dx_ref[rows, :] = dx
                dxb_ref[rows, :] = (out_scale * dx).astype(BF)
                return carry

            lax.fori_loop(0, tm // EPI_ROWS, chunk, 0)

    if blocked:
        lspec = pl.BlockSpec((1, tm, tk), lambda i, k: (k, i, 0))
        rspec = pl.BlockSpec((1, D, tk), lambda i, k: (k, 0, 0))
    else:
        lspec = pl.BlockSpec((tm, tk), lambda i, k: (i, k))
        rspec = pl.BlockSpec((tk, D), lambda i, k: (k, 0))
    row = pl.BlockSpec((tm, D), lambda i, k: (i, 0))
    once = _once((tm, D), lambda i, k: (i, 0))
    vec = pl.BlockSpec((1, D), lambda i, k: (0, 0))
    flat = [t for pr in pairs for t in pr]
    return _call(body, grid=(S // tm, nk), name=name, args=[*flat, x, gn, dxo], comm=comm,
                 in_specs=[lspec, rspec] * npair + [once, vec, once], out_specs=[row, row, vec],
                 out_shape=[jax.ShapeDtypeStruct((S, D), F32), jax.ShapeDtypeStruct((S, D), BF), jax.ShapeDtypeStruct((1, D), F32)],
                 scratch=[pltpu.VMEM((tm, D), F32)])


def _mm_nn(a, b, tm, tn, col0, col1, dtype, name, comm=()):
    M, K = a.shape
    n0, nn = col0 // tn, (col1 - col0) // tn

    def body(a_ref, b_ref, o_ref):
        o_ref[...] = _nn(a_ref[...], b_ref[...]).astype(dtype)

    return _call(body, grid=(nn, M // tm), name=name, args=[a, b], comm=comm,
                 in_specs=[pl.BlockSpec((tm, K), lambda n, i: (i, 0)), pl.BlockSpec((K, tn), lambda n, i: (0, n0 + n))],
                 out_specs=[pl.BlockSpec((tm, tn), lambda n, i: (i, n))], out_shape=[jax.ShapeDtypeStruct((M, nn * tn), dtype)])


def _mm_tn(a, b, tm, tn, ts, blocked, name, comm=(), ncols=None):
    S, M = a.shape
    N = b.shape[1] if ncols is None else ncols
    ns = S // ts
    per_tile = tn // blocked if blocked else 0

    def body(a_ref, b_ref, o_ref, acc_ref):
        s = pl.program_id(2)

        @pl.when(s == 0)
        def _():
            acc_ref[...] = jnp.zeros_like(acc_ref)

        acc_ref[...] += _tn(a_ref[...], b_ref[...])

        @pl.when(s == ns - 1)
        def _():
            if blocked:
                for t in range(per_tile):
                    o_ref[t] = acc_ref[:, t * blocked:(t + 1) * blocked].astype(BF)
            else:
                o_ref[...] = acc_ref[...].astype(BF)

    if blocked:
        ospec = pl.BlockSpec((per_tile, tm, blocked), lambda i, n, s: (n, i, 0))
        oshape = jax.ShapeDtypeStruct((N // blocked, M, blocked), BF)
    else:
        ospec = pl.BlockSpec((tm, tn), lambda i, n, s: (i, n))
        oshape = jax.ShapeDtypeStruct((M, N), BF)
    return _call(body, grid=(M // tm, N // tn, ns), name=name, args=[a, b], comm=comm,
                 in_specs=[pl.BlockSpec((ts, tm), lambda i, n, s: (s, i)), pl.BlockSpec((ts, tn), lambda i, n, s: (s, n))],
                 out_specs=[ospec], out_shape=[oshape], scratch=[pltpu.VMEM((tm, tn), F32)])


def _rope_tables(S):
    half = ROPE_DIM // 2
    inv_freq = ROPE_THETA ** (-jnp.arange(0, ROPE_DIM, 2, dtype=F32) / ROPE_DIM)
    ang = jnp.arange(S, dtype=F32)[:, None] * inv_freq[None, :]
    cos, sin = jnp.cos(ang), jnp.sin(ang)
    zeros = jnp.zeros((S, HEAD_DIM - ROPE_DIM), F32)
    c = jnp.concatenate([cos, cos, jnp.ones((S, HEAD_DIM - ROPE_DIM), F32)], axis=1)
    sm = jnp.concatenate([-sin, jnp.zeros((S, half), F32), zeros], axis=1)
    sp = jnp.concatenate([jnp.zeros((S, half), F32), sin, zeros], axis=1)
    return c, sm, sp


def _rope(t, c, sm, sp):
    return t * c + pltpu.roll(t, HEAD_DIM - ROPE_DIM // 2, 1) * sm + pltpu.roll(t, ROPE_DIM // 2, 1) * sp


def _rope_t(dy, c, sm, sp):
    return dy * c + pltpu.roll(dy * sm, ROPE_DIM // 2, 1) + pltpu.roll(dy * sp, HEAD_DIM - ROPE_DIM // 2, 1)


def _att_mask(i):
    qi = lax.broadcasted_iota(jnp.int32, (BLK, 2 * BLK), 0)
    kj = lax.broadcasted_iota(jnp.int32, (BLK, 2 * BLK), 1)
    diff = qi + BLK - kj
    first_key = jnp.where(i > 0, 0, BLK)
    return (diff >= 0) & (diff <= BLK) & (kj >= first_key)


def _res_rows(r, i, n, d):
    if d == 1:
        return pl.ds(pl.multiple_of(i * n, n), n)
    return pl.ds(r + i * (n * d), n, stride=d)


def _att_specs(S, gi):
    def sect(off):
        base = (off + gi * GROUP_W) // HEAD_DIM
        return _once((S, HEAD_DIM), lambda hh, r: (0, base + hh))

    tab = pl.BlockSpec((S, HEAD_DIM), lambda hh, r: (0, 0))
    head = pl.BlockSpec((S, HEAD_DIM), lambda hh, r: (0, hh))
    return sect, tab, head


def _att_fwd(qkv, tabs, gi, d, name, comm=()):
    S = qkv.shape[0]
    L = S // d
    sect, tab, head = _att_specs(S, gi)
    nblk = L // BLK
    scale = HEAD_DIM ** -0.5

    def body(q_ref, k_ref, v_ref, c_ref, sm_ref, sp_ref, o_ref, lse_ref, qr, kp, vp):
        r = pl.program_id(1)
        res = _res_rows(r, 0, L, d)
        c, sm, sp = c_ref[res, :], sm_ref[res, :], sp_ref[res, :]
        qr[...] = _rope(q_ref[res, :], c, sm, sp).astype(BF)
        kp[pl.ds(0, BLK), :] = jnp.zeros((BLK, HEAD_DIM), BF)
        vp[pl.ds(0, BLK), :] = jnp.zeros((BLK, HEAD_DIM), BF)
        kp[pl.ds(BLK, L), :] = _rope(k_ref[res, :], c, sm, sp).astype(BF)
        vp[pl.ds(BLK, L), :] = v_ref[res, :].astype(BF)

        def blk(i, carry):
            r0 = pl.multiple_of(i * BLK, BLK)
            s = _nt(qr[pl.ds(r0, BLK), :], kp[pl.ds(r0, 2 * BLK), :]) * scale
            s = jnp.where(_att_mask(i), s, NEG)
            m = jnp.max(s, axis=-1, keepdims=True)
            p = jnp.exp(s - m)
            l = jnp.sum(p, axis=-1, keepdims=True)
            out = _res_rows(r, i, BLK, d)
            o_ref[out, :] = _nn(p.astype(BF), vp[pl.ds(r0, 2 * BLK), :]) / l
            lse_ref[out, :] = jnp.broadcast_to(m + jnp.log(l), (BLK, HEAD_DIM))
            return carry

        lax.fori_loop(0, nblk, blk, 0, unroll=min(4, nblk))

    shp = jax.ShapeDtypeStruct((S, GROUP_W), F32)
    return _call(body, grid=(HEADS_PER_GROUP, d), name=name, args=[qkv, qkv, qkv, *tabs], comm=comm,
                 in_specs=[sect(Q_OFF), sect(K_OFF), sect(V_OFF), tab, tab, tab], out_specs=[head, head], out_shape=[shp, shp],
                 scratch=[pltpu.VMEM((L, HEAD_DIM), BF), pltpu.VMEM((L + BLK, HEAD_DIM), BF), pltpu.VMEM((L + BLK, HEAD_DIM), BF)])


def _att_combine(os, lses, name):
    S = os[0].shape[0]
    tm = _tile(S, 512)

    def body(o0, o1, o2, l0, l1, l2, oa_ref, lse_ref):
        a, b, c = l0[...], l1[...], l2[...]
        mx = jnp.maximum(jnp.maximum(a, b), c)
        wa, wb, wc = jnp.exp(a - mx), jnp.exp(b - mx), jnp.exp(c - mx)
        den = wa + wb + wc
        oa_ref[...] = ((wa * o0[...] + wb * o1[...] + wc * o2[...]) / den).astype(BF)
        lse_ref[...] = mx + jnp.log(den)

    row = pl.BlockSpec((tm, GROUP_W), lambda i: (i, 0))
    return _call(body, grid=(S // tm,), name=name, args=[*os, *lses], in_specs=[row] * 6, out_specs=[row, row],
                 out_shape=[jax.ShapeDtypeStruct((S, GROUP_W), BF), jax.ShapeDtypeStruct((S, GROUP_W), F32)])[0]


def _att_bwd(qkv, tabs, do, lse, dvec, gi, d, name, comm=()):
    S = qkv.shape[0]
    L = S // d
    sect, tab, head = _att_specs(S, gi)
    stat = _once((S, HEAD_DIM), lambda hh, r: (0, hh))
    nblk = L // BLK
    scale = HEAD_DIM ** -0.5

    def body(q_ref, k_ref, v_ref, c_ref, sm_ref, sp_ref, do_ref, lse_ref, dv_ref, dq_out, dk_out, dv_out, qr, kp, vp, dkp, dvp):
        r = pl.program_id(1)
        res = _res_rows(r, 0, L, d)
        c, sm, sp = c_ref[res, :], sm_ref[res, :], sp_ref[res, :]
        qr[...] = _rope(q_ref[res, :], c, sm, sp).astype(BF)
        kp[pl.ds(0, BLK), :] = jnp.zeros((BLK, HEAD_DIM), BF)
        vp[pl.ds(0, BLK), :] = jnp.zeros((BLK, HEAD_DIM), BF)
        kp[pl.ds(BLK, L), :] = _rope(k_ref[res, :], c, sm, sp).astype(BF)
        vp[pl.ds(BLK, L), :] = v_ref[res, :].astype(BF)
        dkp[...] = jnp.zeros_like(dkp)
        dvp[...] = jnp.zeros_like(dvp)

        def blk(i, carry):
            r0 = pl.multiple_of(i * BLK, BLK)
            rows, win, pos = pl.ds(r0, BLK), pl.ds(r0, 2 * BLK), _res_rows(r, i, BLK, d)
            q, kw, vw, dob = qr[rows, :], kp[win, :], vp[win, :], do_ref[pos, :].astype(BF)
            s = jnp.where(_att_mask(i), _nt(q, kw) * scale, NEG)
            p = jnp.exp(s - lse_ref[pos, :][:, :1])
            ds = p * (_nt(dob, vw) - dv_ref[pos, :][:, :1]) * scale
            dsb = ds.astype(BF)
            dq_out[pos, :] = _rope_t(_nn(dsb, kw), c_ref[pos, :], sm_ref[pos, :], sp_ref[pos, :])
            dkp[win, :] += _tn(dsb, q)
            dvp[win, :] += _tn(p.astype(BF), dob)
            return carry

        lax.fori_loop(0, nblk, blk, 0, unroll=2)
        dk_out[res, :] = _rope_t(dkp[pl.ds(BLK, L), :], c, sm, sp)
        dv_out[res, :] = dvp[pl.ds(BLK, L), :]

    shp = jax.ShapeDtypeStruct((S, GROUP_W), F32)
    return _call(body, grid=(HEADS_PER_GROUP, d), name=name, args=[qkv, qkv, qkv, *tabs, do, lse, dvec], comm=comm,
                 in_specs=[sect(Q_OFF), sect(K_OFF), sect(V_OFF), tab, tab, tab, stat, stat, stat],
                 out_specs=[head, head, head], out_shape=[shp, shp, shp],
                 scratch=[pltpu.VMEM((L, HEAD_DIM), BF), pltpu.VMEM((L + BLK, HEAD_DIM), BF), pltpu.VMEM((L + BLK, HEAD_DIM), BF),
                          pltpu.VMEM((L + BLK, HEAD_DIM), F32), pltpu.VMEM((L + BLK, HEAD_DIM), F32)])


def _sg_parts(u_ref, vs_ref, g_ref, b_ref):
    uv = u_ref[...].astype(F32)
    vv = vs_ref[...].astype(F32)
    vg = _gelu(vv)
    mu = jnp.mean(vg, axis=-1, keepdims=True)
    vc = vg - mu
    rs = lax.rsqrt(jnp.mean(vc * vc, axis=-1, keepdims=True) + LN_EPS)
    y = vc * rs
    return uv, vv, rs, y, y * g_ref[...] + b_ref[...]


def _sg_wmask():
    t = lax.broadcasted_iota(jnp.int32, (BLK, BLK), 0)
    s = lax.broadcasted_iota(jnp.int32, (BLK, BLK), 1)
    return s <= t


def _sg_fwd(proj, sgw, sgbT, lng, lnb, name):
    S, P = proj.shape

    def body(u_ref, vs_ref, w_ref, bt_ref, g_ref, b_ref, z_ref):
        uv, _, _, _, vln = _sg_parts(u_ref, vs_ref, g_ref, b_ref)
        ug = _gelu(uv)
        vb = vln.astype(BF)
        mask = _sg_wmask()
        bt = bt_ref[...]
        for g in range(SG_GROUPS):
            cols = slice(g * BLK, (g + 1) * BLK)
            w = jnp.where(mask, w_ref[g], 0.0).astype(BF)
            sp = _nn(w, vb[:, cols]) + bt[:, g:g + 1]
            z_ref[:, cols] = (ug[:, cols] * sp).astype(BF)

    tile = lambda off: pl.BlockSpec((BLK, SG_W), lambda i: (i, off // SG_W))
    full = lambda shape: pl.BlockSpec(shape, lambda i: (0,) * len(shape))
    return _call(body, grid=(S // BLK,), name=name, args=[proj, proj, sgw, sgbT, lng, lnb],
                 in_specs=[tile(R_U), tile(R_VS), full((SG_GROUPS, BLK, BLK)), full((BLK, BLK)), full((1, SG_W)), full((1, SG_W))],
                 out_specs=[pl.BlockSpec((BLK, SG_W), lambda i: (i, 0))], out_shape=[jax.ShapeDtypeStruct((S, SG_W), BF)])[0][0]


def _sg_bwd(proj, dz, sgw, sgbT, lng, lnb, name):
    S, P = proj.shape

    def body(u_ref, vs_ref, dz_ref, w_ref, bt_ref, g_ref, b_ref, du_ref, dvs_ref, dw_ref, dbt_ref, dg_ref, db_ref, dvln):
        @pl.when(pl.program_id(0) == 0)
        def _():
            dw_ref[...] = jnp.zeros_like(dw_ref)
            dbt_ref[...] = jnp.zeros_like(dbt_ref)
            dg_ref[...] = jnp.zeros_like(dg_ref)
            db_ref[...] = jnp.zeros_like(db_ref)

        uv, vv, rs, y, vln = _sg_parts(u_ref, vs_ref, g_ref, b_ref)
        ug = _gelu(uv)
        vb = vln.astype(BF)
        dzv = dz_ref[...].astype(F32)
        dsp = dzv * ug
        dspb = dsp.astype(BF)
        mask = _sg_wmask()
        bt = bt_ref[...]
        lane = lax.broadcasted_iota(jnp.int32, (BLK, BLK), 1)
        dbt = jnp.zeros((BLK, BLK), F32)
        for g in range(SG_GROUPS):
            cols = slice(g * BLK, (g + 1) * BLK)
            w = jnp.where(mask, w_ref[g], 0.0).astype(BF)
            sp = _nn(w, vb[:, cols]) + bt[:, g:g + 1]
            du_ref[:, cols] = (dzv[:, cols] * sp * _gelu_grad(uv[:, cols])).astype(BF)
            dw_ref[g] += jnp.where(mask, _nt(dspb[:, cols], vb[:, cols]), 0.0)
            dbt = dbt + jnp.where(lane == g, jnp.sum(dsp[:, cols], axis=-1, keepdims=True), 0.0)
            dvln[:, cols] = _tn(w, dspb[:, cols])
        dbt_ref[...] += dbt
        dvl = dvln[...]
        dg_ref[...] += jnp.sum(dvl * y, axis=0, keepdims=True)
        db_ref[...] += jnp.sum(dvl, axis=0, keepdims=True)
        dy = dvl * g_ref[...]
        dvg = rs * (dy - jnp.mean(dy, axis=-1, keepdims=True) - y * jnp.mean(dy * y, axis=-1, keepdims=True))
        dvs_ref[...] = (dvg * _gelu_grad(vv)).astype(BF)

    tile = lambda off: pl.BlockSpec((BLK, SG_W), lambda i: (i, off // SG_W))
    full = lambda shape: pl.BlockSpec(shape, lambda i: (0,) * len(shape))
    row = pl.BlockSpec((BLK, SG_W), lambda i: (i, 0))
    return _call(body, grid=(S // BLK,), name=name, args=[proj, proj, dz, sgw, sgbT, lng, lnb],
                 in_specs=[tile(R_U), tile(R_VS), row, full((SG_GROUPS, BLK, BLK)), full((BLK, BLK)), full((1, SG_W)), full((1, SG_W))],
                 out_specs=[row, row, full((SG_GROUPS, BLK, BLK)), full((BLK, BLK)), full((1, SG_W)), full((1, SG_W))],
                 out_shape=[jax.ShapeDtypeStruct((S, SG_W), BF), jax.ShapeDtypeStruct((S, SG_W), BF),
                            jax.ShapeDtypeStruct((SG_GROUPS, BLK, BLK), F32), jax.ShapeDtypeStruct((BLK, BLK), F32),
                            jax.ShapeDtypeStruct((1, SG_W), F32), jax.ShapeDtypeStruct((1, SG_W), F32)],
                 scratch=[pltpu.VMEM((BLK, SG_W), F32)])[0]


def _gate_merge(oatt, z, watt, wsg, proj, name, comm=()):
    S = oatt.shape[0]
    nb, _, Db = watt.shape
    D = nb * Db
    tm = _tile(S, 512)
    ga, gs = R_GA // Db, (R_GA + D) // Db

    def body(oa_ref, z_ref, wa_ref, ws_ref, ga_ref, gs_ref, ya_ref, ys_ref, mg_ref):
        ya = _nn(oa_ref[...], wa_ref[0])
        ys = _nn(z_ref[...], ws_ref[0])
        ya_ref[...] = ya.astype(BF)
        ys_ref[...] = ys.astype(BF)
        mg_ref[...] = (jax.nn.sigmoid(ga_ref[...].astype(F32)) * ya + jax.nn.sigmoid(gs_ref[...].astype(F32)) * ys).astype(BF)

    out = pl.BlockSpec((tm, Db), lambda i, j: (i, j))
    shp = jax.ShapeDtypeStruct((S, D), BF)
    return _call(body, grid=(S // tm, nb), name=name, args=[oatt, z, watt, wsg, proj, proj], comm=comm,
                 in_specs=[pl.BlockSpec((tm, GROUP_W), lambda i, j: (i, 0)), pl.BlockSpec((tm, SG_W), lambda i, j: (i, 0)),
                           pl.BlockSpec((1, GROUP_W, Db), lambda i, j: (j, 0, 0)), pl.BlockSpec((1, SG_W, Db), lambda i, j: (j, 0, 0)),
                           pl.BlockSpec((tm, Db), lambda i, j: (i, ga + j)), pl.BlockSpec((tm, Db), lambda i, j: (i, gs + j))],
                 out_specs=[out, out, out], out_shape=[shp, shp, shp])


def _mix_out(merged, wout, x, gn, name):
    S, D = x.shape
    tm = _tile(S, 256)

    def body(m_ref, w_ref, x_ref, gn_ref, xo_ref, hn_ref):
        xo = x_ref[...] + _nn(m_ref[...], w_ref[...])
        r = lax.rsqrt(jnp.mean(xo * xo, axis=-1, keepdims=True) + NORM_EPS)
        xo_ref[...] = xo
        hn_ref[...] = (xo * r * gn_ref[...]).astype(BF)

    row = pl.BlockSpec((tm, D), lambda i: (i, 0))
    return _call(body, grid=(S // tm,), name=name, args=[merged, wout, x, gn],
                 in_specs=[row, pl.BlockSpec((D, D), lambda i: (0, 0)), row, pl.BlockSpec((1, D), lambda i: (0, 0))],
                 out_specs=[row, row], out_shape=[jax.ShapeDtypeStruct((S, D), F32), jax.ShapeDtypeStruct((S, D), BF)])[0]


def _mix_bwd_gate(dmix, wout, ya, ys, proj, name):
    S, D = dmix.shape
    tm, tn = _tile(S, 512), 512
    ga, gs = R_GA // tn, (R_GA + D) // tn

    def body(dm_ref, w_ref, ya_ref, ys_ref, ga_ref, gs_ref, dya_ref, dys_ref, dga_ref, dgs_ref):
        w_rows = pl.ds(pl.multiple_of(pl.program_id(1) * tn, tn), tn)
        dm = _nt(dm_ref[...], w_ref[w_rows, :])
        sa = jax.nn.sigmoid(ga_ref[...].astype(F32))
        ss = jax.nn.sigmoid(gs_ref[...].astype(F32))
        dya_ref[...] = (dm * sa).astype(BF)
        dys_ref[...] = (dm * ss).astype(BF)
        dga_ref[...] = (dm * ya_ref[...].astype(F32) * sa * (1.0 - sa)).astype(BF)
        dgs_ref[...] = (dm * ys_ref[...].astype(F32) * ss * (1.0 - ss)).astype(BF)

    out = pl.BlockSpec((tm, tn), lambda i, n: (i, n))
    shp = jax.ShapeDtypeStruct((S, D), BF)
    return _call(body, grid=(S // tm, D // tn), name=name, args=[dmix, wout, ya, ys, proj, proj],
                 in_specs=[pl.BlockSpec((tm, D), lambda i, n: (i, 0)), pl.BlockSpec((D, D), lambda i, n: (0, 0)), out, out,
                           pl.BlockSpec((tm, tn), lambda i, n: (i, ga + n)), pl.BlockSpec((tm, tn), lambda i, n: (i, gs + n))],
                 out_specs=[out] * 4, out_shape=[shp] * 4)[0]


def _att_sg_dout(dya, dys, watt, wsg, oatt, name, comm=()):
    S, D = dya.shape
    nb, _, Db = watt.shape
    tm = _tile(S, 512)

    def body(dya_ref, dys_ref, wa_ref, ws_ref, oa_ref, do_ref, dz_ref, dvec_ref):
        def back(dy_ref, w_ref, rows):
            tot = None
            for j in range(nb):
                part = _nt(dy_ref[:, j * Db:(j + 1) * Db], w_ref[j, rows, :])
                tot = part if tot is None else tot + part
            return tot

        dov = back(dya_ref, wa_ref, slice(0, GROUP_W))
        do_ref[...] = dov
        for c0 in range(0, SG_W, GROUP_W):
            dz_ref[:, c0:c0 + GROUP_W] = back(dys_ref, ws_ref, slice(c0, c0 + GROUP_W)).astype(BF)
        prod = dov * oa_ref[...].astype(F32)
        for hh in range(HEADS_PER_GROUP):
            cols = slice(hh * HEAD_DIM, (hh + 1) * HEAD_DIM)
            dvec_ref[:, cols] = jnp.broadcast_to(jnp.sum(prod[:, cols], axis=-1, keepdims=True), (tm, HEAD_DIM))

    row = pl.BlockSpec((tm, D), lambda i: (i, 0))
    att = pl.BlockSpec((tm, GROUP_W), lambda i: (i, 0))
    return _call(body, grid=(S // tm,), name=name, args=[dya, dys, watt, wsg, oatt], comm=comm,
                 in_specs=[row, row, pl.BlockSpec((nb, GROUP_W, Db), lambda i: (0, 0, 0)), pl.BlockSpec((nb, SG_W, Db), lambda i: (0, 0, 0)), att],
                 out_specs=[att, pl.BlockSpec((tm, SG_W), lambda i: (i, 0)), att],
                 out_shape=[jax.ShapeDtypeStruct((S, GROUP_W), F32), jax.ShapeDtypeStruct((S, SG_W), BF), jax.ShapeDtypeStruct((S, GROUP_W), F32)])[0]


def _small_allreduce(pack, name):
    R = pack.shape[0]

    def body(p_ref, o_ref, gath, send, recv):
        x, y, c = _place()
        me = 4 * x + 2 * y + c
        gath[me] = p_ref[...]
        copies = []
        for r in range(1, N_DEV):
            px, py, pc = _flip(x, r & 4), _flip(y, r & 2), _flip(c, r & 1)
            peer = 4 * px + 2 * py + pc
            mk = lambda dst: pltpu.make_async_remote_copy(src_ref=p_ref, dst_ref=dst, send_sem=send.at[r - 1], recv_sem=recv.at[r - 1],
                                                          device_id=(px, py, pc), device_id_type=MESH)
            snd = mk(gath.at[me])
            snd.start()
            copies.append((snd, mk(gath.at[peer])))
        for snd, rcv in copies:
            rcv.wait_recv()
            snd.wait_send()
        acc = gath[0]
        for s in range(1, N_DEV):
            acc = acc + gath[s]
        o_ref[...] = acc

    vm = pl.BlockSpec(memory_space=pltpu.VMEM)
    return pl.pallas_call(
        body, name=name, in_specs=[vm], out_specs=vm, out_shape=jax.ShapeDtypeStruct(pack.shape, F32),
        scratch_shapes=[pltpu.VMEM((N_DEV, R, 128), F32), pltpu.SemaphoreType.DMA((7,)), pltpu.SemaphoreType.DMA((7,))],
        compiler_params=pltpu.CompilerParams(vmem_limit_bytes=VMEM_LIMIT),
    )(pack)


def _row_tile(R, C, elems=262144):
    tr = R
    while tr * C > elems and tr % 32 == 0:
        tr //= 2
    return tr


def _pair_add(parts, other, name):
    _, R, C = parts.shape
    tr = _row_tile(R, C, 1048576)

    def body(c_ref, p_ref, o_ref, s_ref):
        s_ref[0] = (p_ref[0].astype(F32) + o_ref[0].astype(F32)).astype(BF)

    core = lax.axis_index("c").astype(jnp.int32).reshape(1)
    return pl.pallas_call(
        body, name=name,
        grid_spec=pltpu.PrefetchScalarGridSpec(
            num_scalar_prefetch=1, grid=(N_CHIP, R // tr),
            in_specs=[pl.BlockSpec((1, tr, C), lambda q, i, c: (2 * q + c[0], i, 0)), pl.BlockSpec((1, tr, C), lambda q, i, c: (q, i, 0))],
            out_specs=pl.BlockSpec((1, tr, C), lambda q, i, c: (q, i, 0))),
        out_shape=jax.ShapeDtypeStruct((N_CHIP, R, C), BF),
        compiler_params=pltpu.CompilerParams(dimension_semantics=("arbitrary", "arbitrary"), vmem_limit_bytes=VMEM_LIMIT),
    )(core, parts, other)


def _adamw(parts, w, m, v, name):
    ns, R, C = parts.shape
    tr = _row_tile(R, C, 524288)
    c1 = 1.0 - ADAM_B1 ** ADAM_STEP
    c2 = 1.0 - ADAM_B2 ** ADAM_STEP

    def body(p_ref, w_ref, m_ref, v_ref, g_ref, d_ref, nm_ref, nv_ref):
        g = p_ref[0].astype(F32)
        for s in range(1, ns):
            g = g + p_ref[s].astype(F32)
        mn = ADAM_B1 * m_ref[...] + (1.0 - ADAM_B1) * g
        vn = ADAM_B2 * v_ref[...] + (1.0 - ADAM_B2) * (g * g)
        g_ref[...] = g
        nm_ref[...] = mn
        nv_ref[...] = vn
        d_ref[...] = -ADAM_LR * ((mn / c1) / (jnp.sqrt(vn / c2) + ADAM_EPS) + ADAM_WD * w_ref[...])

    row = pl.BlockSpec((tr, C), lambda i: (i, 0))
    shp = jax.ShapeDtypeStruct((R, C), F32)
    return _call(body, grid=(R // tr,), name=name, args=[parts, w, m, v],
                 in_specs=[pl.BlockSpec((ns, tr, C), lambda i: (0, i, 0)), row, row, row], out_specs=[row] * 4, out_shape=[shp] * 4)[0]


def _pad_rows(a, rows):
    return jnp.pad(a, ((0, rows - a.shape[0]), (0, 0)))


def kernel(x, ffn1_norm, ffn1_w_gate, ffn1_w_up, ffn1_w_down, mix_norm, w_in, sg_ln_g, sg_ln_b, sg_w, sg_b, w_att_out, w_sg_out, w_out, ffn2_norm, ffn2_w_gate, ffn2_w_up, ffn2_w_down, final_norm, loss_target, m_ffn1_norm, m_ffn1_w_gate, m_ffn1_w_up, m_ffn1_w_down, m_mix_norm, m_w_in, m_sg_ln_g, m_sg_ln_b, m_sg_w, m_sg_b, m_w_att_out, m_w_sg_out, m_w_out, m_ffn2_norm, m_ffn2_w_gate, m_ffn2_w_up, m_ffn2_w_down, m_final_norm, v_ffn1_norm, v_ffn1_w_gate, v_ffn1_w_up, v_ffn1_w_down, v_mix_norm, v_w_in, v_sg_ln_g, v_sg_ln_b, v_sg_w, v_sg_b, v_w_att_out, v_w_sg_out, v_w_out, v_ffn2_norm, v_ffn2_w_gate, v_ffn2_w_up, v_ffn2_w_down, v_final_norm):
    S, D = x.shape[1], x.shape[2]
    Pb = w_in.shape[2]
    P = N_DEV * Pb
    assert P == GA_OFF + 2 * D and D % (N_DEV * 128) == 0 and S % (BLK * DILATIONS[-1]) == 0
    xs, tgt = x[0], loss_target[0]

    sharded = dict(ffn1_w_gate=ffn1_w_gate, ffn1_w_up=ffn1_w_up, ffn1_w_down=ffn1_w_down, w_in=w_in, w_att_out=w_att_out,
                   w_sg_out=w_sg_out, w_out=w_out, ffn2_w_gate=ffn2_w_gate, ffn2_w_up=ffn2_w_up, ffn2_w_down=ffn2_w_down)
    sb = {n: w[0].astype(BF) for n, w in sharded.items()}

    (wg1,) = _comm_only(_Gather([sb["ffn1_w_gate"]]), "gather_ffn1")
    h1 = _rms_fwd(xs, ffn1_norm, "rms1")
    win_top, win_bot = sb["w_in"][: D // 4], sb["w_in"][D // 4:]
    (g1,), ((wu1,),) = _ffn_gate(h1, wg1, "ffn1_gate", comm=[_Gather([sb["ffn1_w_up"]], 0.9, 0.55)])
    (u1, a1), ((wd1, win8a),) = _ffn_up_act(h1, wu1, g1, "ffn1_up_act", comm=[_Gather([sb["ffn1_w_down"], win_top], 1.0, 0.6)])
    (x1, h2), ((win8b,),) = _ffn_down_norm(a1, wd1, xs, mix_norm, "ffn1_down", comm=[_Gather([win_bot], 0.9, 0.55)])
    win = jnp.concatenate([w8.transpose(1, 0, 2).reshape(w8.shape[1], P) for w8 in (win8a, win8b)], axis=0)
    tm_proj = _tile(S, 1024)
    (qkv,), ((wg2,),) = _mm_nn(h2, win, tm_proj, 512, 0, U_OFF, F32, "proj_qkv", comm=[_Gather([sb["ffn2_w_gate"]], 1.0, 0.7)])
    (rest,), ((wu2,),) = _mm_nn(h2, win, tm_proj, 512, U_OFF, P, BF, "proj_rest", comm=[_Gather([sb["ffn2_w_up"]], 0.75, 0.45)])
    tabs = _rope_tables(S)
    rides = [[_Gather([sb["w_att_out"], sb["w_sg_out"]], 0.9, 0.5)], [_Gather([sb["w_out"]], 0.85, 0.45)], []]
    os, lses, late = [], [], []
    for gi, d in enumerate(DILATIONS):
        (o, l), got_here = _att_fwd(qkv, tabs, gi, d, f"att_fwd{gi}", comm=rides[gi])
        late += [w for g in got_here for w in g]
        os.append(o)
        lses.append(l)
    watt, wsg, wout8 = late
    wout = wout8.reshape(D, D)
    oatt, lse = _att_combine(os, lses, "att_combine")
    sgw = sg_w[0]
    sgbT = jnp.pad(sg_b[0].T, ((0, 0), (0, BLK - SG_GROUPS)))
    z = _sg_fwd(rest, sgw, sgbT, sg_ln_g, sg_ln_b, "sg_fwd")
    (ya, ys, merged), _ = _gate_merge(oatt, z, watt, wsg, rest, "gate_merge")
    x2, h3 = _mix_out(merged, wout, x1, ffn2_norm, "mix_out")
    (g3, u3, a3), ((wd2,),) = _ffn_up(h3, wg2, wu2, "ffn2_up", comm=[_Gather([sb["ffn2_w_down"]], 0.6, 0.35)])
    dx3, dyb3, d_final, loss_part = _ffn_down_loss(a3, wd2, x2, final_norm.reshape(1, D), tgt, "ffn2_down_loss")

    Fb = wg2.shape[2]
    Db = watt.shape[2]
    p_pad = -(-P // PROJ_TK) * PROJ_TK
    winT = jnp.concatenate([w8.transpose(0, 2, 1).reshape(P, w8.shape[1]) for w8 in (win8a, win8b)], axis=1)
    winT = jnp.pad(winT, ((0, p_pad - P), (0, 0)))
    (dg3, du3), _ = _ffn_bwd_act(dyb3, wd2, g3, u3, "ffn2_bwd_act")
    (dwd2,), _ = _ffn_dwd(a3, dyb3, "ffn2_dwd")
    (dwg2, dwu2), _ = _ffn_dwgu(h3, dg3, du3, "ffn2_dwgu")
    ffn2_parts = [dwd2, dwg2, dwu2]
    (dx2, dmixb, d_ffn2n), (ffn2_other,) = _dh_rms_bwd([(dg3, wg2), (du3, wu2)], True, Fb, x2, ffn2_norm, dx3, 1.0, "ffn2_dh",
                                                     comm=[_Swap(ffn2_parts)])
    ffn2_sums = [_pair_add(p, o, f"pair_ffn2_{i}") for i, (p, o) in enumerate(zip(ffn2_parts, ffn2_other))]

    dya, dys, dga, dgs = _mix_bwd_gate(dmixb, wout, ya, ys, rest, "mix_bwd_gate")
    (dwout,), _ = _mm_tn(merged, dmixb, _tile(D, 1024), _tile(D, 1024), _tile(S, 1024), False, "dw_out")
    do, dz, dvec = _att_sg_dout(dya, dys, watt, wsg, oatt, "att_sg_dout")
    (dwatt,), _ = _mm_tn(oatt, dya, GROUP_W, 2 * Db, _tile(S, 1024), Db, "dw_att")
    (dwsg,), _ = _mm_tn(z, dys, SG_W, 2 * Db, _tile(S, 1024), Db, "dw_sg")
    mix_parts = [dwout.reshape(N_DEV, D // N_DEV, D), dwatt, dwsg]
    du, dvs, d_sgw, d_sgbT, d_lng, d_lnb = _sg_bwd(rest, dz, sgw, sgbT, sg_ln_g, sg_ln_b, "sg_bwd")
    dqs, dks, dvs_att, ffn2_got = [], [], [], []
    for gi, d in enumerate(DILATIONS):
        ride = [_Ici([ffn2_sums[0]])] if gi == 2 else []
        (dq, dk, dv), got_here = _att_bwd(qkv, tabs, do, lse, dvec, gi, d, f"att_bwd{gi}", comm=ride)
        ffn2_got += [g[0] for g in got_here]
        dqs.append(dq)
        dks.append(dk)
        dvs_att.append(dv)
    dproj = jnp.concatenate([t.astype(BF) for t in dqs + dks + dvs_att] + [du, dvs, dga, dgs, jnp.zeros((S, p_pad - P), BF)], axis=1)
    (dx1, dyb1, d_mixn), (ffn2_rest, mix_other) = _dh_rms_bwd([(dproj, winT)], False, PROJ_TK, x1, mix_norm, dx2, 0.5, "proj_dh",
                                                            comm=[_Ici(ffn2_sums[1:]), _Swap(mix_parts)])
    ffn2_got += ffn2_rest
    mix_sums = [_pair_add(p, o, f"pair_mix_{i}") for i, (p, o) in enumerate(zip(mix_parts, mix_other))]
    (dwd1,), (mix_got,) = _ffn_dwd(a1, dyb1, "ffn1_dwd", comm=[_Ici(mix_sums)])
    rows = lambda a: a.reshape(-1, 128)
    pad8 = lambda a: _pad_rows(a, -(-a.shape[0] // 8) * 8)
    small = [("sg_w", rows(d_sgw), sg_w, m_sg_w, v_sg_w), ("mix_norm", rows(d_mixn), mix_norm, m_mix_norm, v_mix_norm),
             ("ffn2_norm", rows(d_ffn2n), ffn2_norm, m_ffn2_norm, v_ffn2_norm), ("final_norm", rows(d_final), final_norm, m_final_norm, v_final_norm),
             ("sg_ln_g", rows(d_lng), sg_ln_g, m_sg_ln_g, v_sg_ln_g), ("sg_ln_b", rows(d_lnb), sg_ln_b, m_sg_ln_b, v_sg_ln_b),
             ("sg_b", d_sgbT[:, :SG_GROUPS].T, sg_b, m_sg_b, v_sg_b)]
    gpack = jnp.concatenate([pad8(g) for _, g, _, _, _ in small] + [pad8(loss_part)], axis=0)
    (dwin,), ((wd1_other,), (gpacks,)) = _mm_tn(h2, dproj, D, 512, _tile(S, 2048), False, "dw_in", ncols=P,
                                              comm=[_Swap([dwd1]), _Spread([gpack])])
    dwin = dwin.reshape(D, N_DEV, Pb).transpose(1, 0, 2)
    wd1_sum = _pair_add(dwd1, wd1_other, "pair_wd1")
    (dg1, du1), ((wd1_got,), (win_other,)) = _ffn_bwd_act(dyb1, wd1, g1, u1, "ffn1_bwd_act", comm=[_Ici([wd1_sum]), _Swap([dwin])])
    win_sum = _pair_add(dwin, win_other, "pair_win")
    (dwg1, dwu1), ((win_got,),) = _ffn_dwgu(h1, dg1, du1, "ffn1_dwgu", comm=[_Ici([win_sum])])
    gu_parts = [dwg1, dwu1]
    gu_other = _comm_only(_Swap(gu_parts), "swap_gu1")
    gu_sums = [_pair_add(p, o, f"pair_gu1_{i}") for i, (p, o) in enumerate(zip(gu_parts, gu_other))]
    (dx0, _, d_ffn1n), (gu_got,) = _dh_rms_bwd([(dg1, wg1), (du1, wu1)], True, Fb, xs, ffn1_norm, dx1, 1.0, "ffn1_dh",
                                               comm=[_Ici(gu_sums)])

    got = dict(ffn2_w_down=ffn2_got[0], ffn2_w_gate=ffn2_got[1], ffn2_w_up=ffn2_got[2], w_out=mix_got[0], w_att_out=mix_got[1],
               w_sg_out=mix_got[2], w_in=win_got, ffn1_w_gate=gu_got[0], ffn1_w_up=gu_got[1], ffn1_w_down=wd1_got)
    moments = dict(ffn1_w_gate=(m_ffn1_w_gate, v_ffn1_w_gate), ffn1_w_up=(m_ffn1_w_up, v_ffn1_w_up),
                   ffn1_w_down=(m_ffn1_w_down, v_ffn1_w_down), w_in=(m_w_in, v_w_in), w_att_out=(m_w_att_out, v_w_att_out),
                   w_sg_out=(m_w_sg_out, v_w_sg_out), w_out=(m_w_out, v_w_out), ffn2_w_gate=(m_ffn2_w_gate, v_ffn2_w_gate),
                   ffn2_w_up=(m_ffn2_w_up, v_ffn2_w_up), ffn2_w_down=(m_ffn2_w_down, v_ffn2_w_down))
    res = {}
    for n in sharded:
        mm, vv = moments[n]
        outs = _adamw(got[n], sharded[n][0], mm[0], vv[0], "adamw_" + n)
        res[n] = [o[None] for o in outs]

    zero8 = jnp.zeros((8, 128), F32)
    wpack = jnp.concatenate([pad8(rows(w)) for _, _, w, _, _ in small] + [zero8], axis=0)
    mpack = jnp.concatenate([pad8(rows(m)) for _, _, _, m, _ in small] + [zero8], axis=0)
    vpack = jnp.concatenate([pad8(rows(v)) for _, _, _, _, v in small] + [zero8], axis=0)
    packs = _adamw(gpacks, wpack, mpack, vpack, "adamw_small")
    off = 0
    for n, g, w, _, _ in small:
        r = g.shape[0]
        res[n] = [p[off:off + r].reshape(w.shape) for p in packs]
        off += -(-r // 8) * 8
    loss = packs[0][off, 0]
    g_first = _small_allreduce(rows(d_ffn1n), "allreduce_ffn1_norm")
    res["ffn1_norm"] = [p.reshape(ffn1_norm.shape) for p in
                        _adamw(g_first[None], rows(ffn1_norm), rows(m_ffn1_norm), rows(v_ffn1_norm), "adamw_ffn1_norm")]

    order = ["ffn1_norm", "ffn1_w_gate", "ffn1_w_up", "ffn1_w_down", "mix_norm", "w_in", "sg_ln_g", "sg_ln_b", "sg_w", "sg_b",
             "w_att_out", "w_sg_out", "w_out", "ffn2_norm", "ffn2_w_gate", "ffn2_w_up", "ffn2_w_down", "final_norm"]
    return (loss, dx0[None], *[res[n][0] for n in order], *[res[n][1] for n in order], *[res[n][2] for n in order],
            *[res[n][3] for n in order])
```

```python
import math

import jax
import jax.numpy as jnp
from jax import lax
from jax.experimental import pallas as pl
from jax.experimental.pallas import tpu as pltpu

BF = jnp.bfloat16
F32 = jnp.float32
MESH = pl.DeviceIdType.MESH
N_DEV = 8
N_CHIP = 4

HEAD_DIM = 128
HEADS_PER_GROUP = 4
GROUP_W = HEADS_PER_GROUP * HEAD_DIM
DILATIONS = (1, 4, 16)
ATT_W = len(DILATIONS) * GROUP_W
SG_W = 1536
SG_GROUPS = 12
BLK = 128
ROPE_DIM = 32
ROPE_THETA = 500000.0
NORM_EPS = 1e-6
LN_EPS = 1e-5
Q_OFF, K_OFF, V_OFF, U_OFF, VS_OFF, GA_OFF = 0, ATT_W, 2 * ATT_W, 3 * ATT_W, 3 * ATT_W + SG_W, 3 * ATT_W + 2 * SG_W

ADAM_LR, ADAM_B1, ADAM_B2, ADAM_EPS, ADAM_WD, ADAM_STEP = 0.001, 0.9, 0.999, 1e-08, 0.01, 10

VMEM_LIMIT = 56 * 1024 * 1024
NEG = -1e30
ANY = pl.BlockSpec(memory_space=pl.ANY)
EPI_ROWS = 128
ACC_COLS = 512
FFN_PAIR = 2
PROJ_TK = 1536
R_U, R_VS, R_GA = 0, SG_W, 2 * SG_W


def _once(shape, index_map):
    return pl.BlockSpec(shape, index_map, pipeline_mode=pl.Buffered(1))


def _tile(n, pref):
    t = min(n, pref)
    while n % t:
        t //= 2
    return t


def _nt(a, b):
    return lax.dot_general(a, b, (((1,), (1,)), ((), ())), preferred_element_type=F32)


def _tn(a, b):
    return lax.dot_general(a, b, (((0,), (0,)), ((), ())), preferred_element_type=F32)


def _nn(a, b):
    return jnp.dot(a, b, preferred_element_type=F32)


def _acc_dots(acc_ref, terms, transposed_rhs=False):
    n = acc_ref.shape[1]
    width = min(n, ACC_COLS)
    for c0 in range(0, n, width):
        cols = slice(c0, c0 + width)
        tot = None
        for lhs, rhs in terms:
            part = _nt(lhs, rhs(cols)) if transposed_rhs else _nn(lhs, rhs(cols))
            tot = part if tot is None else tot + part
        acc_ref[:, cols] += tot


def _gelu(x):
    return 0.5 * x * (1.0 + lax.erf(x * (2.0 ** -0.5)))


def _gelu_grad(x):
    return 0.5 * (1.0 + lax.erf(x * (2.0 ** -0.5))) + x * jnp.exp(-0.5 * x * x) * (1.0 / math.sqrt(2.0 * math.pi))


def _place():
    x, y, c = lax.axis_index("x"), lax.axis_index("y"), lax.axis_index("c")
    return x, y, c


def _flip(v, bit):
    return 1 - v if bit else v


class _Gather:
    def __init__(self, shards, mid_frac=1.0, relay_frac=0.5):
        self.arrays = list(shards)
        self.relay_frac = relay_frac
        self.mid_frac = mid_frac
        nw = len(shards)
        self.out_shape = [jax.ShapeDtypeStruct((N_DEV,) + s.shape, s.dtype) for s in shards]
        self.scratch = [pltpu.SemaphoreType.DMA((nw, 7)), pltpu.SemaphoreType.DMA((nw, 7)), pltpu.SemaphoreType.DMA((nw,))]

    def _parts(self, ins, outs, sems):
        x, y, c = _place()
        send, recv, loc = sems
        south = c == 0
        near = (jnp.where(south, x, 1 - x), jnp.where(south, 1 - y, y), c)
        far = (jnp.where(south, 1 - x, x), jnp.where(south, y, 1 - y), c)
        diag = (1 - x, 1 - y, c)

        def copy(k, s, block, to, src=None):
            dst = outs[k].at[4 * block[0] + 2 * block[1] + block[2]]
            return pltpu.make_async_remote_copy(src_ref=dst if src is None else src, dst_ref=dst, send_sem=send.at[k, s],
                                                recv_sem=recv.at[k, s], device_id=to, device_id_type=MESH)

        def first(k):
            me = (x, y, c)
            return [copy(k, 0, me, (x, y, 1 - c), src=ins[k]), copy(k, 1, me, (1 - x, y, c), src=ins[k]),
                    copy(k, 2, me, (x, 1 - y, c), src=ins[k])]

        def local(k):
            return pltpu.make_async_copy(ins[k], outs[k].at[4 * x + 2 * y + c], loc.at[k])

        return x, y, c, near, far, diag, copy, first, local

    def start(self, ins, outs, sems):
        *_, first, local = self._parts(ins, outs, sems)
        for k in range(len(ins)):
            local(k).start()
            for cp in first(k):
                cp.start()

    def relay(self, ins, outs, sems):
        x, y, c, near, far, _, copy, _, _ = self._parts(ins, outs, sems)
        for k in range(len(ins)):
            copy(k, 2 - c, near, (x, y, c)).wait_recv()
            copy(k, 3, near, far).start()
            copy(k, 5 - c, near, (x, y, 1 - c)).start()

    def mid(self, ins, outs, sems):
        x, y, c, _, far, diag, copy, _, _ = self._parts(ins, outs, sems)
        for k in range(len(ins)):
            copy(k, 1 + c, far, (x, y, c)).wait_recv()
            copy(k, 4 + c, far, (x, y, 1 - c)).start()
            copy(k, 3, diag, (x, y, c)).wait_recv()
            copy(k, 6, diag, (x, y, 1 - c)).start()

    def finish(self, ins, outs, sems):
        x, y, c, near, _, _, copy, first, local = self._parts(ins, outs, sems)
        sib = (x, y, 1 - c)
        for k in range(len(ins)):
            copy(k, 0, sib, (x, y, c)).wait_recv()
            copy(k, 4, (1 - x, y, 1 - c), (x, y, c)).wait_recv()
            copy(k, 5, (x, 1 - y, 1 - c), (x, y, c)).wait_recv()
            copy(k, 6, (1 - x, 1 - y, 1 - c), (x, y, c)).wait_recv()
        for k in range(len(ins)):
            for cp in first(k):
                cp.wait_send()
            for s in (3, 4, 5, 6):
                copy(k, s, near, sib).wait_send()
            local(k).wait()


class _Swap:
    def __init__(self, parts):
        self.arrays = list(parts)
        nw = len(parts)
        self.out_shape = [jax.ShapeDtypeStruct((N_CHIP,) + p.shape[1:], p.dtype) for p in parts]
        self.scratch = [pltpu.SemaphoreType.DMA((nw, N_CHIP)), pltpu.SemaphoreType.DMA((nw, N_CHIP))]

    def _copy(self, ins, outs, sems, k, q):
        x, y, c = _place()
        return pltpu.make_async_remote_copy(src_ref=ins[k].at[2 * q + 1 - c], dst_ref=outs[k].at[q], send_sem=sems[0].at[k, q],
                                            recv_sem=sems[1].at[k, q], device_id=(x, y, 1 - c), device_id_type=MESH)

    mid_frac = None

    def start(self, ins, outs, sems):
        for k in range(len(ins)):
            for q in range(N_CHIP):
                self._copy(ins, outs, sems, k, q).start()

    def finish(self, ins, outs, sems):
        for k in range(len(ins)):
            for q in range(N_CHIP):
                self._copy(ins, outs, sems, k, q).wait()


class _Ici:
    mid_frac = None

    def __init__(self, sums):
        self.arrays = list(sums)
        nw = len(sums)
        self.out_shape = [jax.ShapeDtypeStruct(s.shape, s.dtype) for s in sums]
        self.scratch = [pltpu.SemaphoreType.DMA((nw, 3)), pltpu.SemaphoreType.DMA((nw, 3)), pltpu.SemaphoreType.DMA((nw,))]

    def _copies(self, ins, outs, sems, k):
        x, y, c = _place()
        myq = 2 * x + y
        out = []
        for r in range(1, N_CHIP):
            px, py = _flip(x, r & 2), _flip(y, r & 1)
            pq = 2 * px + py
            mk = lambda dst: pltpu.make_async_remote_copy(src_ref=ins[k].at[pq], dst_ref=dst, send_sem=sems[0].at[k, r - 1],
                                                          recv_sem=sems[1].at[k, r - 1], device_id=(px, py, c), device_id_type=MESH)
            out.append((mk(outs[k].at[myq]), mk(outs[k].at[pq])))
        return out, pltpu.make_async_copy(ins[k].at[myq], outs[k].at[myq], sems[2].at[k])

    def start(self, ins, outs, sems):
        for k in range(len(ins)):
            remote, local = self._copies(ins, outs, sems, k)
            local.start()
            for snd, _ in remote:
                snd.start()

    def finish(self, ins, outs, sems):
        for k in range(len(ins)):
            remote, local = self._copies(ins, outs, sems, k)
            for snd, rcv in remote:
                rcv.wait_recv()
                snd.wait_send()
            local.wait()


class _Spread:
    mid_frac = None

    def __init__(self, arrays):
        self.arrays = list(arrays)
        nw = len(arrays)
        self.out_shape = [jax.ShapeDtypeStruct((N_DEV,) + a.shape, a.dtype) for a in arrays]
        self.scratch = [pltpu.SemaphoreType.DMA((nw, 7)), pltpu.SemaphoreType.DMA((nw, 7)), pltpu.SemaphoreType.DMA((nw,))]

    def _copies(self, ins, outs, sems, k):
        x, y, c = _place()
        me = 4 * x + 2 * y + c
        out = []
        for r in range(1, N_DEV):
            px, py, pc = _flip(x, r & 4), _flip(y, r & 2), _flip(c, r & 1)
            peer = 4 * px + 2 * py + pc
            mk = lambda dst: pltpu.make_async_remote_copy(src_ref=ins[k], dst_ref=dst, send_sem=sems[0].at[k, r - 1],
                                                          recv_sem=sems[1].at[k, r - 1], device_id=(px, py, pc), device_id_type=MESH)
            out.append((mk(outs[k].at[me]), mk(outs[k].at[peer])))
        return out, pltpu.make_async_copy(ins[k], outs[k].at[me], sems[2].at[k])

    def start(self, ins, outs, sems):
        for k in range(len(ins)):
            remote, local = self._copies(ins, outs, sems, k)
            local.start()
            for snd, _ in remote:
                snd.start()

    def finish(self, ins, outs, sems):
        for k in range(len(ins)):
            remote, local = self._copies(ins, outs, sems, k)
            for snd, rcv in remote:
                rcv.wait_recv()
                snd.wait_send()
            local.wait()


def _call(body, *, grid, in_specs, out_specs, out_shape, name, args, scratch=(), comm=()):
    comm = list(comm)
    n_in, n_out, n_scr = len(in_specs), len(out_specs), len(scratch)
    total = math.prod(grid) if grid else 1

    def wrapped(*refs):
        p = n_in
        cin = []
        for cm in comm:
            cin.append(refs[p:p + len(cm.arrays)])
            p += len(cm.arrays)
        own_out = refs[p:p + n_out]
        p += n_out
        cout = []
        for cm in comm:
            cout.append(refs[p:p + len(cm.arrays)])
            p += len(cm.arrays)
        own_scr = refs[p:p + n_scr]
        p += n_scr
        csem = []
        for cm in comm:
            csem.append(refs[p:p + len(cm.scratch)])
            p += len(cm.scratch)
        step = 0
        for axis, g in enumerate(grid):
            step = step * g + pl.program_id(axis)

        def at(when, what):
            if total == 1:
                what()
            else:
                pl.when(step == when)(what)

        def starts():
            for cm, i, o, s in zip(comm, cin, cout, csem):
                cm.start(i, o, s)

        def finishes():
            for cm, i, o, s in zip(comm, cin, cout, csem):
                cm.finish(i, o, s)

        if comm:
            at(0, starts)
        if body is not None:
            body(*refs[:n_in], *own_out, *own_scr)
        for cm, i, o, s in zip(comm, cin, cout, csem):
            if cm.mid_frac is not None:
                at(min(total - 1, int(total * cm.relay_frac)), lambda cm=cm, i=i, o=o, s=s: cm.relay(i, o, s))
                at(min(total - 1, int(total * cm.mid_frac)), lambda cm=cm, i=i, o=o, s=s: cm.mid(i, o, s))
        if comm:
            at(total - 1, finishes)

    kw = dict(grid=tuple(grid)) if grid else {}
    outs = pl.pallas_call(
        wrapped, name=name, **kw,
        in_specs=list(in_specs) + [ANY for cm in comm for _ in cm.arrays],
        out_specs=list(out_specs) + [ANY for cm in comm for _ in cm.arrays],
        out_shape=list(out_shape) + [s for cm in comm for s in cm.out_shape],
        scratch_shapes=list(scratch) + [s for cm in comm for s in cm.scratch],
        compiler_params=pltpu.CompilerParams(dimension_semantics=("arbitrary",) * len(grid), vmem_limit_bytes=VMEM_LIMIT),
    )(*args, *[a for cm in comm for a in cm.arrays])
    own, p, per = list(outs[:n_out]), n_out, []
    for cm in comm:
        per.append(list(outs[p:p + len(cm.arrays)]))
        p += len(cm.arrays)
    return own, per


def _comm_only(cm, name):
    return _call(None, grid=(), in_specs=[], out_specs=[], out_shape=[], name=name, args=[], comm=[cm])[1][0]


def _rms_fwd(x, g, name, comm=()):
    S, D = x.shape
    tm = _tile(S, 512)

    def body(x_ref, g_ref, o_ref):
        xv = x_ref[...]
        r = lax.rsqrt(jnp.mean(xv * xv, axis=-1, keepdims=True) + NORM_EPS)
        o_ref[...] = (xv * r * g_ref[...]).astype(BF)

    return _call(body, grid=(S // tm,), name=name, args=[x, g], comm=comm,
                 in_specs=[pl.BlockSpec((tm, D), lambda i: (i, 0)), pl.BlockSpec((1, D), lambda i: (0, 0))],
                 out_specs=[pl.BlockSpec((tm, D), lambda i: (i, 0))], out_shape=[jax.ShapeDtypeStruct((S, D), BF)])


def _ffn_up(h, wg, wu, name, comm=()):
    S, D = h.shape
    nb, _, Fb = wg.shape
    tm = _tile(S, 512)

    def body(h_ref, wg_ref, wu_ref, g_ref, u_ref, a_ref):
        hv = h_ref[...]
        g = _nn(hv, wg_ref[0])
        u = _nn(hv, wu_ref[0])
        g_ref[0] = g.astype(BF)
        u_ref[0] = u.astype(BF)
        a_ref[0] = (g * jax.nn.sigmoid(g) * u).astype(BF)

    act = pl.BlockSpec((1, tm, Fb), lambda j, i: (j, i, 0))
    w = pl.BlockSpec((1, D, Fb), lambda j, i: (j, 0, 0))
    shp = jax.ShapeDtypeStruct((nb, S, Fb), BF)
    return _call(body, grid=(nb, S // tm), name=name, args=[h, wg, wu], comm=comm,
                 in_specs=[pl.BlockSpec((tm, D), lambda j, i: (i, 0)), w, w], out_specs=[act, act, act], out_shape=[shp, shp, shp])


def _ffn_gate(h, wg, name, comm=()):
    S, D = h.shape
    nb, _, Fb = wg.shape
    tm = _tile(S, 512)

    def body(h_ref, wg_ref, g_ref):
        g_ref[0] = _nn(h_ref[...], wg_ref[0]).astype(BF)

    act = pl.BlockSpec((1, tm, Fb), lambda j, i: (j, i, 0))
    return _call(body, grid=(nb, S // tm), name=name, args=[h, wg], comm=comm,
                 in_specs=[pl.BlockSpec((tm, D), lambda j, i: (i, 0)), pl.BlockSpec((1, D, Fb), lambda j, i: (j, 0, 0))],
                 out_specs=[act], out_shape=[jax.ShapeDtypeStruct((nb, S, Fb), BF)])


def _ffn_up_act(h, wu, g, name, comm=()):
    S, D = h.shape
    nb, _, Fb = wu.shape
    tm = _tile(S, 512)

    def body(h_ref, wu_ref, g_ref, u_ref, a_ref):
        u = _nn(h_ref[...], wu_ref[0])
        gv = g_ref[0].astype(F32)
        u_ref[0] = u.astype(BF)
        a_ref[0] = (gv * jax.nn.sigmoid(gv) * u).astype(BF)

    act = pl.BlockSpec((1, tm, Fb), lambda j, i: (j, i, 0))
    shp = jax.ShapeDtypeStruct((nb, S, Fb), BF)
    return _call(body, grid=(nb, S // tm), name=name, args=[h, wu, g], comm=comm,
                 in_specs=[pl.BlockSpec((tm, D), lambda j, i: (i, 0)), pl.BlockSpec((1, D, Fb), lambda j, i: (j, 0, 0)), act],
                 out_specs=[act, act], out_shape=[shp, shp])


def _ffn_down_norm(a, wd, x, gn, name, comm=()):
    nb, S, Fb = a.shape
    D = wd.shape[2]
    tm = _tile(S, 512)

    nj = nb // FFN_PAIR

    def body(a_ref, wd_ref, x_ref, gn_ref, xo_ref, hn_ref, acc_ref):
        j = pl.program_id(1)

        @pl.when(j == 0)
        def _():
            acc_ref[...] = jnp.zeros_like(acc_ref)

        _acc_dots(acc_ref, [(a_ref[b], lambda cols, b=b: wd_ref[b, :, cols]) for b in range(FFN_PAIR)])

        @pl.when(j == nj - 1)
        def _():
            def chunk(t, carry):
                rows = pl.ds(pl.multiple_of(t * EPI_ROWS, EPI_ROWS), EPI_ROWS)
                xo = x_ref[rows, :] + 0.5 * acc_ref[rows, :]
                r = lax.rsqrt(jnp.mean(xo * xo, axis=-1, keepdims=True) + NORM_EPS)
                xo_ref[rows, :] = xo
                hn_ref[rows, :] = (xo * r * gn_ref[...]).astype(BF)
                return carry

            lax.fori_loop(0, tm // EPI_ROWS, chunk, 0)

    row = pl.BlockSpec((tm, D), lambda i, j: (i, 0))
    return _call(body, grid=(S // tm, nj), name=name, args=[a, wd, x, gn], comm=comm,
                 in_specs=[pl.BlockSpec((FFN_PAIR, tm, Fb), lambda i, j: (j, i, 0)), pl.BlockSpec((FFN_PAIR, Fb, D), lambda i, j: (j, 0, 0)),
                           _once((tm, D), lambda i, j: (i, 0)), pl.BlockSpec((1, D), lambda i, j: (0, 0))],
                 out_specs=[row, row], out_shape=[jax.ShapeDtypeStruct((S, D), F32), jax.ShapeDtypeStruct((S, D), BF)],
                 scratch=[pltpu.VMEM((tm, D), F32)])


def _ffn_down_loss(a, wd, x, gf, tgt, name):
    nb, S, Fb = a.shape
    D = wd.shape[2]
    tm = _tile(S, 512)

    nj = nb // FFN_PAIR

    def body(a_ref, wd_ref, x_ref, gf_ref, t_ref, dx_ref, dxb_ref, dgf_ref, loss_ref, acc_ref):
        i, j = pl.program_id(0), pl.program_id(1)

        @pl.when(j == 0)
        def _():
            acc_ref[...] = jnp.zeros_like(acc_ref)

        _acc_dots(acc_ref, [(a_ref[b], lambda cols, b=b: wd_ref[b, :, cols]) for b in range(FFN_PAIR)])

        @pl.when((j == nj - 1) & (i == 0))
        def _():
            dgf_ref[...] = jnp.zeros_like(dgf_ref)
            loss_ref[...] = jnp.zeros_like(loss_ref)

        @pl.when(j == nj - 1)
        def _():
            def chunk(t, carry):
                rows = pl.ds(pl.multiple_of(t * EPI_ROWS, EPI_ROWS), EPI_ROWS)
                xo = x_ref[rows, :] + 0.5 * acc_ref[rows, :]
                r = lax.rsqrt(jnp.mean(xo * xo, axis=-1, keepdims=True) + NORM_EPS)
                xh = xo * r
                gf = gf_ref[...]
                e = xh * gf - t_ref[rows, :]
                loss_ref[...] += jnp.sum(jnp.mean(e * e, axis=-1, keepdims=True), axis=0, keepdims=True) * 0.5
                dy = e * (1.0 / D)
                dgf_ref[...] += jnp.sum(dy * xh, axis=0, keepdims=True)
                dxh = dy * gf
                dx = r * (dxh - xh * jnp.mean(dxh * xh, axis=-1, keepdims=True))
                dx_ref[rows, :] = dx
                dxb_ref[rows, :] = (0.5 * dx).astype(BF)
                return carry

            lax.fori_loop(0, tm // EPI_ROWS, chunk, 0)

    row = pl.BlockSpec((tm, D), lambda i, j: (i, 0))
    once = _once((tm, D), lambda i, j: (i, 0))
    vec = pl.BlockSpec((1, D), lambda i, j: (0, 0))
    return _call(body, grid=(S // tm, nj), name=name, args=[a, wd, x, gf, tgt],
                 in_specs=[pl.BlockSpec((FFN_PAIR, tm, Fb), lambda i, j: (j, i, 0)), pl.BlockSpec((FFN_PAIR, Fb, D), lambda i, j: (j, 0, 0)),
                           once, vec, once],
                 out_specs=[row, row, vec, pl.BlockSpec((1, 128), lambda i, j: (0, 0))],
                 out_shape=[jax.ShapeDtypeStruct((S, D), F32), jax.ShapeDtypeStruct((S, D), BF), jax.ShapeDtypeStruct((1, D), F32),
                            jax.ShapeDtypeStruct((1, 128), F32)],
                 scratch=[pltpu.VMEM((tm, D), F32)])[0]


def _ffn_bwd_act(dyb, wd, g, u, name, comm=()):
    S, D = dyb.shape
    nb, Fb, _ = wd.shape
    tm = _tile(S, 512)

    def body(dy_ref, wd_ref, g_ref, u_ref, dg_ref, du_ref):
        da = _nt(dy_ref[...], wd_ref[0])
        gv = g_ref[0].astype(F32)
        uv = u_ref[0].astype(F32)
        sg = jax.nn.sigmoid(gv)
        du_ref[0] = (da * gv * sg).astype(BF)
        dg_ref[0] = (da * uv * sg * (1.0 + gv * (1.0 - sg))).astype(BF)

    act = pl.BlockSpec((1, tm, Fb), lambda j, i: (j, i, 0))
    shp = jax.ShapeDtypeStruct((nb, S, Fb), BF)
    return _call(body, grid=(nb, S // tm), name=name, args=[dyb, wd, g, u], comm=comm,
                 in_specs=[pl.BlockSpec((tm, D), lambda j, i: (i, 0)), pl.BlockSpec((1, Fb, D), lambda j, i: (j, 0, 0)), act, act],
                 out_specs=[act, act], out_shape=[shp, shp])


def _ffn_dwd(a, dyb, name, comm=()):
    nb, S, Fb = a.shape
    D = dyb.shape[1]
    ts = _tile(S, 512)
    ns = S // ts

    def body(a_ref, dy_ref, o_ref, acc_ref):
        s = pl.program_id(1)

        @pl.when(s == 0)
        def _():
            acc_ref[...] = jnp.zeros_like(acc_ref)

        acc_ref[...] += _tn(a_ref[0], dy_ref[...])

        @pl.when(s == ns - 1)
        def _():
            o_ref[0] = acc_ref[...].astype(BF)

    return _call(body, grid=(nb, ns), name=name, args=[a, dyb], comm=comm,
                 in_specs=[pl.BlockSpec((1, ts, Fb), lambda j, s: (j, s, 0)), pl.BlockSpec((ts, D), lambda j, s: (s, 0))],
                 out_specs=[pl.BlockSpec((1, Fb, D), lambda j, s: (j, 0, 0))], out_shape=[jax.ShapeDtypeStruct((nb, Fb, D), BF)],
                 scratch=[pltpu.VMEM((Fb, D), F32)])


def _ffn_dwgu(h, dg, du, name, comm=()):
    S, D = h.shape
    nb, _, Fb = dg.shape
    ts = _tile(S, 512)
    ns = S // ts

    def body(h_ref, dg_ref, du_ref, og_ref, ou_ref, accg_ref, accu_ref):
        s = pl.program_id(1)

        @pl.when(s == 0)
        def _():
            accg_ref[...] = jnp.zeros_like(accg_ref)
            accu_ref[...] = jnp.zeros_like(accu_ref)

        hv = h_ref[...]
        accg_ref[...] += _tn(hv, dg_ref[0])
        accu_ref[...] += _tn(hv, du_ref[0])

        @pl.when(s == ns - 1)
        def _():
            og_ref[0] = accg_ref[...].astype(BF)
            ou_ref[0] = accu_ref[...].astype(BF)

    act = pl.BlockSpec((1, ts, Fb), lambda j, s: (j, s, 0))
    out = pl.BlockSpec((1, D, Fb), lambda j, s: (j, 0, 0))
    shp = jax.ShapeDtypeStruct((nb, D, Fb), BF)
    return _call(body, grid=(nb, ns), name=name, args=[h, dg, du], comm=comm,
                 in_specs=[pl.BlockSpec((ts, D), lambda j, s: (s, 0)), act, act], out_specs=[out, out], out_shape=[shp, shp],
                 scratch=[pltpu.VMEM((D, Fb), F32), pltpu.VMEM((D, Fb), F32)])


def _dh_rms_bwd(pairs, blocked, tk, x, gn, dxo, out_scale, name, comm=()):
    S, D = x.shape
    nk = pairs[0][0].shape[0] if blocked else pairs[0][0].shape[1] // tk
    tm = _tile(S, 512)
    npair = len(pairs)

    def body(*refs):
        ins = refs[: 2 * npair]
        x_ref, gn_ref, dxo_ref, dx_ref, dxb_ref, dgn_ref, acc_ref = refs[2 * npair:]
        i, k = pl.program_id(0), pl.program_id(1)

        @pl.when(k == 0)
        def _():
            acc_ref[...] = jnp.zeros_like(acc_ref)

        if blocked:
            terms = [(ins[2 * p][0], lambda cols, r=ins[2 * p + 1]: r[0, cols, :]) for p in range(npair)]
        else:
            terms = [(ins[2 * p][...], lambda cols, r=ins[2 * p + 1]: r[:, cols]) for p in range(npair)]
        _acc_dots(acc_ref, terms, transposed_rhs=blocked)

        @pl.when((k == nk - 1) & (i == 0))
        def _():
            dgn_ref[...] = jnp.zeros_like(dgn_ref)

        @pl.when(k == nk - 1)
        def _():
            def chunk(t, carry):
                rows = pl.ds(pl.multiple_of(t * EPI_ROWS, EPI_ROWS), EPI_ROWS)
                xv = x_ref[rows, :]
                r = lax.rsqrt(jnp.mean(xv * xv, axis=-1, keepdims=True) + NORM_EPS)
                xh = xv * r
                dh = acc_ref[rows, :]
                dgn_ref[...] += jnp.sum(dh * xh, axis=0, keepdims=True)
                dxh = dh * gn_ref[...]
                dx = dxo_ref[rows, :] + r * (dxh - xh * jnp.mean(dxh * xh, axis=-1, keepdims=True))
                dx_ref[rows, :] = dx
                dxb_ref[rows, :] = (out_scale * dx).astype(BF)
                return carry

            lax.fori_loop(0, tm // EPI_ROWS, chunk, 0)

    if blocked:
        lspec = pl.BlockSpec((1, tm, tk), lambda i, k: (k, i, 0))
        rspec = pl.BlockSpec((1, D, tk), lambda i, k: (k, 0, 0))
    else:
        lspec = pl.BlockSpec((tm, tk), lambda i, k: (i, k))
        rspec = pl.BlockSpec((tk, D), lambda i, k: (k, 0))
    row = pl.BlockSpec((tm, D), lambda i, k: (i, 0))
    once = _once((tm, D), lambda i, k: (i, 0))
    vec = pl.BlockSpec((1, D), lambda i, k: (0, 0))
    flat = [t for pr in pairs for t in pr]
    return _call(body, grid=(S // tm, nk), name=name, args=[*flat, x, gn, dxo], comm=comm,
                 in_specs=[lspec, rspec] * npair + [once, vec, once], out_specs=[row, row, vec],
                 out_shape=[jax.ShapeDtypeStruct((S, D), F32), jax.ShapeDtypeStruct((S, D), BF), jax.ShapeDtypeStruct((1, D), F32)],
                 scratch=[pltpu.VMEM((tm, D), F32)])


def _mm_nn(a, b, tm, tn, col0, col1, dtype, name, comm=()):
    M, K = a.shape
    n0, nn = col0 // tn, (col1 - col0) // tn

    def body(a_ref, b_ref, o_ref):
        o_ref[...] = _nn(a_ref[...], b_ref[...]).astype(dtype)

    return _call(body, grid=(nn, M // tm), name=name, args=[a, b], comm=comm,
                 in_specs=[pl.BlockSpec((tm, K), lambda n, i: (i, 0)), pl.BlockSpec((K, tn), lambda n, i: (0, n0 + n))],
                 out_specs=[pl.BlockSpec((tm, tn), lambda n, i: (i, n))], out_shape=[jax.ShapeDtypeStruct((M, nn * tn), dtype)])


def _mm_tn(a, b, tm, tn, ts, blocked, name, comm=(), ncols=None):
    S, M = a.shape
    N = b.shape[1] if ncols is None else ncols
    ns = S // ts
    per_tile = tn // blocked if blocked else 0

    def body(a_ref, b_ref, o_ref, acc_ref):
        s = pl.program_id(2)

        @pl.when(s == 0)
        def _():
            acc_ref[...] = jnp.zeros_like(acc_ref)

        acc_ref[...] += _tn(a_ref[...], b_ref[...])

        @pl.when(s == ns - 1)
        def _():
            if blocked:
                for t in range(per_tile):
                    o_ref[t] = acc_ref[:, t * blocked:(t + 1) * blocked].astype(BF)
            else:
                o_ref[...] = acc_ref[...].astype(BF)

    if blocked:
        ospec = pl.BlockSpec((per_tile, tm, blocked), lambda i, n, s: (n, i, 0))
        oshape = jax.ShapeDtypeStruct((N // blocked, M, blocked), BF)
    else:
        ospec = pl.BlockSpec((tm, tn), lambda i, n, s: (i, n))
        oshape = jax.ShapeDtypeStruct((M, N), BF)
    return _call(body, grid=(M // tm, N // tn, ns), name=name, args=[a, b], comm=comm,
                 in_specs=[pl.BlockSpec((ts, tm), lambda i, n, s: (s, i)), pl.BlockSpec((ts, tn), lambda i, n, s: (s, n))],
                 out_specs=[ospec], out_shape=[oshape], scratch=[pltpu.VMEM((tm, tn), F32)])


def _rope_tables(S):
    half = ROPE_DIM // 2
    inv_freq = ROPE_THETA ** (-jnp.arange(0, ROPE_DIM, 2, dtype=F32) / ROPE_DIM)
    ang = jnp.arange(S, dtype=F32)[:, None] * inv_freq[None, :]
    cos, sin = jnp.cos(ang), jnp.sin(ang)
    zeros = jnp.zeros((S, HEAD_DIM - ROPE_DIM), F32)
    c = jnp.concatenate([cos, cos, jnp.ones((S, HEAD_DIM - ROPE_DIM), F32)], axis=1)
    sm = jnp.concatenate([-sin, jnp.zeros((S, half), F32), zeros], axis=1)
    sp = jnp.concatenate([jnp.zeros((S, half), F32), sin, zeros], axis=1)
    return c, sm, sp


def _rope(t, c, sm, sp):
    return t * c + pltpu.roll(t, HEAD_DIM - ROPE_DIM // 2, 1) * sm + pltpu.roll(t, ROPE_DIM // 2, 1) * sp


def _rope_t(dy, c, sm, sp):
    return dy * c + pltpu.roll(dy * sm, ROPE_DIM // 2, 1) + pltpu.roll(dy * sp, HEAD_DIM - ROPE_DIM // 2, 1)


def _att_mask(i):
    qi = lax.broadcasted_iota(jnp.int32, (BLK, 2 * BLK), 0)
    kj = lax.broadcasted_iota(jnp.int32, (BLK, 2 * BLK), 1)
    diff = qi + BLK - kj
    first_key = jnp.where(i > 0, 0, BLK)
    return (diff >= 0) & (diff <= BLK) & (kj >= first_key)


def _res_rows(r, i, n, d):
    if d == 1:
        return pl.ds(pl.multiple_of(i * n, n), n)
    return pl.ds(r + i * (n * d), n, stride=d)


def _att_specs(S, gi):
    def sect(off):
        base = (off + gi * GROUP_W) // HEAD_DIM
        return _once((S, HEAD_DIM), lambda hh: (0, base + hh))

    tab = pl.BlockSpec((S, HEAD_DIM), lambda hh: (0, 0))
    head = pl.BlockSpec((S, HEAD_DIM), lambda hh: (0, hh))
    return sect, tab, head


def _each_residue(d, fn):
    if d == 1:
        fn(0)
    else:
        lax.fori_loop(0, d, lambda r, carry: (fn(r), carry)[1], 0)


def _att_fwd(qkv, tabs, gi, d, name, comm=()):
    S = qkv.shape[0]
    L = S // d
    sect, tab, head = _att_specs(S, gi)
    nblk = L // BLK
    scale = HEAD_DIM ** -0.5

    def body(q_ref, k_ref, v_ref, c_ref, sm_ref, sp_ref, o_ref, lse_ref, qr, kp, vp):
        kp[pl.ds(0, BLK), :] = jnp.zeros((BLK, HEAD_DIM), BF)
        vp[pl.ds(0, BLK), :] = jnp.zeros((BLK, HEAD_DIM), BF)

        def residue(r):
            res = _res_rows(r, 0, L, d)
            c, sm, sp = c_ref[res, :], sm_ref[res, :], sp_ref[res, :]
            qr[...] = _rope(q_ref[res, :], c, sm, sp).astype(BF)
            kp[pl.ds(BLK, L), :] = _rope(k_ref[res, :], c, sm, sp).astype(BF)
            vp[pl.ds(BLK, L), :] = v_ref[res, :].astype(BF)

            def blk(i, carry):
                r0 = pl.multiple_of(i * BLK, BLK)
                s = _nt(qr[pl.ds(r0, BLK), :], kp[pl.ds(r0, 2 * BLK), :]) * scale
                s = jnp.where(_att_mask(i), s, NEG)
                m = jnp.max(s, axis=-1, keepdims=True)
                p = jnp.exp(s - m)
                l = jnp.sum(p, axis=-1, keepdims=True)
                out = _res_rows(r, i, BLK, d)
                o_ref[out, :] = _nn(p.astype(BF), vp[pl.ds(r0, 2 * BLK), :]) / l
                lse_ref[out, :] = jnp.broadcast_to(m + jnp.log(l), (BLK, HEAD_DIM))
                return carry

            lax.fori_loop(0, nblk, blk, 0, unroll=min(4, nblk))

        _each_residue(d, residue)

    shp = jax.ShapeDtypeStruct((S, GROUP_W), F32)
    return _call(body, grid=(HEADS_PER_GROUP,), name=name, args=[qkv, qkv, qkv, *tabs], comm=comm,
                 in_specs=[sect(Q_OFF), sect(K_OFF), sect(V_OFF), tab, tab, tab], out_specs=[head, head], out_shape=[shp, shp],
                 scratch=[pltpu.VMEM((L, HEAD_DIM), BF), pltpu.VMEM((L + BLK, HEAD_DIM), BF), pltpu.VMEM((L + BLK, HEAD_DIM), BF)])


def _att_combine(os, lses, name):
    S = os[0].shape[0]
    tm = _tile(S, 512)

    def body(o0, o1, o2, l0, l1, l2, oa_ref, lse_ref):
        a, b, c = l0[...], l1[...], l2[...]
        mx = jnp.maximum(jnp.maximum(a, b), c)
        wa, wb, wc = jnp.exp(a - mx), jnp.exp(b - mx), jnp.exp(c - mx)
        den = wa + wb + wc
        oa_ref[...] = ((wa * o0[...] + wb * o1[...] + wc * o2[...]) / den).astype(BF)
        lse_ref[...] = mx + jnp.log(den)

    row = pl.BlockSpec((tm, GROUP_W), lambda i: (i, 0))
    return _call(body, grid=(S // tm,), name=name, args=[*os, *lses], in_specs=[row] * 6, out_specs=[row, row],
                 out_shape=[jax.ShapeDtypeStruct((S, GROUP_W), BF), jax.ShapeDtypeStruct((S, GROUP_W), F32)])[0]


def _att_bwd(qkv, tabs, do, lse, dvec, gi, d, name, comm=()):
    S = qkv.shape[0]
    L = S // d
    sect, tab, head = _att_specs(S, gi)
    stat = _once((S, HEAD_DIM), lambda hh: (0, hh))
    nblk = L // BLK
    scale = HEAD_DIM ** -0.5

    def body(q_ref, k_ref, v_ref, c_ref, sm_ref, sp_ref, do_ref, lse_ref, dv_ref, dq_out, dk_out, dv_out, qr, kp, vp, dkp, dvp):
        kp[pl.ds(0, BLK), :] = jnp.zeros((BLK, HEAD_DIM), BF)
        vp[pl.ds(0, BLK), :] = jnp.zeros((BLK, HEAD_DIM), BF)

        def residue(r):
            res = _res_rows(r, 0, L, d)
            c, sm, sp = c_ref[res, :], sm_ref[res, :], sp_ref[res, :]
            qr[...] = _rope(q_ref[res, :], c, sm, sp).astype(BF)
            kp[pl.ds(BLK, L), :] = _rope(k_ref[res, :], c, sm, sp).astype(BF)
            vp[pl.ds(BLK, L), :] = v_ref[res, :].astype(BF)
            dkp[...] = jnp.zeros_like(dkp)
            dvp[...] = jnp.zeros_like(dvp)

            def blk(i, carry):
                r0 = pl.multiple_of(i * BLK, BLK)
                rows, win, pos = pl.ds(r0, BLK), pl.ds(r0, 2 * BLK), _res_rows(r, i, BLK, d)
                q, kw, vw, dob = qr[rows, :], kp[win, :], vp[win, :], do_ref[pos, :].astype(BF)
                s = jnp.where(_att_mask(i), _nt(q, kw) * scale, NEG)
                p = jnp.exp(s - lse_ref[pos, :][:, :1])
                ds = p * (_nt(dob, vw) - dv_ref[pos, :][:, :1]) * scale
                dsb = ds.astype(BF)
                dq_out[pos, :] = _rope_t(_nn(dsb, kw), c_ref[pos, :], sm_ref[pos, :], sp_ref[pos, :])
                dkp[win, :] += _tn(dsb, q)
                dvp[win, :] += _tn(p.astype(BF), dob)
                return carry

            lax.fori_loop(0, nblk, blk, 0, unroll=2)
            dk_out[res, :] = _rope_t(dkp[pl.ds(BLK, L), :], c, sm, sp)
            dv_out[res, :] = dvp[pl.ds(BLK, L), :]

        _each_residue(d, residue)

    shp = jax.ShapeDtypeStruct((S, GROUP_W), F32)
    return _call(body, grid=(HEADS_PER_GROUP,), name=name, args=[qkv, qkv, qkv, *tabs, do, lse, dvec], comm=comm,
                 in_specs=[sect(Q_OFF), sect(K_OFF), sect(V_OFF), tab, tab, tab, stat, stat, stat],
                 out_specs=[head, head, head], out_shape=[shp, shp, shp],
                 scratch=[pltpu.VMEM((L, HEAD_DIM), BF), pltpu.VMEM((L + BLK, HEAD_DIM), BF), pltpu.VMEM((L + BLK, HEAD_DIM), BF),
                          pltpu.VMEM((L + BLK, HEAD_DIM), F32), pltpu.VMEM((L + BLK, HEAD_DIM), F32)])


def _sg_parts(u_ref, vs_ref, g_ref, b_ref):
    uv = u_ref[...].astype(F32)
    vv = vs_ref[...].astype(F32)
    vg = _gelu(vv)
    mu = jnp.mean(vg, axis=-1, keepdims=True)
    vc = vg - mu
    rs = lax.rsqrt(jnp.mean(vc * vc, axis=-1, keepdims=True) + LN_EPS)
    y = vc * rs
    return uv, vv, rs, y, y * g_ref[...] + b_ref[...]


def _sg_wmask():
    t = lax.broadcasted_iota(jnp.int32, (BLK, BLK), 0)
    s = lax.broadcasted_iota(jnp.int32, (BLK, BLK), 1)
    return s <= t


def _sg_fwd(proj, sgw, sgbT, lng, lnb, name):
    S, P = proj.shape

    def body(u_ref, vs_ref, w_ref, bt_ref, g_ref, b_ref, z_ref):
        uv, _, _, _, vln = _sg_parts(u_ref, vs_ref, g_ref, b_ref)
        ug = _gelu(uv)
        vb = vln.astype(BF)
        mask = _sg_wmask()
        bt = bt_ref[...]
        for g in range(SG_GROUPS):
            cols = slice(g * BLK, (g + 1) * BLK)
            w = jnp.where(mask, w_ref[g], 0.0).astype(BF)
            sp = _nn(w, vb[:, cols]) + bt[:, g:g + 1]
            z_ref[:, cols] = (ug[:, cols] * sp).astype(BF)

    tile = lambda off: pl.BlockSpec((BLK, SG_W), lambda i: (i, off // SG_W))
    full = lambda shape: pl.BlockSpec(shape, lambda i: (0,) * len(shape))
    return _call(body, grid=(S // BLK,), name=name, args=[proj, proj, sgw, sgbT, lng, lnb],
                 in_specs=[tile(R_U), tile(R_VS), full((SG_GROUPS, BLK, BLK)), full((BLK, BLK)), full((1, SG_W)), full((1, SG_W))],
                 out_specs=[pl.BlockSpec((BLK, SG_W), lambda i: (i, 0))], out_shape=[jax.ShapeDtypeStruct((S, SG_W), BF)])[0][0]


def _sg_bwd(proj, dz, sgw, sgbT, lng, lnb, name):
    S, P = proj.shape

    def body(u_ref, vs_ref, dz_ref, w_ref, bt_ref, g_ref, b_ref, du_ref, dvs_ref, dw_ref, dbt_ref, dg_ref, db_ref, dvln):
        @pl.when(pl.program_id(0) == 0)
        def _():
            dw_ref[...] = jnp.zeros_like(dw_ref)
            dbt_ref[...] = jnp.zeros_like(dbt_ref)
            dg_ref[...] = jnp.zeros_like(dg_ref)
            db_ref[...] = jnp.zeros_like(db_ref)

        uv, vv, rs, y, vln = _sg_parts(u_ref, vs_ref, g_ref, b_ref)
        ug = _gelu(uv)
        vb = vln.astype(BF)
        dzv = dz_ref[...].astype(F32)
        dsp = dzv * ug
        dspb = dsp.astype(BF)
        mask = _sg_wmask()
        bt = bt_ref[...]
        lane = lax.broadcasted_iota(jnp.int32, (BLK, BLK), 1)
        dbt = jnp.zeros((BLK, BLK), F32)
        for g in range(SG_GROUPS):
            cols = slice(g * BLK, (g + 1) * BLK)
            w = jnp.where(mask, w_ref[g], 0.0).astype(BF)
            sp = _nn(w, vb[:, cols]) + bt[:, g:g + 1]
            du_ref[:, cols] = (dzv[:, cols] * sp * _gelu_grad(uv[:, cols])).astype(BF)
            dw_ref[g] += jnp.where(mask, _nt(dspb[:, cols], vb[:, cols]), 0.0)
            dbt = dbt + jnp.where(lane == g, jnp.sum(dsp[:, cols], axis=-1, keepdims=True), 0.0)
            dvln[:, cols] = _tn(w, dspb[:, cols])
        dbt_ref[...] += dbt
        dvl = dvln[...]
        dg_ref[...] += jnp.sum(dvl * y, axis=0, keepdims=True)
        db_ref[...] += jnp.sum(dvl, axis=0, keepdims=True)
        dy = dvl * g_ref[...]
        dvg = rs * (dy - jnp.mean(dy, axis=-1, keepdims=True) - y * jnp.mean(dy * y, axis=-1, keepdims=True))
        dvs_ref[...] = (dvg * _gelu_grad(vv)).astype(BF)

    tile = lambda off: pl.BlockSpec((BLK, SG_W), lambda i: (i, off // SG_W))
    full = lambda shape: pl.BlockSpec(shape, lambda i: (0,) * len(shape))
    row = pl.BlockSpec((BLK, SG_W), lambda i: (i, 0))
    return _call(body, grid=(S // BLK,), name=name, args=[proj, proj, dz, sgw, sgbT, lng, lnb],
                 in_specs=[tile(R_U), tile(R_VS), row, full((SG_GROUPS, BLK, BLK)), full((BLK, BLK)), full((1, SG_W)), full((1, SG_W))],
                 out_specs=[row, row, full((SG_GROUPS, BLK, BLK)), full((BLK, BLK)), full((1, SG_W)), full((1, SG_W))],
                 out_shape=[jax.ShapeDtypeStruct((S, SG_W), BF), jax.ShapeDtypeStruct((S, SG_W), BF),
                            jax.ShapeDtypeStruct((SG_GROUPS, BLK, BLK), F32), jax.ShapeDtypeStruct((BLK, BLK), F32),
                            jax.ShapeDtypeStruct((1, SG_W), F32), jax.ShapeDtypeStruct((1, SG_W), F32)],
                 scratch=[pltpu.VMEM((BLK, SG_W), F32)])[0]


def _gate_merge(oatt, z, watt, wsg, proj, name, comm=()):
    S = oatt.shape[0]
    nb, _, Db = watt.shape
    D = nb * Db
    tm = _tile(S, 512)
    half = D // 2
    ga, gs = R_GA // half, (R_GA + D) // half

    def body(oa_ref, z_ref, wa_ref, ws_ref, ga0, ga1, gs0, gs1, ya_ref, ys_ref, mg_ref):
        oa, zv = oa_ref[...], z_ref[...]
        for j in range(nb):
            cols = slice(j * Db, (j + 1) * Db)
            g_a, g_s = (ga0, gs0) if j < nb // 2 else (ga1, gs1)
            gcols = slice((j % (nb // 2)) * Db, (j % (nb // 2) + 1) * Db)
            ya = _nn(oa, wa_ref[j])
            ys = _nn(zv, ws_ref[j])
            ya_ref[:, cols] = ya.astype(BF)
            ys_ref[:, cols] = ys.astype(BF)
            mg_ref[:, cols] = (jax.nn.sigmoid(g_a[:, gcols].astype(F32)) * ya + jax.nn.sigmoid(g_s[:, gcols].astype(F32)) * ys).astype(BF)

    out = pl.BlockSpec((tm, D), lambda i: (i, 0))
    gate = lambda b: pl.BlockSpec((tm, half), lambda i: (i, b))
    shp = jax.ShapeDtypeStruct((S, D), BF)
    return _call(body, grid=(S // tm,), name=name, args=[oatt, z, watt, wsg, proj, proj, proj, proj], comm=comm,
                 in_specs=[pl.BlockSpec((tm, GROUP_W), lambda i: (i, 0)), pl.BlockSpec((tm, SG_W), lambda i: (i, 0)),
                           pl.BlockSpec((nb, GROUP_W, Db), lambda i: (0, 0, 0)), pl.BlockSpec((nb, SG_W, Db), lambda i: (0, 0, 0)),
                           gate(ga), gate(ga + 1), gate(gs), gate(gs + 1)],
                 out_specs=[out, out, out], out_shape=[shp, shp, shp])


def _mix_out(merged, wout, x, gn, name):
    S, D = x.shape
    tm = _tile(S, 256)

    def body(m_ref, w_ref, x_ref, gn_ref, xo_ref, hn_ref):
        xo = x_ref[...] + _nn(m_ref[...], w_ref[...])
        r = lax.rsqrt(jnp.mean(xo * xo, axis=-1, keepdims=True) + NORM_EPS)
        xo_ref[...] = xo
        hn_ref[...] = (xo * r * gn_ref[...]).astype(BF)

    row = pl.BlockSpec((tm, D), lambda i: (i, 0))
    return _call(body, grid=(S // tm,), name=name, args=[merged, wout, x, gn],
                 in_specs=[row, pl.BlockSpec((D, D), lambda i: (0, 0)), row, pl.BlockSpec((1, D), lambda i: (0, 0))],
                 out_specs=[row, row], out_shape=[jax.ShapeDtypeStruct((S, D), F32), jax.ShapeDtypeStruct((S, D), BF)])[0]


def _mix_bwd_gate(dmix, wout, ya, ys, proj, name):
    S, D = dmix.shape
    tm, tn = _tile(S, 512), 512
    ga, gs = R_GA // tn, (R_GA + D) // tn

    def body(dm_ref, w_ref, ya_ref, ys_ref, ga_ref, gs_ref, dya_ref, dys_ref, dga_ref, dgs_ref):
        w_rows = pl.ds(pl.multiple_of(pl.program_id(1) * tn, tn), tn)
        dm = _nt(dm_ref[...], w_ref[w_rows, :])
        sa = jax.nn.sigmoid(ga_ref[...].astype(F32))
        ss = jax.nn.sigmoid(gs_ref[...].astype(F32))
        dya_ref[...] = (dm * sa).astype(BF)
        dys_ref[...] = (dm * ss).astype(BF)
        dga_ref[...] = (dm * ya_ref[...].astype(F32) * sa * (1.0 - sa)).astype(BF)
        dgs_ref[...] = (dm * ys_ref[...].astype(F32) * ss * (1.0 - ss)).astype(BF)

    out = pl.BlockSpec((tm, tn), lambda i, n: (i, n))
    shp = jax.ShapeDtypeStruct((S, D), BF)
    return _call(body, grid=(S // tm, D // tn), name=name, args=[dmix, wout, ya, ys, proj, proj],
                 in_specs=[pl.BlockSpec((tm, D), lambda i, n: (i, 0)), pl.BlockSpec((D, D), lambda i, n: (0, 0)), out, out,
                           pl.BlockSpec((tm, tn), lambda i, n: (i, ga + n)), pl.BlockSpec((tm, tn), lambda i, n: (i, gs + n))],
                 out_specs=[out] * 4, out_shape=[shp] * 4)[0]


def _att_sg_dout(dya, dys, watt, wsg, oatt, name, comm=()):
    S, D = dya.shape
    nb, _, Db = watt.shape
    tm = _tile(S, 512)

    def body(dya_ref, dys_ref, wa_ref, ws_ref, oa_ref, do_ref, dz_ref, dvec_ref):
        def back(dy_ref, w_ref, rows):
            tot = None
            for j in range(nb):
                part = _nt(dy_ref[:, j * Db:(j + 1) * Db], w_ref[j, rows, :])
                tot = part if tot is None else tot + part
            return tot

        dov = back(dya_ref, wa_ref, slice(0, GROUP_W))
        do_ref[...] = dov
        for c0 in range(0, SG_W, GROUP_W):
            dz_ref[:, c0:c0 + GROUP_W] = back(dys_ref, ws_ref, slice(c0, c0 + GROUP_W)).astype(BF)
        prod = dov * oa_ref[...].astype(F32)
        for hh in range(HEADS_PER_GROUP):
            cols = slice(hh * HEAD_DIM, (hh + 1) * HEAD_DIM)
            dvec_ref[:, cols] = jnp.broadcast_to(jnp.sum(prod[:, cols], axis=-1, keepdims=True), (tm, HEAD_DIM))

    row = pl.BlockSpec((tm, D), lambda i: (i, 0))
    att = pl.BlockSpec((tm, GROUP_W), lambda i: (i, 0))
    return _call(body, grid=(S // tm,), name=name, args=[dya, dys, watt, wsg, oatt], comm=comm,
                 in_specs=[row, row, pl.BlockSpec((nb, GROUP_W, Db), lambda i: (0, 0, 0)), pl.BlockSpec((nb, SG_W, Db), lambda i: (0, 0, 0)), att],
                 out_specs=[att, pl.BlockSpec((tm, SG_W), lambda i: (i, 0)), att],
                 out_shape=[jax.ShapeDtypeStruct((S, GROUP_W), F32), jax.ShapeDtypeStruct((S, SG_W), BF), jax.ShapeDtypeStruct((S, GROUP_W), F32)])[0]


def _small_allreduce(pack, name):
    R = pack.shape[0]

    def body(p_ref, o_ref, gath, send, recv):
        x, y, c = _place()
        me = 4 * x + 2 * y + c
        gath[me] = p_ref[...]
        copies = []
        for r in range(1, N_DEV):
            px, py, pc = _flip(x, r & 4), _flip(y, r & 2), _flip(c, r & 1)
            peer = 4 * px + 2 * py + pc
            mk = lambda dst: pltpu.make_async_remote_copy(src_ref=p_ref, dst_ref=dst, send_sem=send.at[r - 1], recv_sem=recv.at[r - 1],
                                                          device_id=(px, py, pc), device_id_type=MESH)
            snd = mk(gath.at[me])
            snd.start()
            copies.append((snd, mk(gath.at[peer])))
        for snd, rcv in copies:
            rcv.wait_recv()
            snd.wait_send()
        acc = gath[0]
        for s in range(1, N_DEV):
            acc = acc + gath[s]
        o_ref[...] = acc

    vm = pl.BlockSpec(memory_space=pltpu.VMEM)
    return pl.pallas_call(
        body, name=name, in_specs=[vm], out_specs=vm, out_shape=jax.ShapeDtypeStruct(pack.shape, F32),
        scratch_shapes=[pltpu.VMEM((N_DEV, R, 128), F32), pltpu.SemaphoreType.DMA((7,)), pltpu.SemaphoreType.DMA((7,))],
        compiler_params=pltpu.CompilerParams(vmem_limit_bytes=VMEM_LIMIT),
    )(pack)


def _row_tile(R, C, elems=262144):
    tr = R
    while tr * C > elems and tr % 32 == 0:
        tr //= 2
    return tr


def _pair_add(parts, other, name):
    _, R, C = parts.shape
    tr = _row_tile(R, C, 1048576)

    def body(c_ref, p_ref, o_ref, s_ref):
        s_ref[0] = (p_ref[0].astype(F32) + o_ref[0].astype(F32)).astype(BF)

    core = lax.axis_index("c").astype(jnp.int32).reshape(1)
    return pl.pallas_call(
        body, name=name,
        grid_spec=pltpu.PrefetchScalarGridSpec(
            num_scalar_prefetch=1, grid=(N_CHIP, R // tr),
            in_specs=[pl.BlockSpec((1, tr, C), lambda q, i, c: (2 * q + c[0], i, 0)), pl.BlockSpec((1, tr, C), lambda q, i, c: (q, i, 0))],
            out_specs=pl.BlockSpec((1, tr, C), lambda q, i, c: (q, i, 0))),
        out_shape=jax.ShapeDtypeStruct((N_CHIP, R, C), BF),
        compiler_params=pltpu.CompilerParams(dimension_semantics=("arbitrary", "arbitrary"), vmem_limit_bytes=VMEM_LIMIT),
    )(core, parts, other)


def _adamw(parts, w, m, v, name):
    ns, R, C = parts.shape
    tr = _row_tile(R, C, 524288)
    c1 = 1.0 - ADAM_B1 ** ADAM_STEP
    c2 = 1.0 - ADAM_B2 ** ADAM_STEP

    def body(p_ref, w_ref, m_ref, v_ref, g_ref, d_ref, nm_ref, nv_ref):
        g = p_ref[0].astype(F32)
        for s in range(1, ns):
            g = g + p_ref[s].astype(F32)
        mn = ADAM_B1 * m_ref[...] + (1.0 - ADAM_B1) * g
        vn = ADAM_B2 * v_ref[...] + (1.0 - ADAM_B2) * (g * g)
        g_ref[...] = g
        nm_ref[...] = mn
        nv_ref[...] = vn
        d_ref[...] = -ADAM_LR * ((mn / c1) / (jnp.sqrt(vn / c2) + ADAM_EPS) + ADAM_WD * w_ref[...])

    row = pl.BlockSpec((tr, C), lambda i: (i, 0))
    shp = jax.ShapeDtypeStruct((R, C), F32)
    return _call(body, grid=(R // tr,), name=name, args=[parts, w, m, v],
                 in_specs=[pl.BlockSpec((ns, tr, C), lambda i: (0, i, 0)), row, row, row], out_specs=[row] * 4, out_shape=[shp] * 4)[0]


def _pad_rows(a, rows):
    return jnp.pad(a, ((0, rows - a.shape[0]), (0, 0)))


def kernel(x, ffn1_norm, ffn1_w_gate, ffn1_w_up, ffn1_w_down, mix_norm, w_in, sg_ln_g, sg_ln_b, sg_w, sg_b, w_att_out, w_sg_out, w_out, ffn2_norm, ffn2_w_gate, ffn2_w_up, ffn2_w_down, final_norm, loss_target, m_ffn1_norm, m_ffn1_w_gate, m_ffn1_w_up, m_ffn1_w_down, m_mix_norm, m_w_in, m_sg_ln_g, m_sg_ln_b, m_sg_w, m_sg_b, m_w_att_out, m_w_sg_out, m_w_out, m_ffn2_norm, m_ffn2_w_gate, m_ffn2_w_up, m_ffn2_w_down, m_final_norm, v_ffn1_norm, v_ffn1_w_gate, v_ffn1_w_up, v_ffn1_w_down, v_mix_norm, v_w_in, v_sg_ln_g, v_sg_ln_b, v_sg_w, v_sg_b, v_w_att_out, v_w_sg_out, v_w_out, v_ffn2_norm, v_ffn2_w_gate, v_ffn2_w_up, v_ffn2_w_down, v_final_norm):
    S, D = x.shape[1], x.shape[2]
    Pb = w_in.shape[2]
    P = N_DEV * Pb
    assert P == GA_OFF + 2 * D and D % (N_DEV * 128) == 0 and S % (BLK * DILATIONS[-1]) == 0
    xs, tgt = x[0], loss_target[0]

    sharded = dict(ffn1_w_gate=ffn1_w_gate, ffn1_w_up=ffn1_w_up, ffn1_w_down=ffn1_w_down, w_in=w_in, w_att_out=w_att_out,
                   w_sg_out=w_sg_out, w_out=w_out, ffn2_w_gate=ffn2_w_gate, ffn2_w_up=ffn2_w_up, ffn2_w_down=ffn2_w_down)
    sb = {n: w[0].astype(BF) for n, w in sharded.items()}

    (h1,), ((wg1,),) = _rms_fwd(xs, ffn1_norm, "rms1", comm=[_Gather([sb["ffn1_w_gate"]], 1.0, 1.0)])
    win_top, win_bot = sb["w_in"][: D // 4], sb["w_in"][D // 4:]
    (g1,), ((wu1,),) = _ffn_gate(h1, wg1, "ffn1_gate", comm=[_Gather([sb["ffn1_w_up"]], 0.9, 0.55)])
    (u1, a1), ((wd1, win8a),) = _ffn_up_act(h1, wu1, g1, "ffn1_up_act", comm=[_Gather([sb["ffn1_w_down"], win_top], 1.0, 0.6)])
    (x1, h2), ((win8b,),) = _ffn_down_norm(a1, wd1, xs, mix_norm, "ffn1_down", comm=[_Gather([win_bot], 0.9, 0.55)])
    win = jnp.concatenate([w8.transpose(1, 0, 2).reshape(w8.shape[1], P) for w8 in (win8a, win8b)], axis=0)
    tm_proj = _tile(S, 1024)
    (qkv,), ((wg2,),) = _mm_nn(h2, win, tm_proj, 512, 0, U_OFF, F32, "proj_qkv", comm=[_Gather([sb["ffn2_w_gate"]], 1.0, 0.7)])
    (rest,), ((wu2,),) = _mm_nn(h2, win, tm_proj, 512, U_OFF, P, BF, "proj_rest", comm=[_Gather([sb["ffn2_w_up"]], 0.75, 0.45)])
    tabs = _rope_tables(S)
    rides = [[_Gather([sb["w_att_out"], sb["w_sg_out"]], 0.9, 0.5)], [_Gather([sb["w_out"]], 0.85, 0.45)], []]
    os, lses, late = [], [], []
    for gi, d in enumerate(DILATIONS):
        (o, l), got_here = _att_fwd(qkv, tabs, gi, d, f"att_fwd{gi}", comm=rides[gi])
        late += [w for g in got_here for w in g]
        os.append(o)
        lses.append(l)
    watt, wsg, wout8 = late
    wout = wout8.reshape(D, D)
    oatt, lse = _att_combine(os, lses, "att_combine")
    sgw = sg_w[0]
    sgbT = jnp.pad(sg_b[0].T, ((0, 0), (0, BLK - SG_GROUPS)))
    z = _sg_fwd(rest, sgw, sgbT, sg_ln_g, sg_ln_b, "sg_fwd")
    (ya, ys, merged), _ = _gate_merge(oatt, z, watt, wsg, rest, "gate_merge")
    x2, h3 = _mix_out(merged, wout, x1, ffn2_norm, "mix_out")
    (g3, u3, a3), ((wd2,),) = _ffn_up(h3, wg2, wu2, "ffn2_up", comm=[_Gather([sb["ffn2_w_down"]], 0.6, 0.35)])
    dx3, dyb3, d_final, loss_part = _ffn_down_loss(a3, wd2, x2, final_norm.reshape(1, D), tgt, "ffn2_down_loss")

    Fb = wg2.shape[2]
    Db = watt.shape[2]
    p_pad = -(-P // PROJ_TK) * PROJ_TK
    winT = jnp.concatenate([w8.transpose(0, 2, 1).reshape(P, w8.shape[1]) for w8 in (win8a, win8b)], axis=1)
    winT = jnp.pad(winT, ((0, p_pad - P), (0, 0)))
    (dg3, du3), _ = _ffn_bwd_act(dyb3, wd2, g3, u3, "ffn2_bwd_act")
    (dwd2,), _ = _ffn_dwd(a3, dyb3, "ffn2_dwd")
    (dwg2, dwu2), _ = _ffn_dwgu(h3, dg3, du3, "ffn2_dwgu")
    ffn2_parts = [dwd2, dwg2, dwu2]
    (dx2, dmixb, d_ffn2n), (ffn2_other,) = _dh_rms_bwd([(dg3, wg2), (du3, wu2)], True, Fb, x2, ffn2_norm, dx3, 1.0, "ffn2_dh",
                                                     comm=[_Swap(ffn2_parts)])
    ffn2_sums = [_pair_add(p, o, f"pair_ffn2_{i}") for i, (p, o) in enumerate(zip(ffn2_parts, ffn2_other))]

    dya, dys, dga, dgs = _mix_bwd_gate(dmixb, wout, ya, ys, rest, "mix_bwd_gate")
    (dwout,), _ = _mm_tn(merged, dmixb, _tile(D, 1024), _tile(D, 1024), _tile(S, 1024), False, "dw_out")
    do, dz, dvec = _att_sg_dout(dya, dys, watt, wsg, oatt, "att_sg_dout")
    (dwatt,), _ = _mm_tn(oatt, dya, GROUP_W, 2 * Db, _tile(S, 1024), Db, "dw_att")
    (dwsg,), _ = _mm_tn(z, dys, SG_W, 2 * Db, _tile(S, 1024), Db, "dw_sg")
    mix_parts = [dwout.reshape(N_DEV, D // N_DEV, D), dwatt, dwsg]
    du, dvs, d_sgw, d_sgbT, d_lng, d_lnb = _sg_bwd(rest, dz, sgw, sgbT, sg_ln_g, sg_ln_b, "sg_bwd")
    dqs, dks, dvs_att, ffn2_got = [], [], [], []
    for gi, d in enumerate(DILATIONS):
        ride = [_Ici([ffn2_sums[0]])] if gi == 2 else []
        (dq, dk, dv), got_here = _att_bwd(qkv, tabs, do, lse, dvec, gi, d, f"att_bwd{gi}", comm=ride)
        ffn2_got += [g[0] for g in got_here]
        dqs.append(dq)
        dks.append(dk)
        dvs_att.append(dv)
    dproj = jnp.concatenate([t.astype(BF) for t in dqs + dks + dvs_att] + [du, dvs, dga, dgs, jnp.zeros((S, p_pad - P), BF)], axis=1)
    (dx1, dyb1, d_mixn), (ffn2_rest, mix_other) = _dh_rms_bwd([(dproj, winT)], False, PROJ_TK, x1, mix_norm, dx2, 0.5, "proj_dh",
                                                            comm=[_Ici(ffn2_sums[1:]), _Swap(mix_parts)])
    ffn2_got += ffn2_rest
    mix_sums = [_pair_add(p, o, f"pair_mix_{i}") for i, (p, o) in enumerate(zip(mix_parts, mix_other))]
    (dwd1,), (mix_got,) = _ffn_dwd(a1, dyb1, "ffn1_dwd", comm=[_Ici(mix_sums)])
    rows = lambda a: a.reshape(-1, 128)
    pad8 = lambda a: _pad_rows(a, -(-a.shape[0] // 8) * 8)
    small = [("sg_w", rows(d_sgw), sg_w, m_sg_w, v_sg_w), ("mix_norm", rows(d_mixn), mix_norm, m_mix_norm, v_mix_norm),
             ("ffn2_norm", rows(d_ffn2n), ffn2_norm, m_ffn2_norm, v_ffn2_norm), ("final_norm", rows(d_final), final_norm, m_final_norm, v_final_norm),
             ("sg_ln_g", rows(d_lng), sg_ln_g, m_sg_ln_g, v_sg_ln_g), ("sg_ln_b", rows(d_lnb), sg_ln_b, m_sg_ln_b, v_sg_ln_b),
             ("sg_b", d_sgbT[:, :SG_GROUPS].T, sg_b, m_sg_b, v_sg_b)]
    gpack = jnp.concatenate([pad8(g) for _, g, _, _, _ in small] + [pad8(loss_part)], axis=0)
    (dwin,), ((wd1_other,), (gpacks,)) = _mm_tn(h2, dproj, D, 512, _tile(S, 2048), False, "dw_in", ncols=P,
                                              comm=[_Swap([dwd1]), _Spread([gpack])])
    dwin = dwin.reshape(D, N_DEV, Pb).transpose(1, 0, 2)
    wd1_sum = _pair_add(dwd1, wd1_other, "pair_wd1")
    (dg1, du1), ((wd1_got,), (win_other,)) = _ffn_bwd_act(dyb1, wd1, g1, u1, "ffn1_bwd_act", comm=[_Ici([wd1_sum]), _Swap([dwin])])
    win_sum = _pair_add(dwin, win_other, "pair_win")
    (dwg1, dwu1), ((win_got,),) = _ffn_dwgu(h1, dg1, du1, "ffn1_dwgu", comm=[_Ici([win_sum])])
    gu_parts = [dwg1, dwu1]
    gu_other = _comm_only(_Swap(gu_parts), "swap_gu1")
    gu_sums = [_pair_add(p, o, f"pair_gu1_{i}") for i, (p, o) in enumerate(zip(gu_parts, gu_other))]
    (dx0, _, d_ffn1n), (gu_got,) = _dh_rms_bwd([(dg1, wg1), (du1, wu1)], True, Fb, xs, ffn1_norm, dx1, 1.0, "ffn1_dh",
                                               comm=[_Ici(gu_sums)])

    got = dict(ffn2_w_down=ffn2_got[0], ffn2_w_gate=ffn2_got[1], ffn2_w_up=ffn2_got[2], w_out=mix_got[0], w_att_out=mix_got[1],
               w_sg_out=mix_got[2], w_in=win_got, ffn1_w_gate=gu_got[0], ffn1_w_up=gu_got[1], ffn1_w_down=wd1_got)
    moments = dict(ffn1_w_gate=(m_ffn1_w_gate, v_ffn1_w_gate), ffn1_w_up=(m_ffn1_w_up, v_ffn1_w_up),
                   ffn1_w_down=(m_ffn1_w_down, v_ffn1_w_down), w_in=(m_w_in, v_w_in), w_att_out=(m_w_att_out, v_w_att_out),
                   w_sg_out=(m_w_sg_out, v_w_sg_out), w_out=(m_w_out, v_w_out), ffn2_w_gate=(m_ffn2_w_gate, v_ffn2_w_gate),
                   ffn2_w_up=(m_ffn2_w_up, v_ffn2_w_up), ffn2_w_down=(m_ffn2_w_down, v_ffn2_w_down))
    res = {}
    for n in sharded:
        mm, vv = moments[n]
        outs = _adamw(got[n], sharded[n][0], mm[0], vv[0], "adamw_" + n)
        res[n] = [o[None] for o in outs]

    zero8 = jnp.zeros((8, 128), F32)
    wpack = jnp.concatenate([pad8(rows(w)) for _, _, w, _, _ in small] + [zero8], axis=0)
    mpack = jnp.concatenate([pad8(rows(m)) for _, _, _, m, _ in small] + [zero8], axis=0)
    vpack = jnp.concatenate([pad8(rows(v)) for _, _, _, _, v in small] + [zero8], axis=0)
    packs = _adamw(gpacks, wpack, mpack, vpack, "adamw_small")
    off = 0
    for n, g, w, _, _ in small:
        r = g.shape[0]
        res[n] = [p[off:off + r].reshape(w.shape) for p in packs]
        off += -(-r // 8) * 8
    loss = packs[0][off, 0]
    g_first = _small_allreduce(rows(d_ffn1n), "allreduce_ffn1_norm")
    res["ffn1_norm"] = [p.reshape(ffn1_norm.shape) for p in
                        _adamw(g_first[None], rows(ffn1_norm), rows(m_ffn1_norm), rows(v_ffn1_norm), "adamw_ffn1_norm")]

    order = ["ffn1_norm", "ffn1_w_gate", "ffn1_w_up", "ffn1_w_down", "mix_norm", "w_in", "sg_ln_g", "sg_ln_b", "sg_w", "sg_b",
             "w_att_out", "w_sg_out", "w_out", "ffn2_norm", "ffn2_w_gate", "ffn2_w_up", "ffn2_w_down", "final_norm"]
    return (loss, dx0[None], *[res[n][0] for n in order], *[res[n][1] for n in order], *[res[n][2] for n in order],
            *[res[n][3] for n in order])
```

```python
import math

import jax
import jax.numpy as jnp
from jax import lax
from jax.experimental import pallas as pl
from jax.experimental.pallas import tpu as pltpu

BF = jnp.bfloat16
F32 = jnp.float32
MESH = pl.DeviceIdType.MESH
N_DEV = 8
N_CHIP = 4

HEAD_DIM = 128
HEADS_PER_GROUP = 4
GROUP_W = HEADS_PER_GROUP * HEAD_DIM
DILATIONS = (1, 4, 16)
ATT_W = len(DILATIONS) * GROUP_W
SG_W = 1536
SG_GROUPS = 12
BLK = 128
ROPE_DIM = 32
ROPE_THETA = 500000.0
NORM_EPS = 1e-6
LN_EPS = 1e-5
Q_OFF, K_OFF, V_OFF, U_OFF, VS_OFF, GA_OFF = 0, ATT_W, 2 * ATT_W, 3 * ATT_W, 3 * ATT_W + SG_W, 3 * ATT_W + 2 * SG_W

ADAM_LR, ADAM_B1, ADAM_B2, ADAM_EPS, ADAM_WD, ADAM_STEP = 0.001, 0.9, 0.999, 1e-08, 0.01, 10

VMEM_LIMIT = 56 * 1024 * 1024
NEG = -1e30
ANY = pl.BlockSpec(memory_space=pl.ANY)
EPI_ROWS = 128
ACC_COLS = 512
FFN_PAIR = 2
PROJ_TK = 1536
R_U, R_VS, R_GA = 0, SG_W, 2 * SG_W


def _once(shape, index_map):
    return pl.BlockSpec(shape, index_map, pipeline_mode=pl.Buffered(1))


def _tile(n, pref):
    t = min(n, pref)
    while n % t:
        t //= 2
    return t


def _nt(a, b):
    return lax.dot_general(a, b, (((1,), (1,)), ((), ())), preferred_element_type=F32)


def _tn(a, b):
    return lax.dot_general(a, b, (((0,), (0,)), ((), ())), preferred_element_type=F32)


def _nn(a, b):
    return jnp.dot(a, b, preferred_element_type=F32)


def _acc_dots(acc_ref, terms, transposed_rhs=False):
    n = acc_ref.shape[1]
    width = min(n, ACC_COLS)
    for c0 in range(0, n, width):
        cols = slice(c0, c0 + width)
        tot = None
        for lhs, rhs in terms:
            part = _nt(lhs, rhs(cols)) if transposed_rhs else _nn(lhs, rhs(cols))
            tot = part if tot is None else tot + part
        acc_ref[:, cols] += tot


def _gelu(x):
    return 0.5 * x * (1.0 + lax.erf(x * (2.0 ** -0.5)))


def _gelu_grad(x):
    return 0.5 * (1.0 + lax.erf(x * (2.0 ** -0.5))) + x * jnp.exp(-0.5 * x * x) * (1.0 / math.sqrt(2.0 * math.pi))


def _place():
    x, y, c = lax.axis_index("x"), lax.axis_index("y"), lax.axis_index("c")
    return x, y, c


def _flip(v, bit):
    return 1 - v if bit else v


class _Gather:
    def __init__(self, shards, mid_frac=1.0, relay_frac=0.5):
        self.arrays = list(shards)
        self.relay_frac = relay_frac
        self.mid_frac = mid_frac
        nw = len(shards)
        self.out_shape = [jax.ShapeDtypeStruct((N_DEV,) + s.shape, s.dtype) for s in shards]
        self.scratch = [pltpu.SemaphoreType.DMA((nw, 7)), pltpu.SemaphoreType.DMA((nw, 7)), pltpu.SemaphoreType.DMA((nw,))]

    def _parts(self, ins, outs, sems):
        x, y, c = _place()
        send, recv, loc = sems
        south = c == 0
        near = (jnp.where(south, x, 1 - x), jnp.where(south, 1 - y, y), c)
        far = (jnp.where(south, 1 - x, x), jnp.where(south, y, 1 - y), c)
        diag = (1 - x, 1 - y, c)

        def copy(k, s, block, to, src=None):
            dst = outs[k].at[4 * block[0] + 2 * block[1] + block[2]]
            return pltpu.make_async_remote_copy(src_ref=dst if src is None else src, dst_ref=dst, send_sem=send.at[k, s],
                                                recv_sem=recv.at[k, s], device_id=to, device_id_type=MESH)

        def first(k):
            me = (x, y, c)
            return [copy(k, 0, me, (x, y, 1 - c), src=ins[k]), copy(k, 1, me, (1 - x, y, c), src=ins[k]),
                    copy(k, 2, me, (x, 1 - y, c), src=ins[k])]

        def local(k):
            return pltpu.make_async_copy(ins[k], outs[k].at[4 * x + 2 * y + c], loc.at[k])

        return x, y, c, near, far, diag, copy, first, local

    def start(self, ins, outs, sems):
        *_, first, local = self._parts(ins, outs, sems)
        for k in range(len(ins)):
            local(k).start()
            for cp in first(k):
                cp.start()

    def relay(self, ins, outs, sems):
        x, y, c, near, far, _, copy, _, _ = self._parts(ins, outs, sems)
        for k in range(len(ins)):
            copy(k, 2 - c, near, (x, y, c)).wait_recv()
            copy(k, 3, near, far).start()
            copy(k, 5 - c, near, (x, y, 1 - c)).start()

    def mid(self, ins, outs, sems):
        x, y, c, _, far, diag, copy, _, _ = self._parts(ins, outs, sems)
        for k in range(len(ins)):
            copy(k, 1 + c, far, (x, y, c)).wait_recv()
            copy(k, 4 + c, far, (x, y, 1 - c)).start()
            copy(k, 3, diag, (x, y, c)).wait_recv()
            copy(k, 6, diag, (x, y, 1 - c)).start()

    def finish(self, ins, outs, sems):
        x, y, c, near, _, _, copy, first, local = self._parts(ins, outs, sems)
        sib = (x, y, 1 - c)
        for k in range(len(ins)):
            copy(k, 0, sib, (x, y, c)).wait_recv()
            copy(k, 4, (1 - x, y, 1 - c), (x, y, c)).wait_recv()
            copy(k, 5, (x, 1 - y, 1 - c), (x, y, c)).wait_recv()
            copy(k, 6, (1 - x, 1 - y, 1 - c), (x, y, c)).wait_recv()
        for k in range(len(ins)):
            for cp in first(k):
                cp.wait_send()
            for s in (3, 4, 5, 6):
                copy(k, s, near, sib).wait_send()
            local(k).wait()


class _Swap:
    def __init__(self, parts):
        self.arrays = list(parts)
        nw = len(parts)
        self.out_shape = [jax.ShapeDtypeStruct((N_CHIP,) + p.shape[1:], p.dtype) for p in parts]
        self.scratch = [pltpu.SemaphoreType.DMA((nw, N_CHIP)), pltpu.SemaphoreType.DMA((nw, N_CHIP))]

    def _copy(self, ins, outs, sems, k, q):
        x, y, c = _place()
        return pltpu.make_async_remote_copy(src_ref=ins[k].at[2 * q + 1 - c], dst_ref=outs[k].at[q], send_sem=sems[0].at[k, q],
                                            recv_sem=sems[1].at[k, q], device_id=(x, y, 1 - c), device_id_type=MESH)

    mid_frac = None

    def start(self, ins, outs, sems):
        for k in range(len(ins)):
            for q in range(N_CHIP):
                self._copy(ins, outs, sems, k, q).start()

    def finish(self, ins, outs, sems):
        for k in range(len(ins)):
            for q in range(N_CHIP):
                self._copy(ins, outs, sems, k, q).wait()


class _Ici:
    mid_frac = None

    def __init__(self, sums):
        self.arrays = list(sums)
        nw = len(sums)
        self.out_shape = [jax.ShapeDtypeStruct(s.shape, s.dtype) for s in sums]
        self.scratch = [pltpu.SemaphoreType.DMA((nw, 3)), pltpu.SemaphoreType.DMA((nw, 3)), pltpu.SemaphoreType.DMA((nw,))]

    def _copies(self, ins, outs, sems, k):
        x, y, c = _place()
        myq = 2 * x + y
        out = []
        for r in range(1, N_CHIP):
            px, py = _flip(x, r & 2), _flip(y, r & 1)
            pq = 2 * px + py
            mk = lambda dst: pltpu.make_async_remote_copy(src_ref=ins[k].at[pq], dst_ref=dst, send_sem=sems[0].at[k, r - 1],
                                                          recv_sem=sems[1].at[k, r - 1], device_id=(px, py, c), device_id_type=MESH)
            out.append((mk(outs[k].at[myq]), mk(outs[k].at[pq])))
        return out, pltpu.make_async_copy(ins[k].at[myq], outs[k].at[myq], sems[2].at[k])

    def start(self, ins, outs, sems):
        for k in range(len(ins)):
            remote, local = self._copies(ins, outs, sems, k)
            local.start()
            for snd, _ in remote:
                snd.start()

    def finish(self, ins, outs, sems):
        for k in range(len(ins)):
            remote, local = self._copies(ins, outs, sems, k)
            for snd, rcv in remote:
                rcv.wait_recv()
                snd.wait_send()
            local.wait()


class _Spread:
    mid_frac = None

    def __init__(self, arrays):
        self.arrays = list(arrays)
        nw = len(arrays)
        self.out_shape = [jax.ShapeDtypeStruct((N_DEV,) + a.shape, a.dtype) for a in arrays]
        self.scratch = [pltpu.SemaphoreType.DMA((nw, 7)), pltpu.SemaphoreType.DMA((nw, 7)), pltpu.SemaphoreType.DMA((nw,))]

    def _copies(self, ins, outs, sems, k):
        x, y, c = _place()
        me = 4 * x + 2 * y + c
        out = []
        for r in range(1, N_DEV):
            px, py, pc = _flip(x, r & 4), _flip(y, r & 2), _flip(c, r & 1)
            peer = 4 * px + 2 * py + pc
            mk = lambda dst: pltpu.make_async_remote_copy(src_ref=ins[k], dst_ref=dst, send_sem=sems[0].at[k, r - 1],
                                                          recv_sem=sems[1].at[k, r - 1], device_id=(px, py, pc), device_id_type=MESH)
            out.append((mk(outs[k].at[me]), mk(outs[k].at[peer])))
        return out, pltpu.make_async_copy(ins[k], outs[k].at[me], sems[2].at[k])

    def start(self, ins, outs, sems):
        for k in range(len(ins)):
            remote, local = self._copies(ins, outs, sems, k)
            local.start()
            for snd, _ in remote:
                snd.start()

    def finish(self, ins, outs, sems):
        for k in range(len(ins)):
            remote, local = self._copies(ins, outs, sems, k)
            for snd, rcv in remote:
                rcv.wait_recv()
                snd.wait_send()
            local.wait()


def _call(body, *, grid, in_specs, out_specs, out_shape, name, args, scratch=(), comm=()):
    comm = list(comm)
    n_in, n_out, n_scr = len(in_specs), len(out_specs), len(scratch)
    total = math.prod(grid) if grid else 1

    def wrapped(*refs):
        p = n_in
        cin = []
        for cm in comm:
            cin.append(refs[p:p + len(cm.arrays)])
            p += len(cm.arrays)
        own_out = refs[p:p + n_out]
        p += n_out
        cout = []
        for cm in comm:
            cout.append(refs[p:p + len(cm.arrays)])
            p += len(cm.arrays)
        own_scr = refs[p:p + n_scr]
        p += n_scr
        csem = []
        for cm in comm:
            csem.append(refs[p:p + len(cm.scratch)])
            p += len(cm.scratch)
        step = 0
        for axis, g in enumerate(grid):
            step = step * g + pl.program_id(axis)

        def at(when, what):
            if total == 1:
                what()
            else:
                pl.when(step == when)(what)

        def starts():
            for cm, i, o, s in zip(comm, cin, cout, csem):
                cm.start(i, o, s)

        def finishes():
            for cm, i, o, s in zip(comm, cin, cout, csem):
                cm.finish(i, o, s)

        if comm:
            at(0, starts)
        if body is not None:
            body(*refs[:n_in], *own_out, *own_scr)
        for cm, i, o, s in zip(comm, cin, cout, csem):
            if cm.mid_frac is not None:
                at(min(total - 1, int(total * cm.relay_frac)), lambda cm=cm, i=i, o=o, s=s: cm.relay(i, o, s))
                at(min(total - 1, int(total * cm.mid_frac)), lambda cm=cm, i=i, o=o, s=s: cm.mid(i, o, s))
        if comm:
            at(total - 1, finishes)

    kw = dict(grid=tuple(grid)) if grid else {}
    outs = pl.pallas_call(
        wrapped, name=name, **kw,
        in_specs=list(in_specs) + [ANY for cm in comm for _ in cm.arrays],
        out_specs=list(out_specs) + [ANY for cm in comm for _ in cm.arrays],
        out_shape=list(out_shape) + [s for cm in comm for s in cm.out_shape],
        scratch_shapes=list(scratch) + [s for cm in comm for s in cm.scratch],
        compiler_params=pltpu.CompilerParams(dimension_semantics=("arbitrary",) * len(grid), vmem_limit_bytes=VMEM_LIMIT),
    )(*args, *[a for cm in comm for a in cm.arrays])
    own, p, per = list(outs[:n_out]), n_out, []
    for cm in comm:
        per.append(list(outs[p:p + len(cm.arrays)]))
        p += len(cm.arrays)
    return own, per


def _comm_only(cm, name):
    return _call(None, grid=(), in_specs=[], out_specs=[], out_shape=[], name=name, args=[], comm=[cm])[1][0]


def _rms_fwd(x, g, name, comm=()):
    S, D = x.shape
    tm = _tile(S, 512)

    def body(x_ref, g_ref, o_ref):
        xv = x_ref[...]
        r = lax.rsqrt(jnp.mean(xv * xv, axis=-1, keepdims=True) + NORM_EPS)
        o_ref[...] = (xv * r * g_ref[...]).astype(BF)

    return _call(body, grid=(S // tm,), name=name, args=[x, g], comm=comm,
                 in_specs=[pl.BlockSpec((tm, D), lambda i: (i, 0)), pl.BlockSpec((1, D), lambda i: (0, 0))],
                 out_specs=[pl.BlockSpec((tm, D), lambda i: (i, 0))], out_shape=[jax.ShapeDtypeStruct((S, D), BF)])


def _ffn_up(h, wg, wu, name, comm=()):
    S, D = h.shape
    nb, Fb, _ = wg.shape
    tm = _tile(S, 512)

    def body(h_ref, wg_ref, wu_ref, g_ref, u_ref, a_ref):
        hv = h_ref[...]
        g = _nt(hv, wg_ref[0])
        u = _nt(hv, wu_ref[0])
        g_ref[0] = g.astype(BF)
        u_ref[0] = u.astype(BF)
        a_ref[0] = (g * jax.nn.sigmoid(g) * u).astype(BF)

    act = pl.BlockSpec((1, tm, Fb), lambda j, i: (j, i, 0))
    w = pl.BlockSpec((1, Fb, D), lambda j, i: (j, 0, 0))
    shp = jax.ShapeDtypeStruct((nb, S, Fb), BF)
    return _call(body, grid=(nb, S // tm), name=name, args=[h, wg, wu], comm=comm,
                 in_specs=[pl.BlockSpec((tm, D), lambda j, i: (i, 0)), w, w], out_specs=[act, act, act], out_shape=[shp, shp, shp])


def _ffn_gate(h, wg, name, comm=()):
    S, D = h.shape
    nb, Fb, _ = wg.shape
    tm = _tile(S, 512)

    def body(h_ref, wg_ref, g_ref):
        g_ref[0] = _nt(h_ref[...], wg_ref[0]).astype(BF)

    act = pl.BlockSpec((1, tm, Fb), lambda j, i: (j, i, 0))
    return _call(body, grid=(nb, S // tm), name=name, args=[h, wg], comm=comm,
                 in_specs=[pl.BlockSpec((tm, D), lambda j, i: (i, 0)), pl.BlockSpec((1, Fb, D), lambda j, i: (j, 0, 0))],
                 out_specs=[act], out_shape=[jax.ShapeDtypeStruct((nb, S, Fb), BF)])


def _ffn_up_act(h, wu, g, name, comm=()):
    S, D = h.shape
    nb, Fb, _ = wu.shape
    tm = _tile(S, 512)

    def body(h_ref, wu_ref, g_ref, u_ref, a_ref):
        u = _nt(h_ref[...], wu_ref[0])
        gv = g_ref[0].astype(F32)
        u_ref[0] = u.astype(BF)
        a_ref[0] = (gv * jax.nn.sigmoid(gv) * u).astype(BF)

    act = pl.BlockSpec((1, tm, Fb), lambda j, i: (j, i, 0))
    shp = jax.ShapeDtypeStruct((nb, S, Fb), BF)
    return _call(body, grid=(nb, S // tm), name=name, args=[h, wu, g], comm=comm,
                 in_specs=[pl.BlockSpec((tm, D), lambda j, i: (i, 0)), pl.BlockSpec((1, Fb, D), lambda j, i: (j, 0, 0)), act],
                 out_specs=[act, act], out_shape=[shp, shp])


def _ffn_down_norm(a, wd, x, gn, name, comm=()):
    nb, S, Fb = a.shape
    D = wd.shape[2]
    tm = _tile(S, 512)

    nj = nb // FFN_PAIR

    def body(a_ref, wd_ref, x_ref, gn_ref, xo_ref, hn_ref, acc_ref):
        j = pl.program_id(1)

        @pl.when(j == 0)
        def _():
            acc_ref[...] = jnp.zeros_like(acc_ref)

        _acc_dots(acc_ref, [(a_ref[b], lambda cols, b=b: wd_ref[b, :, cols]) for b in range(FFN_PAIR)])

        @pl.when(j == nj - 1)
        def _():
            def chunk(t, carry):
                rows = pl.ds(pl.multiple_of(t * EPI_ROWS, EPI_ROWS), EPI_ROWS)
                xo = x_ref[rows, :] + 0.5 * acc_ref[rows, :]
                r = lax.rsqrt(jnp.mean(xo * xo, axis=-1, keepdims=True) + NORM_EPS)
                xo_ref[rows, :] = xo
                hn_ref[rows, :] = (xo * r * gn_ref[...]).astype(BF)
                return carry

            lax.fori_loop(0, tm // EPI_ROWS, chunk, 0)

    row = pl.BlockSpec((tm, D), lambda i, j: (i, 0))
    return _call(body, grid=(S // tm, nj), name=name, args=[a, wd, x, gn], comm=comm,
                 in_specs=[pl.BlockSpec((FFN_PAIR, tm, Fb), lambda i, j: (j, i, 0)), pl.BlockSpec((FFN_PAIR, Fb, D), lambda i, j: (j, 0, 0)),
                           _once((tm, D), lambda i, j: (i, 0)), pl.BlockSpec((1, D), lambda i, j: (0, 0))],
                 out_specs=[row, row], out_shape=[jax.ShapeDtypeStruct((S, D), F32), jax.ShapeDtypeStruct((S, D), BF)],
                 scratch=[pltpu.VMEM((tm, D), F32)])


def _ffn_down_loss(a, wd, x, gf, tgt, name):
    nb, S, Fb = a.shape
    D = wd.shape[2]
    tm = _tile(S, 512)

    nj = nb // FFN_PAIR

    def body(a_ref, wd_ref, x_ref, gf_ref, t_ref, dx_ref, dxb_ref, dgf_ref, loss_ref, acc_ref):
        i, j = pl.program_id(0), pl.program_id(1)

        @pl.when(j == 0)
        def _():
            acc_ref[...] = jnp.zeros_like(acc_ref)

        _acc_dots(acc_ref, [(a_ref[b], lambda cols, b=b: wd_ref[b, :, cols]) for b in range(FFN_PAIR)])

        @pl.when((j == nj - 1) & (i == 0))
        def _():
            dgf_ref[...] = jnp.zeros_like(dgf_ref)
            loss_ref[...] = jnp.zeros_like(loss_ref)

        @pl.when(j == nj - 1)
        def _():
            def chunk(t, carry):
                rows = pl.ds(pl.multiple_of(t * EPI_ROWS, EPI_ROWS), EPI_ROWS)
                xo = x_ref[rows, :] + 0.5 * acc_ref[rows, :]
                r = lax.rsqrt(jnp.mean(xo * xo, axis=-1, keepdims=True) + NORM_EPS)
                xh = xo * r
                gf = gf_ref[...]
                e = xh * gf - t_ref[rows, :]
                loss_ref[...] += jnp.sum(jnp.mean(e * e, axis=-1, keepdims=True), axis=0, keepdims=True) * 0.5
                dy = e * (1.0 / D)
                dgf_ref[...] += jnp.sum(dy * xh, axis=0, keepdims=True)
                dxh = dy * gf
                dx = r * (dxh - xh * jnp.mean(dxh * xh, axis=-1, keepdims=True))
                dx_ref[rows, :] = dx
                dxb_ref[rows, :] = (0.5 * dx).astype(BF)
                return carry

            lax.fori_loop(0, tm // EPI_ROWS, chunk, 0)

    row = pl.BlockSpec((tm, D), lambda i, j: (i, 0))
    once = _once((tm, D), lambda i, j: (i, 0))
    vec = pl.BlockSpec((1, D), lambda i, j: (0, 0))
    return _call(body, grid=(S // tm, nj), name=name, args=[a, wd, x, gf, tgt],
                 in_specs=[pl.BlockSpec((FFN_PAIR, tm, Fb), lambda i, j: (j, i, 0)), pl.BlockSpec((FFN_PAIR, Fb, D), lambda i, j: (j, 0, 0)),
                           once, vec, once],
                 out_specs=[row, row, vec, pl.BlockSpec((1, 128), lambda i, j: (0, 0))],
                 out_shape=[jax.ShapeDtypeStruct((S, D), F32), jax.ShapeDtypeStruct((S, D), BF), jax.ShapeDtypeStruct((1, D), F32),
                            jax.ShapeDtypeStruct((1, 128), F32)],
                 scratch=[pltpu.VMEM((tm, D), F32)])[0]


def _ffn_bwd_act(dyb, wd, g, u, name, comm=()):
    S, D = dyb.shape
    nb, Fb, _ = wd.shape
    tm = _tile(S, 512)

    def body(dy_ref, wd_ref, g_ref, u_ref, dg_ref, du_ref):
        da = _nt(dy_ref[...], wd_ref[0])
        gv = g_ref[0].astype(F32)
        uv = u_ref[0].astype(F32)
        sg = jax.nn.sigmoid(gv)
        du_ref[0] = (da * gv * sg).astype(BF)
        dg_ref[0] = (da * uv * sg * (1.0 + gv * (1.0 - sg))).astype(BF)

    act = pl.BlockSpec((1, tm, Fb), lambda j, i: (j, i, 0))
    shp = jax.ShapeDtypeStruct((nb, S, Fb), BF)
    return _call(body, grid=(nb, S // tm), name=name, args=[dyb, wd, g, u], comm=comm,
                 in_specs=[pl.BlockSpec((tm, D), lambda j, i: (i, 0)), pl.BlockSpec((1, Fb, D), lambda j, i: (j, 0, 0)), act, act],
                 out_specs=[act, act], out_shape=[shp, shp])


def _ffn_dwd(a, dyb, name, comm=()):
    nb, S, Fb = a.shape
    D = dyb.shape[1]
    ts = _tile(S, 512)
    ns = S // ts

    def body(a_ref, dy_ref, o_ref, acc_ref):
        s = pl.program_id(1)

        @pl.when(s == 0)
        def _():
            acc_ref[...] = jnp.zeros_like(acc_ref)

        acc_ref[...] += _tn(a_ref[0], dy_ref[...])

        @pl.when(s == ns - 1)
        def _():
            o_ref[0] = acc_ref[...].astype(BF)

    return _call(body, grid=(nb, ns), name=name, args=[a, dyb], comm=comm,
                 in_specs=[pl.BlockSpec((1, ts, Fb), lambda j, s: (j, s, 0)), pl.BlockSpec((ts, D), lambda j, s: (s, 0))],
                 out_specs=[pl.BlockSpec((1, Fb, D), lambda j, s: (j, 0, 0))], out_shape=[jax.ShapeDtypeStruct((nb, Fb, D), BF)],
                 scratch=[pltpu.VMEM((Fb, D), F32)])


def _ffn_dwgu(h, dg, du, name, comm=()):
    S, D = h.shape
    nb, _, Fb = dg.shape
    ts = _tile(S, 512)
    ns = S // ts

    def body(h_ref, dg_ref, du_ref, og_ref, ou_ref, accg_ref, accu_ref):
        s = pl.program_id(1)

        @pl.when(s == 0)
        def _():
            accg_ref[...] = jnp.zeros_like(accg_ref)
            accu_ref[...] = jnp.zeros_like(accu_ref)

        hv = h_ref[...]
        accg_ref[...] += _tn(dg_ref[0], hv)
        accu_ref[...] += _tn(du_ref[0], hv)

        @pl.when(s == ns - 1)
        def _():
            og_ref[0] = accg_ref[...].astype(BF)
            ou_ref[0] = accu_ref[...].astype(BF)

    act = pl.BlockSpec((1, ts, Fb), lambda j, s: (j, s, 0))
    out = pl.BlockSpec((1, Fb, D), lambda j, s: (j, 0, 0))
    shp = jax.ShapeDtypeStruct((nb, Fb, D), BF)
    return _call(body, grid=(nb, ns), name=name, args=[h, dg, du], comm=comm,
                 in_specs=[pl.BlockSpec((ts, D), lambda j, s: (s, 0)), act, act], out_specs=[out, out], out_shape=[shp, shp],
                 scratch=[pltpu.VMEM((Fb, D), F32), pltpu.VMEM((Fb, D), F32)])


def _dh_rms_bwd(pairs, blocked, tk, x, gn, dxo, out_scale, name, comm=(), tail=None):
    S, D = x.shape
    nk = pairs[0][0].shape[0] if blocked else pairs[0][0].shape[1] // tk
    tm = _tile(S, 512)
    npair = len(pairs)
    assert blocked or (npair == 1 and tail is not None and nk >= 2)
    nin = 2 * npair + (0 if blocked else 1)

    def body(*refs):
        ins = refs[:nin]
        x_ref, gn_ref, dxo_ref, dx_ref, dxb_ref, dgn_ref, acc_ref = refs[nin:]
        i, k = pl.program_id(0), pl.program_id(1)

        @pl.when(k == 0)
        def _():
            acc_ref[...] = jnp.zeros_like(acc_ref)

        if blocked:
            _acc_dots(acc_ref, [(ins[2 * p][0], lambda cols, r=ins[2 * p + 1]: r[0, :, cols]) for p in range(npair)])
        else:
            @pl.when(k < nk - 1)
            def _():
                _acc_dots(acc_ref, [(ins[0][...], lambda cols: ins[1][:, cols])])

            @pl.when(k == nk - 1)
            def _():
                _acc_dots(acc_ref, [(ins[0][...], lambda cols: ins[2][:, cols])])

        @pl.when((k == nk - 1) & (i == 0))
        def _():
            dgn_ref[...] = jnp.zeros_like(dgn_ref)

        @pl.when(k == nk - 1)
        def _():
            def chunk(t, carry):
                rows = pl.ds(pl.multiple_of(t * EPI_ROWS, EPI_ROWS), EPI_ROWS)
                xv = x_ref[rows, :]
                r = lax.rsqrt(jnp.mean(xv * xv, axis=-1, keepdims=True) + NORM_EPS)
                xh = xv * r
                dh = acc_ref[rows, :]
                dgn_ref[...] += jnp.sum(dh * xh, axis=0, keepdims=True)
                dxh = dh * gn_ref[...]
                dx = dxo_ref[rows, :] + r * (dxh - xh * jnp.mean(dxh * xh, axis=-1, keepdims=True))
                dx_ref[rows, :] = dx
                dxb_ref[rows, :] = (out_scale * dx).astype(BF)
                return carry

            lax.fori_loop(0, tm // EPI_ROWS, chunk, 0)

    if blocked:
        mats = [pl.BlockSpec((1, tm, tk), lambda i, k: (k, i, 0)), pl.BlockSpec((1, tk, D), lambda i, k: (k, 0, 0))] * npair
        flat = [t for pr in pairs for t in pr]
    else:
        mats = [pl.BlockSpec((tm, tk), lambda i, k: (i, k)), pl.BlockSpec((tk, D), lambda i, k: (jnp.minimum(k, nk - 2), 0)),
                pl.BlockSpec((tk, D), lambda i, k: (0, 0))]
        flat = [*pairs[0], tail]
    row = pl.BlockSpec((tm, D), lambda i, k: (i, 0))
    once = _once((tm, D), lambda i, k: (i, 0))
    vec = pl.BlockSpec((1, D), lambda i, k: (0, 0))
    return _call(body, grid=(S // tm, nk), name=name, args=[*flat, x, gn, dxo], comm=comm,
                 in_specs=mats + [once, vec, once], out_specs=[row, row, vec],
                 out_shape=[jax.ShapeDtypeStruct((S, D), F32), jax.ShapeDtypeStruct((S, D), BF), jax.ShapeDtypeStruct((1, D), F32)],
                 scratch=[pltpu.VMEM((tm, D), F32)])


def _mm_nt(a, bT, tm, tn, col0, col1, dtype, name, comm=()):
    M, K = a.shape
    n0, nn = col0 // tn, (col1 - col0) // tn

    def body(a_ref, b_ref, o_ref):
        o_ref[...] = _nt(a_ref[...], b_ref[...]).astype(dtype)

    return _call(body, grid=(nn, M // tm), name=name, args=[a, bT], comm=comm,
                 in_specs=[pl.BlockSpec((tm, K), lambda n, i: (i, 0)), pl.BlockSpec((tn, K), lambda n, i: (n0 + n, 0))],
                 out_specs=[pl.BlockSpec((tm, tn), lambda n, i: (i, n))], out_shape=[jax.ShapeDtypeStruct((M, nn * tn), dtype)])


def _mm_tn(a, b, tm, tn, ts, blocked, name, comm=(), mrows=None):
    S, M = a.shape[0], (a.shape[1] if mrows is None else mrows)
    N = b.shape[1]
    ns = S // ts
    per_tile = tn // blocked if blocked else 0

    def body(a_ref, b_ref, o_ref, acc_ref):
        s = pl.program_id(2)

        @pl.when(s == 0)
        def _():
            acc_ref[...] = jnp.zeros_like(acc_ref)

        acc_ref[...] += _tn(a_ref[...], b_ref[...])

        @pl.when(s == ns - 1)
        def _():
            if blocked:
                for t in range(per_tile):
                    o_ref[t] = acc_ref[:, t * blocked:(t + 1) * blocked].astype(BF)
            else:
                o_ref[...] = acc_ref[...].astype(BF)

    if blocked:
        ospec = pl.BlockSpec((per_tile, tm, blocked), lambda i, n, s: (n, i, 0))
        oshape = jax.ShapeDtypeStruct((N // blocked, M, blocked), BF)
    else:
        ospec = pl.BlockSpec((tm, tn), lambda i, n, s: (i, n))
        oshape = jax.ShapeDtypeStruct((M, N), BF)
    return _call(body, grid=(M // tm, N // tn, ns), name=name, args=[a, b], comm=comm,
                 in_specs=[pl.BlockSpec((ts, tm), lambda i, n, s: (s, i)), pl.BlockSpec((ts, tn), lambda i, n, s: (s, n))],
                 out_specs=[ospec], out_shape=[oshape], scratch=[pltpu.VMEM((tm, tn), F32)])


def _rope_tables(S):
    half = ROPE_DIM // 2
    inv_freq = ROPE_THETA ** (-jnp.arange(0, ROPE_DIM, 2, dtype=F32) / ROPE_DIM)
    ang = jnp.arange(S, dtype=F32)[:, None] * inv_freq[None, :]
    cos, sin = jnp.cos(ang), jnp.sin(ang)
    zeros = jnp.zeros((S, HEAD_DIM - ROPE_DIM), F32)
    c = jnp.concatenate([cos, cos, jnp.ones((S, HEAD_DIM - ROPE_DIM), F32)], axis=1)
    sm = jnp.concatenate([-sin, jnp.zeros((S, half), F32), zeros], axis=1)
    sp = jnp.concatenate([jnp.zeros((S, half), F32), sin, zeros], axis=1)
    return c, sm, sp


def _rope(t, c, sm, sp):
    return t * c + pltpu.roll(t, HEAD_DIM - ROPE_DIM // 2, 1) * sm + pltpu.roll(t, ROPE_DIM // 2, 1) * sp


def _rope_t(dy, c, sm, sp):
    return dy * c + pltpu.roll(dy * sm, ROPE_DIM // 2, 1) + pltpu.roll(dy * sp, HEAD_DIM - ROPE_DIM // 2, 1)


def _att_mask(i):
    qi = lax.broadcasted_iota(jnp.int32, (BLK, 2 * BLK), 0)
    kj = lax.broadcasted_iota(jnp.int32, (BLK, 2 * BLK), 1)
    diff = qi + BLK - kj
    first_key = jnp.where(i > 0, 0, BLK)
    return (diff >= 0) & (diff <= BLK) & (kj >= first_key)


def _res_rows(r, i, n, d):
    if d == 1:
        return pl.ds(pl.multiple_of(i * n, n), n)
    return pl.ds(r + i * (n * d), n, stride=d)


def _att_specs(S, gi):
    def sect(off):
        base = (off + gi * GROUP_W) // HEAD_DIM
        return _once((S, HEAD_DIM), lambda hh: (0, base + hh))

    tab = pl.BlockSpec((S, HEAD_DIM), lambda hh: (0, 0))
    head = pl.BlockSpec((S, HEAD_DIM), lambda hh: (0, hh))
    return sect, tab, head


def _each_residue(d, fn):
    if d == 1:
        fn(0)
    else:
        lax.fori_loop(0, d, lambda r, carry: (fn(r), carry)[1], 0)


def _att_fwd(qkv, tabs, gi, d, name, comm=()):
    S = qkv.shape[0]
    L = S // d
    sect, tab, head = _att_specs(S, gi)
    nblk = L // BLK
    scale = HEAD_DIM ** -0.5

    def body(q_ref, k_ref, v_ref, c_ref, sm_ref, sp_ref, o_ref, lse_ref, qr, kp, vp):
        kp[pl.ds(0, BLK), :] = jnp.zeros((BLK, HEAD_DIM), BF)
        vp[pl.ds(0, BLK), :] = jnp.zeros((BLK, HEAD_DIM), BF)

        def residue(r):
            res = _res_rows(r, 0, L, d)
            c, sm, sp = c_ref[res, :], sm_ref[res, :], sp_ref[res, :]
            qr[...] = _rope(q_ref[res, :], c, sm, sp).astype(BF)
            kp[pl.ds(BLK, L), :] = _rope(k_ref[res, :], c, sm, sp).astype(BF)
            vp[pl.ds(BLK, L), :] = v_ref[res, :].astype(BF)

            def blk(i, carry):
                r0 = pl.multiple_of(i * BLK, BLK)
                s = _nt(qr[pl.ds(r0, BLK), :], kp[pl.ds(r0, 2 * BLK), :]) * scale
                s = jnp.where(_att_mask(i), s, NEG)
                m = jnp.max(s, axis=-1, keepdims=True)
                p = jnp.exp(s - m)
                l = jnp.sum(p, axis=-1, keepdims=True)
                out = _res_rows(r, i, BLK, d)
                o_ref[out, :] = _nn(p.astype(BF), vp[pl.ds(r0, 2 * BLK), :]) / l
                lse_ref[out, :] = jnp.broadcast_to(m + jnp.log(l), (BLK, HEAD_DIM))
                return carry

            lax.fori_loop(0, nblk, blk, 0, unroll=min(4, nblk))

        _each_residue(d, residue)

    shp = jax.ShapeDtypeStruct((S, GROUP_W), F32)
    return _call(body, grid=(HEADS_PER_GROUP,), name=name, args=[qkv, qkv, qkv, *tabs], comm=comm,
                 in_specs=[sect(Q_OFF), sect(K_OFF), sect(V_OFF), tab, tab, tab], out_specs=[head, head], out_shape=[shp, shp],
                 scratch=[pltpu.VMEM((L, HEAD_DIM), BF), pltpu.VMEM((L + BLK, HEAD_DIM), BF), pltpu.VMEM((L + BLK, HEAD_DIM), BF)])


def _att_combine(os, lses, name):
    S = os[0].shape[0]
    tm = _tile(S, 512)

    def body(o0, o1, o2, l0, l1, l2, oa_ref, lse_ref):
        a, b, c = l0[...], l1[...], l2[...]
        mx = jnp.maximum(jnp.maximum(a, b), c)
        wa, wb, wc = jnp.exp(a - mx), jnp.exp(b - mx), jnp.exp(c - mx)
        den = wa + wb + wc
        oa_ref[...] = ((wa * o0[...] + wb * o1[...] + wc * o2[...]) / den).astype(BF)
        lse_ref[...] = mx + jnp.log(den)

    row = pl.BlockSpec((tm, GROUP_W), lambda i: (i, 0))
    return _call(body, grid=(S // tm,), name=name, args=[*os, *lses], in_specs=[row] * 6, out_specs=[row, row],
                 out_shape=[jax.ShapeDtypeStruct((S, GROUP_W), BF), jax.ShapeDtypeStruct((S, GROUP_W), F32)])[0]


def _att_bwd(qkv, tabs, do, lse, dvec, gi, d, name, comm=()):
    S = qkv.shape[0]
    L = S // d
    sect, tab, head = _att_specs(S, gi)
    stat = _once((S, HEAD_DIM), lambda hh: (0, hh))
    nblk = L // BLK
    scale = HEAD_DIM ** -0.5

    def body(q_ref, k_ref, v_ref, c_ref, sm_ref, sp_ref, do_ref, lse_ref, dv_ref, dq_out, dk_out, dv_out, qr, kp, vp, dkp, dvp):
        kp[pl.ds(0, BLK), :] = jnp.zeros((BLK, HEAD_DIM), BF)
        vp[pl.ds(0, BLK), :] = jnp.zeros((BLK, HEAD_DIM), BF)

        def residue(r):
            res = _res_rows(r, 0, L, d)
            c, sm, sp = c_ref[res, :], sm_ref[res, :], sp_ref[res, :]
            qr[...] = _rope(q_ref[res, :], c, sm, sp).astype(BF)
            kp[pl.ds(BLK, L), :] = _rope(k_ref[res, :], c, sm, sp).astype(BF)
            vp[pl.ds(BLK, L), :] = v_ref[res, :].astype(BF)
            dkp[...] = jnp.zeros_like(dkp)
            dvp[...] = jnp.zeros_like(dvp)

            def blk(i, carry):
                r0 = pl.multiple_of(i * BLK, BLK)
                rows, win, pos = pl.ds(r0, BLK), pl.ds(r0, 2 * BLK), _res_rows(r, i, BLK, d)
                q, kw, vw, dob = qr[rows, :], kp[win, :], vp[win, :], do_ref[pos, :].astype(BF)
                s = jnp.where(_att_mask(i), _nt(q, kw) * scale, NEG)
                p = jnp.exp(s - lse_ref[pos, :][:, :1])
                ds = p * (_nt(dob, vw) - dv_ref[pos, :][:, :1]) * scale
                dsb = ds.astype(BF)
                dq_out[pos, :] = _rope_t(_nn(dsb, kw), c_ref[pos, :], sm_ref[pos, :], sp_ref[pos, :])
                dkp[win, :] += _tn(dsb, q)
                dvp[win, :] += _tn(p.astype(BF), dob)
                return carry

            lax.fori_loop(0, nblk, blk, 0, unroll=2)
            dk_out[res, :] = _rope_t(dkp[pl.ds(BLK, L), :], c, sm, sp)
            dv_out[res, :] = dvp[pl.ds(BLK, L), :]

        _each_residue(d, residue)

    shp = jax.ShapeDtypeStruct((S, GROUP_W), F32)
    return _call(body, grid=(HEADS_PER_GROUP,), name=name, args=[qkv, qkv, qkv, *tabs, do, lse, dvec], comm=comm,
                 in_specs=[sect(Q_OFF), sect(K_OFF), sect(V_OFF), tab, tab, tab, stat, stat, stat],
                 out_specs=[head, head, head], out_shape=[shp, shp, shp],
                 scratch=[pltpu.VMEM((L, HEAD_DIM), BF), pltpu.VMEM((L + BLK, HEAD_DIM), BF), pltpu.VMEM((L + BLK, HEAD_DIM), BF),
                          pltpu.VMEM((L + BLK, HEAD_DIM), F32), pltpu.VMEM((L + BLK, HEAD_DIM), F32)])


def _sg_parts(u_ref, vs_ref, g_ref, b_ref):
    uv = u_ref[...].astype(F32)
    vv = vs_ref[...].astype(F32)
    vg = _gelu(vv)
    mu = jnp.mean(vg, axis=-1, keepdims=True)
    vc = vg - mu
    rs = lax.rsqrt(jnp.mean(vc * vc, axis=-1, keepdims=True) + LN_EPS)
    y = vc * rs
    return uv, vv, rs, y, y * g_ref[...] + b_ref[...]


def _sg_wmask():
    t = lax.broadcasted_iota(jnp.int32, (BLK, BLK), 0)
    s = lax.broadcasted_iota(jnp.int32, (BLK, BLK), 1)
    return s <= t


def _sg_fwd(proj, sgw, sgbT, lng, lnb, name):
    S, P = proj.shape

    def body(u_ref, vs_ref, w_ref, bt_ref, g_ref, b_ref, z_ref):
        uv, _, _, _, vln = _sg_parts(u_ref, vs_ref, g_ref, b_ref)
        ug = _gelu(uv)
        vb = vln.astype(BF)
        mask = _sg_wmask()
        bt = bt_ref[...]
        for g in range(SG_GROUPS):
            cols = slice(g * BLK, (g + 1) * BLK)
            w = jnp.where(mask, w_ref[g], 0.0).astype(BF)
            sp = _nn(w, vb[:, cols]) + bt[:, g:g + 1]
            z_ref[:, cols] = (ug[:, cols] * sp).astype(BF)

    tile = lambda off: pl.BlockSpec((BLK, SG_W), lambda i: (i, off // SG_W))
    full = lambda shape: pl.BlockSpec(shape, lambda i: (0,) * len(shape))
    return _call(body, grid=(S // BLK,), name=name, args=[proj, proj, sgw, sgbT, lng, lnb],
                 in_specs=[tile(R_U), tile(R_VS), full((SG_GROUPS, BLK, BLK)), full((BLK, BLK)), full((1, SG_W)), full((1, SG_W))],
                 out_specs=[pl.BlockSpec((BLK, SG_W), lambda i: (i, 0))], out_shape=[jax.ShapeDtypeStruct((S, SG_W), BF)])[0][0]


def _sg_bwd(proj, dz, sgw, sgbT, lng, lnb, name):
    S, P = proj.shape

    def body(u_ref, vs_ref, dz_ref, w_ref, bt_ref, g_ref, b_ref, du_ref, dvs_ref, dw_ref, dbt_ref, dg_ref, db_ref, dvln):
        @pl.when(pl.program_id(0) == 0)
        def _():
            dw_ref[...] = jnp.zeros_like(dw_ref)
            dbt_ref[...] = jnp.zeros_like(dbt_ref)
            dg_ref[...] = jnp.zeros_like(dg_ref)
            db_ref[...] = jnp.zeros_like(db_ref)

        uv, vv, rs, y, vln = _sg_parts(u_ref, vs_ref, g_ref, b_ref)
        ug = _gelu(uv)
        vb = vln.astype(BF)
        dzv = dz_ref[...].astype(F32)
        dsp = dzv * ug
        dspb = dsp.astype(BF)
        mask = _sg_wmask()
        bt = bt_ref[...]
        lane = lax.broadcasted_iota(jnp.int32, (BLK, BLK), 1)
        dbt = jnp.zeros((BLK, BLK), F32)
        for g in range(SG_GROUPS):
            cols = slice(g * BLK, (g + 1) * BLK)
            w = jnp.where(mask, w_ref[g], 0.0).astype(BF)
            sp = _nn(w, vb[:, cols]) + bt[:, g:g + 1]
            du_ref[:, cols] = (dzv[:, cols] * sp * _gelu_grad(uv[:, cols])).astype(BF)
            dw_ref[g] += jnp.where(mask, _nt(dspb[:, cols], vb[:, cols]), 0.0)
            dbt = dbt + jnp.where(lane == g, jnp.sum(dsp[:, cols], axis=-1, keepdims=True), 0.0)
            dvln[:, cols] = _tn(w, dspb[:, cols])
        dbt_ref[...] += dbt
        dvl = dvln[...]
        dg_ref[...] += jnp.sum(dvl * y, axis=0, keepdims=True)
        db_ref[...] += jnp.sum(dvl, axis=0, keepdims=True)
        dy = dvl * g_ref[...]
        dvg = rs * (dy - jnp.mean(dy, axis=-1, keepdims=True) - y * jnp.mean(dy * y, axis=-1, keepdims=True))
        dvs_ref[...] = (dvg * _gelu_grad(vv)).astype(BF)

    tile = lambda off: pl.BlockSpec((BLK, SG_W), lambda i: (i, off // SG_W))
    full = lambda shape: pl.BlockSpec(shape, lambda i: (0,) * len(shape))
    row = pl.BlockSpec((BLK, SG_W), lambda i: (i, 0))
    return _call(body, grid=(S // BLK,), name=name, args=[proj, proj, dz, sgw, sgbT, lng, lnb],
                 in_specs=[tile(R_U), tile(R_VS), row, full((SG_GROUPS, BLK, BLK)), full((BLK, BLK)), full((1, SG_W)), full((1, SG_W))],
                 out_specs=[row, row, full((SG_GROUPS, BLK, BLK)), full((BLK, BLK)), full((1, SG_W)), full((1, SG_W))],
                 out_shape=[jax.ShapeDtypeStruct((S, SG_W), BF), jax.ShapeDtypeStruct((S, SG_W), BF),
                            jax.ShapeDtypeStruct((SG_GROUPS, BLK, BLK), F32), jax.ShapeDtypeStruct((BLK, BLK), F32),
                            jax.ShapeDtypeStruct((1, SG_W), F32), jax.ShapeDtypeStruct((1, SG_W), F32)],
                 scratch=[pltpu.VMEM((BLK, SG_W), F32)])[0]


def _gate_merge(oatt, z, watt, wsg, proj, name, comm=()):
    S = oatt.shape[0]
    nb, _, Db = watt.shape
    D = nb * Db
    tm = _tile(S, 512)
    half = D // 2
    ga, gs = R_GA // half, (R_GA + D) // half

    def body(oa_ref, z_ref, wa_ref, ws_ref, ga0, ga1, gs0, gs1, ya_ref, ys_ref, mg_ref):
        oa, zv = oa_ref[...], z_ref[...]
        for j in range(nb):
            cols = slice(j * Db, (j + 1) * Db)
            g_a, g_s = (ga0, gs0) if j < nb // 2 else (ga1, gs1)
            gcols = slice((j % (nb // 2)) * Db, (j % (nb // 2) + 1) * Db)
            ya = _nn(oa, wa_ref[j])
            ys = _nn(zv, ws_ref[j])
            ya_ref[:, cols] = ya.astype(BF)
            ys_ref[:, cols] = ys.astype(BF)
            mg_ref[:, cols] = (jax.nn.sigmoid(g_a[:, gcols].astype(F32)) * ya + jax.nn.sigmoid(g_s[:, gcols].astype(F32)) * ys).astype(BF)

    out = pl.BlockSpec((tm, D), lambda i: (i, 0))
    gate = lambda b: pl.BlockSpec((tm, half), lambda i: (i, b))
    shp = jax.ShapeDtypeStruct((S, D), BF)
    return _call(body, grid=(S // tm,), name=name, args=[oatt, z, watt, wsg, proj, proj, proj, proj], comm=comm,
                 in_specs=[pl.BlockSpec((tm, GROUP_W), lambda i: (i, 0)), pl.BlockSpec((tm, SG_W), lambda i: (i, 0)),
                           pl.BlockSpec((nb, GROUP_W, Db), lambda i: (0, 0, 0)), pl.BlockSpec((nb, SG_W, Db), lambda i: (0, 0, 0)),
                           gate(ga), gate(ga + 1), gate(gs), gate(gs + 1)],
                 out_specs=[out, out, out], out_shape=[shp, shp, shp])


def _mix_out(merged, wout, x, gn, name):
    S, D = x.shape
    tm = _tile(S, 256)

    def body(m_ref, w_ref, x_ref, gn_ref, xo_ref, hn_ref):
        xo = x_ref[...] + _nn(m_ref[...], w_ref[...])
        r = lax.rsqrt(jnp.mean(xo * xo, axis=-1, keepdims=True) + NORM_EPS)
        xo_ref[...] = xo
        hn_ref[...] = (xo * r * gn_ref[...]).astype(BF)

    row = pl.BlockSpec((tm, D), lambda i: (i, 0))
    return _call(body, grid=(S // tm,), name=name, args=[merged, wout, x, gn],
                 in_specs=[row, pl.BlockSpec((D, D), lambda i: (0, 0)), row, pl.BlockSpec((1, D), lambda i: (0, 0))],
                 out_specs=[row, row], out_shape=[jax.ShapeDtypeStruct((S, D), F32), jax.ShapeDtypeStruct((S, D), BF)])[0]


def _mix_bwd_gate(dmix, wout, ya, ys, proj, name):
    S, D = dmix.shape
    tm, tn = _tile(S, 512), 512
    ga, gs = R_GA // tn, (R_GA + D) // tn

    def body(dm_ref, w_ref, ya_ref, ys_ref, ga_ref, gs_ref, dya_ref, dys_ref, dga_ref, dgs_ref):
        w_rows = pl.ds(pl.multiple_of(pl.program_id(1) * tn, tn), tn)
        dm = _nt(dm_ref[...], w_ref[w_rows, :])
        sa = jax.nn.sigmoid(ga_ref[...].astype(F32))
        ss = jax.nn.sigmoid(gs_ref[...].astype(F32))
        dya_ref[...] = (dm * sa).astype(BF)
        dys_ref[...] = (dm * ss).astype(BF)
        dga_ref[...] = (dm * ya_ref[...].astype(F32) * sa * (1.0 - sa)).astype(BF)
        dgs_ref[...] = (dm * ys_ref[...].astype(F32) * ss * (1.0 - ss)).astype(BF)

    out = pl.BlockSpec((tm, tn), lambda i, n: (i, n))
    shp = jax.ShapeDtypeStruct((S, D), BF)
    return _call(body, grid=(S // tm, D // tn), name=name, args=[dmix, wout, ya, ys, proj, proj],
                 in_specs=[pl.BlockSpec((tm, D), lambda i, n: (i, 0)), pl.BlockSpec((D, D), lambda i, n: (0, 0)), out, out,
                           pl.BlockSpec((tm, tn), lambda i, n: (i, ga + n)), pl.BlockSpec((tm, tn), lambda i, n: (i, gs + n))],
                 out_specs=[out] * 4, out_shape=[shp] * 4)[0]


def _att_sg_dout(dya, dys, watt, wsg, oatt, name, comm=()):
    S, D = dya.shape
    nb, _, Db = watt.shape
    tm = _tile(S, 512)

    def body(dya_ref, dys_ref, wa_ref, ws_ref, oa_ref, do_ref, dz_ref, dvec_ref):
        def back(dy_ref, w_ref, rows):
            tot = None
            for j in range(nb):
                part = _nt(dy_ref[:, j * Db:(j + 1) * Db], w_ref[j, rows, :])
                tot = part if tot is None else tot + part
            return tot

        dov = back(dya_ref, wa_ref, slice(0, GROUP_W))
        do_ref[...] = dov
        for c0 in range(0, SG_W, GROUP_W):
            dz_ref[:, c0:c0 + GROUP_W] = back(dys_ref, ws_ref, slice(c0, c0 + GROUP_W)).astype(BF)
        prod = dov * oa_ref[...].astype(F32)
        for hh in range(HEADS_PER_GROUP):
            cols = slice(hh * HEAD_DIM, (hh + 1) * HEAD_DIM)
            dvec_ref[:, cols] = jnp.broadcast_to(jnp.sum(prod[:, cols], axis=-1, keepdims=True), (tm, HEAD_DIM))

    row = pl.BlockSpec((tm, D), lambda i: (i, 0))
    att = pl.BlockSpec((tm, GROUP_W), lambda i: (i, 0))
    return _call(body, grid=(S // tm,), name=name, args=[dya, dys, watt, wsg, oatt], comm=comm,
                 in_specs=[row, row, pl.BlockSpec((nb, GROUP_W, Db), lambda i: (0, 0, 0)), pl.BlockSpec((nb, SG_W, Db), lambda i: (0, 0, 0)), att],
                 out_specs=[att, pl.BlockSpec((tm, SG_W), lambda i: (i, 0)), att],
                 out_shape=[jax.ShapeDtypeStruct((S, GROUP_W), F32), jax.ShapeDtypeStruct((S, SG_W), BF), jax.ShapeDtypeStruct((S, GROUP_W), F32)])[0]


def _small_allreduce(pack, name):
    R = pack.shape[0]

    def body(p_ref, o_ref, gath, send, recv):
        x, y, c = _place()
        me = 4 * x + 2 * y + c
        gath[me] = p_ref[...]
        copies = []
        for r in range(1, N_DEV):
            px, py, pc = _flip(x, r & 4), _flip(y, r & 2), _flip(c, r & 1)
            peer = 4 * px + 2 * py + pc
            mk = lambda dst: pltpu.make_async_remote_copy(src_ref=p_ref, dst_ref=dst, send_sem=send.at[r - 1], recv_sem=recv.at[r - 1],
                                                          device_id=(px, py, pc), device_id_type=MESH)
            snd = mk(gath.at[me])
            snd.start()
            copies.append((snd, mk(gath.at[peer])))
        for snd, rcv in copies:
            rcv.wait_recv()
            snd.wait_send()
        acc = gath[0]
        for s in range(1, N_DEV):
            acc = acc + gath[s]
        o_ref[...] = acc

    vm = pl.BlockSpec(memory_space=pltpu.VMEM)
    return pl.pallas_call(
        body, name=name, in_specs=[vm], out_specs=vm, out_shape=jax.ShapeDtypeStruct(pack.shape, F32),
        scratch_shapes=[pltpu.VMEM((N_DEV, R, 128), F32), pltpu.SemaphoreType.DMA((7,)), pltpu.SemaphoreType.DMA((7,))],
        compiler_params=pltpu.CompilerParams(vmem_limit_bytes=VMEM_LIMIT),
    )(pack)


def _row_tile(R, C, elems=262144):
    fits = [t for t in range(16, R + 1, 16) if R % t == 0 and t * C <= elems]
    return max(fits) if fits else R


def _pair_add(parts, other, name):
    _, R, C = parts.shape
    tr = _row_tile(R, C, 1048576)

    def body(c_ref, p_ref, o_ref, s_ref):
        s_ref[0] = (p_ref[0].astype(F32) + o_ref[0].astype(F32)).astype(BF)

    core = lax.axis_index("c").astype(jnp.int32).reshape(1)
    return pl.pallas_call(
        body, name=name,
        grid_spec=pltpu.PrefetchScalarGridSpec(
            num_scalar_prefetch=1, grid=(N_CHIP, R // tr),
            in_specs=[pl.BlockSpec((1, tr, C), lambda q, i, c: (2 * q + c[0], i, 0)), pl.BlockSpec((1, tr, C), lambda q, i, c: (q, i, 0))],
            out_specs=pl.BlockSpec((1, tr, C), lambda q, i, c: (q, i, 0))),
        out_shape=jax.ShapeDtypeStruct((N_CHIP, R, C), BF),
        compiler_params=pltpu.CompilerParams(dimension_semantics=("arbitrary", "arbitrary"), vmem_limit_bytes=VMEM_LIMIT),
    )(core, parts, other)


def _adamw(parts, w, m, v, name):
    ns, R, C = parts.shape
    tr = _row_tile(R, C, 524288)
    c1 = 1.0 - ADAM_B1 ** ADAM_STEP
    c2 = 1.0 - ADAM_B2 ** ADAM_STEP

    def body(p_ref, w_ref, m_ref, v_ref, g_ref, d_ref, nm_ref, nv_ref):
        g = p_ref[0].astype(F32)
        for s in range(1, ns):
            g = g + p_ref[s].astype(F32)
        mn = ADAM_B1 * m_ref[...] + (1.0 - ADAM_B1) * g
        vn = ADAM_B2 * v_ref[...] + (1.0 - ADAM_B2) * (g * g)
        g_ref[...] = g
        nm_ref[...] = mn
        nv_ref[...] = vn
        d_ref[...] = -ADAM_LR * ((mn / c1) / (jnp.sqrt(vn / c2) + ADAM_EPS) + ADAM_WD * w_ref[...])

    row = pl.BlockSpec((tr, C), lambda i: (i, 0))
    shp = jax.ShapeDtypeStruct((R, C), F32)
    return _call(body, grid=(R // tr,), name=name, args=[parts, w, m, v],
                 in_specs=[pl.BlockSpec((ns, tr, C), lambda i: (0, i, 0)), row, row, row], out_specs=[row] * 4, out_shape=[shp] * 4)[0]


def _pad_rows(a, rows):
    return jnp.pad(a, ((0, rows - a.shape[0]), (0, 0)))


def kernel(x, ffn1_norm, ffn1_w_gate, ffn1_w_up, ffn1_w_down, mix_norm, w_in, sg_ln_g, sg_ln_b, sg_w, sg_b, w_att_out, w_sg_out, w_out, ffn2_norm, ffn2_w_gate, ffn2_w_up, ffn2_w_down, final_norm, loss_target, m_ffn1_norm, m_ffn1_w_gate, m_ffn1_w_up, m_ffn1_w_down, m_mix_norm, m_w_in, m_sg_ln_g, m_sg_ln_b, m_sg_w, m_sg_b, m_w_att_out, m_w_sg_out, m_w_out, m_ffn2_norm, m_ffn2_w_gate, m_ffn2_w_up, m_ffn2_w_down, m_final_norm, v_ffn1_norm, v_ffn1_w_gate, v_ffn1_w_up, v_ffn1_w_down, v_mix_norm, v_w_in, v_sg_ln_g, v_sg_ln_b, v_sg_w, v_sg_b, v_w_att_out, v_w_sg_out, v_w_out, v_ffn2_norm, v_ffn2_w_gate, v_ffn2_w_up, v_ffn2_w_down, v_final_norm):
    S, D = x.shape[1], x.shape[2]
    Pb = w_in.shape[2]
    P = N_DEV * Pb
    assert P == GA_OFF + 2 * D and D % (N_DEV * 128) == 0 and S % (BLK * DILATIONS[-1]) == 0
    xs, tgt = x[0], loss_target[0]

    sharded = dict(ffn1_w_gate=ffn1_w_gate, ffn1_w_up=ffn1_w_up, ffn1_w_down=ffn1_w_down, w_in=w_in, w_att_out=w_att_out,
                   w_sg_out=w_sg_out, w_out=w_out, ffn2_w_gate=ffn2_w_gate, ffn2_w_up=ffn2_w_up, ffn2_w_down=ffn2_w_down)
    cols = ("ffn1_w_gate", "ffn1_w_up", "w_in", "ffn2_w_gate", "ffn2_w_up")
    local = lambda n, a: a[0].T if n in cols else a[0]
    back = lambda n, a: a.T[None] if n in cols else a[None]
    wloc = {n: local(n, w) for n, w in sharded.items()}
    sb = {n: w.astype(BF) for n, w in wloc.items()}

    (h1,), ((wg1,),) = _rms_fwd(xs, ffn1_norm, "rms1", comm=[_Gather([sb["ffn1_w_gate"]], 1.0, 1.0)])
    (g1,), ((wu1,),) = _ffn_gate(h1, wg1, "ffn1_gate", comm=[_Gather([sb["ffn1_w_up"]], 0.9, 0.55)])
    (u1, a1), ((wd1,),) = _ffn_up_act(h1, wu1, g1, "ffn1_up_act", comm=[_Gather([sb["ffn1_w_down"]], 0.9, 0.55)])
    (x1, h2), ((winT8,),) = _ffn_down_norm(a1, wd1, xs, mix_norm, "ffn1_down", comm=[_Gather([sb["w_in"]], 1.0, 0.7)])
    winT = winT8.reshape(P, D)
    tm_proj = _tile(S, 1024)
    (qkv,), ((wg2,),) = _mm_nt(h2, winT, tm_proj, 512, 0, U_OFF, F32, "proj_qkv", comm=[_Gather([sb["ffn2_w_gate"]], 1.0, 0.7)])
    (rest,), ((wu2,),) = _mm_nt(h2, winT, tm_proj, 512, U_OFF, P, BF, "proj_rest", comm=[_Gather([sb["ffn2_w_up"]], 0.75, 0.45)])
    tabs = _rope_tables(S)
    rides = [[_Gather([sb["w_att_out"], sb["w_sg_out"]], 0.9, 0.5)], [_Gather([sb["w_out"]], 0.85, 0.45)], []]
    os, lses, late = [], [], []
    for gi, d in enumerate(DILATIONS):
        (o, l), got_here = _att_fwd(qkv, tabs, gi, d, f"att_fwd{gi}", comm=rides[gi])
        late += [w for g in got_here for w in g]
        os.append(o)
        lses.append(l)
    watt, wsg, wout8 = late
    wout = wout8.reshape(D, D)
    oatt, lse = _att_combine(os, lses, "att_combine")
    sgw = sg_w[0]
    sgbT = jnp.pad(sg_b[0].T, ((0, 0), (0, BLK - SG_GROUPS)))
    z = _sg_fwd(rest, sgw, sgbT, sg_ln_g, sg_ln_b, "sg_fwd")
    (ya, ys, merged), _ = _gate_merge(oatt, z, watt, wsg, rest, "gate_merge")
    x2, h3 = _mix_out(merged, wout, x1, ffn2_norm, "mix_out")
    (g3, u3, a3), ((wd2,),) = _ffn_up(h3, wg2, wu2, "ffn2_up", comm=[_Gather([sb["ffn2_w_down"]], 0.6, 0.35)])
    dx3, dyb3, d_final, loss_part = _ffn_down_loss(a3, wd2, x2, final_norm.reshape(1, D), tgt, "ffn2_down_loss")

    Fb = wg2.shape[1]
    Db = watt.shape[2]
    p_pad = -(-P // PROJ_TK) * PROJ_TK
    win_tail = _pad_rows(winT[p_pad - PROJ_TK:], PROJ_TK)
    (dg3, du3), _ = _ffn_bwd_act(dyb3, wd2, g3, u3, "ffn2_bwd_act")
    (dwd2,), _ = _ffn_dwd(a3, dyb3, "ffn2_dwd")
    (dwg2, dwu2), _ = _ffn_dwgu(h3, dg3, du3, "ffn2_dwgu")
    ffn2_parts = [dwd2, dwg2, dwu2]
    (dx2, dmixb, d_ffn2n), (ffn2_other,) = _dh_rms_bwd([(dg3, wg2), (du3, wu2)], True, Fb, x2, ffn2_norm, dx3, 1.0, "ffn2_dh",
                                                     comm=[_Swap(ffn2_parts)])
    ffn2_sums = [_pair_add(p, o, f"pair_ffn2_{i}") for i, (p, o) in enumerate(zip(ffn2_parts, ffn2_other))]

    dya, dys, dga, dgs = _mix_bwd_gate(dmixb, wout, ya, ys, rest, "mix_bwd_gate")
    (dwout,), _ = _mm_tn(merged, dmixb, _tile(D, 1024), _tile(D, 1024), _tile(S, 1024), False, "dw_out")
    do, dz, dvec = _att_sg_dout(dya, dys, watt, wsg, oatt, "att_sg_dout")
    (dwatt,), _ = _mm_tn(oatt, dya, GROUP_W, 2 * Db, _tile(S, 1024), Db, "dw_att")
    (dwsg,), _ = _mm_tn(z, dys, SG_W, 2 * Db, _tile(S, 1024), Db, "dw_sg")
    mix_parts = [dwout.reshape(N_DEV, D // N_DEV, D), dwatt, dwsg]
    du, dvs, d_sgw, d_sgbT, d_lng, d_lnb = _sg_bwd(rest, dz, sgw, sgbT, sg_ln_g, sg_ln_b, "sg_bwd")
    dqs, dks, dvs_att, ffn2_got = [], [], [], []
    for gi, d in enumerate(DILATIONS):
        ride = [_Ici([ffn2_sums[0]])] if gi == 2 else []
        (dq, dk, dv), got_here = _att_bwd(qkv, tabs, do, lse, dvec, gi, d, f"att_bwd{gi}", comm=ride)
        ffn2_got += [g[0] for g in got_here]
        dqs.append(dq)
        dks.append(dk)
        dvs_att.append(dv)
    dproj = jnp.concatenate([t.astype(BF) for t in dqs + dks + dvs_att] + [du, dvs, dga, dgs, jnp.zeros((S, p_pad - P), BF)], axis=1)
    (dx1, dyb1, d_mixn), (ffn2_rest, mix_other) = _dh_rms_bwd([(dproj, winT)], False, PROJ_TK, x1, mix_norm, dx2, 0.5, "proj_dh",
                                                            comm=[_Ici(ffn2_sums[1:]), _Swap(mix_parts)], tail=win_tail)
    ffn2_got += ffn2_rest
    mix_sums = [_pair_add(p, o, f"pair_mix_{i}") for i, (p, o) in enumerate(zip(mix_parts, mix_other))]
    (dwd1,), (mix_got,) = _ffn_dwd(a1, dyb1, "ffn1_dwd", comm=[_Ici(mix_sums)])
    rows = lambda a: a.reshape(-1, 128)
    pad8 = lambda a: _pad_rows(a, -(-a.shape[0] // 8) * 8)
    small = [("sg_w", rows(d_sgw), sg_w, m_sg_w, v_sg_w), ("mix_norm", rows(d_mixn), mix_norm, m_mix_norm, v_mix_norm),
             ("ffn2_norm", rows(d_ffn2n), ffn2_norm, m_ffn2_norm, v_ffn2_norm), ("final_norm", rows(d_final), final_norm, m_final_norm, v_final_norm),
             ("sg_ln_g", rows(d_lng), sg_ln_g, m_sg_ln_g, v_sg_ln_g), ("sg_ln_b", rows(d_lnb), sg_ln_b, m_sg_ln_b, v_sg_ln_b),
             ("sg_b", d_sgbT[:, :SG_GROUPS].T, sg_b, m_sg_b, v_sg_b)]
    gpack = jnp.concatenate([pad8(g) for _, g, _, _, _ in small] + [pad8(loss_part)], axis=0)
    (dwin,), ((wd1_other,), (gpacks,)) = _mm_tn(dproj, h2, 512, D, _tile(S, 2048), False, "dw_in", mrows=P,
                                              comm=[_Swap([dwd1]), _Spread([gpack])])
    dwin = dwin.reshape(N_DEV, Pb, D)
    wd1_sum = _pair_add(dwd1, wd1_other, "pair_wd1")
    (dg1, du1), ((wd1_got,), (win_other,)) = _ffn_bwd_act(dyb1, wd1, g1, u1, "ffn1_bwd_act", comm=[_Ici([wd1_sum]), _Swap([dwin])])
    win_sum = _pair_add(dwin, win_other, "pair_win")
    (dwg1, dwu1), ((win_got,),) = _ffn_dwgu(h1, dg1, du1, "ffn1_dwgu", comm=[_Ici([win_sum])])
    gu_parts = [dwg1, dwu1]
    gu_other = _comm_only(_Swap(gu_parts), "swap_gu1")
    gu_sums = [_pair_add(p, o, f"pair_gu1_{i}") for i, (p, o) in enumerate(zip(gu_parts, gu_other))]
    (dx0, _, d_ffn1n), (gu_got,) = _dh_rms_bwd([(dg1, wg1), (du1, wu1)], True, Fb, xs, ffn1_norm, dx1, 1.0, "ffn1_dh",
                                               comm=[_Ici(gu_sums)])

    got = dict(ffn2_w_down=ffn2_got[0], ffn2_w_gate=ffn2_got[1], ffn2_w_up=ffn2_got[2], w_out=mix_got[0], w_att_out=mix_got[1],
               w_sg_out=mix_got[2], w_in=win_got, ffn1_w_gate=gu_got[0], ffn1_w_up=gu_got[1], ffn1_w_down=wd1_got)
    moments = dict(ffn1_w_gate=(m_ffn1_w_gate, v_ffn1_w_gate), ffn1_w_up=(m_ffn1_w_up, v_ffn1_w_up),
                   ffn1_w_down=(m_ffn1_w_down, v_ffn1_w_down), w_in=(m_w_in, v_w_in), w_att_out=(m_w_att_out, v_w_att_out),
                   w_sg_out=(m_w_sg_out, v_w_sg_out), w_out=(m_w_out, v_w_out), ffn2_w_gate=(m_ffn2_w_gate, v_ffn2_w_gate),
                   ffn2_w_up=(m_ffn2_w_up, v_ffn2_w_up), ffn2_w_down=(m_ffn2_w_down, v_ffn2_w_down))
    res = {}
    for n in sharded:
        mm, vv = moments[n]
        outs = _adamw(got[n], wloc[n], local(n, mm), local(n, vv), "adamw_" + n)
        res[n] = [back(n, o) for o in outs]

    zero8 = jnp.zeros((8, 128), F32)
    wpack = jnp.concatenate([pad8(rows(w)) for _, _, w, _, _ in small] + [zero8], axis=0)
    mpack = jnp.concatenate([pad8(rows(m)) for _, _, _, m, _ in small] + [zero8], axis=0)
    vpack = jnp.concatenate([pad8(rows(v)) for _, _, _, _, v in small] + [zero8], axis=0)
    packs = _adamw(gpacks, wpack, mpack, vpack, "adamw_small")
    off = 0
    for n, g, w, _, _ in small:
        r = g.shape[0]
        res[n] = [p[off:off + r].reshape(w.shape) for p in packs]
        off += -(-r // 8) * 8
    loss = packs[0][off, 0]
    g_first = _small_allreduce(rows(d_ffn1n), "allreduce_ffn1_norm")
    res["ffn1_norm"] = [p.reshape(ffn1_norm.shape) for p in
                        _adamw(g_first[None], rows(ffn1_norm), rows(m_ffn1_norm), rows(v_ffn1_norm), "adamw_ffn1_norm")]

    order = ["ffn1_norm", "ffn1_w_gate", "ffn1_w_up", "ffn1_w_down", "mix_norm", "w_in", "sg_ln_g", "sg_ln_b", "sg_w", "sg_b",
             "w_att_out", "w_sg_out", "w_out", "ffn2_norm", "ffn2_w_gate", "ffn2_w_up", "ffn2_w_down", "final_norm"]
    return (loss, dx0[None], *[res[n][0] for n in order], *[res[n][1] for n in order], *[res[n][2] for n in order],
            *[res[n][3] for n in order])
```

```python
import math

import jax
import jax.numpy as jnp
from jax import lax
from jax.experimental import pallas as pl
from jax.experimental.pallas import tpu as pltpu

BF = jnp.bfloat16
F32 = jnp.float32
MESH = pl.DeviceIdType.MESH
N_DEV = 8
N_CHIP = 4

HEAD_DIM = 128
HEADS_PER_GROUP = 4
GROUP_W = HEADS_PER_GROUP * HEAD_DIM
DILATIONS = (1, 4, 16)
ATT_W = len(DILATIONS) * GROUP_W
SG_W = 1536
SG_GROUPS = 12
BLK = 128
ROPE_DIM = 32
ROPE_THETA = 500000.0
NORM_EPS = 1e-6
LN_EPS = 1e-5
Q_OFF, K_OFF, V_OFF, U_OFF, VS_OFF, GA_OFF = 0, ATT_W, 2 * ATT_W, 3 * ATT_W, 3 * ATT_W + SG_W, 3 * ATT_W + 2 * SG_W

ADAM_LR, ADAM_B1, ADAM_B2, ADAM_EPS, ADAM_WD, ADAM_STEP = 0.001, 0.9, 0.999, 1e-08, 0.01, 10

VMEM_LIMIT = 56 * 1024 * 1024
NEG = -1e30
ANY = pl.BlockSpec(memory_space=pl.ANY)
EPI_ROWS = 128
ACC_COLS = 512
FFN_PAIR = 2
FFN_ROWS = 1024
PROJ_TK = 1536
R_U, R_VS, R_GA = 0, SG_W, 2 * SG_W


def _once(shape, index_map):
    return pl.BlockSpec(shape, index_map, pipeline_mode=pl.Buffered(1))


def _tile(n, pref):
    t = min(n, pref)
    while n % t:
        t //= 2
    return t


def _nt(a, b):
    return lax.dot_general(a, b, (((1,), (1,)), ((), ())), preferred_element_type=F32)


def _tn(a, b):
    return lax.dot_general(a, b, (((0,), (0,)), ((), ())), preferred_element_type=F32)


def _nn(a, b):
    return jnp.dot(a, b, preferred_element_type=F32)


def _acc_dots(acc_ref, terms, transposed_rhs=False):
    n = acc_ref.shape[1]
    width = min(n, ACC_COLS)
    for c0 in range(0, n, width):
        cols = slice(c0, c0 + width)
        tot = None
        for lhs, rhs in terms:
            part = _nt(lhs, rhs(cols)) if transposed_rhs else _nn(lhs, rhs(cols))
            tot = part if tot is None else tot + part
        acc_ref[:, cols] += tot


def _gelu(x):
    return 0.5 * x * (1.0 + lax.erf(x * (2.0 ** -0.5)))


def _gelu_grad(x):
    return 0.5 * (1.0 + lax.erf(x * (2.0 ** -0.5))) + x * jnp.exp(-0.5 * x * x) * (1.0 / math.sqrt(2.0 * math.pi))


def _place():
    x, y, c = lax.axis_index("x"), lax.axis_index("y"), lax.axis_index("c")
    return x, y, c


def _flip(v, bit):
    return 1 - v if bit else v


class _Gather:
    def __init__(self, shards, mid_frac=1.0, relay_frac=0.5):
        self.arrays = list(shards)
        self.relay_frac = relay_frac
        self.mid_frac = mid_frac
        nw = len(shards)
        self.out_shape = [jax.ShapeDtypeStruct((N_DEV,) + s.shape, s.dtype) for s in shards]
        self.scratch = [pltpu.SemaphoreType.DMA((nw, 7)), pltpu.SemaphoreType.DMA((nw, 7)), pltpu.SemaphoreType.DMA((nw,))]

    def _parts(self, ins, outs, sems):
        x, y, c = _place()
        send, recv, loc = sems
        south = c == 0
        near = (jnp.where(south, x, 1 - x), jnp.where(south, 1 - y, y), c)
        far = (jnp.where(south, 1 - x, x), jnp.where(south, y, 1 - y), c)
        diag = (1 - x, 1 - y, c)

        def copy(k, s, block, to, src=None):
            dst = outs[k].at[4 * block[0] + 2 * block[1] + block[2]]
            return pltpu.make_async_remote_copy(src_ref=dst if src is None else src, dst_ref=dst, send_sem=send.at[k, s],
                                                recv_sem=recv.at[k, s], device_id=to, device_id_type=MESH)

        def first(k):
            me = (x, y, c)
            return [copy(k, 0, me, (x, y, 1 - c), src=ins[k]), copy(k, 1, me, (1 - x, y, c), src=ins[k]),
                    copy(k, 2, me, (x, 1 - y, c), src=ins[k])]

        def local(k):
            return pltpu.make_async_copy(ins[k], outs[k].at[4 * x + 2 * y + c], loc.at[k])

        return x, y, c, near, far, diag, copy, first, local

    def start(self, ins, outs, sems):
        *_, first, local = self._parts(ins, outs, sems)
        for k in range(len(ins)):
            local(k).start()
            for cp in first(k):
                cp.start()

    def relay(self, ins, outs, sems):
        x, y, c, near, far, _, copy, _, _ = self._parts(ins, outs, sems)
        for k in range(len(ins)):
            copy(k, 2 - c, near, (x, y, c)).wait_recv()
            copy(k, 3, near, far).start()
            copy(k, 5 - c, near, (x, y, 1 - c)).start()

    def mid(self, ins, outs, sems):
        x, y, c, _, far, diag, copy, _, _ = self._parts(ins, outs, sems)
        for k in range(len(ins)):
            copy(k, 1 + c, far, (x, y, c)).wait_recv()
            copy(k, 4 + c, far, (x, y, 1 - c)).start()
            copy(k, 3, diag, (x, y, c)).wait_recv()
            copy(k, 6, diag, (x, y, 1 - c)).start()

    def finish(self, ins, outs, sems):
        x, y, c, near, _, _, copy, first, local = self._parts(ins, outs, sems)
        sib = (x, y, 1 - c)
        for k in range(len(ins)):
            copy(k, 0, sib, (x, y, c)).wait_recv()
            copy(k, 4, (1 - x, y, 1 - c), (x, y, c)).wait_recv()
            copy(k, 5, (x, 1 - y, 1 - c), (x, y, c)).wait_recv()
            copy(k, 6, (1 - x, 1 - y, 1 - c), (x, y, c)).wait_recv()
        for k in range(len(ins)):
            for cp in first(k):
                cp.wait_send()
            for s in (3, 4, 5, 6):
                copy(k, s, near, sib).wait_send()
            local(k).wait()


class _Swap:
    def __init__(self, parts):
        self.arrays = list(parts)
        nw = len(parts)
        self.out_shape = [jax.ShapeDtypeStruct((N_CHIP,) + p.shape[1:], p.dtype) for p in parts]
        self.scratch = [pltpu.SemaphoreType.DMA((nw, N_CHIP)), pltpu.SemaphoreType.DMA((nw, N_CHIP))]

    def _copy(self, ins, outs, sems, k, q):
        x, y, c = _place()
        return pltpu.make_async_remote_copy(src_ref=ins[k].at[2 * q + 1 - c], dst_ref=outs[k].at[q], send_sem=sems[0].at[k, q],
                                            recv_sem=sems[1].at[k, q], device_id=(x, y, 1 - c), device_id_type=MESH)

    mid_frac = None

    def start(self, ins, outs, sems):
        for k in range(len(ins)):
            for q in range(N_CHIP):
                self._copy(ins, outs, sems, k, q).start()

    def finish(self, ins, outs, sems):
        for k in range(len(ins)):
            for q in range(N_CHIP):
                self._copy(ins, outs, sems, k, q).wait()


class _Ici:
    mid_frac = None

    def __init__(self, sums):
        self.arrays = list(sums)
        nw = len(sums)
        self.out_shape = [jax.ShapeDtypeStruct(s.shape, s.dtype) for s in sums]
        self.scratch = [pltpu.SemaphoreType.DMA((nw, 3)), pltpu.SemaphoreType.DMA((nw, 3)), pltpu.SemaphoreType.DMA((nw,))]

    def _copies(self, ins, outs, sems, k):
        x, y, c = _place()
        myq = 2 * x + y
        out = []
        for r in range(1, N_CHIP):
            px, py = _flip(x, r & 2), _flip(y, r & 1)
            pq = 2 * px + py
            mk = lambda dst: pltpu.make_async_remote_copy(src_ref=ins[k].at[pq], dst_ref=dst, send_sem=sems[0].at[k, r - 1],
                                                          recv_sem=sems[1].at[k, r - 1], device_id=(px, py, c), device_id_type=MESH)
            out.append((mk(outs[k].at[myq]), mk(outs[k].at[pq])))
        return out, pltpu.make_async_copy(ins[k].at[myq], outs[k].at[myq], sems[2].at[k])

    def start(self, ins, outs, sems):
        for k in range(len(ins)):
            remote, local = self._copies(ins, outs, sems, k)
            local.start()
            for snd, _ in remote:
                snd.start()

    def finish(self, ins, outs, sems):
        for k in range(len(ins)):
            remote, local = self._copies(ins, outs, sems, k)
            for snd, rcv in remote:
                rcv.wait_recv()
                snd.wait_send()
            local.wait()


class _Spread:
    mid_frac = None

    def __init__(self, arrays):
        self.arrays = list(arrays)
        nw = len(arrays)
        self.out_shape = [jax.ShapeDtypeStruct((N_DEV,) + a.shape, a.dtype) for a in arrays]
        self.scratch = [pltpu.SemaphoreType.DMA((nw, 7)), pltpu.SemaphoreType.DMA((nw, 7)), pltpu.SemaphoreType.DMA((nw,))]

    def _copies(self, ins, outs, sems, k):
        x, y, c = _place()
        me = 4 * x + 2 * y + c
        out = []
        for r in range(1, N_DEV):
            px, py, pc = _flip(x, r & 4), _flip(y, r & 2), _flip(c, r & 1)
            peer = 4 * px + 2 * py + pc
            mk = lambda dst: pltpu.make_async_remote_copy(src_ref=ins[k], dst_ref=dst, send_sem=sems[0].at[k, r - 1],
                                                          recv_sem=sems[1].at[k, r - 1], device_id=(px, py, pc), device_id_type=MESH)
            out.append((mk(outs[k].at[me]), mk(outs[k].at[peer])))
        return out, pltpu.make_async_copy(ins[k], outs[k].at[me], sems[2].at[k])

    def start(self, ins, outs, sems):
        for k in range(len(ins)):
            remote, local = self._copies(ins, outs, sems, k)
            local.start()
            for snd, _ in remote:
                snd.start()

    def finish(self, ins, outs, sems):
        for k in range(len(ins)):
            remote, local = self._copies(ins, outs, sems, k)
            for snd, rcv in remote:
                rcv.wait_recv()
                snd.wait_send()
            local.wait()


def _call(body, *, grid, in_specs, out_specs, out_shape, name, args, scratch=(), comm=()):
    comm = list(comm)
    n_in, n_out, n_scr = len(in_specs), len(out_specs), len(scratch)
    total = math.prod(grid) if grid else 1

    def wrapped(*refs):
        p = n_in
        cin = []
        for cm in comm:
            cin.append(refs[p:p + len(cm.arrays)])
            p += len(cm.arrays)
        own_out = refs[p:p + n_out]
        p += n_out
        cout = []
        for cm in comm:
            cout.append(refs[p:p + len(cm.arrays)])
            p += len(cm.arrays)
        own_scr = refs[p:p + n_scr]
        p += n_scr
        csem = []
        for cm in comm:
            csem.append(refs[p:p + len(cm.scratch)])
            p += len(cm.scratch)
        step = 0
        for axis, g in enumerate(grid):
            step = step * g + pl.program_id(axis)

        def at(when, what):
            if total == 1:
                what()
            else:
                pl.when(step == when)(what)

        def starts():
            for cm, i, o, s in zip(comm, cin, cout, csem):
                cm.start(i, o, s)

        def finishes():
            for cm, i, o, s in zip(comm, cin, cout, csem):
                cm.finish(i, o, s)

        if comm:
            at(0, starts)
        if body is not None:
            body(*refs[:n_in], *own_out, *own_scr)
        for cm, i, o, s in zip(comm, cin, cout, csem):
            if cm.mid_frac is not None:
                at(min(total - 1, int(total * cm.relay_frac)), lambda cm=cm, i=i, o=o, s=s: cm.relay(i, o, s))
                at(min(total - 1, int(total * cm.mid_frac)), lambda cm=cm, i=i, o=o, s=s: cm.mid(i, o, s))
        if comm:
            at(total - 1, finishes)

    kw = dict(grid=tuple(grid)) if grid else {}
    outs = pl.pallas_call(
        wrapped, name=name, **kw,
        in_specs=list(in_specs) + [ANY for cm in comm for _ in cm.arrays],
        out_specs=list(out_specs) + [ANY for cm in comm for _ in cm.arrays],
        out_shape=list(out_shape) + [s for cm in comm for s in cm.out_shape],
        scratch_shapes=list(scratch) + [s for cm in comm for s in cm.scratch],
        compiler_params=pltpu.CompilerParams(dimension_semantics=("arbitrary",) * len(grid), vmem_limit_bytes=VMEM_LIMIT),
    )(*args, *[a for cm in comm for a in cm.arrays])
    own, p, per = list(outs[:n_out]), n_out, []
    for cm in comm:
        per.append(list(outs[p:p + len(cm.arrays)]))
        p += len(cm.arrays)
    return own, per


def _comm_only(cm, name):
    return _call(None, grid=(), in_specs=[], out_specs=[], out_shape=[], name=name, args=[], comm=[cm])[1][0]


def _rms_fwd(x, g, name, comm=()):
    S, D = x.shape
    tm = _tile(S, 512)

    def body(x_ref, g_ref, o_ref):
        xv = x_ref[...]
        r = lax.rsqrt(jnp.mean(xv * xv, axis=-1, keepdims=True) + NORM_EPS)
        o_ref[...] = (xv * r * g_ref[...]).astype(BF)

    return _call(body, grid=(S // tm,), name=name, args=[x, g], comm=comm,
                 in_specs=[pl.BlockSpec((tm, D), lambda i: (i, 0)), pl.BlockSpec((1, D), lambda i: (0, 0))],
                 out_specs=[pl.BlockSpec((tm, D), lambda i: (i, 0))], out_shape=[jax.ShapeDtypeStruct((S, D), BF)])


def _ffn_up(h, wg, wu, name, comm=()):
    S, D = h.shape
    nb, Fb, _ = wg.shape
    tm = _tile(S, FFN_ROWS)

    def body(h_ref, wg_ref, wu_ref, g_ref, u_ref, a_ref):
        hv = h_ref[...]
        g = _nt(hv, wg_ref[0])
        u = _nt(hv, wu_ref[0])
        g_ref[0] = g.astype(BF)
        u_ref[0] = u.astype(BF)
        a_ref[0] = (g * jax.nn.sigmoid(g) * u).astype(BF)

    act = pl.BlockSpec((1, tm, Fb), lambda j, i: (j, i, 0))
    w = pl.BlockSpec((1, Fb, D), lambda j, i: (j, 0, 0))
    shp = jax.ShapeDtypeStruct((nb, S, Fb), BF)
    return _call(body, grid=(nb, S // tm), name=name, args=[h, wg, wu], comm=comm,
                 in_specs=[pl.BlockSpec((tm, D), lambda j, i: (i, 0)), w, w], out_specs=[act, act, act], out_shape=[shp, shp, shp])


def _ffn_gate(h, wg, name, comm=()):
    S, D = h.shape
    nb, Fb, _ = wg.shape
    tm = _tile(S, FFN_ROWS)

    def body(h_ref, wg_ref, g_ref):
        g_ref[0] = _nt(h_ref[...], wg_ref[0]).astype(BF)

    act = pl.BlockSpec((1, tm, Fb), lambda j, i: (j, i, 0))
    return _call(body, grid=(nb, S // tm), name=name, args=[h, wg], comm=comm,
                 in_specs=[pl.BlockSpec((tm, D), lambda j, i: (i, 0)), pl.BlockSpec((1, Fb, D), lambda j, i: (j, 0, 0))],
                 out_specs=[act], out_shape=[jax.ShapeDtypeStruct((nb, S, Fb), BF)])


def _ffn_up_act(h, wu, g, name, comm=()):
    S, D = h.shape
    nb, Fb, _ = wu.shape
    tm = _tile(S, FFN_ROWS)

    def body(h_ref, wu_ref, g_ref, u_ref, a_ref):
        u = _nt(h_ref[...], wu_ref[0])
        gv = g_ref[0].astype(F32)
        u_ref[0] = u.astype(BF)
        a_ref[0] = (gv * jax.nn.sigmoid(gv) * u).astype(BF)

    act = pl.BlockSpec((1, tm, Fb), lambda j, i: (j, i, 0))
    shp = jax.ShapeDtypeStruct((nb, S, Fb), BF)
    return _call(body, grid=(nb, S // tm), name=name, args=[h, wu, g], comm=comm,
                 in_specs=[pl.BlockSpec((tm, D), lambda j, i: (i, 0)), pl.BlockSpec((1, Fb, D), lambda j, i: (j, 0, 0)), act],
                 out_specs=[act, act], out_shape=[shp, shp])


def _ffn_down_norm(a, wd, x, gn, name, comm=()):
    nb, S, Fb = a.shape
    D = wd.shape[2]
    tm = _tile(S, 512)

    nj = nb // FFN_PAIR

    def body(a_ref, wd_ref, x_ref, gn_ref, xo_ref, hn_ref, acc_ref):
        j = pl.program_id(1)

        @pl.when(j == 0)
        def _():
            acc_ref[...] = jnp.zeros_like(acc_ref)

        _acc_dots(acc_ref, [(a_ref[b], lambda cols, b=b: wd_ref[b, :, cols]) for b in range(FFN_PAIR)])

        @pl.when(j == nj - 1)
        def _():
            def chunk(t, carry):
                rows = pl.ds(pl.multiple_of(t * EPI_ROWS, EPI_ROWS), EPI_ROWS)
                xo = x_ref[rows, :] + 0.5 * acc_ref[rows, :]
                r = lax.rsqrt(jnp.mean(xo * xo, axis=-1, keepdims=True) + NORM_EPS)
                xo_ref[rows, :] = xo
                hn_ref[rows, :] = (xo * r * gn_ref[...]).astype(BF)
                return carry

            lax.fori_loop(0, tm // EPI_ROWS, chunk, 0)

    row = pl.BlockSpec((tm, D), lambda i, j: (i, 0))
    return _call(body, grid=(S // tm, nj), name=name, args=[a, wd, x, gn], comm=comm,
                 in_specs=[pl.BlockSpec((FFN_PAIR, tm, Fb), lambda i, j: (j, i, 0)), pl.BlockSpec((FFN_PAIR, Fb, D), lambda i, j: (j, 0, 0)),
                           _once((tm, D), lambda i, j: (i, 0)), pl.BlockSpec((1, D), lambda i, j: (0, 0))],
                 out_specs=[row, row], out_shape=[jax.ShapeDtypeStruct((S, D), F32), jax.ShapeDtypeStruct((S, D), BF)],
                 scratch=[pltpu.VMEM((tm, D), F32)])


def _ffn_down_loss(a, wd, x, gf, tgt, name):
    nb, S, Fb = a.shape
    D = wd.shape[2]
    tm = _tile(S, 512)

    nj = nb // FFN_PAIR

    def body(a_ref, wd_ref, x_ref, gf_ref, t_ref, dx_ref, dxb_ref, dgf_ref, loss_ref, acc_ref):
        i, j = pl.program_id(0), pl.program_id(1)

        @pl.when(j == 0)
        def _():
            acc_ref[...] = jnp.zeros_like(acc_ref)

        _acc_dots(acc_ref, [(a_ref[b], lambda cols, b=b: wd_ref[b, :, cols]) for b in range(FFN_PAIR)])

        @pl.when((j == nj - 1) & (i == 0))
        def _():
            dgf_ref[...] = jnp.zeros_like(dgf_ref)
            loss_ref[...] = jnp.zeros_like(loss_ref)

        @pl.when(j == nj - 1)
        def _():
            def chunk(t, carry):
                rows = pl.ds(pl.multiple_of(t * EPI_ROWS, EPI_ROWS), EPI_ROWS)
                xo = x_ref[rows, :] + 0.5 * acc_ref[rows, :]
                r = lax.rsqrt(jnp.mean(xo * xo, axis=-1, keepdims=True) + NORM_EPS)
                xh = xo * r
                gf = gf_ref[...]
                e = xh * gf - t_ref[rows, :]
                loss_ref[...] += jnp.sum(jnp.mean(e * e, axis=-1, keepdims=True), axis=0, keepdims=True) * 0.5
                dy = e * (1.0 / D)
                dgf_ref[...] += jnp.sum(dy * xh, axis=0, keepdims=True)
                dxh = dy * gf
                dx = r * (dxh - xh * jnp.mean(dxh * xh, axis=-1, keepdims=True))
                dx_ref[rows, :] = dx
                dxb_ref[rows, :] = (0.5 * dx).astype(BF)
                return carry

            lax.fori_loop(0, tm // EPI_ROWS, chunk, 0)

    row = pl.BlockSpec((tm, D), lambda i, j: (i, 0))
    once = _once((tm, D), lambda i, j: (i, 0))
    vec = pl.BlockSpec((1, D), lambda i, j: (0, 0))
    return _call(body, grid=(S // tm, nj), name=name, args=[a, wd, x, gf, tgt],
                 in_specs=[pl.BlockSpec((FFN_PAIR, tm, Fb), lambda i, j: (j, i, 0)), pl.BlockSpec((FFN_PAIR, Fb, D), lambda i, j: (j, 0, 0)),
                           once, vec, once],
                 out_specs=[row, row, vec, pl.BlockSpec((1, 128), lambda i, j: (0, 0))],
                 out_shape=[jax.ShapeDtypeStruct((S, D), F32), jax.ShapeDtypeStruct((S, D), BF), jax.ShapeDtypeStruct((1, D), F32),
                            jax.ShapeDtypeStruct((1, 128), F32)],
                 scratch=[pltpu.VMEM((tm, D), F32)])[0]


def _ffn_bwd_act(dyb, wd, g, u, name, comm=()):
    S, D = dyb.shape
    nb, Fb, _ = wd.shape
    tm = _tile(S, FFN_ROWS)

    def body(dy_ref, wd_ref, g_ref, u_ref, dg_ref, du_ref):
        da = _nt(dy_ref[...], wd_ref[0])
        gv = g_ref[0].astype(F32)
        uv = u_ref[0].astype(F32)
        sg = jax.nn.sigmoid(gv)
        du_ref[0] = (da * gv * sg).astype(BF)
        dg_ref[0] = (da * uv * sg * (1.0 + gv * (1.0 - sg))).astype(BF)

    act = pl.BlockSpec((1, tm, Fb), lambda j, i: (j, i, 0))
    shp = jax.ShapeDtypeStruct((nb, S, Fb), BF)
    return _call(body, grid=(nb, S // tm), name=name, args=[dyb, wd, g, u], comm=comm,
                 in_specs=[pl.BlockSpec((tm, D), lambda j, i: (i, 0)), pl.BlockSpec((1, Fb, D), lambda j, i: (j, 0, 0)), act, act],
                 out_specs=[act, act], out_shape=[shp, shp])


def _ffn_dwd(a, dyb, name, comm=()):
    nb, S, Fb = a.shape
    D = dyb.shape[1]
    ts = _tile(S, FFN_ROWS)
    ns = S // ts

    def body(a_ref, dy_ref, o_ref, acc_ref):
        s = pl.program_id(1)

        @pl.when(s == 0)
        def _():
            acc_ref[...] = jnp.zeros_like(acc_ref)

        acc_ref[...] += _tn(a_ref[0], dy_ref[...])

        @pl.when(s == ns - 1)
        def _():
            o_ref[0] = acc_ref[...].astype(BF)

    return _call(body, grid=(nb, ns), name=name, args=[a, dyb], comm=comm,
                 in_specs=[pl.BlockSpec((1, ts, Fb), lambda j, s: (j, s, 0)), pl.BlockSpec((ts, D), lambda j, s: (s, 0))],
                 out_specs=[pl.BlockSpec((1, Fb, D), lambda j, s: (j, 0, 0))], out_shape=[jax.ShapeDtypeStruct((nb, Fb, D), BF)],
                 scratch=[pltpu.VMEM((Fb, D), F32)])


def _ffn_dwgu(h, dg, du, name, comm=()):
    S, D = h.shape
    nb, _, Fb = dg.shape
    ts = _tile(S, FFN_ROWS)
    ns = S // ts

    def body(h_ref, dg_ref, du_ref, og_ref, ou_ref, accg_ref, accu_ref):
        s = pl.program_id(1)

        @pl.when(s == 0)
        def _():
            accg_ref[...] = jnp.zeros_like(accg_ref)
            accu_ref[...] = jnp.zeros_like(accu_ref)

        hv = h_ref[...]
        accg_ref[...] += _tn(dg_ref[0], hv)
        accu_ref[...] += _tn(du_ref[0], hv)

        @pl.when(s == ns - 1)
        def _():
            og_ref[0] = accg_ref[...].astype(BF)
            ou_ref[0] = accu_ref[...].astype(BF)

    act = pl.BlockSpec((1, ts, Fb), lambda j, s: (j, s, 0))
    out = pl.BlockSpec((1, Fb, D), lambda j, s: (j, 0, 0))
    shp = jax.ShapeDtypeStruct((nb, Fb, D), BF)
    return _call(body, grid=(nb, ns), name=name, args=[h, dg, du], comm=comm,
                 in_specs=[pl.BlockSpec((ts, D), lambda j, s: (s, 0)), act, act], out_specs=[out, out], out_shape=[shp, shp],
                 scratch=[pltpu.VMEM((Fb, D), F32), pltpu.VMEM((Fb, D), F32)])


def _dh_rms_bwd(pairs, blocked, tk, x, gn, dxo, out_scale, name, comm=(), tail=None):
    S, D = x.shape
    nk = pairs[0][0].shape[0] if blocked else pairs[0][0].shape[1] // tk
    tm = _tile(S, 512)
    npair = len(pairs)
    assert blocked or (npair == 1 and tail is not None and nk >= 2)
    nin = 2 * npair + (0 if blocked else 1)

    def body(*refs):
        ins = refs[:nin]
        x_ref, gn_ref, dxo_ref, dx_ref, dxb_ref, dgn_ref, acc_ref = refs[nin:]
        i, k = pl.program_id(0), pl.program_id(1)

        @pl.when(k == 0)
        def _():
            acc_ref[...] = jnp.zeros_like(acc_ref)

        if blocked:
            _acc_dots(acc_ref, [(ins[2 * p][0], lambda cols, r=ins[2 * p + 1]: r[0, :, cols]) for p in range(npair)])
        else:
            @pl.when(k < nk - 1)
            def _():
                _acc_dots(acc_ref, [(ins[0][...], lambda cols: ins[1][:, cols])])

            @pl.when(k == nk - 1)
            def _():
                _acc_dots(acc_ref, [(ins[0][...], lambda cols: ins[2][:, cols])])

        @pl.when((k == nk - 1) & (i == 0))
        def _():
            dgn_ref[...] = jnp.zeros_like(dgn_ref)

        @pl.when(k == nk - 1)
        def _():
            def chunk(t, carry):
                rows = pl.ds(pl.multiple_of(t * EPI_ROWS, EPI_ROWS), EPI_ROWS)
                xv = x_ref[rows, :]
                r = lax.rsqrt(jnp.mean(xv * xv, axis=-1, keepdims=True) + NORM_EPS)
                xh = xv * r
                dh = acc_ref[rows, :]
                dgn_ref[...] += jnp.sum(dh * xh, axis=0, keepdims=True)
                dxh = dh * gn_ref[...]
                dx = dxo_ref[rows, :] + r * (dxh - xh * jnp.mean(dxh * xh, axis=-1, keepdims=True))
                dx_ref[rows, :] = dx
                dxb_ref[rows, :] = (out_scale * dx).astype(BF)
                return carry

            lax.fori_loop(0, tm // EPI_ROWS, chunk, 0)

    if blocked:
        mats = [pl.BlockSpec((1, tm, tk), lambda i, k: (k, i, 0)), pl.BlockSpec((1, tk, D), lambda i, k: (k, 0, 0))] * npair
        flat = [t for pr in pairs for t in pr]
    else:
        mats = [pl.BlockSpec((tm, tk), lambda i, k: (i, k)), pl.BlockSpec((tk, D), lambda i, k: (jnp.minimum(k, nk - 2), 0)),
                pl.BlockSpec((tk, D), lambda i, k: (0, 0))]
        flat = [*pairs[0], tail]
    row = pl.BlockSpec((tm, D), lambda i, k: (i, 0))
    once = _once((tm, D), lambda i, k: (i, 0))
    vec = pl.BlockSpec((1, D), lambda i, k: (0, 0))
    return _call(body, grid=(S // tm, nk), name=name, args=[*flat, x, gn, dxo], comm=comm,
                 in_specs=mats + [once, vec, once], out_specs=[row, row, vec],
                 out_shape=[jax.ShapeDtypeStruct((S, D), F32), jax.ShapeDtypeStruct((S, D), BF), jax.ShapeDtypeStruct((1, D), F32)],
                 scratch=[pltpu.VMEM((tm, D), F32)])


def _mm_nt(a, bT, tm, tn, col0, col1, dtype, name, comm=()):
    M, K = a.shape
    n0, nn = col0 // tn, (col1 - col0) // tn

    def body(a_ref, b_ref, o_ref):
        o_ref[...] = _nt(a_ref[...], b_ref[...]).astype(dtype)

    return _call(body, grid=(nn, M // tm), name=name, args=[a, bT], comm=comm,
                 in_specs=[pl.BlockSpec((tm, K), lambda n, i: (i, 0)), pl.BlockSpec((tn, K), lambda n, i: (n0 + n, 0))],
                 out_specs=[pl.BlockSpec((tm, tn), lambda n, i: (i, n))], out_shape=[jax.ShapeDtypeStruct((M, nn * tn), dtype)])


def _mm_tn(a, b, tm, tn, ts, blocked, name, comm=(), mrows=None):
    S, M = a.shape[0], (a.shape[1] if mrows is None else mrows)
    N = b.shape[1]
    ns = S // ts
    per_tile = tn // blocked if blocked else 0

    def body(a_ref, b_ref, o_ref, acc_ref):
        s = pl.program_id(2)

        @pl.when(s == 0)
        def _():
            acc_ref[...] = jnp.zeros_like(acc_ref)

        acc_ref[...] += _tn(a_ref[...], b_ref[...])

        @pl.when(s == ns - 1)
        def _():
            if blocked:
                for t in range(per_tile):
                    o_ref[t] = acc_ref[:, t * blocked:(t + 1) * blocked].astype(BF)
            else:
                o_ref[...] = acc_ref[...].astype(BF)

    if blocked:
        ospec = pl.BlockSpec((per_tile, tm, blocked), lambda i, n, s: (n, i, 0))
        oshape = jax.ShapeDtypeStruct((N // blocked, M, blocked), BF)
    else:
        ospec = pl.BlockSpec((tm, tn), lambda i, n, s: (i, n))
        oshape = jax.ShapeDtypeStruct((M, N), BF)
    return _call(body, grid=(M // tm, N // tn, ns), name=name, args=[a, b], comm=comm,
                 in_specs=[pl.BlockSpec((ts, tm), lambda i, n, s: (s, i)), pl.BlockSpec((ts, tn), lambda i, n, s: (s, n))],
                 out_specs=[ospec], out_shape=[oshape], scratch=[pltpu.VMEM((tm, tn), F32)])


def _rope_tables(S):
    half = ROPE_DIM // 2
    inv_freq = ROPE_THETA ** (-jnp.arange(0, ROPE_DIM, 2, dtype=F32) / ROPE_DIM)
    ang = jnp.arange(S, dtype=F32)[:, None] * inv_freq[None, :]
    cos, sin = jnp.cos(ang), jnp.sin(ang)
    zeros = jnp.zeros((S, HEAD_DIM - ROPE_DIM), F32)
    c = jnp.concatenate([cos, cos, jnp.ones((S, HEAD_DIM - ROPE_DIM), F32)], axis=1)
    sm = jnp.concatenate([-sin, jnp.zeros((S, half), F32), zeros], axis=1)
    sp = jnp.concatenate([jnp.zeros((S, half), F32), sin, zeros], axis=1)
    return c, sm, sp


def _rope(t, c, sm, sp):
    return t * c + pltpu.roll(t, HEAD_DIM - ROPE_DIM // 2, 1) * sm + pltpu.roll(t, ROPE_DIM // 2, 1) * sp


def _rope_t(dy, c, sm, sp):
    return dy * c + pltpu.roll(dy * sm, ROPE_DIM // 2, 1) + pltpu.roll(dy * sp, HEAD_DIM - ROPE_DIM // 2, 1)


def _att_mask(i):
    qi = lax.broadcasted_iota(jnp.int32, (BLK, 2 * BLK), 0)
    kj = lax.broadcasted_iota(jnp.int32, (BLK, 2 * BLK), 1)
    diff = qi + BLK - kj
    first_key = jnp.where(i > 0, 0, BLK)
    return (diff >= 0) & (diff <= BLK) & (kj >= first_key)


def _res_rows(r, i, n, d):
    if d == 1:
        return pl.ds(pl.multiple_of(i * n, n), n)
    return pl.ds(r + i * (n * d), n, stride=d)


def _att_specs(S, gi):
    def sect(off):
        base = (off + gi * GROUP_W) // HEAD_DIM
        return _once((S, HEAD_DIM), lambda hh: (0, base + hh))

    tab = pl.BlockSpec((S, HEAD_DIM), lambda hh: (0, 0))
    head = pl.BlockSpec((S, HEAD_DIM), lambda hh: (0, hh))
    return sect, tab, head


def _each_residue(d, fn):
    if d == 1:
        fn(0)
    else:
        lax.fori_loop(0, d, lambda r, carry: (fn(r), carry)[1], 0)


def _att_fwd(qkv, tabs, gi, d, name, comm=()):
    S = qkv.shape[0]
    L = S // d
    sect, tab, head = _att_specs(S, gi)
    nblk = L // BLK
    scale = HEAD_DIM ** -0.5

    def body(q_ref, k_ref, v_ref, c_ref, sm_ref, sp_ref, o_ref, lse_ref, qr, kp, vp):
        kp[pl.ds(0, BLK), :] = jnp.zeros((BLK, HEAD_DIM), BF)
        vp[pl.ds(0, BLK), :] = jnp.zeros((BLK, HEAD_DIM), BF)

        def residue(r):
            res = _res_rows(r, 0, L, d)
            c, sm, sp = c_ref[res, :], sm_ref[res, :], sp_ref[res, :]
            qr[...] = _rope(q_ref[res, :], c, sm, sp).astype(BF)
            kp[pl.ds(BLK, L), :] = _rope(k_ref[res, :], c, sm, sp).astype(BF)
            vp[pl.ds(BLK, L), :] = v_ref[res, :].astype(BF)

            def blk(i, carry):
                r0 = pl.multiple_of(i * BLK, BLK)
                s = _nt(qr[pl.ds(r0, BLK), :], kp[pl.ds(r0, 2 * BLK), :]) * scale
                s = jnp.where(_att_mask(i), s, NEG)
                m = jnp.max(s, axis=-1, keepdims=True)
                p = jnp.exp(s - m)
                l = jnp.sum(p, axis=-1, keepdims=True)
                out = _res_rows(r, i, BLK, d)
                o_ref[out, :] = _nn(p.astype(BF), vp[pl.ds(r0, 2 * BLK), :]) / l
                lse_ref[out, :] = jnp.broadcast_to(m + jnp.log(l), (BLK, HEAD_DIM))
                return carry

            lax.fori_loop(0, nblk, blk, 0, unroll=min(4, nblk))

        _each_residue(d, residue)

    shp = jax.ShapeDtypeStruct((S, GROUP_W), F32)
    return _call(body, grid=(HEADS_PER_GROUP,), name=name, args=[qkv, qkv, qkv, *tabs], comm=comm,
                 in_specs=[sect(Q_OFF), sect(K_OFF), sect(V_OFF), tab, tab, tab], out_specs=[head, head], out_shape=[shp, shp],
                 scratch=[pltpu.VMEM((L, HEAD_DIM), BF), pltpu.VMEM((L + BLK, HEAD_DIM), BF), pltpu.VMEM((L + BLK, HEAD_DIM), BF)])


def _att_combine(os, lses, name):
    S = os[0].shape[0]
    tm = _tile(S, 512)

    def body(o0, o1, o2, l0, l1, l2, oa_ref, lse_ref):
        a, b, c = l0[...], l1[...], l2[...]
        mx = jnp.maximum(jnp.maximum(a, b), c)
        wa, wb, wc = jnp.exp(a - mx), jnp.exp(b - mx), jnp.exp(c - mx)
        den = wa + wb + wc
        oa_ref[...] = ((wa * o0[...] + wb * o1[...] + wc * o2[...]) / den).astype(BF)
        lse_ref[...] = mx + jnp.log(den)

    row = pl.BlockSpec((tm, GROUP_W), lambda i: (i, 0))
    return _call(body, grid=(S // tm,), name=name, args=[*os, *lses], in_specs=[row] * 6, out_specs=[row, row],
                 out_shape=[jax.ShapeDtypeStruct((S, GROUP_W), BF), jax.ShapeDtypeStruct((S, GROUP_W), F32)])[0]


def _att_bwd(qkv, tabs, do, lse, dvec, gi, d, name, comm=()):
    S = qkv.shape[0]
    L = S // d
    sect, tab, head = _att_specs(S, gi)
    stat = _once((S, HEAD_DIM), lambda hh: (0, hh))
    nblk = L // BLK
    scale = HEAD_DIM ** -0.5

    def body(q_ref, k_ref, v_ref, c_ref, sm_ref, sp_ref, do_ref, lse_ref, dv_ref, dq_out, dk_out, dv_out, qr, kp, vp, dkp, dvp):
        kp[pl.ds(0, BLK), :] = jnp.zeros((BLK, HEAD_DIM), BF)
        vp[pl.ds(0, BLK), :] = jnp.zeros((BLK, HEAD_DIM), BF)

        def residue(r):
            res = _res_rows(r, 0, L, d)
            c, sm, sp = c_ref[res, :], sm_ref[res, :], sp_ref[res, :]
            qr[...] = _rope(q_ref[res, :], c, sm, sp).astype(BF)
            kp[pl.ds(BLK, L), :] = _rope(k_ref[res, :], c, sm, sp).astype(BF)
            vp[pl.ds(BLK, L), :] = v_ref[res, :].astype(BF)
            dkp[...] = jnp.zeros_like(dkp)
            dvp[...] = jnp.zeros_like(dvp)

            def blk(i, carry):
                r0 = pl.multiple_of(i * BLK, BLK)
                rows, win, pos = pl.ds(r0, BLK), pl.ds(r0, 2 * BLK), _res_rows(r, i, BLK, d)
                q, kw, vw, dob = qr[rows, :], kp[win, :], vp[win, :], do_ref[pos, :].astype(BF)
                s = jnp.where(_att_mask(i), _nt(q, kw) * scale, NEG)
                p = jnp.exp(s - lse_ref[pos, :][:, :1])
                ds = p * (_nt(dob, vw) - dv_ref[pos, :][:, :1]) * scale
                dsb = ds.astype(BF)
                dq_out[pos, :] = _rope_t(_nn(dsb, kw), c_ref[pos, :], sm_ref[pos, :], sp_ref[pos, :])
                dkp[win, :] += _tn(dsb, q)
                dvp[win, :] += _tn(p.astype(BF), dob)
                return carry

            lax.fori_loop(0, nblk, blk, 0, unroll=2)
            dk_out[res, :] = _rope_t(dkp[pl.ds(BLK, L), :], c, sm, sp)
            dv_out[res, :] = dvp[pl.ds(BLK, L), :]

        _each_residue(d, residue)

    shp = jax.ShapeDtypeStruct((S, GROUP_W), F32)
    return _call(body, grid=(HEADS_PER_GROUP,), name=name, args=[qkv, qkv, qkv, *tabs, do, lse, dvec], comm=comm,
                 in_specs=[sect(Q_OFF), sect(K_OFF), sect(V_OFF), tab, tab, tab, stat, stat, stat],
                 out_specs=[head, head, head], out_shape=[shp, shp, shp],
                 scratch=[pltpu.VMEM((L, HEAD_DIM), BF), pltpu.VMEM((L + BLK, HEAD_DIM), BF), pltpu.VMEM((L + BLK, HEAD_DIM), BF),
                          pltpu.VMEM((L + BLK, HEAD_DIM), F32), pltpu.VMEM((L + BLK, HEAD_DIM), F32)])


def _sg_parts(u_ref, vs_ref, g_ref, b_ref):
    uv = u_ref[...].astype(F32)
    vv = vs_ref[...].astype(F32)
    vg = _gelu(vv)
    mu = jnp.mean(vg, axis=-1, keepdims=True)
    vc = vg - mu
    rs = lax.rsqrt(jnp.mean(vc * vc, axis=-1, keepdims=True) + LN_EPS)
    y = vc * rs
    return uv, vv, rs, y, y * g_ref[...] + b_ref[...]


def _sg_wmask():
    t = lax.broadcasted_iota(jnp.int32, (BLK, BLK), 0)
    s = lax.broadcasted_iota(jnp.int32, (BLK, BLK), 1)
    return s <= t


def _sg_fwd(proj, sgw, sgbT, lng, lnb, name):
    S, P = proj.shape

    def body(u_ref, vs_ref, w_ref, bt_ref, g_ref, b_ref, z_ref):
        uv, _, _, _, vln = _sg_parts(u_ref, vs_ref, g_ref, b_ref)
        ug = _gelu(uv)
        vb = vln.astype(BF)
        mask = _sg_wmask()
        bt = bt_ref[...]
        for g in range(SG_GROUPS):
            cols = slice(g * BLK, (g + 1) * BLK)
            w = jnp.where(mask, w_ref[g], 0.0).astype(BF)
            sp = _nn(w, vb[:, cols]) + bt[:, g:g + 1]
            z_ref[:, cols] = (ug[:, cols] * sp).astype(BF)

    tile = lambda off: pl.BlockSpec((BLK, SG_W), lambda i: (i, off // SG_W))
    full = lambda shape: pl.BlockSpec(shape, lambda i: (0,) * len(shape))
    return _call(body, grid=(S // BLK,), name=name, args=[proj, proj, sgw, sgbT, lng, lnb],
                 in_specs=[tile(R_U), tile(R_VS), full((SG_GROUPS, BLK, BLK)), full((BLK, BLK)), full((1, SG_W)), full((1, SG_W))],
                 out_specs=[pl.BlockSpec((BLK, SG_W), lambda i: (i, 0))], out_shape=[jax.ShapeDtypeStruct((S, SG_W), BF)])[0][0]


def _sg_bwd(proj, dz, sgw, sgbT, lng, lnb, name):
    S, P = proj.shape

    def body(u_ref, vs_ref, dz_ref, w_ref, bt_ref, g_ref, b_ref, du_ref, dvs_ref, dw_ref, dbt_ref, dg_ref, db_ref, dvln):
        @pl.when(pl.program_id(0) == 0)
        def _():
            dw_ref[...] = jnp.zeros_like(dw_ref)
            dbt_ref[...] = jnp.zeros_like(dbt_ref)
            dg_ref[...] = jnp.zeros_like(dg_ref)
            db_ref[...] = jnp.zeros_like(db_ref)

        uv, vv, rs, y, vln = _sg_parts(u_ref, vs_ref, g_ref, b_ref)
        ug = _gelu(uv)
        vb = vln.astype(BF)
        dzv = dz_ref[...].astype(F32)
        dsp = dzv * ug
        dspb = dsp.astype(BF)
        mask = _sg_wmask()
        bt = bt_ref[...]
        lane = lax.broadcasted_iota(jnp.int32, (BLK, BLK), 1)
        dbt = jnp.zeros((BLK, BLK), F32)
        for g in range(SG_GROUPS):
            cols = slice(g * BLK, (g + 1) * BLK)
            w = jnp.where(mask, w_ref[g], 0.0).astype(BF)
            sp = _nn(w, vb[:, cols]) + bt[:, g:g + 1]
            du_ref[:, cols] = (dzv[:, cols] * sp * _gelu_grad(uv[:, cols])).astype(BF)
            dw_ref[g] += jnp.where(mask, _nt(dspb[:, cols], vb[:, cols]), 0.0)
            dbt = dbt + jnp.where(lane == g, jnp.sum(dsp[:, cols], axis=-1, keepdims=True), 0.0)
            dvln[:, cols] = _tn(w, dspb[:, cols])
        dbt_ref[...] += dbt
        dvl = dvln[...]
        dg_ref[...] += jnp.sum(dvl * y, axis=0, keepdims=True)
        db_ref[...] += jnp.sum(dvl, axis=0, keepdims=True)
        dy = dvl * g_ref[...]
        dvg = rs * (dy - jnp.mean(dy, axis=-1, keepdims=True) - y * jnp.mean(dy * y, axis=-1, keepdims=True))
        dvs_ref[...] = (dvg * _gelu_grad(vv)).astype(BF)

    tile = lambda off: pl.BlockSpec((BLK, SG_W), lambda i: (i, off // SG_W))
    full = lambda shape: pl.BlockSpec(shape, lambda i: (0,) * len(shape))
    row = pl.BlockSpec((BLK, SG_W), lambda i: (i, 0))
    return _call(body, grid=(S // BLK,), name=name, args=[proj, proj, dz, sgw, sgbT, lng, lnb],
                 in_specs=[tile(R_U), tile(R_VS), row, full((SG_GROUPS, BLK, BLK)), full((BLK, BLK)), full((1, SG_W)), full((1, SG_W))],
                 out_specs=[row, row, full((SG_GROUPS, BLK, BLK)), full((BLK, BLK)), full((1, SG_W)), full((1, SG_W))],
                 out_shape=[jax.ShapeDtypeStruct((S, SG_W), BF), jax.ShapeDtypeStruct((S, SG_W), BF),
                            jax.ShapeDtypeStruct((SG_GROUPS, BLK, BLK), F32), jax.ShapeDtypeStruct((BLK, BLK), F32),
                            jax.ShapeDtypeStruct((1, SG_W), F32), jax.ShapeDtypeStruct((1, SG_W), F32)],
                 scratch=[pltpu.VMEM((BLK, SG_W), F32)])[0]


def _gate_merge(oatt, z, watt, wsg, proj, name, comm=()):
    S = oatt.shape[0]
    nb, _, Db = watt.shape
    D = nb * Db
    tm = _tile(S, 512)
    half = D // 2
    ga, gs = R_GA // half, (R_GA + D) // half

    def body(oa_ref, z_ref, wa_ref, ws_ref, ga0, ga1, gs0, gs1, ya_ref, ys_ref, mg_ref):
        oa, zv = oa_ref[...], z_ref[...]
        for j in range(nb):
            cols = slice(j * Db, (j + 1) * Db)
            g_a, g_s = (ga0, gs0) if j < nb // 2 else (ga1, gs1)
            gcols = slice((j % (nb // 2)) * Db, (j % (nb // 2) + 1) * Db)
            ya = _nn(oa, wa_ref[j])
            ys = _nn(zv, ws_ref[j])
            ya_ref[:, cols] = ya.astype(BF)
            ys_ref[:, cols] = ys.astype(BF)
            mg_ref[:, cols] = (jax.nn.sigmoid(g_a[:, gcols].astype(F32)) * ya + jax.nn.sigmoid(g_s[:, gcols].astype(F32)) * ys).astype(BF)

    out = pl.BlockSpec((tm, D), lambda i: (i, 0))
    gate = lambda b: pl.BlockSpec((tm, half), lambda i: (i, b))
    shp = jax.ShapeDtypeStruct((S, D), BF)
    return _call(body, grid=(S // tm,), name=name, args=[oatt, z, watt, wsg, proj, proj, proj, proj], comm=comm,
                 in_specs=[pl.BlockSpec((tm, GROUP_W), lambda i: (i, 0)), pl.BlockSpec((tm, SG_W), lambda i: (i, 0)),
                           pl.BlockSpec((nb, GROUP_W, Db), lambda i: (0, 0, 0)), pl.BlockSpec((nb, SG_W, Db), lambda i: (0, 0, 0)),
                           gate(ga), gate(ga + 1), gate(gs), gate(gs + 1)],
                 out_specs=[out, out, out], out_shape=[shp, shp, shp])


def _mix_out(merged, wout, x, gn, name):
    S, D = x.shape
    tm = _tile(S, 256)

    def body(m_ref, w_ref, x_ref, gn_ref, xo_ref, hn_ref):
        xo = x_ref[...] + _nn(m_ref[...], w_ref[...])
        r = lax.rsqrt(jnp.mean(xo * xo, axis=-1, keepdims=True) + NORM_EPS)
        xo_ref[...] = xo
        hn_ref[...] = (xo * r * gn_ref[...]).astype(BF)

    row = pl.BlockSpec((tm, D), lambda i: (i, 0))
    return _call(body, grid=(S // tm,), name=name, args=[merged, wout, x, gn],
                 in_specs=[row, pl.BlockSpec((D, D), lambda i: (0, 0)), row, pl.BlockSpec((1, D), lambda i: (0, 0))],
                 out_specs=[row, row], out_shape=[jax.ShapeDtypeStruct((S, D), F32), jax.ShapeDtypeStruct((S, D), BF)])[0]


def _mix_bwd_gate(dmix, wout, ya, ys, proj, name):
    S, D = dmix.shape
    tm, tn = _tile(S, 256), min(512, D // 2)
    half = D // 2
    ga, gs = R_GA // half, (R_GA + D) // half

    def body(dm_ref, w_ref, ya_ref, ys_ref, ga0, ga1, gs0, gs1, dya_ref, dys_ref, dga_ref, dgs_ref):
        dmv = dm_ref[...]
        for c0 in range(0, D, tn):
            cols = slice(c0, c0 + tn)
            g_a, g_s = (ga0, gs0) if c0 < half else (ga1, gs1)
            gcols = slice(c0 % half, c0 % half + tn)
            dm = _nt(dmv, w_ref[cols, :])
            sa = jax.nn.sigmoid(g_a[:, gcols].astype(F32))
            ss = jax.nn.sigmoid(g_s[:, gcols].astype(F32))
            dya_ref[:, cols] = (dm * sa).astype(BF)
            dys_ref[:, cols] = (dm * ss).astype(BF)
            dga_ref[:, cols] = (dm * ya_ref[:, cols].astype(F32) * sa * (1.0 - sa)).astype(BF)
            dgs_ref[:, cols] = (dm * ys_ref[:, cols].astype(F32) * ss * (1.0 - ss)).astype(BF)

    row = pl.BlockSpec((tm, D), lambda i: (i, 0))
    gate = lambda b: pl.BlockSpec((tm, half), lambda i: (i, b))
    shp = jax.ShapeDtypeStruct((S, D), BF)
    return _call(body, grid=(S // tm,), name=name, args=[dmix, wout, ya, ys, proj, proj, proj, proj],
                 in_specs=[row, pl.BlockSpec((D, D), lambda i: (0, 0)), row, row, gate(ga), gate(ga + 1), gate(gs), gate(gs + 1)],
                 out_specs=[row] * 4, out_shape=[shp] * 4)[0]


def _att_sg_dout(dya, dys, watt, wsg, oatt, name, comm=()):
    S, D = dya.shape
    nb, _, Db = watt.shape
    tm = _tile(S, 512)

    def body(dya_ref, dys_ref, wa_ref, ws_ref, oa_ref, do_ref, dz_ref, dvec_ref):
        def back(dy_ref, w_ref, rows):
            tot = None
            for j in range(nb):
                part = _nt(dy_ref[:, j * Db:(j + 1) * Db], w_ref[j, rows, :])
                tot = part if tot is None else tot + part
            return tot

        dov = back(dya_ref, wa_ref, slice(0, GROUP_W))
        do_ref[...] = dov
        for c0 in range(0, SG_W, GROUP_W):
            dz_ref[:, c0:c0 + GROUP_W] = back(dys_ref, ws_ref, slice(c0, c0 + GROUP_W)).astype(BF)
        prod = dov * oa_ref[...].astype(F32)
        for hh in range(HEADS_PER_GROUP):
            cols = slice(hh * HEAD_DIM, (hh + 1) * HEAD_DIM)
            dvec_ref[:, cols] = jnp.broadcast_to(jnp.sum(prod[:, cols], axis=-1, keepdims=True), (tm, HEAD_DIM))

    row = pl.BlockSpec((tm, D), lambda i: (i, 0))
    att = pl.BlockSpec((tm, GROUP_W), lambda i: (i, 0))
    return _call(body, grid=(S // tm,), name=name, args=[dya, dys, watt, wsg, oatt], comm=comm,
                 in_specs=[row, row, pl.BlockSpec((nb, GROUP_W, Db), lambda i: (0, 0, 0)), pl.BlockSpec((nb, SG_W, Db), lambda i: (0, 0, 0)), att],
                 out_specs=[att, pl.BlockSpec((tm, SG_W), lambda i: (i, 0)), att],
                 out_shape=[jax.ShapeDtypeStruct((S, GROUP_W), F32), jax.ShapeDtypeStruct((S, SG_W), BF), jax.ShapeDtypeStruct((S, GROUP_W), F32)])[0]


def _small_allreduce(pack, name):
    R = pack.shape[0]

    def body(p_ref, o_ref, gath, send, recv):
        x, y, c = _place()
        me = 4 * x + 2 * y + c
        gath[me] = p_ref[...]
        copies = []
        for r in range(1, N_DEV):
            px, py, pc = _flip(x, r & 4), _flip(y, r & 2), _flip(c, r & 1)
            peer = 4 * px + 2 * py + pc
            mk = lambda dst: pltpu.make_async_remote_copy(src_ref=p_ref, dst_ref=dst, send_sem=send.at[r - 1], recv_sem=recv.at[r - 1],
                                                          device_id=(px, py, pc), device_id_type=MESH)
            snd = mk(gath.at[me])
            snd.start()
            copies.append((snd, mk(gath.at[peer])))
        for snd, rcv in copies:
            rcv.wait_recv()
            snd.wait_send()
        acc = gath[0]
        for s in range(1, N_DEV):
            acc = acc + gath[s]
        o_ref[...] = acc

    vm = pl.BlockSpec(memory_space=pltpu.VMEM)
    return pl.pallas_call(
        body, name=name, in_specs=[vm], out_specs=vm, out_shape=jax.ShapeDtypeStruct(pack.shape, F32),
        scratch_shapes=[pltpu.VMEM((N_DEV, R, 128), F32), pltpu.SemaphoreType.DMA((7,)), pltpu.SemaphoreType.DMA((7,))],
        compiler_params=pltpu.CompilerParams(vmem_limit_bytes=VMEM_LIMIT),
    )(pack)


def _row_tile(R, C, elems=262144):
    fits = [t for t in range(16, R + 1, 16) if R % t == 0 and t * C <= elems]
    return max(fits) if fits else R


def _pair_add(parts, other, name):
    _, R, C = parts.shape
    tr = _row_tile(R, C, 1048576)

    def body(c_ref, p_ref, o_ref, s_ref):
        s_ref[0] = (p_ref[0].astype(F32) + o_ref[0].astype(F32)).astype(BF)

    core = lax.axis_index("c").astype(jnp.int32).reshape(1)
    return pl.pallas_call(
        body, name=name,
        grid_spec=pltpu.PrefetchScalarGridSpec(
            num_scalar_prefetch=1, grid=(N_CHIP, R // tr),
            in_specs=[pl.BlockSpec((1, tr, C), lambda q, i, c: (2 * q + c[0], i, 0)), pl.BlockSpec((1, tr, C), lambda q, i, c: (q, i, 0))],
            out_specs=pl.BlockSpec((1, tr, C), lambda q, i, c: (q, i, 0))),
        out_shape=jax.ShapeDtypeStruct((N_CHIP, R, C), BF),
        compiler_params=pltpu.CompilerParams(dimension_semantics=("arbitrary", "arbitrary"), vmem_limit_bytes=VMEM_LIMIT),
    )(core, parts, other)


def _adamw(parts, w, m, v, name):
    ns, R, C = parts.shape
    tr = _row_tile(R, C, 524288)
    c1 = 1.0 - ADAM_B1 ** ADAM_STEP
    c2 = 1.0 - ADAM_B2 ** ADAM_STEP

    def body(p_ref, w_ref, m_ref, v_ref, g_ref, d_ref, nm_ref, nv_ref):
        g = p_ref[0].astype(F32)
        for s in range(1, ns):
            g = g + p_ref[s].astype(F32)
        mn = ADAM_B1 * m_ref[...] + (1.0 - ADAM_B1) * g
        vn = ADAM_B2 * v_ref[...] + (1.0 - ADAM_B2) * (g * g)
        g_ref[...] = g
        nm_ref[...] = mn
        nv_ref[...] = vn
        d_ref[...] = -ADAM_LR * ((mn / c1) / (jnp.sqrt(vn / c2) + ADAM_EPS) + ADAM_WD * w_ref[...])

    row = pl.BlockSpec((tr, C), lambda i: (i, 0))
    shp = jax.ShapeDtypeStruct((R, C), F32)
    return _call(body, grid=(R // tr,), name=name, args=[parts, w, m, v],
                 in_specs=[pl.BlockSpec((ns, tr, C), lambda i: (0, i, 0)), row, row, row], out_specs=[row] * 4, out_shape=[shp] * 4)[0]


def _pad_rows(a, rows):
    return jnp.pad(a, ((0, rows - a.shape[0]), (0, 0)))


def kernel(x, ffn1_norm, ffn1_w_gate, ffn1_w_up, ffn1_w_down, mix_norm, w_in, sg_ln_g, sg_ln_b, sg_w, sg_b, w_att_out, w_sg_out, w_out, ffn2_norm, ffn2_w_gate, ffn2_w_up, ffn2_w_down, final_norm, loss_target, m_ffn1_norm, m_ffn1_w_gate, m_ffn1_w_up, m_ffn1_w_down, m_mix_norm, m_w_in, m_sg_ln_g, m_sg_ln_b, m_sg_w, m_sg_b, m_w_att_out, m_w_sg_out, m_w_out, m_ffn2_norm, m_ffn2_w_gate, m_ffn2_w_up, m_ffn2_w_down, m_final_norm, v_ffn1_norm, v_ffn1_w_gate, v_ffn1_w_up, v_ffn1_w_down, v_mix_norm, v_w_in, v_sg_ln_g, v_sg_ln_b, v_sg_w, v_sg_b, v_w_att_out, v_w_sg_out, v_w_out, v_ffn2_norm, v_ffn2_w_gate, v_ffn2_w_up, v_ffn2_w_down, v_final_norm):
    S, D = x.shape[1], x.shape[2]
    Pb = w_in.shape[2]
    P = N_DEV * Pb
    assert P == GA_OFF + 2 * D and D % (N_DEV * 128) == 0 and S % (BLK * DILATIONS[-1]) == 0
    xs, tgt = x[0], loss_target[0]

    sharded = dict(ffn1_w_gate=ffn1_w_gate, ffn1_w_up=ffn1_w_up, ffn1_w_down=ffn1_w_down, w_in=w_in, w_att_out=w_att_out,
                   w_sg_out=w_sg_out, w_out=w_out, ffn2_w_gate=ffn2_w_gate, ffn2_w_up=ffn2_w_up, ffn2_w_down=ffn2_w_down)
    cols = ("ffn1_w_gate", "ffn1_w_up", "w_in", "ffn2_w_gate", "ffn2_w_up")
    local = lambda n, a: a[0].T if n in cols else a[0]
    back = lambda n, a: a.T[None] if n in cols else a[None]
    wloc = {n: local(n, w) for n, w in sharded.items()}
    sb = {n: w.astype(BF) for n, w in wloc.items()}

    (h1,), ((wg1,),) = _rms_fwd(xs, ffn1_norm, "rms1", comm=[_Gather([sb["ffn1_w_gate"]], 1.0, 1.0)])
    (g1,), ((wu1,),) = _ffn_gate(h1, wg1, "ffn1_gate", comm=[_Gather([sb["ffn1_w_up"]], 0.9, 0.55)])
    (u1, a1), ((wd1,),) = _ffn_up_act(h1, wu1, g1, "ffn1_up_act", comm=[_Gather([sb["ffn1_w_down"]], 0.9, 0.55)])
    (x1, h2), ((winT8,),) = _ffn_down_norm(a1, wd1, xs, mix_norm, "ffn1_down", comm=[_Gather([sb["w_in"]], 1.0, 0.7)])
    winT = winT8.reshape(P, D)
    tm_proj = _tile(S, 1024)
    (qkv,), ((wg2,),) = _mm_nt(h2, winT, tm_proj, 512, 0, U_OFF, F32, "proj_qkv", comm=[_Gather([sb["ffn2_w_gate"]], 1.0, 0.7)])
    (rest,), ((wu2,),) = _mm_nt(h2, winT, tm_proj, 512, U_OFF, P, BF, "proj_rest", comm=[_Gather([sb["ffn2_w_up"]], 0.75, 0.45)])
    tabs = _rope_tables(S)
    rides = [[_Gather([sb["w_att_out"], sb["w_sg_out"]], 0.9, 0.5)], [_Gather([sb["w_out"]], 0.85, 0.45)], []]
    os, lses, late = [], [], []
    for gi, d in enumerate(DILATIONS):
        (o, l), got_here = _att_fwd(qkv, tabs, gi, d, f"att_fwd{gi}", comm=rides[gi])
        late += [w for g in got_here for w in g]
        os.append(o)
        lses.append(l)
    watt, wsg, wout8 = late
    wout = wout8.reshape(D, D)
    oatt, lse = _att_combine(os, lses, "att_combine")
    sgw = sg_w[0]
    sgbT = jnp.pad(sg_b[0].T, ((0, 0), (0, BLK - SG_GROUPS)))
    z = _sg_fwd(rest, sgw, sgbT, sg_ln_g, sg_ln_b, "sg_fwd")
    (ya, ys, merged), _ = _gate_merge(oatt, z, watt, wsg, rest, "gate_merge")
    x2, h3 = _mix_out(merged, wout, x1, ffn2_norm, "mix_out")
    (g3, u3, a3), ((wd2,),) = _ffn_up(h3, wg2, wu2, "ffn2_up", comm=[_Gather([sb["ffn2_w_down"]], 0.6, 0.35)])
    dx3, dyb3, d_final, loss_part = _ffn_down_loss(a3, wd2, x2, final_norm.reshape(1, D), tgt, "ffn2_down_loss")

    Fb = wg2.shape[1]
    Db = watt.shape[2]
    p_pad = -(-P // PROJ_TK) * PROJ_TK
    win_tail = _pad_rows(winT[p_pad - PROJ_TK:], PROJ_TK)
    (dg3, du3), _ = _ffn_bwd_act(dyb3, wd2, g3, u3, "ffn2_bwd_act")
    (dwd2,), _ = _ffn_dwd(a3, dyb3, "ffn2_dwd")
    (dwg2, dwu2), _ = _ffn_dwgu(h3, dg3, du3, "ffn2_dwgu")
    ffn2_parts = [dwd2, dwg2, dwu2]
    (dx2, dmixb, d_ffn2n), (ffn2_other,) = _dh_rms_bwd([(dg3, wg2), (du3, wu2)], True, Fb, x2, ffn2_norm, dx3, 1.0, "ffn2_dh",
                                                     comm=[_Swap(ffn2_parts)])
    ffn2_sums = [_pair_add(p, o, f"pair_ffn2_{i}") for i, (p, o) in enumerate(zip(ffn2_parts, ffn2_other))]

    dya, dys, dga, dgs = _mix_bwd_gate(dmixb, wout, ya, ys, rest, "mix_bwd_gate")
    (dwout,), _ = _mm_tn(merged, dmixb, _tile(D, 1024), _tile(D, 1024), _tile(S, 1024), False, "dw_out")
    do, dz, dvec = _att_sg_dout(dya, dys, watt, wsg, oatt, "att_sg_dout")
    (dwatt,), _ = _mm_tn(oatt, dya, GROUP_W, 2 * Db, _tile(S, 1024), Db, "dw_att")
    (dwsg,), _ = _mm_tn(z, dys, SG_W, 2 * Db, _tile(S, 1024), Db, "dw_sg")
    mix_parts = [dwout.reshape(N_DEV, D // N_DEV, D), dwatt, dwsg]
    du, dvs, d_sgw, d_sgbT, d_lng, d_lnb = _sg_bwd(rest, dz, sgw, sgbT, sg_ln_g, sg_ln_b, "sg_bwd")
    dqs, dks, dvs_att, ffn2_got = [], [], [], []
    for gi, d in enumerate(DILATIONS):
        ride = [_Ici([ffn2_sums[0]])] if gi == 2 else []
        (dq, dk, dv), got_here = _att_bwd(qkv, tabs, do, lse, dvec, gi, d, f"att_bwd{gi}", comm=ride)
        ffn2_got += [g[0] for g in got_here]
        dqs.append(dq)
        dks.append(dk)
        dvs_att.append(dv)
    dproj = jnp.concatenate([t.astype(BF) for t in dqs + dks + dvs_att] + [du, dvs, dga, dgs, jnp.zeros((S, p_pad - P), BF)], axis=1)
    (dx1, dyb1, d_mixn), (ffn2_rest, mix_other) = _dh_rms_bwd([(dproj, winT)], False, PROJ_TK, x1, mix_norm, dx2, 0.5, "proj_dh",
                                                            comm=[_Ici(ffn2_sums[1:]), _Swap(mix_parts)], tail=win_tail)
    ffn2_got += ffn2_rest
    mix_sums = [_pair_add(p, o, f"pair_mix_{i}") for i, (p, o) in enumerate(zip(mix_parts, mix_other))]
    (dwd1,), (mix_got,) = _ffn_dwd(a1, dyb1, "ffn1_dwd", comm=[_Ici(mix_sums)])
    rows = lambda a: a.reshape(-1, 128)
    pad8 = lambda a: _pad_rows(a, -(-a.shape[0] // 8) * 8)
    small = [("sg_w", rows(d_sgw), sg_w, m_sg_w, v_sg_w), ("mix_norm", rows(d_mixn), mix_norm, m_mix_norm, v_mix_norm),
             ("ffn2_norm", rows(d_ffn2n), ffn2_norm, m_ffn2_norm, v_ffn2_norm), ("final_norm", rows(d_final), final_norm, m_final_norm, v_final_norm),
             ("sg_ln_g", rows(d_lng), sg_ln_g, m_sg_ln_g, v_sg_ln_g), ("sg_ln_b", rows(d_lnb), sg_ln_b, m_sg_ln_b, v_sg_ln_b),
             ("sg_b", d_sgbT[:, :SG_GROUPS].T, sg_b, m_sg_b, v_sg_b)]
    gpack = jnp.concatenate([pad8(g) for _, g, _, _, _ in small] + [pad8(loss_part)], axis=0)
    (dwin,), ((wd1_other,), (gpacks,)) = _mm_tn(dproj, h2, 512, D, _tile(S, 2048), False, "dw_in", mrows=P,
                                              comm=[_Swap([dwd1]), _Spread([gpack])])
    dwin = dwin.reshape(N_DEV, Pb, D)
    wd1_sum = _pair_add(dwd1, wd1_other, "pair_wd1")
    (dg1, du1), ((wd1_got,), (win_other,)) = _ffn_bwd_act(dyb1, wd1, g1, u1, "ffn1_bwd_act", comm=[_Ici([wd1_sum]), _Swap([dwin])])
    win_sum = _pair_add(dwin, win_other, "pair_win")
    (dwg1, dwu1), ((win_got,),) = _ffn_dwgu(h1, dg1, du1, "ffn1_dwgu", comm=[_Ici([win_sum])])
    gu_parts = [dwg1, dwu1]
    gu_other = _comm_only(_Swap(gu_parts), "swap_gu1")
    gu_sums = [_pair_add(p, o, f"pair_gu1_{i}") for i, (p, o) in enumerate(zip(gu_parts, gu_other))]
    (dx0, _, d_ffn1n), (gu_got,) = _dh_rms_bwd([(dg1, wg1), (du1, wu1)], True, Fb, xs, ffn1_norm, dx1, 1.0, "ffn1_dh",
                                               comm=[_Ici(gu_sums)])

    got = dict(ffn2_w_down=ffn2_got[0], ffn2_w_gate=ffn2_got[1], ffn2_w_up=ffn2_got[2], w_out=mix_got[0], w_att_out=mix_got[1],
               w_sg_out=mix_got[2], w_in=win_got, ffn1_w_gate=gu_got[0], ffn1_w_up=gu_got[1], ffn1_w_down=wd1_got)
    moments = dict(ffn1_w_gate=(m_ffn1_w_gate, v_ffn1_w_gate), ffn1_w_up=(m_ffn1_w_up, v_ffn1_w_up),
                   ffn1_w_down=(m_ffn1_w_down, v_ffn1_w_down), w_in=(m_w_in, v_w_in), w_att_out=(m_w_att_out, v_w_att_out),
                   w_sg_out=(m_w_sg_out, v_w_sg_out), w_out=(m_w_out, v_w_out), ffn2_w_gate=(m_ffn2_w_gate, v_ffn2_w_gate),
                   ffn2_w_up=(m_ffn2_w_up, v_ffn2_w_up), ffn2_w_down=(m_ffn2_w_down, v_ffn2_w_down))
    res = {}
    for n in sharded:
        mm, vv = moments[n]
        outs = _adamw(got[n], wloc[n], local(n, mm), local(n, vv), "adamw_" + n)
        res[n] = [back(n, o) for o in outs]

    zero8 = jnp.zeros((8, 128), F32)
    wpack = jnp.concatenate([pad8(rows(w)) for _, _, w, _, _ in small] + [zero8], axis=0)
    mpack = jnp.concatenate([pad8(rows(m)) for _, _, _, m, _ in small] + [zero8], axis=0)
    vpack = jnp.concatenate([pad8(rows(v)) for _, _, _, _, v in small] + [zero8], axis=0)
    packs = _adamw(gpacks, wpack, mpack, vpack, "adamw_small")
    off = 0
    for n, g, w, _, _ in small:
        r = g.shape[0]
        res[n] = [p[off:off + r].reshape(w.shape) for p in packs]
        off += -(-r // 8) * 8
    loss = packs[0][off, 0]
    g_first = _small_allreduce(rows(d_ffn1n), "allreduce_ffn1_norm")
    res["ffn1_norm"] = [p.reshape(ffn1_norm.shape) for p in
                        _adamw(g_first[None], rows(ffn1_norm), rows(m_ffn1_norm), rows(v_ffn1_norm), "adamw_ffn1_norm")]

    order = ["ffn1_norm", "ffn1_w_gate", "ffn1_w_up", "ffn1_w_down", "mix_norm", "w_in", "sg_ln_g", "sg_ln_b", "sg_w", "sg_b",
             "w_att_out", "w_sg_out", "w_out", "ffn2_norm", "ffn2_w_gate", "ffn2_w_up", "ffn2_w_down", "final_norm"]
    return (loss, dx0[None], *[res[n][0] for n in order], *[res[n][1] for n in order], *[res[n][2] for n in order],
            *[res[n][3] for n in order])
```

```python
import math

import jax
import jax.numpy as jnp
from jax import lax
from jax.experimental import pallas as pl
from jax.experimental.pallas import tpu as pltpu

BF = jnp.bfloat16
F32 = jnp.float32
MESH = pl.DeviceIdType.MESH
N_DEV = 8
N_CHIP = 4

HEAD_DIM = 128
HEADS_PER_GROUP = 4
GROUP_W = HEADS_PER_GROUP * HEAD_DIM
DILATIONS = (1, 4, 16)
ATT_W = len(DILATIONS) * GROUP_W
SG_W = 1536
SG_GROUPS = 12
BLK = 128
ROPE_DIM = 32
ROPE_THETA = 500000.0
NORM_EPS = 1e-6
LN_EPS = 1e-5
Q_OFF, K_OFF, V_OFF, U_OFF, VS_OFF, GA_OFF = 0, ATT_W, 2 * ATT_W, 3 * ATT_W, 3 * ATT_W + SG_W, 3 * ATT_W + 2 * SG_W

ADAM_LR, ADAM_B1, ADAM_B2, ADAM_EPS, ADAM_WD, ADAM_STEP = 0.001, 0.9, 0.999, 1e-08, 0.01, 10

VMEM_LIMIT = 56 * 1024 * 1024
NEG = -1e30
ANY = pl.BlockSpec(memory_space=pl.ANY)
EPI_ROWS = 128
ACC_COLS = 512
FFN_PAIR = 2
FFN_ROWS = 1024
PROJ_TK = 1536
R_U, R_VS, R_GA = 0, SG_W, 2 * SG_W


def _once(shape, index_map):
    return pl.BlockSpec(shape, index_map, pipeline_mode=pl.Buffered(1))


def _tile(n, pref):
    t = min(n, pref)
    while n % t:
        t //= 2
    return t


def _nt(a, b):
    return lax.dot_general(a, b, (((1,), (1,)), ((), ())), preferred_element_type=F32)


def _tn(a, b):
    return lax.dot_general(a, b, (((0,), (0,)), ((), ())), preferred_element_type=F32)


def _nn(a, b):
    return jnp.dot(a, b, preferred_element_type=F32)


def _acc_dots(acc_ref, terms, transposed_rhs=False):
    n = acc_ref.shape[1]
    width = min(n, ACC_COLS)
    for c0 in range(0, n, width):
        cols = slice(c0, c0 + width)
        tot = None
        for lhs, rhs in terms:
            part = _nt(lhs, rhs(cols)) if transposed_rhs else _nn(lhs, rhs(cols))
            tot = part if tot is None else tot + part
        acc_ref[:, cols] += tot


def _gelu(x):
    return 0.5 * x * (1.0 + lax.erf(x * (2.0 ** -0.5)))


def _gelu_grad(x):
    return 0.5 * (1.0 + lax.erf(x * (2.0 ** -0.5))) + x * jnp.exp(-0.5 * x * x) * (1.0 / math.sqrt(2.0 * math.pi))


def _place():
    x, y, c = lax.axis_index("x"), lax.axis_index("y"), lax.axis_index("c")
    return x, y, c


def _flip(v, bit):
    return 1 - v if bit else v


class _Gather:
    def __init__(self, shards, mid_frac=1.0, relay_frac=0.5):
        self.arrays = list(shards)
        self.relay_frac = relay_frac
        self.mid_frac = mid_frac
        nw = len(shards)
        self.out_shape = [jax.ShapeDtypeStruct((N_DEV,) + s.shape, s.dtype) for s in shards]
        self.scratch = [pltpu.SemaphoreType.DMA((nw, 7)), pltpu.SemaphoreType.DMA((nw, 7)), pltpu.SemaphoreType.DMA((nw,))]

    def _parts(self, ins, outs, sems):
        x, y, c = _place()
        send, recv, loc = sems
        south = c == 0
        near = (jnp.where(south, x, 1 - x), jnp.where(south, 1 - y, y), c)
        far = (jnp.where(south, 1 - x, x), jnp.where(south, y, 1 - y), c)
        diag = (1 - x, 1 - y, c)

        def copy(k, s, block, to, src=None):
            dst = outs[k].at[4 * block[0] + 2 * block[1] + block[2]]
            return pltpu.make_async_remote_copy(src_ref=dst if src is None else src, dst_ref=dst, send_sem=send.at[k, s],
                                                recv_sem=recv.at[k, s], device_id=to, device_id_type=MESH)

        def first(k):
            me = (x, y, c)
            return [copy(k, 0, me, (x, y, 1 - c), src=ins[k]), copy(k, 1, me, (1 - x, y, c), src=ins[k]),
                    copy(k, 2, me, (x, 1 - y, c), src=ins[k])]

        def local(k):
            return pltpu.make_async_copy(ins[k], outs[k].at[4 * x + 2 * y + c], loc.at[k])

        return x, y, c, near, far, diag, copy, first, local

    def start(self, ins, outs, sems):
        *_, first, local = self._parts(ins, outs, sems)
        for k in range(len(ins)):
            local(k).start()
            for cp in first(k):
                cp.start()

    def relay(self, ins, outs, sems):
        x, y, c, near, far, _, copy, _, _ = self._parts(ins, outs, sems)
        for k in range(len(ins)):
            copy(k, 2 - c, near, (x, y, c)).wait_recv()
            copy(k, 3, near, far).start()
            copy(k, 5 - c, near, (x, y, 1 - c)).start()

    def mid(self, ins, outs, sems):
        x, y, c, _, far, diag, copy, _, _ = self._parts(ins, outs, sems)
        for k in range(len(ins)):
            copy(k, 1 + c, far, (x, y, c)).wait_recv()
            copy(k, 4 + c, far, (x, y, 1 - c)).start()
            copy(k, 3, diag, (x, y, c)).wait_recv()
            copy(k, 6, diag, (x, y, 1 - c)).start()

    def finish(self, ins, outs, sems):
        x, y, c, near, _, _, copy, first, local = self._parts(ins, outs, sems)
        sib = (x, y, 1 - c)
        for k in range(len(ins)):
            copy(k, 0, sib, (x, y, c)).wait_recv()
            copy(k, 4, (1 - x, y, 1 - c), (x, y, c)).wait_recv()
            copy(k, 5, (x, 1 - y, 1 - c), (x, y, c)).wait_recv()
            copy(k, 6, (1 - x, 1 - y, 1 - c), (x, y, c)).wait_recv()
        for k in range(len(ins)):
            for cp in first(k):
                cp.wait_send()
            for s in (3, 4, 5, 6):
                copy(k, s, near, sib).wait_send()
            local(k).wait()


class _Swap:
    def __init__(self, parts):
        self.arrays = list(parts)
        nw = len(parts)
        self.out_shape = [jax.ShapeDtypeStruct((N_CHIP,) + p.shape[1:], p.dtype) for p in parts]
        self.scratch = [pltpu.SemaphoreType.DMA((nw, N_CHIP)), pltpu.SemaphoreType.DMA((nw, N_CHIP))]

    def _copy(self, ins, outs, sems, k, q):
        x, y, c = _place()
        return pltpu.make_async_remote_copy(src_ref=ins[k].at[2 * q + 1 - c], dst_ref=outs[k].at[q], send_sem=sems[0].at[k, q],
                                            recv_sem=sems[1].at[k, q], device_id=(x, y, 1 - c), device_id_type=MESH)

    mid_frac = None

    def start(self, ins, outs, sems):
        for k in range(len(ins)):
            for q in range(N_CHIP):
                self._copy(ins, outs, sems, k, q).start()

    def finish(self, ins, outs, sems):
        for k in range(len(ins)):
            for q in range(N_CHIP):
                self._copy(ins, outs, sems, k, q).wait()


class _Ici:
    mid_frac = None

    def __init__(self, sums):
        self.arrays = list(sums)
        nw = len(sums)
        self.out_shape = [jax.ShapeDtypeStruct(s.shape, s.dtype) for s in sums]
        self.scratch = [pltpu.SemaphoreType.DMA((nw, 3)), pltpu.SemaphoreType.DMA((nw, 3)), pltpu.SemaphoreType.DMA((nw,))]

    def _copies(self, ins, outs, sems, k):
        x, y, c = _place()
        myq = 2 * x + y
        out = []
        for r in range(1, N_CHIP):
            px, py = _flip(x, r & 2), _flip(y, r & 1)
            pq = 2 * px + py
            mk = lambda dst: pltpu.make_async_remote_copy(src_ref=ins[k].at[pq], dst_ref=dst, send_sem=sems[0].at[k, r - 1],
                                                          recv_sem=sems[1].at[k, r - 1], device_id=(px, py, c), device_id_type=MESH)
            out.append((mk(outs[k].at[myq]), mk(outs[k].at[pq])))
        return out, pltpu.make_async_copy(ins[k].at[myq], outs[k].at[myq], sems[2].at[k])

    def start(self, ins, outs, sems):
        for k in range(len(ins)):
            remote, local = self._copies(ins, outs, sems, k)
            local.start()
            for snd, _ in remote:
                snd.start()

    def finish(self, ins, outs, sems):
        for k in range(len(ins)):
            remote, local = self._copies(ins, outs, sems, k)
            for snd, rcv in remote:
                rcv.wait_recv()
                snd.wait_send()
            local.wait()


class _Spread:
    mid_frac = None

    def __init__(self, arrays):
        self.arrays = list(arrays)
        nw = len(arrays)
        self.out_shape = [jax.ShapeDtypeStruct((N_DEV,) + a.shape, a.dtype) for a in arrays]
        self.scratch = [pltpu.SemaphoreType.DMA((nw, 7)), pltpu.SemaphoreType.DMA((nw, 7)), pltpu.SemaphoreType.DMA((nw,))]

    def _copies(self, ins, outs, sems, k):
        x, y, c = _place()
        me = 4 * x + 2 * y + c
        out = []
        for r in range(1, N_DEV):
            px, py, pc = _flip(x, r & 4), _flip(y, r & 2), _flip(c, r & 1)
            peer = 4 * px + 2 * py + pc
            mk = lambda dst: pltpu.make_async_remote_copy(src_ref=ins[k], dst_ref=dst, send_sem=sems[0].at[k, r - 1],
                                                          recv_sem=sems[1].at[k, r - 1], device_id=(px, py, pc), device_id_type=MESH)
            out.append((mk(outs[k].at[me]), mk(outs[k].at[peer])))
        return out, pltpu.make_async_copy(ins[k], outs[k].at[me], sems[2].at[k])

    def start(self, ins, outs, sems):
        for k in range(len(ins)):
            remote, local = self._copies(ins, outs, sems, k)
            local.start()
            for snd, _ in remote:
                snd.start()

    def finish(self, ins, outs, sems):
        for k in range(len(ins)):
            remote, local = self._copies(ins, outs, sems, k)
            for snd, rcv in remote:
                rcv.wait_recv()
                snd.wait_send()
            local.wait()


def _call(body, *, grid, in_specs, out_specs, out_shape, name, args, scratch=(), comm=()):
    comm = list(comm)
    n_in, n_out, n_scr = len(in_specs), len(out_specs), len(scratch)
    total = math.prod(grid) if grid else 1

    def wrapped(*refs):
        p = n_in
        cin = []
        for cm in comm:
            cin.append(refs[p:p + len(cm.arrays)])
            p += len(cm.arrays)
        own_out = refs[p:p + n_out]
        p += n_out
        cout = []
        for cm in comm:
            cout.append(refs[p:p + len(cm.arrays)])
            p += len(cm.arrays)
        own_scr = refs[p:p + n_scr]
        p += n_scr
        csem = []
        for cm in comm:
            csem.append(refs[p:p + len(cm.scratch)])
            p += len(cm.scratch)
        step = 0
        for axis, g in enumerate(grid):
            step = step * g + pl.program_id(axis)

        def at(when, what):
            if total == 1:
                what()
            else:
                pl.when(step == when)(what)

        def starts():
            for cm, i, o, s in zip(comm, cin, cout, csem):
                cm.start(i, o, s)

        def finishes():
            for cm, i, o, s in zip(comm, cin, cout, csem):
                cm.finish(i, o, s)

        if comm:
            at(0, starts)
        if body is not None:
            body(*refs[:n_in], *own_out, *own_scr)
        for cm, i, o, s in zip(comm, cin, cout, csem):
            if cm.mid_frac is not None:
                at(min(total - 1, int(total * cm.relay_frac)), lambda cm=cm, i=i, o=o, s=s: cm.relay(i, o, s))
                at(min(total - 1, int(total * cm.mid_frac)), lambda cm=cm, i=i, o=o, s=s: cm.mid(i, o, s))
        if comm:
            at(total - 1, finishes)

    kw = dict(grid=tuple(grid)) if grid else {}
    outs = pl.pallas_call(
        wrapped, name=name, **kw,
        in_specs=list(in_specs) + [ANY for cm in comm for _ in cm.arrays],
        out_specs=list(out_specs) + [ANY for cm in comm for _ in cm.arrays],
        out_shape=list(out_shape) + [s for cm in comm for s in cm.out_shape],
        scratch_shapes=list(scratch) + [s for cm in comm for s in cm.scratch],
        compiler_params=pltpu.CompilerParams(dimension_semantics=("arbitrary",) * len(grid), vmem_limit_bytes=VMEM_LIMIT),
    )(*args, *[a for cm in comm for a in cm.arrays])
    own, p, per = list(outs[:n_out]), n_out, []
    for cm in comm:
        per.append(list(outs[p:p + len(cm.arrays)]))
        p += len(cm.arrays)
    return own, per


def _comm_only(cm, name):
    return _call(None, grid=(), in_specs=[], out_specs=[], out_shape=[], name=name, args=[], comm=[cm])[1][0]


def _rms_fwd(x, g, name, comm=()):
    S, D = x.shape
    tm = _tile(S, 512)

    def body(x_ref, g_ref, o_ref):
        xv = x_ref[...]
        r = lax.rsqrt(jnp.mean(xv * xv, axis=-1, keepdims=True) + NORM_EPS)
        o_ref[...] = (xv * r * g_ref[...]).astype(BF)

    return _call(body, grid=(S // tm,), name=name, args=[x, g], comm=comm,
                 in_specs=[pl.BlockSpec((tm, D), lambda i: (i, 0)), pl.BlockSpec((1, D), lambda i: (0, 0))],
                 out_specs=[pl.BlockSpec((tm, D), lambda i: (i, 0))], out_shape=[jax.ShapeDtypeStruct((S, D), BF)])


def _ffn_up(h, wg, wu, name, comm=()):
    S, D = h.shape
    nb, Fb, _ = wg.shape
    tm = _tile(S, FFN_ROWS)

    def body(h_ref, wg_ref, wu_ref, g_ref, u_ref, a_ref):
        hv = h_ref[...]
        g = _nt(hv, wg_ref[0])
        u = _nt(hv, wu_ref[0])
        g_ref[0] = g.astype(BF)
        u_ref[0] = u.astype(BF)
        a_ref[0] = (g * jax.nn.sigmoid(g) * u).astype(BF)

    act = pl.BlockSpec((1, tm, Fb), lambda j, i: (j, i, 0))
    w = pl.BlockSpec((1, Fb, D), lambda j, i: (j, 0, 0))
    shp = jax.ShapeDtypeStruct((nb, S, Fb), BF)
    return _call(body, grid=(nb, S // tm), name=name, args=[h, wg, wu], comm=comm,
                 in_specs=[pl.BlockSpec((tm, D), lambda j, i: (i, 0)), w, w], out_specs=[act, act, act], out_shape=[shp, shp, shp])


def _ffn_gate(h, wg, name, comm=()):
    S, D = h.shape
    nb, Fb, _ = wg.shape
    tm = _tile(S, FFN_ROWS)

    def body(h_ref, wg_ref, g_ref):
        g_ref[0] = _nt(h_ref[...], wg_ref[0]).astype(BF)

    act = pl.BlockSpec((1, tm, Fb), lambda j, i: (j, i, 0))
    return _call(body, grid=(nb, S // tm), name=name, args=[h, wg], comm=comm,
                 in_specs=[pl.BlockSpec((tm, D), lambda j, i: (i, 0)), pl.BlockSpec((1, Fb, D), lambda j, i: (j, 0, 0))],
                 out_specs=[act], out_shape=[jax.ShapeDtypeStruct((nb, S, Fb), BF)])


def _ffn_up_act(h, wu, g, name, comm=()):
    S, D = h.shape
    nb, Fb, _ = wu.shape
    tm = _tile(S, FFN_ROWS)

    def body(h_ref, wu_ref, g_ref, u_ref, a_ref):
        u = _nt(h_ref[...], wu_ref[0])
        gv = g_ref[0].astype(F32)
        u_ref[0] = u.astype(BF)
        a_ref[0] = (gv * jax.nn.sigmoid(gv) * u).astype(BF)

    act = pl.BlockSpec((1, tm, Fb), lambda j, i: (j, i, 0))
    shp = jax.ShapeDtypeStruct((nb, S, Fb), BF)
    return _call(body, grid=(nb, S // tm), name=name, args=[h, wu, g], comm=comm,
                 in_specs=[pl.BlockSpec((tm, D), lambda j, i: (i, 0)), pl.BlockSpec((1, Fb, D), lambda j, i: (j, 0, 0)), act],
                 out_specs=[act, act], out_shape=[shp, shp])


def _ffn_down_norm(a, wd, x, gn, name, comm=()):
    nb, S, Fb = a.shape
    D = wd.shape[2]
    tm = _tile(S, 512)

    nj = nb // FFN_PAIR

    def body(a_ref, wd_ref, x_ref, gn_ref, xo_ref, hn_ref, acc_ref):
        j = pl.program_id(1)

        @pl.when(j == 0)
        def _():
            acc_ref[...] = jnp.zeros_like(acc_ref)

        _acc_dots(acc_ref, [(a_ref[b], lambda cols, b=b: wd_ref[b, :, cols]) for b in range(FFN_PAIR)])

        @pl.when(j == nj - 1)
        def _():
            def chunk(t, carry):
                rows = pl.ds(pl.multiple_of(t * EPI_ROWS, EPI_ROWS), EPI_ROWS)
                xo = x_ref[rows, :] + 0.5 * acc_ref[rows, :]
                r = lax.rsqrt(jnp.mean(xo * xo, axis=-1, keepdims=True) + NORM_EPS)
                xo_ref[rows, :] = xo
                hn_ref[rows, :] = (xo * r * gn_ref[...]).astype(BF)
                return carry

            lax.fori_loop(0, tm // EPI_ROWS, chunk, 0)

    row = pl.BlockSpec((tm, D), lambda i, j: (i, 0))
    return _call(body, grid=(S // tm, nj), name=name, args=[a, wd, x, gn], comm=comm,
                 in_specs=[pl.BlockSpec((FFN_PAIR, tm, Fb), lambda i, j: (j, i, 0)), pl.BlockSpec((FFN_PAIR, Fb, D), lambda i, j: (j, 0, 0)),
                           row, pl.BlockSpec((1, D), lambda i, j: (0, 0))],
                 out_specs=[row, row], out_shape=[jax.ShapeDtypeStruct((S, D), F32), jax.ShapeDtypeStruct((S, D), BF)],
                 scratch=[pltpu.VMEM((tm, D), F32)])


def _ffn_down_loss(a, wd, x, gf, tgt, name):
    nb, S, Fb = a.shape
    D = wd.shape[2]
    tm = _tile(S, 512)

    nj = nb // FFN_PAIR

    def body(a_ref, wd_ref, x_ref, gf_ref, t_ref, dx_ref, dxb_ref, dgf_ref, loss_ref, acc_ref):
        i, j = pl.program_id(0), pl.program_id(1)

        @pl.when(j == 0)
        def _():
            acc_ref[...] = jnp.zeros_like(acc_ref)

        _acc_dots(acc_ref, [(a_ref[b], lambda cols, b=b: wd_ref[b, :, cols]) for b in range(FFN_PAIR)])

        @pl.when((j == nj - 1) & (i == 0))
        def _():
            dgf_ref[...] = jnp.zeros_like(dgf_ref)
            loss_ref[...] = jnp.zeros_like(loss_ref)

        @pl.when(j == nj - 1)
        def _():
            def chunk(t, carry):
                rows = pl.ds(pl.multiple_of(t * EPI_ROWS, EPI_ROWS), EPI_ROWS)
                xo = x_ref[rows, :] + 0.5 * acc_ref[rows, :]
                r = lax.rsqrt(jnp.mean(xo * xo, axis=-1, keepdims=True) + NORM_EPS)
                xh = xo * r
                gf = gf_ref[...]
                e = xh * gf - t_ref[rows, :]
                loss_ref[...] += jnp.sum(jnp.mean(e * e, axis=-1, keepdims=True), axis=0, keepdims=True) * 0.5
                dy = e * (1.0 / D)
                dgf_ref[...] += jnp.sum(dy * xh, axis=0, keepdims=True)
                dxh = dy * gf
                dx = r * (dxh - xh * jnp.mean(dxh * xh, axis=-1, keepdims=True))
                dx_ref[rows, :] = dx
                dxb_ref[rows, :] = (0.5 * dx).astype(BF)
                return carry

            lax.fori_loop(0, tm // EPI_ROWS, chunk, 0)

    row = pl.BlockSpec((tm, D), lambda i, j: (i, 0))
    once = row
    vec = pl.BlockSpec((1, D), lambda i, j: (0, 0))
    return _call(body, grid=(S // tm, nj), name=name, args=[a, wd, x, gf, tgt],
                 in_specs=[pl.BlockSpec((FFN_PAIR, tm, Fb), lambda i, j: (j, i, 0)), pl.BlockSpec((FFN_PAIR, Fb, D), lambda i, j: (j, 0, 0)),
                           once, vec, once],
                 out_specs=[row, row, vec, pl.BlockSpec((1, 128), lambda i, j: (0, 0))],
                 out_shape=[jax.ShapeDtypeStruct((S, D), F32), jax.ShapeDtypeStruct((S, D), BF), jax.ShapeDtypeStruct((1, D), F32),
                            jax.ShapeDtypeStruct((1, 128), F32)],
                 scratch=[pltpu.VMEM((tm, D), F32)])[0]


def _ffn_bwd_act(dyb, wd, g, u, name, comm=()):
    S, D = dyb.shape
    nb, Fb, _ = wd.shape
    tm = _tile(S, FFN_ROWS)

    def body(dy_ref, wd_ref, g_ref, u_ref, dg_ref, du_ref):
        da = _nt(dy_ref[...], wd_ref[0])
        gv = g_ref[0].astype(F32)
        uv = u_ref[0].astype(F32)
        sg = jax.nn.sigmoid(gv)
        du_ref[0] = (da * gv * sg).astype(BF)
        dg_ref[0] = (da * uv * sg * (1.0 + gv * (1.0 - sg))).astype(BF)

    act = pl.BlockSpec((1, tm, Fb), lambda j, i: (j, i, 0))
    shp = jax.ShapeDtypeStruct((nb, S, Fb), BF)
    return _call(body, grid=(nb, S // tm), name=name, args=[dyb, wd, g, u], comm=comm,
                 in_specs=[pl.BlockSpec((tm, D), lambda j, i: (i, 0)), pl.BlockSpec((1, Fb, D), lambda j, i: (j, 0, 0)), act, act],
                 out_specs=[act, act], out_shape=[shp, shp])


def _ffn_dwd(a, dyb, name, comm=()):
    nb, S, Fb = a.shape
    D = dyb.shape[1]
    ts = _tile(S, FFN_ROWS)
    ns = S // ts

    def body(a_ref, dy_ref, o_ref, acc_ref):
        s = pl.program_id(1)

        @pl.when(s == 0)
        def _():
            acc_ref[...] = jnp.zeros_like(acc_ref)

        acc_ref[...] += _tn(a_ref[0], dy_ref[...])

        @pl.when(s == ns - 1)
        def _():
            o_ref[0] = acc_ref[...].astype(BF)

    return _call(body, grid=(nb, ns), name=name, args=[a, dyb], comm=comm,
                 in_specs=[pl.BlockSpec((1, ts, Fb), lambda j, s: (j, s, 0)), pl.BlockSpec((ts, D), lambda j, s: (s, 0))],
                 out_specs=[pl.BlockSpec((1, Fb, D), lambda j, s: (j, 0, 0))], out_shape=[jax.ShapeDtypeStruct((nb, Fb, D), BF)],
                 scratch=[pltpu.VMEM((Fb, D), F32)])


def _ffn_dwgu(h, dg, du, name, comm=()):
    S, D = h.shape
    nb, _, Fb = dg.shape
    ts = _tile(S, FFN_ROWS)
    ns = S // ts

    def body(h_ref, dg_ref, du_ref, og_ref, ou_ref, accg_ref, accu_ref):
        s = pl.program_id(1)

        @pl.when(s == 0)
        def _():
            accg_ref[...] = jnp.zeros_like(accg_ref)
            accu_ref[...] = jnp.zeros_like(accu_ref)

        hv = h_ref[...]
        accg_ref[...] += _tn(dg_ref[0], hv)
        accu_ref[...] += _tn(du_ref[0], hv)

        @pl.when(s == ns - 1)
        def _():
            og_ref[0] = accg_ref[...].astype(BF)
            ou_ref[0] = accu_ref[...].astype(BF)

    act = pl.BlockSpec((1, ts, Fb), lambda j, s: (j, s, 0))
    out = pl.BlockSpec((1, Fb, D), lambda j, s: (j, 0, 0))
    shp = jax.ShapeDtypeStruct((nb, Fb, D), BF)
    return _call(body, grid=(nb, ns), name=name, args=[h, dg, du], comm=comm,
                 in_specs=[pl.BlockSpec((ts, D), lambda j, s: (s, 0)), act, act], out_specs=[out, out], out_shape=[shp, shp],
                 scratch=[pltpu.VMEM((Fb, D), F32), pltpu.VMEM((Fb, D), F32)])


def _dh_rms_bwd(pairs, blocked, tk, x, gn, dxo, out_scale, name, comm=(), tail=None):
    S, D = x.shape
    nk = pairs[0][0].shape[0] if blocked else pairs[0][0].shape[1] // tk
    tm = _tile(S, 512)
    npair = len(pairs)
    assert blocked or (npair == 1 and tail is not None and nk >= 2)
    nin = 2 * npair + (0 if blocked else 1)

    def body(*refs):
        ins = refs[:nin]
        x_ref, gn_ref, dxo_ref, dx_ref, dxb_ref, dgn_ref, acc_ref = refs[nin:]
        i, k = pl.program_id(0), pl.program_id(1)

        @pl.when(k == 0)
        def _():
            acc_ref[...] = jnp.zeros_like(acc_ref)

        if blocked:
            _acc_dots(acc_ref, [(ins[2 * p][0], lambda cols, r=ins[2 * p + 1]: r[0, :, cols]) for p in range(npair)])
        else:
            @pl.when(k < nk - 1)
            def _():
                _acc_dots(acc_ref, [(ins[0][...], lambda cols: ins[1][:, cols])])

            @pl.when(k == nk - 1)
            def _():
                _acc_dots(acc_ref, [(ins[0][...], lambda cols: ins[2][:, cols])])

        @pl.when((k == nk - 1) & (i == 0))
        def _():
            dgn_ref[...] = jnp.zeros_like(dgn_ref)

        @pl.when(k == nk - 1)
        def _():
            def chunk(t, carry):
                rows = pl.ds(pl.multiple_of(t * EPI_ROWS, EPI_ROWS), EPI_ROWS)
                xv = x_ref[rows, :]
                r = lax.rsqrt(jnp.mean(xv * xv, axis=-1, keepdims=True) + NORM_EPS)
                xh = xv * r
                dh = acc_ref[rows, :]
                dgn_ref[...] += jnp.sum(dh * xh, axis=0, keepdims=True)
                dxh = dh * gn_ref[...]
                dx = dxo_ref[rows, :] + r * (dxh - xh * jnp.mean(dxh * xh, axis=-1, keepdims=True))
                dx_ref[rows, :] = dx
                dxb_ref[rows, :] = (out_scale * dx).astype(BF)
                return carry

            lax.fori_loop(0, tm // EPI_ROWS, chunk, 0)

    if blocked:
        mats = [pl.BlockSpec((1, tm, tk), lambda i, k: (k, i, 0)), pl.BlockSpec((1, tk, D), lambda i, k: (k, 0, 0))] * npair
        flat = [t for pr in pairs for t in pr]
    else:
        mats = [pl.BlockSpec((tm, tk), lambda i, k: (i, k)), pl.BlockSpec((tk, D), lambda i, k: (jnp.minimum(k, nk - 2), 0)),
                pl.BlockSpec((tk, D), lambda i, k: (0, 0))]
        flat = [*pairs[0], tail]
    row = pl.BlockSpec((tm, D), lambda i, k: (i, 0))
    once = row if blocked else _once((tm, D), lambda i, k: (i, 0))
    vec = pl.BlockSpec((1, D), lambda i, k: (0, 0))
    return _call(body, grid=(S // tm, nk), name=name, args=[*flat, x, gn, dxo], comm=comm,
                 in_specs=mats + [once, vec, once], out_specs=[row, row, vec],
                 out_shape=[jax.ShapeDtypeStruct((S, D), F32), jax.ShapeDtypeStruct((S, D), BF), jax.ShapeDtypeStruct((1, D), F32)],
                 scratch=[pltpu.VMEM((tm, D), F32)])


def _proj_split(a, bT, tm, tn, split, name, comm=()):
    M, K = a.shape
    N = bT.shape[0]
    n_first = split // tn

    def body(a_ref, b_ref, first_ref, rest_ref):
        n = pl.program_id(1)
        y = _nt(a_ref[...], b_ref[...])

        @pl.when(n < n_first)
        def _():
            first_ref[...] = y

        @pl.when(n >= n_first)
        def _():
            rest_ref[...] = y.astype(BF)

    return _call(body, grid=(M // tm, N // tn), name=name, args=[a, bT], comm=comm,
                 in_specs=[pl.BlockSpec((tm, K), lambda i, n: (i, 0)), pl.BlockSpec((tn, K), lambda i, n: (n, 0))],
                 out_specs=[pl.BlockSpec((tm, tn), lambda i, n: (i, jnp.minimum(n, n_first - 1))),
                            pl.BlockSpec((tm, tn), lambda i, n: (i, jnp.maximum(n - n_first, 0)))],
                 out_shape=[jax.ShapeDtypeStruct((M, split), F32), jax.ShapeDtypeStruct((M, N - split), BF)])


def _mm_tn(a, b, tm, tn, ts, blocked, name, comm=(), mrows=None):
    S, M = a.shape[0], (a.shape[1] if mrows is None else mrows)
    N = b.shape[1]
    ns = S // ts
    per_tile = tn // blocked if blocked else 0

    def body(a_ref, b_ref, o_ref, acc_ref):
        s = pl.program_id(2)

        @pl.when(s == 0)
        def _():
            acc_ref[...] = jnp.zeros_like(acc_ref)

        acc_ref[...] += _tn(a_ref[...], b_ref[...])

        @pl.when(s == ns - 1)
        def _():
            if blocked:
                for t in range(per_tile):
                    o_ref[t] = acc_ref[:, t * blocked:(t + 1) * blocked].astype(BF)
            else:
                o_ref[...] = acc_ref[...].astype(BF)

    if blocked:
        ospec = pl.BlockSpec((per_tile, tm, blocked), lambda i, n, s: (n, i, 0))
        oshape = jax.ShapeDtypeStruct((N // blocked, M, blocked), BF)
    else:
        ospec = pl.BlockSpec((tm, tn), lambda i, n, s: (i, n))
        oshape = jax.ShapeDtypeStruct((M, N), BF)
    return _call(body, grid=(M // tm, N // tn, ns), name=name, args=[a, b], comm=comm,
                 in_specs=[pl.BlockSpec((ts, tm), lambda i, n, s: (s, i)), pl.BlockSpec((ts, tn), lambda i, n, s: (s, n))],
                 out_specs=[ospec], out_shape=[oshape], scratch=[pltpu.VMEM((tm, tn), F32)])


def _rope_tables(S):
    half = ROPE_DIM // 2
    inv_freq = ROPE_THETA ** (-jnp.arange(0, ROPE_DIM, 2, dtype=F32) / ROPE_DIM)
    ang = jnp.arange(S, dtype=F32)[:, None] * inv_freq[None, :]
    cos, sin = jnp.cos(ang), jnp.sin(ang)
    zeros = jnp.zeros((S, HEAD_DIM - ROPE_DIM), F32)
    c = jnp.concatenate([cos, cos, jnp.ones((S, HEAD_DIM - ROPE_DIM), F32)], axis=1)
    sm = jnp.concatenate([-sin, jnp.zeros((S, half), F32), zeros], axis=1)
    sp = jnp.concatenate([jnp.zeros((S, half), F32), sin, zeros], axis=1)
    return c, sm, sp


def _rope(t, c, sm, sp):
    return t * c + pltpu.roll(t, HEAD_DIM - ROPE_DIM // 2, 1) * sm + pltpu.roll(t, ROPE_DIM // 2, 1) * sp


def _rope_t(dy, c, sm, sp):
    return dy * c + pltpu.roll(dy * sm, ROPE_DIM // 2, 1) + pltpu.roll(dy * sp, HEAD_DIM - ROPE_DIM // 2, 1)


def _att_mask(i):
    qi = lax.broadcasted_iota(jnp.int32, (BLK, 2 * BLK), 0)
    kj = lax.broadcasted_iota(jnp.int32, (BLK, 2 * BLK), 1)
    diff = qi + BLK - kj
    first_key = jnp.where(i > 0, 0, BLK)
    return (diff >= 0) & (diff <= BLK) & (kj >= first_key)


def _res_rows(r, i, n, d):
    if d == 1:
        return pl.ds(pl.multiple_of(i * n, n), n)
    return pl.ds(r + i * (n * d), n, stride=d)


def _att_specs(S, gi):
    def sect(off):
        base = (off + gi * GROUP_W) // HEAD_DIM
        return _once((S, HEAD_DIM), lambda hh: (0, base + hh))

    tab = pl.BlockSpec((S, HEAD_DIM), lambda hh: (0, 0))
    head = pl.BlockSpec((S, HEAD_DIM), lambda hh: (0, hh))
    return sect, tab, head


def _each_residue(d, fn):
    if d == 1:
        fn(0)
    else:
        lax.fori_loop(0, d, lambda r, carry: (fn(r), carry)[1], 0)


def _att_fwd(qkv, tabs, gi, d, name, comm=()):
    S = qkv.shape[0]
    L = S // d
    sect, tab, head = _att_specs(S, gi)
    nblk = L // BLK
    scale = HEAD_DIM ** -0.5

    def body(q_ref, k_ref, v_ref, c_ref, sm_ref, sp_ref, o_ref, lse_ref, qr, kp, vp):
        kp[pl.ds(0, BLK), :] = jnp.zeros((BLK, HEAD_DIM), BF)
        vp[pl.ds(0, BLK), :] = jnp.zeros((BLK, HEAD_DIM), BF)

        def residue(r):
            res = _res_rows(r, 0, L, d)
            c, sm, sp = c_ref[res, :], sm_ref[res, :], sp_ref[res, :]
            qr[...] = _rope(q_ref[res, :], c, sm, sp).astype(BF)
            kp[pl.ds(BLK, L), :] = _rope(k_ref[res, :], c, sm, sp).astype(BF)
            vp[pl.ds(BLK, L), :] = v_ref[res, :].astype(BF)

            def blk(i, carry):
                r0 = pl.multiple_of(i * BLK, BLK)
                s = _nt(qr[pl.ds(r0, BLK), :], kp[pl.ds(r0, 2 * BLK), :]) * scale
                s = jnp.where(_att_mask(i), s, NEG)
                m = jnp.max(s, axis=-1, keepdims=True)
                p = jnp.exp(s - m)
                l = jnp.sum(p, axis=-1, keepdims=True)
                out = _res_rows(r, i, BLK, d)
                o_ref[out, :] = _nn(p.astype(BF), vp[pl.ds(r0, 2 * BLK), :]) / l
                lse_ref[out, :] = jnp.broadcast_to(m + jnp.log(l), (BLK, HEAD_DIM))
                return carry

            lax.fori_loop(0, nblk, blk, 0, unroll=min(4, nblk))

        _each_residue(d, residue)

    shp = jax.ShapeDtypeStruct((S, GROUP_W), F32)
    return _call(body, grid=(HEADS_PER_GROUP,), name=name, args=[qkv, qkv, qkv, *tabs], comm=comm,
                 in_specs=[sect(Q_OFF), sect(K_OFF), sect(V_OFF), tab, tab, tab], out_specs=[head, head], out_shape=[shp, shp],
                 scratch=[pltpu.VMEM((L, HEAD_DIM), BF), pltpu.VMEM((L + BLK, HEAD_DIM), BF), pltpu.VMEM((L + BLK, HEAD_DIM), BF)])


def _att_combine(os, lses, name):
    S = os[0].shape[0]
    tm = _tile(S, 512)

    def body(o0, o1, o2, l0, l1, l2, oa_ref, lse_ref):
        a, b, c = l0[...], l1[...], l2[...]
        mx = jnp.maximum(jnp.maximum(a, b), c)
        wa, wb, wc = jnp.exp(a - mx), jnp.exp(b - mx), jnp.exp(c - mx)
        den = wa + wb + wc
        oa_ref[...] = ((wa * o0[...] + wb * o1[...] + wc * o2[...]) / den).astype(BF)
        lse_ref[...] = mx + jnp.log(den)

    row = pl.BlockSpec((tm, GROUP_W), lambda i: (i, 0))
    return _call(body, grid=(S // tm,), name=name, args=[*os, *lses], in_specs=[row] * 6, out_specs=[row, row],
                 out_shape=[jax.ShapeDtypeStruct((S, GROUP_W), BF), jax.ShapeDtypeStruct((S, GROUP_W), F32)])[0]


def _att_bwd(qkv, tabs, do, lse, dvec, gi, d, name, comm=()):
    S = qkv.shape[0]
    L = S // d
    sect, tab, head = _att_specs(S, gi)
    stat = _once((S, HEAD_DIM), lambda hh: (0, hh))
    nblk = L // BLK
    scale = HEAD_DIM ** -0.5

    def body(q_ref, k_ref, v_ref, c_ref, sm_ref, sp_ref, do_ref, lse_ref, dv_ref, dq_out, dk_out, dv_out, qr, kp, vp, dkp, dvp):
        kp[pl.ds(0, BLK), :] = jnp.zeros((BLK, HEAD_DIM), BF)
        vp[pl.ds(0, BLK), :] = jnp.zeros((BLK, HEAD_DIM), BF)

        def residue(r):
            res = _res_rows(r, 0, L, d)
            c, sm, sp = c_ref[res, :], sm_ref[res, :], sp_ref[res, :]
            qr[...] = _rope(q_ref[res, :], c, sm, sp).astype(BF)
            kp[pl.ds(BLK, L), :] = _rope(k_ref[res, :], c, sm, sp).astype(BF)
            vp[pl.ds(BLK, L), :] = v_ref[res, :].astype(BF)
            dkp[...] = jnp.zeros_like(dkp)
            dvp[...] = jnp.zeros_like(dvp)

            def blk(i, carry):
                r0 = pl.multiple_of(i * BLK, BLK)
                rows, win, pos = pl.ds(r0, BLK), pl.ds(r0, 2 * BLK), _res_rows(r, i, BLK, d)
                q, kw, vw, dob = qr[rows, :], kp[win, :], vp[win, :], do_ref[pos, :].astype(BF)
                s = jnp.where(_att_mask(i), _nt(q, kw) * scale, NEG)
                p = jnp.exp(s - lse_ref[pos, :][:, :1])
                ds = p * (_nt(dob, vw) - dv_ref[pos, :][:, :1]) * scale
                dsb = ds.astype(BF)
                dq_out[pos, :] = _rope_t(_nn(dsb, kw), c_ref[pos, :], sm_ref[pos, :], sp_ref[pos, :])
                dkp[win, :] += _tn(dsb, q)
                dvp[win, :] += _tn(p.astype(BF), dob)
                return carry

            lax.fori_loop(0, nblk, blk, 0, unroll=2)
            dk_out[res, :] = _rope_t(dkp[pl.ds(BLK, L), :], c, sm, sp)
            dv_out[res, :] = dvp[pl.ds(BLK, L), :]

        _each_residue(d, residue)

    shp = jax.ShapeDtypeStruct((S, GROUP_W), F32)
    return _call(body, grid=(HEADS_PER_GROUP,), name=name, args=[qkv, qkv, qkv, *tabs, do, lse, dvec], comm=comm,
                 in_specs=[sect(Q_OFF), sect(K_OFF), sect(V_OFF), tab, tab, tab, stat, stat, stat],
                 out_specs=[head, head, head], out_shape=[shp, shp, shp],
                 scratch=[pltpu.VMEM((L, HEAD_DIM), BF), pltpu.VMEM((L + BLK, HEAD_DIM), BF), pltpu.VMEM((L + BLK, HEAD_DIM), BF),
                          pltpu.VMEM((L + BLK, HEAD_DIM), F32), pltpu.VMEM((L + BLK, HEAD_DIM), F32)])


def _sg_parts(u_ref, vs_ref, g_ref, b_ref):
    uv = u_ref[...].astype(F32)
    vv = vs_ref[...].astype(F32)
    vg = _gelu(vv)
    mu = jnp.mean(vg, axis=-1, keepdims=True)
    vc = vg - mu
    rs = lax.rsqrt(jnp.mean(vc * vc, axis=-1, keepdims=True) + LN_EPS)
    y = vc * rs
    return uv, vv, rs, y, y * g_ref[...] + b_ref[...]


def _sg_wmask():
    t = lax.broadcasted_iota(jnp.int32, (BLK, BLK), 0)
    s = lax.broadcasted_iota(jnp.int32, (BLK, BLK), 1)
    return s <= t


def _sg_fwd(proj, sgw, sgbT, lng, lnb, name):
    S, P = proj.shape

    def body(u_ref, vs_ref, w_ref, bt_ref, g_ref, b_ref, z_ref):
        uv, _, _, _, vln = _sg_parts(u_ref, vs_ref, g_ref, b_ref)
        ug = _gelu(uv)
        vb = vln.astype(BF)
        mask = _sg_wmask()
        bt = bt_ref[...]
        for g in range(SG_GROUPS):
            cols = slice(g * BLK, (g + 1) * BLK)
            w = jnp.where(mask, w_ref[g], 0.0).astype(BF)
            sp = _nn(w, vb[:, cols]) + bt[:, g:g + 1]
            z_ref[:, cols] = (ug[:, cols] * sp).astype(BF)

    tile = lambda off: pl.BlockSpec((BLK, SG_W), lambda i: (i, off // SG_W))
    full = lambda shape: pl.BlockSpec(shape, lambda i: (0,) * len(shape))
    return _call(body, grid=(S // BLK,), name=name, args=[proj, proj, sgw, sgbT, lng, lnb],
                 in_specs=[tile(R_U), tile(R_VS), full((SG_GROUPS, BLK, BLK)), full((BLK, BLK)), full((1, SG_W)), full((1, SG_W))],
                 out_specs=[pl.BlockSpec((BLK, SG_W), lambda i: (i, 0))], out_shape=[jax.ShapeDtypeStruct((S, SG_W), BF)])[0][0]


def _sg_bwd(proj, dz, sgw, sgbT, lng, lnb, name):
    S, P = proj.shape

    def body(u_ref, vs_ref, dz_ref, w_ref, bt_ref, g_ref, b_ref, du_ref, dvs_ref, dw_ref, dbt_ref, dg_ref, db_ref, dvln):
        @pl.when(pl.program_id(0) == 0)
        def _():
            dw_ref[...] = jnp.zeros_like(dw_ref)
            dbt_ref[...] = jnp.zeros_like(dbt_ref)
            dg_ref[...] = jnp.zeros_like(dg_ref)
            db_ref[...] = jnp.zeros_like(db_ref)

        uv, vv, rs, y, vln = _sg_parts(u_ref, vs_ref, g_ref, b_ref)
        ug = _gelu(uv)
        vb = vln.astype(BF)
        dzv = dz_ref[...].astype(F32)
        dsp = dzv * ug
        dspb = dsp.astype(BF)
        mask = _sg_wmask()
        bt = bt_ref[...]
        lane = lax.broadcasted_iota(jnp.int32, (BLK, BLK), 1)
        dbt = jnp.zeros((BLK, BLK), F32)
        for g in range(SG_GROUPS):
            cols = slice(g * BLK, (g + 1) * BLK)
            w = jnp.where(mask, w_ref[g], 0.0).astype(BF)
            sp = _nn(w, vb[:, cols]) + bt[:, g:g + 1]
            du_ref[:, cols] = (dzv[:, cols] * sp * _gelu_grad(uv[:, cols])).astype(BF)
            dw_ref[g] += jnp.where(mask, _nt(dspb[:, cols], vb[:, cols]), 0.0)
            dbt = dbt + jnp.where(lane == g, jnp.sum(dsp[:, cols], axis=-1, keepdims=True), 0.0)
            dvln[:, cols] = _tn(w, dspb[:, cols])
        dbt_ref[...] += dbt
        dvl = dvln[...]
        dg_ref[...] += jnp.sum(dvl * y, axis=0, keepdims=True)
        db_ref[...] += jnp.sum(dvl, axis=0, keepdims=True)
        dy = dvl * g_ref[...]
        dvg = rs * (dy - jnp.mean(dy, axis=-1, keepdims=True) - y * jnp.mean(dy * y, axis=-1, keepdims=True))
        dvs_ref[...] = (dvg * _gelu_grad(vv)).astype(BF)

    tile = lambda off: pl.BlockSpec((BLK, SG_W), lambda i: (i, off // SG_W))
    full = lambda shape: pl.BlockSpec(shape, lambda i: (0,) * len(shape))
    row = pl.BlockSpec((BLK, SG_W), lambda i: (i, 0))
    return _call(body, grid=(S // BLK,), name=name, args=[proj, proj, dz, sgw, sgbT, lng, lnb],
                 in_specs=[tile(R_U), tile(R_VS), row, full((SG_GROUPS, BLK, BLK)), full((BLK, BLK)), full((1, SG_W)), full((1, SG_W))],
                 out_specs=[row, row, full((SG_GROUPS, BLK, BLK)), full((BLK, BLK)), full((1, SG_W)), full((1, SG_W))],
                 out_shape=[jax.ShapeDtypeStruct((S, SG_W), BF), jax.ShapeDtypeStruct((S, SG_W), BF),
                            jax.ShapeDtypeStruct((SG_GROUPS, BLK, BLK), F32), jax.ShapeDtypeStruct((BLK, BLK), F32),
                            jax.ShapeDtypeStruct((1, SG_W), F32), jax.ShapeDtypeStruct((1, SG_W), F32)],
                 scratch=[pltpu.VMEM((BLK, SG_W), F32)])[0]


def _gate_merge(oatt, z, watt, wsg, proj, name, comm=()):
    S = oatt.shape[0]
    nb, _, Db = watt.shape
    D = nb * Db
    tm = _tile(S, 512)
    half = D // 2
    ga, gs = R_GA // half, (R_GA + D) // half

    def body(oa_ref, z_ref, wa_ref, ws_ref, ga0, ga1, gs0, gs1, ya_ref, ys_ref, mg_ref):
        oa, zv = oa_ref[...], z_ref[...]
        for j in range(nb):
            cols = slice(j * Db, (j + 1) * Db)
            g_a, g_s = (ga0, gs0) if j < nb // 2 else (ga1, gs1)
            gcols = slice((j % (nb // 2)) * Db, (j % (nb // 2) + 1) * Db)
            ya = _nn(oa, wa_ref[j])
            ys = _nn(zv, ws_ref[j])
            ya_ref[:, cols] = ya.astype(BF)
            ys_ref[:, cols] = ys.astype(BF)
            mg_ref[:, cols] = (jax.nn.sigmoid(g_a[:, gcols].astype(F32)) * ya + jax.nn.sigmoid(g_s[:, gcols].astype(F32)) * ys).astype(BF)

    out = pl.BlockSpec((tm, D), lambda i: (i, 0))
    gate = lambda b: pl.BlockSpec((tm, half), lambda i: (i, b))
    shp = jax.ShapeDtypeStruct((S, D), BF)
    return _call(body, grid=(S // tm,), name=name, args=[oatt, z, watt, wsg, proj, proj, proj, proj], comm=comm,
                 in_specs=[pl.BlockSpec((tm, GROUP_W), lambda i: (i, 0)), pl.BlockSpec((tm, SG_W), lambda i: (i, 0)),
                           pl.BlockSpec((nb, GROUP_W, Db), lambda i: (0, 0, 0)), pl.BlockSpec((nb, SG_W, Db), lambda i: (0, 0, 0)),
                           gate(ga), gate(ga + 1), gate(gs), gate(gs + 1)],
                 out_specs=[out, out, out], out_shape=[shp, shp, shp])


def _mix_out(merged, wout, x, gn, name):
    S, D = x.shape
    tm = _tile(S, 256)

    def body(m_ref, w_ref, x_ref, gn_ref, xo_ref, hn_ref):
        xo = x_ref[...] + _nn(m_ref[...], w_ref[...])
        r = lax.rsqrt(jnp.mean(xo * xo, axis=-1, keepdims=True) + NORM_EPS)
        xo_ref[...] = xo
        hn_ref[...] = (xo * r * gn_ref[...]).astype(BF)

    row = pl.BlockSpec((tm, D), lambda i: (i, 0))
    return _call(body, grid=(S // tm,), name=name, args=[merged, wout, x, gn],
                 in_specs=[row, pl.BlockSpec((D, D), lambda i: (0, 0)), row, pl.BlockSpec((1, D), lambda i: (0, 0))],
                 out_specs=[row, row], out_shape=[jax.ShapeDtypeStruct((S, D), F32), jax.ShapeDtypeStruct((S, D), BF)])[0]


def _mix_bwd_gate(dmix, wout, ya, ys, proj, name):
    S, D = dmix.shape
    tm, tn = _tile(S, 256), min(512, D // 2)
    half = D // 2
    ga, gs = R_GA // half, (R_GA + D) // half

    def body(dm_ref, w_ref, ya_ref, ys_ref, ga0, ga1, gs0, gs1, dya_ref, dys_ref, dga_ref, dgs_ref):
        dmv = dm_ref[...]
        for c0 in range(0, D, tn):
            cols = slice(c0, c0 + tn)
            g_a, g_s = (ga0, gs0) if c0 < half else (ga1, gs1)
            gcols = slice(c0 % half, c0 % half + tn)
            dm = _nt(dmv, w_ref[cols, :])
            sa = jax.nn.sigmoid(g_a[:, gcols].astype(F32))
            ss = jax.nn.sigmoid(g_s[:, gcols].astype(F32))
            dya_ref[:, cols] = (dm * sa).astype(BF)
            dys_ref[:, cols] = (dm * ss).astype(BF)
            dga_ref[:, cols] = (dm * ya_ref[:, cols].astype(F32) * sa * (1.0 - sa)).astype(BF)
            dgs_ref[:, cols] = (dm * ys_ref[:, cols].astype(F32) * ss * (1.0 - ss)).astype(BF)

    row = pl.BlockSpec((tm, D), lambda i: (i, 0))
    gate = lambda b: pl.BlockSpec((tm, half), lambda i: (i, b))
    shp = jax.ShapeDtypeStruct((S, D), BF)
    return _call(body, grid=(S // tm,), name=name, args=[dmix, wout, ya, ys, proj, proj, proj, proj],
                 in_specs=[row, pl.BlockSpec((D, D), lambda i: (0, 0)), row, row, gate(ga), gate(ga + 1), gate(gs), gate(gs + 1)],
                 out_specs=[row] * 4, out_shape=[shp] * 4)[0]


def _att_sg_dout(dya, dys, watt, wsg, oatt, name, comm=()):
    S, D = dya.shape
    nb, _, Db = watt.shape
    tm = _tile(S, 512)

    def body(dya_ref, dys_ref, wa_ref, ws_ref, oa_ref, do_ref, dz_ref, dvec_ref):
        def back(dy_ref, w_ref, rows):
            tot = None
            for j in range(nb):
                part = _nt(dy_ref[:, j * Db:(j + 1) * Db], w_ref[j, rows, :])
                tot = part if tot is None else tot + part
            return tot

        dov = back(dya_ref, wa_ref, slice(0, GROUP_W))
        do_ref[...] = dov
        for c0 in range(0, SG_W, GROUP_W):
            dz_ref[:, c0:c0 + GROUP_W] = back(dys_ref, ws_ref, slice(c0, c0 + GROUP_W)).astype(BF)
        prod = dov * oa_ref[...].astype(F32)
        for hh in range(HEADS_PER_GROUP):
            cols = slice(hh * HEAD_DIM, (hh + 1) * HEAD_DIM)
            dvec_ref[:, cols] = jnp.broadcast_to(jnp.sum(prod[:, cols], axis=-1, keepdims=True), (tm, HEAD_DIM))

    row = pl.BlockSpec((tm, D), lambda i: (i, 0))
    att = pl.BlockSpec((tm, GROUP_W), lambda i: (i, 0))
    return _call(body, grid=(S // tm,), name=name, args=[dya, dys, watt, wsg, oatt], comm=comm,
                 in_specs=[row, row, pl.BlockSpec((nb, GROUP_W, Db), lambda i: (0, 0, 0)), pl.BlockSpec((nb, SG_W, Db), lambda i: (0, 0, 0)), att],
                 out_specs=[att, pl.BlockSpec((tm, SG_W), lambda i: (i, 0)), att],
                 out_shape=[jax.ShapeDtypeStruct((S, GROUP_W), F32), jax.ShapeDtypeStruct((S, SG_W), BF), jax.ShapeDtypeStruct((S, GROUP_W), F32)])[0]


def _small_allreduce(pack, name):
    R = pack.shape[0]

    def body(p_ref, o_ref, gath, send, recv):
        x, y, c = _place()
        me = 4 * x + 2 * y + c
        gath[me] = p_ref[...]
        copies = []
        for r in range(1, N_DEV):
            px, py, pc = _flip(x, r & 4), _flip(y, r & 2), _flip(c, r & 1)
            peer = 4 * px + 2 * py + pc
            mk = lambda dst: pltpu.make_async_remote_copy(src_ref=p_ref, dst_ref=dst, send_sem=send.at[r - 1], recv_sem=recv.at[r - 1],
                                                          device_id=(px, py, pc), device_id_type=MESH)
            snd = mk(gath.at[me])
            snd.start()
            copies.append((snd, mk(gath.at[peer])))
        for snd, rcv in copies:
            rcv.wait_recv()
            snd.wait_send()
        acc = gath[0]
        for s in range(1, N_DEV):
            acc = acc + gath[s]
        o_ref[...] = acc

    vm = pl.BlockSpec(memory_space=pltpu.VMEM)
    return pl.pallas_call(
        body, name=name, in_specs=[vm], out_specs=vm, out_shape=jax.ShapeDtypeStruct(pack.shape, F32),
        scratch_shapes=[pltpu.VMEM((N_DEV, R, 128), F32), pltpu.SemaphoreType.DMA((7,)), pltpu.SemaphoreType.DMA((7,))],
        compiler_params=pltpu.CompilerParams(vmem_limit_bytes=VMEM_LIMIT),
    )(pack)


def _row_tile(R, C, elems=262144):
    fits = [t for t in range(16, R + 1, 16) if R % t == 0 and t * C <= elems]
    return max(fits) if fits else R


def _pair_add(parts, other, name):
    _, R, C = parts.shape
    tr = _row_tile(R, C, 1048576)

    def body(c_ref, p_ref, o_ref, s_ref):
        s_ref[0] = (p_ref[0].astype(F32) + o_ref[0].astype(F32)).astype(BF)

    core = lax.axis_index("c").astype(jnp.int32).reshape(1)
    return pl.pallas_call(
        body, name=name,
        grid_spec=pltpu.PrefetchScalarGridSpec(
            num_scalar_prefetch=1, grid=(N_CHIP, R // tr),
            in_specs=[pl.BlockSpec((1, tr, C), lambda q, i, c: (2 * q + c[0], i, 0)), pl.BlockSpec((1, tr, C), lambda q, i, c: (q, i, 0))],
            out_specs=pl.BlockSpec((1, tr, C), lambda q, i, c: (q, i, 0))),
        out_shape=jax.ShapeDtypeStruct((N_CHIP, R, C), BF),
        compiler_params=pltpu.CompilerParams(dimension_semantics=("arbitrary", "arbitrary"), vmem_limit_bytes=VMEM_LIMIT),
    )(core, parts, other)


def _adamw(parts, w, m, v, name):
    ns, R, C = parts.shape
    tr = _row_tile(R, C, 524288)
    c1 = 1.0 - ADAM_B1 ** ADAM_STEP
    c2 = 1.0 - ADAM_B2 ** ADAM_STEP

    def body(p_ref, w_ref, m_ref, v_ref, g_ref, d_ref, nm_ref, nv_ref):
        g = p_ref[0].astype(F32)
        for s in range(1, ns):
            g = g + p_ref[s].astype(F32)
        mn = ADAM_B1 * m_ref[...] + (1.0 - ADAM_B1) * g
        vn = ADAM_B2 * v_ref[...] + (1.0 - ADAM_B2) * (g * g)
        g_ref[...] = g
        nm_ref[...] = mn
        nv_ref[...] = vn
        d_ref[...] = -ADAM_LR * ((mn / c1) / (jnp.sqrt(vn / c2) + ADAM_EPS) + ADAM_WD * w_ref[...])

    row = pl.BlockSpec((tr, C), lambda i: (i, 0))
    shp = jax.ShapeDtypeStruct((R, C), F32)
    return _call(body, grid=(R // tr,), name=name, args=[parts, w, m, v],
                 in_specs=[pl.BlockSpec((ns, tr, C), lambda i: (0, i, 0)), row, row, row], out_specs=[row] * 4, out_shape=[shp] * 4)[0]


def _pad_rows(a, rows):
    return jnp.pad(a, ((0, rows - a.shape[0]), (0, 0)))


def kernel(x, ffn1_norm, ffn1_w_gate, ffn1_w_up, ffn1_w_down, mix_norm, w_in, sg_ln_g, sg_ln_b, sg_w, sg_b, w_att_out, w_sg_out, w_out, ffn2_norm, ffn2_w_gate, ffn2_w_up, ffn2_w_down, final_norm, loss_target, m_ffn1_norm, m_ffn1_w_gate, m_ffn1_w_up, m_ffn1_w_down, m_mix_norm, m_w_in, m_sg_ln_g, m_sg_ln_b, m_sg_w, m_sg_b, m_w_att_out, m_w_sg_out, m_w_out, m_ffn2_norm, m_ffn2_w_gate, m_ffn2_w_up, m_ffn2_w_down, m_final_norm, v_ffn1_norm, v_ffn1_w_gate, v_ffn1_w_up, v_ffn1_w_down, v_mix_norm, v_w_in, v_sg_ln_g, v_sg_ln_b, v_sg_w, v_sg_b, v_w_att_out, v_w_sg_out, v_w_out, v_ffn2_norm, v_ffn2_w_gate, v_ffn2_w_up, v_ffn2_w_down, v_final_norm):
    S, D = x.shape[1], x.shape[2]
    Pb = w_in.shape[2]
    P = N_DEV * Pb
    assert P == GA_OFF + 2 * D and D % (N_DEV * 128) == 0 and S % (BLK * DILATIONS[-1]) == 0
    xs, tgt = x[0], loss_target[0]

    sharded = dict(ffn1_w_gate=ffn1_w_gate, ffn1_w_up=ffn1_w_up, ffn1_w_down=ffn1_w_down, w_in=w_in, w_att_out=w_att_out,
                   w_sg_out=w_sg_out, w_out=w_out, ffn2_w_gate=ffn2_w_gate, ffn2_w_up=ffn2_w_up, ffn2_w_down=ffn2_w_down)
    cols = ("ffn1_w_gate", "ffn1_w_up", "w_in", "ffn2_w_gate", "ffn2_w_up")
    local = lambda n, a: a[0].T if n in cols else a[0]
    back = lambda n, a: a.T[None] if n in cols else a[None]
    wloc = {n: local(n, w) for n, w in sharded.items()}
    sb = {n: w.astype(BF) for n, w in wloc.items()}

    (h1,), ((wg1,),) = _rms_fwd(xs, ffn1_norm, "rms1", comm=[_Gather([sb["ffn1_w_gate"]], 1.0, 1.0)])
    (g1,), ((wu1,),) = _ffn_gate(h1, wg1, "ffn1_gate", comm=[_Gather([sb["ffn1_w_up"]], 0.9, 0.55)])
    (u1, a1), ((wd1,),) = _ffn_up_act(h1, wu1, g1, "ffn1_up_act", comm=[_Gather([sb["ffn1_w_down"]], 0.9, 0.55)])
    (x1, h2), ((winT8,),) = _ffn_down_norm(a1, wd1, xs, mix_norm, "ffn1_down", comm=[_Gather([sb["w_in"]], 1.0, 0.7)])
    winT = winT8.reshape(P, D)
    (qkv, rest), ((wg2, wu2),) = _proj_split(h2, winT, _tile(S, 1024), 512, U_OFF, "proj",
                                           comm=[_Gather([sb["ffn2_w_gate"], sb["ffn2_w_up"]], 0.85, 0.5)])
    tabs = _rope_tables(S)
    rides = [[_Gather([sb["w_att_out"], sb["w_sg_out"]], 0.9, 0.5)], [_Gather([sb["w_out"]], 0.85, 0.45)], []]
    os, lses, late = [], [], []
    for gi, d in enumerate(DILATIONS):
        (o, l), got_here = _att_fwd(qkv, tabs, gi, d, f"att_fwd{gi}", comm=rides[gi])
        late += [w for g in got_here for w in g]
        os.append(o)
        lses.append(l)
    watt, wsg, wout8 = late
    wout = wout8.reshape(D, D)
    oatt, lse = _att_combine(os, lses, "att_combine")
    sgw = sg_w[0]
    sgbT = jnp.pad(sg_b[0].T, ((0, 0), (0, BLK - SG_GROUPS)))
    z = _sg_fwd(rest, sgw, sgbT, sg_ln_g, sg_ln_b, "sg_fwd")
    (ya, ys, merged), _ = _gate_merge(oatt, z, watt, wsg, rest, "gate_merge")
    x2, h3 = _mix_out(merged, wout, x1, ffn2_norm, "mix_out")
    (g3, u3, a3), ((wd2,),) = _ffn_up(h3, wg2, wu2, "ffn2_up", comm=[_Gather([sb["ffn2_w_down"]], 0.6, 0.35)])
    dx3, dyb3, d_final, loss_part = _ffn_down_loss(a3, wd2, x2, final_norm.reshape(1, D), tgt, "ffn2_down_loss")

    Fb = wg2.shape[1]
    Db = watt.shape[2]
    p_pad = -(-P // PROJ_TK) * PROJ_TK
    win_tail = _pad_rows(winT[p_pad - PROJ_TK:], PROJ_TK)
    (dg3, du3), _ = _ffn_bwd_act(dyb3, wd2, g3, u3, "ffn2_bwd_act")
    (dwd2,), _ = _ffn_dwd(a3, dyb3, "ffn2_dwd")
    (dwg2, dwu2), _ = _ffn_dwgu(h3, dg3, du3, "ffn2_dwgu")
    ffn2_parts = [dwd2, dwg2, dwu2]
    (dx2, dmixb, d_ffn2n), (ffn2_other,) = _dh_rms_bwd([(dg3, wg2), (du3, wu2)], True, Fb, x2, ffn2_norm, dx3, 1.0, "ffn2_dh",
                                                     comm=[_Swap(ffn2_parts)])
    ffn2_sums = [_pair_add(p, o, f"pair_ffn2_{i}") for i, (p, o) in enumerate(zip(ffn2_parts, ffn2_other))]

    dya, dys, dga, dgs = _mix_bwd_gate(dmixb, wout, ya, ys, rest, "mix_bwd_gate")
    (dwout,), _ = _mm_tn(merged, dmixb, _tile(D, 1024), _tile(D, 1024), _tile(S, 1024), False, "dw_out")
    do, dz, dvec = _att_sg_dout(dya, dys, watt, wsg, oatt, "att_sg_dout")
    (dwatt,), _ = _mm_tn(oatt, dya, GROUP_W, 2 * Db, _tile(S, 1024), Db, "dw_att")
    (dwsg,), _ = _mm_tn(z, dys, SG_W, 2 * Db, _tile(S, 1024), Db, "dw_sg")
    mix_parts = [dwout.reshape(N_DEV, D // N_DEV, D), dwatt, dwsg]
    du, dvs, d_sgw, d_sgbT, d_lng, d_lnb = _sg_bwd(rest, dz, sgw, sgbT, sg_ln_g, sg_ln_b, "sg_bwd")
    dqs, dks, dvs_att, ffn2_got = [], [], [], []
    for gi, d in enumerate(DILATIONS):
        ride = [_Ici([ffn2_sums[0]])] if gi == 2 else []
        (dq, dk, dv), got_here = _att_bwd(qkv, tabs, do, lse, dvec, gi, d, f"att_bwd{gi}", comm=ride)
        ffn2_got += [g[0] for g in got_here]
        dqs.append(dq)
        dks.append(dk)
        dvs_att.append(dv)
    dproj = jnp.concatenate([t.astype(BF) for t in dqs + dks + dvs_att] + [du, dvs, dga, dgs, jnp.zeros((S, p_pad - P), BF)], axis=1)
    (dx1, dyb1, d_mixn), (ffn2_rest, mix_other) = _dh_rms_bwd([(dproj, winT)], False, PROJ_TK, x1, mix_norm, dx2, 0.5, "proj_dh",
                                                            comm=[_Ici(ffn2_sums[1:]), _Swap(mix_parts)], tail=win_tail)
    ffn2_got += ffn2_rest
    mix_sums = [_pair_add(p, o, f"pair_mix_{i}") for i, (p, o) in enumerate(zip(mix_parts, mix_other))]
    (dwd1,), (mix_got,) = _ffn_dwd(a1, dyb1, "ffn1_dwd", comm=[_Ici(mix_sums)])
    rows = lambda a: a.reshape(-1, 128)
    pad8 = lambda a: _pad_rows(a, -(-a.shape[0] // 8) * 8)
    small = [("sg_w", rows(d_sgw), sg_w, m_sg_w, v_sg_w), ("mix_norm", rows(d_mixn), mix_norm, m_mix_norm, v_mix_norm),
             ("ffn2_norm", rows(d_ffn2n), ffn2_norm, m_ffn2_norm, v_ffn2_norm), ("final_norm", rows(d_final), final_norm, m_final_norm, v_final_norm),
             ("sg_ln_g", rows(d_lng), sg_ln_g, m_sg_ln_g, v_sg_ln_g), ("sg_ln_b", rows(d_lnb), sg_ln_b, m_sg_ln_b, v_sg_ln_b),
             ("sg_b", d_sgbT[:, :SG_GROUPS].T, sg_b, m_sg_b, v_sg_b)]
    gpack = jnp.concatenate([pad8(g) for _, g, _, _, _ in small] + [pad8(loss_part)], axis=0)
    (dwin,), ((wd1_other,), (gpacks,)) = _mm_tn(dproj, h2, 512, D, _tile(S, 2048), False, "dw_in", mrows=P,
                                              comm=[_Swap([dwd1]), _Spread([gpack])])
    dwin = dwin.reshape(N_DEV, Pb, D)
    wd1_sum = _pair_add(dwd1, wd1_other, "pair_wd1")
    (dg1, du1), ((wd1_got,), (win_other,)) = _ffn_bwd_act(dyb1, wd1, g1, u1, "ffn1_bwd_act", comm=[_Ici([wd1_sum]), _Swap([dwin])])
    win_sum = _pair_add(dwin, win_other, "pair_win")
    (dwg1, dwu1), ((win_got,),) = _ffn_dwgu(h1, dg1, du1, "ffn1_dwgu", comm=[_Ici([win_sum])])
    gu_parts = [dwg1, dwu1]
    gu_other = _comm_only(_Swap(gu_parts), "swap_gu1")
    gu_sums = [_pair_add(p, o, f"pair_gu1_{i}") for i, (p, o) in enumerate(zip(gu_parts, gu_other))]
    (dx0, _, d_ffn1n), (gu_got,) = _dh_rms_bwd([(dg1, wg1), (du1, wu1)], True, Fb, xs, ffn1_norm, dx1, 1.0, "ffn1_dh",
                                               comm=[_Ici(gu_sums)])

    got = dict(ffn2_w_down=ffn2_got[0], ffn2_w_gate=ffn2_got[1], ffn2_w_up=ffn2_got[2], w_out=mix_got[0], w_att_out=mix_got[1],
               w_sg_out=mix_got[2], w_in=win_got, ffn1_w_gate=gu_got[0], ffn1_w_up=gu_got[1], ffn1_w_down=wd1_got)
    moments = dict(ffn1_w_gate=(m_ffn1_w_gate, v_ffn1_w_gate), ffn1_w_up=(m_ffn1_w_up, v_ffn1_w_up),
                   ffn1_w_down=(m_ffn1_w_down, v_ffn1_w_down), w_in=(m_w_in, v_w_in), w_att_out=(m_w_att_out, v_w_att_out),
                   w_sg_out=(m_w_sg_out, v_w_sg_out), w_out=(m_w_out, v_w_out), ffn2_w_gate=(m_ffn2_w_gate, v_ffn2_w_gate),
                   ffn2_w_up=(m_ffn2_w_up, v_ffn2_w_up), ffn2_w_down=(m_ffn2_w_down, v_ffn2_w_down))
    res = {}
    for n in sharded:
        mm, vv = moments[n]
        outs = _adamw(got[n], wloc[n], local(n, mm), local(n, vv), "adamw_" + n)
        res[n] = [back(n, o) for o in outs]

    zero8 = jnp.zeros((8, 128), F32)
    wpack = jnp.concatenate([pad8(rows(w)) for _, _, w, _, _ in small] + [zero8], axis=0)
    mpack = jnp.concatenate([pad8(rows(m)) for _, _, _, m, _ in small] + [zero8], axis=0)
    vpack = jnp.concatenate([pad8(rows(v)) for _, _, _, _, v in small] + [zero8], axis=0)
    packs = _adamw(gpacks, wpack, mpack, vpack, "adamw_small")
    off = 0
    for n, g, w, _, _ in small:
        r = g.shape[0]
        res[n] = [p[off:off + r].reshape(w.shape) for p in packs]
        off += -(-r // 8) * 8
    loss = packs[0][off, 0]
    g_first = _small_allreduce(rows(d_ffn1n), "allreduce_ffn1_norm")
    res["ffn1_norm"] = [p.reshape(ffn1_norm.shape) for p in
                        _adamw(g_first[None], rows(ffn1_norm), rows(m_ffn1_norm), rows(v_ffn1_norm), "adamw_ffn1_norm")]

    order = ["ffn1_norm", "ffn1_w_gate", "ffn1_w_up", "ffn1_w_down", "mix_norm", "w_in", "sg_ln_g", "sg_ln_b", "sg_w", "sg_b",
             "w_att_out", "w_sg_out", "w_out", "ffn2_norm", "ffn2_w_gate", "ffn2_w_up", "ffn2_w_down", "final_norm"]
    return (loss, dx0[None], *[res[n][0] for n in order], *[res[n][1] for n in order], *[res[n][2] for n in order],
            *[res[n][3] for n in order])
```

```python
import math

import jax
import jax.numpy as jnp
from jax import lax
from jax.experimental import pallas as pl
from jax.experimental.pallas import tpu as pltpu

BF = jnp.bfloat16
F32 = jnp.float32
MESH = pl.DeviceIdType.MESH
N_DEV = 8
N_CHIP = 4

HEAD_DIM = 128
HEADS_PER_GROUP = 4
GROUP_W = HEADS_PER_GROUP * HEAD_DIM
DILATIONS = (1, 4, 16)
ATT_W = len(DILATIONS) * GROUP_W
SG_W = 1536
SG_GROUPS = 12
BLK = 128
ROPE_DIM = 32
ROPE_THETA = 500000.0
NORM_EPS = 1e-6
LN_EPS = 1e-5
Q_OFF, K_OFF, V_OFF, U_OFF, VS_OFF, GA_OFF = 0, ATT_W, 2 * ATT_W, 3 * ATT_W, 3 * ATT_W + SG_W, 3 * ATT_W + 2 * SG_W

ADAM_LR, ADAM_B1, ADAM_B2, ADAM_EPS, ADAM_WD, ADAM_STEP = 0.001, 0.9, 0.999, 1e-08, 0.01, 10

VMEM_LIMIT = 56 * 1024 * 1024
NEG = -1e30
ANY = pl.BlockSpec(memory_space=pl.ANY)
EPI_ROWS = 128
ACC_COLS = 512
FFN_PAIR = 2
FFN_ROWS = 1024
PROJ_TK = 1536
R_U, R_VS, R_GA = 0, SG_W, 2 * SG_W


def _once(shape, index_map):
    return pl.BlockSpec(shape, index_map, pipeline_mode=pl.Buffered(1))


def _tile(n, pref):
    t = min(n, pref)
    while n % t:
        t //= 2
    return t


def _nt(a, b):
    return lax.dot_general(a, b, (((1,), (1,)), ((), ())), preferred_element_type=F32)


def _tn(a, b):
    return lax.dot_general(a, b, (((0,), (0,)), ((), ())), preferred_element_type=F32)


def _nn(a, b):
    return jnp.dot(a, b, preferred_element_type=F32)


def _acc_dots(acc_ref, terms, transposed_rhs=False):
    n = acc_ref.shape[1]
    width = min(n, ACC_COLS)
    for c0 in range(0, n, width):
        cols = slice(c0, c0 + width)
        tot = None
        for lhs, rhs in terms:
            part = _nt(lhs, rhs(cols)) if transposed_rhs else _nn(lhs, rhs(cols))
            tot = part if tot is None else tot + part
        acc_ref[:, cols] += tot


def _gauss_cdf(x):
    return 0.5 * (1.0 + lax.erf(x * (2.0 ** -0.5)))


def _gelu_grad(x, cdf):
    return cdf + x * jnp.exp(-0.5 * x * x) * (1.0 / math.sqrt(2.0 * math.pi))


def _place():
    x, y, c = lax.axis_index("x"), lax.axis_index("y"), lax.axis_index("c")
    return x, y, c


def _flip(v, bit):
    return 1 - v if bit else v


class _Gather:
    def __init__(self, shards, mid_frac=1.0, relay_frac=0.5):
        self.arrays = list(shards)
        self.relay_frac = relay_frac
        self.mid_frac = mid_frac
        nw = len(shards)
        self.out_shape = [jax.ShapeDtypeStruct((N_DEV,) + s.shape, s.dtype) for s in shards]
        self.scratch = [pltpu.SemaphoreType.DMA((nw, 7)), pltpu.SemaphoreType.DMA((nw, 7)), pltpu.SemaphoreType.DMA((nw,))]

    def _parts(self, ins, outs, sems):
        x, y, c = _place()
        send, recv, loc = sems
        south = c == 0
        near = (jnp.where(south, x, 1 - x), jnp.where(south, 1 - y, y), c)
        far = (jnp.where(south, 1 - x, x), jnp.where(south, y, 1 - y), c)
        diag = (1 - x, 1 - y, c)

        def copy(k, s, block, to, src=None):
            dst = outs[k].at[4 * block[0] + 2 * block[1] + block[2]]
            return pltpu.make_async_remote_copy(src_ref=dst if src is None else src, dst_ref=dst, send_sem=send.at[k, s],
                                                recv_sem=recv.at[k, s], device_id=to, device_id_type=MESH)

        def first(k):
            me = (x, y, c)
            return [copy(k, 0, me, (x, y, 1 - c), src=ins[k]), copy(k, 1, me, (1 - x, y, c), src=ins[k]),
                    copy(k, 2, me, (x, 1 - y, c), src=ins[k])]

        def local(k):
            return pltpu.make_async_copy(ins[k], outs[k].at[4 * x + 2 * y + c], loc.at[k])

        return x, y, c, near, far, diag, copy, first, local

    def start(self, ins, outs, sems):
        *_, first, local = self._parts(ins, outs, sems)
        for k in range(len(ins)):
            local(k).start()
            for cp in first(k):
                cp.start()

    def relay(self, ins, outs, sems):
        x, y, c, near, far, _, copy, _, _ = self._parts(ins, outs, sems)
        for k in range(len(ins)):
            copy(k, 2 - c, near, (x, y, c)).wait_recv()
            copy(k, 3, near, far).start()
            copy(k, 5 - c, near, (x, y, 1 - c)).start()

    def mid(self, ins, outs, sems):
        x, y, c, _, far, diag, copy, _, _ = self._parts(ins, outs, sems)
        for k in range(len(ins)):
            copy(k, 1 + c, far, (x, y, c)).wait_recv()
            copy(k, 4 + c, far, (x, y, 1 - c)).start()
            copy(k, 3, diag, (x, y, c)).wait_recv()
            copy(k, 6, diag, (x, y, 1 - c)).start()

    def finish(self, ins, outs, sems):
        x, y, c, near, _, _, copy, first, local = self._parts(ins, outs, sems)
        sib = (x, y, 1 - c)
        for k in range(len(ins)):
            copy(k, 0, sib, (x, y, c)).wait_recv()
            copy(k, 4, (1 - x, y, 1 - c), (x, y, c)).wait_recv()
            copy(k, 5, (x, 1 - y, 1 - c), (x, y, c)).wait_recv()
            copy(k, 6, (1 - x, 1 - y, 1 - c), (x, y, c)).wait_recv()
        for k in range(len(ins)):
            for cp in first(k):
                cp.wait_send()
            for s in (3, 4, 5, 6):
                copy(k, s, near, sib).wait_send()
            local(k).wait()


class _Swap:
    def __init__(self, parts):
        self.arrays = list(parts)
        nw = len(parts)
        self.out_shape = [jax.ShapeDtypeStruct((N_CHIP,) + p.shape[1:], p.dtype) for p in parts]
        self.scratch = [pltpu.SemaphoreType.DMA((nw, N_CHIP)), pltpu.SemaphoreType.DMA((nw, N_CHIP))]

    def _copy(self, ins, outs, sems, k, q):
        x, y, c = _place()
        return pltpu.make_async_remote_copy(src_ref=ins[k].at[2 * q + 1 - c], dst_ref=outs[k].at[q], send_sem=sems[0].at[k, q],
                                            recv_sem=sems[1].at[k, q], device_id=(x, y, 1 - c), device_id_type=MESH)

    mid_frac = None

    def start(self, ins, outs, sems):
        for k in range(len(ins)):
            for q in range(N_CHIP):
                self._copy(ins, outs, sems, k, q).start()

    def finish(self, ins, outs, sems):
        for k in range(len(ins)):
            for q in range(N_CHIP):
                self._copy(ins, outs, sems, k, q).wait()


class _Ici:
    mid_frac = None

    def __init__(self, sums):
        self.arrays = list(sums)
        nw = len(sums)
        self.out_shape = [jax.ShapeDtypeStruct(s.shape, s.dtype) for s in sums]
        self.scratch = [pltpu.SemaphoreType.DMA((nw, 3)), pltpu.SemaphoreType.DMA((nw, 3)), pltpu.SemaphoreType.DMA((nw,))]

    def _copies(self, ins, outs, sems, k):
        x, y, c = _place()
        myq = 2 * x + y
        out = []
        for r in range(1, N_CHIP):
            px, py = _flip(x, r & 2), _flip(y, r & 1)
            pq = 2 * px + py
            mk = lambda dst: pltpu.make_async_remote_copy(src_ref=ins[k].at[pq], dst_ref=dst, send_sem=sems[0].at[k, r - 1],
                                                          recv_sem=sems[1].at[k, r - 1], device_id=(px, py, c), device_id_type=MESH)
            out.append((mk(outs[k].at[myq]), mk(outs[k].at[pq])))
        return out, pltpu.make_async_copy(ins[k].at[myq], outs[k].at[myq], sems[2].at[k])

    def start(self, ins, outs, sems):
        for k in range(len(ins)):
            remote, local = self._copies(ins, outs, sems, k)
            local.start()
            for snd, _ in remote:
                snd.start()

    def finish(self, ins, outs, sems):
        for k in range(len(ins)):
            remote, local = self._copies(ins, outs, sems, k)
            for snd, rcv in remote:
                rcv.wait_recv()
                snd.wait_send()
            local.wait()


class _Spread:
    mid_frac = None

    def __init__(self, arrays):
        self.arrays = list(arrays)
        nw = len(arrays)
        self.out_shape = [jax.ShapeDtypeStruct((N_DEV,) + a.shape, a.dtype) for a in arrays]
        self.scratch = [pltpu.SemaphoreType.DMA((nw, 7)), pltpu.SemaphoreType.DMA((nw, 7)), pltpu.SemaphoreType.DMA((nw,))]

    def _copies(self, ins, outs, sems, k):
        x, y, c = _place()
        me = 4 * x + 2 * y + c
        out = []
        for r in range(1, N_DEV):
            px, py, pc = _flip(x, r & 4), _flip(y, r & 2), _flip(c, r & 1)
            peer = 4 * px + 2 * py + pc
            mk = lambda dst: pltpu.make_async_remote_copy(src_ref=ins[k], dst_ref=dst, send_sem=sems[0].at[k, r - 1],
                                                          recv_sem=sems[1].at[k, r - 1], device_id=(px, py, pc), device_id_type=MESH)
            out.append((mk(outs[k].at[me]), mk(outs[k].at[peer])))
        return out, pltpu.make_async_copy(ins[k], outs[k].at[me], sems[2].at[k])

    def start(self, ins, outs, sems):
        for k in range(len(ins)):
            remote, local = self._copies(ins, outs, sems, k)
            local.start()
            for snd, _ in remote:
                snd.start()

    def finish(self, ins, outs, sems):
        for k in range(len(ins)):
            remote, local = self._copies(ins, outs, sems, k)
            for snd, rcv in remote:
                rcv.wait_recv()
                snd.wait_send()
            local.wait()


def _call(body, *, grid, in_specs, out_specs, out_shape, name, args, scratch=(), comm=()):
    comm = list(comm)
    n_in, n_out, n_scr = len(in_specs), len(out_specs), len(scratch)
    total = math.prod(grid) if grid else 1

    def wrapped(*refs):
        p = n_in
        cin = []
        for cm in comm:
            cin.append(refs[p:p + len(cm.arrays)])
            p += len(cm.arrays)
        own_out = refs[p:p + n_out]
        p += n_out
        cout = []
        for cm in comm:
            cout.append(refs[p:p + len(cm.arrays)])
            p += len(cm.arrays)
        own_scr = refs[p:p + n_scr]
        p += n_scr
        csem = []
        for cm in comm:
            csem.append(refs[p:p + len(cm.scratch)])
            p += len(cm.scratch)
        step = 0
        for axis, g in enumerate(grid):
            step = step * g + pl.program_id(axis)

        def at(when, what):
            if total == 1:
                what()
            else:
                pl.when(step == when)(what)

        def starts():
            for cm, i, o, s in zip(comm, cin, cout, csem):
                cm.start(i, o, s)

        def finishes():
            for cm, i, o, s in zip(comm, cin, cout, csem):
                cm.finish(i, o, s)

        if comm:
            at(0, starts)
        if body is not None:
            body(*refs[:n_in], *own_out, *own_scr)
        for cm, i, o, s in zip(comm, cin, cout, csem):
            if cm.mid_frac is not None:
                at(min(total - 1, int(total * cm.relay_frac)), lambda cm=cm, i=i, o=o, s=s: cm.relay(i, o, s))
                at(min(total - 1, int(total * cm.mid_frac)), lambda cm=cm, i=i, o=o, s=s: cm.mid(i, o, s))
        if comm:
            at(total - 1, finishes)

    kw = dict(grid=tuple(grid)) if grid else {}
    outs = pl.pallas_call(
        wrapped, name=name, **kw,
        in_specs=list(in_specs) + [ANY for cm in comm for _ in cm.arrays],
        out_specs=list(out_specs) + [ANY for cm in comm for _ in cm.arrays],
        out_shape=list(out_shape) + [s for cm in comm for s in cm.out_shape],
        scratch_shapes=list(scratch) + [s for cm in comm for s in cm.scratch],
        compiler_params=pltpu.CompilerParams(dimension_semantics=("arbitrary",) * len(grid), vmem_limit_bytes=VMEM_LIMIT),
    )(*args, *[a for cm in comm for a in cm.arrays])
    own, p, per = list(outs[:n_out]), n_out, []
    for cm in comm:
        per.append(list(outs[p:p + len(cm.arrays)]))
        p += len(cm.arrays)
    return own, per


def _comm_only(cm, name):
    return _call(None, grid=(), in_specs=[], out_specs=[], out_shape=[], name=name, args=[], comm=[cm])[1][0]


def _rms_fwd(x, g, name, comm=()):
    S, D = x.shape
    tm = _tile(S, 512)

    def body(x_ref, g_ref, o_ref):
        xv = x_ref[...]
        r = lax.rsqrt(jnp.mean(xv * xv, axis=-1, keepdims=True) + NORM_EPS)
        o_ref[...] = (xv * r * g_ref[...]).astype(BF)

    return _call(body, grid=(S // tm,), name=name, args=[x, g], comm=comm,
                 in_specs=[pl.BlockSpec((tm, D), lambda i: (i, 0)), pl.BlockSpec((1, D), lambda i: (0, 0))],
                 out_specs=[pl.BlockSpec((tm, D), lambda i: (i, 0))], out_shape=[jax.ShapeDtypeStruct((S, D), BF)])


def _ffn_up(h, wg, wu, name, comm=()):
    S, D = h.shape
    nb, Fb, _ = wg.shape
    tm = _tile(S, FFN_ROWS)

    def body(h_ref, wg_ref, wu_ref, g_ref, u_ref, a_ref):
        hv = h_ref[...]
        g = _nt(hv, wg_ref[0])
        u = _nt(hv, wu_ref[0])
        g_ref[0] = g.astype(BF)
        u_ref[0] = u.astype(BF)
        a_ref[0] = (g * jax.nn.sigmoid(g) * u).astype(BF)

    act = pl.BlockSpec((1, tm, Fb), lambda j, i: (j, i, 0))
    w = pl.BlockSpec((1, Fb, D), lambda j, i: (j, 0, 0))
    shp = jax.ShapeDtypeStruct((nb, S, Fb), BF)
    return _call(body, grid=(nb, S // tm), name=name, args=[h, wg, wu], comm=comm,
                 in_specs=[pl.BlockSpec((tm, D), lambda j, i: (i, 0)), w, w], out_specs=[act, act, act], out_shape=[shp, shp, shp])


def _ffn_gate(h, wg, name, comm=()):
    S, D = h.shape
    nb, Fb, _ = wg.shape
    tm = _tile(S, FFN_ROWS)

    def body(h_ref, wg_ref, g_ref):
        g_ref[0] = _nt(h_ref[...], wg_ref[0]).astype(BF)

    act = pl.BlockSpec((1, tm, Fb), lambda j, i: (j, i, 0))
    return _call(body, grid=(nb, S // tm), name=name, args=[h, wg], comm=comm,
                 in_specs=[pl.BlockSpec((tm, D), lambda j, i: (i, 0)), pl.BlockSpec((1, Fb, D), lambda j, i: (j, 0, 0))],
                 out_specs=[act], out_shape=[jax.ShapeDtypeStruct((nb, S, Fb), BF)])


def _ffn_up_act(h, wu, g, name, comm=()):
    S, D = h.shape
    nb, Fb, _ = wu.shape
    tm = _tile(S, FFN_ROWS)

    def body(h_ref, wu_ref, g_ref, u_ref, a_ref):
        u = _nt(h_ref[...], wu_ref[0])
        gv = g_ref[0].astype(F32)
        u_ref[0] = u.astype(BF)
        a_ref[0] = (gv * jax.nn.sigmoid(gv) * u).astype(BF)

    act = pl.BlockSpec((1, tm, Fb), lambda j, i: (j, i, 0))
    shp = jax.ShapeDtypeStruct((nb, S, Fb), BF)
    return _call(body, grid=(nb, S // tm), name=name, args=[h, wu, g], comm=comm,
                 in_specs=[pl.BlockSpec((tm, D), lambda j, i: (i, 0)), pl.BlockSpec((1, Fb, D), lambda j, i: (j, 0, 0)), act],
                 out_specs=[act, act], out_shape=[shp, shp])


def _ffn_down_norm(a, wd, x, gn, name, comm=()):
    nb, S, Fb = a.shape
    D = wd.shape[2]
    tm = _tile(S, 512)

    nj = nb // FFN_PAIR

    def body(a_ref, wd_ref, x_ref, gn_ref, xo_ref, hn_ref, acc_ref):
        j = pl.program_id(1)

        @pl.when(j == 0)
        def _():
            acc_ref[...] = jnp.zeros_like(acc_ref)

        _acc_dots(acc_ref, [(a_ref[b], lambda cols, b=b: wd_ref[b, :, cols]) for b in range(FFN_PAIR)])

        @pl.when(j == nj - 1)
        def _():
            def chunk(t, carry):
                rows = pl.ds(pl.multiple_of(t * EPI_ROWS, EPI_ROWS), EPI_ROWS)
                xo = x_ref[rows, :] + 0.5 * acc_ref[rows, :]
                r = lax.rsqrt(jnp.mean(xo * xo, axis=-1, keepdims=True) + NORM_EPS)
                xo_ref[rows, :] = xo
                hn_ref[rows, :] = (xo * r * gn_ref[...]).astype(BF)
                return carry

            lax.fori_loop(0, tm // EPI_ROWS, chunk, 0)

    row = pl.BlockSpec((tm, D), lambda i, j: (i, 0))
    return _call(body, grid=(S // tm, nj), name=name, args=[a, wd, x, gn], comm=comm,
                 in_specs=[pl.BlockSpec((FFN_PAIR, tm, Fb), lambda i, j: (j, i, 0)), pl.BlockSpec((FFN_PAIR, Fb, D), lambda i, j: (j, 0, 0)),
                           row, pl.BlockSpec((1, D), lambda i, j: (0, 0))],
                 out_specs=[row, row], out_shape=[jax.ShapeDtypeStruct((S, D), F32), jax.ShapeDtypeStruct((S, D), BF)],
                 scratch=[pltpu.VMEM((tm, D), F32)])


def _ffn_down_loss(a, wd, x, gf, tgt, name):
    nb, S, Fb = a.shape
    D = wd.shape[2]
    tm = _tile(S, 512)

    nj = nb // FFN_PAIR

    def body(a_ref, wd_ref, x_ref, gf_ref, t_ref, dx_ref, dxb_ref, dgf_ref, loss_ref, acc_ref):
        i, j = pl.program_id(0), pl.program_id(1)

        @pl.when(j == 0)
        def _():
            acc_ref[...] = jnp.zeros_like(acc_ref)

        _acc_dots(acc_ref, [(a_ref[b], lambda cols, b=b: wd_ref[b, :, cols]) for b in range(FFN_PAIR)])

        @pl.when((j == nj - 1) & (i == 0))
        def _():
            dgf_ref[...] = jnp.zeros_like(dgf_ref)
            loss_ref[...] = jnp.zeros_like(loss_ref)

        @pl.when(j == nj - 1)
        def _():
            def chunk(t, carry):
                rows = pl.ds(pl.multiple_of(t * EPI_ROWS, EPI_ROWS), EPI_ROWS)
                xo = x_ref[rows, :] + 0.5 * acc_ref[rows, :]
                r = lax.rsqrt(jnp.mean(xo * xo, axis=-1, keepdims=True) + NORM_EPS)
                xh = xo * r
                gf = gf_ref[...]
                e = xh * gf - t_ref[rows, :]
                loss_ref[...] += jnp.sum(jnp.mean(e * e, axis=-1, keepdims=True), axis=0, keepdims=True) * 0.5
                dy = e * (1.0 / D)
                dgf_ref[...] += jnp.sum(dy * xh, axis=0, keepdims=True)
                dxh = dy * gf
                dx = r * (dxh - xh * jnp.mean(dxh * xh, axis=-1, keepdims=True))
                dx_ref[rows, :] = dx
                dxb_ref[rows, :] = (0.5 * dx).astype(BF)
                return carry

            lax.fori_loop(0, tm // EPI_ROWS, chunk, 0)

    row = pl.BlockSpec((tm, D), lambda i, j: (i, 0))
    once = row
    vec = pl.BlockSpec((1, D), lambda i, j: (0, 0))
    return _call(body, grid=(S // tm, nj), name=name, args=[a, wd, x, gf, tgt],
                 in_specs=[pl.BlockSpec((FFN_PAIR, tm, Fb), lambda i, j: (j, i, 0)), pl.BlockSpec((FFN_PAIR, Fb, D), lambda i, j: (j, 0, 0)),
                           once, vec, once],
                 out_specs=[row, row, vec, pl.BlockSpec((1, 128), lambda i, j: (0, 0))],
                 out_shape=[jax.ShapeDtypeStruct((S, D), F32), jax.ShapeDtypeStruct((S, D), BF), jax.ShapeDtypeStruct((1, D), F32),
                            jax.ShapeDtypeStruct((1, 128), F32)],
                 scratch=[pltpu.VMEM((tm, D), F32)])[0]


def _ffn_bwd_act(dyb, wd, g, u, name, comm=()):
    S, D = dyb.shape
    nb, Fb, _ = wd.shape
    tm = _tile(S, FFN_ROWS)

    def body(dy_ref, wd_ref, g_ref, u_ref, dg_ref, du_ref):
        da = _nt(dy_ref[...], wd_ref[0])
        gv = g_ref[0].astype(F32)
        uv = u_ref[0].astype(F32)
        sg = jax.nn.sigmoid(gv)
        du_ref[0] = (da * gv * sg).astype(BF)
        dg_ref[0] = (da * uv * sg * (1.0 + gv * (1.0 - sg))).astype(BF)

    act = pl.BlockSpec((1, tm, Fb), lambda j, i: (j, i, 0))
    shp = jax.ShapeDtypeStruct((nb, S, Fb), BF)
    return _call(body, grid=(nb, S // tm), name=name, args=[dyb, wd, g, u], comm=comm,
                 in_specs=[pl.BlockSpec((tm, D), lambda j, i: (i, 0)), pl.BlockSpec((1, Fb, D), lambda j, i: (j, 0, 0)), act, act],
                 out_specs=[act, act], out_shape=[shp, shp])


def _ffn_dwd(a, dyb, name, comm=()):
    nb, S, Fb = a.shape
    D = dyb.shape[1]
    ts = _tile(S, FFN_ROWS)
    ns = S // ts

    def body(a_ref, dy_ref, o_ref, acc_ref):
        s = pl.program_id(1)

        @pl.when(s == 0)
        def _():
            acc_ref[...] = jnp.zeros_like(acc_ref)

        acc_ref[...] += _tn(a_ref[0], dy_ref[...])

        @pl.when(s == ns - 1)
        def _():
            o_ref[0] = acc_ref[...].astype(BF)

    return _call(body, grid=(nb, ns), name=name, args=[a, dyb], comm=comm,
                 in_specs=[pl.BlockSpec((1, ts, Fb), lambda j, s: (j, s, 0)), pl.BlockSpec((ts, D), lambda j, s: (s, 0))],
                 out_specs=[pl.BlockSpec((1, Fb, D), lambda j, s: (j, 0, 0))], out_shape=[jax.ShapeDtypeStruct((nb, Fb, D), BF)],
                 scratch=[pltpu.VMEM((Fb, D), F32)])


def _ffn_dwgu(h, dg, du, name, comm=()):
    S, D = h.shape
    nb, _, Fb = dg.shape
    ts = _tile(S, FFN_ROWS)
    ns = S // ts

    def body(h_ref, dg_ref, du_ref, og_ref, ou_ref, accg_ref, accu_ref):
        s = pl.program_id(1)

        @pl.when(s == 0)
        def _():
            accg_ref[...] = jnp.zeros_like(accg_ref)
            accu_ref[...] = jnp.zeros_like(accu_ref)

        hv = h_ref[...]
        accg_ref[...] += _tn(dg_ref[0], hv)
        accu_ref[...] += _tn(du_ref[0], hv)

        @pl.when(s == ns - 1)
        def _():
            og_ref[0] = accg_ref[...].astype(BF)
            ou_ref[0] = accu_ref[...].astype(BF)

    act = pl.BlockSpec((1, ts, Fb), lambda j, s: (j, s, 0))
    out = pl.BlockSpec((1, Fb, D), lambda j, s: (j, 0, 0))
    shp = jax.ShapeDtypeStruct((nb, Fb, D), BF)
    return _call(body, grid=(nb, ns), name=name, args=[h, dg, du], comm=comm,
                 in_specs=[pl.BlockSpec((ts, D), lambda j, s: (s, 0)), act, act], out_specs=[out, out], out_shape=[shp, shp],
                 scratch=[pltpu.VMEM((Fb, D), F32), pltpu.VMEM((Fb, D), F32)])


def _dh_rms_bwd(pairs, blocked, tk, x, gn, dxo, out_scale, name, comm=(), tail=None):
    S, D = x.shape
    nk = pairs[0][0].shape[0] if blocked else pairs[0][0].shape[1] // tk
    tm = _tile(S, 512)
    npair = len(pairs)
    assert blocked or (npair == 1 and tail is not None and nk >= 2)
    nin = 2 * npair + (0 if blocked else 1)

    def body(*refs):
        ins = refs[:nin]
        x_ref, gn_ref, dxo_ref, dx_ref, dxb_ref, dgn_ref, acc_ref = refs[nin:]
        i, k = pl.program_id(0), pl.program_id(1)

        @pl.when(k == 0)
        def _():
            acc_ref[...] = jnp.zeros_like(acc_ref)

        if blocked:
            _acc_dots(acc_ref, [(ins[2 * p][0], lambda cols, r=ins[2 * p + 1]: r[0, :, cols]) for p in range(npair)])
        else:
            @pl.when(k < nk - 1)
            def _():
                _acc_dots(acc_ref, [(ins[0][...], lambda cols: ins[1][:, cols])])

            @pl.when(k == nk - 1)
            def _():
                _acc_dots(acc_ref, [(ins[0][...], lambda cols: ins[2][:, cols])])

        @pl.when((k == nk - 1) & (i == 0))
        def _():
            dgn_ref[...] = jnp.zeros_like(dgn_ref)

        @pl.when(k == nk - 1)
        def _():
            def chunk(t, carry):
                rows = pl.ds(pl.multiple_of(t * EPI_ROWS, EPI_ROWS), EPI_ROWS)
                xv = x_ref[rows, :]
                r = lax.rsqrt(jnp.mean(xv * xv, axis=-1, keepdims=True) + NORM_EPS)
                xh = xv * r
                dh = acc_ref[rows, :]
                dgn_ref[...] += jnp.sum(dh * xh, axis=0, keepdims=True)
                dxh = dh * gn_ref[...]
                dx = dxo_ref[rows, :] + r * (dxh - xh * jnp.mean(dxh * xh, axis=-1, keepdims=True))
                dx_ref[rows, :] = dx
                dxb_ref[rows, :] = (out_scale * dx).astype(BF)
                return carry

            lax.fori_loop(0, tm // EPI_ROWS, chunk, 0)

    if blocked:
        mats = [pl.BlockSpec((1, tm, tk), lambda i, k: (k, i, 0)), pl.BlockSpec((1, tk, D), lambda i, k: (k, 0, 0))] * npair
        flat = [t for pr in pairs for t in pr]
    else:
        mats = [pl.BlockSpec((tm, tk), lambda i, k: (i, k)), pl.BlockSpec((tk, D), lambda i, k: (jnp.minimum(k, nk - 2), 0)),
                pl.BlockSpec((tk, D), lambda i, k: (0, 0))]
        flat = [*pairs[0], tail]
    row = pl.BlockSpec((tm, D), lambda i, k: (i, 0))
    once = row
    vec = pl.BlockSpec((1, D), lambda i, k: (0, 0))
    return _call(body, grid=(S // tm, nk), name=name, args=[*flat, x, gn, dxo], comm=comm,
                 in_specs=mats + [once, vec, once], out_specs=[row, row, vec],
                 out_shape=[jax.ShapeDtypeStruct((S, D), F32), jax.ShapeDtypeStruct((S, D), BF), jax.ShapeDtypeStruct((1, D), F32)],
                 scratch=[pltpu.VMEM((tm, D), F32)])


def _proj_split(a, bT, tm, tn, split, name, comm=()):
    M, K = a.shape
    N = bT.shape[0]
    n_first = split // tn

    def body(a_ref, b_ref, first_ref, rest_ref):
        n = pl.program_id(1)
        y = _nt(a_ref[...], b_ref[...])

        @pl.when(n < n_first)
        def _():
            first_ref[...] = y

        @pl.when(n >= n_first)
        def _():
            rest_ref[...] = y.astype(BF)

    return _call(body, grid=(M // tm, N // tn), name=name, args=[a, bT], comm=comm,
                 in_specs=[pl.BlockSpec((tm, K), lambda i, n: (i, 0)), pl.BlockSpec((tn, K), lambda i, n: (n, 0))],
                 out_specs=[pl.BlockSpec((tm, tn), lambda i, n: (i, jnp.minimum(n, n_first - 1))),
                            pl.BlockSpec((tm, tn), lambda i, n: (i, jnp.maximum(n - n_first, 0)))],
                 out_shape=[jax.ShapeDtypeStruct((M, split), F32), jax.ShapeDtypeStruct((M, N - split), BF)])


def _mm_tn(a, b, tm, tn, ts, blocked, name, comm=(), mrows=None):
    S, M = a.shape[0], (a.shape[1] if mrows is None else mrows)
    N = b.shape[1]
    ns = S // ts
    per_tile = tn // blocked if blocked else 0

    def body(a_ref, b_ref, o_ref, acc_ref):
        s = pl.program_id(2)

        @pl.when(s == 0)
        def _():
            acc_ref[...] = jnp.zeros_like(acc_ref)

        acc_ref[...] += _tn(a_ref[...], b_ref[...])

        @pl.when(s == ns - 1)
        def _():
            if blocked:
                for t in range(per_tile):
                    o_ref[t] = acc_ref[:, t * blocked:(t + 1) * blocked].astype(BF)
            else:
                o_ref[...] = acc_ref[...].astype(BF)

    if blocked:
        ospec = pl.BlockSpec((per_tile, tm, blocked), lambda i, n, s: (n, i, 0))
        oshape = jax.ShapeDtypeStruct((N // blocked, M, blocked), BF)
    else:
        ospec = pl.BlockSpec((tm, tn), lambda i, n, s: (i, n))
        oshape = jax.ShapeDtypeStruct((M, N), BF)
    return _call(body, grid=(M // tm, N // tn, ns), name=name, args=[a, b], comm=comm,
                 in_specs=[pl.BlockSpec((ts, tm), lambda i, n, s: (s, i)), pl.BlockSpec((ts, tn), lambda i, n, s: (s, n))],
                 out_specs=[ospec], out_shape=[oshape], scratch=[pltpu.VMEM((tm, tn), F32)])


def _rope_tables(S):
    half = ROPE_DIM // 2
    inv_freq = ROPE_THETA ** (-jnp.arange(0, ROPE_DIM, 2, dtype=F32) / ROPE_DIM)
    ang = jnp.arange(S, dtype=F32)[:, None] * inv_freq[None, :]
    cos, sin = jnp.cos(ang), jnp.sin(ang)
    zeros = jnp.zeros((S, HEAD_DIM - ROPE_DIM), F32)
    c = jnp.concatenate([cos, cos, jnp.ones((S, HEAD_DIM - ROPE_DIM), F32)], axis=1)
    sm = jnp.concatenate([-sin, jnp.zeros((S, half), F32), zeros], axis=1)
    sp = jnp.concatenate([jnp.zeros((S, half), F32), sin, zeros], axis=1)
    return c, sm, sp


def _rope(t, c, sm, sp):
    return t * c + pltpu.roll(t, HEAD_DIM - ROPE_DIM // 2, 1) * sm + pltpu.roll(t, ROPE_DIM // 2, 1) * sp


def _rope_t(dy, c, sm, sp):
    return dy * c + pltpu.roll(dy * sm, ROPE_DIM // 2, 1) + pltpu.roll(dy * sp, HEAD_DIM - ROPE_DIM // 2, 1)


def _att_mask(i):
    qi = lax.broadcasted_iota(jnp.int32, (BLK, 2 * BLK), 0)
    kj = lax.broadcasted_iota(jnp.int32, (BLK, 2 * BLK), 1)
    diff = qi + BLK - kj
    first_key = jnp.where(i > 0, 0, BLK)
    return (diff >= 0) & (diff <= BLK) & (kj >= first_key)


def _res_rows(r, i, n, d):
    if d == 1:
        return pl.ds(pl.multiple_of(i * n, n), n)
    return pl.ds(r + i * (n * d), n, stride=d)


def _att_specs(S, gi):
    def sect(off):
        base = (off + gi * GROUP_W) // HEAD_DIM
        return _once((S, HEAD_DIM), lambda hh: (0, base + hh))

    tab = pl.BlockSpec((S, HEAD_DIM), lambda hh: (0, 0))
    head = pl.BlockSpec((S, HEAD_DIM), lambda hh: (0, hh))
    return sect, tab, head


def _each_residue(d, fn):
    if d == 1:
        fn(0)
    else:
        lax.fori_loop(0, d, lambda r, carry: (fn(r), carry)[1], 0)


def _att_fwd(qkv, tabs, gi, d, name, comm=()):
    S = qkv.shape[0]
    L = S // d
    sect, tab, head = _att_specs(S, gi)
    nblk = L // BLK
    scale = HEAD_DIM ** -0.5

    def body(q_ref, k_ref, v_ref, c_ref, sm_ref, sp_ref, o_ref, lse_ref, qr, kp, vp):
        kp[pl.ds(0, BLK), :] = jnp.zeros((BLK, HEAD_DIM), BF)
        vp[pl.ds(0, BLK), :] = jnp.zeros((BLK, HEAD_DIM), BF)

        def residue(r):
            res = _res_rows(r, 0, L, d)
            c, sm, sp = c_ref[res, :], sm_ref[res, :], sp_ref[res, :]
            qr[...] = _rope(q_ref[res, :], c, sm, sp).astype(BF)
            kp[pl.ds(BLK, L), :] = _rope(k_ref[res, :], c, sm, sp).astype(BF)
            vp[pl.ds(BLK, L), :] = v_ref[res, :].astype(BF)

            def blk(i, carry):
                r0 = pl.multiple_of(i * BLK, BLK)
                s = _nt(qr[pl.ds(r0, BLK), :], kp[pl.ds(r0, 2 * BLK), :]) * scale
                s = jnp.where(_att_mask(i), s, NEG)
                m = jnp.max(s, axis=-1, keepdims=True)
                p = jnp.exp(s - m)
                l = jnp.sum(p, axis=-1, keepdims=True)
                out = _res_rows(r, i, BLK, d)
                o_ref[out, :] = _nn(p.astype(BF), vp[pl.ds(r0, 2 * BLK), :]) / l
                lse_ref[out, :] = jnp.broadcast_to(m + jnp.log(l), (BLK, HEAD_DIM))
                return carry

            lax.fori_loop(0, nblk, blk, 0, unroll=min(4, nblk))

        _each_residue(d, residue)

    shp = jax.ShapeDtypeStruct((S, GROUP_W), F32)
    return _call(body, grid=(HEADS_PER_GROUP,), name=name, args=[qkv, qkv, qkv, *tabs], comm=comm,
                 in_specs=[sect(Q_OFF), sect(K_OFF), sect(V_OFF), tab, tab, tab], out_specs=[head, head], out_shape=[shp, shp],
                 scratch=[pltpu.VMEM((L, HEAD_DIM), BF), pltpu.VMEM((L + BLK, HEAD_DIM), BF), pltpu.VMEM((L + BLK, HEAD_DIM), BF)])


def _att_combine(os, lses, name):
    S = os[0].shape[0]
    tm = _tile(S, 512)

    def body(o0, o1, o2, l0, l1, l2, oa_ref, lse_ref):
        a, b, c = l0[...], l1[...], l2[...]
        mx = jnp.maximum(jnp.maximum(a, b), c)
        wa, wb, wc = jnp.exp(a - mx), jnp.exp(b - mx), jnp.exp(c - mx)
        den = wa + wb + wc
        oa_ref[...] = ((wa * o0[...] + wb * o1[...] + wc * o2[...]) / den).astype(BF)
        lse_ref[...] = mx + jnp.log(den)

    row = pl.BlockSpec((tm, GROUP_W), lambda i: (i, 0))
    return _call(body, grid=(S // tm,), name=name, args=[*os, *lses], in_specs=[row] * 6, out_specs=[row, row],
                 out_shape=[jax.ShapeDtypeStruct((S, GROUP_W), BF), jax.ShapeDtypeStruct((S, GROUP_W), F32)])[0]


def _att_bwd(qkv, tabs, do, lse, dvec, gi, d, name, comm=()):
    S = qkv.shape[0]
    L = S // d
    sect, tab, head = _att_specs(S, gi)
    stat = _once((S, HEAD_DIM), lambda hh: (0, hh))
    nblk = L // BLK
    scale = HEAD_DIM ** -0.5

    def body(q_ref, k_ref, v_ref, c_ref, sm_ref, sp_ref, do_ref, lse_ref, dv_ref, dq_out, dk_out, dv_out, qr, kp, vp, dkp, dvp):
        kp[pl.ds(0, BLK), :] = jnp.zeros((BLK, HEAD_DIM), BF)
        vp[pl.ds(0, BLK), :] = jnp.zeros((BLK, HEAD_DIM), BF)

        def residue(r):
            res = _res_rows(r, 0, L, d)
            c, sm, sp = c_ref[res, :], sm_ref[res, :], sp_ref[res, :]
            qr[...] = _rope(q_ref[res, :], c, sm, sp).astype(BF)
            kp[pl.ds(BLK, L), :] = _rope(k_ref[res, :], c, sm, sp).astype(BF)
            vp[pl.ds(BLK, L), :] = v_ref[res, :].astype(BF)
            dkp[...] = jnp.zeros_like(dkp)
            dvp[...] = jnp.zeros_like(dvp)

            def blk(i, carry):
                r0 = pl.multiple_of(i * BLK, BLK)
                rows, win, pos = pl.ds(r0, BLK), pl.ds(r0, 2 * BLK), _res_rows(r, i, BLK, d)
                q, kw, vw, dob = qr[rows, :], kp[win, :], vp[win, :], do_ref[pos, :].astype(BF)
                s = jnp.where(_att_mask(i), _nt(q, kw) * scale, NEG)
                p = jnp.exp(s - lse_ref[pos, :][:, :1])
                ds = p * (_nt(dob, vw) - dv_ref[pos, :][:, :1]) * scale
                dsb = ds.astype(BF)
                dq_out[pos, :] = _rope_t(_nn(dsb, kw), c_ref[pos, :], sm_ref[pos, :], sp_ref[pos, :])
                dkp[win, :] += _tn(dsb, q)
                dvp[win, :] += _tn(p.astype(BF), dob)
                return carry

            lax.fori_loop(0, nblk, blk, 0, unroll=2)
            dk_out[res, :] = _rope_t(dkp[pl.ds(BLK, L), :], c, sm, sp)
            dv_out[res, :] = dvp[pl.ds(BLK, L), :]

        _each_residue(d, residue)

    shp = jax.ShapeDtypeStruct((S, GROUP_W), F32)
    return _call(body, grid=(HEADS_PER_GROUP,), name=name, args=[qkv, qkv, qkv, *tabs, do, lse, dvec], comm=comm,
                 in_specs=[sect(Q_OFF), sect(K_OFF), sect(V_OFF), tab, tab, tab, stat, stat, stat],
                 out_specs=[head, head, head], out_shape=[shp, shp, shp],
                 scratch=[pltpu.VMEM((L, HEAD_DIM), BF), pltpu.VMEM((L + BLK, HEAD_DIM), BF), pltpu.VMEM((L + BLK, HEAD_DIM), BF),
                          pltpu.VMEM((L + BLK, HEAD_DIM), F32), pltpu.VMEM((L + BLK, HEAD_DIM), F32)])


def _sg_parts(u_ref, vs_ref, g_ref, b_ref):
    uv = u_ref[...].astype(F32)
    vv = vs_ref[...].astype(F32)
    cv = _gauss_cdf(vv)
    vg = vv * cv
    mu = jnp.mean(vg, axis=-1, keepdims=True)
    vc = vg - mu
    rs = lax.rsqrt(jnp.mean(vc * vc, axis=-1, keepdims=True) + LN_EPS)
    y = vc * rs
    return uv, vv, cv, rs, y, y * g_ref[...] + b_ref[...]


def _sg_wmask():
    t = lax.broadcasted_iota(jnp.int32, (BLK, BLK), 0)
    s = lax.broadcasted_iota(jnp.int32, (BLK, BLK), 1)
    return s <= t


def _sg_fwd(proj, sgw, sgbT, lng, lnb, name):
    S, P = proj.shape

    def body(u_ref, vs_ref, w_ref, bt_ref, g_ref, b_ref, z_ref):
        uv, _, _, _, _, vln = _sg_parts(u_ref, vs_ref, g_ref, b_ref)
        ug = uv * _gauss_cdf(uv)
        vb = vln.astype(BF)
        mask = _sg_wmask()
        bt = bt_ref[...]
        for g in range(SG_GROUPS):
            cols = slice(g * BLK, (g + 1) * BLK)
            w = jnp.where(mask, w_ref[g], 0.0).astype(BF)
            sp = _nn(w, vb[:, cols]) + bt[:, g:g + 1]
            z_ref[:, cols] = (ug[:, cols] * sp).astype(BF)

    tile = lambda off: pl.BlockSpec((BLK, SG_W), lambda i: (i, off // SG_W))
    full = lambda shape: pl.BlockSpec(shape, lambda i: (0,) * len(shape))
    return _call(body, grid=(S // BLK,), name=name, args=[proj, proj, sgw, sgbT, lng, lnb],
                 in_specs=[tile(R_U), tile(R_VS), full((SG_GROUPS, BLK, BLK)), full((BLK, BLK)), full((1, SG_W)), full((1, SG_W))],
                 out_specs=[pl.BlockSpec((BLK, SG_W), lambda i: (i, 0))], out_shape=[jax.ShapeDtypeStruct((S, SG_W), BF)])[0][0]


def _sg_bwd(proj, dz, sgw, sgbT, lng, lnb, name):
    S, P = proj.shape

    def body(u_ref, vs_ref, dz_ref, w_ref, bt_ref, g_ref, b_ref, du_ref, dvs_ref, dw_ref, dbt_ref, dg_ref, db_ref, dvln):
        @pl.when(pl.program_id(0) == 0)
        def _():
            dw_ref[...] = jnp.zeros_like(dw_ref)
            dbt_ref[...] = jnp.zeros_like(dbt_ref)
            dg_ref[...] = jnp.zeros_like(dg_ref)
            db_ref[...] = jnp.zeros_like(db_ref)

        uv, vv, cv, rs, y, vln = _sg_parts(u_ref, vs_ref, g_ref, b_ref)
        cu = _gauss_cdf(uv)
        ug = uv * cu
        dug = _gelu_grad(uv, cu)
        vb = vln.astype(BF)
        dzv = dz_ref[...].astype(F32)
        dsp = dzv * ug
        dspb = dsp.astype(BF)
        mask = _sg_wmask()
        bt = bt_ref[...]
        lane = lax.broadcasted_iota(jnp.int32, (BLK, BLK), 1)
        dbt = jnp.zeros((BLK, BLK), F32)
        for g in range(SG_GROUPS):
            cols = slice(g * BLK, (g + 1) * BLK)
            w = jnp.where(mask, w_ref[g], 0.0).astype(BF)
            sp = _nn(w, vb[:, cols]) + bt[:, g:g + 1]
            du_ref[:, cols] = (dzv[:, cols] * sp * dug[:, cols]).astype(BF)
            dw_ref[g] += jnp.where(mask, _nt(dspb[:, cols], vb[:, cols]), 0.0)
            dbt = dbt + jnp.where(lane == g, jnp.sum(dsp[:, cols], axis=-1, keepdims=True), 0.0)
            dvln[:, cols] = _tn(w, dspb[:, cols])
        dbt_ref[...] += dbt
        dvl = dvln[...]
        dg_ref[...] += jnp.sum(dvl * y, axis=0, keepdims=True)
        db_ref[...] += jnp.sum(dvl, axis=0, keepdims=True)
        dy = dvl * g_ref[...]
        dvg = rs * (dy - jnp.mean(dy, axis=-1, keepdims=True) - y * jnp.mean(dy * y, axis=-1, keepdims=True))
        dvs_ref[...] = (dvg * _gelu_grad(vv, cv)).astype(BF)

    tile = lambda off: pl.BlockSpec((BLK, SG_W), lambda i: (i, off // SG_W))
    full = lambda shape: pl.BlockSpec(shape, lambda i: (0,) * len(shape))
    row = pl.BlockSpec((BLK, SG_W), lambda i: (i, 0))
    return _call(body, grid=(S // BLK,), name=name, args=[proj, proj, dz, sgw, sgbT, lng, lnb],
                 in_specs=[tile(R_U), tile(R_VS), row, full((SG_GROUPS, BLK, BLK)), full((BLK, BLK)), full((1, SG_W)), full((1, SG_W))],
                 out_specs=[row, row, full((SG_GROUPS, BLK, BLK)), full((BLK, BLK)), full((1, SG_W)), full((1, SG_W))],
                 out_shape=[jax.ShapeDtypeStruct((S, SG_W), BF), jax.ShapeDtypeStruct((S, SG_W), BF),
                            jax.ShapeDtypeStruct((SG_GROUPS, BLK, BLK), F32), jax.ShapeDtypeStruct((BLK, BLK), F32),
                            jax.ShapeDtypeStruct((1, SG_W), F32), jax.ShapeDtypeStruct((1, SG_W), F32)],
                 scratch=[pltpu.VMEM((BLK, SG_W), F32)])[0]


def _gate_merge(oatt, z, watt, wsg, proj, name, comm=()):
    S = oatt.shape[0]
    nb, _, Db = watt.shape
    D = nb * Db
    tm = _tile(S, 512)
    half = D // 2
    ga, gs = R_GA // half, (R_GA + D) // half

    def body(oa_ref, z_ref, wa_ref, ws_ref, ga0, ga1, gs0, gs1, ya_ref, ys_ref, mg_ref):
        oa, zv = oa_ref[...], z_ref[...]
        for j in range(nb):
            cols = slice(j * Db, (j + 1) * Db)
            g_a, g_s = (ga0, gs0) if j < nb // 2 else (ga1, gs1)
            gcols = slice((j % (nb // 2)) * Db, (j % (nb // 2) + 1) * Db)
            ya = _nn(oa, wa_ref[j])
            ys = _nn(zv, ws_ref[j])
            ya_ref[:, cols] = ya.astype(BF)
            ys_ref[:, cols] = ys.astype(BF)
            mg_ref[:, cols] = (jax.nn.sigmoid(g_a[:, gcols].astype(F32)) * ya + jax.nn.sigmoid(g_s[:, gcols].astype(F32)) * ys).astype(BF)

    out = pl.BlockSpec((tm, D), lambda i: (i, 0))
    gate = lambda b: pl.BlockSpec((tm, half), lambda i: (i, b))
    shp = jax.ShapeDtypeStruct((S, D), BF)
    return _call(body, grid=(S // tm,), name=name, args=[oatt, z, watt, wsg, proj, proj, proj, proj], comm=comm,
                 in_specs=[pl.BlockSpec((tm, GROUP_W), lambda i: (i, 0)), pl.BlockSpec((tm, SG_W), lambda i: (i, 0)),
                           pl.BlockSpec((nb, GROUP_W, Db), lambda i: (0, 0, 0)), pl.BlockSpec((nb, SG_W, Db), lambda i: (0, 0, 0)),
                           gate(ga), gate(ga + 1), gate(gs), gate(gs + 1)],
                 out_specs=[out, out, out], out_shape=[shp, shp, shp])


def _mix_out(merged, wout, x, gn, name):
    S, D = x.shape
    tm = _tile(S, 256)

    def body(m_ref, w_ref, x_ref, gn_ref, xo_ref, hn_ref):
        xo = x_ref[...] + _nn(m_ref[...], w_ref[...])
        r = lax.rsqrt(jnp.mean(xo * xo, axis=-1, keepdims=True) + NORM_EPS)
        xo_ref[...] = xo
        hn_ref[...] = (xo * r * gn_ref[...]).astype(BF)

    row = pl.BlockSpec((tm, D), lambda i: (i, 0))
    return _call(body, grid=(S // tm,), name=name, args=[merged, wout, x, gn],
                 in_specs=[row, pl.BlockSpec((D, D), lambda i: (0, 0)), row, pl.BlockSpec((1, D), lambda i: (0, 0))],
                 out_specs=[row, row], out_shape=[jax.ShapeDtypeStruct((S, D), F32), jax.ShapeDtypeStruct((S, D), BF)])[0]


def _mix_bwd_gate(dmix, wout, ya, ys, proj, name):
    S, D = dmix.shape
    tm, tn = _tile(S, 256), min(512, D // 2)
    half = D // 2
    ga, gs = R_GA // half, (R_GA + D) // half

    def body(dm_ref, w_ref, ya_ref, ys_ref, ga0, ga1, gs0, gs1, dya_ref, dys_ref, dga_ref, dgs_ref):
        dmv = dm_ref[...]
        for c0 in range(0, D, tn):
            cols = slice(c0, c0 + tn)
            g_a, g_s = (ga0, gs0) if c0 < half else (ga1, gs1)
            gcols = slice(c0 % half, c0 % half + tn)
            dm = _nt(dmv, w_ref[cols, :])
            sa = jax.nn.sigmoid(g_a[:, gcols].astype(F32))
            ss = jax.nn.sigmoid(g_s[:, gcols].astype(F32))
            dya_ref[:, cols] = (dm * sa).astype(BF)
            dys_ref[:, cols] = (dm * ss).astype(BF)
            dga_ref[:, cols] = (dm * ya_ref[:, cols].astype(F32) * sa * (1.0 - sa)).astype(BF)
            dgs_ref[:, cols] = (dm * ys_ref[:, cols].astype(F32) * ss * (1.0 - ss)).astype(BF)

    row = pl.BlockSpec((tm, D), lambda i: (i, 0))
    gate = lambda b: pl.BlockSpec((tm, half), lambda i: (i, b))
    shp = jax.ShapeDtypeStruct((S, D), BF)
    return _call(body, grid=(S // tm,), name=name, args=[dmix, wout, ya, ys, proj, proj, proj, proj],
                 in_specs=[row, pl.BlockSpec((D, D), lambda i: (0, 0)), row, row, gate(ga), gate(ga + 1), gate(gs), gate(gs + 1)],
                 out_specs=[row] * 4, out_shape=[shp] * 4)[0]


def _att_sg_dout(dya, dys, watt, wsg, oatt, name, comm=()):
    S, D = dya.shape
    nb, _, Db = watt.shape
    tm = _tile(S, 512)

    def body(dya_ref, dys_ref, wa_ref, ws_ref, oa_ref, do_ref, dz_ref, dvec_ref):
        def back(dy_ref, w_ref, rows):
            tot = None
            for j in range(nb):
                part = _nt(dy_ref[:, j * Db:(j + 1) * Db], w_ref[j, rows, :])
                tot = part if tot is None else tot + part
            return tot

        dov = back(dya_ref, wa_ref, slice(0, GROUP_W))
        do_ref[...] = dov
        for c0 in range(0, SG_W, GROUP_W):
            dz_ref[:, c0:c0 + GROUP_W] = back(dys_ref, ws_ref, slice(c0, c0 + GROUP_W)).astype(BF)
        prod = dov * oa_ref[...].astype(F32)
        for hh in range(HEADS_PER_GROUP):
            cols = slice(hh * HEAD_DIM, (hh + 1) * HEAD_DIM)
            dvec_ref[:, cols] = jnp.broadcast_to(jnp.sum(prod[:, cols], axis=-1, keepdims=True), (tm, HEAD_DIM))

    row = pl.BlockSpec((tm, D), lambda i: (i, 0))
    att = pl.BlockSpec((tm, GROUP_W), lambda i: (i, 0))
    return _call(body, grid=(S // tm,), name=name, args=[dya, dys, watt, wsg, oatt], comm=comm,
                 in_specs=[row, row, pl.BlockSpec((nb, GROUP_W, Db), lambda i: (0, 0, 0)), pl.BlockSpec((nb, SG_W, Db), lambda i: (0, 0, 0)), att],
                 out_specs=[att, pl.BlockSpec((tm, SG_W), lambda i: (i, 0)), att],
                 out_shape=[jax.ShapeDtypeStruct((S, GROUP_W), F32), jax.ShapeDtypeStruct((S, SG_W), BF), jax.ShapeDtypeStruct((S, GROUP_W), F32)])[0]


def _small_allreduce(pack, name):
    R = pack.shape[0]

    def body(p_ref, o_ref, gath, send, recv):
        x, y, c = _place()
        me = 4 * x + 2 * y + c
        gath[me] = p_ref[...]
        copies = []
        for r in range(1, N_DEV):
            px, py, pc = _flip(x, r & 4), _flip(y, r & 2), _flip(c, r & 1)
            peer = 4 * px + 2 * py + pc
            mk = lambda dst: pltpu.make_async_remote_copy(src_ref=p_ref, dst_ref=dst, send_sem=send.at[r - 1], recv_sem=recv.at[r - 1],
                                                          device_id=(px, py, pc), device_id_type=MESH)
            snd = mk(gath.at[me])
            snd.start()
            copies.append((snd, mk(gath.at[peer])))
        for snd, rcv in copies:
            rcv.wait_recv()
            snd.wait_send()
        acc = gath[0]
        for s in range(1, N_DEV):
            acc = acc + gath[s]
        o_ref[...] = acc

    vm = pl.BlockSpec(memory_space=pltpu.VMEM)
    return pl.pallas_call(
        body, name=name, in_specs=[vm], out_specs=vm, out_shape=jax.ShapeDtypeStruct(pack.shape, F32),
        scratch_shapes=[pltpu.VMEM((N_DEV, R, 128), F32), pltpu.SemaphoreType.DMA((7,)), pltpu.SemaphoreType.DMA((7,))],
        compiler_params=pltpu.CompilerParams(vmem_limit_bytes=VMEM_LIMIT),
    )(pack)


def _row_tile(R, C, elems=262144):
    fits = [t for t in range(16, R + 1, 16) if R % t == 0 and t * C <= elems]
    return max(fits) if fits else R


def _pair_add(parts, other, name):
    _, R, C = parts.shape
    tr = _row_tile(R, C, 1048576)

    def body(c_ref, p_ref, o_ref, s_ref):
        s_ref[0] = (p_ref[0].astype(F32) + o_ref[0].astype(F32)).astype(BF)

    core = lax.axis_index("c").astype(jnp.int32).reshape(1)
    return pl.pallas_call(
        body, name=name,
        grid_spec=pltpu.PrefetchScalarGridSpec(
            num_scalar_prefetch=1, grid=(N_CHIP, R // tr),
            in_specs=[pl.BlockSpec((1, tr, C), lambda q, i, c: (2 * q + c[0], i, 0)), pl.BlockSpec((1, tr, C), lambda q, i, c: (q, i, 0))],
            out_specs=pl.BlockSpec((1, tr, C), lambda q, i, c: (q, i, 0))),
        out_shape=jax.ShapeDtypeStruct((N_CHIP, R, C), BF),
        compiler_params=pltpu.CompilerParams(dimension_semantics=("arbitrary", "arbitrary"), vmem_limit_bytes=VMEM_LIMIT),
    )(core, parts, other)


def _adamw(parts, w, m, v, name):
    ns, R, C = parts.shape
    tr = _row_tile(R, C, 524288)
    c1 = 1.0 - ADAM_B1 ** ADAM_STEP
    c2 = 1.0 - ADAM_B2 ** ADAM_STEP

    def body(p_ref, w_ref, m_ref, v_ref, g_ref, d_ref, nm_ref, nv_ref):
        g = p_ref[0].astype(F32)
        for s in range(1, ns):
            g = g + p_ref[s].astype(F32)
        mn = ADAM_B1 * m_ref[...] + (1.0 - ADAM_B1) * g
        vn = ADAM_B2 * v_ref[...] + (1.0 - ADAM_B2) * (g * g)
        g_ref[...] = g
        nm_ref[...] = mn
        nv_ref[...] = vn
        d_ref[...] = -ADAM_LR * ((mn / c1) / (jnp.sqrt(vn / c2) + ADAM_EPS) + ADAM_WD * w_ref[...])

    row = pl.BlockSpec((tr, C), lambda i: (i, 0))
    shp = jax.ShapeDtypeStruct((R, C), F32)
    return _call(body, grid=(R // tr,), name=name, args=[parts, w, m, v],
                 in_specs=[pl.BlockSpec((ns, tr, C), lambda i: (0, i, 0)), row, row, row], out_specs=[row] * 4, out_shape=[shp] * 4)[0]


def _pad_rows(a, rows):
    return jnp.pad(a, ((0, rows - a.shape[0]), (0, 0)))


def kernel(x, ffn1_norm, ffn1_w_gate, ffn1_w_up, ffn1_w_down, mix_norm, w_in, sg_ln_g, sg_ln_b, sg_w, sg_b, w_att_out, w_sg_out, w_out, ffn2_norm, ffn2_w_gate, ffn2_w_up, ffn2_w_down, final_norm, loss_target, m_ffn1_norm, m_ffn1_w_gate, m_ffn1_w_up, m_ffn1_w_down, m_mix_norm, m_w_in, m_sg_ln_g, m_sg_ln_b, m_sg_w, m_sg_b, m_w_att_out, m_w_sg_out, m_w_out, m_ffn2_norm, m_ffn2_w_gate, m_ffn2_w_up, m_ffn2_w_down, m_final_norm, v_ffn1_norm, v_ffn1_w_gate, v_ffn1_w_up, v_ffn1_w_down, v_mix_norm, v_w_in, v_sg_ln_g, v_sg_ln_b, v_sg_w, v_sg_b, v_w_att_out, v_w_sg_out, v_w_out, v_ffn2_norm, v_ffn2_w_gate, v_ffn2_w_up, v_ffn2_w_down, v_final_norm):
    S, D = x.shape[1], x.shape[2]
    Pb = w_in.shape[2]
    P = N_DEV * Pb
    assert P == GA_OFF + 2 * D and D % (N_DEV * 128) == 0 and S % (BLK * DILATIONS[-1]) == 0
    xs, tgt = x[0], loss_target[0]

    sharded = dict(ffn1_w_gate=ffn1_w_gate, ffn1_w_up=ffn1_w_up, ffn1_w_down=ffn1_w_down, w_in=w_in, w_att_out=w_att_out,
                   w_sg_out=w_sg_out, w_out=w_out, ffn2_w_gate=ffn2_w_gate, ffn2_w_up=ffn2_w_up, ffn2_w_down=ffn2_w_down)
    cols = ("ffn1_w_gate", "ffn1_w_up", "w_in", "ffn2_w_gate", "ffn2_w_up")
    local = lambda n, a: a[0].T if n in cols else a[0]
    back = lambda n, a: a.T[None] if n in cols else a[None]
    wloc = {n: local(n, w) for n, w in sharded.items()}
    sb = {n: w.astype(BF) for n, w in wloc.items()}

    (h1,), ((wg1,),) = _rms_fwd(xs, ffn1_norm, "rms1", comm=[_Gather([sb["ffn1_w_gate"]], 1.0, 1.0)])
    (g1,), ((wu1,),) = _ffn_gate(h1, wg1, "ffn1_gate", comm=[_Gather([sb["ffn1_w_up"]], 0.9, 0.55)])
    (u1, a1), ((wd1,),) = _ffn_up_act(h1, wu1, g1, "ffn1_up_act", comm=[_Gather([sb["ffn1_w_down"]], 0.9, 0.55)])
    (x1, h2), ((winT8,),) = _ffn_down_norm(a1, wd1, xs, mix_norm, "ffn1_down", comm=[_Gather([sb["w_in"]], 1.0, 0.7)])
    winT = winT8.reshape(P, D)
    (qkv, rest), ((wg2, wu2),) = _proj_split(h2, winT, _tile(S, 1024), 512, U_OFF, "proj",
                                           comm=[_Gather([sb["ffn2_w_gate"], sb["ffn2_w_up"]], 0.85, 0.5)])
    tabs = _rope_tables(S)
    rides = [[_Gather([sb["w_att_out"], sb["w_sg_out"]], 0.9, 0.5)], [_Gather([sb["w_out"]], 0.85, 0.45)], []]
    os, lses, late = [], [], []
    for gi, d in enumerate(DILATIONS):
        (o, l), got_here = _att_fwd(qkv, tabs, gi, d, f"att_fwd{gi}", comm=rides[gi])
        late += [w for g in got_here for w in g]
        os.append(o)
        lses.append(l)
    watt, wsg, wout8 = late
    wout = wout8.reshape(D, D)
    oatt, lse = _att_combine(os, lses, "att_combine")
    sgw = sg_w[0]
    sgbT = jnp.pad(sg_b[0].T, ((0, 0), (0, BLK - SG_GROUPS)))
    z = _sg_fwd(rest, sgw, sgbT, sg_ln_g, sg_ln_b, "sg_fwd")
    (ya, ys, merged), _ = _gate_merge(oatt, z, watt, wsg, rest, "gate_merge")
    x2, h3 = _mix_out(merged, wout, x1, ffn2_norm, "mix_out")
    (g3, u3, a3), ((wd2,),) = _ffn_up(h3, wg2, wu2, "ffn2_up", comm=[_Gather([sb["ffn2_w_down"]], 0.6, 0.35)])
    dx3, dyb3, d_final, loss_part = _ffn_down_loss(a3, wd2, x2, final_norm.reshape(1, D), tgt, "ffn2_down_loss")

    Fb = wg2.shape[1]
    Db = watt.shape[2]
    p_pad = -(-P // PROJ_TK) * PROJ_TK
    win_tail = _pad_rows(winT[p_pad - PROJ_TK:], PROJ_TK)
    (dg3, du3), _ = _ffn_bwd_act(dyb3, wd2, g3, u3, "ffn2_bwd_act")
    (dwd2,), _ = _ffn_dwd(a3, dyb3, "ffn2_dwd")
    (dwg2, dwu2), _ = _ffn_dwgu(h3, dg3, du3, "ffn2_dwgu")
    ffn2_parts = [dwd2, dwg2, dwu2]
    (dx2, dmixb, d_ffn2n), (ffn2_other,) = _dh_rms_bwd([(dg3, wg2), (du3, wu2)], True, Fb, x2, ffn2_norm, dx3, 1.0, "ffn2_dh",
                                                     comm=[_Swap(ffn2_parts)])
    ffn2_sums = [_pair_add(p, o, f"pair_ffn2_{i}") for i, (p, o) in enumerate(zip(ffn2_parts, ffn2_other))]

    dya, dys, dga, dgs = _mix_bwd_gate(dmixb, wout, ya, ys, rest, "mix_bwd_gate")
    (dwout,), _ = _mm_tn(merged, dmixb, _tile(D, 1024), _tile(D, 1024), _tile(S, 1024), False, "dw_out")
    do, dz, dvec = _att_sg_dout(dya, dys, watt, wsg, oatt, "att_sg_dout")
    (dwatt,), _ = _mm_tn(oatt, dya, GROUP_W, 2 * Db, _tile(S, 1024), Db, "dw_att")
    (dwsg,), _ = _mm_tn(z, dys, SG_W, 2 * Db, _tile(S, 1024), Db, "dw_sg")
    mix_parts = [dwout.reshape(N_DEV, D // N_DEV, D), dwatt, dwsg]
    du, dvs, d_sgw, d_sgbT, d_lng, d_lnb = _sg_bwd(rest, dz, sgw, sgbT, sg_ln_g, sg_ln_b, "sg_bwd")
    dqs, dks, dvs_att, ffn2_got = [], [], [], []
    for gi, d in enumerate(DILATIONS):
        ride = [_Ici([ffn2_sums[0]])] if gi == 2 else []
        (dq, dk, dv), got_here = _att_bwd(qkv, tabs, do, lse, dvec, gi, d, f"att_bwd{gi}", comm=ride)
        ffn2_got += [g[0] for g in got_here]
        dqs.append(dq)
        dks.append(dk)
        dvs_att.append(dv)
    dproj = jnp.concatenate([t.astype(BF) for t in dqs + dks + dvs_att] + [du, dvs, dga, dgs, jnp.zeros((S, p_pad - P), BF)], axis=1)
    (dx1, dyb1, d_mixn), (ffn2_rest, mix_other) = _dh_rms_bwd([(dproj, winT)], False, PROJ_TK, x1, mix_norm, dx2, 0.5, "proj_dh",
                                                            comm=[_Ici(ffn2_sums[1:]), _Swap(mix_parts)], tail=win_tail)
    ffn2_got += ffn2_rest
    mix_sums = [_pair_add(p, o, f"pair_mix_{i}") for i, (p, o) in enumerate(zip(mix_parts, mix_other))]
    (dwd1,), (mix_got,) = _ffn_dwd(a1, dyb1, "ffn1_dwd", comm=[_Ici(mix_sums)])
    rows = lambda a: a.reshape(-1, 128)
    pad8 = lambda a: _pad_rows(a, -(-a.shape[0] // 8) * 8)
    small = [("sg_w", rows(d_sgw), sg_w, m_sg_w, v_sg_w), ("mix_norm", rows(d_mixn), mix_norm, m_mix_norm, v_mix_norm),
             ("ffn2_norm", rows(d_ffn2n), ffn2_norm, m_ffn2_norm, v_ffn2_norm), ("final_norm", rows(d_final), final_norm, m_final_norm, v_final_norm),
             ("sg_ln_g", rows(d_lng), sg_ln_g, m_sg_ln_g, v_sg_ln_g), ("sg_ln_b", rows(d_lnb), sg_ln_b, m_sg_ln_b, v_sg_ln_b),
             ("sg_b", d_sgbT[:, :SG_GROUPS].T, sg_b, m_sg_b, v_sg_b)]
    gpack = jnp.concatenate([pad8(g) for _, g, _, _, _ in small] + [pad8(loss_part)], axis=0)
    (dwin,), ((wd1_other,), (gpacks,)) = _mm_tn(dproj, h2, 512, D, _tile(S, 2048), False, "dw_in", mrows=P,
                                              comm=[_Swap([dwd1]), _Spread([gpack])])
    dwin = dwin.reshape(N_DEV, Pb, D)
    wd1_sum = _pair_add(dwd1, wd1_other, "pair_wd1")
    (dg1, du1), ((wd1_got,), (win_other,)) = _ffn_bwd_act(dyb1, wd1, g1, u1, "ffn1_bwd_act", comm=[_Ici([wd1_sum]), _Swap([dwin])])
    win_sum = _pair_add(dwin, win_other, "pair_win")
    (dwg1, dwu1), ((win_got,),) = _ffn_dwgu(h1, dg1, du1, "ffn1_dwgu", comm=[_Ici([win_sum])])
    gu_parts = [dwg1, dwu1]
    gu_other = _comm_only(_Swap(gu_parts), "swap_gu1")
    gu_sums = [_pair_add(p, o, f"pair_gu1_{i}") for i, (p, o) in enumerate(zip(gu_parts, gu_other))]
    (dx0, _, d_ffn1n), (gu_got,) = _dh_rms_bwd([(dg1, wg1), (du1, wu1)], True, Fb, xs, ffn1_norm, dx1, 1.0, "ffn1_dh",
                                               comm=[_Ici(gu_sums)])

    got = dict(ffn2_w_down=ffn2_got[0], ffn2_w_gate=ffn2_got[1], ffn2_w_up=ffn2_got[2], w_out=mix_got[0], w_att_out=mix_got[1],
               w_sg_out=mix_got[2], w_in=win_got, ffn1_w_gate=gu_got[0], ffn1_w_up=gu_got[1], ffn1_w_down=wd1_got)
    moments = dict(ffn1_w_gate=(m_ffn1_w_gate, v_ffn1_w_gate), ffn1_w_up=(m_ffn1_w_up, v_ffn1_w_up),
                   ffn1_w_down=(m_ffn1_w_down, v_ffn1_w_down), w_in=(m_w_in, v_w_in), w_att_out=(m_w_att_out, v_w_att_out),
                   w_sg_out=(m_w_sg_out, v_w_sg_out), w_out=(m_w_out, v_w_out), ffn2_w_gate=(m_ffn2_w_gate, v_ffn2_w_gate),
                   ffn2_w_up=(m_ffn2_w_up, v_ffn2_w_up), ffn2_w_down=(m_ffn2_w_down, v_ffn2_w_down))
    res = {}
    for n in sharded:
        mm, vv = moments[n]
        outs = _adamw(got[n], wloc[n], local(n, mm), local(n, vv), "adamw_" + n)
        res[n] = [back(n, o) for o in outs]

    zero8 = jnp.zeros((8, 128), F32)
    wpack = jnp.concatenate([pad8(rows(w)) for _, _, w, _, _ in small] + [zero8], axis=0)
    mpack = jnp.concatenate([pad8(rows(m)) for _, _, _, m, _ in small] + [zero8], axis=0)
    vpack = jnp.concatenate([pad8(rows(v)) for _, _, _, _, v in small] + [zero8], axis=0)
    packs = _adamw(gpacks, wpack, mpack, vpack, "adamw_small")
    off = 0
    for n, g, w, _, _ in small:
        r = g.shape[0]
        res[n] = [p[off:off + r].reshape(w.shape) for p in packs]
        off += -(-r // 8) * 8
    loss = packs[0][off, 0]
    g_first = _small_allreduce(rows(d_ffn1n), "allreduce_ffn1_norm")
    res["ffn1_norm"] = [p.reshape(ffn1_norm.shape) for p in
                        _adamw(g_first[None], rows(ffn1_norm), rows(m_ffn1_norm), rows(v_ffn1_norm), "adamw_ffn1_norm")]

    order = ["ffn1_norm", "ffn1_w_gate", "ffn1_w_up", "ffn1_w_down", "mix_norm", "w_in", "sg_ln_g", "sg_ln_b", "sg_w", "sg_b",
             "w_att_out", "w_sg_out", "w_out", "ffn2_norm", "ffn2_w_gate", "ffn2_w_up", "ffn2_w_down", "final_norm"]
    return (loss, dx0[None], *[res[n][0] for n in order], *[res[n][1] for n in order], *[res[n][2] for n in order],
            *[res[n][3] for n in order])
```

```python
import math

import jax
import jax.numpy as jnp
from jax import lax
from jax.experimental import pallas as pl
from jax.experimental.pallas import tpu as pltpu

BF = jnp.bfloat16
F32 = jnp.float32
MESH = pl.DeviceIdType.MESH
N_DEV = 8
N_CHIP = 4

HEAD_DIM = 128
HEADS_PER_GROUP = 4
GROUP_W = HEADS_PER_GROUP * HEAD_DIM
DILATIONS = (1, 4, 16)
ATT_W = len(DILATIONS) * GROUP_W
SG_W = 1536
SG_GROUPS = 12
BLK = 128
ROPE_DIM = 32
ROPE_THETA = 500000.0
NORM_EPS = 1e-6
LN_EPS = 1e-5
Q_OFF, K_OFF, V_OFF, U_OFF, VS_OFF, GA_OFF = 0, ATT_W, 2 * ATT_W, 3 * ATT_W, 3 * ATT_W + SG_W, 3 * ATT_W + 2 * SG_W

ADAM_LR, ADAM_B1, ADAM_B2, ADAM_EPS, ADAM_WD, ADAM_STEP = 0.001, 0.9, 0.999, 1e-08, 0.01, 10

VMEM_LIMIT = 56 * 1024 * 1024
NEG = -1e30
ANY = pl.BlockSpec(memory_space=pl.ANY)
EPI_ROWS = 128
ACC_COLS = 512
FFN_PAIR = 2
FFN_ROWS = 1024
PROJ_TK = 1536
R_U, R_VS, R_GA = 0, SG_W, 2 * SG_W


def _once(shape, index_map):
    return pl.BlockSpec(shape, index_map, pipeline_mode=pl.Buffered(1))


def _tile(n, pref):
    t = min(n, pref)
    while n % t:
        t //= 2
    return t


def _nt(a, b):
    return lax.dot_general(a, b, (((1,), (1,)), ((), ())), preferred_element_type=F32)


def _tn(a, b):
    return lax.dot_general(a, b, (((0,), (0,)), ((), ())), preferred_element_type=F32)


def _nn(a, b):
    return jnp.dot(a, b, preferred_element_type=F32)


def _acc_dots(acc_ref, terms, transposed_rhs=False):
    n = acc_ref.shape[1]
    width = min(n, ACC_COLS)
    for c0 in range(0, n, width):
        cols = slice(c0, c0 + width)
        tot = None
        for lhs, rhs in terms:
            part = _nt(lhs, rhs(cols)) if transposed_rhs else _nn(lhs, rhs(cols))
            tot = part if tot is None else tot + part
        acc_ref[:, cols] += tot


def _gauss_cdf(x):
    return 0.5 * (1.0 + lax.erf(x * (2.0 ** -0.5)))


def _gelu_grad(x, cdf):
    return cdf + x * jnp.exp(-0.5 * x * x) * (1.0 / math.sqrt(2.0 * math.pi))


def _place():
    x, y, c = lax.axis_index("x"), lax.axis_index("y"), lax.axis_index("c")
    return x, y, c


def _flip(v, bit):
    return 1 - v if bit else v


class _Gather:
    def __init__(self, shards, mid_frac=1.0, relay_frac=0.5):
        self.arrays = list(shards)
        self.relay_frac = relay_frac
        self.mid_frac = mid_frac
        nw = len(shards)
        self.out_shape = [jax.ShapeDtypeStruct((N_DEV,) + s.shape, s.dtype) for s in shards]
        self.scratch = [pltpu.SemaphoreType.DMA((nw, 7)), pltpu.SemaphoreType.DMA((nw, 7)), pltpu.SemaphoreType.DMA((nw,))]

    def _parts(self, ins, outs, sems):
        x, y, c = _place()
        send, recv, loc = sems
        south = c == 0
        near = (jnp.where(south, x, 1 - x), jnp.where(south, 1 - y, y), c)
        far = (jnp.where(south, 1 - x, x), jnp.where(south, y, 1 - y), c)
        diag = (1 - x, 1 - y, c)

        def copy(k, s, block, to, src=None):
            dst = outs[k].at[4 * block[0] + 2 * block[1] + block[2]]
            return pltpu.make_async_remote_copy(src_ref=dst if src is None else src, dst_ref=dst, send_sem=send.at[k, s],
                                                recv_sem=recv.at[k, s], device_id=to, device_id_type=MESH)

        def first(k):
            me = (x, y, c)
            return [copy(k, 0, me, (x, y, 1 - c), src=ins[k]), copy(k, 1, me, (1 - x, y, c), src=ins[k]),
                    copy(k, 2, me, (x, 1 - y, c), src=ins[k])]

        def local(k):
            return pltpu.make_async_copy(ins[k], outs[k].at[4 * x + 2 * y + c], loc.at[k])

        return x, y, c, near, far, diag, copy, first, local

    def start(self, ins, outs, sems):
        *_, first, local = self._parts(ins, outs, sems)
        for k in range(len(ins)):
            local(k).start()
            for cp in first(k):
                cp.start()

    def relay(self, ins, outs, sems):
        x, y, c, near, far, _, copy, _, _ = self._parts(ins, outs, sems)
        for k in range(len(ins)):
            copy(k, 2 - c, near, (x, y, c)).wait_recv()
            copy(k, 3, near, far).start()
            copy(k, 5 - c, near, (x, y, 1 - c)).start()

    def mid(self, ins, outs, sems):
        x, y, c, _, far, diag, copy, _, _ = self._parts(ins, outs, sems)
        for k in range(len(ins)):
            copy(k, 1 + c, far, (x, y, c)).wait_recv()
            copy(k, 4 + c, far, (x, y, 1 - c)).start()
            copy(k, 3, diag, (x, y, c)).wait_recv()
            copy(k, 6, diag, (x, y, 1 - c)).start()

    def finish(self, ins, outs, sems):
        x, y, c, near, _, _, copy, first, local = self._parts(ins, outs, sems)
        sib = (x, y, 1 - c)
        for k in range(len(ins)):
            copy(k, 0, sib, (x, y, c)).wait_recv()
            copy(k, 4, (1 - x, y, 1 - c), (x, y, c)).wait_recv()
            copy(k, 5, (x, 1 - y, 1 - c), (x, y, c)).wait_recv()
            copy(k, 6, (1 - x, 1 - y, 1 - c), (x, y, c)).wait_recv()
        for k in range(len(ins)):
            for cp in first(k):
                cp.wait_send()
            for s in (3, 4, 5, 6):
                copy(k, s, near, sib).wait_send()
            local(k).wait()


class _Swap:
    def __init__(self, parts):
        self.arrays = list(parts)
        nw = len(parts)
        self.out_shape = [jax.ShapeDtypeStruct((N_CHIP,) + p.shape[1:], p.dtype) for p in parts]
        self.scratch = [pltpu.SemaphoreType.DMA((nw, N_CHIP)), pltpu.SemaphoreType.DMA((nw, N_CHIP))]

    def _copy(self, ins, outs, sems, k, q):
        x, y, c = _place()
        return pltpu.make_async_remote_copy(src_ref=ins[k].at[2 * q + 1 - c], dst_ref=outs[k].at[q], send_sem=sems[0].at[k, q],
                                            recv_sem=sems[1].at[k, q], device_id=(x, y, 1 - c), device_id_type=MESH)

    mid_frac = None

    def start(self, ins, outs, sems):
        for k in range(len(ins)):
            for q in range(N_CHIP):
                self._copy(ins, outs, sems, k, q).start()

    def finish(self, ins, outs, sems):
        for k in range(len(ins)):
            for q in range(N_CHIP):
                self._copy(ins, outs, sems, k, q).wait()


class _Ici:
    mid_frac = None

    def __init__(self, sums):
        self.arrays = list(sums)
        nw = len(sums)
        self.out_shape = [jax.ShapeDtypeStruct(s.shape, s.dtype) for s in sums]
        self.scratch = [pltpu.SemaphoreType.DMA((nw, 3)), pltpu.SemaphoreType.DMA((nw, 3)), pltpu.SemaphoreType.DMA((nw,))]

    def _copies(self, ins, outs, sems, k):
        x, y, c = _place()
        myq = 2 * x + y
        out = []
        for r in range(1, N_CHIP):
            px, py = _flip(x, r & 2), _flip(y, r & 1)
            pq = 2 * px + py
            mk = lambda dst: pltpu.make_async_remote_copy(src_ref=ins[k].at[pq], dst_ref=dst, send_sem=sems[0].at[k, r - 1],
                                                          recv_sem=sems[1].at[k, r - 1], device_id=(px, py, c), device_id_type=MESH)
            out.append((mk(outs[k].at[myq]), mk(outs[k].at[pq])))
        return out, pltpu.make_async_copy(ins[k].at[myq], outs[k].at[myq], sems[2].at[k])

    def start(self, ins, outs, sems):
        for k in range(len(ins)):
            remote, local = self._copies(ins, outs, sems, k)
            local.start()
            for snd, _ in remote:
                snd.start()

    def finish(self, ins, outs, sems):
        for k in range(len(ins)):
            remote, local = self._copies(ins, outs, sems, k)
            for snd, rcv in remote:
                rcv.wait_recv()
                snd.wait_send()
            local.wait()


class _Spread:
    mid_frac = None

    def __init__(self, arrays):
        self.arrays = list(arrays)
        nw = len(arrays)
        self.out_shape = [jax.ShapeDtypeStruct((N_DEV,) + a.shape, a.dtype) for a in arrays]
        self.scratch = [pltpu.SemaphoreType.DMA((nw, 7)), pltpu.SemaphoreType.DMA((nw, 7)), pltpu.SemaphoreType.DMA((nw,))]

    def _copies(self, ins, outs, sems, k):
        x, y, c = _place()
        me = 4 * x + 2 * y + c
        out = []
        for r in range(1, N_DEV):
            px, py, pc = _flip(x, r & 4), _flip(y, r & 2), _flip(c, r & 1)
            peer = 4 * px + 2 * py + pc
            mk = lambda dst: pltpu.make_async_remote_copy(src_ref=ins[k], dst_ref=dst, send_sem=sems[0].at[k, r - 1],
                                                          recv_sem=sems[1].at[k, r - 1], device_id=(px, py, pc), device_id_type=MESH)
            out.append((mk(outs[k].at[me]), mk(outs[k].at[peer])))
        return out, pltpu.make_async_copy(ins[k], outs[k].at[me], sems[2].at[k])

    def start(self, ins, outs, sems):
        for k in range(len(ins)):
            remote, local = self._copies(ins, outs, sems, k)
            local.start()
            for snd, _ in remote:
                snd.start()

    def finish(self, ins, outs, sems):
        for k in range(len(ins)):
            remote, local = self._copies(ins, outs, sems, k)
            for snd, rcv in remote:
                rcv.wait_recv()
                snd.wait_send()
            local.wait()


def _call(body, *, grid, in_specs, out_specs, out_shape, name, args, scratch=(), comm=()):
    comm = list(comm)
    n_in, n_out, n_scr = len(in_specs), len(out_specs), len(scratch)
    total = math.prod(grid) if grid else 1

    def wrapped(*refs):
        p = n_in
        cin = []
        for cm in comm:
            cin.append(refs[p:p + len(cm.arrays)])
            p += len(cm.arrays)
        own_out = refs[p:p + n_out]
        p += n_out
        cout = []
        for cm in comm:
            cout.append(refs[p:p + len(cm.arrays)])
            p += len(cm.arrays)
        own_scr = refs[p:p + n_scr]
        p += n_scr
        csem = []
        for cm in comm:
            csem.append(refs[p:p + len(cm.scratch)])
            p += len(cm.scratch)
        step = 0
        for axis, g in enumerate(grid):
            step = step * g + pl.program_id(axis)

        def at(when, what):
            if total == 1:
                what()
            else:
                pl.when(step == when)(what)

        def starts():
            for cm, i, o, s in zip(comm, cin, cout, csem):
                cm.start(i, o, s)

        def finishes():
            for cm, i, o, s in zip(comm, cin, cout, csem):
                cm.finish(i, o, s)

        if comm:
            at(0, starts)
        if body is not None:
            body(*refs[:n_in], *own_out, *own_scr)
        for cm, i, o, s in zip(comm, cin, cout, csem):
            if cm.mid_frac is not None:
                at(min(total - 1, int(total * cm.relay_frac)), lambda cm=cm, i=i, o=o, s=s: cm.relay(i, o, s))
                at(min(total - 1, int(total * cm.mid_frac)), lambda cm=cm, i=i, o=o, s=s: cm.mid(i, o, s))
        if comm:
            at(total - 1, finishes)

    kw = dict(grid=tuple(grid)) if grid else {}
    outs = pl.pallas_call(
        wrapped, name=name, **kw,
        in_specs=list(in_specs) + [ANY for cm in comm for _ in cm.arrays],
        out_specs=list(out_specs) + [ANY for cm in comm for _ in cm.arrays],
        out_shape=list(out_shape) + [s for cm in comm for s in cm.out_shape],
        scratch_shapes=list(scratch) + [s for cm in comm for s in cm.scratch],
        compiler_params=pltpu.CompilerParams(dimension_semantics=("arbitrary",) * len(grid), vmem_limit_bytes=VMEM_LIMIT),
    )(*args, *[a for cm in comm for a in cm.arrays])
    own, p, per = list(outs[:n_out]), n_out, []
    for cm in comm:
        per.append(list(outs[p:p + len(cm.arrays)]))
        p += len(cm.arrays)
    return own, per


def _comm_only(cm, name):
    return _call(None, grid=(), in_specs=[], out_specs=[], out_shape=[], name=name, args=[], comm=[cm])[1][0]


def _rms_fwd(x, g, name, comm=()):
    S, D = x.shape
    tm = _tile(S, 512)

    def body(x_ref, g_ref, o_ref):
        xv = x_ref[...]
        r = lax.rsqrt(jnp.mean(xv * xv, axis=-1, keepdims=True) + NORM_EPS)
        o_ref[...] = (xv * r * g_ref[...]).astype(BF)

    return _call(body, grid=(S // tm,), name=name, args=[x, g], comm=comm,
                 in_specs=[pl.BlockSpec((tm, D), lambda i: (i, 0)), pl.BlockSpec((1, D), lambda i: (0, 0))],
                 out_specs=[pl.BlockSpec((tm, D), lambda i: (i, 0))], out_shape=[jax.ShapeDtypeStruct((S, D), BF)])


def _ffn_up(h, wg, wu, name, comm=()):
    S, D = h.shape
    nb, Fb, _ = wg.shape
    tm = _tile(S, FFN_ROWS)

    def body(h_ref, wg_ref, wu_ref, g_ref, u_ref, a_ref):
        hv = h_ref[...]
        g = _nt(hv, wg_ref[0])
        u = _nt(hv, wu_ref[0])
        g_ref[0] = g.astype(BF)
        u_ref[0] = u.astype(BF)
        a_ref[0] = (g * jax.nn.sigmoid(g) * u).astype(BF)

    act = pl.BlockSpec((1, tm, Fb), lambda j, i: (j, i, 0))
    w = pl.BlockSpec((1, Fb, D), lambda j, i: (j, 0, 0))
    shp = jax.ShapeDtypeStruct((nb, S, Fb), BF)
    return _call(body, grid=(nb, S // tm), name=name, args=[h, wg, wu], comm=comm,
                 in_specs=[pl.BlockSpec((tm, D), lambda j, i: (i, 0)), w, w], out_specs=[act, act, act], out_shape=[shp, shp, shp])


def _ffn_gate(h, wg, name, comm=()):
    S, D = h.shape
    nb, Fb, _ = wg.shape
    tm = _tile(S, FFN_ROWS)

    def body(h_ref, wg_ref, g_ref):
        g_ref[0] = _nt(h_ref[...], wg_ref[0]).astype(BF)

    act = pl.BlockSpec((1, tm, Fb), lambda j, i: (j, i, 0))
    return _call(body, grid=(nb, S // tm), name=name, args=[h, wg], comm=comm,
                 in_specs=[pl.BlockSpec((tm, D), lambda j, i: (i, 0)), pl.BlockSpec((1, Fb, D), lambda j, i: (j, 0, 0))],
                 out_specs=[act], out_shape=[jax.ShapeDtypeStruct((nb, S, Fb), BF)])


def _ffn_up_act(h, wu, g, name, comm=()):
    S, D = h.shape
    nb, Fb, _ = wu.shape
    tm = _tile(S, FFN_ROWS)

    def body(h_ref, wu_ref, g_ref, u_ref, a_ref):
        u = _nt(h_ref[...], wu_ref[0])
        gv = g_ref[0].astype(F32)
        u_ref[0] = u.astype(BF)
        a_ref[0] = (gv * jax.nn.sigmoid(gv) * u).astype(BF)

    act = pl.BlockSpec((1, tm, Fb), lambda j, i: (j, i, 0))
    shp = jax.ShapeDtypeStruct((nb, S, Fb), BF)
    return _call(body, grid=(nb, S // tm), name=name, args=[h, wu, g], comm=comm,
                 in_specs=[pl.BlockSpec((tm, D), lambda j, i: (i, 0)), pl.BlockSpec((1, Fb, D), lambda j, i: (j, 0, 0)), act],
                 out_specs=[act, act], out_shape=[shp, shp])


def _ffn_down_norm(a, wd, x, gn, name, comm=()):
    nb, S, Fb = a.shape
    D = wd.shape[2]
    tm = _tile(S, 512)

    nj = nb // FFN_PAIR

    def body(a_ref, wd_ref, x_ref, gn_ref, xo_ref, hn_ref, acc_ref):
        j = pl.program_id(1)

        @pl.when(j == 0)
        def _():
            acc_ref[...] = jnp.zeros_like(acc_ref)

        _acc_dots(acc_ref, [(a_ref[b], lambda cols, b=b: wd_ref[b, :, cols]) for b in range(FFN_PAIR)])

        @pl.when(j == nj - 1)
        def _():
            def chunk(t, carry):
                rows = pl.ds(pl.multiple_of(t * EPI_ROWS, EPI_ROWS), EPI_ROWS)
                xo = x_ref[rows, :] + 0.5 * acc_ref[rows, :]
                r = lax.rsqrt(jnp.mean(xo * xo, axis=-1, keepdims=True) + NORM_EPS)
                xo_ref[rows, :] = xo
                hn_ref[rows, :] = (xo * r * gn_ref[...]).astype(BF)
                return carry

            lax.fori_loop(0, tm // EPI_ROWS, chunk, 0)

    row = pl.BlockSpec((tm, D), lambda i, j: (i, 0))
    return _call(body, grid=(S // tm, nj), name=name, args=[a, wd, x, gn], comm=comm,
                 in_specs=[pl.BlockSpec((FFN_PAIR, tm, Fb), lambda i, j: (j, i, 0)), pl.BlockSpec((FFN_PAIR, Fb, D), lambda i, j: (j, 0, 0)),
                           row, pl.BlockSpec((1, D), lambda i, j: (0, 0))],
                 out_specs=[row, row], out_shape=[jax.ShapeDtypeStruct((S, D), F32), jax.ShapeDtypeStruct((S, D), BF)],
                 scratch=[pltpu.VMEM((tm, D), F32)])


def _ffn_down_loss(a, wd, x, gf, tgt, name):
    nb, S, Fb = a.shape
    D = wd.shape[2]
    tm = _tile(S, 512)

    nj = nb // FFN_PAIR

    def body(a_ref, wd_ref, x_ref, gf_ref, t_ref, dx_ref, dxb_ref, dgf_ref, loss_ref, acc_ref):
        i, j = pl.program_id(0), pl.program_id(1)

        @pl.when(j == 0)
        def _():
            acc_ref[...] = jnp.zeros_like(acc_ref)

        _acc_dots(acc_ref, [(a_ref[b], lambda cols, b=b: wd_ref[b, :, cols]) for b in range(FFN_PAIR)])

        @pl.when((j == nj - 1) & (i == 0))
        def _():
            dgf_ref[...] = jnp.zeros_like(dgf_ref)
            loss_ref[...] = jnp.zeros_like(loss_ref)

        @pl.when(j == nj - 1)
        def _():
            def chunk(t, carry):
                rows = pl.ds(pl.multiple_of(t * EPI_ROWS, EPI_ROWS), EPI_ROWS)
                xo = x_ref[rows, :] + 0.5 * acc_ref[rows, :]
                r = lax.rsqrt(jnp.mean(xo * xo, axis=-1, keepdims=True) + NORM_EPS)
                xh = xo * r
                gf = gf_ref[...]
                e = xh * gf - t_ref[rows, :]
                loss_ref[...] += jnp.sum(jnp.mean(e * e, axis=-1, keepdims=True), axis=0, keepdims=True) * 0.5
                dy = e * (1.0 / D)
                dgf_ref[...] += jnp.sum(dy * xh, axis=0, keepdims=True)
                dxh = dy * gf
                dx = r * (dxh - xh * jnp.mean(dxh * xh, axis=-1, keepdims=True))
                dx_ref[rows, :] = dx
                dxb_ref[rows, :] = (0.5 * dx).astype(BF)
                return carry

            lax.fori_loop(0, tm // EPI_ROWS, chunk, 0)

    row = pl.BlockSpec((tm, D), lambda i, j: (i, 0))
    once = row
    vec = pl.BlockSpec((1, D), lambda i, j: (0, 0))
    return _call(body, grid=(S // tm, nj), name=name, args=[a, wd, x, gf, tgt],
                 in_specs=[pl.BlockSpec((FFN_PAIR, tm, Fb), lambda i, j: (j, i, 0)), pl.BlockSpec((FFN_PAIR, Fb, D), lambda i, j: (j, 0, 0)),
                           once, vec, once],
                 out_specs=[row, row, vec, pl.BlockSpec((1, 128), lambda i, j: (0, 0))],
                 out_shape=[jax.ShapeDtypeStruct((S, D), F32), jax.ShapeDtypeStruct((S, D), BF), jax.ShapeDtypeStruct((1, D), F32),
                            jax.ShapeDtypeStruct((1, 128), F32)],
                 scratch=[pltpu.VMEM((tm, D), F32)])[0]


def _ffn_bwd_act(dyb, wd, g, u, name, comm=()):
    S, D = dyb.shape
    nb, Fb, _ = wd.shape
    tm = _tile(S, FFN_ROWS)

    def body(dy_ref, wd_ref, g_ref, u_ref, dg_ref, du_ref):
        da = _nt(dy_ref[...], wd_ref[0])
        gv = g_ref[0].astype(F32)
        uv = u_ref[0].astype(F32)
        sg = jax.nn.sigmoid(gv)
        du_ref[0] = (da * gv * sg).astype(BF)
        dg_ref[0] = (da * uv * sg * (1.0 + gv * (1.0 - sg))).astype(BF)

    act = pl.BlockSpec((1, tm, Fb), lambda j, i: (j, i, 0))
    shp = jax.ShapeDtypeStruct((nb, S, Fb), BF)
    return _call(body, grid=(nb, S // tm), name=name, args=[dyb, wd, g, u], comm=comm,
                 in_specs=[pl.BlockSpec((tm, D), lambda j, i: (i, 0)), pl.BlockSpec((1, Fb, D), lambda j, i: (j, 0, 0)), act, act],
                 out_specs=[act, act], out_shape=[shp, shp])


def _ffn_dwd(a, dyb, name, comm=()):
    nb, S, Fb = a.shape
    D = dyb.shape[1]
    ts = _tile(S, FFN_ROWS)
    ns = S // ts

    def body(a_ref, dy_ref, o_ref, acc_ref):
        s = pl.program_id(1)

        @pl.when(s == 0)
        def _():
            acc_ref[...] = jnp.zeros_like(acc_ref)

        acc_ref[...] += _tn(a_ref[0], dy_ref[...])

        @pl.when(s == ns - 1)
        def _():
            o_ref[0] = acc_ref[...].astype(BF)

    return _call(body, grid=(nb, ns), name=name, args=[a, dyb], comm=comm,
                 in_specs=[pl.BlockSpec((1, ts, Fb), lambda j, s: (j, s, 0)), pl.BlockSpec((ts, D), lambda j, s: (s, 0))],
                 out_specs=[pl.BlockSpec((1, Fb, D), lambda j, s: (j, 0, 0))], out_shape=[jax.ShapeDtypeStruct((nb, Fb, D), BF)],
                 scratch=[pltpu.VMEM((Fb, D), F32)])


def _ffn_dwgu(h, dg, du, name, comm=()):
    S, D = h.shape
    nb, _, Fb = dg.shape
    ts = _tile(S, FFN_ROWS)
    ns = S // ts

    def body(h_ref, dg_ref, du_ref, og_ref, ou_ref, accg_ref, accu_ref):
        s = pl.program_id(1)

        @pl.when(s == 0)
        def _():
            accg_ref[...] = jnp.zeros_like(accg_ref)
            accu_ref[...] = jnp.zeros_like(accu_ref)

        hv = h_ref[...]
        accg_ref[...] += _tn(dg_ref[0], hv)
        accu_ref[...] += _tn(du_ref[0], hv)

        @pl.when(s == ns - 1)
        def _():
            og_ref[0] = accg_ref[...].astype(BF)
            ou_ref[0] = accu_ref[...].astype(BF)

    act = pl.BlockSpec((1, ts, Fb), lambda j, s: (j, s, 0))
    out = pl.BlockSpec((1, Fb, D), lambda j, s: (j, 0, 0))
    shp = jax.ShapeDtypeStruct((nb, Fb, D), BF)
    return _call(body, grid=(nb, ns), name=name, args=[h, dg, du], comm=comm,
                 in_specs=[pl.BlockSpec((ts, D), lambda j, s: (s, 0)), act, act], out_specs=[out, out], out_shape=[shp, shp],
                 scratch=[pltpu.VMEM((Fb, D), F32), pltpu.VMEM((Fb, D), F32)])


def _dh_rms_bwd(pairs, blocked, tk, x, gn, dxo, out_scale, name, comm=(), tail=None):
    S, D = x.shape
    nk = pairs[0][0].shape[0] if blocked else pairs[0][0].shape[1] // tk
    tm = _tile(S, 512)
    npair = len(pairs)
    assert blocked or (npair == 1 and tail is not None and nk >= 2)
    nin = 2 * npair + (0 if blocked else 1)

    def body(*refs):
        ins = refs[:nin]
        x_ref, gn_ref, dxo_ref, dx_ref, dxb_ref, dgn_ref, acc_ref = refs[nin:]
        i, k = pl.program_id(0), pl.program_id(1)

        @pl.when(k == 0)
        def _():
            acc_ref[...] = jnp.zeros_like(acc_ref)

        if blocked:
            _acc_dots(acc_ref, [(ins[2 * p][0], lambda cols, r=ins[2 * p + 1]: r[0, :, cols]) for p in range(npair)])
        else:
            @pl.when(k < nk - 1)
            def _():
                _acc_dots(acc_ref, [(ins[0][...], lambda cols: ins[1][:, cols])])

            @pl.when(k == nk - 1)
            def _():
                _acc_dots(acc_ref, [(ins[0][...], lambda cols: ins[2][:, cols])])

        @pl.when((k == nk - 1) & (i == 0))
        def _():
            dgn_ref[...] = jnp.zeros_like(dgn_ref)

        @pl.when(k == nk - 1)
        def _():
            def chunk(t, carry):
                rows = pl.ds(pl.multiple_of(t * EPI_ROWS, EPI_ROWS), EPI_ROWS)
                xv = x_ref[rows, :]
                r = lax.rsqrt(jnp.mean(xv * xv, axis=-1, keepdims=True) + NORM_EPS)
                xh = xv * r
                dh = acc_ref[rows, :]
                dgn_ref[...] += jnp.sum(dh * xh, axis=0, keepdims=True)
                dxh = dh * gn_ref[...]
                dx = dxo_ref[rows, :] + r * (dxh - xh * jnp.mean(dxh * xh, axis=-1, keepdims=True))
                dx_ref[rows, :] = dx
                dxb_ref[rows, :] = (out_scale * dx).astype(BF)
                return carry

            lax.fori_loop(0, tm // EPI_ROWS, chunk, 0)

    if blocked:
        mats = [pl.BlockSpec((1, tm, tk), lambda i, k: (k, i, 0)), pl.BlockSpec((1, tk, D), lambda i, k: (k, 0, 0))] * npair
        flat = [t for pr in pairs for t in pr]
    else:
        mats = [pl.BlockSpec((tm, tk), lambda i, k: (i, k)), pl.BlockSpec((tk, D), lambda i, k: (jnp.minimum(k, nk - 2), 0)),
                pl.BlockSpec((tk, D), lambda i, k: (0, 0))]
        flat = [*pairs[0], tail]
    row = pl.BlockSpec((tm, D), lambda i, k: (i, 0))
    once = row
    vec = pl.BlockSpec((1, D), lambda i, k: (0, 0))
    return _call(body, grid=(S // tm, nk), name=name, args=[*flat, x, gn, dxo], comm=comm,
                 in_specs=mats + [once, vec, once], out_specs=[row, row, vec],
                 out_shape=[jax.ShapeDtypeStruct((S, D), F32), jax.ShapeDtypeStruct((S, D), BF), jax.ShapeDtypeStruct((1, D), F32)],
                 scratch=[pltpu.VMEM((tm, D), F32)])


def _ffn_dh(dg, du, wgT, wuT, x, gn, dxo, name, comm=()):
    S, D = x.shape
    nb, _, Fb = dg.shape
    tm = _tile(S, 512)
    ni, er = S // tm, tm // nb
    assert er % 16 == 0

    def body(dg_ref, wg_ref, du_ref, wu_ref, x_ref, gn_ref, dxo_ref, dx_ref, dxb_ref, dgn_ref, acc_ref):
        i, k = pl.program_id(0), pl.program_id(1)
        slot = i % 2

        @pl.when((i == 0) & (k == 0))
        def _():
            acc_ref[...] = jnp.zeros_like(acc_ref)
            dgn_ref[...] = jnp.zeros_like(dgn_ref)

        @pl.when((i > 0) & (k == 0))
        def _():
            acc_ref[slot] = jnp.zeros((tm, D), F32)

        def finish_rows():
            rows = pl.ds(pl.multiple_of(k * er, er), er)
            xv = x_ref[rows, :]
            r = lax.rsqrt(jnp.mean(xv * xv, axis=-1, keepdims=True) + NORM_EPS)
            xh = xv * r
            dh = acc_ref[1 - slot, rows, :]
            dgn_ref[...] += jnp.where(i > 0, jnp.sum(dh * xh, axis=0, keepdims=True), 0.0)
            dxh = dh * gn_ref[...]
            dx = dxo_ref[rows, :] + r * (dxh - xh * jnp.mean(dxh * xh, axis=-1, keepdims=True))
            dx_ref[rows, :] = dx
            dxb_ref[rows, :] = dx.astype(BF)

        @pl.when(i < ni)
        def _():
            _acc_dots(acc_ref.at[slot], [(dg_ref[0], lambda cols: wg_ref[0, :, cols]), (du_ref[0], lambda cols: wu_ref[0, :, cols])])
            finish_rows()

        @pl.when(i == ni)
        def _():
            finish_rows()

    last = lambda i, k: jnp.where(i == ni, nb - 1, k)
    act = pl.BlockSpec((1, tm, Fb), lambda i, k: (last(i, k), jnp.minimum(i, ni - 1), 0))
    w = pl.BlockSpec((1, Fb, D), lambda i, k: (last(i, k), 0, 0))
    prev = pl.BlockSpec((tm, D), lambda i, k: (jnp.maximum(i - 1, 0), 0))
    vec = pl.BlockSpec((1, D), lambda i, k: (0, 0))
    return _call(body, grid=(ni + 1, nb), name=name, args=[dg, wgT, du, wuT, x, gn, dxo], comm=comm,
                 in_specs=[act, w, act, w, prev, vec, prev], out_specs=[prev, prev, vec],
                 out_shape=[jax.ShapeDtypeStruct((S, D), F32), jax.ShapeDtypeStruct((S, D), BF), jax.ShapeDtypeStruct((1, D), F32)],
                 scratch=[pltpu.VMEM((2, tm, D), F32)])


def _proj_split(a, bT, tm, tn, split, name, comm=()):
    M, K = a.shape
    N = bT.shape[0]
    n_first = split // tn

    def body(a_ref, b_ref, first_ref, rest_ref):
        n = pl.program_id(1)
        y = _nt(a_ref[...], b_ref[...])

        @pl.when(n < n_first)
        def _():
            first_ref[...] = y

        @pl.when(n >= n_first)
        def _():
            rest_ref[...] = y.astype(BF)

    return _call(body, grid=(M // tm, N // tn), name=name, args=[a, bT], comm=comm,
                 in_specs=[pl.BlockSpec((tm, K), lambda i, n: (i, 0)), pl.BlockSpec((tn, K), lambda i, n: (n, 0))],
                 out_specs=[pl.BlockSpec((tm, tn), lambda i, n: (i, jnp.minimum(n, n_first - 1))),
                            pl.BlockSpec((tm, tn), lambda i, n: (i, jnp.maximum(n - n_first, 0)))],
                 out_shape=[jax.ShapeDtypeStruct((M, split), F32), jax.ShapeDtypeStruct((M, N - split), BF)])


def _mm_tn(a, b, tm, tn, ts, blocked, name, comm=(), mrows=None):
    S, M = a.shape[0], (a.shape[1] if mrows is None else mrows)
    N = b.shape[1]
    ns = S // ts
    per_tile = tn // blocked if blocked else 0

    def body(a_ref, b_ref, o_ref, acc_ref):
        s = pl.program_id(2)

        @pl.when(s == 0)
        def _():
            acc_ref[...] = jnp.zeros_like(acc_ref)

        acc_ref[...] += _tn(a_ref[...], b_ref[...])

        @pl.when(s == ns - 1)
        def _():
            if blocked:
                for t in range(per_tile):
                    o_ref[t] = acc_ref[:, t * blocked:(t + 1) * blocked].astype(BF)
            else:
                o_ref[...] = acc_ref[...].astype(BF)

    if blocked:
        ospec = pl.BlockSpec((per_tile, tm, blocked), lambda i, n, s: (n, i, 0))
        oshape = jax.ShapeDtypeStruct((N // blocked, M, blocked), BF)
    else:
        ospec = pl.BlockSpec((tm, tn), lambda i, n, s: (i, n))
        oshape = jax.ShapeDtypeStruct((M, N), BF)
    return _call(body, grid=(M // tm, N // tn, ns), name=name, args=[a, b], comm=comm,
                 in_specs=[pl.BlockSpec((ts, tm), lambda i, n, s: (s, i)), pl.BlockSpec((ts, tn), lambda i, n, s: (s, n))],
                 out_specs=[ospec], out_shape=[oshape], scratch=[pltpu.VMEM((tm, tn), F32)])


def _rope_tables(S):
    half = ROPE_DIM // 2
    inv_freq = ROPE_THETA ** (-jnp.arange(0, ROPE_DIM, 2, dtype=F32) / ROPE_DIM)
    ang = jnp.arange(S, dtype=F32)[:, None] * inv_freq[None, :]
    cos, sin = jnp.cos(ang), jnp.sin(ang)
    zeros = jnp.zeros((S, HEAD_DIM - ROPE_DIM), F32)
    c = jnp.concatenate([cos, cos, jnp.ones((S, HEAD_DIM - ROPE_DIM), F32)], axis=1)
    sm = jnp.concatenate([-sin, jnp.zeros((S, half), F32), zeros], axis=1)
    sp = jnp.concatenate([jnp.zeros((S, half), F32), sin, zeros], axis=1)
    return c, sm, sp


def _rope(t, c, sm, sp):
    return t * c + pltpu.roll(t, HEAD_DIM - ROPE_DIM // 2, 1) * sm + pltpu.roll(t, ROPE_DIM // 2, 1) * sp


def _rope_t(dy, c, sm, sp):
    return dy * c + pltpu.roll(dy * sm, ROPE_DIM // 2, 1) + pltpu.roll(dy * sp, HEAD_DIM - ROPE_DIM // 2, 1)


def _att_mask(i):
    qi = lax.broadcasted_iota(jnp.int32, (BLK, 2 * BLK), 0)
    kj = lax.broadcasted_iota(jnp.int32, (BLK, 2 * BLK), 1)
    diff = qi + BLK - kj
    first_key = jnp.where(i > 0, 0, BLK)
    return (diff >= 0) & (diff <= BLK) & (kj >= first_key)


def _res_rows(r, i, n, d):
    if d == 1:
        return pl.ds(pl.multiple_of(i * n, n), n)
    return pl.ds(r + i * (n * d), n, stride=d)


def _att_specs(S, gi):
    def sect(off):
        base = (off + gi * GROUP_W) // HEAD_DIM
        return _once((S, HEAD_DIM), lambda hh: (0, base + hh))

    tab = pl.BlockSpec((S, HEAD_DIM), lambda hh: (0, 0))
    head = pl.BlockSpec((S, HEAD_DIM), lambda hh: (0, hh))
    return sect, tab, head


def _each_residue(d, fn):
    if d == 1:
        fn(0)
    else:
        lax.fori_loop(0, d, lambda r, carry: (fn(r), carry)[1], 0)


def _att_fwd(qkv, tabs, gi, d, name, comm=()):
    S = qkv.shape[0]
    L = S // d
    sect, tab, head = _att_specs(S, gi)
    nblk = L // BLK
    scale = HEAD_DIM ** -0.5

    def body(q_ref, k_ref, v_ref, c_ref, sm_ref, sp_ref, o_ref, lse_ref, qr, kp, vp):
        kp[pl.ds(0, BLK), :] = jnp.zeros((BLK, HEAD_DIM), BF)
        vp[pl.ds(0, BLK), :] = jnp.zeros((BLK, HEAD_DIM), BF)

        def residue(r):
            res = _res_rows(r, 0, L, d)
            c, sm, sp = c_ref[res, :], sm_ref[res, :], sp_ref[res, :]
            qr[...] = _rope(q_ref[res, :], c, sm, sp).astype(BF)
            kp[pl.ds(BLK, L), :] = _rope(k_ref[res, :], c, sm, sp).astype(BF)
            vp[pl.ds(BLK, L), :] = v_ref[res, :].astype(BF)

            def blk(i, carry):
                r0 = pl.multiple_of(i * BLK, BLK)
                s = _nt(qr[pl.ds(r0, BLK), :], kp[pl.ds(r0, 2 * BLK), :]) * scale
                s = jnp.where(_att_mask(i), s, NEG)
                m = jnp.max(s, axis=-1, keepdims=True)
                p = jnp.exp(s - m)
                l = jnp.sum(p, axis=-1, keepdims=True)
                out = _res_rows(r, i, BLK, d)
                o_ref[out, :] = _nn(p.astype(BF), vp[pl.ds(r0, 2 * BLK), :]) / l
                lse_ref[out, :] = jnp.broadcast_to(m + jnp.log(l), (BLK, HEAD_DIM))
                return carry

            lax.fori_loop(0, nblk, blk, 0, unroll=min(4, nblk))

        _each_residue(d, residue)

    shp = jax.ShapeDtypeStruct((S, GROUP_W), F32)
    return _call(body, grid=(HEADS_PER_GROUP,), name=name, args=[qkv, qkv, qkv, *tabs], comm=comm,
                 in_specs=[sect(Q_OFF), sect(K_OFF), sect(V_OFF), tab, tab, tab], out_specs=[head, head], out_shape=[shp, shp],
                 scratch=[pltpu.VMEM((L, HEAD_DIM), BF), pltpu.VMEM((L + BLK, HEAD_DIM), BF), pltpu.VMEM((L + BLK, HEAD_DIM), BF)])


def _att_combine(os, lses, name):
    S = os[0].shape[0]
    tm = _tile(S, 512)

    def body(o0, o1, o2, l0, l1, l2, oa_ref, lse_ref):
        a, b, c = l0[...], l1[...], l2[...]
        mx = jnp.maximum(jnp.maximum(a, b), c)
        wa, wb, wc = jnp.exp(a - mx), jnp.exp(b - mx), jnp.exp(c - mx)
        den = wa + wb + wc
        oa_ref[...] = ((wa * o0[...] + wb * o1[...] + wc * o2[...]) / den).astype(BF)
        lse_ref[...] = mx + jnp.log(den)

    row = pl.BlockSpec((tm, GROUP_W), lambda i: (i, 0))
    return _call(body, grid=(S // tm,), name=name, args=[*os, *lses], in_specs=[row] * 6, out_specs=[row, row],
                 out_shape=[jax.ShapeDtypeStruct((S, GROUP_W), BF), jax.ShapeDtypeStruct((S, GROUP_W), F32)])[0]


def _att_bwd(qkv, tabs, do, lse, dvec, gi, d, name, comm=()):
    S = qkv.shape[0]
    L = S // d
    sect, tab, head = _att_specs(S, gi)
    stat = _once((S, HEAD_DIM), lambda hh: (0, hh))
    nblk = L // BLK
    scale = HEAD_DIM ** -0.5

    def body(q_ref, k_ref, v_ref, c_ref, sm_ref, sp_ref, do_ref, lse_ref, dv_ref, dq_out, dk_out, dv_out, qr, kp, vp, dkp, dvp):
        kp[pl.ds(0, BLK), :] = jnp.zeros((BLK, HEAD_DIM), BF)
        vp[pl.ds(0, BLK), :] = jnp.zeros((BLK, HEAD_DIM), BF)

        def residue(r):
            res = _res_rows(r, 0, L, d)
            c, sm, sp = c_ref[res, :], sm_ref[res, :], sp_ref[res, :]
            qr[...] = _rope(q_ref[res, :], c, sm, sp).astype(BF)
            kp[pl.ds(BLK, L), :] = _rope(k_ref[res, :], c, sm, sp).astype(BF)
            vp[pl.ds(BLK, L), :] = v_ref[res, :].astype(BF)
            dkp[...] = jnp.zeros_like(dkp)
            dvp[...] = jnp.zeros_like(dvp)

            def blk(i, carry):
                r0 = pl.multiple_of(i * BLK, BLK)
                rows, win, pos = pl.ds(r0, BLK), pl.ds(r0, 2 * BLK), _res_rows(r, i, BLK, d)
                q, kw, vw, dob = qr[rows, :], kp[win, :], vp[win, :], do_ref[pos, :].astype(BF)
                s = jnp.where(_att_mask(i), _nt(q, kw) * scale, NEG)
                p = jnp.exp(s - lse_ref[pos, :][:, :1])
                ds = p * (_nt(dob, vw) - dv_ref[pos, :][:, :1]) * scale
                dsb = ds.astype(BF)
                dq_out[pos, :] = _rope_t(_nn(dsb, kw), c_ref[pos, :], sm_ref[pos, :], sp_ref[pos, :])
                dkp[win, :] += _tn(dsb, q)
                dvp[win, :] += _tn(p.astype(BF), dob)
                return carry

            lax.fori_loop(0, nblk, blk, 0, unroll=2)
            dk_out[res, :] = _rope_t(dkp[pl.ds(BLK, L), :], c, sm, sp)
            dv_out[res, :] = dvp[pl.ds(BLK, L), :]

        _each_residue(d, residue)

    shp = jax.ShapeDtypeStruct((S, GROUP_W), F32)
    return _call(body, grid=(HEADS_PER_GROUP,), name=name, args=[qkv, qkv, qkv, *tabs, do, lse, dvec], comm=comm,
                 in_specs=[sect(Q_OFF), sect(K_OFF), sect(V_OFF), tab, tab, tab, stat, stat, stat],
                 out_specs=[head, head, head], out_shape=[shp, shp, shp],
                 scratch=[pltpu.VMEM((L, HEAD_DIM), BF), pltpu.VMEM((L + BLK, HEAD_DIM), BF), pltpu.VMEM((L + BLK, HEAD_DIM), BF),
                          pltpu.VMEM((L + BLK, HEAD_DIM), F32), pltpu.VMEM((L + BLK, HEAD_DIM), F32)])


def _sg_parts(u_ref, vs_ref, g_ref, b_ref):
    uv = u_ref[...].astype(F32)
    vv = vs_ref[...].astype(F32)
    cv = _gauss_cdf(vv)
    vg = vv * cv
    mu = jnp.mean(vg, axis=-1, keepdims=True)
    vc = vg - mu
    rs = lax.rsqrt(jnp.mean(vc * vc, axis=-1, keepdims=True) + LN_EPS)
    y = vc * rs
    return uv, vv, cv, rs, y, y * g_ref[...] + b_ref[...]


def _sg_wmask():
    t = lax.broadcasted_iota(jnp.int32, (BLK, BLK), 0)
    s = lax.broadcasted_iota(jnp.int32, (BLK, BLK), 1)
    return s <= t


def _sg_fwd(proj, sgw, sgbT, lng, lnb, name):
    S, P = proj.shape

    def body(u_ref, vs_ref, w_ref, bt_ref, g_ref, b_ref, z_ref):
        uv, _, _, _, _, vln = _sg_parts(u_ref, vs_ref, g_ref, b_ref)
        ug = uv * _gauss_cdf(uv)
        vb = vln.astype(BF)
        mask = _sg_wmask()
        bt = bt_ref[...]
        for g in range(SG_GROUPS):
            cols = slice(g * BLK, (g + 1) * BLK)
            w = jnp.where(mask, w_ref[g], 0.0).astype(BF)
            sp = _nn(w, vb[:, cols]) + bt[:, g:g + 1]
            z_ref[:, cols] = (ug[:, cols] * sp).astype(BF)

    tile = lambda off: pl.BlockSpec((BLK, SG_W), lambda i: (i, off // SG_W))
    full = lambda shape: pl.BlockSpec(shape, lambda i: (0,) * len(shape))
    return _call(body, grid=(S // BLK,), name=name, args=[proj, proj, sgw, sgbT, lng, lnb],
                 in_specs=[tile(R_U), tile(R_VS), full((SG_GROUPS, BLK, BLK)), full((BLK, BLK)), full((1, SG_W)), full((1, SG_W))],
                 out_specs=[pl.BlockSpec((BLK, SG_W), lambda i: (i, 0))], out_shape=[jax.ShapeDtypeStruct((S, SG_W), BF)])[0][0]


def _sg_bwd(proj, dz, sgw, sgbT, lng, lnb, name):
    S, P = proj.shape

    def body(u_ref, vs_ref, dz_ref, w_ref, bt_ref, g_ref, b_ref, du_ref, dvs_ref, dw_ref, dbt_ref, dg_ref, db_ref, dvln):
        @pl.when(pl.program_id(0) == 0)
        def _():
            dw_ref[...] = jnp.zeros_like(dw_ref)
            dbt_ref[...] = jnp.zeros_like(dbt_ref)
            dg_ref[...] = jnp.zeros_like(dg_ref)
            db_ref[...] = jnp.zeros_like(db_ref)

        uv, vv, cv, rs, y, vln = _sg_parts(u_ref, vs_ref, g_ref, b_ref)
        cu = _gauss_cdf(uv)
        ug = uv * cu
        dug = _gelu_grad(uv, cu)
        vb = vln.astype(BF)
        dzv = dz_ref[...].astype(F32)
        dsp = dzv * ug
        dspb = dsp.astype(BF)
        mask = _sg_wmask()
        bt = bt_ref[...]
        lane = lax.broadcasted_iota(jnp.int32, (BLK, BLK), 1)
        dbt = jnp.zeros((BLK, BLK), F32)
        for g in range(SG_GROUPS):
            cols = slice(g * BLK, (g + 1) * BLK)
            w = jnp.where(mask, w_ref[g], 0.0).astype(BF)
            sp = _nn(w, vb[:, cols]) + bt[:, g:g + 1]
            du_ref[:, cols] = (dzv[:, cols] * sp * dug[:, cols]).astype(BF)
            dw_ref[g] += jnp.where(mask, _nt(dspb[:, cols], vb[:, cols]), 0.0)
            dbt = dbt + jnp.where(lane == g, jnp.sum(dsp[:, cols], axis=-1, keepdims=True), 0.0)
            dvln[:, cols] = _tn(w, dspb[:, cols])
        dbt_ref[...] += dbt
        dvl = dvln[...]
        dg_ref[...] += jnp.sum(dvl * y, axis=0, keepdims=True)
        db_ref[...] += jnp.sum(dvl, axis=0, keepdims=True)
        dy = dvl * g_ref[...]
        dvg = rs * (dy - jnp.mean(dy, axis=-1, keepdims=True) - y * jnp.mean(dy * y, axis=-1, keepdims=True))
        dvs_ref[...] = (dvg * _gelu_grad(vv, cv)).astype(BF)

    tile = lambda off: pl.BlockSpec((BLK, SG_W), lambda i: (i, off // SG_W))
    full = lambda shape: pl.BlockSpec(shape, lambda i: (0,) * len(shape))
    row = pl.BlockSpec((BLK, SG_W), lambda i: (i, 0))
    return _call(body, grid=(S // BLK,), name=name, args=[proj, proj, dz, sgw, sgbT, lng, lnb],
                 in_specs=[tile(R_U), tile(R_VS), row, full((SG_GROUPS, BLK, BLK)), full((BLK, BLK)), full((1, SG_W)), full((1, SG_W))],
                 out_specs=[row, row, full((SG_GROUPS, BLK, BLK)), full((BLK, BLK)), full((1, SG_W)), full((1, SG_W))],
                 out_shape=[jax.ShapeDtypeStruct((S, SG_W), BF), jax.ShapeDtypeStruct((S, SG_W), BF),
                            jax.ShapeDtypeStruct((SG_GROUPS, BLK, BLK), F32), jax.ShapeDtypeStruct((BLK, BLK), F32),
                            jax.ShapeDtypeStruct((1, SG_W), F32), jax.ShapeDtypeStruct((1, SG_W), F32)],
                 scratch=[pltpu.VMEM((BLK, SG_W), F32)])[0]


def _gate_merge(oatt, z, watt, wsg, proj, name, comm=()):
    S = oatt.shape[0]
    nb, _, Db = watt.shape
    D = nb * Db
    tm = _tile(S, 512)
    half = D // 2
    ga, gs = R_GA // half, (R_GA + D) // half

    def body(oa_ref, z_ref, wa_ref, ws_ref, ga0, ga1, gs0, gs1, ya_ref, ys_ref, mg_ref):
        oa, zv = oa_ref[...], z_ref[...]
        for j in range(nb):
            cols = slice(j * Db, (j + 1) * Db)
            g_a, g_s = (ga0, gs0) if j < nb // 2 else (ga1, gs1)
            gcols = slice((j % (nb // 2)) * Db, (j % (nb // 2) + 1) * Db)
            ya = _nn(oa, wa_ref[j])
            ys = _nn(zv, ws_ref[j])
            ya_ref[:, cols] = ya.astype(BF)
            ys_ref[:, cols] = ys.astype(BF)
            mg_ref[:, cols] = (jax.nn.sigmoid(g_a[:, gcols].astype(F32)) * ya + jax.nn.sigmoid(g_s[:, gcols].astype(F32)) * ys).astype(BF)

    out = pl.BlockSpec((tm, D), lambda i: (i, 0))
    gate = lambda b: pl.BlockSpec((tm, half), lambda i: (i, b))
    shp = jax.ShapeDtypeStruct((S, D), BF)
    return _call(body, grid=(S // tm,), name=name, args=[oatt, z, watt, wsg, proj, proj, proj, proj], comm=comm,
                 in_specs=[pl.BlockSpec((tm, GROUP_W), lambda i: (i, 0)), pl.BlockSpec((tm, SG_W), lambda i: (i, 0)),
                           pl.BlockSpec((nb, GROUP_W, Db), lambda i: (0, 0, 0)), pl.BlockSpec((nb, SG_W, Db), lambda i: (0, 0, 0)),
                           gate(ga), gate(ga + 1), gate(gs), gate(gs + 1)],
                 out_specs=[out, out, out], out_shape=[shp, shp, shp])


def _mix_out(merged, wout, x, gn, name):
    S, D = x.shape
    tm = _tile(S, 256)

    def body(m_ref, w_ref, x_ref, gn_ref, xo_ref, hn_ref):
        xo = x_ref[...] + _nn(m_ref[...], w_ref[...])
        r = lax.rsqrt(jnp.mean(xo * xo, axis=-1, keepdims=True) + NORM_EPS)
        xo_ref[...] = xo
        hn_ref[...] = (xo * r * gn_ref[...]).astype(BF)

    row = pl.BlockSpec((tm, D), lambda i: (i, 0))
    return _call(body, grid=(S // tm,), name=name, args=[merged, wout, x, gn],
                 in_specs=[row, pl.BlockSpec((D, D), lambda i: (0, 0)), row, pl.BlockSpec((1, D), lambda i: (0, 0))],
                 out_specs=[row, row], out_shape=[jax.ShapeDtypeStruct((S, D), F32), jax.ShapeDtypeStruct((S, D), BF)])[0]


def _mix_bwd_gate(dmix, wout, ya, ys, proj, name):
    S, D = dmix.shape
    tm, tn = _tile(S, 256), min(512, D // 2)
    half = D // 2
    ga, gs = R_GA // half, (R_GA + D) // half

    def body(dm_ref, w_ref, ya_ref, ys_ref, ga0, ga1, gs0, gs1, dya_ref, dys_ref, dga_ref, dgs_ref):
        dmv = dm_ref[...]
        for c0 in range(0, D, tn):
            cols = slice(c0, c0 + tn)
            g_a, g_s = (ga0, gs0) if c0 < half else (ga1, gs1)
            gcols = slice(c0 % half, c0 % half + tn)
            dm = _nt(dmv, w_ref[cols, :])
            sa = jax.nn.sigmoid(g_a[:, gcols].astype(F32))
            ss = jax.nn.sigmoid(g_s[:, gcols].astype(F32))
            dya_ref[:, cols] = (dm * sa).astype(BF)
            dys_ref[:, cols] = (dm * ss).astype(BF)
            dga_ref[:, cols] = (dm * ya_ref[:, cols].astype(F32) * sa * (1.0 - sa)).astype(BF)
            dgs_ref[:, cols] = (dm * ys_ref[:, cols].astype(F32) * ss * (1.0 - ss)).astype(BF)

    row = pl.BlockSpec((tm, D), lambda i: (i, 0))
    gate = lambda b: pl.BlockSpec((tm, half), lambda i: (i, b))
    shp = jax.ShapeDtypeStruct((S, D), BF)
    return _call(body, grid=(S // tm,), name=name, args=[dmix, wout, ya, ys, proj, proj, proj, proj],
                 in_specs=[row, pl.BlockSpec((D, D), lambda i: (0, 0)), row, row, gate(ga), gate(ga + 1), gate(gs), gate(gs + 1)],
                 out_specs=[row] * 4, out_shape=[shp] * 4)[0]


def _att_sg_dout(dya, dys, watt, wsg, oatt, name, comm=()):
    S, D = dya.shape
    nb, _, Db = watt.shape
    tm = _tile(S, 512)

    def body(dya_ref, dys_ref, wa_ref, ws_ref, oa_ref, do_ref, dz_ref, dvec_ref):
        def back(dy_ref, w_ref, rows):
            tot = None
            for j in range(nb):
                part = _nt(dy_ref[:, j * Db:(j + 1) * Db], w_ref[j, rows, :])
                tot = part if tot is None else tot + part
            return tot

        dov = back(dya_ref, wa_ref, slice(0, GROUP_W))
        do_ref[...] = dov
        for c0 in range(0, SG_W, GROUP_W):
            dz_ref[:, c0:c0 + GROUP_W] = back(dys_ref, ws_ref, slice(c0, c0 + GROUP_W)).astype(BF)
        prod = dov * oa_ref[...].astype(F32)
        for hh in range(HEADS_PER_GROUP):
            cols = slice(hh * HEAD_DIM, (hh + 1) * HEAD_DIM)
            dvec_ref[:, cols] = jnp.broadcast_to(jnp.sum(prod[:, cols], axis=-1, keepdims=True), (tm, HEAD_DIM))

    row = pl.BlockSpec((tm, D), lambda i: (i, 0))
    att = pl.BlockSpec((tm, GROUP_W), lambda i: (i, 0))
    return _call(body, grid=(S // tm,), name=name, args=[dya, dys, watt, wsg, oatt], comm=comm,
                 in_specs=[row, row, pl.BlockSpec((nb, GROUP_W, Db), lambda i: (0, 0, 0)), pl.BlockSpec((nb, SG_W, Db), lambda i: (0, 0, 0)), att],
                 out_specs=[att, pl.BlockSpec((tm, SG_W), lambda i: (i, 0)), att],
                 out_shape=[jax.ShapeDtypeStruct((S, GROUP_W), F32), jax.ShapeDtypeStruct((S, SG_W), BF), jax.ShapeDtypeStruct((S, GROUP_W), F32)])[0]


def _small_allreduce(pack, name):
    R = pack.shape[0]

    def body(p_ref, o_ref, gath, send, recv):
        x, y, c = _place()
        me = 4 * x + 2 * y + c
        gath[me] = p_ref[...]
        copies = []
        for r in range(1, N_DEV):
            px, py, pc = _flip(x, r & 4), _flip(y, r & 2), _flip(c, r & 1)
            peer = 4 * px + 2 * py + pc
            mk = lambda dst: pltpu.make_async_remote_copy(src_ref=p_ref, dst_ref=dst, send_sem=send.at[r - 1], recv_sem=recv.at[r - 1],
                                                          device_id=(px, py, pc), device_id_type=MESH)
            snd = mk(gath.at[me])
            snd.start()
            copies.append((snd, mk(gath.at[peer])))
        for snd, rcv in copies:
            rcv.wait_recv()
            snd.wait_send()
        acc = gath[0]
        for s in range(1, N_DEV):
            acc = acc + gath[s]
        o_ref[...] = acc

    vm = pl.BlockSpec(memory_space=pltpu.VMEM)
    return pl.pallas_call(
        body, name=name, in_specs=[vm], out_specs=vm, out_shape=jax.ShapeDtypeStruct(pack.shape, F32),
        scratch_shapes=[pltpu.VMEM((N_DEV, R, 128), F32), pltpu.SemaphoreType.DMA((7,)), pltpu.SemaphoreType.DMA((7,))],
        compiler_params=pltpu.CompilerParams(vmem_limit_bytes=VMEM_LIMIT),
    )(pack)


def _row_tile(R, C, elems=262144):
    fits = [t for t in range(16, R + 1, 16) if R % t == 0 and t * C <= elems]
    return max(fits) if fits else R


def _pair_add(parts, other, name):
    _, R, C = parts.shape
    tr = _row_tile(R, C, 1048576)

    def body(c_ref, p_ref, o_ref, s_ref):
        s_ref[0] = (p_ref[0].astype(F32) + o_ref[0].astype(F32)).astype(BF)

    core = lax.axis_index("c").astype(jnp.int32).reshape(1)
    return pl.pallas_call(
        body, name=name,
        grid_spec=pltpu.PrefetchScalarGridSpec(
            num_scalar_prefetch=1, grid=(N_CHIP, R // tr),
            in_specs=[pl.BlockSpec((1, tr, C), lambda q, i, c: (2 * q + c[0], i, 0)), pl.BlockSpec((1, tr, C), lambda q, i, c: (q, i, 0))],
            out_specs=pl.BlockSpec((1, tr, C), lambda q, i, c: (q, i, 0))),
        out_shape=jax.ShapeDtypeStruct((N_CHIP, R, C), BF),
        compiler_params=pltpu.CompilerParams(dimension_semantics=("arbitrary", "arbitrary"), vmem_limit_bytes=VMEM_LIMIT),
    )(core, parts, other)


def _adamw(parts, w, m, v, name):
    ns, R, C = parts.shape
    tr = _row_tile(R, C, 524288)
    c1 = 1.0 - ADAM_B1 ** ADAM_STEP
    c2 = 1.0 - ADAM_B2 ** ADAM_STEP

    def body(p_ref, w_ref, m_ref, v_ref, g_ref, d_ref, nm_ref, nv_ref):
        g = p_ref[0].astype(F32)
        for s in range(1, ns):
            g = g + p_ref[s].astype(F32)
        mn = ADAM_B1 * m_ref[...] + (1.0 - ADAM_B1) * g
        vn = ADAM_B2 * v_ref[...] + (1.0 - ADAM_B2) * (g * g)
        g_ref[...] = g
        nm_ref[...] = mn
        nv_ref[...] = vn
        d_ref[...] = -ADAM_LR * ((mn / c1) / (jnp.sqrt(vn / c2) + ADAM_EPS) + ADAM_WD * w_ref[...])

    row = pl.BlockSpec((tr, C), lambda i: (i, 0))
    shp = jax.ShapeDtypeStruct((R, C), F32)
    return _call(body, grid=(R // tr,), name=name, args=[parts, w, m, v],
                 in_specs=[pl.BlockSpec((ns, tr, C), lambda i: (0, i, 0)), row, row, row], out_specs=[row] * 4, out_shape=[shp] * 4)[0]


def _pad_rows(a, rows):
    return jnp.pad(a, ((0, rows - a.shape[0]), (0, 0)))


def kernel(x, ffn1_norm, ffn1_w_gate, ffn1_w_up, ffn1_w_down, mix_norm, w_in, sg_ln_g, sg_ln_b, sg_w, sg_b, w_att_out, w_sg_out, w_out, ffn2_norm, ffn2_w_gate, ffn2_w_up, ffn2_w_down, final_norm, loss_target, m_ffn1_norm, m_ffn1_w_gate, m_ffn1_w_up, m_ffn1_w_down, m_mix_norm, m_w_in, m_sg_ln_g, m_sg_ln_b, m_sg_w, m_sg_b, m_w_att_out, m_w_sg_out, m_w_out, m_ffn2_norm, m_ffn2_w_gate, m_ffn2_w_up, m_ffn2_w_down, m_final_norm, v_ffn1_norm, v_ffn1_w_gate, v_ffn1_w_up, v_ffn1_w_down, v_mix_norm, v_w_in, v_sg_ln_g, v_sg_ln_b, v_sg_w, v_sg_b, v_w_att_out, v_w_sg_out, v_w_out, v_ffn2_norm, v_ffn2_w_gate, v_ffn2_w_up, v_ffn2_w_down, v_final_norm):
    S, D = x.shape[1], x.shape[2]
    Pb = w_in.shape[2]
    P = N_DEV * Pb
    assert P == GA_OFF + 2 * D and D % (N_DEV * 128) == 0 and S % (BLK * DILATIONS[-1]) == 0
    xs, tgt = x[0], loss_target[0]

    sharded = dict(ffn1_w_gate=ffn1_w_gate, ffn1_w_up=ffn1_w_up, ffn1_w_down=ffn1_w_down, w_in=w_in, w_att_out=w_att_out,
                   w_sg_out=w_sg_out, w_out=w_out, ffn2_w_gate=ffn2_w_gate, ffn2_w_up=ffn2_w_up, ffn2_w_down=ffn2_w_down)
    cols = ("ffn1_w_gate", "ffn1_w_up", "w_in", "ffn2_w_gate", "ffn2_w_up")
    local = lambda n, a: a[0].T if n in cols else a[0]
    back = lambda n, a: a.T[None] if n in cols else a[None]
    wloc = {n: local(n, w) for n, w in sharded.items()}
    sb = {n: w.astype(BF) for n, w in wloc.items()}

    (h1,), ((wg1,),) = _rms_fwd(xs, ffn1_norm, "rms1", comm=[_Gather([sb["ffn1_w_gate"]], 1.0, 1.0)])
    (g1,), ((wu1,),) = _ffn_gate(h1, wg1, "ffn1_gate", comm=[_Gather([sb["ffn1_w_up"]], 0.9, 0.55)])
    (u1, a1), ((wd1,),) = _ffn_up_act(h1, wu1, g1, "ffn1_up_act", comm=[_Gather([sb["ffn1_w_down"]], 0.9, 0.55)])
    (x1, h2), ((winT8,),) = _ffn_down_norm(a1, wd1, xs, mix_norm, "ffn1_down", comm=[_Gather([sb["w_in"]], 1.0, 0.7)])
    winT = winT8.reshape(P, D)
    (qkv, rest), ((wg2, wu2),) = _proj_split(h2, winT, _tile(S, 1024), 512, U_OFF, "proj",
                                           comm=[_Gather([sb["ffn2_w_gate"], sb["ffn2_w_up"]], 0.85, 0.5)])
    tabs = _rope_tables(S)
    rides = [[_Gather([sb["w_att_out"], sb["w_sg_out"]], 0.9, 0.5)], [_Gather([sb["w_out"]], 0.85, 0.45)], []]
    os, lses, late = [], [], []
    for gi, d in enumerate(DILATIONS):
        (o, l), got_here = _att_fwd(qkv, tabs, gi, d, f"att_fwd{gi}", comm=rides[gi])
        late += [w for g in got_here for w in g]
        os.append(o)
        lses.append(l)
    watt, wsg, wout8 = late
    wout = wout8.reshape(D, D)
    oatt, lse = _att_combine(os, lses, "att_combine")
    sgw = sg_w[0]
    sgbT = jnp.pad(sg_b[0].T, ((0, 0), (0, BLK - SG_GROUPS)))
    z = _sg_fwd(rest, sgw, sgbT, sg_ln_g, sg_ln_b, "sg_fwd")
    (ya, ys, merged), _ = _gate_merge(oatt, z, watt, wsg, rest, "gate_merge")
    x2, h3 = _mix_out(merged, wout, x1, ffn2_norm, "mix_out")
    (g3, u3, a3), ((wd2,),) = _ffn_up(h3, wg2, wu2, "ffn2_up", comm=[_Gather([sb["ffn2_w_down"]], 0.6, 0.35)])
    dx3, dyb3, d_final, loss_part = _ffn_down_loss(a3, wd2, x2, final_norm.reshape(1, D), tgt, "ffn2_down_loss")

    Fb = wg2.shape[1]
    Db = watt.shape[2]
    p_pad = -(-P // PROJ_TK) * PROJ_TK
    win_tail = _pad_rows(winT[p_pad - PROJ_TK:], PROJ_TK)
    (dg3, du3), _ = _ffn_bwd_act(dyb3, wd2, g3, u3, "ffn2_bwd_act")
    (dwd2,), _ = _ffn_dwd(a3, dyb3, "ffn2_dwd")
    (dwg2, dwu2), _ = _ffn_dwgu(h3, dg3, du3, "ffn2_dwgu")
    ffn2_parts = [dwd2, dwg2, dwu2]
    (dx2, dmixb, d_ffn2n), (ffn2_other,) = _ffn_dh(dg3, du3, wg2, wu2, x2, ffn2_norm, dx3, "ffn2_dh", comm=[_Swap(ffn2_parts)])
    ffn2_sums = [_pair_add(p, o, f"pair_ffn2_{i}") for i, (p, o) in enumerate(zip(ffn2_parts, ffn2_other))]

    dya, dys, dga, dgs = _mix_bwd_gate(dmixb, wout, ya, ys, rest, "mix_bwd_gate")
    (dwout,), _ = _mm_tn(merged, dmixb, _tile(D, 1024), _tile(D, 1024), _tile(S, 1024), False, "dw_out")
    do, dz, dvec = _att_sg_dout(dya, dys, watt, wsg, oatt, "att_sg_dout")
    (dwatt,), _ = _mm_tn(oatt, dya, GROUP_W, 2 * Db, _tile(S, 1024), Db, "dw_att")
    (dwsg,), _ = _mm_tn(z, dys, SG_W, 2 * Db, _tile(S, 1024), Db, "dw_sg")
    mix_parts = [dwout.reshape(N_DEV, D // N_DEV, D), dwatt, dwsg]
    du, dvs, d_sgw, d_sgbT, d_lng, d_lnb = _sg_bwd(rest, dz, sgw, sgbT, sg_ln_g, sg_ln_b, "sg_bwd")
    dqs, dks, dvs_att, ffn2_got = [], [], [], []
    for gi, d in enumerate(DILATIONS):
        ride = [_Ici([ffn2_sums[0]])] if gi == 2 else []
        (dq, dk, dv), got_here = _att_bwd(qkv, tabs, do, lse, dvec, gi, d, f"att_bwd{gi}", comm=ride)
        ffn2_got += [g[0] for g in got_here]
        dqs.append(dq)
        dks.append(dk)
        dvs_att.append(dv)
    dproj = jnp.concatenate([t.astype(BF) for t in dqs + dks + dvs_att] + [du, dvs, dga, dgs, jnp.zeros((S, p_pad - P), BF)], axis=1)
    (dx1, dyb1, d_mixn), (ffn2_rest, mix_other) = _dh_rms_bwd([(dproj, winT)], False, PROJ_TK, x1, mix_norm, dx2, 0.5, "proj_dh",
                                                            comm=[_Ici(ffn2_sums[1:]), _Swap(mix_parts)], tail=win_tail)
    ffn2_got += ffn2_rest
    mix_sums = [_pair_add(p, o, f"pair_mix_{i}") for i, (p, o) in enumerate(zip(mix_parts, mix_other))]
    (dwd1,), (mix_got,) = _ffn_dwd(a1, dyb1, "ffn1_dwd", comm=[_Ici(mix_sums)])
    rows = lambda a: a.reshape(-1, 128)
    pad8 = lambda a: _pad_rows(a, -(-a.shape[0] // 8) * 8)
    small = [("sg_w", rows(d_sgw), sg_w, m_sg_w, v_sg_w), ("mix_norm", rows(d_mixn), mix_norm, m_mix_norm, v_mix_norm),
             ("ffn2_norm", rows(d_ffn2n), ffn2_norm, m_ffn2_norm, v_ffn2_norm), ("final_norm", rows(d_final), final_norm, m_final_norm, v_final_norm),
             ("sg_ln_g", rows(d_lng), sg_ln_g, m_sg_ln_g, v_sg_ln_g), ("sg_ln_b", rows(d_lnb), sg_ln_b, m_sg_ln_b, v_sg_ln_b),
             ("sg_b", d_sgbT[:, :SG_GROUPS].T, sg_b, m_sg_b, v_sg_b)]
    gpack = jnp.concatenate([pad8(g) for _, g, _, _, _ in small] + [pad8(loss_part)], axis=0)
    (dwin,), ((wd1_other,), (gpacks,)) = _mm_tn(dproj, h2, 512, D, _tile(S, 2048), False, "dw_in", mrows=P,
                                              comm=[_Swap([dwd1]), _Spread([gpack])])
    dwin = dwin.reshape(N_DEV, Pb, D)
    wd1_sum = _pair_add(dwd1, wd1_other, "pair_wd1")
    (dg1, du1), ((wd1_got,), (win_other,)) = _ffn_bwd_act(dyb1, wd1, g1, u1, "ffn1_bwd_act", comm=[_Ici([wd1_sum]), _Swap([dwin])])
    win_sum = _pair_add(dwin, win_other, "pair_win")
    (dwg1, dwu1), ((win_got,),) = _ffn_dwgu(h1, dg1, du1, "ffn1_dwgu", comm=[_Ici([win_sum])])
    gu_parts = [dwg1, dwu1]
    gu_other = _comm_only(_Swap(gu_parts), "swap_gu1")
    gu_sums = [_pair_add(p, o, f"pair_gu1_{i}") for i, (p, o) in enumerate(zip(gu_parts, gu_other))]
    (dx0, _, d_ffn1n), (gu_got,) = _ffn_dh(dg1, du1, wg1, wu1, xs, ffn1_norm, dx1, "ffn1_dh", comm=[_Ici(gu_sums)])

    got = dict(ffn2_w_down=ffn2_got[0], ffn2_w_gate=ffn2_got[1], ffn2_w_up=ffn2_got[2], w_out=mix_got[0], w_att_out=mix_got[1],
               w_sg_out=mix_got[2], w_in=win_got, ffn1_w_gate=gu_got[0], ffn1_w_up=gu_got[1], ffn1_w_down=wd1_got)
    moments = dict(ffn1_w_gate=(m_ffn1_w_gate, v_ffn1_w_gate), ffn1_w_up=(m_ffn1_w_up, v_ffn1_w_up),
                   ffn1_w_down=(m_ffn1_w_down, v_ffn1_w_down), w_in=(m_w_in, v_w_in), w_att_out=(m_w_att_out, v_w_att_out),
                   w_sg_out=(m_w_sg_out, v_w_sg_out), w_out=(m_w_out, v_w_out), ffn2_w_gate=(m_ffn2_w_gate, v_ffn2_w_gate),
                   ffn2_w_up=(m_ffn2_w_up, v_ffn2_w_up), ffn2_w_down=(m_ffn2_w_down, v_ffn2_w_down))
    res = {}
    for n in sharded:
        mm, vv = moments[n]
        outs = _adamw(got[n], wloc[n], local(n, mm), local(n, vv), "adamw_" + n)
        res[n] = [back(n, o) for o in outs]

    zero8 = jnp.zeros((8, 128), F32)
    wpack = jnp.concatenate([pad8(rows(w)) for _, _, w, _, _ in small] + [zero8], axis=0)
    mpack = jnp.concatenate([pad8(rows(m)) for _, _, _, m, _ in small] + [zero8], axis=0)
    vpack = jnp.concatenate([pad8(rows(v)) for _, _, _, _, v in small] + [zero8], axis=0)
    packs = _adamw(gpacks, wpack, mpack, vpack, "adamw_small")
    off = 0
    for n, g, w, _, _ in small:
        r = g.shape[0]
        res[n] = [p[off:off + r].reshape(w.shape) for p in packs]
        off += -(-r // 8) * 8
    loss = packs[0][off, 0]
    g_first = _small_allreduce(rows(d_ffn1n), "allreduce_ffn1_norm")
    res["ffn1_norm"] = [p.reshape(ffn1_norm.shape) for p in
                        _adamw(g_first[None], rows(ffn1_norm), rows(m_ffn1_norm), rows(v_ffn1_norm), "adamw_ffn1_norm")]

    order = ["ffn1_norm", "ffn1_w_gate", "ffn1_w_up", "ffn1_w_down", "mix_norm", "w_in", "sg_ln_g", "sg_ln_b", "sg_w", "sg_b",
             "w_att_out", "w_sg_out", "w_out", "ffn2_norm", "ffn2_w_gate", "ffn2_w_up", "ffn2_w_down", "final_norm"]
    return (loss, dx0[None], *[res[n][0] for n in order], *[res[n][1] for n in order], *[res[n][2] for n in order],
            *[res[n][3] for n in order])
```

```python
import math

import jax
import jax.numpy as jnp
from jax import lax
from jax.experimental import pallas as pl
from jax.experimental.pallas import tpu as pltpu

BF = jnp.bfloat16
F32 = jnp.float32
MESH = pl.DeviceIdType.MESH
N_DEV = 8
N_CHIP = 4

HEAD_DIM = 128
HEADS_PER_GROUP = 4
GROUP_W = HEADS_PER_GROUP * HEAD_DIM
DILATIONS = (1, 4, 16)
ATT_W = len(DILATIONS) * GROUP_W
SG_W = 1536
SG_GROUPS = 12
BLK = 128
ROPE_DIM = 32
ROPE_THETA = 500000.0
NORM_EPS = 1e-6
LN_EPS = 1e-5
Q_OFF, K_OFF, V_OFF, U_OFF, VS_OFF, GA_OFF = 0, ATT_W, 2 * ATT_W, 3 * ATT_W, 3 * ATT_W + SG_W, 3 * ATT_W + 2 * SG_W

ADAM_LR, ADAM_B1, ADAM_B2, ADAM_EPS, ADAM_WD, ADAM_STEP = 0.001, 0.9, 0.999, 1e-08, 0.01, 10

VMEM_LIMIT = 56 * 1024 * 1024
NEG = -1e30
ANY = pl.BlockSpec(memory_space=pl.ANY)
EPI_ROWS = 128
ACC_COLS = 512
FFN_PAIR = 2
FFN_ROWS = 1024
PROJ_TK = 1536
R_U, R_VS, R_GA = 0, SG_W, 2 * SG_W


def _once(shape, index_map):
    return pl.BlockSpec(shape, index_map, pipeline_mode=pl.Buffered(1))


def _tile(n, pref):
    t = min(n, pref)
    while n % t:
        t //= 2
    return t


def _nt(a, b):
    return lax.dot_general(a, b, (((1,), (1,)), ((), ())), preferred_element_type=F32)


def _tn(a, b):
    return lax.dot_general(a, b, (((0,), (0,)), ((), ())), preferred_element_type=F32)


def _nn(a, b):
    return jnp.dot(a, b, preferred_element_type=F32)


def _acc_dots(acc_ref, terms, transposed_rhs=False):
    n = acc_ref.shape[1]
    width = min(n, ACC_COLS)
    for c0 in range(0, n, width):
        cols = slice(c0, c0 + width)
        tot = None
        for lhs, rhs in terms:
            part = _nt(lhs, rhs(cols)) if transposed_rhs else _nn(lhs, rhs(cols))
            tot = part if tot is None else tot + part
        acc_ref[:, cols] += tot


def _gauss_cdf(x):
    return 0.5 * (1.0 + lax.erf(x * (2.0 ** -0.5)))


def _gelu_grad(x, cdf):
    return cdf + x * jnp.exp(-0.5 * x * x) * (1.0 / math.sqrt(2.0 * math.pi))


def _place():
    x, y, c = lax.axis_index("x"), lax.axis_index("y"), lax.axis_index("c")
    return x, y, c


def _flip(v, bit):
    return 1 - v if bit else v


class _Gather:
    def __init__(self, shards, mid_frac=1.0, relay_frac=0.5):
        self.arrays = list(shards)
        self.relay_frac = relay_frac
        self.mid_frac = mid_frac
        nw = len(shards)
        self.out_shape = [jax.ShapeDtypeStruct((N_DEV,) + s.shape, s.dtype) for s in shards]
        self.scratch = [pltpu.SemaphoreType.DMA((nw, 7)), pltpu.SemaphoreType.DMA((nw, 7)), pltpu.SemaphoreType.DMA((nw,))]

    def _parts(self, ins, outs, sems):
        x, y, c = _place()
        send, recv, loc = sems
        south = c == 0
        near = (jnp.where(south, x, 1 - x), jnp.where(south, 1 - y, y), c)
        far = (jnp.where(south, 1 - x, x), jnp.where(south, y, 1 - y), c)
        diag = (1 - x, 1 - y, c)

        def copy(k, s, block, to, src=None):
            dst = outs[k].at[4 * block[0] + 2 * block[1] + block[2]]
            return pltpu.make_async_remote_copy(src_ref=dst if src is None else src, dst_ref=dst, send_sem=send.at[k, s],
                                                recv_sem=recv.at[k, s], device_id=to, device_id_type=MESH)

        def first(k):
            me = (x, y, c)
            return [copy(k, 0, me, (x, y, 1 - c), src=ins[k]), copy(k, 1, me, (1 - x, y, c), src=ins[k]),
                    copy(k, 2, me, (x, 1 - y, c), src=ins[k])]

        def local(k):
            return pltpu.make_async_copy(ins[k], outs[k].at[4 * x + 2 * y + c], loc.at[k])

        return x, y, c, near, far, diag, copy, first, local

    def start(self, ins, outs, sems):
        *_, first, local = self._parts(ins, outs, sems)
        for k in range(len(ins)):
            local(k).start()
            for cp in first(k):
                cp.start()

    def relay(self, ins, outs, sems):
        x, y, c, near, far, _, copy, _, _ = self._parts(ins, outs, sems)
        for k in range(len(ins)):
            copy(k, 2 - c, near, (x, y, c)).wait_recv()
            copy(k, 3, near, far).start()
            copy(k, 5 - c, near, (x, y, 1 - c)).start()

    def mid(self, ins, outs, sems):
        x, y, c, _, far, diag, copy, _, _ = self._parts(ins, outs, sems)
        for k in range(len(ins)):
            copy(k, 1 + c, far, (x, y, c)).wait_recv()
            copy(k, 4 + c, far, (x, y, 1 - c)).start()
            copy(k, 3, diag, (x, y, c)).wait_recv()
            copy(k, 6, diag, (x, y, 1 - c)).start()

    def finish(self, ins, outs, sems):
        x, y, c, near, _, _, copy, first, local = self._parts(ins, outs, sems)
        sib = (x, y, 1 - c)
        for k in range(len(ins)):
            copy(k, 0, sib, (x, y, c)).wait_recv()
            copy(k, 4, (1 - x, y, 1 - c), (x, y, c)).wait_recv()
            copy(k, 5, (x, 1 - y, 1 - c), (x, y, c)).wait_recv()
            copy(k, 6, (1 - x, 1 - y, 1 - c), (x, y, c)).wait_recv()
        for k in range(len(ins)):
            for cp in first(k):
                cp.wait_send()
            for s in (3, 4, 5, 6):
                copy(k, s, near, sib).wait_send()
            local(k).wait()


class _Swap:
    def __init__(self, parts):
        self.arrays = list(parts)
        nw = len(parts)
        self.out_shape = [jax.ShapeDtypeStruct((N_CHIP,) + p.shape[1:], p.dtype) for p in parts]
        self.scratch = [pltpu.SemaphoreType.DMA((nw, N_CHIP)), pltpu.SemaphoreType.DMA((nw, N_CHIP))]

    def _copy(self, ins, outs, sems, k, q):
        x, y, c = _place()
        return pltpu.make_async_remote_copy(src_ref=ins[k].at[2 * q + 1 - c], dst_ref=outs[k].at[q], send_sem=sems[0].at[k, q],
                                            recv_sem=sems[1].at[k, q], device_id=(x, y, 1 - c), device_id_type=MESH)

    mid_frac = None

    def start(self, ins, outs, sems):
        for k in range(len(ins)):
            for q in range(N_CHIP):
                self._copy(ins, outs, sems, k, q).start()

    def finish(self, ins, outs, sems):
        for k in range(len(ins)):
            for q in range(N_CHIP):
                self._copy(ins, outs, sems, k, q).wait()


class _Ici:
    mid_frac = None

    def __init__(self, sums):
        self.arrays = list(sums)
        nw = len(sums)
        self.out_shape = [jax.ShapeDtypeStruct(s.shape, s.dtype) for s in sums]
        self.scratch = [pltpu.SemaphoreType.DMA((nw, 3)), pltpu.SemaphoreType.DMA((nw, 3)), pltpu.SemaphoreType.DMA((nw,))]

    def _copies(self, ins, outs, sems, k):
        x, y, c = _place()
        myq = 2 * x + y
        out = []
        for r in range(1, N_CHIP):
            px, py = _flip(x, r & 2), _flip(y, r & 1)
            pq = 2 * px + py
            mk = lambda dst: pltpu.make_async_remote_copy(src_ref=ins[k].at[pq], dst_ref=dst, send_sem=sems[0].at[k, r - 1],
                                                          recv_sem=sems[1].at[k, r - 1], device_id=(px, py, c), device_id_type=MESH)
            out.append((mk(outs[k].at[myq]), mk(outs[k].at[pq])))
        return out, pltpu.make_async_copy(ins[k].at[myq], outs[k].at[myq], sems[2].at[k])

    def start(self, ins, outs, sems):
        for k in range(len(ins)):
            remote, local = self._copies(ins, outs, sems, k)
            local.start()
            for snd, _ in remote:
                snd.start()

    def finish(self, ins, outs, sems):
        for k in range(len(ins)):
            remote, local = self._copies(ins, outs, sems, k)
            for snd, rcv in remote:
                rcv.wait_recv()
                snd.wait_send()
            local.wait()


class _Spread:
    mid_frac = None

    def __init__(self, arrays):
        self.arrays = list(arrays)
        nw = len(arrays)
        self.out_shape = [jax.ShapeDtypeStruct((N_DEV,) + a.shape, a.dtype) for a in arrays]
        self.scratch = [pltpu.SemaphoreType.DMA((nw, 7)), pltpu.SemaphoreType.DMA((nw, 7)), pltpu.SemaphoreType.DMA((nw,))]

    def _copies(self, ins, outs, sems, k):
        x, y, c = _place()
        me = 4 * x + 2 * y + c
        out = []
        for r in range(1, N_DEV):
            px, py, pc = _flip(x, r & 4), _flip(y, r & 2), _flip(c, r & 1)
            peer = 4 * px + 2 * py + pc
            mk = lambda dst: pltpu.make_async_remote_copy(src_ref=ins[k], dst_ref=dst, send_sem=sems[0].at[k, r - 1],
                                                          recv_sem=sems[1].at[k, r - 1], device_id=(px, py, pc), device_id_type=MESH)
            out.append((mk(outs[k].at[me]), mk(outs[k].at[peer])))
        return out, pltpu.make_async_copy(ins[k], outs[k].at[me], sems[2].at[k])

    def start(self, ins, outs, sems):
        for k in range(len(ins)):
            remote, local = self._copies(ins, outs, sems, k)
            local.start()
            for snd, _ in remote:
                snd.start()

    def finish(self, ins, outs, sems):
        for k in range(len(ins)):
            remote, local = self._copies(ins, outs, sems, k)
            for snd, rcv in remote:
                rcv.wait_recv()
                snd.wait_send()
            local.wait()


def _call(body, *, grid, in_specs, out_specs, out_shape, name, args, scratch=(), comm=()):
    comm = list(comm)
    n_in, n_out, n_scr = len(in_specs), len(out_specs), len(scratch)
    total = math.prod(grid) if grid else 1

    def wrapped(*refs):
        p = n_in
        cin = []
        for cm in comm:
            cin.append(refs[p:p + len(cm.arrays)])
            p += len(cm.arrays)
        own_out = refs[p:p + n_out]
        p += n_out
        cout = []
        for cm in comm:
            cout.append(refs[p:p + len(cm.arrays)])
            p += len(cm.arrays)
        own_scr = refs[p:p + n_scr]
        p += n_scr
        csem = []
        for cm in comm:
            csem.append(refs[p:p + len(cm.scratch)])
            p += len(cm.scratch)
        step = 0
        for axis, g in enumerate(grid):
            step = step * g + pl.program_id(axis)

        def at(when, what):
            if total == 1:
                what()
            else:
                pl.when(step == when)(what)

        def starts():
            for cm, i, o, s in zip(comm, cin, cout, csem):
                cm.start(i, o, s)

        def finishes():
            for cm, i, o, s in zip(comm, cin, cout, csem):
                cm.finish(i, o, s)

        if comm:
            at(0, starts)
        if body is not None:
            body(*refs[:n_in], *own_out, *own_scr)
        for cm, i, o, s in zip(comm, cin, cout, csem):
            if cm.mid_frac is not None:
                at(min(total - 1, int(total * cm.relay_frac)), lambda cm=cm, i=i, o=o, s=s: cm.relay(i, o, s))
                at(min(total - 1, int(total * cm.mid_frac)), lambda cm=cm, i=i, o=o, s=s: cm.mid(i, o, s))
        if comm:
            at(total - 1, finishes)

    kw = dict(grid=tuple(grid)) if grid else {}
    outs = pl.pallas_call(
        wrapped, name=name, **kw,
        in_specs=list(in_specs) + [ANY for cm in comm for _ in cm.arrays],
        out_specs=list(out_specs) + [ANY for cm in comm for _ in cm.arrays],
        out_shape=list(out_shape) + [s for cm in comm for s in cm.out_shape],
        scratch_shapes=list(scratch) + [s for cm in comm for s in cm.scratch],
        compiler_params=pltpu.CompilerParams(dimension_semantics=("arbitrary",) * len(grid), vmem_limit_bytes=VMEM_LIMIT),
    )(*args, *[a for cm in comm for a in cm.arrays])
    own, p, per = list(outs[:n_out]), n_out, []
    for cm in comm:
        per.append(list(outs[p:p + len(cm.arrays)]))
        p += len(cm.arrays)
    return own, per


def _comm_only(cm, name):
    return _call(None, grid=(), in_specs=[], out_specs=[], out_shape=[], name=name, args=[], comm=[cm])[1][0]


def _rms_fwd(x, g, name, comm=()):
    S, D = x.shape
    tm = _tile(S, 512)

    def body(x_ref, g_ref, o_ref):
        xv = x_ref[...]
        r = lax.rsqrt(jnp.mean(xv * xv, axis=-1, keepdims=True) + NORM_EPS)
        o_ref[...] = (xv * r * g_ref[...]).astype(BF)

    return _call(body, grid=(S // tm,), name=name, args=[x, g], comm=comm,
                 in_specs=[pl.BlockSpec((tm, D), lambda i: (i, 0)), pl.BlockSpec((1, D), lambda i: (0, 0))],
                 out_specs=[pl.BlockSpec((tm, D), lambda i: (i, 0))], out_shape=[jax.ShapeDtypeStruct((S, D), BF)])


def _ffn_up(h, wg, wu, name, comm=()):
    S, D = h.shape
    nb, Fb, _ = wg.shape
    tm = _tile(S, FFN_ROWS)

    def body(h_ref, wg_ref, wu_ref, g_ref, u_ref, a_ref):
        hv = h_ref[...]
        g = _nt(hv, wg_ref[0])
        u = _nt(hv, wu_ref[0])
        g_ref[0] = g.astype(BF)
        u_ref[0] = u.astype(BF)
        a_ref[0] = (g * jax.nn.sigmoid(g) * u).astype(BF)

    act = pl.BlockSpec((1, tm, Fb), lambda j, i: (j, i, 0))
    w = pl.BlockSpec((1, Fb, D), lambda j, i: (j, 0, 0))
    shp = jax.ShapeDtypeStruct((nb, S, Fb), BF)
    return _call(body, grid=(nb, S // tm), name=name, args=[h, wg, wu], comm=comm,
                 in_specs=[pl.BlockSpec((tm, D), lambda j, i: (i, 0)), w, w], out_specs=[act, act, act], out_shape=[shp, shp, shp])


def _ffn_gate(h, wg, name, comm=()):
    S, D = h.shape
    nb, Fb, _ = wg.shape
    tm = _tile(S, FFN_ROWS)

    def body(h_ref, wg_ref, g_ref):
        g_ref[0] = _nt(h_ref[...], wg_ref[0]).astype(BF)

    act = pl.BlockSpec((1, tm, Fb), lambda j, i: (j, i, 0))
    return _call(body, grid=(nb, S // tm), name=name, args=[h, wg], comm=comm,
                 in_specs=[pl.BlockSpec((tm, D), lambda j, i: (i, 0)), pl.BlockSpec((1, Fb, D), lambda j, i: (j, 0, 0))],
                 out_specs=[act], out_shape=[jax.ShapeDtypeStruct((nb, S, Fb), BF)])


def _ffn_up_act(h, wu, g, name, comm=()):
    S, D = h.shape
    nb, Fb, _ = wu.shape
    tm = _tile(S, FFN_ROWS)

    def body(h_ref, wu_ref, g_ref, u_ref, a_ref):
        u = _nt(h_ref[...], wu_ref[0])
        gv = g_ref[0].astype(F32)
        u_ref[0] = u.astype(BF)
        a_ref[0] = (gv * jax.nn.sigmoid(gv) * u).astype(BF)

    act = pl.BlockSpec((1, tm, Fb), lambda j, i: (j, i, 0))
    shp = jax.ShapeDtypeStruct((nb, S, Fb), BF)
    return _call(body, grid=(nb, S // tm), name=name, args=[h, wu, g], comm=comm,
                 in_specs=[pl.BlockSpec((tm, D), lambda j, i: (i, 0)), pl.BlockSpec((1, Fb, D), lambda j, i: (j, 0, 0)), act],
                 out_specs=[act, act], out_shape=[shp, shp])


def _ffn_down_norm(a, wd, x, gn, name, comm=()):
    nb, S, Fb = a.shape
    D = wd.shape[2]
    tm = _tile(S, 512)

    nj = nb // FFN_PAIR

    def body(a_ref, wd_ref, x_ref, gn_ref, xo_ref, hn_ref, acc_ref):
        j = pl.program_id(1)

        @pl.when(j == 0)
        def _():
            acc_ref[...] = jnp.zeros_like(acc_ref)

        _acc_dots(acc_ref, [(a_ref[b], lambda cols, b=b: wd_ref[b, :, cols]) for b in range(FFN_PAIR)])

        @pl.when(j == nj - 1)
        def _():
            def chunk(t, carry):
                rows = pl.ds(pl.multiple_of(t * EPI_ROWS, EPI_ROWS), EPI_ROWS)
                xo = x_ref[rows, :] + 0.5 * acc_ref[rows, :]
                r = lax.rsqrt(jnp.mean(xo * xo, axis=-1, keepdims=True) + NORM_EPS)
                xo_ref[rows, :] = xo
                hn_ref[rows, :] = (xo * r * gn_ref[...]).astype(BF)
                return carry

            lax.fori_loop(0, tm // EPI_ROWS, chunk, 0)

    row = pl.BlockSpec((tm, D), lambda i, j: (i, 0))
    return _call(body, grid=(S // tm, nj), name=name, args=[a, wd, x, gn], comm=comm,
                 in_specs=[pl.BlockSpec((FFN_PAIR, tm, Fb), lambda i, j: (j, i, 0)), pl.BlockSpec((FFN_PAIR, Fb, D), lambda i, j: (j, 0, 0)),
                           row, pl.BlockSpec((1, D), lambda i, j: (0, 0))],
                 out_specs=[row, row], out_shape=[jax.ShapeDtypeStruct((S, D), F32), jax.ShapeDtypeStruct((S, D), BF)],
                 scratch=[pltpu.VMEM((tm, D), F32)])


def _ffn_down_loss(a, wd, x, gf, tgt, name):
    nb, S, Fb = a.shape
    D = wd.shape[2]
    tm = _tile(S, 512)

    nj = nb // FFN_PAIR

    def body(a_ref, wd_ref, x_ref, gf_ref, t_ref, dx_ref, dxb_ref, dgf_ref, loss_ref, acc_ref):
        i, j = pl.program_id(0), pl.program_id(1)

        @pl.when(j == 0)
        def _():
            acc_ref[...] = jnp.zeros_like(acc_ref)

        _acc_dots(acc_ref, [(a_ref[b], lambda cols, b=b: wd_ref[b, :, cols]) for b in range(FFN_PAIR)])

        @pl.when((j == nj - 1) & (i == 0))
        def _():
            dgf_ref[...] = jnp.zeros_like(dgf_ref)
            loss_ref[...] = jnp.zeros_like(loss_ref)

        @pl.when(j == nj - 1)
        def _():
            def chunk(t, carry):
                rows = pl.ds(pl.multiple_of(t * EPI_ROWS, EPI_ROWS), EPI_ROWS)
                xo = x_ref[rows, :] + 0.5 * acc_ref[rows, :]
                r = lax.rsqrt(jnp.mean(xo * xo, axis=-1, keepdims=True) + NORM_EPS)
                xh = xo * r
                gf = gf_ref[...]
                e = xh * gf - t_ref[rows, :]
                loss_ref[...] += jnp.sum(jnp.mean(e * e, axis=-1, keepdims=True), axis=0, keepdims=True) * 0.5
                dy = e * (1.0 / D)
                dgf_ref[...] += jnp.sum(dy * xh, axis=0, keepdims=True)
                dxh = dy * gf
                dx = r * (dxh - xh * jnp.mean(dxh * xh, axis=-1, keepdims=True))
                dx_ref[rows, :] = dx
                dxb_ref[rows, :] = (0.5 * dx).astype(BF)
                return carry

            lax.fori_loop(0, tm // EPI_ROWS, chunk, 0)

    row = pl.BlockSpec((tm, D), lambda i, j: (i, 0))
    once = row
    vec = pl.BlockSpec((1, D), lambda i, j: (0, 0))
    return _call(body, grid=(S // tm, nj), name=name, args=[a, wd, x, gf, tgt],
                 in_specs=[pl.BlockSpec((FFN_PAIR, tm, Fb), lambda i, j: (j, i, 0)), pl.BlockSpec((FFN_PAIR, Fb, D), lambda i, j: (j, 0, 0)),
                           once, vec, once],
                 out_specs=[row, row, vec, pl.BlockSpec((1, 128), lambda i, j: (0, 0))],
                 out_shape=[jax.ShapeDtypeStruct((S, D), F32), jax.ShapeDtypeStruct((S, D), BF), jax.ShapeDtypeStruct((1, D), F32),
                            jax.ShapeDtypeStruct((1, 128), F32)],
                 scratch=[pltpu.VMEM((tm, D), F32)])[0]


def _ffn_bwd_act(dyb, wd, g, u, name, comm=()):
    S, D = dyb.shape
    nb, Fb, _ = wd.shape
    tm = _tile(S, FFN_ROWS)

    def body(dy_ref, wd_ref, g_ref, u_ref, dg_ref, du_ref):
        rows = pl.ds(pl.multiple_of(pl.program_id(1) * tm, tm), tm)
        da = _nt(dy_ref[rows, :], wd_ref[0])
        gv = g_ref[0].astype(F32)
        uv = u_ref[0].astype(F32)
        sg = jax.nn.sigmoid(gv)
        du_ref[0] = (da * gv * sg).astype(BF)
        dg_ref[0] = (da * uv * sg * (1.0 + gv * (1.0 - sg))).astype(BF)

    act = pl.BlockSpec((1, tm, Fb), lambda j, i: (j, i, 0))
    shp = jax.ShapeDtypeStruct((nb, S, Fb), BF)
    return _call(body, grid=(nb, S // tm), name=name, args=[dyb, wd, g, u], comm=comm,
                 in_specs=[pl.BlockSpec((S, D), lambda j, i: (0, 0)), pl.BlockSpec((1, Fb, D), lambda j, i: (j, 0, 0)), act, act],
                 out_specs=[act, act], out_shape=[shp, shp])


def _ffn_dwd(a, dyb, name, comm=()):
    nb, S, Fb = a.shape
    D = dyb.shape[1]
    ts = S
    ns = S // ts

    def body(a_ref, dy_ref, o_ref, acc_ref):
        s = pl.program_id(1)

        @pl.when(s == 0)
        def _():
            acc_ref[...] = jnp.zeros_like(acc_ref)

        acc_ref[...] += _tn(a_ref[0], dy_ref[...])

        @pl.when(s == ns - 1)
        def _():
            o_ref[0] = acc_ref[...].astype(BF)

    return _call(body, grid=(nb, ns), name=name, args=[a, dyb], comm=comm,
                 in_specs=[pl.BlockSpec((1, ts, Fb), lambda j, s: (j, s, 0)), pl.BlockSpec((ts, D), lambda j, s: (s, 0))],
                 out_specs=[pl.BlockSpec((1, Fb, D), lambda j, s: (j, 0, 0))], out_shape=[jax.ShapeDtypeStruct((nb, Fb, D), BF)],
                 scratch=[pltpu.VMEM((Fb, D), F32)])


def _ffn_dwgu(h, dg, du, name, comm=()):
    S, D = h.shape
    nb, _, Fb = dg.shape
    ts = _tile(S, FFN_ROWS)
    ns = S // ts

    def body(h_ref, dg_ref, du_ref, og_ref, ou_ref, accg_ref, accu_ref):
        s = pl.program_id(1)

        @pl.when(s == 0)
        def _():
            accg_ref[...] = jnp.zeros_like(accg_ref)
            accu_ref[...] = jnp.zeros_like(accu_ref)

        hv = h_ref[...]
        accg_ref[...] += _tn(dg_ref[0], hv)
        accu_ref[...] += _tn(du_ref[0], hv)

        @pl.when(s == ns - 1)
        def _():
            og_ref[0] = accg_ref[...].astype(BF)
            ou_ref[0] = accu_ref[...].astype(BF)

    act = pl.BlockSpec((1, ts, Fb), lambda j, s: (j, s, 0))
    out = pl.BlockSpec((1, Fb, D), lambda j, s: (j, 0, 0))
    shp = jax.ShapeDtypeStruct((nb, Fb, D), BF)
    return _call(body, grid=(nb, ns), name=name, args=[h, dg, du], comm=comm,
                 in_specs=[pl.BlockSpec((ts, D), lambda j, s: (s, 0)), act, act], out_specs=[out, out], out_shape=[shp, shp],
                 scratch=[pltpu.VMEM((Fb, D), F32), pltpu.VMEM((Fb, D), F32)])


def _dh_rms_bwd(pairs, blocked, tk, x, gn, dxo, out_scale, name, comm=(), tail=None):
    S, D = x.shape
    nk = pairs[0][0].shape[0] if blocked else pairs[0][0].shape[1] // tk
    tm = _tile(S, 512)
    npair = len(pairs)
    assert blocked or (npair == 1 and tail is not None and nk >= 2)
    nin = 2 * npair + (0 if blocked else 1)

    def body(*refs):
        ins = refs[:nin]
        x_ref, gn_ref, dxo_ref, dx_ref, dxb_ref, dgn_ref, acc_ref = refs[nin:]
        i, k = pl.program_id(0), pl.program_id(1)

        @pl.when(k == 0)
        def _():
            acc_ref[...] = jnp.zeros_like(acc_ref)

        if blocked:
            _acc_dots(acc_ref, [(ins[2 * p][0], lambda cols, r=ins[2 * p + 1]: r[0, :, cols]) for p in range(npair)])
        else:
            @pl.when(k < nk - 1)
            def _():
                _acc_dots(acc_ref, [(ins[0][...], lambda cols: ins[1][:, cols])])

            @pl.when(k == nk - 1)
            def _():
                _acc_dots(acc_ref, [(ins[0][...], lambda cols: ins[2][:, cols])])

        @pl.when((k == nk - 1) & (i == 0))
        def _():
            dgn_ref[...] = jnp.zeros_like(dgn_ref)

        @pl.when(k == nk - 1)
        def _():
            def chunk(t, carry):
                rows = pl.ds(pl.multiple_of(t * EPI_ROWS, EPI_ROWS), EPI_ROWS)
                xv = x_ref[rows, :]
                r = lax.rsqrt(jnp.mean(xv * xv, axis=-1, keepdims=True) + NORM_EPS)
                xh = xv * r
                dh = acc_ref[rows, :]
                dgn_ref[...] += jnp.sum(dh * xh, axis=0, keepdims=True)
                dxh = dh * gn_ref[...]
                dx = dxo_ref[rows, :] + r * (dxh - xh * jnp.mean(dxh * xh, axis=-1, keepdims=True))
                dx_ref[rows, :] = dx
                dxb_ref[rows, :] = (out_scale * dx).astype(BF)
                return carry

            lax.fori_loop(0, tm // EPI_ROWS, chunk, 0)

    if blocked:
        mats = [pl.BlockSpec((1, tm, tk), lambda i, k: (k, i, 0)), pl.BlockSpec((1, tk, D), lambda i, k: (k, 0, 0))] * npair
        flat = [t for pr in pairs for t in pr]
    else:
        mats = [pl.BlockSpec((tm, tk), lambda i, k: (i, k)), pl.BlockSpec((tk, D), lambda i, k: (jnp.minimum(k, nk - 2), 0)),
                pl.BlockSpec((tk, D), lambda i, k: (0, 0))]
        flat = [*pairs[0], tail]
    row = pl.BlockSpec((tm, D), lambda i, k: (i, 0))
    once = row
    vec = pl.BlockSpec((1, D), lambda i, k: (0, 0))
    return _call(body, grid=(S // tm, nk), name=name, args=[*flat, x, gn, dxo], comm=comm,
                 in_specs=mats + [once, vec, once], out_specs=[row, row, vec],
                 out_shape=[jax.ShapeDtypeStruct((S, D), F32), jax.ShapeDtypeStruct((S, D), BF), jax.ShapeDtypeStruct((1, D), F32)],
                 scratch=[pltpu.VMEM((tm, D), F32)])


def _ffn_dh(dg, du, wgT, wuT, x, gn, dxo, name, comm=()):
    S, D = x.shape
    nb, _, Fb = dg.shape
    tm = _tile(S, 512)
    ni, er = S // tm, tm // nb
    assert er % 16 == 0

    def body(dg_ref, wg_ref, du_ref, wu_ref, x_ref, gn_ref, dxo_ref, dx_ref, dxb_ref, dgn_ref, acc_ref):
        i, k = pl.program_id(0), pl.program_id(1)
        slot = i % 2

        @pl.when((i == 0) & (k == 0))
        def _():
            acc_ref[...] = jnp.zeros_like(acc_ref)
            dgn_ref[...] = jnp.zeros_like(dgn_ref)

        @pl.when((i > 0) & (k == 0))
        def _():
            acc_ref[slot] = jnp.zeros((tm, D), F32)

        def finish_rows():
            rows = pl.ds(pl.multiple_of(k * er, er), er)
            xv = x_ref[rows, :]
            r = lax.rsqrt(jnp.mean(xv * xv, axis=-1, keepdims=True) + NORM_EPS)
            xh = xv * r
            dh = acc_ref[1 - slot, rows, :]
            dgn_ref[...] += jnp.where(i > 0, jnp.sum(dh * xh, axis=0, keepdims=True), 0.0)
            dxh = dh * gn_ref[...]
            dx = dxo_ref[rows, :] + r * (dxh - xh * jnp.mean(dxh * xh, axis=-1, keepdims=True))
            dx_ref[rows, :] = dx
            dxb_ref[rows, :] = dx.astype(BF)

        @pl.when(i < ni)
        def _():
            _acc_dots(acc_ref.at[slot], [(dg_ref[0], lambda cols: wg_ref[0, :, cols]), (du_ref[0], lambda cols: wu_ref[0, :, cols])])
            finish_rows()

        @pl.when(i == ni)
        def _():
            finish_rows()

    last = lambda i, k: jnp.where(i == ni, nb - 1, k)
    act = pl.BlockSpec((1, tm, Fb), lambda i, k: (last(i, k), jnp.minimum(i, ni - 1), 0))
    w = pl.BlockSpec((1, Fb, D), lambda i, k: (last(i, k), 0, 0))
    prev = pl.BlockSpec((tm, D), lambda i, k: (jnp.maximum(i - 1, 0), 0))
    vec = pl.BlockSpec((1, D), lambda i, k: (0, 0))
    return _call(body, grid=(ni + 1, nb), name=name, args=[dg, wgT, du, wuT, x, gn, dxo], comm=comm,
                 in_specs=[act, w, act, w, prev, vec, prev], out_specs=[prev, prev, vec],
                 out_shape=[jax.ShapeDtypeStruct((S, D), F32), jax.ShapeDtypeStruct((S, D), BF), jax.ShapeDtypeStruct((1, D), F32)],
                 scratch=[pltpu.VMEM((2, tm, D), F32)])


def _proj_split(a, bT, tm, tn, split, name, comm=()):
    M, K = a.shape
    N = bT.shape[0]
    n_first = split // tn

    def body(a_ref, b_ref, first_ref, rest_ref):
        n = pl.program_id(1)
        y = _nt(a_ref[...], b_ref[...])

        @pl.when(n < n_first)
        def _():
            first_ref[...] = y

        @pl.when(n >= n_first)
        def _():
            rest_ref[...] = y.astype(BF)

    return _call(body, grid=(M // tm, N // tn), name=name, args=[a, bT], comm=comm,
                 in_specs=[pl.BlockSpec((tm, K), lambda i, n: (i, 0)), pl.BlockSpec((tn, K), lambda i, n: (n, 0))],
                 out_specs=[pl.BlockSpec((tm, tn), lambda i, n: (i, jnp.minimum(n, n_first - 1))),
                            pl.BlockSpec((tm, tn), lambda i, n: (i, jnp.maximum(n - n_first, 0)))],
                 out_shape=[jax.ShapeDtypeStruct((M, split), F32), jax.ShapeDtypeStruct((M, N - split), BF)])


def _mm_tn(a, b, tm, tn, ts, blocked, name, comm=(), mrows=None):
    S, M = a.shape[0], (a.shape[1] if mrows is None else mrows)
    N = b.shape[1]
    ns = S // ts
    per_tile = tn // blocked if blocked else 0

    def body(a_ref, b_ref, o_ref, acc_ref):
        s = pl.program_id(2)

        @pl.when(s == 0)
        def _():
            acc_ref[...] = jnp.zeros_like(acc_ref)

        acc_ref[...] += _tn(a_ref[...], b_ref[...])

        @pl.when(s == ns - 1)
        def _():
            if blocked:
                for t in range(per_tile):
                    o_ref[t] = acc_ref[:, t * blocked:(t + 1) * blocked].astype(BF)
            else:
                o_ref[...] = acc_ref[...].astype(BF)

    if blocked:
        ospec = pl.BlockSpec((per_tile, tm, blocked), lambda i, n, s: (n, i, 0))
        oshape = jax.ShapeDtypeStruct((N // blocked, M, blocked), BF)
    else:
        ospec = pl.BlockSpec((tm, tn), lambda i, n, s: (i, n))
        oshape = jax.ShapeDtypeStruct((M, N), BF)
    return _call(body, grid=(M // tm, N // tn, ns), name=name, args=[a, b], comm=comm,
                 in_specs=[pl.BlockSpec((ts, tm), lambda i, n, s: (s, i)), pl.BlockSpec((ts, tn), lambda i, n, s: (s, n))],
                 out_specs=[ospec], out_shape=[oshape], scratch=[pltpu.VMEM((tm, tn), F32)])


def _rope_tables(S):
    half = ROPE_DIM // 2
    inv_freq = ROPE_THETA ** (-jnp.arange(0, ROPE_DIM, 2, dtype=F32) / ROPE_DIM)
    ang = jnp.arange(S, dtype=F32)[:, None] * inv_freq[None, :]
    cos, sin = jnp.cos(ang), jnp.sin(ang)
    zeros = jnp.zeros((S, HEAD_DIM - ROPE_DIM), F32)
    c = jnp.concatenate([cos, cos, jnp.ones((S, HEAD_DIM - ROPE_DIM), F32)], axis=1)
    sm = jnp.concatenate([-sin, jnp.zeros((S, half), F32), zeros], axis=1)
    sp = jnp.concatenate([jnp.zeros((S, half), F32), sin, zeros], axis=1)
    return c, sm, sp


def _rope(t, c, sm, sp):
    return t * c + pltpu.roll(t, HEAD_DIM - ROPE_DIM // 2, 1) * sm + pltpu.roll(t, ROPE_DIM // 2, 1) * sp


def _rope_t(dy, c, sm, sp):
    return dy * c + pltpu.roll(dy * sm, ROPE_DIM // 2, 1) + pltpu.roll(dy * sp, HEAD_DIM - ROPE_DIM // 2, 1)


def _att_mask(i):
    qi = lax.broadcasted_iota(jnp.int32, (BLK, 2 * BLK), 0)
    kj = lax.broadcasted_iota(jnp.int32, (BLK, 2 * BLK), 1)
    diff = qi + BLK - kj
    first_key = jnp.where(i > 0, 0, BLK)
    return (diff >= 0) & (diff <= BLK) & (kj >= first_key)


def _res_rows(r, i, n, d):
    if d == 1:
        return pl.ds(pl.multiple_of(i * n, n), n)
    return pl.ds(r + i * (n * d), n, stride=d)


def _att_specs(S, gi):
    def sect(off):
        base = (off + gi * GROUP_W) // HEAD_DIM
        return _once((S, HEAD_DIM), lambda hh: (0, base + hh))

    tab = pl.BlockSpec((S, HEAD_DIM), lambda hh: (0, 0))
    head = pl.BlockSpec((S, HEAD_DIM), lambda hh: (0, hh))
    return sect, tab, head


def _each_residue(d, fn):
    if d == 1:
        fn(0)
    else:
        lax.fori_loop(0, d, lambda r, carry: (fn(r), carry)[1], 0)


def _att_fwd(qkv, tabs, gi, d, name, comm=()):
    S = qkv.shape[0]
    L = S // d
    sect, tab, head = _att_specs(S, gi)
    nblk = L // BLK
    scale = HEAD_DIM ** -0.5

    def body(q_ref, k_ref, v_ref, c_ref, sm_ref, sp_ref, o_ref, lse_ref, qr, kp, vp):
        kp[pl.ds(0, BLK), :] = jnp.zeros((BLK, HEAD_DIM), BF)
        vp[pl.ds(0, BLK), :] = jnp.zeros((BLK, HEAD_DIM), BF)

        def residue(r):
            res = _res_rows(r, 0, L, d)
            c, sm, sp = c_ref[res, :], sm_ref[res, :], sp_ref[res, :]
            qr[...] = _rope(q_ref[res, :], c, sm, sp).astype(BF)
            kp[pl.ds(BLK, L), :] = _rope(k_ref[res, :], c, sm, sp).astype(BF)
            vp[pl.ds(BLK, L), :] = v_ref[res, :].astype(BF)

            def blk(i, carry):
                r0 = pl.multiple_of(i * BLK, BLK)
                s = _nt(qr[pl.ds(r0, BLK), :], kp[pl.ds(r0, 2 * BLK), :]) * scale
                s = jnp.where(_att_mask(i), s, NEG)
                m = jnp.max(s, axis=-1, keepdims=True)
                p = jnp.exp(s - m)
                l = jnp.sum(p, axis=-1, keepdims=True)
                out = _res_rows(r, i, BLK, d)
                o_ref[out, :] = _nn(p.astype(BF), vp[pl.ds(r0, 2 * BLK), :]) / l
                lse_ref[out, :] = jnp.broadcast_to(m + jnp.log(l), (BLK, HEAD_DIM))
                return carry

            lax.fori_loop(0, nblk, blk, 0, unroll=min(4, nblk))

        _each_residue(d, residue)

    shp = jax.ShapeDtypeStruct((S, GROUP_W), F32)
    return _call(body, grid=(HEADS_PER_GROUP,), name=name, args=[qkv, qkv, qkv, *tabs], comm=comm,
                 in_specs=[sect(Q_OFF), sect(K_OFF), sect(V_OFF), tab, tab, tab], out_specs=[head, head], out_shape=[shp, shp],
                 scratch=[pltpu.VMEM((L, HEAD_DIM), BF), pltpu.VMEM((L + BLK, HEAD_DIM), BF), pltpu.VMEM((L + BLK, HEAD_DIM), BF)])


def _att_combine(os, lses, name):
    S = os[0].shape[0]
    tm = _tile(S, 512)

    def body(o0, o1, o2, l0, l1, l2, oa_ref, lse_ref):
        a, b, c = l0[...], l1[...], l2[...]
        mx = jnp.maximum(jnp.maximum(a, b), c)
        wa, wb, wc = jnp.exp(a - mx), jnp.exp(b - mx), jnp.exp(c - mx)
        den = wa + wb + wc
        oa_ref[...] = ((wa * o0[...] + wb * o1[...] + wc * o2[...]) / den).astype(BF)
        lse_ref[...] = mx + jnp.log(den)

    row = pl.BlockSpec((tm, GROUP_W), lambda i: (i, 0))
    return _call(body, grid=(S // tm,), name=name, args=[*os, *lses], in_specs=[row] * 6, out_specs=[row, row],
                 out_shape=[jax.ShapeDtypeStruct((S, GROUP_W), BF), jax.ShapeDtypeStruct((S, GROUP_W), F32)])[0]


def _att_bwd(qkv, tabs, do, lse, dvec, gi, d, name, comm=()):
    S = qkv.shape[0]
    L = S // d
    sect, tab, head = _att_specs(S, gi)
    stat = _once((S, HEAD_DIM), lambda hh: (0, hh))
    nblk = L // BLK
    scale = HEAD_DIM ** -0.5

    def body(q_ref, k_ref, v_ref, c_ref, sm_ref, sp_ref, do_ref, lse_ref, dv_ref, dq_out, dk_out, dv_out, qr, kp, vp, dkp, dvp):
        kp[pl.ds(0, BLK), :] = jnp.zeros((BLK, HEAD_DIM), BF)
        vp[pl.ds(0, BLK), :] = jnp.zeros((BLK, HEAD_DIM), BF)

        def residue(r):
            res = _res_rows(r, 0, L, d)
            c, sm, sp = c_ref[res, :], sm_ref[res, :], sp_ref[res, :]
            qr[...] = _rope(q_ref[res, :], c, sm, sp).astype(BF)
            kp[pl.ds(BLK, L), :] = _rope(k_ref[res, :], c, sm, sp).astype(BF)
            vp[pl.ds(BLK, L), :] = v_ref[res, :].astype(BF)
            dkp[...] = jnp.zeros_like(dkp)
            dvp[...] = jnp.zeros_like(dvp)

            def blk(i, carry):
                r0 = pl.multiple_of(i * BLK, BLK)
                rows, win, pos = pl.ds(r0, BLK), pl.ds(r0, 2 * BLK), _res_rows(r, i, BLK, d)
                q, kw, vw, dob = qr[rows, :], kp[win, :], vp[win, :], do_ref[pos, :].astype(BF)
                s = jnp.where(_att_mask(i), _nt(q, kw) * scale, NEG)
                p = jnp.exp(s - lse_ref[pos, :][:, :1])
                ds = p * (_nt(dob, vw) - dv_ref[pos, :][:, :1]) * scale
                dsb = ds.astype(BF)
                dq_out[pos, :] = _rope_t(_nn(dsb, kw), c_ref[pos, :], sm_ref[pos, :], sp_ref[pos, :])
                dkp[win, :] += _tn(dsb, q)
                dvp[win, :] += _tn(p.astype(BF), dob)
                return carry

            lax.fori_loop(0, nblk, blk, 0, unroll=2)
            dk_out[res, :] = _rope_t(dkp[pl.ds(BLK, L), :], c, sm, sp)
            dv_out[res, :] = dvp[pl.ds(BLK, L), :]

        _each_residue(d, residue)

    shp = jax.ShapeDtypeStruct((S, GROUP_W), F32)
    return _call(body, grid=(HEADS_PER_GROUP,), name=name, args=[qkv, qkv, qkv, *tabs, do, lse, dvec], comm=comm,
                 in_specs=[sect(Q_OFF), sect(K_OFF), sect(V_OFF), tab, tab, tab, stat, stat, stat],
                 out_specs=[head, head, head], out_shape=[shp, shp, shp],
                 scratch=[pltpu.VMEM((L, HEAD_DIM), BF), pltpu.VMEM((L + BLK, HEAD_DIM), BF), pltpu.VMEM((L + BLK, HEAD_DIM), BF),
                          pltpu.VMEM((L + BLK, HEAD_DIM), F32), pltpu.VMEM((L + BLK, HEAD_DIM), F32)])


def _sg_parts(u_ref, vs_ref, g_ref, b_ref):
    uv = u_ref[...].astype(F32)
    vv = vs_ref[...].astype(F32)
    cv = _gauss_cdf(vv)
    vg = vv * cv
    mu = jnp.mean(vg, axis=-1, keepdims=True)
    vc = vg - mu
    rs = lax.rsqrt(jnp.mean(vc * vc, axis=-1, keepdims=True) + LN_EPS)
    y = vc * rs
    return uv, vv, cv, rs, y, y * g_ref[...] + b_ref[...]


def _sg_wmask():
    t = lax.broadcasted_iota(jnp.int32, (BLK, BLK), 0)
    s = lax.broadcasted_iota(jnp.int32, (BLK, BLK), 1)
    return s <= t


def _sg_fwd(proj, sgw, sgbT, lng, lnb, name):
    S, P = proj.shape

    def body(u_ref, vs_ref, w_ref, bt_ref, g_ref, b_ref, z_ref):
        uv, _, _, _, _, vln = _sg_parts(u_ref, vs_ref, g_ref, b_ref)
        ug = uv * _gauss_cdf(uv)
        vb = vln.astype(BF)
        mask = _sg_wmask()
        bt = bt_ref[...]
        for g in range(SG_GROUPS):
            cols = slice(g * BLK, (g + 1) * BLK)
            w = jnp.where(mask, w_ref[g], 0.0).astype(BF)
            sp = _nn(w, vb[:, cols]) + bt[:, g:g + 1]
            z_ref[:, cols] = (ug[:, cols] * sp).astype(BF)

    tile = lambda off: pl.BlockSpec((BLK, SG_W), lambda i: (i, off // SG_W))
    full = lambda shape: pl.BlockSpec(shape, lambda i: (0,) * len(shape))
    return _call(body, grid=(S // BLK,), name=name, args=[proj, proj, sgw, sgbT, lng, lnb],
                 in_specs=[tile(R_U), tile(R_VS), full((SG_GROUPS, BLK, BLK)), full((BLK, BLK)), full((1, SG_W)), full((1, SG_W))],
                 out_specs=[pl.BlockSpec((BLK, SG_W), lambda i: (i, 0))], out_shape=[jax.ShapeDtypeStruct((S, SG_W), BF)])[0][0]


def _sg_bwd(proj, dz, sgw, sgbT, lng, lnb, name):
    S, P = proj.shape

    def body(u_ref, vs_ref, dz_ref, w_ref, bt_ref, g_ref, b_ref, du_ref, dvs_ref, dw_ref, dbt_ref, dg_ref, db_ref, dvln):
        @pl.when(pl.program_id(0) == 0)
        def _():
            dw_ref[...] = jnp.zeros_like(dw_ref)
            dbt_ref[...] = jnp.zeros_like(dbt_ref)
            dg_ref[...] = jnp.zeros_like(dg_ref)
            db_ref[...] = jnp.zeros_like(db_ref)

        uv, vv, cv, rs, y, vln = _sg_parts(u_ref, vs_ref, g_ref, b_ref)
        cu = _gauss_cdf(uv)
        ug = uv * cu
        dug = _gelu_grad(uv, cu)
        vb = vln.astype(BF)
        dzv = dz_ref[...].astype(F32)
        dsp = dzv * ug
        dspb = dsp.astype(BF)
        mask = _sg_wmask()
        bt = bt_ref[...]
        lane = lax.broadcasted_iota(jnp.int32, (BLK, BLK), 1)
        dbt = jnp.zeros((BLK, BLK), F32)
        for g in range(SG_GROUPS):
            cols = slice(g * BLK, (g + 1) * BLK)
            w = jnp.where(mask, w_ref[g], 0.0).astype(BF)
            sp = _nn(w, vb[:, cols]) + bt[:, g:g + 1]
            du_ref[:, cols] = (dzv[:, cols] * sp * dug[:, cols]).astype(BF)
            dw_ref[g] += jnp.where(mask, _nt(dspb[:, cols], vb[:, cols]), 0.0)
            dbt = dbt + jnp.where(lane == g, jnp.sum(dsp[:, cols], axis=-1, keepdims=True), 0.0)
            dvln[:, cols] = _tn(w, dspb[:, cols])
        dbt_ref[...] += dbt
        dvl = dvln[...]
        dg_ref[...] += jnp.sum(dvl * y, axis=0, keepdims=True)
        db_ref[...] += jnp.sum(dvl, axis=0, keepdims=True)
        dy = dvl * g_ref[...]
        dvg = rs * (dy - jnp.mean(dy, axis=-1, keepdims=True) - y * jnp.mean(dy * y, axis=-1, keepdims=True))
        dvs_ref[...] = (dvg * _gelu_grad(vv, cv)).astype(BF)

    tile = lambda off: pl.BlockSpec((BLK, SG_W), lambda i: (i, off // SG_W))
    full = lambda shape: pl.BlockSpec(shape, lambda i: (0,) * len(shape))
    row = pl.BlockSpec((BLK, SG_W), lambda i: (i, 0))
    return _call(body, grid=(S // BLK,), name=name, args=[proj, proj, dz, sgw, sgbT, lng, lnb],
                 in_specs=[tile(R_U), tile(R_VS), row, full((SG_GROUPS, BLK, BLK)), full((BLK, BLK)), full((1, SG_W)), full((1, SG_W))],
                 out_specs=[row, row, full((SG_GROUPS, BLK, BLK)), full((BLK, BLK)), full((1, SG_W)), full((1, SG_W))],
                 out_shape=[jax.ShapeDtypeStruct((S, SG_W), BF), jax.ShapeDtypeStruct((S, SG_W), BF),
                            jax.ShapeDtypeStruct((SG_GROUPS, BLK, BLK), F32), jax.ShapeDtypeStruct((BLK, BLK), F32),
                            jax.ShapeDtypeStruct((1, SG_W), F32), jax.ShapeDtypeStruct((1, SG_W), F32)],
                 scratch=[pltpu.VMEM((BLK, SG_W), F32)])[0]


def _gate_merge(oatt, z, watt, wsg, proj, name, comm=()):
    S = oatt.shape[0]
    nb, _, Db = watt.shape
    D = nb * Db
    tm = _tile(S, 512)
    half = D // 2
    ga, gs = R_GA // half, (R_GA + D) // half

    def body(oa_ref, z_ref, wa_ref, ws_ref, ga0, ga1, gs0, gs1, ya_ref, ys_ref, mg_ref):
        oa, zv = oa_ref[...], z_ref[...]
        for j in range(nb):
            cols = slice(j * Db, (j + 1) * Db)
            g_a, g_s = (ga0, gs0) if j < nb // 2 else (ga1, gs1)
            gcols = slice((j % (nb // 2)) * Db, (j % (nb // 2) + 1) * Db)
            ya = _nn(oa, wa_ref[j])
            ys = _nn(zv, ws_ref[j])
            ya_ref[:, cols] = ya.astype(BF)
            ys_ref[:, cols] = ys.astype(BF)
            mg_ref[:, cols] = (jax.nn.sigmoid(g_a[:, gcols].astype(F32)) * ya + jax.nn.sigmoid(g_s[:, gcols].astype(F32)) * ys).astype(BF)

    out = pl.BlockSpec((tm, D), lambda i: (i, 0))
    gate = lambda b: pl.BlockSpec((tm, half), lambda i: (i, b))
    shp = jax.ShapeDtypeStruct((S, D), BF)
    return _call(body, grid=(S // tm,), name=name, args=[oatt, z, watt, wsg, proj, proj, proj, proj], comm=comm,
                 in_specs=[pl.BlockSpec((tm, GROUP_W), lambda i: (i, 0)), pl.BlockSpec((tm, SG_W), lambda i: (i, 0)),
                           pl.BlockSpec((nb, GROUP_W, Db), lambda i: (0, 0, 0)), pl.BlockSpec((nb, SG_W, Db), lambda i: (0, 0, 0)),
                           gate(ga), gate(ga + 1), gate(gs), gate(gs + 1)],
                 out_specs=[out, out, out], out_shape=[shp, shp, shp])


def _mix_out(merged, wout, x, gn, name):
    S, D = x.shape
    tm = _tile(S, 256)

    def body(m_ref, w_ref, x_ref, gn_ref, xo_ref, hn_ref):
        xo = x_ref[...] + _nn(m_ref[...], w_ref[...])
        r = lax.rsqrt(jnp.mean(xo * xo, axis=-1, keepdims=True) + NORM_EPS)
        xo_ref[...] = xo
        hn_ref[...] = (xo * r * gn_ref[...]).astype(BF)

    row = pl.BlockSpec((tm, D), lambda i: (i, 0))
    return _call(body, grid=(S // tm,), name=name, args=[merged, wout, x, gn],
                 in_specs=[row, pl.BlockSpec((D, D), lambda i: (0, 0)), row, pl.BlockSpec((1, D), lambda i: (0, 0))],
                 out_specs=[row, row], out_shape=[jax.ShapeDtypeStruct((S, D), F32), jax.ShapeDtypeStruct((S, D), BF)])[0]


def _mix_bwd_gate(dmix, wout, ya, ys, proj, name):
    S, D = dmix.shape
    tm, tn = _tile(S, 256), min(512, D // 2)
    half = D // 2
    ga, gs = R_GA // half, (R_GA + D) // half

    def body(dm_ref, w_ref, ya_ref, ys_ref, ga0, ga1, gs0, gs1, dya_ref, dys_ref, dga_ref, dgs_ref):
        dmv = dm_ref[...]
        for c0 in range(0, D, tn):
            cols = slice(c0, c0 + tn)
            g_a, g_s = (ga0, gs0) if c0 < half else (ga1, gs1)
            gcols = slice(c0 % half, c0 % half + tn)
            dm = _nt(dmv, w_ref[cols, :])
            sa = jax.nn.sigmoid(g_a[:, gcols].astype(F32))
            ss = jax.nn.sigmoid(g_s[:, gcols].astype(F32))
            dya_ref[:, cols] = (dm * sa).astype(BF)
            dys_ref[:, cols] = (dm * ss).astype(BF)
            dga_ref[:, cols] = (dm * ya_ref[:, cols].astype(F32) * sa * (1.0 - sa)).astype(BF)
            dgs_ref[:, cols] = (dm * ys_ref[:, cols].astype(F32) * ss * (1.0 - ss)).astype(BF)

    row = pl.BlockSpec((tm, D), lambda i: (i, 0))
    gate = lambda b: pl.BlockSpec((tm, half), lambda i: (i, b))
    shp = jax.ShapeDtypeStruct((S, D), BF)
    return _call(body, grid=(S // tm,), name=name, args=[dmix, wout, ya, ys, proj, proj, proj, proj],
                 in_specs=[row, pl.BlockSpec((D, D), lambda i: (0, 0)), row, row, gate(ga), gate(ga + 1), gate(gs), gate(gs + 1)],
                 out_specs=[row] * 4, out_shape=[shp] * 4)[0]


def _att_sg_dout(dya, dys, watt, wsg, oatt, name, comm=()):
    S, D = dya.shape
    nb, _, Db = watt.shape
    tm = _tile(S, 512)

    def body(dya_ref, dys_ref, wa_ref, ws_ref, oa_ref, do_ref, dz_ref, dvec_ref):
        def back(dy_ref, w_ref, rows):
            tot = None
            for j in range(nb):
                part = _nt(dy_ref[:, j * Db:(j + 1) * Db], w_ref[j, rows, :])
                tot = part if tot is None else tot + part
            return tot

        dov = back(dya_ref, wa_ref, slice(0, GROUP_W))
        do_ref[...] = dov
        for c0 in range(0, SG_W, GROUP_W):
            dz_ref[:, c0:c0 + GROUP_W] = back(dys_ref, ws_ref, slice(c0, c0 + GROUP_W)).astype(BF)
        prod = dov * oa_ref[...].astype(F32)
        for hh in range(HEADS_PER_GROUP):
            cols = slice(hh * HEAD_DIM, (hh + 1) * HEAD_DIM)
            dvec_ref[:, cols] = jnp.broadcast_to(jnp.sum(prod[:, cols], axis=-1, keepdims=True), (tm, HEAD_DIM))

    row = pl.BlockSpec((tm, D), lambda i: (i, 0))
    att = pl.BlockSpec((tm, GROUP_W), lambda i: (i, 0))
    return _call(body, grid=(S // tm,), name=name, args=[dya, dys, watt, wsg, oatt], comm=comm,
                 in_specs=[row, row, pl.BlockSpec((nb, GROUP_W, Db), lambda i: (0, 0, 0)), pl.BlockSpec((nb, SG_W, Db), lambda i: (0, 0, 0)), att],
                 out_specs=[att, pl.BlockSpec((tm, SG_W), lambda i: (i, 0)), att],
                 out_shape=[jax.ShapeDtypeStruct((S, GROUP_W), F32), jax.ShapeDtypeStruct((S, SG_W), BF), jax.ShapeDtypeStruct((S, GROUP_W), F32)])[0]


def _small_allreduce(pack, name):
    R = pack.shape[0]

    def body(p_ref, o_ref, gath, send, recv):
        x, y, c = _place()
        me = 4 * x + 2 * y + c
        gath[me] = p_ref[...]
        copies = []
        for r in range(1, N_DEV):
            px, py, pc = _flip(x, r & 4), _flip(y, r & 2), _flip(c, r & 1)
            peer = 4 * px + 2 * py + pc
            mk = lambda dst: pltpu.make_async_remote_copy(src_ref=p_ref, dst_ref=dst, send_sem=send.at[r - 1], recv_sem=recv.at[r - 1],
                                                          device_id=(px, py, pc), device_id_type=MESH)
            snd = mk(gath.at[me])
            snd.start()
            copies.append((snd, mk(gath.at[peer])))
        for snd, rcv in copies:
            rcv.wait_recv()
            snd.wait_send()
        acc = gath[0]
        for s in range(1, N_DEV):
            acc = acc + gath[s]
        o_ref[...] = acc

    vm = pl.BlockSpec(memory_space=pltpu.VMEM)
    return pl.pallas_call(
        body, name=name, in_specs=[vm], out_specs=vm, out_shape=jax.ShapeDtypeStruct(pack.shape, F32),
        scratch_shapes=[pltpu.VMEM((N_DEV, R, 128), F32), pltpu.SemaphoreType.DMA((7,)), pltpu.SemaphoreType.DMA((7,))],
        compiler_params=pltpu.CompilerParams(vmem_limit_bytes=VMEM_LIMIT),
    )(pack)


def _row_tile(R, C, elems=262144):
    fits = [t for t in range(16, R + 1, 16) if R % t == 0 and t * C <= elems]
    return max(fits) if fits else R


def _pair_add(parts, other, name):
    _, R, C = parts.shape
    tr = _row_tile(R, C, 1048576)

    def body(c_ref, p_ref, o_ref, s_ref):
        s_ref[0] = (p_ref[0].astype(F32) + o_ref[0].astype(F32)).astype(BF)

    core = lax.axis_index("c").astype(jnp.int32).reshape(1)
    return pl.pallas_call(
        body, name=name,
        grid_spec=pltpu.PrefetchScalarGridSpec(
            num_scalar_prefetch=1, grid=(N_CHIP, R // tr),
            in_specs=[pl.BlockSpec((1, tr, C), lambda q, i, c: (2 * q + c[0], i, 0)), pl.BlockSpec((1, tr, C), lambda q, i, c: (q, i, 0))],
            out_specs=pl.BlockSpec((1, tr, C), lambda q, i, c: (q, i, 0))),
        out_shape=jax.ShapeDtypeStruct((N_CHIP, R, C), BF),
        compiler_params=pltpu.CompilerParams(dimension_semantics=("arbitrary", "arbitrary"), vmem_limit_bytes=VMEM_LIMIT),
    )(core, parts, other)


def _adamw(parts, w, m, v, name):
    ns, R, C = parts.shape
    tr = _row_tile(R, C, 524288)
    c1 = 1.0 - ADAM_B1 ** ADAM_STEP
    c2 = 1.0 - ADAM_B2 ** ADAM_STEP

    def body(p_ref, w_ref, m_ref, v_ref, g_ref, d_ref, nm_ref, nv_ref):
        g = p_ref[0].astype(F32)
        for s in range(1, ns):
            g = g + p_ref[s].astype(F32)
        mn = ADAM_B1 * m_ref[...] + (1.0 - ADAM_B1) * g
        vn = ADAM_B2 * v_ref[...] + (1.0 - ADAM_B2) * (g * g)
        g_ref[...] = g
        nm_ref[...] = mn
        nv_ref[...] = vn
        d_ref[...] = -ADAM_LR * ((mn / c1) / (jnp.sqrt(vn / c2) + ADAM_EPS) + ADAM_WD * w_ref[...])

    row = pl.BlockSpec((tr, C), lambda i: (i, 0))
    shp = jax.ShapeDtypeStruct((R, C), F32)
    return _call(body, grid=(R // tr,), name=name, args=[parts, w, m, v],
                 in_specs=[pl.BlockSpec((ns, tr, C), lambda i: (0, i, 0)), row, row, row], out_specs=[row] * 4, out_shape=[shp] * 4)[0]


def _pad_rows(a, rows):
    return jnp.pad(a, ((0, rows - a.shape[0]), (0, 0)))


def kernel(x, ffn1_norm, ffn1_w_gate, ffn1_w_up, ffn1_w_down, mix_norm, w_in, sg_ln_g, sg_ln_b, sg_w, sg_b, w_att_out, w_sg_out, w_out, ffn2_norm, ffn2_w_gate, ffn2_w_up, ffn2_w_down, final_norm, loss_target, m_ffn1_norm, m_ffn1_w_gate, m_ffn1_w_up, m_ffn1_w_down, m_mix_norm, m_w_in, m_sg_ln_g, m_sg_ln_b, m_sg_w, m_sg_b, m_w_att_out, m_w_sg_out, m_w_out, m_ffn2_norm, m_ffn2_w_gate, m_ffn2_w_up, m_ffn2_w_down, m_final_norm, v_ffn1_norm, v_ffn1_w_gate, v_ffn1_w_up, v_ffn1_w_down, v_mix_norm, v_w_in, v_sg_ln_g, v_sg_ln_b, v_sg_w, v_sg_b, v_w_att_out, v_w_sg_out, v_w_out, v_ffn2_norm, v_ffn2_w_gate, v_ffn2_w_up, v_ffn2_w_down, v_final_norm):
    S, D = x.shape[1], x.shape[2]
    Pb = w_in.shape[2]
    P = N_DEV * Pb
    assert P == GA_OFF + 2 * D and D % (N_DEV * 128) == 0 and S % (BLK * DILATIONS[-1]) == 0
    xs, tgt = x[0], loss_target[0]

    sharded = dict(ffn1_w_gate=ffn1_w_gate, ffn1_w_up=ffn1_w_up, ffn1_w_down=ffn1_w_down, w_in=w_in, w_att_out=w_att_out,
                   w_sg_out=w_sg_out, w_out=w_out, ffn2_w_gate=ffn2_w_gate, ffn2_w_up=ffn2_w_up, ffn2_w_down=ffn2_w_down)
    cols = ("ffn1_w_gate", "ffn1_w_up", "w_in", "ffn2_w_gate", "ffn2_w_up")
    local = lambda n, a: a[0].T if n in cols else a[0]
    back = lambda n, a: a.T[None] if n in cols else a[None]
    wloc = {n: local(n, w) for n, w in sharded.items()}
    sb = {n: w.astype(BF) for n, w in wloc.items()}

    (h1,), ((wg1,),) = _rms_fwd(xs, ffn1_norm, "rms1", comm=[_Gather([sb["ffn1_w_gate"]], 1.0, 1.0)])
    (g1,), ((wu1,),) = _ffn_gate(h1, wg1, "ffn1_gate", comm=[_Gather([sb["ffn1_w_up"]], 0.9, 0.55)])
    (u1, a1), ((wd1,),) = _ffn_up_act(h1, wu1, g1, "ffn1_up_act", comm=[_Gather([sb["ffn1_w_down"]], 0.9, 0.55)])
    (x1, h2), ((winT8,),) = _ffn_down_norm(a1, wd1, xs, mix_norm, "ffn1_down", comm=[_Gather([sb["w_in"]], 1.0, 0.7)])
    winT = winT8.reshape(P, D)
    (qkv, rest), ((wg2, wu2),) = _proj_split(h2, winT, _tile(S, 1024), 512, U_OFF, "proj",
                                           comm=[_Gather([sb["ffn2_w_gate"], sb["ffn2_w_up"]], 0.85, 0.5)])
    tabs = _rope_tables(S)
    rides = [[_Gather([sb["w_att_out"], sb["w_sg_out"]], 0.9, 0.5)], [_Gather([sb["w_out"]], 0.85, 0.45)], []]
    os, lses, late = [], [], []
    for gi, d in enumerate(DILATIONS):
        (o, l), got_here = _att_fwd(qkv, tabs, gi, d, f"att_fwd{gi}", comm=rides[gi])
        late += [w for g in got_here for w in g]
        os.append(o)
        lses.append(l)
    watt, wsg, wout8 = late
    wout = wout8.reshape(D, D)
    oatt, lse = _att_combine(os, lses, "att_combine")
    sgw = sg_w[0]
    sgbT = jnp.pad(sg_b[0].T, ((0, 0), (0, BLK - SG_GROUPS)))
    z = _sg_fwd(rest, sgw, sgbT, sg_ln_g, sg_ln_b, "sg_fwd")
    (ya, ys, merged), _ = _gate_merge(oatt, z, watt, wsg, rest, "gate_merge")
    x2, h3 = _mix_out(merged, wout, x1, ffn2_norm, "mix_out")
    (g3, u3, a3), ((wd2,),) = _ffn_up(h3, wg2, wu2, "ffn2_up", comm=[_Gather([sb["ffn2_w_down"]], 0.6, 0.35)])
    dx3, dyb3, d_final, loss_part = _ffn_down_loss(a3, wd2, x2, final_norm.reshape(1, D), tgt, "ffn2_down_loss")

    Fb = wg2.shape[1]
    Db = watt.shape[2]
    p_pad = -(-P // PROJ_TK) * PROJ_TK
    win_tail = _pad_rows(winT[p_pad - PROJ_TK:], PROJ_TK)
    (dg3, du3), _ = _ffn_bwd_act(dyb3, wd2, g3, u3, "ffn2_bwd_act")
    (dwd2,), _ = _ffn_dwd(a3, dyb3, "ffn2_dwd")
    (dwg2, dwu2), _ = _ffn_dwgu(h3, dg3, du3, "ffn2_dwgu")
    ffn2_parts = [dwd2, dwg2, dwu2]
    (dx2, dmixb, d_ffn2n), (ffn2_other,) = _ffn_dh(dg3, du3, wg2, wu2, x2, ffn2_norm, dx3, "ffn2_dh", comm=[_Swap(ffn2_parts)])
    ffn2_sums = [_pair_add(p, o, f"pair_ffn2_{i}") for i, (p, o) in enumerate(zip(ffn2_parts, ffn2_other))]

    dya, dys, dga, dgs = _mix_bwd_gate(dmixb, wout, ya, ys, rest, "mix_bwd_gate")
    (dwout,), _ = _mm_tn(merged, dmixb, _tile(D, 1024), _tile(D, 1024), _tile(S, 1024), False, "dw_out")
    do, dz, dvec = _att_sg_dout(dya, dys, watt, wsg, oatt, "att_sg_dout")
    (dwatt,), _ = _mm_tn(oatt, dya, GROUP_W, 2 * Db, _tile(S, 1024), Db, "dw_att")
    (dwsg,), _ = _mm_tn(z, dys, SG_W, 2 * Db, _tile(S, 1024), Db, "dw_sg")
    mix_parts = [dwout.reshape(N_DEV, D // N_DEV, D), dwatt, dwsg]
    du, dvs, d_sgw, d_sgbT, d_lng, d_lnb = _sg_bwd(rest, dz, sgw, sgbT, sg_ln_g, sg_ln_b, "sg_bwd")
    dqs, dks, dvs_att, ffn2_got = [], [], [], []
    for gi, d in enumerate(DILATIONS):
        ride = [_Ici([ffn2_sums[0]])] if gi == 2 else []
        (dq, dk, dv), got_here = _att_bwd(qkv, tabs, do, lse, dvec, gi, d, f"att_bwd{gi}", comm=ride)
        ffn2_got += [g[0] for g in got_here]
        dqs.append(dq)
        dks.append(dk)
        dvs_att.append(dv)
    dproj = jnp.concatenate([t.astype(BF) for t in dqs + dks + dvs_att] + [du, dvs, dga, dgs, jnp.zeros((S, p_pad - P), BF)], axis=1)
    (dx1, dyb1, d_mixn), (ffn2_rest, mix_other) = _dh_rms_bwd([(dproj, winT)], False, PROJ_TK, x1, mix_norm, dx2, 0.5, "proj_dh",
                                                            comm=[_Ici(ffn2_sums[1:]), _Swap(mix_parts)], tail=win_tail)
    ffn2_got += ffn2_rest
    mix_sums = [_pair_add(p, o, f"pair_mix_{i}") for i, (p, o) in enumerate(zip(mix_parts, mix_other))]
    (dwd1,), (mix_got,) = _ffn_dwd(a1, dyb1, "ffn1_dwd", comm=[_Ici(mix_sums)])
    rows = lambda a: a.reshape(-1, 128)
    pad8 = lambda a: _pad_rows(a, -(-a.shape[0] // 8) * 8)
    small = [("sg_w", rows(d_sgw), sg_w, m_sg_w, v_sg_w), ("mix_norm", rows(d_mixn), mix_norm, m_mix_norm, v_mix_norm),
             ("ffn2_norm", rows(d_ffn2n), ffn2_norm, m_ffn2_norm, v_ffn2_norm), ("final_norm", rows(d_final), final_norm, m_final_norm, v_final_norm),
             ("sg_ln_g", rows(d_lng), sg_ln_g, m_sg_ln_g, v_sg_ln_g), ("sg_ln_b", rows(d_lnb), sg_ln_b, m_sg_ln_b, v_sg_ln_b),
             ("sg_b", d_sgbT[:, :SG_GROUPS].T, sg_b, m_sg_b, v_sg_b)]
    gpack = jnp.concatenate([pad8(g) for _, g, _, _, _ in small] + [pad8(loss_part)], axis=0)
    (dwin,), ((wd1_other,), (gpacks,)) = _mm_tn(dproj, h2, 512, D, S, False, "dw_in", mrows=P,
                                              comm=[_Swap([dwd1]), _Spread([gpack])])
    dwin = dwin.reshape(N_DEV, Pb, D)
    wd1_sum = _pair_add(dwd1, wd1_other, "pair_wd1")
    (dg1, du1), ((wd1_got,), (win_other,)) = _ffn_bwd_act(dyb1, wd1, g1, u1, "ffn1_bwd_act", comm=[_Ici([wd1_sum]), _Swap([dwin])])
    win_sum = _pair_add(dwin, win_other, "pair_win")
    (dwg1, dwu1), ((win_got,),) = _ffn_dwgu(h1, dg1, du1, "ffn1_dwgu", comm=[_Ici([win_sum])])
    gu_parts = [dwg1, dwu1]
    gu_other = _comm_only(_Swap(gu_parts), "swap_gu1")
    gu_sums = [_pair_add(p, o, f"pair_gu1_{i}") for i, (p, o) in enumerate(zip(gu_parts, gu_other))]
    (dx0, _, d_ffn1n), (gu_got,) = _ffn_dh(dg1, du1, wg1, wu1, xs, ffn1_norm, dx1, "ffn1_dh", comm=[_Ici(gu_sums)])

    got = dict(ffn2_w_down=ffn2_got[0], ffn2_w_gate=ffn2_got[1], ffn2_w_up=ffn2_got[2], w_out=mix_got[0], w_att_out=mix_got[1],
               w_sg_out=mix_got[2], w_in=win_got, ffn1_w_gate=gu_got[0], ffn1_w_up=gu_got[1], ffn1_w_down=wd1_got)
    moments = dict(ffn1_w_gate=(m_ffn1_w_gate, v_ffn1_w_gate), ffn1_w_up=(m_ffn1_w_up, v_ffn1_w_up),
                   ffn1_w_down=(m_ffn1_w_down, v_ffn1_w_down), w_in=(m_w_in, v_w_in), w_att_out=(m_w_att_out, v_w_att_out),
                   w_sg_out=(m_w_sg_out, v_w_sg_out), w_out=(m_w_out, v_w_out), ffn2_w_gate=(m_ffn2_w_gate, v_ffn2_w_gate),
                   ffn2_w_up=(m_ffn2_w_up, v_ffn2_w_up), ffn2_w_down=(m_ffn2_w_down, v_ffn2_w_down))
    res = {}
    for n in sharded:
        mm, vv = moments[n]
        outs = _adamw(got[n], wloc[n], local(n, mm), local(n, vv), "adamw_" + n)
        res[n] = [back(n, o) for o in outs]

    zero8 = jnp.zeros((8, 128), F32)
    wpack = jnp.concatenate([pad8(rows(w)) for _, _, w, _, _ in small] + [zero8], axis=0)
    mpack = jnp.concatenate([pad8(rows(m)) for _, _, _, m, _ in small] + [zero8], axis=0)
    vpack = jnp.concatenate([pad8(rows(v)) for _, _, _, _, v in small] + [zero8], axis=0)
    packs = _adamw(gpacks, wpack, mpack, vpack, "adamw_small")
    off = 0
    for n, g, w, _, _ in small:
        r = g.shape[0]
        res[n] = [p[off:off + r].reshape(w.shape) for p in packs]
        off += -(-r // 8) * 8
    loss = packs[0][off, 0]
    g_first = _small_allreduce(rows(d_ffn1n), "allreduce_ffn1_norm")
    res["ffn1_norm"] = [p.reshape(ffn1_norm.shape) for p in
                        _adamw(g_first[None], rows(ffn1_norm), rows(m_ffn1_norm), rows(v_ffn1_norm), "adamw_ffn1_norm")]

    order = ["ffn1_norm", "ffn1_w_gate", "ffn1_w_up", "ffn1_w_down", "mix_norm", "w_in", "sg_ln_g", "sg_ln_b", "sg_w", "sg_b",
             "w_att_out", "w_sg_out", "w_out", "ffn2_norm", "ffn2_w_gate", "ffn2_w_up", "ffn2_w_down", "final_norm"]
    return (loss, dx0[None], *[res[n][0] for n in order], *[res[n][1] for n in order], *[res[n][2] for n in order],
            *[res[n][3] for n in order])
```

```python
import math

import jax
import jax.numpy as jnp
from jax import lax
from jax.experimental import pallas as pl
from jax.experimental.pallas import tpu as pltpu

BF = jnp.bfloat16
F32 = jnp.float32
MESH = pl.DeviceIdType.MESH
N_DEV = 8
N_CHIP = 4

HEAD_DIM = 128
HEADS_PER_GROUP = 4
GROUP_W = HEADS_PER_GROUP * HEAD_DIM
DILATIONS = (1, 4, 16)
ATT_W = len(DILATIONS) * GROUP_W
SG_W = 1536
SG_GROUPS = 12
BLK = 128
ROPE_DIM = 32
ROPE_THETA = 500000.0
NORM_EPS = 1e-6
LN_EPS = 1e-5
Q_OFF, K_OFF, V_OFF, U_OFF, VS_OFF, GA_OFF = 0, ATT_W, 2 * ATT_W, 3 * ATT_W, 3 * ATT_W + SG_W, 3 * ATT_W + 2 * SG_W

ADAM_LR, ADAM_B1, ADAM_B2, ADAM_EPS, ADAM_WD, ADAM_STEP = 0.001, 0.9, 0.999, 1e-08, 0.01, 10

VMEM_LIMIT = 56 * 1024 * 1024
NEG = -1e30
ANY = pl.BlockSpec(memory_space=pl.ANY)
EPI_ROWS = 128
ACC_COLS = 512
FFN_PAIR = 2
FFN_ROWS = 1024
PROJ_TK = 1536
R_U, R_VS, R_GA = 0, SG_W, 2 * SG_W


def _once(shape, index_map):
    return pl.BlockSpec(shape, index_map, pipeline_mode=pl.Buffered(1))


def _tile(n, pref):
    t = min(n, pref)
    while n % t:
        t //= 2
    return t


def _nt(a, b):
    return lax.dot_general(a, b, (((1,), (1,)), ((), ())), preferred_element_type=F32)


def _tn(a, b):
    return lax.dot_general(a, b, (((0,), (0,)), ((), ())), preferred_element_type=F32)


def _nn(a, b):
    return jnp.dot(a, b, preferred_element_type=F32)


def _acc_dots(acc_ref, terms, transposed_rhs=False):
    n = acc_ref.shape[1]
    width = min(n, ACC_COLS)
    for c0 in range(0, n, width):
        cols = slice(c0, c0 + width)
        tot = None
        for lhs, rhs in terms:
            part = _nt(lhs, rhs(cols)) if transposed_rhs else _nn(lhs, rhs(cols))
            tot = part if tot is None else tot + part
        acc_ref[:, cols] += tot


def _gauss_cdf(x):
    return 0.5 * (1.0 + lax.erf(x * (2.0 ** -0.5)))


def _gelu_grad(x, cdf):
    return cdf + x * jnp.exp(-0.5 * x * x) * (1.0 / math.sqrt(2.0 * math.pi))


def _place():
    x, y, c = lax.axis_index("x"), lax.axis_index("y"), lax.axis_index("c")
    return x, y, c


def _flip(v, bit):
    return 1 - v if bit else v


class _Gather:
    def __init__(self, shards, mid_frac=1.0, relay_frac=0.5):
        self.arrays = list(shards)
        self.relay_frac = relay_frac
        self.mid_frac = mid_frac
        nw = len(shards)
        self.out_shape = [jax.ShapeDtypeStruct((N_DEV,) + s.shape, s.dtype) for s in shards]
        self.scratch = [pltpu.SemaphoreType.DMA((nw, 7)), pltpu.SemaphoreType.DMA((nw, 7)), pltpu.SemaphoreType.DMA((nw,))]

    def _parts(self, ins, outs, sems):
        x, y, c = _place()
        send, recv, loc = sems
        south = c == 0
        near = (jnp.where(south, x, 1 - x), jnp.where(south, 1 - y, y), c)
        far = (jnp.where(south, 1 - x, x), jnp.where(south, y, 1 - y), c)
        diag = (1 - x, 1 - y, c)

        def copy(k, s, block, to, src=None):
            dst = outs[k].at[4 * block[0] + 2 * block[1] + block[2]]
            return pltpu.make_async_remote_copy(src_ref=dst if src is None else src, dst_ref=dst, send_sem=send.at[k, s],
                                                recv_sem=recv.at[k, s], device_id=to, device_id_type=MESH)

        def first(k):
            me = (x, y, c)
            return [copy(k, 0, me, (x, y, 1 - c), src=ins[k]), copy(k, 1, me, (1 - x, y, c), src=ins[k]),
                    copy(k, 2, me, (x, 1 - y, c), src=ins[k])]

        def local(k):
            return pltpu.make_async_copy(ins[k], outs[k].at[4 * x + 2 * y + c], loc.at[k])

        return x, y, c, near, far, diag, copy, first, local

    def start(self, ins, outs, sems):
        *_, first, local = self._parts(ins, outs, sems)
        for k in range(len(ins)):
            local(k).start()
            for cp in first(k):
                cp.start()

    def relay(self, ins, outs, sems):
        x, y, c, near, far, _, copy, _, _ = self._parts(ins, outs, sems)
        for k in range(len(ins)):
            copy(k, 2 - c, near, (x, y, c)).wait_recv()
            copy(k, 3, near, far).start()
            copy(k, 5 - c, near, (x, y, 1 - c)).start()

    def mid(self, ins, outs, sems):
        x, y, c, _, far, diag, copy, _, _ = self._parts(ins, outs, sems)
        for k in range(len(ins)):
            copy(k, 1 + c, far, (x, y, c)).wait_recv()
            copy(k, 4 + c, far, (x, y, 1 - c)).start()
            copy(k, 3, diag, (x, y, c)).wait_recv()
            copy(k, 6, diag, (x, y, 1 - c)).start()

    def finish(self, ins, outs, sems):
        x, y, c, near, _, _, copy, first, local = self._parts(ins, outs, sems)
        sib = (x, y, 1 - c)
        for k in range(len(ins)):
            copy(k, 0, sib, (x, y, c)).wait_recv()
            copy(k, 4, (1 - x, y, 1 - c), (x, y, c)).wait_recv()
            copy(k, 5, (x, 1 - y, 1 - c), (x, y, c)).wait_recv()
            copy(k, 6, (1 - x, 1 - y, 1 - c), (x, y, c)).wait_recv()
        for k in range(len(ins)):
            for cp in first(k):
                cp.wait_send()
            for s in (3, 4, 5, 6):
                copy(k, s, near, sib).wait_send()
            local(k).wait()


class _Swap:
    def __init__(self, parts):
        self.arrays = list(parts)
        nw = len(parts)
        self.out_shape = [jax.ShapeDtypeStruct((N_CHIP,) + p.shape[1:], p.dtype) for p in parts]
        self.scratch = [pltpu.SemaphoreType.DMA((nw, N_CHIP)), pltpu.SemaphoreType.DMA((nw, N_CHIP))]

    def _copy(self, ins, outs, sems, k, q):
        x, y, c = _place()
        return pltpu.make_async_remote_copy(src_ref=ins[k].at[2 * q + 1 - c], dst_ref=outs[k].at[q], send_sem=sems[0].at[k, q],
                                            recv_sem=sems[1].at[k, q], device_id=(x, y, 1 - c), device_id_type=MESH)

    mid_frac = None

    def start(self, ins, outs, sems):
        for k in range(len(ins)):
            for q in range(N_CHIP):
                self._copy(ins, outs, sems, k, q).start()

    def finish(self, ins, outs, sems):
        for k in range(len(ins)):
            for q in range(N_CHIP):
                self._copy(ins, outs, sems, k, q).wait()


class _Ici:
    mid_frac = None

    def __init__(self, sums):
        self.arrays = list(sums)
        nw = len(sums)
        self.out_shape = [jax.ShapeDtypeStruct(s.shape, s.dtype) for s in sums]
        self.scratch = [pltpu.SemaphoreType.DMA((nw, 3)), pltpu.SemaphoreType.DMA((nw, 3)), pltpu.SemaphoreType.DMA((nw,))]

    def _copies(self, ins, outs, sems, k):
        x, y, c = _place()
        myq = 2 * x + y
        out = []
        for r in range(1, N_CHIP):
            px, py = _flip(x, r & 2), _flip(y, r & 1)
            pq = 2 * px + py
            mk = lambda dst: pltpu.make_async_remote_copy(src_ref=ins[k].at[pq], dst_ref=dst, send_sem=sems[0].at[k, r - 1],
                                                          recv_sem=sems[1].at[k, r - 1], device_id=(px, py, c), device_id_type=MESH)
            out.append((mk(outs[k].at[myq]), mk(outs[k].at[pq])))
        return out, pltpu.make_async_copy(ins[k].at[myq], outs[k].at[myq], sems[2].at[k])

    def start(self, ins, outs, sems):
        for k in range(len(ins)):
            remote, local = self._copies(ins, outs, sems, k)
            local.start()
            for snd, _ in remote:
                snd.start()

    def finish(self, ins, outs, sems):
        for k in range(len(ins)):
            remote, local = self._copies(ins, outs, sems, k)
            for snd, rcv in remote:
                rcv.wait_recv()
                snd.wait_send()
            local.wait()


class _Spread:
    mid_frac = None

    def __init__(self, arrays):
        self.arrays = list(arrays)
        nw = len(arrays)
        self.out_shape = [jax.ShapeDtypeStruct((N_DEV,) + a.shape, a.dtype) for a in arrays]
        self.scratch = [pltpu.SemaphoreType.DMA((nw, 7)), pltpu.SemaphoreType.DMA((nw, 7)), pltpu.SemaphoreType.DMA((nw,))]

    def _copies(self, ins, outs, sems, k):
        x, y, c = _place()
        me = 4 * x + 2 * y + c
        out = []
        for r in range(1, N_DEV):
            px, py, pc = _flip(x, r & 4), _flip(y, r & 2), _flip(c, r & 1)
            peer = 4 * px + 2 * py + pc
            mk = lambda dst: pltpu.make_async_remote_copy(src_ref=ins[k], dst_ref=dst, send_sem=sems[0].at[k, r - 1],
                                                          recv_sem=sems[1].at[k, r - 1], device_id=(px, py, pc), device_id_type=MESH)
            out.append((mk(outs[k].at[me]), mk(outs[k].at[peer])))
        return out, pltpu.make_async_copy(ins[k], outs[k].at[me], sems[2].at[k])

    def start(self, ins, outs, sems):
        for k in range(len(ins)):
            remote, local = self._copies(ins, outs, sems, k)
            local.start()
            for snd, _ in remote:
                snd.start()

    def finish(self, ins, outs, sems):
        for k in range(len(ins)):
            remote, local = self._copies(ins, outs, sems, k)
            for snd, rcv in remote:
                rcv.wait_recv()
                snd.wait_send()
            local.wait()


def _call(body, *, grid, in_specs, out_specs, out_shape, name, args, scratch=(), comm=()):
    comm = list(comm)
    n_in, n_out, n_scr = len(in_specs), len(out_specs), len(scratch)
    total = math.prod(grid) if grid else 1

    def wrapped(*refs):
        p = n_in
        cin = []
        for cm in comm:
            cin.append(refs[p:p + len(cm.arrays)])
            p += len(cm.arrays)
        own_out = refs[p:p + n_out]
        p += n_out
        cout = []
        for cm in comm:
            cout.append(refs[p:p + len(cm.arrays)])
            p += len(cm.arrays)
        own_scr = refs[p:p + n_scr]
        p += n_scr
        csem = []
        for cm in comm:
            csem.append(refs[p:p + len(cm.scratch)])
            p += len(cm.scratch)
        step = 0
        for axis, g in enumerate(grid):
            step = step * g + pl.program_id(axis)

        def at(when, what):
            if total == 1:
                what()
            else:
                pl.when(step == when)(what)

        def starts():
            for cm, i, o, s in zip(comm, cin, cout, csem):
                cm.start(i, o, s)

        def finishes():
            for cm, i, o, s in zip(comm, cin, cout, csem):
                cm.finish(i, o, s)

        if comm:
            at(0, starts)
        if body is not None:
            body(*refs[:n_in], *own_out, *own_scr)
        for cm, i, o, s in zip(comm, cin, cout, csem):
            if cm.mid_frac is not None:
                at(min(total - 1, int(total * cm.relay_frac)), lambda cm=cm, i=i, o=o, s=s: cm.relay(i, o, s))
                at(min(total - 1, int(total * cm.mid_frac)), lambda cm=cm, i=i, o=o, s=s: cm.mid(i, o, s))
        if comm:
            at(total - 1, finishes)

    kw = dict(grid=tuple(grid)) if grid else {}
    outs = pl.pallas_call(
        wrapped, name=name, **kw,
        in_specs=list(in_specs) + [ANY for cm in comm for _ in cm.arrays],
        out_specs=list(out_specs) + [ANY for cm in comm for _ in cm.arrays],
        out_shape=list(out_shape) + [s for cm in comm for s in cm.out_shape],
        scratch_shapes=list(scratch) + [s for cm in comm for s in cm.scratch],
        compiler_params=pltpu.CompilerParams(dimension_semantics=("arbitrary",) * len(grid), vmem_limit_bytes=VMEM_LIMIT),
    )(*args, *[a for cm in comm for a in cm.arrays])
    own, p, per = list(outs[:n_out]), n_out, []
    for cm in comm:
        per.append(list(outs[p:p + len(cm.arrays)]))
        p += len(cm.arrays)
    return own, per


def _comm_only(cm, name):
    return _call(None, grid=(), in_specs=[], out_specs=[], out_shape=[], name=name, args=[], comm=[cm])[1][0]


def _rms_fwd(x, g, name, comm=()):
    S, D = x.shape
    tm = _tile(S, 512)

    def body(x_ref, g_ref, o_ref):
        xv = x_ref[...]
        r = lax.rsqrt(jnp.mean(xv * xv, axis=-1, keepdims=True) + NORM_EPS)
        o_ref[...] = (xv * r * g_ref[...]).astype(BF)

    return _call(body, grid=(S // tm,), name=name, args=[x, g], comm=comm,
                 in_specs=[pl.BlockSpec((tm, D), lambda i: (i, 0)), pl.BlockSpec((1, D), lambda i: (0, 0))],
                 out_specs=[pl.BlockSpec((tm, D), lambda i: (i, 0))], out_shape=[jax.ShapeDtypeStruct((S, D), BF)])


def _ffn_up(h, wg, wu, name, comm=()):
    S, D = h.shape
    nb, Fb, _ = wg.shape
    tm = _tile(S, FFN_ROWS)

    def body(h_ref, wg_ref, wu_ref, g_ref, u_ref, a_ref):
        hv = h_ref[...]
        g = _nt(hv, wg_ref[0])
        u = _nt(hv, wu_ref[0])
        g_ref[0] = g.astype(BF)
        u_ref[0] = u.astype(BF)
        a_ref[0] = (g * jax.nn.sigmoid(g) * u).astype(BF)

    act = pl.BlockSpec((1, tm, Fb), lambda j, i: (j, i, 0))
    w = pl.BlockSpec((1, Fb, D), lambda j, i: (j, 0, 0))
    shp = jax.ShapeDtypeStruct((nb, S, Fb), BF)
    return _call(body, grid=(nb, S // tm), name=name, args=[h, wg, wu], comm=comm,
                 in_specs=[pl.BlockSpec((tm, D), lambda j, i: (i, 0)), w, w], out_specs=[act, act, act], out_shape=[shp, shp, shp])


def _ffn_gate(h, wg, name, comm=()):
    S, D = h.shape
    nb, Fb, _ = wg.shape
    tm = _tile(S, FFN_ROWS)

    def body(h_ref, wg_ref, g_ref):
        g_ref[0] = _nt(h_ref[...], wg_ref[0]).astype(BF)

    act = pl.BlockSpec((1, tm, Fb), lambda j, i: (j, i, 0))
    return _call(body, grid=(nb, S // tm), name=name, args=[h, wg], comm=comm,
                 in_specs=[pl.BlockSpec((tm, D), lambda j, i: (i, 0)), pl.BlockSpec((1, Fb, D), lambda j, i: (j, 0, 0))],
                 out_specs=[act], out_shape=[jax.ShapeDtypeStruct((nb, S, Fb), BF)])


def _ffn_up_act(h, wu, g, name, comm=()):
    S, D = h.shape
    nb, Fb, _ = wu.shape
    tm = _tile(S, FFN_ROWS)

    def body(h_ref, wu_ref, g_ref, u_ref, a_ref):
        u = _nt(h_ref[...], wu_ref[0])
        gv = g_ref[0].astype(F32)
        u_ref[0] = u.astype(BF)
        a_ref[0] = (gv * jax.nn.sigmoid(gv) * u).astype(BF)

    act = pl.BlockSpec((1, tm, Fb), lambda j, i: (j, i, 0))
    shp = jax.ShapeDtypeStruct((nb, S, Fb), BF)
    return _call(body, grid=(nb, S // tm), name=name, args=[h, wu, g], comm=comm,
                 in_specs=[pl.BlockSpec((tm, D), lambda j, i: (i, 0)), pl.BlockSpec((1, Fb, D), lambda j, i: (j, 0, 0)), act],
                 out_specs=[act, act], out_shape=[shp, shp])


def _ffn_down_norm(a, wd, x, gn, name, comm=()):
    nb, S, Fb = a.shape
    D = wd.shape[2]
    tm = _tile(S, 512)

    nj = nb // FFN_PAIR

    def body(a_ref, wd_ref, x_ref, gn_ref, xo_ref, hn_ref, acc_ref):
        j = pl.program_id(1)

        @pl.when(j == 0)
        def _():
            acc_ref[...] = jnp.zeros_like(acc_ref)

        _acc_dots(acc_ref, [(a_ref[b], lambda cols, b=b: wd_ref[b, :, cols]) for b in range(FFN_PAIR)])

        @pl.when(j == nj - 1)
        def _():
            def chunk(t, carry):
                rows = pl.ds(pl.multiple_of(t * EPI_ROWS, EPI_ROWS), EPI_ROWS)
                xo = x_ref[rows, :] + 0.5 * acc_ref[rows, :]
                r = lax.rsqrt(jnp.mean(xo * xo, axis=-1, keepdims=True) + NORM_EPS)
                xo_ref[rows, :] = xo
                hn_ref[rows, :] = (xo * r * gn_ref[...]).astype(BF)
                return carry

            lax.fori_loop(0, tm // EPI_ROWS, chunk, 0)

    row = pl.BlockSpec((tm, D), lambda i, j: (i, 0))
    return _call(body, grid=(S // tm, nj), name=name, args=[a, wd, x, gn], comm=comm,
                 in_specs=[pl.BlockSpec((FFN_PAIR, tm, Fb), lambda i, j: (j, i, 0)), pl.BlockSpec((FFN_PAIR, Fb, D), lambda i, j: (j, 0, 0)),
                           row, pl.BlockSpec((1, D), lambda i, j: (0, 0))],
                 out_specs=[row, row], out_shape=[jax.ShapeDtypeStruct((S, D), F32), jax.ShapeDtypeStruct((S, D), BF)],
                 scratch=[pltpu.VMEM((tm, D), F32)])


def _ffn_down_loss(a, wd, x, gf, tgt, name):
    nb, S, Fb = a.shape
    D = wd.shape[2]
    tm = _tile(S, 512)

    nj = nb // FFN_PAIR

    def body(a_ref, wd_ref, x_ref, gf_ref, t_ref, dx_ref, dxb_ref, dgf_ref, loss_ref, acc_ref):
        i, j = pl.program_id(0), pl.program_id(1)

        @pl.when(j == 0)
        def _():
            acc_ref[...] = jnp.zeros_like(acc_ref)

        _acc_dots(acc_ref, [(a_ref[b], lambda cols, b=b: wd_ref[b, :, cols]) for b in range(FFN_PAIR)])

        @pl.when((j == nj - 1) & (i == 0))
        def _():
            dgf_ref[...] = jnp.zeros_like(dgf_ref)
            loss_ref[...] = jnp.zeros_like(loss_ref)

        @pl.when(j == nj - 1)
        def _():
            def chunk(t, carry):
                rows = pl.ds(pl.multiple_of(t * EPI_ROWS, EPI_ROWS), EPI_ROWS)
                xo = x_ref[rows, :] + 0.5 * acc_ref[rows, :]
                r = lax.rsqrt(jnp.mean(xo * xo, axis=-1, keepdims=True) + NORM_EPS)
                xh = xo * r
                gf = gf_ref[...]
                e = xh * gf - t_ref[rows, :]
                loss_ref[...] += jnp.sum(jnp.mean(e * e, axis=-1, keepdims=True), axis=0, keepdims=True) * 0.5
                dy = e * (1.0 / D)
                dgf_ref[...] += jnp.sum(dy * xh, axis=0, keepdims=True)
                dxh = dy * gf
                dx = r * (dxh - xh * jnp.mean(dxh * xh, axis=-1, keepdims=True))
                dx_ref[rows, :] = dx
                dxb_ref[rows, :] = (0.5 * dx).astype(BF)
                return carry

            lax.fori_loop(0, tm // EPI_ROWS, chunk, 0)

    row = pl.BlockSpec((tm, D), lambda i, j: (i, 0))
    once = row
    vec = pl.BlockSpec((1, D), lambda i, j: (0, 0))
    return _call(body, grid=(S // tm, nj), name=name, args=[a, wd, x, gf, tgt],
                 in_specs=[pl.BlockSpec((FFN_PAIR, tm, Fb), lambda i, j: (j, i, 0)), pl.BlockSpec((FFN_PAIR, Fb, D), lambda i, j: (j, 0, 0)),
                           once, vec, once],
                 out_specs=[row, row, vec, pl.BlockSpec((1, 128), lambda i, j: (0, 0))],
                 out_shape=[jax.ShapeDtypeStruct((S, D), F32), jax.ShapeDtypeStruct((S, D), BF), jax.ShapeDtypeStruct((1, D), F32),
                            jax.ShapeDtypeStruct((1, 128), F32)],
                 scratch=[pltpu.VMEM((tm, D), F32)])[0]


def _ffn_bwd_act(dyb, wd, g, u, name, comm=()):
    S, D = dyb.shape
    nb, Fb, _ = wd.shape
    tm = _tile(S, FFN_ROWS)

    def body(dy_ref, wd_ref, g_ref, u_ref, dg_ref, du_ref):
        rows = pl.ds(pl.multiple_of(pl.program_id(1) * tm, tm), tm)
        da = _nt(dy_ref[rows, :], wd_ref[0])
        gv = g_ref[0].astype(F32)
        uv = u_ref[0].astype(F32)
        sg = jax.nn.sigmoid(gv)
        du_ref[0] = (da * gv * sg).astype(BF)
        dg_ref[0] = (da * uv * sg * (1.0 + gv * (1.0 - sg))).astype(BF)

    act = pl.BlockSpec((1, tm, Fb), lambda j, i: (j, i, 0))
    shp = jax.ShapeDtypeStruct((nb, S, Fb), BF)
    return _call(body, grid=(nb, S // tm), name=name, args=[dyb, wd, g, u], comm=comm,
                 in_specs=[pl.BlockSpec((S, D), lambda j, i: (0, 0)), pl.BlockSpec((1, Fb, D), lambda j, i: (j, 0, 0)), act, act],
                 out_specs=[act, act], out_shape=[shp, shp])


def _ffn_dwd(a, dyb, name, comm=()):
    nb, S, Fb = a.shape
    D = dyb.shape[1]
    ts = S
    ns = S // ts

    def body(a_ref, dy_ref, o_ref, acc_ref):
        s = pl.program_id(1)

        @pl.when(s == 0)
        def _():
            acc_ref[...] = jnp.zeros_like(acc_ref)

        acc_ref[...] += _tn(a_ref[0], dy_ref[...])

        @pl.when(s == ns - 1)
        def _():
            o_ref[0] = acc_ref[...].astype(BF)

    return _call(body, grid=(nb, ns), name=name, args=[a, dyb], comm=comm,
                 in_specs=[pl.BlockSpec((1, ts, Fb), lambda j, s: (j, s, 0)), pl.BlockSpec((ts, D), lambda j, s: (s, 0))],
                 out_specs=[pl.BlockSpec((1, Fb, D), lambda j, s: (j, 0, 0))], out_shape=[jax.ShapeDtypeStruct((nb, Fb, D), BF)],
                 scratch=[pltpu.VMEM((Fb, D), F32)])


def _ffn_dwgu(h, dg, du, name, comm=()):
    S, D = h.shape
    nb, _, Fb = dg.shape
    ts = _tile(S, FFN_ROWS)
    ns = S // ts

    def body(h_ref, dg_ref, du_ref, og_ref, ou_ref, accg_ref, accu_ref):
        s = pl.program_id(1)

        @pl.when(s == 0)
        def _():
            accg_ref[...] = jnp.zeros_like(accg_ref)
            accu_ref[...] = jnp.zeros_like(accu_ref)

        hv = h_ref[...]
        accg_ref[...] += _tn(dg_ref[0], hv)
        accu_ref[...] += _tn(du_ref[0], hv)

        @pl.when(s == ns - 1)
        def _():
            og_ref[0] = accg_ref[...].astype(BF)
            ou_ref[0] = accu_ref[...].astype(BF)

    act = pl.BlockSpec((1, ts, Fb), lambda j, s: (j, s, 0))
    out = pl.BlockSpec((1, Fb, D), lambda j, s: (j, 0, 0))
    shp = jax.ShapeDtypeStruct((nb, Fb, D), BF)
    return _call(body, grid=(nb, ns), name=name, args=[h, dg, du], comm=comm,
                 in_specs=[pl.BlockSpec((ts, D), lambda j, s: (s, 0)), act, act], out_specs=[out, out], out_shape=[shp, shp],
                 scratch=[pltpu.VMEM((Fb, D), F32), pltpu.VMEM((Fb, D), F32)])


def _dh_rms_bwd(pairs, blocked, tk, x, gn, dxo, out_scale, name, comm=(), tail=None):
    S, D = x.shape
    nk = pairs[0][0].shape[0] if blocked else pairs[0][0].shape[1] // tk
    tm = _tile(S, 512)
    npair = len(pairs)
    assert blocked or (npair == 1 and tail is not None and nk >= 2)
    nin = 2 * npair + (0 if blocked else 1)

    def body(*refs):
        ins = refs[:nin]
        x_ref, gn_ref, dxo_ref, dx_ref, dxb_ref, dgn_ref, acc_ref = refs[nin:]
        i, k = pl.program_id(0), pl.program_id(1)

        @pl.when(k == 0)
        def _():
            acc_ref[...] = jnp.zeros_like(acc_ref)

        if blocked:
            _acc_dots(acc_ref, [(ins[2 * p][0], lambda cols, r=ins[2 * p + 1]: r[0, :, cols]) for p in range(npair)])
        else:
            @pl.when(k < nk - 1)
            def _():
                _acc_dots(acc_ref, [(ins[0][...], lambda cols: ins[1][:, cols])])

            @pl.when(k == nk - 1)
            def _():
                _acc_dots(acc_ref, [(ins[0][...], lambda cols: ins[2][:, cols])])

        @pl.when((k == nk - 1) & (i == 0))
        def _():
            dgn_ref[...] = jnp.zeros_like(dgn_ref)

        @pl.when(k == nk - 1)
        def _():
            def chunk(t, carry):
                rows = pl.ds(pl.multiple_of(t * EPI_ROWS, EPI_ROWS), EPI_ROWS)
                xv = x_ref[rows, :]
                r = lax.rsqrt(jnp.mean(xv * xv, axis=-1, keepdims=True) + NORM_EPS)
                xh = xv * r
                dh = acc_ref[rows, :]
                dgn_ref[...] += jnp.sum(dh * xh, axis=0, keepdims=True)
                dxh = dh * gn_ref[...]
                dx = dxo_ref[rows, :] + r * (dxh - xh * jnp.mean(dxh * xh, axis=-1, keepdims=True))
                dx_ref[rows, :] = dx
                dxb_ref[rows, :] = (out_scale * dx).astype(BF)
                return carry

            lax.fori_loop(0, tm // EPI_ROWS, chunk, 0)

    if blocked:
        mats = [pl.BlockSpec((1, tm, tk), lambda i, k: (k, i, 0)), pl.BlockSpec((1, tk, D), lambda i, k: (k, 0, 0))] * npair
        flat = [t for pr in pairs for t in pr]
    else:
        mats = [pl.BlockSpec((tm, tk), lambda i, k: (i, k)), pl.BlockSpec((tk, D), lambda i, k: (jnp.minimum(k, nk - 2), 0)),
                pl.BlockSpec((tk, D), lambda i, k: (0, 0))]
        flat = [*pairs[0], tail]
    row = pl.BlockSpec((tm, D), lambda i, k: (i, 0))
    once = row
    vec = pl.BlockSpec((1, D), lambda i, k: (0, 0))
    return _call(body, grid=(S // tm, nk), name=name, args=[*flat, x, gn, dxo], comm=comm,
                 in_specs=mats + [once, vec, once], out_specs=[row, row, vec],
                 out_shape=[jax.ShapeDtypeStruct((S, D), F32), jax.ShapeDtypeStruct((S, D), BF), jax.ShapeDtypeStruct((1, D), F32)],
                 scratch=[pltpu.VMEM((tm, D), F32)])


def _ffn_dh(dg, du, wgT, wuT, x, gn, dxo, name, comm=()):
    S, D = x.shape
    nb, _, Fb = dg.shape
    tm = _tile(S, 512)
    ni, er = S // tm, tm // nb
    assert er % 16 == 0

    def body(dg_ref, wg_ref, du_ref, wu_ref, x_ref, gn_ref, dxo_ref, dx_ref, dxb_ref, dgn_ref, acc_ref):
        i, k = pl.program_id(0), pl.program_id(1)
        slot = i % 2

        @pl.when((i == 0) & (k == 0))
        def _():
            acc_ref[...] = jnp.zeros_like(acc_ref)
            dgn_ref[...] = jnp.zeros_like(dgn_ref)

        @pl.when((i > 0) & (k == 0))
        def _():
            acc_ref[slot] = jnp.zeros((tm, D), F32)

        def finish_rows():
            rows = pl.ds(pl.multiple_of(k * er, er), er)
            xv = x_ref[rows, :]
            r = lax.rsqrt(jnp.mean(xv * xv, axis=-1, keepdims=True) + NORM_EPS)
            xh = xv * r
            dh = acc_ref[1 - slot, rows, :]
            dgn_ref[...] += jnp.where(i > 0, jnp.sum(dh * xh, axis=0, keepdims=True), 0.0)
            dxh = dh * gn_ref[...]
            dx = dxo_ref[rows, :] + r * (dxh - xh * jnp.mean(dxh * xh, axis=-1, keepdims=True))
            dx_ref[rows, :] = dx
            dxb_ref[rows, :] = dx.astype(BF)

        @pl.when(i < ni)
        def _():
            _acc_dots(acc_ref.at[slot], [(dg_ref[0], lambda cols: wg_ref[0, :, cols]), (du_ref[0], lambda cols: wu_ref[0, :, cols])])
            finish_rows()

        @pl.when(i == ni)
        def _():
            finish_rows()

    last = lambda i, k: jnp.where(i == ni, nb - 1, k)
    act = pl.BlockSpec((1, tm, Fb), lambda i, k: (last(i, k), jnp.minimum(i, ni - 1), 0))
    w = pl.BlockSpec((1, Fb, D), lambda i, k: (last(i, k), 0, 0))
    prev = pl.BlockSpec((tm, D), lambda i, k: (jnp.maximum(i - 1, 0), 0))
    vec = pl.BlockSpec((1, D), lambda i, k: (0, 0))
    return _call(body, grid=(ni + 1, nb), name=name, args=[dg, wgT, du, wuT, x, gn, dxo], comm=comm,
                 in_specs=[act, w, act, w, prev, vec, prev], out_specs=[prev, prev, vec],
                 out_shape=[jax.ShapeDtypeStruct((S, D), F32), jax.ShapeDtypeStruct((S, D), BF), jax.ShapeDtypeStruct((1, D), F32)],
                 scratch=[pltpu.VMEM((2, tm, D), F32)])


def _proj_split(a, bT, tm, tn, split, name, comm=()):
    M, K = a.shape
    N = bT.shape[0]
    n_first = split // tn

    def body(a_ref, b_ref, first_ref, rest_ref):
        n = pl.program_id(1)
        y = _nt(a_ref[...], b_ref[...])

        @pl.when(n < n_first)
        def _():
            first_ref[...] = y

        @pl.when(n >= n_first)
        def _():
            rest_ref[...] = y.astype(BF)

    return _call(body, grid=(M // tm, N // tn), name=name, args=[a, bT], comm=comm,
                 in_specs=[pl.BlockSpec((tm, K), lambda i, n: (i, 0)), pl.BlockSpec((tn, K), lambda i, n: (n, 0))],
                 out_specs=[pl.BlockSpec((tm, tn), lambda i, n: (i, jnp.minimum(n, n_first - 1))),
                            pl.BlockSpec((tm, tn), lambda i, n: (i, jnp.maximum(n - n_first, 0)))],
                 out_shape=[jax.ShapeDtypeStruct((M, split), F32), jax.ShapeDtypeStruct((M, N - split), BF)])


def _mm_tn(a, b, tm, tn, ts, blocked, name, comm=(), mrows=None):
    S, M = a.shape[0], (a.shape[1] if mrows is None else mrows)
    N = b.shape[1]
    ns = S // ts
    per_tile = tn // blocked if blocked else 0

    def body(a_ref, b_ref, o_ref, acc_ref):
        s = pl.program_id(2)

        @pl.when(s == 0)
        def _():
            acc_ref[...] = jnp.zeros_like(acc_ref)

        acc_ref[...] += _tn(a_ref[...], b_ref[...])

        @pl.when(s == ns - 1)
        def _():
            if blocked:
                for t in range(per_tile):
                    o_ref[t] = acc_ref[:, t * blocked:(t + 1) * blocked].astype(BF)
            else:
                o_ref[...] = acc_ref[...].astype(BF)

    if blocked:
        ospec = pl.BlockSpec((per_tile, tm, blocked), lambda i, n, s: (n, i, 0))
        oshape = jax.ShapeDtypeStruct((N // blocked, M, blocked), BF)
    else:
        ospec = pl.BlockSpec((tm, tn), lambda i, n, s: (i, n))
        oshape = jax.ShapeDtypeStruct((M, N), BF)
    return _call(body, grid=(M // tm, N // tn, ns), name=name, args=[a, b], comm=comm,
                 in_specs=[pl.BlockSpec((ts, tm), lambda i, n, s: (s, i)), pl.BlockSpec((ts, tn), lambda i, n, s: (s, n))],
                 out_specs=[ospec], out_shape=[oshape], scratch=[pltpu.VMEM((tm, tn), F32)])


def _rope_tables(S):
    half = ROPE_DIM // 2
    inv_freq = ROPE_THETA ** (-jnp.arange(0, ROPE_DIM, 2, dtype=F32) / ROPE_DIM)
    ang = jnp.arange(S, dtype=F32)[:, None] * inv_freq[None, :]
    cos, sin = jnp.cos(ang), jnp.sin(ang)
    zeros = jnp.zeros((S, HEAD_DIM - ROPE_DIM), F32)
    c = jnp.concatenate([cos, cos, jnp.ones((S, HEAD_DIM - ROPE_DIM), F32)], axis=1)
    sm = jnp.concatenate([-sin, jnp.zeros((S, half), F32), zeros], axis=1)
    sp = jnp.concatenate([jnp.zeros((S, half), F32), sin, zeros], axis=1)
    return c, sm, sp


def _rope(t, c, sm, sp):
    return t * c + pltpu.roll(t, HEAD_DIM - ROPE_DIM // 2, 1) * sm + pltpu.roll(t, ROPE_DIM // 2, 1) * sp


def _rope_t(dy, c, sm, sp):
    return dy * c + pltpu.roll(dy * sm, ROPE_DIM // 2, 1) + pltpu.roll(dy * sp, HEAD_DIM - ROPE_DIM // 2, 1)


def _att_mask(i):
    qi = lax.broadcasted_iota(jnp.int32, (BLK, 2 * BLK), 0)
    kj = lax.broadcasted_iota(jnp.int32, (BLK, 2 * BLK), 1)
    diff = qi + BLK - kj
    first_key = jnp.where(i > 0, 0, BLK)
    return (diff >= 0) & (diff <= BLK) & (kj >= first_key)


def _res_rows(r, i, n, d):
    if d == 1:
        return pl.ds(pl.multiple_of(i * n, n), n)
    return pl.ds(r + i * (n * d), n, stride=d)


def _att_specs(S, gi):
    def sect(off):
        base = (off + gi * GROUP_W) // HEAD_DIM
        return _once((S, HEAD_DIM), lambda hh: (0, base + hh))

    tab = pl.BlockSpec((S, HEAD_DIM), lambda hh: (0, 0))
    head = pl.BlockSpec((S, HEAD_DIM), lambda hh: (0, hh))
    return sect, tab, head


def _each_residue(d, fn):
    if d == 1:
        fn(0)
    else:
        lax.fori_loop(0, d, lambda r, carry: (fn(r), carry)[1], 0)


def _att_fwd(qkv, tabs, gi, d, name, comm=()):
    S = qkv.shape[0]
    L = S // d
    sect, tab, head = _att_specs(S, gi)
    nblk = L // BLK
    scale = HEAD_DIM ** -0.5

    def body(q_ref, k_ref, v_ref, c_ref, sm_ref, sp_ref, o_ref, lse_ref, qr, kp, vp):
        kp[pl.ds(0, BLK), :] = jnp.zeros((BLK, HEAD_DIM), BF)
        vp[pl.ds(0, BLK), :] = jnp.zeros((BLK, HEAD_DIM), BF)

        def residue(r):
            res = _res_rows(r, 0, L, d)
            c, sm, sp = c_ref[res, :], sm_ref[res, :], sp_ref[res, :]
            qr[...] = _rope(q_ref[res, :], c, sm, sp).astype(BF)
            kp[pl.ds(BLK, L), :] = _rope(k_ref[res, :], c, sm, sp).astype(BF)
            vp[pl.ds(BLK, L), :] = v_ref[res, :].astype(BF)

            def blk(i, carry):
                r0 = pl.multiple_of(i * BLK, BLK)
                s = _nt(qr[pl.ds(r0, BLK), :], kp[pl.ds(r0, 2 * BLK), :]) * scale
                s = jnp.where(_att_mask(i), s, NEG)
                m = jnp.max(s, axis=-1, keepdims=True)
                p = jnp.exp(s - m)
                l = jnp.sum(p, axis=-1, keepdims=True)
                out = _res_rows(r, i, BLK, d)
                o_ref[out, :] = _nn(p.astype(BF), vp[pl.ds(r0, 2 * BLK), :]) / l
                lse_ref[out, :] = jnp.broadcast_to(m + jnp.log(l), (BLK, HEAD_DIM))
                return carry

            lax.fori_loop(0, nblk, blk, 0, unroll=min(4, nblk))

        _each_residue(d, residue)

    shp = jax.ShapeDtypeStruct((S, GROUP_W), F32)
    return _call(body, grid=(HEADS_PER_GROUP,), name=name, args=[qkv, qkv, qkv, *tabs], comm=comm,
                 in_specs=[sect(Q_OFF), sect(K_OFF), sect(V_OFF), tab, tab, tab], out_specs=[head, head], out_shape=[shp, shp],
                 scratch=[pltpu.VMEM((L, HEAD_DIM), BF), pltpu.VMEM((L + BLK, HEAD_DIM), BF), pltpu.VMEM((L + BLK, HEAD_DIM), BF)])


def _att_combine(os, lses, name):
    S = os[0].shape[0]
    tm = _tile(S, 512)

    def body(o0, o1, o2, l0, l1, l2, oa_ref, lse_ref):
        a, b, c = l0[...], l1[...], l2[...]
        mx = jnp.maximum(jnp.maximum(a, b), c)
        wa, wb, wc = jnp.exp(a - mx), jnp.exp(b - mx), jnp.exp(c - mx)
        den = wa + wb + wc
        oa_ref[...] = ((wa * o0[...] + wb * o1[...] + wc * o2[...]) / den).astype(BF)
        lse_ref[...] = mx + jnp.log(den)

    row = pl.BlockSpec((tm, GROUP_W), lambda i: (i, 0))
    return _call(body, grid=(S // tm,), name=name, args=[*os, *lses], in_specs=[row] * 6, out_specs=[row, row],
                 out_shape=[jax.ShapeDtypeStruct((S, GROUP_W), BF), jax.ShapeDtypeStruct((S, GROUP_W), F32)])[0]


def _att_bwd(qkv, tabs, do, lse, dvec, gi, d, name, comm=()):
    S = qkv.shape[0]
    L = S // d
    sect, tab, head = _att_specs(S, gi)
    stat = _once((S, HEAD_DIM), lambda hh: (0, hh))
    nblk = L // BLK
    scale = HEAD_DIM ** -0.5

    def body(q_ref, k_ref, v_ref, c_ref, sm_ref, sp_ref, do_ref, lse_ref, dv_ref, dq_out, dk_out, dv_out, qr, kp, vp, dkp, dvp):
        kp[pl.ds(0, BLK), :] = jnp.zeros((BLK, HEAD_DIM), BF)
        vp[pl.ds(0, BLK), :] = jnp.zeros((BLK, HEAD_DIM), BF)

        def residue(r):
            res = _res_rows(r, 0, L, d)
            c, sm, sp = c_ref[res, :], sm_ref[res, :], sp_ref[res, :]
            qr[...] = _rope(q_ref[res, :], c, sm, sp).astype(BF)
            kp[pl.ds(BLK, L), :] = _rope(k_ref[res, :], c, sm, sp).astype(BF)
            vp[pl.ds(BLK, L), :] = v_ref[res, :].astype(BF)
            dkp[...] = jnp.zeros_like(dkp)
            dvp[...] = jnp.zeros_like(dvp)

            def blk(i, carry):
                r0 = pl.multiple_of(i * BLK, BLK)
                rows, win, pos = pl.ds(r0, BLK), pl.ds(r0, 2 * BLK), _res_rows(r, i, BLK, d)
                q, kw, vw, dob = qr[rows, :], kp[win, :], vp[win, :], do_ref[pos, :].astype(BF)
                s = jnp.where(_att_mask(i), _nt(q, kw) * scale, NEG)
                p = jnp.exp(s - lse_ref[pos, :][:, :1])
                ds = p * (_nt(dob, vw) - dv_ref[pos, :][:, :1]) * scale
                dsb = ds.astype(BF)
                dq_out[pos, :] = _rope_t(_nn(dsb, kw), c_ref[pos, :], sm_ref[pos, :], sp_ref[pos, :])
                dkp[win, :] += _tn(dsb, q)
                dvp[win, :] += _tn(p.astype(BF), dob)
                return carry

            lax.fori_loop(0, nblk, blk, 0, unroll=2)
            dk_out[res, :] = _rope_t(dkp[pl.ds(BLK, L), :], c, sm, sp)
            dv_out[res, :] = dvp[pl.ds(BLK, L), :]

        _each_residue(d, residue)

    shp = jax.ShapeDtypeStruct((S, GROUP_W), F32)
    return _call(body, grid=(HEADS_PER_GROUP,), name=name, args=[qkv, qkv, qkv, *tabs, do, lse, dvec], comm=comm,
                 in_specs=[sect(Q_OFF), sect(K_OFF), sect(V_OFF), tab, tab, tab, stat, stat, stat],
                 out_specs=[head, head, head], out_shape=[shp, shp, shp],
                 scratch=[pltpu.VMEM((L, HEAD_DIM), BF), pltpu.VMEM((L + BLK, HEAD_DIM), BF), pltpu.VMEM((L + BLK, HEAD_DIM), BF),
                          pltpu.VMEM((L + BLK, HEAD_DIM), F32), pltpu.VMEM((L + BLK, HEAD_DIM), F32)])


def _sg_parts(u_ref, vs_ref, g_ref, b_ref):
    uv = u_ref[...].astype(F32)
    vv = vs_ref[...].astype(F32)
    cv = _gauss_cdf(vv)
    vg = vv * cv
    mu = jnp.mean(vg, axis=-1, keepdims=True)
    vc = vg - mu
    rs = lax.rsqrt(jnp.mean(vc * vc, axis=-1, keepdims=True) + LN_EPS)
    y = vc * rs
    return uv, vv, cv, rs, y, y * g_ref[...] + b_ref[...]


def _sg_wmask():
    t = lax.broadcasted_iota(jnp.int32, (BLK, BLK), 0)
    s = lax.broadcasted_iota(jnp.int32, (BLK, BLK), 1)
    return s <= t


def _sg_fwd(proj, sgw, sgbT, lng, lnb, name):
    S, P = proj.shape

    def body(u_ref, vs_ref, w_ref, bt_ref, g_ref, b_ref, z_ref):
        uv, _, _, _, _, vln = _sg_parts(u_ref, vs_ref, g_ref, b_ref)
        ug = uv * _gauss_cdf(uv)
        vb = vln.astype(BF)
        mask = _sg_wmask()
        bt = bt_ref[...]
        for g in range(SG_GROUPS):
            cols = slice(g * BLK, (g + 1) * BLK)
            w = jnp.where(mask, w_ref[g], 0.0).astype(BF)
            sp = _nn(w, vb[:, cols]) + bt[:, g:g + 1]
            z_ref[:, cols] = (ug[:, cols] * sp).astype(BF)

    tile = lambda off: pl.BlockSpec((BLK, SG_W), lambda i: (i, off // SG_W))
    full = lambda shape: pl.BlockSpec(shape, lambda i: (0,) * len(shape))
    return _call(body, grid=(S // BLK,), name=name, args=[proj, proj, sgw, sgbT, lng, lnb],
                 in_specs=[tile(R_U), tile(R_VS), full((SG_GROUPS, BLK, BLK)), full((BLK, BLK)), full((1, SG_W)), full((1, SG_W))],
                 out_specs=[pl.BlockSpec((BLK, SG_W), lambda i: (i, 0))], out_shape=[jax.ShapeDtypeStruct((S, SG_W), BF)])[0][0]


def _sg_bwd(proj, dz, sgw, sgbT, lng, lnb, name):
    S, P = proj.shape

    def body(u_ref, vs_ref, dz_ref, w_ref, bt_ref, g_ref, b_ref, du_ref, dvs_ref, dw_ref, dbt_ref, dg_ref, db_ref, dvln):
        @pl.when(pl.program_id(0) == 0)
        def _():
            dw_ref[...] = jnp.zeros_like(dw_ref)
            dbt_ref[...] = jnp.zeros_like(dbt_ref)
            dg_ref[...] = jnp.zeros_like(dg_ref)
            db_ref[...] = jnp.zeros_like(db_ref)

        uv, vv, cv, rs, y, vln = _sg_parts(u_ref, vs_ref, g_ref, b_ref)
        cu = _gauss_cdf(uv)
        ug = uv * cu
        dug = _gelu_grad(uv, cu)
        vb = vln.astype(BF)
        dzv = dz_ref[...].astype(F32)
        dsp = dzv * ug
        dspb = dsp.astype(BF)
        mask = _sg_wmask()
        bt = bt_ref[...]
        lane = lax.broadcasted_iota(jnp.int32, (BLK, BLK), 1)
        dbt = jnp.zeros((BLK, BLK), F32)
        for g in range(SG_GROUPS):
            cols = slice(g * BLK, (g + 1) * BLK)
            w = jnp.where(mask, w_ref[g], 0.0).astype(BF)
            sp = _nn(w, vb[:, cols]) + bt[:, g:g + 1]
            du_ref[:, cols] = (dzv[:, cols] * sp * dug[:, cols]).astype(BF)
            dw_ref[g] += jnp.where(mask, _nt(dspb[:, cols], vb[:, cols]), 0.0)
            dbt = dbt + jnp.where(lane == g, jnp.sum(dsp[:, cols], axis=-1, keepdims=True), 0.0)
            dvln[:, cols] = _tn(w, dspb[:, cols])
        dbt_ref[...] += dbt
        dvl = dvln[...]
        dg_ref[...] += jnp.sum(dvl * y, axis=0, keepdims=True)
        db_ref[...] += jnp.sum(dvl, axis=0, keepdims=True)
        dy = dvl * g_ref[...]
        dvg = rs * (dy - jnp.mean(dy, axis=-1, keepdims=True) - y * jnp.mean(dy * y, axis=-1, keepdims=True))
        dvs_ref[...] = (dvg * _gelu_grad(vv, cv)).astype(BF)

    tile = lambda off: pl.BlockSpec((BLK, SG_W), lambda i: (i, off // SG_W))
    full = lambda shape: pl.BlockSpec(shape, lambda i: (0,) * len(shape))
    row = pl.BlockSpec((BLK, SG_W), lambda i: (i, 0))
    return _call(body, grid=(S // BLK,), name=name, args=[proj, proj, dz, sgw, sgbT, lng, lnb],
                 in_specs=[tile(R_U), tile(R_VS), row, full((SG_GROUPS, BLK, BLK)), full((BLK, BLK)), full((1, SG_W)), full((1, SG_W))],
                 out_specs=[row, row, full((SG_GROUPS, BLK, BLK)), full((BLK, BLK)), full((1, SG_W)), full((1, SG_W))],
                 out_shape=[jax.ShapeDtypeStruct((S, SG_W), BF), jax.ShapeDtypeStruct((S, SG_W), BF),
                            jax.ShapeDtypeStruct((SG_GROUPS, BLK, BLK), F32), jax.ShapeDtypeStruct((BLK, BLK), F32),
                            jax.ShapeDtypeStruct((1, SG_W), F32), jax.ShapeDtypeStruct((1, SG_W), F32)],
                 scratch=[pltpu.VMEM((BLK, SG_W), F32)])[0]


def _gate_merge(oatt, z, watt, wsg, proj, name, comm=()):
    S = oatt.shape[0]
    nb, _, Db = watt.shape
    D = nb * Db
    tm = _tile(S, 512)
    half = D // 2
    ga, gs = R_GA // half, (R_GA + D) // half

    def body(oa_ref, z_ref, wa_ref, ws_ref, ga0, ga1, gs0, gs1, ya_ref, ys_ref, mg_ref):
        oa, zv = oa_ref[...], z_ref[...]
        for j in range(nb):
            cols = slice(j * Db, (j + 1) * Db)
            g_a, g_s = (ga0, gs0) if j < nb // 2 else (ga1, gs1)
            gcols = slice((j % (nb // 2)) * Db, (j % (nb // 2) + 1) * Db)
            ya = _nn(oa, wa_ref[j])
            ys = _nn(zv, ws_ref[j])
            ya_ref[:, cols] = ya.astype(BF)
            ys_ref[:, cols] = ys.astype(BF)
            mg_ref[:, cols] = (jax.nn.sigmoid(g_a[:, gcols].astype(F32)) * ya + jax.nn.sigmoid(g_s[:, gcols].astype(F32)) * ys).astype(BF)

    out = pl.BlockSpec((tm, D), lambda i: (i, 0))
    gate = lambda b: pl.BlockSpec((tm, half), lambda i: (i, b))
    shp = jax.ShapeDtypeStruct((S, D), BF)
    return _call(body, grid=(S // tm,), name=name, args=[oatt, z, watt, wsg, proj, proj, proj, proj], comm=comm,
                 in_specs=[pl.BlockSpec((tm, GROUP_W), lambda i: (i, 0)), pl.BlockSpec((tm, SG_W), lambda i: (i, 0)),
                           pl.BlockSpec((nb, GROUP_W, Db), lambda i: (0, 0, 0)), pl.BlockSpec((nb, SG_W, Db), lambda i: (0, 0, 0)),
                           gate(ga), gate(ga + 1), gate(gs), gate(gs + 1)],
                 out_specs=[out, out, out], out_shape=[shp, shp, shp])


def _mix_out(merged, wout, x, gn, name):
    S, D = x.shape
    tm = _tile(S, 256)

    def body(m_ref, w_ref, x_ref, gn_ref, xo_ref, hn_ref):
        xo = x_ref[...] + _nn(m_ref[...], w_ref[...])
        r = lax.rsqrt(jnp.mean(xo * xo, axis=-1, keepdims=True) + NORM_EPS)
        xo_ref[...] = xo
        hn_ref[...] = (xo * r * gn_ref[...]).astype(BF)

    row = pl.BlockSpec((tm, D), lambda i: (i, 0))
    return _call(body, grid=(S // tm,), name=name, args=[merged, wout, x, gn],
                 in_specs=[row, pl.BlockSpec((D, D), lambda i: (0, 0)), row, pl.BlockSpec((1, D), lambda i: (0, 0))],
                 out_specs=[row, row], out_shape=[jax.ShapeDtypeStruct((S, D), F32), jax.ShapeDtypeStruct((S, D), BF)])[0]


def _mix_bwd_gate(dmix, wout, ya, ys, proj, name):
    S, D = dmix.shape
    tm, tn = _tile(S, 256), min(512, D // 2)
    half = D // 2
    ga, gs = R_GA // half, (R_GA + D) // half

    def body(dm_ref, w_ref, ya_ref, ys_ref, ga0, ga1, gs0, gs1, dya_ref, dys_ref, dga_ref, dgs_ref):
        dmv = dm_ref[...]
        for c0 in range(0, D, tn):
            cols = slice(c0, c0 + tn)
            g_a, g_s = (ga0, gs0) if c0 < half else (ga1, gs1)
            gcols = slice(c0 % half, c0 % half + tn)
            dm = _nt(dmv, w_ref[cols, :])
            sa = jax.nn.sigmoid(g_a[:, gcols].astype(F32))
            ss = jax.nn.sigmoid(g_s[:, gcols].astype(F32))
            dya_ref[:, cols] = (dm * sa).astype(BF)
            dys_ref[:, cols] = (dm * ss).astype(BF)
            dga_ref[:, cols] = (dm * ya_ref[:, cols].astype(F32) * sa * (1.0 - sa)).astype(BF)
            dgs_ref[:, cols] = (dm * ys_ref[:, cols].astype(F32) * ss * (1.0 - ss)).astype(BF)

    row = pl.BlockSpec((tm, D), lambda i: (i, 0))
    gate = lambda b: pl.BlockSpec((tm, half), lambda i: (i, b))
    shp = jax.ShapeDtypeStruct((S, D), BF)
    return _call(body, grid=(S // tm,), name=name, args=[dmix, wout, ya, ys, proj, proj, proj, proj],
                 in_specs=[row, pl.BlockSpec((D, D), lambda i: (0, 0)), row, row, gate(ga), gate(ga + 1), gate(gs), gate(gs + 1)],
                 out_specs=[row] * 4, out_shape=[shp] * 4)[0]


def _att_sg_dout(dya, dys, watt, wsg, oatt, name, comm=()):
    S, D = dya.shape
    nb, _, Db = watt.shape
    tm = _tile(S, 512)

    def body(dya_ref, dys_ref, wa_ref, ws_ref, oa_ref, do_ref, dz_ref, dvec_ref):
        def back(dy_ref, w_ref, rows):
            tot = None
            for j in range(nb):
                part = _nt(dy_ref[:, j * Db:(j + 1) * Db], w_ref[j, rows, :])
                tot = part if tot is None else tot + part
            return tot

        dov = back(dya_ref, wa_ref, slice(0, GROUP_W))
        do_ref[...] = dov
        for c0 in range(0, SG_W, GROUP_W):
            dz_ref[:, c0:c0 + GROUP_W] = back(dys_ref, ws_ref, slice(c0, c0 + GROUP_W)).astype(BF)
        prod = dov * oa_ref[...].astype(F32)
        for hh in range(HEADS_PER_GROUP):
            cols = slice(hh * HEAD_DIM, (hh + 1) * HEAD_DIM)
            dvec_ref[:, cols] = jnp.broadcast_to(jnp.sum(prod[:, cols], axis=-1, keepdims=True), (tm, HEAD_DIM))

    row = pl.BlockSpec((tm, D), lambda i: (i, 0))
    att = pl.BlockSpec((tm, GROUP_W), lambda i: (i, 0))
    return _call(body, grid=(S // tm,), name=name, args=[dya, dys, watt, wsg, oatt], comm=comm,
                 in_specs=[row, row, pl.BlockSpec((nb, GROUP_W, Db), lambda i: (0, 0, 0)), pl.BlockSpec((nb, SG_W, Db), lambda i: (0, 0, 0)), att],
                 out_specs=[att, pl.BlockSpec((tm, SG_W), lambda i: (i, 0)), att],
                 out_shape=[jax.ShapeDtypeStruct((S, GROUP_W), F32), jax.ShapeDtypeStruct((S, SG_W), BF), jax.ShapeDtypeStruct((S, GROUP_W), F32)])[0]


def _small_allreduce(pack, name):
    R = pack.shape[0]

    def body(p_ref, o_ref, gath, send, recv):
        x, y, c = _place()
        me = 4 * x + 2 * y + c
        gath[me] = p_ref[...]
        copies = []
        for r in range(1, N_DEV):
            px, py, pc = _flip(x, r & 4), _flip(y, r & 2), _flip(c, r & 1)
            peer = 4 * px + 2 * py + pc
            mk = lambda dst: pltpu.make_async_remote_copy(src_ref=p_ref, dst_ref=dst, send_sem=send.at[r - 1], recv_sem=recv.at[r - 1],
                                                          device_id=(px, py, pc), device_id_type=MESH)
            snd = mk(gath.at[me])
            snd.start()
            copies.append((snd, mk(gath.at[peer])))
        for snd, rcv in copies:
            rcv.wait_recv()
            snd.wait_send()
        acc = gath[0]
        for s in range(1, N_DEV):
            acc = acc + gath[s]
        o_ref[...] = acc

    vm = pl.BlockSpec(memory_space=pltpu.VMEM)
    return pl.pallas_call(
        body, name=name, in_specs=[vm], out_specs=vm, out_shape=jax.ShapeDtypeStruct(pack.shape, F32),
        scratch_shapes=[pltpu.VMEM((N_DEV, R, 128), F32), pltpu.SemaphoreType.DMA((7,)), pltpu.SemaphoreType.DMA((7,))],
        compiler_params=pltpu.CompilerParams(vmem_limit_bytes=VMEM_LIMIT),
    )(pack)


def _row_tile(R, C, elems=262144):
    fits = [t for t in range(16, R + 1, 16) if R % t == 0 and t * C <= elems]
    return max(fits) if fits else R


def _pair_add(parts, other, name):
    _, R, C = parts.shape
    tr = _row_tile(R, C, 1048576)

    def body(c_ref, p_ref, o_ref, s_ref):
        s_ref[0] = (p_ref[0].astype(F32) + o_ref[0].astype(F32)).astype(BF)

    core = lax.axis_index("c").astype(jnp.int32).reshape(1)
    return pl.pallas_call(
        body, name=name,
        grid_spec=pltpu.PrefetchScalarGridSpec(
            num_scalar_prefetch=1, grid=(N_CHIP, R // tr),
            in_specs=[pl.BlockSpec((1, tr, C), lambda q, i, c: (2 * q + c[0], i, 0)), pl.BlockSpec((1, tr, C), lambda q, i, c: (q, i, 0))],
            out_specs=pl.BlockSpec((1, tr, C), lambda q, i, c: (q, i, 0))),
        out_shape=jax.ShapeDtypeStruct((N_CHIP, R, C), BF),
        compiler_params=pltpu.CompilerParams(dimension_semantics=("arbitrary", "arbitrary"), vmem_limit_bytes=VMEM_LIMIT),
    )(core, parts, other)


def _adamw(parts, w, m, v, name):
    ns, R, C = parts.shape
    tr = _row_tile(R, C, 524288)
    c1 = 1.0 - ADAM_B1 ** ADAM_STEP
    c2 = 1.0 - ADAM_B2 ** ADAM_STEP

    def body(p_ref, w_ref, m_ref, v_ref, g_ref, d_ref, nm_ref, nv_ref):
        g = p_ref[0].astype(F32)
        for s in range(1, ns):
            g = g + p_ref[s].astype(F32)
        mn = ADAM_B1 * m_ref[...] + (1.0 - ADAM_B1) * g
        vn = ADAM_B2 * v_ref[...] + (1.0 - ADAM_B2) * (g * g)
        g_ref[...] = g
        nm_ref[...] = mn
        nv_ref[...] = vn
        d_ref[...] = -ADAM_LR * ((mn / c1) / (jnp.sqrt(vn / c2) + ADAM_EPS) + ADAM_WD * w_ref[...])

    row = pl.BlockSpec((tr, C), lambda i: (i, 0))
    shp = jax.ShapeDtypeStruct((R, C), F32)
    return _call(body, grid=(R // tr,), name=name, args=[parts, w, m, v],
                 in_specs=[pl.BlockSpec((ns, tr, C), lambda i: (0, i, 0)), row, row, row], out_specs=[row] * 4, out_shape=[shp] * 4)[0]


def _pad_rows(a, rows):
    return jnp.pad(a, ((0, rows - a.shape[0]), (0, 0)))


def kernel(x, ffn1_norm, ffn1_w_gate, ffn1_w_up, ffn1_w_down, mix_norm, w_in, sg_ln_g, sg_ln_b, sg_w, sg_b, w_att_out, w_sg_out, w_out, ffn2_norm, ffn2_w_gate, ffn2_w_up, ffn2_w_down, final_norm, loss_target, m_ffn1_norm, m_ffn1_w_gate, m_ffn1_w_up, m_ffn1_w_down, m_mix_norm, m_w_in, m_sg_ln_g, m_sg_ln_b, m_sg_w, m_sg_b, m_w_att_out, m_w_sg_out, m_w_out, m_ffn2_norm, m_ffn2_w_gate, m_ffn2_w_up, m_ffn2_w_down, m_final_norm, v_ffn1_norm, v_ffn1_w_gate, v_ffn1_w_up, v_ffn1_w_down, v_mix_norm, v_w_in, v_sg_ln_g, v_sg_ln_b, v_sg_w, v_sg_b, v_w_att_out, v_w_sg_out, v_w_out, v_ffn2_norm, v_ffn2_w_gate, v_ffn2_w_up, v_ffn2_w_down, v_final_norm):
    S, D = x.shape[1], x.shape[2]
    Pb = w_in.shape[2]
    P = N_DEV * Pb
    assert P == GA_OFF + 2 * D and D % (N_DEV * 128) == 0 and S % (BLK * DILATIONS[-1]) == 0
    xs, tgt = x[0], loss_target[0]

    sharded = dict(ffn1_w_gate=ffn1_w_gate, ffn1_w_up=ffn1_w_up, ffn1_w_down=ffn1_w_down, w_in=w_in, w_att_out=w_att_out,
                   w_sg_out=w_sg_out, w_out=w_out, ffn2_w_gate=ffn2_w_gate, ffn2_w_up=ffn2_w_up, ffn2_w_down=ffn2_w_down)
    cols = ("ffn1_w_gate", "ffn1_w_up", "w_in", "ffn2_w_gate", "ffn2_w_up")
    local = lambda n, a: a[0].T if n in cols else a[0]
    back = lambda n, a: a.T[None] if n in cols else a[None]
    wloc = {n: local(n, w) for n, w in sharded.items()}
    sb = {n: w.astype(BF) for n, w in wloc.items()}

    (h1,), ((wg1,),) = _rms_fwd(xs, ffn1_norm, "rms1", comm=[_Gather([sb["ffn1_w_gate"]], 1.0, 1.0)])
    (g1,), ((wu1,),) = _ffn_gate(h1, wg1, "ffn1_gate", comm=[_Gather([sb["ffn1_w_up"]], 0.9, 0.55)])
    (u1, a1), ((wd1,),) = _ffn_up_act(h1, wu1, g1, "ffn1_up_act", comm=[_Gather([sb["ffn1_w_down"]], 0.9, 0.55)])
    (x1, h2), ((winT8,),) = _ffn_down_norm(a1, wd1, xs, mix_norm, "ffn1_down", comm=[_Gather([sb["w_in"]], 1.0, 0.7)])
    winT = winT8.reshape(P, D)
    (qkv, rest), ((wg2, wu2),) = _proj_split(h2, winT, _tile(S, 2048), 512, U_OFF, "proj",
                                           comm=[_Gather([sb["ffn2_w_gate"], sb["ffn2_w_up"]], 0.85, 0.5)])
    tabs = _rope_tables(S)
    rides = [[_Gather([sb["w_att_out"], sb["w_sg_out"]], 0.9, 0.5)], [_Gather([sb["w_out"]], 0.85, 0.45)], []]
    os, lses, late = [], [], []
    for gi, d in enumerate(DILATIONS):
        (o, l), got_here = _att_fwd(qkv, tabs, gi, d, f"att_fwd{gi}", comm=rides[gi])
        late += [w for g in got_here for w in g]
        os.append(o)
        lses.append(l)
    watt, wsg, wout8 = late
    wout = wout8.reshape(D, D)
    oatt, lse = _att_combine(os, lses, "att_combine")
    sgw = sg_w[0]
    sgbT = jnp.pad(sg_b[0].T, ((0, 0), (0, BLK - SG_GROUPS)))
    z = _sg_fwd(rest, sgw, sgbT, sg_ln_g, sg_ln_b, "sg_fwd")
    (ya, ys, merged), _ = _gate_merge(oatt, z, watt, wsg, rest, "gate_merge")
    x2, h3 = _mix_out(merged, wout, x1, ffn2_norm, "mix_out")
    (g3, u3, a3), ((wd2,),) = _ffn_up(h3, wg2, wu2, "ffn2_up", comm=[_Gather([sb["ffn2_w_down"]], 0.6, 0.35)])
    dx3, dyb3, d_final, loss_part = _ffn_down_loss(a3, wd2, x2, final_norm.reshape(1, D), tgt, "ffn2_down_loss")

    Fb = wg2.shape[1]
    Db = watt.shape[2]
    p_pad = -(-P // PROJ_TK) * PROJ_TK
    win_tail = _pad_rows(winT[p_pad - PROJ_TK:], PROJ_TK)
    (dg3, du3), _ = _ffn_bwd_act(dyb3, wd2, g3, u3, "ffn2_bwd_act")
    (dwd2,), _ = _ffn_dwd(a3, dyb3, "ffn2_dwd")
    (dwg2, dwu2), _ = _ffn_dwgu(h3, dg3, du3, "ffn2_dwgu")
    ffn2_parts = [dwd2, dwg2, dwu2]
    (dx2, dmixb, d_ffn2n), (ffn2_other,) = _ffn_dh(dg3, du3, wg2, wu2, x2, ffn2_norm, dx3, "ffn2_dh", comm=[_Swap(ffn2_parts)])
    ffn2_sums = [_pair_add(p, o, f"pair_ffn2_{i}") for i, (p, o) in enumerate(zip(ffn2_parts, ffn2_other))]

    dya, dys, dga, dgs = _mix_bwd_gate(dmixb, wout, ya, ys, rest, "mix_bwd_gate")
    (dwout,), _ = _mm_tn(merged, dmixb, _tile(D, 1024), _tile(D, 1024), _tile(S, 1024), False, "dw_out")
    do, dz, dvec = _att_sg_dout(dya, dys, watt, wsg, oatt, "att_sg_dout")
    (dwatt,), _ = _mm_tn(oatt, dya, GROUP_W, 2 * Db, _tile(S, 1024), Db, "dw_att")
    (dwsg,), _ = _mm_tn(z, dys, SG_W, 2 * Db, _tile(S, 1024), Db, "dw_sg")
    mix_parts = [dwout.reshape(N_DEV, D // N_DEV, D), dwatt, dwsg]
    du, dvs, d_sgw, d_sgbT, d_lng, d_lnb = _sg_bwd(rest, dz, sgw, sgbT, sg_ln_g, sg_ln_b, "sg_bwd")
    dqs, dks, dvs_att, ffn2_got = [], [], [], []
    for gi, d in enumerate(DILATIONS):
        ride = [_Ici([ffn2_sums[0]])] if gi == 2 else []
        (dq, dk, dv), got_here = _att_bwd(qkv, tabs, do, lse, dvec, gi, d, f"att_bwd{gi}", comm=ride)
        ffn2_got += [g[0] for g in got_here]
        dqs.append(dq)
        dks.append(dk)
        dvs_att.append(dv)
    dproj = jnp.concatenate([t.astype(BF) for t in dqs + dks + dvs_att] + [du, dvs, dga, dgs, jnp.zeros((S, p_pad - P), BF)], axis=1)
    (dx1, dyb1, d_mixn), (ffn2_rest, mix_other) = _dh_rms_bwd([(dproj, winT)], False, PROJ_TK, x1, mix_norm, dx2, 0.5, "proj_dh",
                                                            comm=[_Ici(ffn2_sums[1:]), _Swap(mix_parts)], tail=win_tail)
    ffn2_got += ffn2_rest
    mix_sums = [_pair_add(p, o, f"pair_mix_{i}") for i, (p, o) in enumerate(zip(mix_parts, mix_other))]
    (dwd1,), (mix_got,) = _ffn_dwd(a1, dyb1, "ffn1_dwd", comm=[_Ici(mix_sums)])
    rows = lambda a: a.reshape(-1, 128)
    pad8 = lambda a: _pad_rows(a, -(-a.shape[0] // 8) * 8)
    small = [("sg_w", rows(d_sgw), sg_w, m_sg_w, v_sg_w), ("mix_norm", rows(d_mixn), mix_norm, m_mix_norm, v_mix_norm),
             ("ffn2_norm", rows(d_ffn2n), ffn2_norm, m_ffn2_norm, v_ffn2_norm), ("final_norm", rows(d_final), final_norm, m_final_norm, v_final_norm),
             ("sg_ln_g", rows(d_lng), sg_ln_g, m_sg_ln_g, v_sg_ln_g), ("sg_ln_b", rows(d_lnb), sg_ln_b, m_sg_ln_b, v_sg_ln_b),
             ("sg_b", d_sgbT[:, :SG_GROUPS].T, sg_b, m_sg_b, v_sg_b)]
    gpack = jnp.concatenate([pad8(g) for _, g, _, _, _ in small] + [pad8(loss_part)], axis=0)
    (dwin,), ((wd1_other,), (gpacks,)) = _mm_tn(dproj, h2, 512, D, S, False, "dw_in", mrows=P,
                                              comm=[_Swap([dwd1]), _Spread([gpack])])
    dwin = dwin.reshape(N_DEV, Pb, D)
    wd1_sum = _pair_add(dwd1, wd1_other, "pair_wd1")
    (dg1, du1), ((wd1_got,), (win_other,)) = _ffn_bwd_act(dyb1, wd1, g1, u1, "ffn1_bwd_act", comm=[_Ici([wd1_sum]), _Swap([dwin])])
    win_sum = _pair_add(dwin, win_other, "pair_win")
    (dwg1, dwu1), ((win_got,),) = _ffn_dwgu(h1, dg1, du1, "ffn1_dwgu", comm=[_Ici([win_sum])])
    gu_parts = [dwg1, dwu1]
    gu_other = _comm_only(_Swap(gu_parts), "swap_gu1")
    gu_sums = [_pair_add(p, o, f"pair_gu1_{i}") for i, (p, o) in enumerate(zip(gu_parts, gu_other))]
    (dx0, _, d_ffn1n), (gu_got,) = _ffn_dh(dg1, du1, wg1, wu1, xs, ffn1_norm, dx1, "ffn1_dh", comm=[_Ici(gu_sums)])

    got = dict(ffn2_w_down=ffn2_got[0], ffn2_w_gate=ffn2_got[1], ffn2_w_up=ffn2_got[2], w_out=mix_got[0], w_att_out=mix_got[1],
               w_sg_out=mix_got[2], w_in=win_got, ffn1_w_gate=gu_got[0], ffn1_w_up=gu_got[1], ffn1_w_down=wd1_got)
    moments = dict(ffn1_w_gate=(m_ffn1_w_gate, v_ffn1_w_gate), ffn1_w_up=(m_ffn1_w_up, v_ffn1_w_up),
                   ffn1_w_down=(m_ffn1_w_down, v_ffn1_w_down), w_in=(m_w_in, v_w_in), w_att_out=(m_w_att_out, v_w_att_out),
                   w_sg_out=(m_w_sg_out, v_w_sg_out), w_out=(m_w_out, v_w_out), ffn2_w_gate=(m_ffn2_w_gate, v_ffn2_w_gate),
                   ffn2_w_up=(m_ffn2_w_up, v_ffn2_w_up), ffn2_w_down=(m_ffn2_w_down, v_ffn2_w_down))
    res = {}
    for n in sharded:
        mm, vv = moments[n]
        outs = _adamw(got[n], wloc[n], local(n, mm), local(n, vv), "adamw_" + n)
        res[n] = [back(n, o) for o in outs]

    zero8 = jnp.zeros((8, 128), F32)
    wpack = jnp.concatenate([pad8(rows(w)) for _, _, w, _, _ in small] + [zero8], axis=0)
    mpack = jnp.concatenate([pad8(rows(m)) for _, _, _, m, _ in small] + [zero8], axis=0)
    vpack = jnp.concatenate([pad8(rows(v)) for _, _, _, _, v in small] + [zero8], axis=0)
    packs = _adamw(gpacks, wpack, mpack, vpack, "adamw_small")
    off = 0
    for n, g, w, _, _ in small:
        r = g.shape[0]
        res[n] = [p[off:off + r].reshape(w.shape) for p in packs]
        off += -(-r // 8) * 8
    loss = packs[0][off, 0]
    g_first = _small_allreduce(rows(d_ffn1n), "allreduce_ffn1_norm")
    res["ffn1_norm"] = [p.reshape(ffn1_norm.shape) for p in
                        _adamw(g_first[None], rows(ffn1_norm), rows(m_ffn1_norm), rows(v_ffn1_norm), "adamw_ffn1_norm")]

    order = ["ffn1_norm", "ffn1_w_gate", "ffn1_w_up", "ffn1_w_down", "mix_norm", "w_in", "sg_ln_g", "sg_ln_b", "sg_w", "sg_b",
             "w_att_out", "w_sg_out", "w_out", "ffn2_norm", "ffn2_w_gate", "ffn2_w_up", "ffn2_w_down", "final_norm"]
    return (loss, dx0[None], *[res[n][0] for n in order], *[res[n][1] for n in order], *[res[n][2] for n in order],
            *[res[n][3] for n in order])
```

```python
import math

import jax
import jax.numpy as jnp
from jax import lax
from jax.experimental import pallas as pl
from jax.experimental.pallas import tpu as pltpu

BF = jnp.bfloat16
F32 = jnp.float32
MESH = pl.DeviceIdType.MESH
N_DEV = 8
N_CHIP = 4

HEAD_DIM = 128
HEADS_PER_GROUP = 4
GROUP_W = HEADS_PER_GROUP * HEAD_DIM
DILATIONS = (1, 4, 16)
ATT_W = len(DILATIONS) * GROUP_W
SG_W = 1536
SG_GROUPS = 12
BLK = 128
ROPE_DIM = 32
ROPE_THETA = 500000.0
NORM_EPS = 1e-6
LN_EPS = 1e-5
Q_OFF, K_OFF, V_OFF, U_OFF, VS_OFF, GA_OFF = 0, ATT_W, 2 * ATT_W, 3 * ATT_W, 3 * ATT_W + SG_W, 3 * ATT_W + 2 * SG_W

ADAM_LR, ADAM_B1, ADAM_B2, ADAM_EPS, ADAM_WD, ADAM_STEP = 0.001, 0.9, 0.999, 1e-08, 0.01, 10

VMEM_LIMIT = 56 * 1024 * 1024
NEG = -1e30
ANY = pl.BlockSpec(memory_space=pl.ANY)
EPI_ROWS = 128
ACC_COLS = 512
FFN_PAIR = 2
FFN_ROWS = 1024
PROJ_TK = 1536
R_U, R_VS, R_GA = 0, SG_W, 2 * SG_W


def _once(shape, index_map):
    return pl.BlockSpec(shape, index_map, pipeline_mode=pl.Buffered(1))


def _tile(n, pref):
    t = min(n, pref)
    while n % t:
        t //= 2
    return t


def _nt(a, b):
    return lax.dot_general(a, b, (((1,), (1,)), ((), ())), preferred_element_type=F32)


def _tn(a, b):
    return lax.dot_general(a, b, (((0,), (0,)), ((), ())), preferred_element_type=F32)


def _nn(a, b):
    return jnp.dot(a, b, preferred_element_type=F32)


def _acc_dots(acc_ref, terms, transposed_rhs=False):
    n = acc_ref.shape[1]
    width = min(n, ACC_COLS)
    for c0 in range(0, n, width):
        cols = slice(c0, c0 + width)
        tot = None
        for lhs, rhs in terms:
            part = _nt(lhs, rhs(cols)) if transposed_rhs else _nn(lhs, rhs(cols))
            tot = part if tot is None else tot + part
        acc_ref[:, cols] += tot


def _gauss_cdf(x):
    return 0.5 * (1.0 + lax.erf(x * (2.0 ** -0.5)))


def _gelu_grad(x, cdf):
    return cdf + x * jnp.exp(-0.5 * x * x) * (1.0 / math.sqrt(2.0 * math.pi))


def _place():
    x, y, c = lax.axis_index("x"), lax.axis_index("y"), lax.axis_index("c")
    return x, y, c


def _flip(v, bit):
    return 1 - v if bit else v


class _Gather:
    def __init__(self, shards, mid_frac=1.0, relay_frac=0.5):
        self.arrays = list(shards)
        self.relay_frac = relay_frac
        self.mid_frac = mid_frac
        nw = len(shards)
        self.out_shape = [jax.ShapeDtypeStruct((N_DEV,) + s.shape, s.dtype) for s in shards]
        self.scratch = [pltpu.SemaphoreType.DMA((nw, 7)), pltpu.SemaphoreType.DMA((nw, 7)), pltpu.SemaphoreType.DMA((nw,))]

    def _parts(self, ins, outs, sems):
        x, y, c = _place()
        send, recv, loc = sems
        south = c == 0
        near = (jnp.where(south, x, 1 - x), jnp.where(south, 1 - y, y), c)
        far = (jnp.where(south, 1 - x, x), jnp.where(south, y, 1 - y), c)
        diag = (1 - x, 1 - y, c)

        def copy(k, s, block, to, src=None):
            dst = outs[k].at[4 * block[0] + 2 * block[1] + block[2]]
            return pltpu.make_async_remote_copy(src_ref=dst if src is None else src, dst_ref=dst, send_sem=send.at[k, s],
                                                recv_sem=recv.at[k, s], device_id=to, device_id_type=MESH)

        def first(k):
            me = (x, y, c)
            return [copy(k, 0, me, (x, y, 1 - c), src=ins[k]), copy(k, 1, me, (1 - x, y, c), src=ins[k]),
                    copy(k, 2, me, (x, 1 - y, c), src=ins[k])]

        def local(k):
            return pltpu.make_async_copy(ins[k], outs[k].at[4 * x + 2 * y + c], loc.at[k])

        return x, y, c, near, far, diag, copy, first, local

    def start(self, ins, outs, sems):
        *_, first, local = self._parts(ins, outs, sems)
        for k in range(len(ins)):
            local(k).start()
            for cp in first(k):
                cp.start()

    def relay(self, ins, outs, sems):
        x, y, c, near, far, _, copy, _, _ = self._parts(ins, outs, sems)
        for k in range(len(ins)):
            copy(k, 2 - c, near, (x, y, c)).wait_recv()
            copy(k, 3, near, far).start()
            copy(k, 5 - c, near, (x, y, 1 - c)).start()

    def mid(self, ins, outs, sems):
        x, y, c, _, far, diag, copy, _, _ = self._parts(ins, outs, sems)
        for k in range(len(ins)):
            copy(k, 1 + c, far, (x, y, c)).wait_recv()
            copy(k, 4 + c, far, (x, y, 1 - c)).start()
            copy(k, 3, diag, (x, y, c)).wait_recv()
            copy(k, 6, diag, (x, y, 1 - c)).start()

    def finish(self, ins, outs, sems):
        x, y, c, near, _, _, copy, first, local = self._parts(ins, outs, sems)
        sib = (x, y, 1 - c)
        for k in range(len(ins)):
            copy(k, 0, sib, (x, y, c)).wait_recv()
            copy(k, 4, (1 - x, y, 1 - c), (x, y, c)).wait_recv()
            copy(k, 5, (x, 1 - y, 1 - c), (x, y, c)).wait_recv()
            copy(k, 6, (1 - x, 1 - y, 1 - c), (x, y, c)).wait_recv()
        for k in range(len(ins)):
            for cp in first(k):
                cp.wait_send()
            for s in (3, 4, 5, 6):
                copy(k, s, near, sib).wait_send()
            local(k).wait()


class _Swap:
    def __init__(self, parts):
        self.arrays = list(parts)
        nw = len(parts)
        self.out_shape = [jax.ShapeDtypeStruct((N_CHIP,) + p.shape[1:], p.dtype) for p in parts]
        self.scratch = [pltpu.SemaphoreType.DMA((nw, N_CHIP)), pltpu.SemaphoreType.DMA((nw, N_CHIP))]

    def _copy(self, ins, outs, sems, k, q):
        x, y, c = _place()
        return pltpu.make_async_remote_copy(src_ref=ins[k].at[2 * q + 1 - c], dst_ref=outs[k].at[q], send_sem=sems[0].at[k, q],
                                            recv_sem=sems[1].at[k, q], device_id=(x, y, 1 - c), device_id_type=MESH)

    mid_frac = None

    def start(self, ins, outs, sems):
        for k in range(len(ins)):
            for q in range(N_CHIP):
                self._copy(ins, outs, sems, k, q).start()

    def finish(self, ins, outs, sems):
        for k in range(len(ins)):
            for q in range(N_CHIP):
                self._copy(ins, outs, sems, k, q).wait()


class _Ici:
    mid_frac = None

    def __init__(self, sums):
        self.arrays = list(sums)
        nw = len(sums)
        self.out_shape = [jax.ShapeDtypeStruct(s.shape, s.dtype) for s in sums]
        self.scratch = [pltpu.SemaphoreType.DMA((nw, 3)), pltpu.SemaphoreType.DMA((nw, 3)), pltpu.SemaphoreType.DMA((nw,))]

    def _copies(self, ins, outs, sems, k):
        x, y, c = _place()
        myq = 2 * x + y
        out = []
        for r in range(1, N_CHIP):
            px, py = _flip(x, r & 2), _flip(y, r & 1)
            pq = 2 * px + py
            mk = lambda dst: pltpu.make_async_remote_copy(src_ref=ins[k].at[pq], dst_ref=dst, send_sem=sems[0].at[k, r - 1],
                                                          recv_sem=sems[1].at[k, r - 1], device_id=(px, py, c), device_id_type=MESH)
            out.append((mk(outs[k].at[myq]), mk(outs[k].at[pq])))
        return out, pltpu.make_async_copy(ins[k].at[myq], outs[k].at[myq], sems[2].at[k])

    def start(self, ins, outs, sems):
        for k in range(len(ins)):
            remote, local = self._copies(ins, outs, sems, k)
            local.start()
            for snd, _ in remote:
                snd.start()

    def finish(self, ins, outs, sems):
        for k in range(len(ins)):
            remote, local = self._copies(ins, outs, sems, k)
            for snd, rcv in remote:
                rcv.wait_recv()
                snd.wait_send()
            local.wait()


class _Spread:
    mid_frac = None

    def __init__(self, arrays):
        self.arrays = list(arrays)
        nw = len(arrays)
        self.out_shape = [jax.ShapeDtypeStruct((N_DEV,) + a.shape, a.dtype) for a in arrays]
        self.scratch = [pltpu.SemaphoreType.DMA((nw, 7)), pltpu.SemaphoreType.DMA((nw, 7)), pltpu.SemaphoreType.DMA((nw,))]

    def _copies(self, ins, outs, sems, k):
        x, y, c = _place()
        me = 4 * x + 2 * y + c
        out = []
        for r in range(1, N_DEV):
            px, py, pc = _flip(x, r & 4), _flip(y, r & 2), _flip(c, r & 1)
            peer = 4 * px + 2 * py + pc
            mk = lambda dst: pltpu.make_async_remote_copy(src_ref=ins[k], dst_ref=dst, send_sem=sems[0].at[k, r - 1],
                                                          recv_sem=sems[1].at[k, r - 1], device_id=(px, py, pc), device_id_type=MESH)
            out.append((mk(outs[k].at[me]), mk(outs[k].at[peer])))
        return out, pltpu.make_async_copy(ins[k], outs[k].at[me], sems[2].at[k])

    def start(self, ins, outs, sems):
        for k in range(len(ins)):
            remote, local = self._copies(ins, outs, sems, k)
            local.start()
            for snd, _ in remote:
                snd.start()

    def finish(self, ins, outs, sems):
        for k in range(len(ins)):
            remote, local = self._copies(ins, outs, sems, k)
            for snd, rcv in remote:
                rcv.wait_recv()
                snd.wait_send()
            local.wait()


def _call(body, *, grid, in_specs, out_specs, out_shape, name, args, scratch=(), comm=()):
    comm = list(comm)
    n_in, n_out, n_scr = len(in_specs), len(out_specs), len(scratch)
    total = math.prod(grid) if grid else 1

    def wrapped(*refs):
        p = n_in
        cin = []
        for cm in comm:
            cin.append(refs[p:p + len(cm.arrays)])
            p += len(cm.arrays)
        own_out = refs[p:p + n_out]
        p += n_out
        cout = []
        for cm in comm:
            cout.append(refs[p:p + len(cm.arrays)])
            p += len(cm.arrays)
        own_scr = refs[p:p + n_scr]
        p += n_scr
        csem = []
        for cm in comm:
            csem.append(refs[p:p + len(cm.scratch)])
            p += len(cm.scratch)
        step = 0
        for axis, g in enumerate(grid):
            step = step * g + pl.program_id(axis)

        def at(when, what):
            if total == 1:
                what()
            else:
                pl.when(step == when)(what)

        def starts():
            for cm, i, o, s in zip(comm, cin, cout, csem):
                cm.start(i, o, s)

        def finishes():
            for cm, i, o, s in zip(comm, cin, cout, csem):
                cm.finish(i, o, s)

        if comm:
            at(0, starts)
        if body is not None:
            body(*refs[:n_in], *own_out, *own_scr)
        for cm, i, o, s in zip(comm, cin, cout, csem):
            if cm.mid_frac is not None:
                at(min(total - 1, int(total * cm.relay_frac)), lambda cm=cm, i=i, o=o, s=s: cm.relay(i, o, s))
                at(min(total - 1, int(total * cm.mid_frac)), lambda cm=cm, i=i, o=o, s=s: cm.mid(i, o, s))
        if comm:
            at(total - 1, finishes)

    kw = dict(grid=tuple(grid)) if grid else {}
    outs = pl.pallas_call(
        wrapped, name=name, **kw,
        in_specs=list(in_specs) + [ANY for cm in comm for _ in cm.arrays],
        out_specs=list(out_specs) + [ANY for cm in comm for _ in cm.arrays],
        out_shape=list(out_shape) + [s for cm in comm for s in cm.out_shape],
        scratch_shapes=list(scratch) + [s for cm in comm for s in cm.scratch],
        compiler_params=pltpu.CompilerParams(dimension_semantics=("arbitrary",) * len(grid), vmem_limit_bytes=VMEM_LIMIT),
    )(*args, *[a for cm in comm for a in cm.arrays])
    own, p, per = list(outs[:n_out]), n_out, []
    for cm in comm:
        per.append(list(outs[p:p + len(cm.arrays)]))
        p += len(cm.arrays)
    return own, per


def _comm_only(cm, name):
    return _call(None, grid=(), in_specs=[], out_specs=[], out_shape=[], name=name, args=[], comm=[cm])[1][0]


def _rms_fwd(x, g, name, comm=()):
    S, D = x.shape
    tm = _tile(S, 512)

    def body(x_ref, g_ref, o_ref):
        xv = x_ref[...]
        r = lax.rsqrt(jnp.mean(xv * xv, axis=-1, keepdims=True) + NORM_EPS)
        o_ref[...] = (xv * r * g_ref[...]).astype(BF)

    return _call(body, grid=(S // tm,), name=name, args=[x, g], comm=comm,
                 in_specs=[pl.BlockSpec((tm, D), lambda i: (i, 0)), pl.BlockSpec((1, D), lambda i: (0, 0))],
                 out_specs=[pl.BlockSpec((tm, D), lambda i: (i, 0))], out_shape=[jax.ShapeDtypeStruct((S, D), BF)])


def _ffn_up(h, wg, wu, name, comm=()):
    S, D = h.shape
    nb, Fb, _ = wg.shape
    tm = _tile(S, FFN_ROWS)

    def body(h_ref, wg_ref, wu_ref, g_ref, u_ref, a_ref):
        hv = h_ref[...]
        g = _nt(hv, wg_ref[0])
        u = _nt(hv, wu_ref[0])
        g_ref[0] = g.astype(BF)
        u_ref[0] = u.astype(BF)
        a_ref[0] = (g * jax.nn.sigmoid(g) * u).astype(BF)

    act = pl.BlockSpec((1, tm, Fb), lambda j, i: (j, i, 0))
    w = pl.BlockSpec((1, Fb, D), lambda j, i: (j, 0, 0))
    shp = jax.ShapeDtypeStruct((nb, S, Fb), BF)
    return _call(body, grid=(nb, S // tm), name=name, args=[h, wg, wu], comm=comm,
                 in_specs=[pl.BlockSpec((tm, D), lambda j, i: (i, 0)), w, w], out_specs=[act, act, act], out_shape=[shp, shp, shp])


def _ffn_gate(h, wg, name, comm=()):
    S, D = h.shape
    nb, Fb, _ = wg.shape
    tm = _tile(S, FFN_ROWS)

    def body(h_ref, wg_ref, g_ref):
        g_ref[0] = _nt(h_ref[...], wg_ref[0]).astype(BF)

    act = pl.BlockSpec((1, tm, Fb), lambda j, i: (j, i, 0))
    return _call(body, grid=(nb, S // tm), name=name, args=[h, wg], comm=comm,
                 in_specs=[pl.BlockSpec((tm, D), lambda j, i: (i, 0)), pl.BlockSpec((1, Fb, D), lambda j, i: (j, 0, 0))],
                 out_specs=[act], out_shape=[jax.ShapeDtypeStruct((nb, S, Fb), BF)])


def _ffn_up_act(h, wu, g, name, comm=()):
    S, D = h.shape
    nb, Fb, _ = wu.shape
    tm = _tile(S, FFN_ROWS)

    def body(h_ref, wu_ref, g_ref, u_ref, a_ref):
        u = _nt(h_ref[...], wu_ref[0])
        gv = g_ref[0].astype(F32)
        u_ref[0] = u.astype(BF)
        a_ref[0] = (gv * jax.nn.sigmoid(gv) * u).astype(BF)

    act = pl.BlockSpec((1, tm, Fb), lambda j, i: (j, i, 0))
    shp = jax.ShapeDtypeStruct((nb, S, Fb), BF)
    return _call(body, grid=(nb, S // tm), name=name, args=[h, wu, g], comm=comm,
                 in_specs=[pl.BlockSpec((tm, D), lambda j, i: (i, 0)), pl.BlockSpec((1, Fb, D), lambda j, i: (j, 0, 0)), act],
                 out_specs=[act, act], out_shape=[shp, shp])


def _ffn_down_norm(a, wd, x, gn, name, comm=()):
    nb, S, Fb = a.shape
    D = wd.shape[2]
    tm = _tile(S, 512)

    nj = nb // FFN_PAIR

    def body(a_ref, wd_ref, x_ref, gn_ref, xo_ref, hn_ref, acc_ref):
        j = pl.program_id(1)

        @pl.when(j == 0)
        def _():
            acc_ref[...] = jnp.zeros_like(acc_ref)

        _acc_dots(acc_ref, [(a_ref[b], lambda cols, b=b: wd_ref[b, :, cols]) for b in range(FFN_PAIR)])

        @pl.when(j == nj - 1)
        def _():
            def chunk(t, carry):
                rows = pl.ds(pl.multiple_of(t * EPI_ROWS, EPI_ROWS), EPI_ROWS)
                xo = x_ref[rows, :] + 0.5 * acc_ref[rows, :]
                r = lax.rsqrt(jnp.mean(xo * xo, axis=-1, keepdims=True) + NORM_EPS)
                xo_ref[rows, :] = xo
                hn_ref[rows, :] = (xo * r * gn_ref[...]).astype(BF)
                return carry

            lax.fori_loop(0, tm // EPI_ROWS, chunk, 0)

    row = pl.BlockSpec((tm, D), lambda i, j: (i, 0))
    return _call(body, grid=(S // tm, nj), name=name, args=[a, wd, x, gn], comm=comm,
                 in_specs=[pl.BlockSpec((FFN_PAIR, tm, Fb), lambda i, j: (j, i, 0)), pl.BlockSpec((FFN_PAIR, Fb, D), lambda i, j: (j, 0, 0)),
                           row, pl.BlockSpec((1, D), lambda i, j: (0, 0))],
                 out_specs=[row, row], out_shape=[jax.ShapeDtypeStruct((S, D), F32), jax.ShapeDtypeStruct((S, D), BF)],
                 scratch=[pltpu.VMEM((tm, D), F32)])


def _ffn_down_loss(a, wd, x, gf, tgt, name):
    nb, S, Fb = a.shape
    D = wd.shape[2]
    tm = _tile(S, 512)

    nj = nb // FFN_PAIR

    def body(a_ref, wd_ref, x_ref, gf_ref, t_ref, dx_ref, dxb_ref, dgf_ref, loss_ref, acc_ref):
        i, j = pl.program_id(0), pl.program_id(1)

        @pl.when(j == 0)
        def _():
            acc_ref[...] = jnp.zeros_like(acc_ref)

        _acc_dots(acc_ref, [(a_ref[b], lambda cols, b=b: wd_ref[b, :, cols]) for b in range(FFN_PAIR)])

        @pl.when((j == nj - 1) & (i == 0))
        def _():
            dgf_ref[...] = jnp.zeros_like(dgf_ref)
            loss_ref[...] = jnp.zeros_like(loss_ref)

        @pl.when(j == nj - 1)
        def _():
            def chunk(t, carry):
                rows = pl.ds(pl.multiple_of(t * EPI_ROWS, EPI_ROWS), EPI_ROWS)
                xo = x_ref[rows, :] + 0.5 * acc_ref[rows, :]
                r = lax.rsqrt(jnp.mean(xo * xo, axis=-1, keepdims=True) + NORM_EPS)
                xh = xo * r
                gf = gf_ref[...]
                e = xh * gf - t_ref[rows, :]
                loss_ref[...] += jnp.sum(jnp.mean(e * e, axis=-1, keepdims=True), axis=0, keepdims=True) * 0.5
                dy = e * (1.0 / D)
                dgf_ref[...] += jnp.sum(dy * xh, axis=0, keepdims=True)
                dxh = dy * gf
                dx = r * (dxh - xh * jnp.mean(dxh * xh, axis=-1, keepdims=True))
                dx_ref[rows, :] = dx
                dxb_ref[rows, :] = (0.5 * dx).astype(BF)
                return carry

            lax.fori_loop(0, tm // EPI_ROWS, chunk, 0)

    row = pl.BlockSpec((tm, D), lambda i, j: (i, 0))
    once = row
    vec = pl.BlockSpec((1, D), lambda i, j: (0, 0))
    return _call(body, grid=(S // tm, nj), name=name, args=[a, wd, x, gf, tgt],
                 in_specs=[pl.BlockSpec((FFN_PAIR, tm, Fb), lambda i, j: (j, i, 0)), pl.BlockSpec((FFN_PAIR, Fb, D), lambda i, j: (j, 0, 0)),
                           once, vec, once],
                 out_specs=[row, row, vec, pl.BlockSpec((1, 128), lambda i, j: (0, 0))],
                 out_shape=[jax.ShapeDtypeStruct((S, D), F32), jax.ShapeDtypeStruct((S, D), BF), jax.ShapeDtypeStruct((1, D), F32),
                            jax.ShapeDtypeStruct((1, 128), F32)],
                 scratch=[pltpu.VMEM((tm, D), F32)])[0]


def _ffn_bwd_act(dyb, wd, g, u, name, comm=()):
    S, D = dyb.shape
    nb, Fb, _ = wd.shape
    tm = _tile(S, FFN_ROWS)

    def body(dy_ref, wd_ref, g_ref, u_ref, dg_ref, du_ref):
        rows = pl.ds(pl.multiple_of(pl.program_id(1) * tm, tm), tm)
        da = _nt(dy_ref[rows, :], wd_ref[0])
        gv = g_ref[0].astype(F32)
        uv = u_ref[0].astype(F32)
        sg = jax.nn.sigmoid(gv)
        du_ref[0] = (da * gv * sg).astype(BF)
        dg_ref[0] = (da * uv * sg * (1.0 + gv * (1.0 - sg))).astype(BF)

    act = pl.BlockSpec((1, tm, Fb), lambda j, i: (j, i, 0))
    shp = jax.ShapeDtypeStruct((nb, S, Fb), BF)
    return _call(body, grid=(nb, S // tm), name=name, args=[dyb, wd, g, u], comm=comm,
                 in_specs=[pl.BlockSpec((S, D), lambda j, i: (0, 0)), pl.BlockSpec((1, Fb, D), lambda j, i: (j, 0, 0)), act, act],
                 out_specs=[act, act], out_shape=[shp, shp])


def _ffn_dwd(a, dyb, name, comm=()):
    nb, S, Fb = a.shape
    D = dyb.shape[1]
    ts = S
    ns = S // ts

    def body(a_ref, dy_ref, o_ref, acc_ref):
        s = pl.program_id(1)

        @pl.when(s == 0)
        def _():
            acc_ref[...] = jnp.zeros_like(acc_ref)

        acc_ref[...] += _tn(a_ref[0], dy_ref[...])

        @pl.when(s == ns - 1)
        def _():
            o_ref[0] = acc_ref[...].astype(BF)

    return _call(body, grid=(nb, ns), name=name, args=[a, dyb], comm=comm,
                 in_specs=[pl.BlockSpec((1, ts, Fb), lambda j, s: (j, s, 0)), pl.BlockSpec((ts, D), lambda j, s: (s, 0))],
                 out_specs=[pl.BlockSpec((1, Fb, D), lambda j, s: (j, 0, 0))], out_shape=[jax.ShapeDtypeStruct((nb, Fb, D), BF)],
                 scratch=[pltpu.VMEM((Fb, D), F32)])


def _ffn_dwgu(h, dg, du, name, comm=()):
    S, D = h.shape
    nb, _, Fb = dg.shape
    ts = _tile(S, FFN_ROWS)
    ns = S // ts

    def body(h_ref, dg_ref, du_ref, og_ref, ou_ref, accg_ref, accu_ref):
        s = pl.program_id(1)

        @pl.when(s == 0)
        def _():
            accg_ref[...] = jnp.zeros_like(accg_ref)
            accu_ref[...] = jnp.zeros_like(accu_ref)

        hv = h_ref[...]
        accg_ref[...] += _tn(dg_ref[0], hv)
        accu_ref[...] += _tn(du_ref[0], hv)

        @pl.when(s == ns - 1)
        def _():
            og_ref[0] = accg_ref[...].astype(BF)
            ou_ref[0] = accu_ref[...].astype(BF)

    act = pl.BlockSpec((1, ts, Fb), lambda j, s: (j, s, 0))
    out = pl.BlockSpec((1, Fb, D), lambda j, s: (j, 0, 0))
    shp = jax.ShapeDtypeStruct((nb, Fb, D), BF)
    return _call(body, grid=(nb, ns), name=name, args=[h, dg, du], comm=comm,
                 in_specs=[pl.BlockSpec((ts, D), lambda j, s: (s, 0)), act, act], out_specs=[out, out], out_shape=[shp, shp],
                 scratch=[pltpu.VMEM((Fb, D), F32), pltpu.VMEM((Fb, D), F32)])


def _dh_rms_bwd(pairs, blocked, tk, x, gn, dxo, out_scale, name, comm=(), tail=None):
    S, D = x.shape
    nk = pairs[0][0].shape[0] if blocked else pairs[0][0].shape[1] // tk
    tm = _tile(S, 512)
    npair = len(pairs)
    assert blocked or (npair == 1 and tail is not None and nk >= 2)
    nin = 2 * npair + (0 if blocked else 1)

    def body(*refs):
        ins = refs[:nin]
        x_ref, gn_ref, dxo_ref, dx_ref, dxb_ref, dgn_ref, acc_ref = refs[nin:]
        i, k = pl.program_id(0), pl.program_id(1)

        @pl.when(k == 0)
        def _():
            acc_ref[...] = jnp.zeros_like(acc_ref)

        if blocked:
            _acc_dots(acc_ref, [(ins[2 * p][0], lambda cols, r=ins[2 * p + 1]: r[0, :, cols]) for p in range(npair)])
        else:
            @pl.when(k < nk - 1)
            def _():
                _acc_dots(acc_ref, [(ins[0][...], lambda cols: ins[1][:, cols])])

            @pl.when(k == nk - 1)
            def _():
                _acc_dots(acc_ref, [(ins[0][...], lambda cols: ins[2][:, cols])])

        @pl.when((k == nk - 1) & (i == 0))
        def _():
            dgn_ref[...] = jnp.zeros_like(dgn_ref)

        @pl.when(k == nk - 1)
        def _():
            def chunk(t, carry):
                rows = pl.ds(pl.multiple_of(t * EPI_ROWS, EPI_ROWS), EPI_ROWS)
                xv = x_ref[rows, :]
                r = lax.rsqrt(jnp.mean(xv * xv, axis=-1, keepdims=True) + NORM_EPS)
                xh = xv * r
                dh = acc_ref[rows, :]
                dgn_ref[...] += jnp.sum(dh * xh, axis=0, keepdims=True)
                dxh = dh * gn_ref[...]
                dx = dxo_ref[rows, :] + r * (dxh - xh * jnp.mean(dxh * xh, axis=-1, keepdims=True))
                dx_ref[rows, :] = dx
                dxb_ref[rows, :] = (out_scale * dx).astype(BF)
                return carry

            lax.fori_loop(0, tm // EPI_ROWS, chunk, 0)

    if blocked:
        mats = [pl.BlockSpec((1, tm, tk), lambda i, k: (k, i, 0)), pl.BlockSpec((1, tk, D), lambda i, k: (k, 0, 0))] * npair
        flat = [t for pr in pairs for t in pr]
    else:
        mats = [pl.BlockSpec((tm, tk), lambda i, k: (i, k)), pl.BlockSpec((tk, D), lambda i, k: (jnp.minimum(k, nk - 2), 0)),
                pl.BlockSpec((tk, D), lambda i, k: (0, 0))]
        flat = [*pairs[0], tail]
    row = pl.BlockSpec((tm, D), lambda i, k: (i, 0))
    once = row
    vec = pl.BlockSpec((1, D), lambda i, k: (0, 0))
    return _call(body, grid=(S // tm, nk), name=name, args=[*flat, x, gn, dxo], comm=comm,
                 in_specs=mats + [once, vec, once], out_specs=[row, row, vec],
                 out_shape=[jax.ShapeDtypeStruct((S, D), F32), jax.ShapeDtypeStruct((S, D), BF), jax.ShapeDtypeStruct((1, D), F32)],
                 scratch=[pltpu.VMEM((tm, D), F32)])


def _ffn_dh(dg, du, wgT, wuT, x, gn, dxo, name, comm=()):
    S, D = x.shape
    nb, _, Fb = dg.shape
    tm = _tile(S, 512)
    ni, er = S // tm, tm // nb
    assert er % 16 == 0

    def body(dg_ref, wg_ref, du_ref, wu_ref, x_ref, gn_ref, dxo_ref, dx_ref, dxb_ref, dgn_ref, acc_ref):
        i, k = pl.program_id(0), pl.program_id(1)
        slot = i % 2

        @pl.when((i == 0) & (k == 0))
        def _():
            acc_ref[...] = jnp.zeros_like(acc_ref)
            dgn_ref[...] = jnp.zeros_like(dgn_ref)

        @pl.when((i > 0) & (k == 0))
        def _():
            acc_ref[slot] = jnp.zeros((tm, D), F32)

        def finish_rows():
            rows = pl.ds(pl.multiple_of(k * er, er), er)
            xv = x_ref[rows, :]
            r = lax.rsqrt(jnp.mean(xv * xv, axis=-1, keepdims=True) + NORM_EPS)
            xh = xv * r
            dh = acc_ref[1 - slot, rows, :]
            dgn_ref[...] += jnp.where(i > 0, jnp.sum(dh * xh, axis=0, keepdims=True), 0.0)
            dxh = dh * gn_ref[...]
            dx = dxo_ref[rows, :] + r * (dxh - xh * jnp.mean(dxh * xh, axis=-1, keepdims=True))
            dx_ref[rows, :] = dx
            dxb_ref[rows, :] = dx.astype(BF)

        @pl.when(i < ni)
        def _():
            _acc_dots(acc_ref.at[slot], [(dg_ref[0], lambda cols: wg_ref[0, :, cols]), (du_ref[0], lambda cols: wu_ref[0, :, cols])])
            finish_rows()

        @pl.when(i == ni)
        def _():
            finish_rows()

    last = lambda i, k: jnp.where(i == ni, nb - 1, k)
    act = pl.BlockSpec((1, tm, Fb), lambda i, k: (last(i, k), jnp.minimum(i, ni - 1), 0))
    w = pl.BlockSpec((1, Fb, D), lambda i, k: (last(i, k), 0, 0))
    prev = pl.BlockSpec((tm, D), lambda i, k: (jnp.maximum(i - 1, 0), 0))
    vec = pl.BlockSpec((1, D), lambda i, k: (0, 0))
    return _call(body, grid=(ni + 1, nb), name=name, args=[dg, wgT, du, wuT, x, gn, dxo], comm=comm,
                 in_specs=[act, w, act, w, prev, vec, prev], out_specs=[prev, prev, vec],
                 out_shape=[jax.ShapeDtypeStruct((S, D), F32), jax.ShapeDtypeStruct((S, D), BF), jax.ShapeDtypeStruct((1, D), F32)],
                 scratch=[pltpu.VMEM((2, tm, D), F32)])


def _proj_split(a, bT, tm, tn, split, name, comm=()):
    M, K = a.shape
    N = bT.shape[0]
    n_first = split // tn

    def body(a_ref, b_ref, first_ref, rest_ref):
        n = pl.program_id(1)
        y = _nt(a_ref[...], b_ref[...])

        @pl.when(n < n_first)
        def _():
            first_ref[...] = y

        @pl.when(n >= n_first)
        def _():
            rest_ref[...] = y.astype(BF)

    return _call(body, grid=(M // tm, N // tn), name=name, args=[a, bT], comm=comm,
                 in_specs=[pl.BlockSpec((tm, K), lambda i, n: (i, 0)), pl.BlockSpec((tn, K), lambda i, n: (n, 0))],
                 out_specs=[pl.BlockSpec((tm, tn), lambda i, n: (i, jnp.minimum(n, n_first - 1))),
                            pl.BlockSpec((tm, tn), lambda i, n: (i, jnp.maximum(n - n_first, 0)))],
                 out_shape=[jax.ShapeDtypeStruct((M, split), F32), jax.ShapeDtypeStruct((M, N - split), BF)])


def _mm_tn(a, b, tm, tn, ts, blocked, name, comm=(), mrows=None):
    S, M = a.shape[0], (a.shape[1] if mrows is None else mrows)
    N = b.shape[1]
    ns = S // ts
    per_tile = tn // blocked if blocked else 0

    def body(a_ref, b_ref, o_ref, acc_ref):
        s = pl.program_id(2)

        @pl.when(s == 0)
        def _():
            acc_ref[...] = jnp.zeros_like(acc_ref)

        acc_ref[...] += _tn(a_ref[...], b_ref[...])

        @pl.when(s == ns - 1)
        def _():
            if blocked:
                for t in range(per_tile):
                    o_ref[t] = acc_ref[:, t * blocked:(t + 1) * blocked].astype(BF)
            else:
                o_ref[...] = acc_ref[...].astype(BF)

    if blocked:
        ospec = pl.BlockSpec((per_tile, tm, blocked), lambda i, n, s: (n, i, 0))
        oshape = jax.ShapeDtypeStruct((N // blocked, M, blocked), BF)
    else:
        ospec = pl.BlockSpec((tm, tn), lambda i, n, s: (i, n))
        oshape = jax.ShapeDtypeStruct((M, N), BF)
    return _call(body, grid=(M // tm, N // tn, ns), name=name, args=[a, b], comm=comm,
                 in_specs=[pl.BlockSpec((ts, tm), lambda i, n, s: (s, i)), pl.BlockSpec((ts, tn), lambda i, n, s: (s, n))],
                 out_specs=[ospec], out_shape=[oshape], scratch=[pltpu.VMEM((tm, tn), F32)])


def _rope_tables(S):
    half = ROPE_DIM // 2
    inv_freq = ROPE_THETA ** (-jnp.arange(0, ROPE_DIM, 2, dtype=F32) / ROPE_DIM)
    ang = jnp.arange(S, dtype=F32)[:, None] * inv_freq[None, :]
    cos, sin = jnp.cos(ang), jnp.sin(ang)
    zeros = jnp.zeros((S, HEAD_DIM - ROPE_DIM), F32)
    c = jnp.concatenate([cos, cos, jnp.ones((S, HEAD_DIM - ROPE_DIM), F32)], axis=1)
    sm = jnp.concatenate([-sin, jnp.zeros((S, half), F32), zeros], axis=1)
    sp = jnp.concatenate([jnp.zeros((S, half), F32), sin, zeros], axis=1)
    return c, sm, sp


def _rope(t, c, sm, sp):
    return t * c + pltpu.roll(t, HEAD_DIM - ROPE_DIM // 2, 1) * sm + pltpu.roll(t, ROPE_DIM // 2, 1) * sp


def _rope_t(dy, c, sm, sp):
    return dy * c + pltpu.roll(dy * sm, ROPE_DIM // 2, 1) + pltpu.roll(dy * sp, HEAD_DIM - ROPE_DIM // 2, 1)


def _att_mask(i):
    qi = lax.broadcasted_iota(jnp.int32, (BLK, 2 * BLK), 0)
    kj = lax.broadcasted_iota(jnp.int32, (BLK, 2 * BLK), 1)
    diff = qi + BLK - kj
    first_key = jnp.where(i > 0, 0, BLK)
    return (diff >= 0) & (diff <= BLK) & (kj >= first_key)


def _res_rows(r, i, n, d):
    if d == 1:
        return pl.ds(pl.multiple_of(i * n, n), n)
    return pl.ds(r + i * (n * d), n, stride=d)


def _att_specs(S, gi):
    def sect(off):
        base = (off + gi * GROUP_W) // HEAD_DIM
        return _once((S, HEAD_DIM), lambda hh: (0, base + hh))

    tab = pl.BlockSpec((S, HEAD_DIM), lambda hh: (0, 0))
    head = pl.BlockSpec((S, HEAD_DIM), lambda hh: (0, hh))
    return sect, tab, head


def _each_residue(d, fn):
    if d == 1:
        fn(0)
    else:
        lax.fori_loop(0, d, lambda r, carry: (fn(r), carry)[1], 0)


def _att_fwd(qkv, tabs, gi, d, name, comm=()):
    S = qkv.shape[0]
    L = S // d
    sect, tab, head = _att_specs(S, gi)
    nblk = L // BLK
    scale = HEAD_DIM ** -0.5

    def body(q_ref, k_ref, v_ref, c_ref, sm_ref, sp_ref, o_ref, lse_ref, qr, kp, vp):
        kp[pl.ds(0, BLK), :] = jnp.zeros((BLK, HEAD_DIM), BF)
        vp[pl.ds(0, BLK), :] = jnp.zeros((BLK, HEAD_DIM), BF)

        def residue(r):
            res = _res_rows(r, 0, L, d)
            c, sm, sp = c_ref[res, :], sm_ref[res, :], sp_ref[res, :]
            qr[...] = _rope(q_ref[res, :], c, sm, sp).astype(BF)
            kp[pl.ds(BLK, L), :] = _rope(k_ref[res, :], c, sm, sp).astype(BF)
            vp[pl.ds(BLK, L), :] = v_ref[res, :].astype(BF)

            def blk(i, carry):
                r0 = pl.multiple_of(i * BLK, BLK)
                s = _nt(qr[pl.ds(r0, BLK), :], kp[pl.ds(r0, 2 * BLK), :]) * scale
                s = jnp.where(_att_mask(i), s, NEG)
                m = jnp.max(s, axis=-1, keepdims=True)
                p = jnp.exp(s - m)
                l = jnp.sum(p, axis=-1, keepdims=True)
                out = _res_rows(r, i, BLK, d)
                o_ref[out, :] = _nn(p.astype(BF), vp[pl.ds(r0, 2 * BLK), :]) / l
                lse_ref[out, :] = jnp.broadcast_to(m + jnp.log(l), (BLK, HEAD_DIM))
                return carry

            lax.fori_loop(0, nblk, blk, 0, unroll=min(4, nblk))

        _each_residue(d, residue)

    shp = jax.ShapeDtypeStruct((S, GROUP_W), F32)
    return _call(body, grid=(HEADS_PER_GROUP,), name=name, args=[qkv, qkv, qkv, *tabs], comm=comm,
                 in_specs=[sect(Q_OFF), sect(K_OFF), sect(V_OFF), tab, tab, tab], out_specs=[head, head], out_shape=[shp, shp],
                 scratch=[pltpu.VMEM((L, HEAD_DIM), BF), pltpu.VMEM((L + BLK, HEAD_DIM), BF), pltpu.VMEM((L + BLK, HEAD_DIM), BF)])


def _att_combine(os, lses, name):
    S = os[0].shape[0]
    tm = _tile(S, 512)

    def body(o0, o1, o2, l0, l1, l2, oa_ref, lse_ref):
        a, b, c = l0[...], l1[...], l2[...]
        mx = jnp.maximum(jnp.maximum(a, b), c)
        wa, wb, wc = jnp.exp(a - mx), jnp.exp(b - mx), jnp.exp(c - mx)
        den = wa + wb + wc
        oa_ref[...] = ((wa * o0[...] + wb * o1[...] + wc * o2[...]) / den).astype(BF)
        lse_ref[...] = mx + jnp.log(den)

    row = pl.BlockSpec((tm, GROUP_W), lambda i: (i, 0))
    return _call(body, grid=(S // tm,), name=name, args=[*os, *lses], in_specs=[row] * 6, out_specs=[row, row],
                 out_shape=[jax.ShapeDtypeStruct((S, GROUP_W), BF), jax.ShapeDtypeStruct((S, GROUP_W), F32)])[0]


def _att_bwd(qkv, tabs, do, lse, dvec, gi, d, name, comm=()):
    S = qkv.shape[0]
    L = S // d
    sect, tab, head = _att_specs(S, gi)
    stat = _once((S, HEAD_DIM), lambda hh: (0, hh))
    nblk = L // BLK
    scale = HEAD_DIM ** -0.5

    def body(q_ref, k_ref, v_ref, c_ref, sm_ref, sp_ref, do_ref, lse_ref, dv_ref, dq_out, dk_out, dv_out, qr, kp, vp, dkp, dvp):
        kp[pl.ds(0, BLK), :] = jnp.zeros((BLK, HEAD_DIM), BF)
        vp[pl.ds(0, BLK), :] = jnp.zeros((BLK, HEAD_DIM), BF)

        def residue(r):
            res = _res_rows(r, 0, L, d)
            c, sm, sp = c_ref[res, :], sm_ref[res, :], sp_ref[res, :]
            qr[...] = _rope(q_ref[res, :], c, sm, sp).astype(BF)
            kp[pl.ds(BLK, L), :] = _rope(k_ref[res, :], c, sm, sp).astype(BF)
            vp[pl.ds(BLK, L), :] = v_ref[res, :].astype(BF)
            dkp[...] = jnp.zeros_like(dkp)
            dvp[...] = jnp.zeros_like(dvp)

            def blk(i, carry):
                r0 = pl.multiple_of(i * BLK, BLK)
                rows, win, pos = pl.ds(r0, BLK), pl.ds(r0, 2 * BLK), _res_rows(r, i, BLK, d)
                q, kw, vw, dob = qr[rows, :], kp[win, :], vp[win, :], do_ref[pos, :].astype(BF)
                s = jnp.where(_att_mask(i), _nt(q, kw) * scale, NEG)
                p = jnp.exp(s - lse_ref[pos, :][:, :1])
                ds = p * (_nt(dob, vw) - dv_ref[pos, :][:, :1]) * scale
                dsb = ds.astype(BF)
                dq_out[pos, :] = _rope_t(_nn(dsb, kw), c_ref[pos, :], sm_ref[pos, :], sp_ref[pos, :])
                dkp[win, :] += _tn(dsb, q)
                dvp[win, :] += _tn(p.astype(BF), dob)
                return carry

            lax.fori_loop(0, nblk, blk, 0, unroll=2)
            dk_out[res, :] = _rope_t(dkp[pl.ds(BLK, L), :], c, sm, sp)
            dv_out[res, :] = dvp[pl.ds(BLK, L), :]

        _each_residue(d, residue)

    shp = jax.ShapeDtypeStruct((S, GROUP_W), F32)
    return _call(body, grid=(HEADS_PER_GROUP,), name=name, args=[qkv, qkv, qkv, *tabs, do, lse, dvec], comm=comm,
                 in_specs=[sect(Q_OFF), sect(K_OFF), sect(V_OFF), tab, tab, tab, stat, stat, stat],
                 out_specs=[head, head, head], out_shape=[shp, shp, shp],
                 scratch=[pltpu.VMEM((L, HEAD_DIM), BF), pltpu.VMEM((L + BLK, HEAD_DIM), BF), pltpu.VMEM((L + BLK, HEAD_DIM), BF),
                          pltpu.VMEM((L + BLK, HEAD_DIM), F32), pltpu.VMEM((L + BLK, HEAD_DIM), F32)])


def _sg_parts(u_ref, vs_ref, g_ref, b_ref):
    uv = u_ref[...].astype(F32)
    vv = vs_ref[...].astype(F32)
    cv = _gauss_cdf(vv)
    vg = vv * cv
    mu = jnp.mean(vg, axis=-1, keepdims=True)
    vc = vg - mu
    rs = lax.rsqrt(jnp.mean(vc * vc, axis=-1, keepdims=True) + LN_EPS)
    y = vc * rs
    return uv, vv, cv, rs, y, y * g_ref[...] + b_ref[...]


def _sg_wmask():
    t = lax.broadcasted_iota(jnp.int32, (BLK, BLK), 0)
    s = lax.broadcasted_iota(jnp.int32, (BLK, BLK), 1)
    return s <= t


def _sg_fwd(proj, sgw, sgbT, lng, lnb, name):
    S, P = proj.shape

    def body(u_ref, vs_ref, w_ref, bt_ref, g_ref, b_ref, z_ref):
        uv, _, _, _, _, vln = _sg_parts(u_ref, vs_ref, g_ref, b_ref)
        ug = uv * _gauss_cdf(uv)
        vb = vln.astype(BF)
        mask = _sg_wmask()
        bt = bt_ref[...]
        for g in range(SG_GROUPS):
            cols = slice(g * BLK, (g + 1) * BLK)
            w = jnp.where(mask, w_ref[g], 0.0).astype(BF)
            sp = _nn(w, vb[:, cols]) + bt[:, g:g + 1]
            z_ref[:, cols] = (ug[:, cols] * sp).astype(BF)

    tile = lambda off: pl.BlockSpec((BLK, SG_W), lambda i: (i, off // SG_W))
    full = lambda shape: pl.BlockSpec(shape, lambda i: (0,) * len(shape))
    return _call(body, grid=(S // BLK,), name=name, args=[proj, proj, sgw, sgbT, lng, lnb],
                 in_specs=[tile(R_U), tile(R_VS), full((SG_GROUPS, BLK, BLK)), full((BLK, BLK)), full((1, SG_W)), full((1, SG_W))],
                 out_specs=[pl.BlockSpec((BLK, SG_W), lambda i: (i, 0))], out_shape=[jax.ShapeDtypeStruct((S, SG_W), BF)])[0][0]


def _sg_bwd(proj, dz, sgw, sgbT, lng, lnb, name):
    S, P = proj.shape

    def body(u_ref, vs_ref, dz_ref, w_ref, bt_ref, g_ref, b_ref, du_ref, dvs_ref, dw_ref, dbt_ref, dg_ref, db_ref, dvln):
        @pl.when(pl.program_id(0) == 0)
        def _():
            dw_ref[...] = jnp.zeros_like(dw_ref)
            dbt_ref[...] = jnp.zeros_like(dbt_ref)
            dg_ref[...] = jnp.zeros_like(dg_ref)
            db_ref[...] = jnp.zeros_like(db_ref)

        uv, vv, cv, rs, y, vln = _sg_parts(u_ref, vs_ref, g_ref, b_ref)
        cu = _gauss_cdf(uv)
        ug = uv * cu
        dug = _gelu_grad(uv, cu)
        vb = vln.astype(BF)
        dzv = dz_ref[...].astype(F32)
        dsp = dzv * ug
        dspb = dsp.astype(BF)
        mask = _sg_wmask()
        bt = bt_ref[...]
        lane = lax.broadcasted_iota(jnp.int32, (BLK, BLK), 1)
        dbt = jnp.zeros((BLK, BLK), F32)
        for g in range(SG_GROUPS):
            cols = slice(g * BLK, (g + 1) * BLK)
            w = jnp.where(mask, w_ref[g], 0.0).astype(BF)
            sp = _nn(w, vb[:, cols]) + bt[:, g:g + 1]
            du_ref[:, cols] = (dzv[:, cols] * sp * dug[:, cols]).astype(BF)
            dw_ref[g] += jnp.where(mask, _nt(dspb[:, cols], vb[:, cols]), 0.0)
            dbt = dbt + jnp.where(lane == g, jnp.sum(dsp[:, cols], axis=-1, keepdims=True), 0.0)
            dvln[:, cols] = _tn(w, dspb[:, cols])
        dbt_ref[...] += dbt
        dvl = dvln[...]
        dg_ref[...] += jnp.sum(dvl * y, axis=0, keepdims=True)
        db_ref[...] += jnp.sum(dvl, axis=0, keepdims=True)
        dy = dvl * g_ref[...]
        dvg = rs * (dy - jnp.mean(dy, axis=-1, keepdims=True) - y * jnp.mean(dy * y, axis=-1, keepdims=True))
        dvs_ref[...] = (dvg * _gelu_grad(vv, cv)).astype(BF)

    tile = lambda off: pl.BlockSpec((BLK, SG_W), lambda i: (i, off // SG_W))
    full = lambda shape: pl.BlockSpec(shape, lambda i: (0,) * len(shape))
    row = pl.BlockSpec((BLK, SG_W), lambda i: (i, 0))
    return _call(body, grid=(S // BLK,), name=name, args=[proj, proj, dz, sgw, sgbT, lng, lnb],
                 in_specs=[tile(R_U), tile(R_VS), row, full((SG_GROUPS, BLK, BLK)), full((BLK, BLK)), full((1, SG_W)), full((1, SG_W))],
                 out_specs=[row, row, full((SG_GROUPS, BLK, BLK)), full((BLK, BLK)), full((1, SG_W)), full((1, SG_W))],
                 out_shape=[jax.ShapeDtypeStruct((S, SG_W), BF), jax.ShapeDtypeStruct((S, SG_W), BF),
                            jax.ShapeDtypeStruct((SG_GROUPS, BLK, BLK), F32), jax.ShapeDtypeStruct((BLK, BLK), F32),
                            jax.ShapeDtypeStruct((1, SG_W), F32), jax.ShapeDtypeStruct((1, SG_W), F32)],
                 scratch=[pltpu.VMEM((BLK, SG_W), F32)])[0]


def _gate_merge(oatt, z, watt, wsg, proj, name, comm=()):
    S = oatt.shape[0]
    nb, _, Db = watt.shape
    D = nb * Db
    tm = _tile(S, 512)
    half = D // 2
    ga, gs = R_GA // half, (R_GA + D) // half

    def body(oa_ref, z_ref, wa_ref, ws_ref, ga0, ga1, gs0, gs1, ya_ref, ys_ref, mg_ref):
        oa, zv = oa_ref[...], z_ref[...]
        for j in range(nb):
            cols = slice(j * Db, (j + 1) * Db)
            g_a, g_s = (ga0, gs0) if j < nb // 2 else (ga1, gs1)
            gcols = slice((j % (nb // 2)) * Db, (j % (nb // 2) + 1) * Db)
            ya = _nn(oa, wa_ref[j])
            ys = _nn(zv, ws_ref[j])
            ya_ref[:, cols] = ya.astype(BF)
            ys_ref[:, cols] = ys.astype(BF)
            mg_ref[:, cols] = (jax.nn.sigmoid(g_a[:, gcols].astype(F32)) * ya + jax.nn.sigmoid(g_s[:, gcols].astype(F32)) * ys).astype(BF)

    out = pl.BlockSpec((tm, D), lambda i: (i, 0))
    gate = lambda b: pl.BlockSpec((tm, half), lambda i: (i, b))
    shp = jax.ShapeDtypeStruct((S, D), BF)
    return _call(body, grid=(S // tm,), name=name, args=[oatt, z, watt, wsg, proj, proj, proj, proj], comm=comm,
                 in_specs=[pl.BlockSpec((tm, GROUP_W), lambda i: (i, 0)), pl.BlockSpec((tm, SG_W), lambda i: (i, 0)),
                           pl.BlockSpec((nb, GROUP_W, Db), lambda i: (0, 0, 0)), pl.BlockSpec((nb, SG_W, Db), lambda i: (0, 0, 0)),
                           gate(ga), gate(ga + 1), gate(gs), gate(gs + 1)],
                 out_specs=[out, out, out], out_shape=[shp, shp, shp])


def _mix_out(merged, wout, x, gn, name):
    S, D = x.shape
    tm = _tile(S, 512)

    def body(m_ref, w_ref, x_ref, gn_ref, xo_ref, hn_ref):
        xo = x_ref[...] + _nn(m_ref[...], w_ref[...])
        r = lax.rsqrt(jnp.mean(xo * xo, axis=-1, keepdims=True) + NORM_EPS)
        xo_ref[...] = xo
        hn_ref[...] = (xo * r * gn_ref[...]).astype(BF)

    row = pl.BlockSpec((tm, D), lambda i: (i, 0))
    return _call(body, grid=(S // tm,), name=name, args=[merged, wout, x, gn],
                 in_specs=[row, pl.BlockSpec((D, D), lambda i: (0, 0)), row, pl.BlockSpec((1, D), lambda i: (0, 0))],
                 out_specs=[row, row], out_shape=[jax.ShapeDtypeStruct((S, D), F32), jax.ShapeDtypeStruct((S, D), BF)])[0]


def _mix_bwd_gate(dmix, wout, ya, ys, proj, name):
    S, D = dmix.shape
    tm, tn = _tile(S, 512), min(512, D // 2)
    half = D // 2
    ga, gs = R_GA // half, (R_GA + D) // half

    def body(dm_ref, w_ref, ya_ref, ys_ref, ga0, ga1, gs0, gs1, dya_ref, dys_ref, dga_ref, dgs_ref):
        dmv = dm_ref[...]
        for c0 in range(0, D, tn):
            cols = slice(c0, c0 + tn)
            g_a, g_s = (ga0, gs0) if c0 < half else (ga1, gs1)
            gcols = slice(c0 % half, c0 % half + tn)
            dm = _nt(dmv, w_ref[cols, :])
            sa = jax.nn.sigmoid(g_a[:, gcols].astype(F32))
            ss = jax.nn.sigmoid(g_s[:, gcols].astype(F32))
            dya_ref[:, cols] = (dm * sa).astype(BF)
            dys_ref[:, cols] = (dm * ss).astype(BF)
            dga_ref[:, cols] = (dm * ya_ref[:, cols].astype(F32) * sa * (1.0 - sa)).astype(BF)
            dgs_ref[:, cols] = (dm * ys_ref[:, cols].astype(F32) * ss * (1.0 - ss)).astype(BF)

    row = pl.BlockSpec((tm, D), lambda i: (i, 0))
    gate = lambda b: pl.BlockSpec((tm, half), lambda i: (i, b))
    shp = jax.ShapeDtypeStruct((S, D), BF)
    return _call(body, grid=(S // tm,), name=name, args=[dmix, wout, ya, ys, proj, proj, proj, proj],
                 in_specs=[row, pl.BlockSpec((D, D), lambda i: (0, 0)), row, row, gate(ga), gate(ga + 1), gate(gs), gate(gs + 1)],
                 out_specs=[row] * 4, out_shape=[shp] * 4)[0]


def _att_sg_dout(dya, dys, watt, wsg, oatt, name, comm=()):
    S, D = dya.shape
    nb, _, Db = watt.shape
    tm = _tile(S, 512)

    def body(dya_ref, dys_ref, wa_ref, ws_ref, oa_ref, do_ref, dz_ref, dvec_ref):
        def back(dy_ref, w_ref, rows):
            tot = None
            for j in range(nb):
                part = _nt(dy_ref[:, j * Db:(j + 1) * Db], w_ref[j, rows, :])
                tot = part if tot is None else tot + part
            return tot

        dov = back(dya_ref, wa_ref, slice(0, GROUP_W))
        do_ref[...] = dov
        for c0 in range(0, SG_W, GROUP_W):
            dz_ref[:, c0:c0 + GROUP_W] = back(dys_ref, ws_ref, slice(c0, c0 + GROUP_W)).astype(BF)
        prod = dov * oa_ref[...].astype(F32)
        for hh in range(HEADS_PER_GROUP):
            cols = slice(hh * HEAD_DIM, (hh + 1) * HEAD_DIM)
            dvec_ref[:, cols] = jnp.broadcast_to(jnp.sum(prod[:, cols], axis=-1, keepdims=True), (tm, HEAD_DIM))

    row = pl.BlockSpec((tm, D), lambda i: (i, 0))
    att = pl.BlockSpec((tm, GROUP_W), lambda i: (i, 0))
    return _call(body, grid=(S // tm,), name=name, args=[dya, dys, watt, wsg, oatt], comm=comm,
                 in_specs=[row, row, pl.BlockSpec((nb, GROUP_W, Db), lambda i: (0, 0, 0)), pl.BlockSpec((nb, SG_W, Db), lambda i: (0, 0, 0)), att],
                 out_specs=[att, pl.BlockSpec((tm, SG_W), lambda i: (i, 0)), att],
                 out_shape=[jax.ShapeDtypeStruct((S, GROUP_W), F32), jax.ShapeDtypeStruct((S, SG_W), BF), jax.ShapeDtypeStruct((S, GROUP_W), F32)])[0]


def _small_allreduce(pack, name):
    R = pack.shape[0]

    def body(p_ref, o_ref, gath, send, recv):
        x, y, c = _place()
        me = 4 * x + 2 * y + c
        gath[me] = p_ref[...]
        copies = []
        for r in range(1, N_DEV):
            px, py, pc = _flip(x, r & 4), _flip(y, r & 2), _flip(c, r & 1)
            peer = 4 * px + 2 * py + pc
            mk = lambda dst: pltpu.make_async_remote_copy(src_ref=p_ref, dst_ref=dst, send_sem=send.at[r - 1], recv_sem=recv.at[r - 1],
                                                          device_id=(px, py, pc), device_id_type=MESH)
            snd = mk(gath.at[me])
            snd.start()
            copies.append((snd, mk(gath.at[peer])))
        for snd, rcv in copies:
            rcv.wait_recv()
            snd.wait_send()
        acc = gath[0]
        for s in range(1, N_DEV):
            acc = acc + gath[s]
        o_ref[...] = acc

    vm = pl.BlockSpec(memory_space=pltpu.VMEM)
    return pl.pallas_call(
        body, name=name, in_specs=[vm], out_specs=vm, out_shape=jax.ShapeDtypeStruct(pack.shape, F32),
        scratch_shapes=[pltpu.VMEM((N_DEV, R, 128), F32), pltpu.SemaphoreType.DMA((7,)), pltpu.SemaphoreType.DMA((7,))],
        compiler_params=pltpu.CompilerParams(vmem_limit_bytes=VMEM_LIMIT),
    )(pack)


def _row_tile(R, C, elems=262144):
    fits = [t for t in range(16, R + 1, 16) if R % t == 0 and t * C <= elems]
    return max(fits) if fits else R


def _pair_add(parts, other, name):
    _, R, C = parts.shape
    tr = _row_tile(R, C, 1048576)

    def body(c_ref, p_ref, o_ref, s_ref):
        s_ref[0] = (p_ref[0].astype(F32) + o_ref[0].astype(F32)).astype(BF)

    core = lax.axis_index("c").astype(jnp.int32).reshape(1)
    return pl.pallas_call(
        body, name=name,
        grid_spec=pltpu.PrefetchScalarGridSpec(
            num_scalar_prefetch=1, grid=(N_CHIP, R // tr),
            in_specs=[pl.BlockSpec((1, tr, C), lambda q, i, c: (2 * q + c[0], i, 0)), pl.BlockSpec((1, tr, C), lambda q, i, c: (q, i, 0))],
            out_specs=pl.BlockSpec((1, tr, C), lambda q, i, c: (q, i, 0))),
        out_shape=jax.ShapeDtypeStruct((N_CHIP, R, C), BF),
        compiler_params=pltpu.CompilerParams(dimension_semantics=("arbitrary", "arbitrary"), vmem_limit_bytes=VMEM_LIMIT),
    )(core, parts, other)


def _adamw(parts, w, m, v, name):
    ns, R, C = parts.shape
    tr = _row_tile(R, C, 524288)
    c1 = 1.0 - ADAM_B1 ** ADAM_STEP
    c2 = 1.0 - ADAM_B2 ** ADAM_STEP

    def body(p_ref, w_ref, m_ref, v_ref, g_ref, d_ref, nm_ref, nv_ref):
        g = p_ref[0].astype(F32)
        for s in range(1, ns):
            g = g + p_ref[s].astype(F32)
        mn = ADAM_B1 * m_ref[...] + (1.0 - ADAM_B1) * g
        vn = ADAM_B2 * v_ref[...] + (1.0 - ADAM_B2) * (g * g)
        g_ref[...] = g
        nm_ref[...] = mn
        nv_ref[...] = vn
        d_ref[...] = -ADAM_LR * ((mn / c1) / (jnp.sqrt(vn / c2) + ADAM_EPS) + ADAM_WD * w_ref[...])

    row = pl.BlockSpec((tr, C), lambda i: (i, 0))
    shp = jax.ShapeDtypeStruct((R, C), F32)
    return _call(body, grid=(R // tr,), name=name, args=[parts, w, m, v],
                 in_specs=[pl.BlockSpec((ns, tr, C), lambda i: (0, i, 0)), row, row, row], out_specs=[row] * 4, out_shape=[shp] * 4)[0]


def _pad_rows(a, rows):
    return jnp.pad(a, ((0, rows - a.shape[0]), (0, 0)))


def kernel(x, ffn1_norm, ffn1_w_gate, ffn1_w_up, ffn1_w_down, mix_norm, w_in, sg_ln_g, sg_ln_b, sg_w, sg_b, w_att_out, w_sg_out, w_out, ffn2_norm, ffn2_w_gate, ffn2_w_up, ffn2_w_down, final_norm, loss_target, m_ffn1_norm, m_ffn1_w_gate, m_ffn1_w_up, m_ffn1_w_down, m_mix_norm, m_w_in, m_sg_ln_g, m_sg_ln_b, m_sg_w, m_sg_b, m_w_att_out, m_w_sg_out, m_w_out, m_ffn2_norm, m_ffn2_w_gate, m_ffn2_w_up, m_ffn2_w_down, m_final_norm, v_ffn1_norm, v_ffn1_w_gate, v_ffn1_w_up, v_ffn1_w_down, v_mix_norm, v_w_in, v_sg_ln_g, v_sg_ln_b, v_sg_w, v_sg_b, v_w_att_out, v_w_sg_out, v_w_out, v_ffn2_norm, v_ffn2_w_gate, v_ffn2_w_up, v_ffn2_w_down, v_final_norm):
    S, D = x.shape[1], x.shape[2]
    Pb = w_in.shape[2]
    P = N_DEV * Pb
    assert P == GA_OFF + 2 * D and D % (N_DEV * 128) == 0 and S % (BLK * DILATIONS[-1]) == 0
    xs, tgt = x[0], loss_target[0]

    sharded = dict(ffn1_w_gate=ffn1_w_gate, ffn1_w_up=ffn1_w_up, ffn1_w_down=ffn1_w_down, w_in=w_in, w_att_out=w_att_out,
                   w_sg_out=w_sg_out, w_out=w_out, ffn2_w_gate=ffn2_w_gate, ffn2_w_up=ffn2_w_up, ffn2_w_down=ffn2_w_down)
    cols = ("ffn1_w_gate", "ffn1_w_up", "w_in", "ffn2_w_gate", "ffn2_w_up")
    local = lambda n, a: a[0].T if n in cols else a[0]
    back = lambda n, a: a.T[None] if n in cols else a[None]
    wloc = {n: local(n, w) for n, w in sharded.items()}
    sb = {n: w.astype(BF) for n, w in wloc.items()}

    (h1,), ((wg1,),) = _rms_fwd(xs, ffn1_norm, "rms1", comm=[_Gather([sb["ffn1_w_gate"]], 1.0, 1.0)])
    (g1,), ((wu1,),) = _ffn_gate(h1, wg1, "ffn1_gate", comm=[_Gather([sb["ffn1_w_up"]], 0.9, 0.55)])
    (u1, a1), ((wd1,),) = _ffn_up_act(h1, wu1, g1, "ffn1_up_act", comm=[_Gather([sb["ffn1_w_down"]], 0.9, 0.55)])
    (x1, h2), ((winT8,),) = _ffn_down_norm(a1, wd1, xs, mix_norm, "ffn1_down", comm=[_Gather([sb["w_in"]], 1.0, 0.7)])
    winT = winT8.reshape(P, D)
    (qkv, rest), ((wg2, wu2),) = _proj_split(h2, winT, _tile(S, 2048), 512, U_OFF, "proj",
                                           comm=[_Gather([sb["ffn2_w_gate"], sb["ffn2_w_up"]], 0.85, 0.5)])
    tabs = _rope_tables(S)
    rides = [[_Gather([sb["w_att_out"], sb["w_sg_out"]], 0.9, 0.5)], [_Gather([sb["w_out"]], 0.85, 0.45)], []]
    os, lses, late = [], [], []
    for gi, d in enumerate(DILATIONS):
        (o, l), got_here = _att_fwd(qkv, tabs, gi, d, f"att_fwd{gi}", comm=rides[gi])
        late += [w for g in got_here for w in g]
        os.append(o)
        lses.append(l)
    watt, wsg, wout8 = late
    wout = wout8.reshape(D, D)
    oatt, lse = _att_combine(os, lses, "att_combine")
    sgw = sg_w[0]
    sgbT = jnp.pad(sg_b[0].T, ((0, 0), (0, BLK - SG_GROUPS)))
    z = _sg_fwd(rest, sgw, sgbT, sg_ln_g, sg_ln_b, "sg_fwd")
    (ya, ys, merged), _ = _gate_merge(oatt, z, watt, wsg, rest, "gate_merge")
    x2, h3 = _mix_out(merged, wout, x1, ffn2_norm, "mix_out")
    (g3, u3, a3), ((wd2,),) = _ffn_up(h3, wg2, wu2, "ffn2_up", comm=[_Gather([sb["ffn2_w_down"]], 0.6, 0.35)])
    dx3, dyb3, d_final, loss_part = _ffn_down_loss(a3, wd2, x2, final_norm.reshape(1, D), tgt, "ffn2_down_loss")

    Fb = wg2.shape[1]
    Db = watt.shape[2]
    p_pad = -(-P // PROJ_TK) * PROJ_TK
    win_tail = _pad_rows(winT[p_pad - PROJ_TK:], PROJ_TK)
    (dg3, du3), _ = _ffn_bwd_act(dyb3, wd2, g3, u3, "ffn2_bwd_act")
    (dwd2,), _ = _ffn_dwd(a3, dyb3, "ffn2_dwd")
    (dwg2, dwu2), _ = _ffn_dwgu(h3, dg3, du3, "ffn2_dwgu")
    ffn2_parts = [dwd2, dwg2, dwu2]
    (dx2, dmixb, d_ffn2n), (ffn2_other,) = _ffn_dh(dg3, du3, wg2, wu2, x2, ffn2_norm, dx3, "ffn2_dh", comm=[_Swap(ffn2_parts)])
    ffn2_sums = [_pair_add(p, o, f"pair_ffn2_{i}") for i, (p, o) in enumerate(zip(ffn2_parts, ffn2_other))]

    dya, dys, dga, dgs = _mix_bwd_gate(dmixb, wout, ya, ys, rest, "mix_bwd_gate")
    (dwout,), _ = _mm_tn(merged, dmixb, _tile(D, 1024), _tile(D, 1024), _tile(S, 1024), False, "dw_out")
    do, dz, dvec = _att_sg_dout(dya, dys, watt, wsg, oatt, "att_sg_dout")
    (dwatt,), _ = _mm_tn(oatt, dya, GROUP_W, 2 * Db, _tile(S, 1024), Db, "dw_att")
    (dwsg,), _ = _mm_tn(z, dys, SG_W, 2 * Db, _tile(S, 1024), Db, "dw_sg")
    mix_parts = [dwout.reshape(N_DEV, D // N_DEV, D), dwatt, dwsg]
    du, dvs, d_sgw, d_sgbT, d_lng, d_lnb = _sg_bwd(rest, dz, sgw, sgbT, sg_ln_g, sg_ln_b, "sg_bwd")
    dqs, dks, dvs_att, ffn2_got = [], [], [], []
    for gi, d in enumerate(DILATIONS):
        ride = [_Ici([ffn2_sums[0]])] if gi == 2 else []
        (dq, dk, dv), got_here = _att_bwd(qkv, tabs, do, lse, dvec, gi, d, f"att_bwd{gi}", comm=ride)
        ffn2_got += [g[0] for g in got_here]
        dqs.append(dq)
        dks.append(dk)
        dvs_att.append(dv)
    dproj = jnp.concatenate([t.astype(BF) for t in dqs + dks + dvs_att] + [du, dvs, dga, dgs, jnp.zeros((S, p_pad - P), BF)], axis=1)
    (dx1, dyb1, d_mixn), (ffn2_rest, mix_other) = _dh_rms_bwd([(dproj, winT)], False, PROJ_TK, x1, mix_norm, dx2, 0.5, "proj_dh",
                                                            comm=[_Ici(ffn2_sums[1:]), _Swap(mix_parts)], tail=win_tail)
    ffn2_got += ffn2_rest
    mix_sums = [_pair_add(p, o, f"pair_mix_{i}") for i, (p, o) in enumerate(zip(mix_parts, mix_other))]
    (dwd1,), (mix_got,) = _ffn_dwd(a1, dyb1, "ffn1_dwd", comm=[_Ici(mix_sums)])
    rows = lambda a: a.reshape(-1, 128)
    pad8 = lambda a: _pad_rows(a, -(-a.shape[0] // 8) * 8)
    small = [("sg_w", rows(d_sgw), sg_w, m_sg_w, v_sg_w), ("mix_norm", rows(d_mixn), mix_norm, m_mix_norm, v_mix_norm),
             ("ffn2_norm", rows(d_ffn2n), ffn2_norm, m_ffn2_norm, v_ffn2_norm), ("final_norm", rows(d_final), final_norm, m_final_norm, v_final_norm),
             ("sg_ln_g", rows(d_lng), sg_ln_g, m_sg_ln_g, v_sg_ln_g), ("sg_ln_b", rows(d_lnb), sg_ln_b, m_sg_ln_b, v_sg_ln_b),
             ("sg_b", d_sgbT[:, :SG_GROUPS].T, sg_b, m_sg_b, v_sg_b)]
    gpack = jnp.concatenate([pad8(g) for _, g, _, _, _ in small] + [pad8(loss_part)], axis=0)
    (dwin,), ((wd1_other,), (gpacks,)) = _mm_tn(dproj, h2, 512, D, S, False, "dw_in", mrows=P,
                                              comm=[_Swap([dwd1]), _Spread([gpack])])
    dwin = dwin.reshape(N_DEV, Pb, D)
    wd1_sum = _pair_add(dwd1, wd1_other, "pair_wd1")
    (dg1, du1), ((wd1_got,), (win_other,)) = _ffn_bwd_act(dyb1, wd1, g1, u1, "ffn1_bwd_act", comm=[_Ici([wd1_sum]), _Swap([dwin])])
    win_sum = _pair_add(dwin, win_other, "pair_win")
    (dwg1, dwu1), ((win_got,),) = _ffn_dwgu(h1, dg1, du1, "ffn1_dwgu", comm=[_Ici([win_sum])])
    gu_parts = [dwg1, dwu1]
    gu_other = _comm_only(_Swap(gu_parts), "swap_gu1")
    gu_sums = [_pair_add(p, o, f"pair_gu1_{i}") for i, (p, o) in enumerate(zip(gu_parts, gu_other))]
    (dx0, _, d_ffn1n), (gu_got,) = _ffn_dh(dg1, du1, wg1, wu1, xs, ffn1_norm, dx1, "ffn1_dh", comm=[_Ici(gu_sums)])

    got = dict(ffn2_w_down=ffn2_got[0], ffn2_w_gate=ffn2_got[1], ffn2_w_up=ffn2_got[2], w_out=mix_got[0], w_att_out=mix_got[1],
               w_sg_out=mix_got[2], w_in=win_got, ffn1_w_gate=gu_got[0], ffn1_w_up=gu_got[1], ffn1_w_down=wd1_got)
    moments = dict(ffn1_w_gate=(m_ffn1_w_gate, v_ffn1_w_gate), ffn1_w_up=(m_ffn1_w_up, v_ffn1_w_up),
                   ffn1_w_down=(m_ffn1_w_down, v_ffn1_w_down), w_in=(m_w_in, v_w_in), w_att_out=(m_w_att_out, v_w_att_out),
                   w_sg_out=(m_w_sg_out, v_w_sg_out), w_out=(m_w_out, v_w_out), ffn2_w_gate=(m_ffn2_w_gate, v_ffn2_w_gate),
                   ffn2_w_up=(m_ffn2_w_up, v_ffn2_w_up), ffn2_w_down=(m_ffn2_w_down, v_ffn2_w_down))
    res = {}
    for n in sharded:
        mm, vv = moments[n]
        outs = _adamw(got[n], wloc[n], local(n, mm), local(n, vv), "adamw_" + n)
        res[n] = [back(n, o) for o in outs]

    zero8 = jnp.zeros((8, 128), F32)
    wpack = jnp.concatenate([pad8(rows(w)) for _, _, w, _, _ in small] + [zero8], axis=0)
    mpack = jnp.concatenate([pad8(rows(m)) for _, _, _, m, _ in small] + [zero8], axis=0)
    vpack = jnp.concatenate([pad8(rows(v)) for _, _, _, _, v in small] + [zero8], axis=0)
    packs = _adamw(gpacks, wpack, mpack, vpack, "adamw_small")
    off = 0
    for n, g, w, _, _ in small:
        r = g.shape[0]
        res[n] = [p[off:off + r].reshape(w.shape) for p in packs]
        off += -(-r // 8) * 8
    loss = packs[0][off, 0]
    g_first = _small_allreduce(rows(d_ffn1n), "allreduce_ffn1_norm")
    res["ffn1_norm"] = [p.reshape(ffn1_norm.shape) for p in
                        _adamw(g_first[None], rows(ffn1_norm), rows(m_ffn1_norm), rows(v_ffn1_norm), "adamw_ffn1_norm")]

    order = ["ffn1_norm", "ffn1_w_gate", "ffn1_w_up", "ffn1_w_down", "mix_norm", "w_in", "sg_ln_g", "sg_ln_b", "sg_w", "sg_b",
             "w_att_out", "w_sg_out", "w_out", "ffn2_norm", "ffn2_w_gate", "ffn2_w_up", "ffn2_w_down", "final_norm"]
    return (loss, dx0[None], *[res[n][0] for n in order], *[res[n][1] for n in order], *[res[n][2] for n in order],
            *[res[n][3] for n in order])
```

```python
import math

import jax
import jax.numpy as jnp
from jax import lax
from jax.experimental import pallas as pl
from jax.experimental.pallas import tpu as pltpu

BF = jnp.bfloat16
F32 = jnp.float32
MESH = pl.DeviceIdType.MESH
N_DEV = 8
N_CHIP = 4

HEAD_DIM = 128
HEADS_PER_GROUP = 4
GROUP_W = HEADS_PER_GROUP * HEAD_DIM
DILATIONS = (1, 4, 16)
ATT_W = len(DILATIONS) * GROUP_W
SG_W = 1536
SG_GROUPS = 12
BLK = 128
ROPE_DIM = 32
ROPE_THETA = 500000.0
NORM_EPS = 1e-6
LN_EPS = 1e-5
Q_OFF, K_OFF, V_OFF, U_OFF, VS_OFF, GA_OFF = 0, ATT_W, 2 * ATT_W, 3 * ATT_W, 3 * ATT_W + SG_W, 3 * ATT_W + 2 * SG_W

ADAM_LR, ADAM_B1, ADAM_B2, ADAM_EPS, ADAM_WD, ADAM_STEP = 0.001, 0.9, 0.999, 1e-08, 0.01, 10

VMEM_LIMIT = 56 * 1024 * 1024
NEG = -1e30
ANY = pl.BlockSpec(memory_space=pl.ANY)
EPI_ROWS = 128
ACC_COLS = 512
FFN_PAIR = 2
FFN_ROWS = 1024
PROJ_TK = 1536
R_U, R_VS, R_GA = 0, SG_W, 2 * SG_W


def _once(shape, index_map):
    return pl.BlockSpec(shape, index_map, pipeline_mode=pl.Buffered(1))


def _tile(n, pref):
    t = min(n, pref)
    while n % t:
        t //= 2
    return t


def _nt(a, b):
    return lax.dot_general(a, b, (((1,), (1,)), ((), ())), preferred_element_type=F32)


def _tn(a, b):
    return lax.dot_general(a, b, (((0,), (0,)), ((), ())), preferred_element_type=F32)


def _nn(a, b):
    return jnp.dot(a, b, preferred_element_type=F32)


def _acc_dots(acc_ref, terms, transposed_rhs=False):
    n = acc_ref.shape[1]
    width = min(n, ACC_COLS)
    for c0 in range(0, n, width):
        cols = slice(c0, c0 + width)
        tot = None
        for lhs, rhs in terms:
            part = _nt(lhs, rhs(cols)) if transposed_rhs else _nn(lhs, rhs(cols))
            tot = part if tot is None else tot + part
        acc_ref[:, cols] += tot


def _gauss_cdf(x):
    return 0.5 * (1.0 + lax.erf(x * (2.0 ** -0.5)))


def _gelu_grad(x, cdf):
    return cdf + x * jnp.exp(-0.5 * x * x) * (1.0 / math.sqrt(2.0 * math.pi))


def _place():
    x, y, c = lax.axis_index("x"), lax.axis_index("y"), lax.axis_index("c")
    return x, y, c


def _flip(v, bit):
    return 1 - v if bit else v


class _Gather:
    def __init__(self, shards, mid_frac=1.0, relay_frac=0.5):
        self.arrays = list(shards)
        self.relay_frac = relay_frac
        self.mid_frac = mid_frac
        nw = len(shards)
        self.out_shape = [jax.ShapeDtypeStruct((N_DEV,) + s.shape, s.dtype) for s in shards]
        self.scratch = [pltpu.SemaphoreType.DMA((nw, 7)), pltpu.SemaphoreType.DMA((nw, 7)), pltpu.SemaphoreType.DMA((nw,))]

    def _parts(self, ins, outs, sems):
        x, y, c = _place()
        send, recv, loc = sems
        south = c == 0
        near = (jnp.where(south, x, 1 - x), jnp.where(south, 1 - y, y), c)
        far = (jnp.where(south, 1 - x, x), jnp.where(south, y, 1 - y), c)
        diag = (1 - x, 1 - y, c)

        def copy(k, s, block, to, src=None):
            dst = outs[k].at[4 * block[0] + 2 * block[1] + block[2]]
            return pltpu.make_async_remote_copy(src_ref=dst if src is None else src, dst_ref=dst, send_sem=send.at[k, s],
                                                recv_sem=recv.at[k, s], device_id=to, device_id_type=MESH)

        def first(k):
            me = (x, y, c)
            return [copy(k, 0, me, (x, y, 1 - c), src=ins[k]), copy(k, 1, me, (1 - x, y, c), src=ins[k]),
                    copy(k, 2, me, (x, 1 - y, c), src=ins[k])]

        def local(k):
            return pltpu.make_async_copy(ins[k], outs[k].at[4 * x + 2 * y + c], loc.at[k])

        return x, y, c, near, far, diag, copy, first, local

    def start(self, ins, outs, sems):
        *_, first, local = self._parts(ins, outs, sems)
        for k in range(len(ins)):
            local(k).start()
            for cp in first(k):
                cp.start()

    def relay(self, ins, outs, sems):
        x, y, c, near, far, _, copy, _, _ = self._parts(ins, outs, sems)
        for k in range(len(ins)):
            copy(k, 2 - c, near, (x, y, c)).wait_recv()
            copy(k, 3, near, far).start()
            copy(k, 5 - c, near, (x, y, 1 - c)).start()

    def mid(self, ins, outs, sems):
        x, y, c, _, far, diag, copy, _, _ = self._parts(ins, outs, sems)
        for k in range(len(ins)):
            copy(k, 1 + c, far, (x, y, c)).wait_recv()
            copy(k, 4 + c, far, (x, y, 1 - c)).start()
            copy(k, 3, diag, (x, y, c)).wait_recv()
            copy(k, 6, diag, (x, y, 1 - c)).start()

    def finish(self, ins, outs, sems):
        x, y, c, near, _, _, copy, first, local = self._parts(ins, outs, sems)
        sib = (x, y, 1 - c)
        for k in range(len(ins)):
            copy(k, 0, sib, (x, y, c)).wait_recv()
            copy(k, 4, (1 - x, y, 1 - c), (x, y, c)).wait_recv()
            copy(k, 5, (x, 1 - y, 1 - c), (x, y, c)).wait_recv()
            copy(k, 6, (1 - x, 1 - y, 1 - c), (x, y, c)).wait_recv()
        for k in range(len(ins)):
            for cp in first(k):
                cp.wait_send()
            for s in (3, 4, 5, 6):
                copy(k, s, near, sib).wait_send()
            local(k).wait()


class _Swap:
    def __init__(self, parts):
        self.arrays = list(parts)
        nw = len(parts)
        self.out_shape = [jax.ShapeDtypeStruct((N_CHIP,) + p.shape[1:], p.dtype) for p in parts]
        self.scratch = [pltpu.SemaphoreType.DMA((nw, N_CHIP)), pltpu.SemaphoreType.DMA((nw, N_CHIP))]

    def _copy(self, ins, outs, sems, k, q):
        x, y, c = _place()
        return pltpu.make_async_remote_copy(src_ref=ins[k].at[2 * q + 1 - c], dst_ref=outs[k].at[q], send_sem=sems[0].at[k, q],
                                            recv_sem=sems[1].at[k, q], device_id=(x, y, 1 - c), device_id_type=MESH)

    mid_frac = None

    def start(self, ins, outs, sems):
        for k in range(len(ins)):
            for q in range(N_CHIP):
                self._copy(ins, outs, sems, k, q).start()

    def finish(self, ins, outs, sems):
        for k in range(len(ins)):
            for q in range(N_CHIP):
                self._copy(ins, outs, sems, k, q).wait()


class _Ici:
    mid_frac = None

    def __init__(self, sums):
        self.arrays = list(sums)
        nw = len(sums)
        self.out_shape = [jax.ShapeDtypeStruct(s.shape, s.dtype) for s in sums]
        self.scratch = [pltpu.SemaphoreType.DMA((nw, 3)), pltpu.SemaphoreType.DMA((nw, 3)), pltpu.SemaphoreType.DMA((nw,))]

    def _copies(self, ins, outs, sems, k):
        x, y, c = _place()
        myq = 2 * x + y
        out = []
        for r in range(1, N_CHIP):
            px, py = _flip(x, r & 2), _flip(y, r & 1)
            pq = 2 * px + py
            mk = lambda dst: pltpu.make_async_remote_copy(src_ref=ins[k].at[pq], dst_ref=dst, send_sem=sems[0].at[k, r - 1],
                                                          recv_sem=sems[1].at[k, r - 1], device_id=(px, py, c), device_id_type=MESH)
            out.append((mk(outs[k].at[myq]), mk(outs[k].at[pq])))
        return out, pltpu.make_async_copy(ins[k].at[myq], outs[k].at[myq], sems[2].at[k])

    def start(self, ins, outs, sems):
        for k in range(len(ins)):
            remote, local = self._copies(ins, outs, sems, k)
            local.start()
            for snd, _ in remote:
                snd.start()

    def finish(self, ins, outs, sems):
        for k in range(len(ins)):
            remote, local = self._copies(ins, outs, sems, k)
            for snd, rcv in remote:
                rcv.wait_recv()
                snd.wait_send()
            local.wait()


class _Spread:
    mid_frac = None

    def __init__(self, arrays):
        self.arrays = list(arrays)
        nw = len(arrays)
        self.out_shape = [jax.ShapeDtypeStruct((N_DEV,) + a.shape, a.dtype) for a in arrays]
        self.scratch = [pltpu.SemaphoreType.DMA((nw, 7)), pltpu.SemaphoreType.DMA((nw, 7)), pltpu.SemaphoreType.DMA((nw,))]

    def _copies(self, ins, outs, sems, k):
        x, y, c = _place()
        me = 4 * x + 2 * y + c
        out = []
        for r in range(1, N_DEV):
            px, py, pc = _flip(x, r & 4), _flip(y, r & 2), _flip(c, r & 1)
            peer = 4 * px + 2 * py + pc
            mk = lambda dst: pltpu.make_async_remote_copy(src_ref=ins[k], dst_ref=dst, send_sem=sems[0].at[k, r - 1],
                                                          recv_sem=sems[1].at[k, r - 1], device_id=(px, py, pc), device_id_type=MESH)
            out.append((mk(outs[k].at[me]), mk(outs[k].at[peer])))
        return out, pltpu.make_async_copy(ins[k], outs[k].at[me], sems[2].at[k])

    def start(self, ins, outs, sems):
        for k in range(len(ins)):
            remote, local = self._copies(ins, outs, sems, k)
            local.start()
            for snd, _ in remote:
                snd.start()

    def finish(self, ins, outs, sems):
        for k in range(len(ins)):
            remote, local = self._copies(ins, outs, sems, k)
            for snd, rcv in remote:
                rcv.wait_recv()
                snd.wait_send()
            local.wait()


def _call(body, *, grid, in_specs, out_specs, out_shape, name, args, scratch=(), comm=()):
    comm = list(comm)
    n_in, n_out, n_scr = len(in_specs), len(out_specs), len(scratch)
    total = math.prod(grid) if grid else 1

    def wrapped(*refs):
        p = n_in
        cin = []
        for cm in comm:
            cin.append(refs[p:p + len(cm.arrays)])
            p += len(cm.arrays)
        own_out = refs[p:p + n_out]
        p += n_out
        cout = []
        for cm in comm:
            cout.append(refs[p:p + len(cm.arrays)])
            p += len(cm.arrays)
        own_scr = refs[p:p + n_scr]
        p += n_scr
        csem = []
        for cm in comm:
            csem.append(refs[p:p + len(cm.scratch)])
            p += len(cm.scratch)
        step = 0
        for axis, g in enumerate(grid):
            step = step * g + pl.program_id(axis)

        def at(when, what):
            if total == 1:
                what()
            else:
                pl.when(step == when)(what)

        def starts():
            for cm, i, o, s in zip(comm, cin, cout, csem):
                cm.start(i, o, s)

        def finishes():
            for cm, i, o, s in zip(comm, cin, cout, csem):
                cm.finish(i, o, s)

        if comm:
            at(0, starts)
        if body is not None:
            body(*refs[:n_in], *own_out, *own_scr)
        for cm, i, o, s in zip(comm, cin, cout, csem):
            if cm.mid_frac is not None:
                at(min(total - 1, int(total * cm.relay_frac)), lambda cm=cm, i=i, o=o, s=s: cm.relay(i, o, s))
                at(min(total - 1, int(total * cm.mid_frac)), lambda cm=cm, i=i, o=o, s=s: cm.mid(i, o, s))
        if comm:
            at(total - 1, finishes)

    kw = dict(grid=tuple(grid)) if grid else {}
    outs = pl.pallas_call(
        wrapped, name=name, **kw,
        in_specs=list(in_specs) + [ANY for cm in comm for _ in cm.arrays],
        out_specs=list(out_specs) + [ANY for cm in comm for _ in cm.arrays],
        out_shape=list(out_shape) + [s for cm in comm for s in cm.out_shape],
        scratch_shapes=list(scratch) + [s for cm in comm for s in cm.scratch],
        compiler_params=pltpu.CompilerParams(dimension_semantics=("arbitrary",) * len(grid), vmem_limit_bytes=VMEM_LIMIT),
    )(*args, *[a for cm in comm for a in cm.arrays])
    own, p, per = list(outs[:n_out]), n_out, []
    for cm in comm:
        per.append(list(outs[p:p + len(cm.arrays)]))
        p += len(cm.arrays)
    return own, per


def _comm_only(cm, name):
    return _call(None, grid=(), in_specs=[], out_specs=[], out_shape=[], name=name, args=[], comm=[cm])[1][0]


def _rms_fwd(x, g, name, comm=()):
    S, D = x.shape
    tm = _tile(S, 512)

    def body(x_ref, g_ref, o_ref):
        xv = x_ref[...]
        r = lax.rsqrt(jnp.mean(xv * xv, axis=-1, keepdims=True) + NORM_EPS)
        o_ref[...] = (xv * r * g_ref[...]).astype(BF)

    return _call(body, grid=(S // tm,), name=name, args=[x, g], comm=comm,
                 in_specs=[pl.BlockSpec((tm, D), lambda i: (i, 0)), pl.BlockSpec((1, D), lambda i: (0, 0))],
                 out_specs=[pl.BlockSpec((tm, D), lambda i: (i, 0))], out_shape=[jax.ShapeDtypeStruct((S, D), BF)])


def _ffn_up(h, wg, wu, name, comm=()):
    S, D = h.shape
    nb, Fb, _ = wg.shape
    tm = _tile(S, FFN_ROWS)

    def body(h_ref, wg_ref, wu_ref, g_ref, u_ref, a_ref):
        hv = h_ref[...]
        g = _nt(hv, wg_ref[0])
        u = _nt(hv, wu_ref[0])
        g_ref[0] = g.astype(BF)
        u_ref[0] = u.astype(BF)
        a_ref[0] = (g * jax.nn.sigmoid(g) * u).astype(BF)

    act = pl.BlockSpec((1, tm, Fb), lambda j, i: (j, i, 0))
    w = pl.BlockSpec((1, Fb, D), lambda j, i: (j, 0, 0))
    shp = jax.ShapeDtypeStruct((nb, S, Fb), BF)
    return _call(body, grid=(nb, S // tm), name=name, args=[h, wg, wu], comm=comm,
                 in_specs=[pl.BlockSpec((tm, D), lambda j, i: (i, 0)), w, w], out_specs=[act, act, act], out_shape=[shp, shp, shp])


def _ffn_gate(h, wg, name, comm=()):
    S, D = h.shape
    nb, Fb, _ = wg.shape
    tm = _tile(S, FFN_ROWS)

    def body(h_ref, wg_ref, g_ref):
        g_ref[0] = _nt(h_ref[...], wg_ref[0]).astype(BF)

    act = pl.BlockSpec((1, tm, Fb), lambda j, i: (j, i, 0))
    return _call(body, grid=(nb, S // tm), name=name, args=[h, wg], comm=comm,
                 in_specs=[pl.BlockSpec((tm, D), lambda j, i: (i, 0)), pl.BlockSpec((1, Fb, D), lambda j, i: (j, 0, 0))],
                 out_specs=[act], out_shape=[jax.ShapeDtypeStruct((nb, S, Fb), BF)])


def _ffn_up_act(h, wu, g, name, comm=()):
    S, D = h.shape
    nb, Fb, _ = wu.shape
    tm = _tile(S, FFN_ROWS)

    def body(h_ref, wu_ref, g_ref, u_ref, a_ref):
        u = _nt(h_ref[...], wu_ref[0])
        gv = g_ref[0].astype(F32)
        u_ref[0] = u.astype(BF)
        a_ref[0] = (gv * jax.nn.sigmoid(gv) * u).astype(BF)

    act = pl.BlockSpec((1, tm, Fb), lambda j, i: (j, i, 0))
    shp = jax.ShapeDtypeStruct((nb, S, Fb), BF)
    return _call(body, grid=(nb, S // tm), name=name, args=[h, wu, g], comm=comm,
                 in_specs=[pl.BlockSpec((tm, D), lambda j, i: (i, 0)), pl.BlockSpec((1, Fb, D), lambda j, i: (j, 0, 0)), act],
                 out_specs=[act, act], out_shape=[shp, shp])


def _ffn_down_norm(a, wd, x, gn, name, comm=()):
    nb, S, Fb = a.shape
    D = wd.shape[2]
    tm = _tile(S, 512)

    nj = nb // FFN_PAIR

    def body(a_ref, wd_ref, x_ref, gn_ref, xo_ref, hn_ref, acc_ref):
        j = pl.program_id(1)

        @pl.when(j == 0)
        def _():
            acc_ref[...] = jnp.zeros_like(acc_ref)

        _acc_dots(acc_ref, [(a_ref[b], lambda cols, b=b: wd_ref[b, :, cols]) for b in range(FFN_PAIR)])

        @pl.when(j == nj - 1)
        def _():
            def chunk(t, carry):
                rows = pl.ds(pl.multiple_of(t * EPI_ROWS, EPI_ROWS), EPI_ROWS)
                xo = x_ref[rows, :] + 0.5 * acc_ref[rows, :]
                r = lax.rsqrt(jnp.mean(xo * xo, axis=-1, keepdims=True) + NORM_EPS)
                xo_ref[rows, :] = xo
                hn_ref[rows, :] = (xo * r * gn_ref[...]).astype(BF)
                return carry

            lax.fori_loop(0, tm // EPI_ROWS, chunk, 0)

    row = pl.BlockSpec((tm, D), lambda i, j: (i, 0))
    return _call(body, grid=(S // tm, nj), name=name, args=[a, wd, x, gn], comm=comm,
                 in_specs=[pl.BlockSpec((FFN_PAIR, tm, Fb), lambda i, j: (j, i, 0)), pl.BlockSpec((FFN_PAIR, Fb, D), lambda i, j: (j, 0, 0)),
                           row, pl.BlockSpec((1, D), lambda i, j: (0, 0))],
                 out_specs=[row, row], out_shape=[jax.ShapeDtypeStruct((S, D), F32), jax.ShapeDtypeStruct((S, D), BF)],
                 scratch=[pltpu.VMEM((tm, D), F32)])


def _ffn_down_loss(a, wd, x, gf, tgt, name):
    nb, S, Fb = a.shape
    D = wd.shape[2]
    tm = _tile(S, 512)

    nj = nb // FFN_PAIR

    def body(a_ref, wd_ref, x_ref, gf_ref, t_ref, dx_ref, dxb_ref, dgf_ref, loss_ref, acc_ref):
        i, j = pl.program_id(0), pl.program_id(1)

        @pl.when(j == 0)
        def _():
            acc_ref[...] = jnp.zeros_like(acc_ref)

        _acc_dots(acc_ref, [(a_ref[b], lambda cols, b=b: wd_ref[b, :, cols]) for b in range(FFN_PAIR)])

        @pl.when((j == nj - 1) & (i == 0))
        def _():
            dgf_ref[...] = jnp.zeros_like(dgf_ref)
            loss_ref[...] = jnp.zeros_like(loss_ref)

        @pl.when(j == nj - 1)
        def _():
            def chunk(t, carry):
                rows = pl.ds(pl.multiple_of(t * EPI_ROWS, EPI_ROWS), EPI_ROWS)
                xo = x_ref[rows, :] + 0.5 * acc_ref[rows, :]
                r = lax.rsqrt(jnp.mean(xo * xo, axis=-1, keepdims=True) + NORM_EPS)
                xh = xo * r
                gf = gf_ref[...]
                e = xh * gf - t_ref[rows, :]
                loss_ref[...] += jnp.sum(jnp.mean(e * e, axis=-1, keepdims=True), axis=0, keepdims=True) * 0.5
                dy = e * (1.0 / D)
                dgf_ref[...] += jnp.sum(dy * xh, axis=0, keepdims=True)
                dxh = dy * gf
                dx = r * (dxh - xh * jnp.mean(dxh * xh, axis=-1, keepdims=True))
                dx_ref[rows, :] = dx
                dxb_ref[rows, :] = (0.5 * dx).astype(BF)
                return carry

            lax.fori_loop(0, tm // EPI_ROWS, chunk, 0)

    row = pl.BlockSpec((tm, D), lambda i, j: (i, 0))
    once = row
    vec = pl.BlockSpec((1, D), lambda i, j: (0, 0))
    return _call(body, grid=(S // tm, nj), name=name, args=[a, wd, x, gf, tgt],
                 in_specs=[pl.BlockSpec((FFN_PAIR, tm, Fb), lambda i, j: (j, i, 0)), pl.BlockSpec((FFN_PAIR, Fb, D), lambda i, j: (j, 0, 0)),
                           once, vec, once],
                 out_specs=[row, row, vec, pl.BlockSpec((1, 128), lambda i, j: (0, 0))],
                 out_shape=[jax.ShapeDtypeStruct((S, D), F32), jax.ShapeDtypeStruct((S, D), BF), jax.ShapeDtypeStruct((1, D), F32),
                            jax.ShapeDtypeStruct((1, 128), F32)],
                 scratch=[pltpu.VMEM((tm, D), F32)])[0]


def _ffn_bwd_act(dyb, wd, g, u, name, comm=()):
    S, D = dyb.shape
    nb, Fb, _ = wd.shape
    tm = _tile(S, FFN_ROWS)

    def body(dy_ref, wd_ref, g_ref, u_ref, dg_ref, du_ref):
        rows = pl.ds(pl.multiple_of(pl.program_id(1) * tm, tm), tm)
        da = _nt(dy_ref[rows, :], wd_ref[0])
        gv = g_ref[0].astype(F32)
        uv = u_ref[0].astype(F32)
        sg = jax.nn.sigmoid(gv)
        du_ref[0] = (da * gv * sg).astype(BF)
        dg_ref[0] = (da * uv * sg * (1.0 + gv * (1.0 - sg))).astype(BF)

    act = pl.BlockSpec((1, tm, Fb), lambda j, i: (j, i, 0))
    shp = jax.ShapeDtypeStruct((nb, S, Fb), BF)
    return _call(body, grid=(nb, S // tm), name=name, args=[dyb, wd, g, u], comm=comm,
                 in_specs=[pl.BlockSpec((S, D), lambda j, i: (0, 0)), pl.BlockSpec((1, Fb, D), lambda j, i: (j, 0, 0)), act, act],
                 out_specs=[act, act], out_shape=[shp, shp])


def _ffn_dwd(a, dyb, name, comm=()):
    nb, S, Fb = a.shape
    D = dyb.shape[1]
    ts = S
    ns = S // ts

    def body(a_ref, dy_ref, o_ref, acc_ref):
        s = pl.program_id(1)

        @pl.when(s == 0)
        def _():
            acc_ref[...] = jnp.zeros_like(acc_ref)

        acc_ref[...] += _tn(a_ref[0], dy_ref[...])

        @pl.when(s == ns - 1)
        def _():
            o_ref[0] = acc_ref[...].astype(BF)

    return _call(body, grid=(nb, ns), name=name, args=[a, dyb], comm=comm,
                 in_specs=[pl.BlockSpec((1, ts, Fb), lambda j, s: (j, s, 0)), pl.BlockSpec((ts, D), lambda j, s: (s, 0))],
                 out_specs=[pl.BlockSpec((1, Fb, D), lambda j, s: (j, 0, 0))], out_shape=[jax.ShapeDtypeStruct((nb, Fb, D), BF)],
                 scratch=[pltpu.VMEM((Fb, D), F32)])


def _ffn_dwgu(h, dg, du, name, comm=()):
    S, D = h.shape
    nb, _, Fb = dg.shape
    ts = _tile(S, FFN_ROWS)
    ns = S // ts

    def body(h_ref, dg_ref, du_ref, og_ref, ou_ref, accg_ref, accu_ref):
        s = pl.program_id(1)

        @pl.when(s == 0)
        def _():
            accg_ref[...] = jnp.zeros_like(accg_ref)
            accu_ref[...] = jnp.zeros_like(accu_ref)

        hv = h_ref[...]
        accg_ref[...] += _tn(dg_ref[0], hv)
        accu_ref[...] += _tn(du_ref[0], hv)

        @pl.when(s == ns - 1)
        def _():
            og_ref[0] = accg_ref[...].astype(BF)
            ou_ref[0] = accu_ref[...].astype(BF)

    act = pl.BlockSpec((1, ts, Fb), lambda j, s: (j, s, 0))
    out = pl.BlockSpec((1, Fb, D), lambda j, s: (j, 0, 0))
    shp = jax.ShapeDtypeStruct((nb, Fb, D), BF)
    return _call(body, grid=(nb, ns), name=name, args=[h, dg, du], comm=comm,
                 in_specs=[pl.BlockSpec((ts, D), lambda j, s: (s, 0)), act, act], out_specs=[out, out], out_shape=[shp, shp],
                 scratch=[pltpu.VMEM((Fb, D), F32), pltpu.VMEM((Fb, D), F32)])


def _dh_rms_bwd(pairs, blocked, tk, x, gn, dxo, out_scale, name, comm=(), tail=None):
    S, D = x.shape
    nk = pairs[0][0].shape[0] if blocked else pairs[0][0].shape[1] // tk
    tm = _tile(S, 512)
    npair = len(pairs)
    assert blocked or (npair == 1 and tail is not None and nk >= 2)
    nin = 2 * npair + (0 if blocked else 1)

    def body(*refs):
        ins = refs[:nin]
        x_ref, gn_ref, dxo_ref, dx_ref, dxb_ref, dgn_ref, acc_ref = refs[nin:]
        i, k = pl.program_id(0), pl.program_id(1)

        @pl.when(k == 0)
        def _():
            acc_ref[...] = jnp.zeros_like(acc_ref)

        if blocked:
            _acc_dots(acc_ref, [(ins[2 * p][0], lambda cols, r=ins[2 * p + 1]: r[0, :, cols]) for p in range(npair)])
        else:
            @pl.when(k < nk - 1)
            def _():
                _acc_dots(acc_ref, [(ins[0][...], lambda cols: ins[1][:, cols])])

            @pl.when(k == nk - 1)
            def _():
                _acc_dots(acc_ref, [(ins[0][...], lambda cols: ins[2][:, cols])])

        @pl.when((k == nk - 1) & (i == 0))
        def _():
            dgn_ref[...] = jnp.zeros_like(dgn_ref)

        @pl.when(k == nk - 1)
        def _():
            def chunk(t, carry):
                rows = pl.ds(pl.multiple_of(t * EPI_ROWS, EPI_ROWS), EPI_ROWS)
                xv = x_ref[rows, :]
                r = lax.rsqrt(jnp.mean(xv * xv, axis=-1, keepdims=True) + NORM_EPS)
                xh = xv * r
                dh = acc_ref[rows, :]
                dgn_ref[...] += jnp.sum(dh * xh, axis=0, keepdims=True)
                dxh = dh * gn_ref[...]
                dx = dxo_ref[rows, :] + r * (dxh - xh * jnp.mean(dxh * xh, axis=-1, keepdims=True))
                dx_ref[rows, :] = dx
                dxb_ref[rows, :] = (out_scale * dx).astype(BF)
                return carry

            lax.fori_loop(0, tm // EPI_ROWS, chunk, 0)

    if blocked:
        mats = [pl.BlockSpec((1, tm, tk), lambda i, k: (k, i, 0)), pl.BlockSpec((1, tk, D), lambda i, k: (k, 0, 0))] * npair
        flat = [t for pr in pairs for t in pr]
    else:
        mats = [pl.BlockSpec((tm, tk), lambda i, k: (i, k)), pl.BlockSpec((tk, D), lambda i, k: (jnp.minimum(k, nk - 2), 0)),
                pl.BlockSpec((tk, D), lambda i, k: (0, 0))]
        flat = [*pairs[0], tail]
    row = pl.BlockSpec((tm, D), lambda i, k: (i, 0))
    once = row
    vec = pl.BlockSpec((1, D), lambda i, k: (0, 0))
    return _call(body, grid=(S // tm, nk), name=name, args=[*flat, x, gn, dxo], comm=comm,
                 in_specs=mats + [once, vec, once], out_specs=[row, row, vec],
                 out_shape=[jax.ShapeDtypeStruct((S, D), F32), jax.ShapeDtypeStruct((S, D), BF), jax.ShapeDtypeStruct((1, D), F32)],
                 scratch=[pltpu.VMEM((tm, D), F32)])


def _ffn_dh(dg, du, wgT, wuT, x, gn, dxo, name, comm=()):
    S, D = x.shape
    nb, _, Fb = dg.shape
    tm = _tile(S, 512)
    ni, er = S // tm, tm // nb
    assert er % 16 == 0

    def body(dg_ref, wg_ref, du_ref, wu_ref, x_ref, gn_ref, dxo_ref, dx_ref, dxb_ref, dgn_ref, acc_ref):
        i, k = pl.program_id(0), pl.program_id(1)
        slot = i % 2

        @pl.when((i == 0) & (k == 0))
        def _():
            acc_ref[...] = jnp.zeros_like(acc_ref)
            dgn_ref[...] = jnp.zeros_like(dgn_ref)

        @pl.when((i > 0) & (k == 0))
        def _():
            acc_ref[slot] = jnp.zeros((tm, D), F32)

        def finish_rows():
            rows = pl.ds(pl.multiple_of(k * er, er), er)
            xv = x_ref[rows, :]
            r = lax.rsqrt(jnp.mean(xv * xv, axis=-1, keepdims=True) + NORM_EPS)
            xh = xv * r
            dh = acc_ref[1 - slot, rows, :]
            dgn_ref[...] += jnp.where(i > 0, jnp.sum(dh * xh, axis=0, keepdims=True), 0.0)
            dxh = dh * gn_ref[...]
            dx = dxo_ref[rows, :] + r * (dxh - xh * jnp.mean(dxh * xh, axis=-1, keepdims=True))
            dx_ref[rows, :] = dx
            dxb_ref[rows, :] = dx.astype(BF)

        @pl.when(i < ni)
        def _():
            _acc_dots(acc_ref.at[slot], [(dg_ref[0], lambda cols: wg_ref[0, :, cols]), (du_ref[0], lambda cols: wu_ref[0, :, cols])])
            finish_rows()

        @pl.when(i == ni)
        def _():
            finish_rows()

    last = lambda i, k: jnp.where(i == ni, nb - 1, k)
    act = pl.BlockSpec((1, tm, Fb), lambda i, k: (last(i, k), jnp.minimum(i, ni - 1), 0))
    w = pl.BlockSpec((1, Fb, D), lambda i, k: (last(i, k), 0, 0))
    prev = pl.BlockSpec((tm, D), lambda i, k: (jnp.maximum(i - 1, 0), 0))
    vec = pl.BlockSpec((1, D), lambda i, k: (0, 0))
    return _call(body, grid=(ni + 1, nb), name=name, args=[dg, wgT, du, wuT, x, gn, dxo], comm=comm,
                 in_specs=[act, w, act, w, prev, vec, prev], out_specs=[prev, prev, vec],
                 out_shape=[jax.ShapeDtypeStruct((S, D), F32), jax.ShapeDtypeStruct((S, D), BF), jax.ShapeDtypeStruct((1, D), F32)],
                 scratch=[pltpu.VMEM((2, tm, D), F32)])


def _proj_split(a, bT, tm, tn, split, name, comm=()):
    M, K = a.shape
    N = bT.shape[0]
    n_first = split // tn

    def body(a_ref, b_ref, first_ref, rest_ref):
        n = pl.program_id(1)
        y = _nt(a_ref[...], b_ref[...])

        @pl.when(n < n_first)
        def _():
            first_ref[...] = y

        @pl.when(n >= n_first)
        def _():
            rest_ref[...] = y.astype(BF)

    return _call(body, grid=(M // tm, N // tn), name=name, args=[a, bT], comm=comm,
                 in_specs=[pl.BlockSpec((tm, K), lambda i, n: (i, 0)), pl.BlockSpec((tn, K), lambda i, n: (n, 0))],
                 out_specs=[pl.BlockSpec((tm, tn), lambda i, n: (i, jnp.minimum(n, n_first - 1))),
                            pl.BlockSpec((tm, tn), lambda i, n: (i, jnp.maximum(n - n_first, 0)))],
                 out_shape=[jax.ShapeDtypeStruct((M, split), F32), jax.ShapeDtypeStruct((M, N - split), BF)])


def _mm_tn(a, b, tm, tn, ts, blocked, name, comm=(), mrows=None):
    S, M = a.shape[0], (a.shape[1] if mrows is None else mrows)
    N = b.shape[1]
    ns = S // ts
    per_tile = tn // blocked if blocked else 0

    def body(a_ref, b_ref, o_ref, acc_ref):
        s = pl.program_id(2)

        @pl.when(s == 0)
        def _():
            acc_ref[...] = jnp.zeros_like(acc_ref)

        acc_ref[...] += _tn(a_ref[...], b_ref[...])

        @pl.when(s == ns - 1)
        def _():
            if blocked:
                for t in range(per_tile):
                    o_ref[t] = acc_ref[:, t * blocked:(t + 1) * blocked].astype(BF)
            else:
                o_ref[...] = acc_ref[...].astype(BF)

    if blocked:
        ospec = pl.BlockSpec((per_tile, tm, blocked), lambda i, n, s: (n, i, 0))
        oshape = jax.ShapeDtypeStruct((N // blocked, M, blocked), BF)
    else:
        ospec = pl.BlockSpec((tm, tn), lambda i, n, s: (i, n))
        oshape = jax.ShapeDtypeStruct((M, N), BF)
    return _call(body, grid=(M // tm, N // tn, ns), name=name, args=[a, b], comm=comm,
                 in_specs=[pl.BlockSpec((ts, tm), lambda i, n, s: (s, i)), pl.BlockSpec((ts, tn), lambda i, n, s: (s, n))],
                 out_specs=[ospec], out_shape=[oshape], scratch=[pltpu.VMEM((tm, tn), F32)])


def _rope_tables(S):
    half = ROPE_DIM // 2
    inv_freq = ROPE_THETA ** (-jnp.arange(0, ROPE_DIM, 2, dtype=F32) / ROPE_DIM)
    ang = jnp.arange(S, dtype=F32)[:, None] * inv_freq[None, :]
    cos, sin = jnp.cos(ang), jnp.sin(ang)
    zeros = jnp.zeros((S, HEAD_DIM - ROPE_DIM), F32)
    c = jnp.concatenate([cos, cos, jnp.ones((S, HEAD_DIM - ROPE_DIM), F32)], axis=1)
    sm = jnp.concatenate([-sin, jnp.zeros((S, half), F32), zeros], axis=1)
    sp = jnp.concatenate([jnp.zeros((S, half), F32), sin, zeros], axis=1)
    return c, sm, sp


def _rope(t, c, sm, sp):
    return t * c + pltpu.roll(t, HEAD_DIM - ROPE_DIM // 2, 1) * sm + pltpu.roll(t, ROPE_DIM // 2, 1) * sp


def _rope_t(dy, c, sm, sp):
    return dy * c + pltpu.roll(dy * sm, ROPE_DIM // 2, 1) + pltpu.roll(dy * sp, HEAD_DIM - ROPE_DIM // 2, 1)


def _att_mask(i):
    qi = lax.broadcasted_iota(jnp.int32, (BLK, 2 * BLK), 0)
    kj = lax.broadcasted_iota(jnp.int32, (BLK, 2 * BLK), 1)
    diff = qi + BLK - kj
    first_key = jnp.where(i > 0, 0, BLK)
    return (diff >= 0) & (diff <= BLK) & (kj >= first_key)


def _res_rows(r, i, n, d):
    if d == 1:
        return pl.ds(pl.multiple_of(i * n, n), n)
    return pl.ds(r + i * (n * d), n, stride=d)


def _att_specs(S, gi):
    def sect(off):
        base = (off + gi * GROUP_W) // HEAD_DIM
        return _once((S, HEAD_DIM), lambda hh: (0, base + hh))

    tab = pl.BlockSpec((S, HEAD_DIM), lambda hh: (0, 0))
    head = pl.BlockSpec((S, HEAD_DIM), lambda hh: (0, hh))
    return sect, tab, head


def _each_residue(d, fn):
    if d == 1:
        fn(0)
    else:
        lax.fori_loop(0, d, lambda r, carry: (fn(r), carry)[1], 0)


def _att_fwd(qkv, tabs, gi, d, name, comm=()):
    S = qkv.shape[0]
    L = S // d
    sect, tab, head = _att_specs(S, gi)
    nblk = L // BLK
    scale = HEAD_DIM ** -0.5

    def body(q_ref, k_ref, v_ref, c_ref, sm_ref, sp_ref, o_ref, lse_ref, qr, kp, vp):
        kp[pl.ds(0, BLK), :] = jnp.zeros((BLK, HEAD_DIM), BF)
        vp[pl.ds(0, BLK), :] = jnp.zeros((BLK, HEAD_DIM), BF)

        def residue(r):
            res = _res_rows(r, 0, L, d)
            c, sm, sp = c_ref[res, :], sm_ref[res, :], sp_ref[res, :]
            qr[...] = _rope(q_ref[res, :], c, sm, sp).astype(BF)
            kp[pl.ds(BLK, L), :] = _rope(k_ref[res, :], c, sm, sp).astype(BF)
            vp[pl.ds(BLK, L), :] = v_ref[res, :].astype(BF)

            def blk(i, carry):
                r0 = pl.multiple_of(i * BLK, BLK)
                s = _nt(qr[pl.ds(r0, BLK), :], kp[pl.ds(r0, 2 * BLK), :]) * scale
                s = jnp.where(_att_mask(i), s, NEG)
                m = jnp.max(s, axis=-1, keepdims=True)
                p = jnp.exp(s - m)
                l = jnp.sum(p, axis=-1, keepdims=True)
                out = _res_rows(r, i, BLK, d)
                o_ref[out, :] = _nn(p.astype(BF), vp[pl.ds(r0, 2 * BLK), :]) / l
                lse_ref[out, :] = jnp.broadcast_to(m + jnp.log(l), (BLK, HEAD_DIM))
                return carry

            lax.fori_loop(0, nblk, blk, 0, unroll=min(4, nblk))

        _each_residue(d, residue)

    shp = jax.ShapeDtypeStruct((S, GROUP_W), F32)
    return _call(body, grid=(HEADS_PER_GROUP,), name=name, args=[qkv, qkv, qkv, *tabs], comm=comm,
                 in_specs=[sect(Q_OFF), sect(K_OFF), sect(V_OFF), tab, tab, tab], out_specs=[head, head], out_shape=[shp, shp],
                 scratch=[pltpu.VMEM((L, HEAD_DIM), BF), pltpu.VMEM((L + BLK, HEAD_DIM), BF), pltpu.VMEM((L + BLK, HEAD_DIM), BF)])


def _att_combine(os, lses, name):
    S = os[0].shape[0]
    tm = _tile(S, 512)

    def body(o0, o1, o2, l0, l1, l2, oa_ref, lse_ref):
        a, b, c = l0[...], l1[...], l2[...]
        mx = jnp.maximum(jnp.maximum(a, b), c)
        wa, wb, wc = jnp.exp(a - mx), jnp.exp(b - mx), jnp.exp(c - mx)
        den = wa + wb + wc
        oa_ref[...] = ((wa * o0[...] + wb * o1[...] + wc * o2[...]) / den).astype(BF)
        lse_ref[...] = mx + jnp.log(den)

    row = pl.BlockSpec((tm, GROUP_W), lambda i: (i, 0))
    return _call(body, grid=(S // tm,), name=name, args=[*os, *lses], in_specs=[row] * 6, out_specs=[row, row],
                 out_shape=[jax.ShapeDtypeStruct((S, GROUP_W), BF), jax.ShapeDtypeStruct((S, GROUP_W), F32)])[0]


def _att_bwd(qkv, tabs, do, lse, dvec, gi, d, name, comm=()):
    S = qkv.shape[0]
    L = S // d
    sect, tab, head = _att_specs(S, gi)
    stat = _once((S, HEAD_DIM), lambda hh: (0, hh))
    nblk = L // BLK
    scale = HEAD_DIM ** -0.5

    def body(q_ref, k_ref, v_ref, c_ref, sm_ref, sp_ref, do_ref, lse_ref, dv_ref, dq_out, dk_out, dv_out, qr, kp, vp, dkp, dvp):
        kp[pl.ds(0, BLK), :] = jnp.zeros((BLK, HEAD_DIM), BF)
        vp[pl.ds(0, BLK), :] = jnp.zeros((BLK, HEAD_DIM), BF)

        def residue(r):
            res = _res_rows(r, 0, L, d)
            c, sm, sp = c_ref[res, :], sm_ref[res, :], sp_ref[res, :]
            qr[...] = _rope(q_ref[res, :], c, sm, sp).astype(BF)
            kp[pl.ds(BLK, L), :] = _rope(k_ref[res, :], c, sm, sp).astype(BF)
            vp[pl.ds(BLK, L), :] = v_ref[res, :].astype(BF)
            dkp[...] = jnp.zeros_like(dkp)
            dvp[...] = jnp.zeros_like(dvp)

            def blk(i, carry):
                r0 = pl.multiple_of(i * BLK, BLK)
                rows, win, pos = pl.ds(r0, BLK), pl.ds(r0, 2 * BLK), _res_rows(r, i, BLK, d)
                q, kw, vw, dob = qr[rows, :], kp[win, :], vp[win, :], do_ref[pos, :].astype(BF)
                s = jnp.where(_att_mask(i), _nt(q, kw) * scale, NEG)
                p = jnp.exp(s - lse_ref[pos, :][:, :1])
                ds = p * (_nt(dob, vw) - dv_ref[pos, :][:, :1]) * scale
                dsb = ds.astype(BF)
                dq_out[pos, :] = _rope_t(_nn(dsb, kw), c_ref[pos, :], sm_ref[pos, :], sp_ref[pos, :])
                dkp[win, :] += _tn(dsb, q)
                dvp[win, :] += _tn(p.astype(BF), dob)
                return carry

            lax.fori_loop(0, nblk, blk, 0, unroll=min(4, nblk))
            dk_out[res, :] = _rope_t(dkp[pl.ds(BLK, L), :], c, sm, sp)
            dv_out[res, :] = dvp[pl.ds(BLK, L), :]

        _each_residue(d, residue)

    shp = jax.ShapeDtypeStruct((S, GROUP_W), F32)
    return _call(body, grid=(HEADS_PER_GROUP,), name=name, args=[qkv, qkv, qkv, *tabs, do, lse, dvec], comm=comm,
                 in_specs=[sect(Q_OFF), sect(K_OFF), sect(V_OFF), tab, tab, tab, stat, stat, stat],
                 out_specs=[head, head, head], out_shape=[shp, shp, shp],
                 scratch=[pltpu.VMEM((L, HEAD_DIM), BF), pltpu.VMEM((L + BLK, HEAD_DIM), BF), pltpu.VMEM((L + BLK, HEAD_DIM), BF),
                          pltpu.VMEM((L + BLK, HEAD_DIM), F32), pltpu.VMEM((L + BLK, HEAD_DIM), F32)])


def _sg_parts(u_ref, vs_ref, g_ref, b_ref):
    uv = u_ref[...].astype(F32)
    vv = vs_ref[...].astype(F32)
    cv = _gauss_cdf(vv)
    vg = vv * cv
    mu = jnp.mean(vg, axis=-1, keepdims=True)
    vc = vg - mu
    rs = lax.rsqrt(jnp.mean(vc * vc, axis=-1, keepdims=True) + LN_EPS)
    y = vc * rs
    return uv, vv, cv, rs, y, y * g_ref[...] + b_ref[...]


def _sg_wmask():
    t = lax.broadcasted_iota(jnp.int32, (BLK, BLK), 0)
    s = lax.broadcasted_iota(jnp.int32, (BLK, BLK), 1)
    return s <= t


def _sg_fwd(proj, sgw, sgbT, lng, lnb, name):
    S, P = proj.shape

    def body(u_ref, vs_ref, w_ref, bt_ref, g_ref, b_ref, z_ref):
        uv, _, _, _, _, vln = _sg_parts(u_ref, vs_ref, g_ref, b_ref)
        ug = uv * _gauss_cdf(uv)
        vb = vln.astype(BF)
        mask = _sg_wmask()
        bt = bt_ref[...]
        for g in range(SG_GROUPS):
            cols = slice(g * BLK, (g + 1) * BLK)
            w = jnp.where(mask, w_ref[g], 0.0).astype(BF)
            sp = _nn(w, vb[:, cols]) + bt[:, g:g + 1]
            z_ref[:, cols] = (ug[:, cols] * sp).astype(BF)

    tile = lambda off: pl.BlockSpec((BLK, SG_W), lambda i: (i, off // SG_W))
    full = lambda shape: pl.BlockSpec(shape, lambda i: (0,) * len(shape))
    return _call(body, grid=(S // BLK,), name=name, args=[proj, proj, sgw, sgbT, lng, lnb],
                 in_specs=[tile(R_U), tile(R_VS), full((SG_GROUPS, BLK, BLK)), full((BLK, BLK)), full((1, SG_W)), full((1, SG_W))],
                 out_specs=[pl.BlockSpec((BLK, SG_W), lambda i: (i, 0))], out_shape=[jax.ShapeDtypeStruct((S, SG_W), BF)])[0][0]


def _sg_bwd(proj, dz, sgw, sgbT, lng, lnb, name):
    S, P = proj.shape

    def body(u_ref, vs_ref, dz_ref, w_ref, bt_ref, g_ref, b_ref, du_ref, dvs_ref, dw_ref, dbt_ref, dg_ref, db_ref, dvln):
        @pl.when(pl.program_id(0) == 0)
        def _():
            dw_ref[...] = jnp.zeros_like(dw_ref)
            dbt_ref[...] = jnp.zeros_like(dbt_ref)
            dg_ref[...] = jnp.zeros_like(dg_ref)
            db_ref[...] = jnp.zeros_like(db_ref)

        uv, vv, cv, rs, y, vln = _sg_parts(u_ref, vs_ref, g_ref, b_ref)
        cu = _gauss_cdf(uv)
        ug = uv * cu
        dug = _gelu_grad(uv, cu)
        vb = vln.astype(BF)
        dzv = dz_ref[...].astype(F32)
        dsp = dzv * ug
        dspb = dsp.astype(BF)
        mask = _sg_wmask()
        bt = bt_ref[...]
        lane = lax.broadcasted_iota(jnp.int32, (BLK, BLK), 1)
        dbt = jnp.zeros((BLK, BLK), F32)
        for g in range(SG_GROUPS):
            cols = slice(g * BLK, (g + 1) * BLK)
            w = jnp.where(mask, w_ref[g], 0.0).astype(BF)
            sp = _nn(w, vb[:, cols]) + bt[:, g:g + 1]
            du_ref[:, cols] = (dzv[:, cols] * sp * dug[:, cols]).astype(BF)
            dw_ref[g] += jnp.where(mask, _nt(dspb[:, cols], vb[:, cols]), 0.0)
            dbt = dbt + jnp.where(lane == g, jnp.sum(dsp[:, cols], axis=-1, keepdims=True), 0.0)
            dvln[:, cols] = _tn(w, dspb[:, cols])
        dbt_ref[...] += dbt
        dvl = dvln[...]
        dg_ref[...] += jnp.sum(dvl * y, axis=0, keepdims=True)
        db_ref[...] += jnp.sum(dvl, axis=0, keepdims=True)
        dy = dvl * g_ref[...]
        dvg = rs * (dy - jnp.mean(dy, axis=-1, keepdims=True) - y * jnp.mean(dy * y, axis=-1, keepdims=True))
        dvs_ref[...] = (dvg * _gelu_grad(vv, cv)).astype(BF)

    tile = lambda off: pl.BlockSpec((BLK, SG_W), lambda i: (i, off // SG_W))
    full = lambda shape: pl.BlockSpec(shape, lambda i: (0,) * len(shape))
    row = pl.BlockSpec((BLK, SG_W), lambda i: (i, 0))
    return _call(body, grid=(S // BLK,), name=name, args=[proj, proj, dz, sgw, sgbT, lng, lnb],
                 in_specs=[tile(R_U), tile(R_VS), row, full((SG_GROUPS, BLK, BLK)), full((BLK, BLK)), full((1, SG_W)), full((1, SG_W))],
                 out_specs=[row, row, full((SG_GROUPS, BLK, BLK)), full((BLK, BLK)), full((1, SG_W)), full((1, SG_W))],
                 out_shape=[jax.ShapeDtypeStruct((S, SG_W), BF), jax.ShapeDtypeStruct((S, SG_W), BF),
                            jax.ShapeDtypeStruct((SG_GROUPS, BLK, BLK), F32), jax.ShapeDtypeStruct((BLK, BLK), F32),
                            jax.ShapeDtypeStruct((1, SG_W), F32), jax.ShapeDtypeStruct((1, SG_W), F32)],
                 scratch=[pltpu.VMEM((BLK, SG_W), F32)])[0]


def _gate_merge(oatt, z, watt, wsg, proj, name, comm=()):
    S = oatt.shape[0]
    nb, _, Db = watt.shape
    D = nb * Db
    tm = _tile(S, 512)
    half = D // 2
    ga, gs = R_GA // half, (R_GA + D) // half

    def body(oa_ref, z_ref, wa_ref, ws_ref, ga0, ga1, gs0, gs1, ya_ref, ys_ref, mg_ref):
        oa, zv = oa_ref[...], z_ref[...]
        for j in range(nb):
            cols = slice(j * Db, (j + 1) * Db)
            g_a, g_s = (ga0, gs0) if j < nb // 2 else (ga1, gs1)
            gcols = slice((j % (nb // 2)) * Db, (j % (nb // 2) + 1) * Db)
            ya = _nn(oa, wa_ref[j])
            ys = _nn(zv, ws_ref[j])
            ya_ref[:, cols] = ya.astype(BF)
            ys_ref[:, cols] = ys.astype(BF)
            mg_ref[:, cols] = (jax.nn.sigmoid(g_a[:, gcols].astype(F32)) * ya + jax.nn.sigmoid(g_s[:, gcols].astype(F32)) * ys).astype(BF)

    out = pl.BlockSpec((tm, D), lambda i: (i, 0))
    gate = lambda b: pl.BlockSpec((tm, half), lambda i: (i, b))
    shp = jax.ShapeDtypeStruct((S, D), BF)
    return _call(body, grid=(S // tm,), name=name, args=[oatt, z, watt, wsg, proj, proj, proj, proj], comm=comm,
                 in_specs=[pl.BlockSpec((tm, GROUP_W), lambda i: (i, 0)), pl.BlockSpec((tm, SG_W), lambda i: (i, 0)),
                           pl.BlockSpec((nb, GROUP_W, Db), lambda i: (0, 0, 0)), pl.BlockSpec((nb, SG_W, Db), lambda i: (0, 0, 0)),
                           gate(ga), gate(ga + 1), gate(gs), gate(gs + 1)],
                 out_specs=[out, out, out], out_shape=[shp, shp, shp])


def _mix_out(merged, wout, x, gn, name):
    S, D = x.shape
    tm = _tile(S, 512)

    def body(m_ref, w_ref, x_ref, gn_ref, xo_ref, hn_ref):
        xo = x_ref[...] + _nn(m_ref[...], w_ref[...])
        r = lax.rsqrt(jnp.mean(xo * xo, axis=-1, keepdims=True) + NORM_EPS)
        xo_ref[...] = xo
        hn_ref[...] = (xo * r * gn_ref[...]).astype(BF)

    row = pl.BlockSpec((tm, D), lambda i: (i, 0))
    return _call(body, grid=(S // tm,), name=name, args=[merged, wout, x, gn],
                 in_specs=[row, pl.BlockSpec((D, D), lambda i: (0, 0)), row, pl.BlockSpec((1, D), lambda i: (0, 0))],
                 out_specs=[row, row], out_shape=[jax.ShapeDtypeStruct((S, D), F32), jax.ShapeDtypeStruct((S, D), BF)])[0]


def _mix_bwd_gate(dmix, wout, ya, ys, proj, name):
    S, D = dmix.shape
    tm, tn = _tile(S, 512), min(512, D // 2)
    half = D // 2
    ga, gs = R_GA // half, (R_GA + D) // half

    def body(dm_ref, w_ref, ya_ref, ys_ref, ga0, ga1, gs0, gs1, dya_ref, dys_ref, dga_ref, dgs_ref):
        dmv = dm_ref[...]
        for c0 in range(0, D, tn):
            cols = slice(c0, c0 + tn)
            g_a, g_s = (ga0, gs0) if c0 < half else (ga1, gs1)
            gcols = slice(c0 % half, c0 % half + tn)
            dm = _nt(dmv, w_ref[cols, :])
            sa = jax.nn.sigmoid(g_a[:, gcols].astype(F32))
            ss = jax.nn.sigmoid(g_s[:, gcols].astype(F32))
            dya_ref[:, cols] = (dm * sa).astype(BF)
            dys_ref[:, cols] = (dm * ss).astype(BF)
            dga_ref[:, cols] = (dm * ya_ref[:, cols].astype(F32) * sa * (1.0 - sa)).astype(BF)
            dgs_ref[:, cols] = (dm * ys_ref[:, cols].astype(F32) * ss * (1.0 - ss)).astype(BF)

    row = pl.BlockSpec((tm, D), lambda i: (i, 0))
    gate = lambda b: pl.BlockSpec((tm, half), lambda i: (i, b))
    shp = jax.ShapeDtypeStruct((S, D), BF)
    return _call(body, grid=(S // tm,), name=name, args=[dmix, wout, ya, ys, proj, proj, proj, proj],
                 in_specs=[row, pl.BlockSpec((D, D), lambda i: (0, 0)), row, row, gate(ga), gate(ga + 1), gate(gs), gate(gs + 1)],
                 out_specs=[row] * 4, out_shape=[shp] * 4)[0]


def _att_sg_dout(dya, dys, watt, wsg, oatt, name, comm=()):
    S, D = dya.shape
    nb, _, Db = watt.shape
    tm = _tile(S, 512)

    def body(dya_ref, dys_ref, wa_ref, ws_ref, oa_ref, do_ref, dz_ref, dvec_ref):
        def back(dy_ref, w_ref, rows):
            tot = None
            for j in range(nb):
                part = _nt(dy_ref[:, j * Db:(j + 1) * Db], w_ref[j, rows, :])
                tot = part if tot is None else tot + part
            return tot

        dov = back(dya_ref, wa_ref, slice(0, GROUP_W))
        do_ref[...] = dov
        for c0 in range(0, SG_W, GROUP_W):
            dz_ref[:, c0:c0 + GROUP_W] = back(dys_ref, ws_ref, slice(c0, c0 + GROUP_W)).astype(BF)
        prod = dov * oa_ref[...].astype(F32)
        for hh in range(HEADS_PER_GROUP):
            cols = slice(hh * HEAD_DIM, (hh + 1) * HEAD_DIM)
            dvec_ref[:, cols] = jnp.broadcast_to(jnp.sum(prod[:, cols], axis=-1, keepdims=True), (tm, HEAD_DIM))

    row = pl.BlockSpec((tm, D), lambda i: (i, 0))
    att = pl.BlockSpec((tm, GROUP_W), lambda i: (i, 0))
    return _call(body, grid=(S // tm,), name=name, args=[dya, dys, watt, wsg, oatt], comm=comm,
                 in_specs=[row, row, pl.BlockSpec((nb, GROUP_W, Db), lambda i: (0, 0, 0)), pl.BlockSpec((nb, SG_W, Db), lambda i: (0, 0, 0)), att],
                 out_specs=[att, pl.BlockSpec((tm, SG_W), lambda i: (i, 0)), att],
                 out_shape=[jax.ShapeDtypeStruct((S, GROUP_W), F32), jax.ShapeDtypeStruct((S, SG_W), BF), jax.ShapeDtypeStruct((S, GROUP_W), F32)])[0]


def _small_allreduce(pack, name):
    R = pack.shape[0]

    def body(p_ref, o_ref, gath, send, recv):
        x, y, c = _place()
        me = 4 * x + 2 * y + c
        gath[me] = p_ref[...]
        copies = []
        for r in range(1, N_DEV):
            px, py, pc = _flip(x, r & 4), _flip(y, r & 2), _flip(c, r & 1)
            peer = 4 * px + 2 * py + pc
            mk = lambda dst: pltpu.make_async_remote_copy(src_ref=p_ref, dst_ref=dst, send_sem=send.at[r - 1], recv_sem=recv.at[r - 1],
                                                          device_id=(px, py, pc), device_id_type=MESH)
            snd = mk(gath.at[me])
            snd.start()
            copies.append((snd, mk(gath.at[peer])))
        for snd, rcv in copies:
            rcv.wait_recv()
            snd.wait_send()
        acc = gath[0]
        for s in range(1, N_DEV):
            acc = acc + gath[s]
        o_ref[...] = acc

    vm = pl.BlockSpec(memory_space=pltpu.VMEM)
    return pl.pallas_call(
        body, name=name, in_specs=[vm], out_specs=vm, out_shape=jax.ShapeDtypeStruct(pack.shape, F32),
        scratch_shapes=[pltpu.VMEM((N_DEV, R, 128), F32), pltpu.SemaphoreType.DMA((7,)), pltpu.SemaphoreType.DMA((7,))],
        compiler_params=pltpu.CompilerParams(vmem_limit_bytes=VMEM_LIMIT),
    )(pack)


def _row_tile(R, C, elems=262144):
    fits = [t for t in range(16, R + 1, 16) if R % t == 0 and t * C <= elems]
    return max(fits) if fits else R


def _pair_add(parts, other, name):
    _, R, C = parts.shape
    tr = _row_tile(R, C, 1048576)

    def body(c_ref, p_ref, o_ref, s_ref):
        s_ref[0] = (p_ref[0].astype(F32) + o_ref[0].astype(F32)).astype(BF)

    core = lax.axis_index("c").astype(jnp.int32).reshape(1)
    return pl.pallas_call(
        body, name=name,
        grid_spec=pltpu.PrefetchScalarGridSpec(
            num_scalar_prefetch=1, grid=(N_CHIP, R // tr),
            in_specs=[pl.BlockSpec((1, tr, C), lambda q, i, c: (2 * q + c[0], i, 0)), pl.BlockSpec((1, tr, C), lambda q, i, c: (q, i, 0))],
            out_specs=pl.BlockSpec((1, tr, C), lambda q, i, c: (q, i, 0))),
        out_shape=jax.ShapeDtypeStruct((N_CHIP, R, C), BF),
        compiler_params=pltpu.CompilerParams(dimension_semantics=("arbitrary", "arbitrary"), vmem_limit_bytes=VMEM_LIMIT),
    )(core, parts, other)


def _adamw(parts, w, m, v, name):
    ns, R, C = parts.shape
    tr = _row_tile(R, C, 524288)
    c1 = 1.0 - ADAM_B1 ** ADAM_STEP
    c2 = 1.0 - ADAM_B2 ** ADAM_STEP

    def body(p_ref, w_ref, m_ref, v_ref, g_ref, d_ref, nm_ref, nv_ref):
        g = p_ref[0].astype(F32)
        for s in range(1, ns):
            g = g + p_ref[s].astype(F32)
        mn = ADAM_B1 * m_ref[...] + (1.0 - ADAM_B1) * g
        vn = ADAM_B2 * v_ref[...] + (1.0 - ADAM_B2) * (g * g)
        g_ref[...] = g
        nm_ref[...] = mn
        nv_ref[...] = vn
        d_ref[...] = -ADAM_LR * ((mn / c1) / (jnp.sqrt(vn / c2) + ADAM_EPS) + ADAM_WD * w_ref[...])

    row = pl.BlockSpec((tr, C), lambda i: (i, 0))
    shp = jax.ShapeDtypeStruct((R, C), F32)
    return _call(body, grid=(R // tr,), name=name, args=[parts, w, m, v],
                 in_specs=[pl.BlockSpec((ns, tr, C), lambda i: (0, i, 0)), row, row, row], out_specs=[row] * 4, out_shape=[shp] * 4)[0]


def _pad_rows(a, rows):
    return jnp.pad(a, ((0, rows - a.shape[0]), (0, 0)))


def kernel(x, ffn1_norm, ffn1_w_gate, ffn1_w_up, ffn1_w_down, mix_norm, w_in, sg_ln_g, sg_ln_b, sg_w, sg_b, w_att_out, w_sg_out, w_out, ffn2_norm, ffn2_w_gate, ffn2_w_up, ffn2_w_down, final_norm, loss_target, m_ffn1_norm, m_ffn1_w_gate, m_ffn1_w_up, m_ffn1_w_down, m_mix_norm, m_w_in, m_sg_ln_g, m_sg_ln_b, m_sg_w, m_sg_b, m_w_att_out, m_w_sg_out, m_w_out, m_ffn2_norm, m_ffn2_w_gate, m_ffn2_w_up, m_ffn2_w_down, m_final_norm, v_ffn1_norm, v_ffn1_w_gate, v_ffn1_w_up, v_ffn1_w_down, v_mix_norm, v_w_in, v_sg_ln_g, v_sg_ln_b, v_sg_w, v_sg_b, v_w_att_out, v_w_sg_out, v_w_out, v_ffn2_norm, v_ffn2_w_gate, v_ffn2_w_up, v_ffn2_w_down, v_final_norm):
    S, D = x.shape[1], x.shape[2]
    Pb = w_in.shape[2]
    P = N_DEV * Pb
    assert P == GA_OFF + 2 * D and D % (N_DEV * 128) == 0 and S % (BLK * DILATIONS[-1]) == 0
    xs, tgt = x[0], loss_target[0]

    sharded = dict(ffn1_w_gate=ffn1_w_gate, ffn1_w_up=ffn1_w_up, ffn1_w_down=ffn1_w_down, w_in=w_in, w_att_out=w_att_out,
                   w_sg_out=w_sg_out, w_out=w_out, ffn2_w_gate=ffn2_w_gate, ffn2_w_up=ffn2_w_up, ffn2_w_down=ffn2_w_down)
    cols = ("ffn1_w_gate", "ffn1_w_up", "w_in", "ffn2_w_gate", "ffn2_w_up")
    local = lambda n, a: a[0].T if n in cols else a[0]
    back = lambda n, a: a.T[None] if n in cols else a[None]
    wloc = {n: local(n, w) for n, w in sharded.items()}
    sb = {n: w.astype(BF) for n, w in wloc.items()}

    (h1,), ((wg1,),) = _rms_fwd(xs, ffn1_norm, "rms1", comm=[_Gather([sb["ffn1_w_gate"]], 1.0, 1.0)])
    (g1,), ((wu1,),) = _ffn_gate(h1, wg1, "ffn1_gate", comm=[_Gather([sb["ffn1_w_up"]], 0.9, 0.55)])
    (u1, a1), ((wd1,),) = _ffn_up_act(h1, wu1, g1, "ffn1_up_act", comm=[_Gather([sb["ffn1_w_down"]], 0.9, 0.55)])
    (x1, h2), ((winT8,),) = _ffn_down_norm(a1, wd1, xs, mix_norm, "ffn1_down", comm=[_Gather([sb["w_in"]], 1.0, 0.7)])
    winT = winT8.reshape(P, D)
    (qkv, rest), ((wg2, wu2),) = _proj_split(h2, winT, _tile(S, 2048), 512, U_OFF, "proj",
                                           comm=[_Gather([sb["ffn2_w_gate"], sb["ffn2_w_up"]], 0.85, 0.5)])
    tabs = _rope_tables(S)
    rides = [[_Gather([sb["w_att_out"], sb["w_sg_out"]], 0.9, 0.5)], [_Gather([sb["w_out"]], 0.85, 0.45)], []]
    os, lses, late = [], [], []
    for gi, d in enumerate(DILATIONS):
        (o, l), got_here = _att_fwd(qkv, tabs, gi, d, f"att_fwd{gi}", comm=rides[gi])
        late += [w for g in got_here for w in g]
        os.append(o)
        lses.append(l)
    watt, wsg, wout8 = late
    wout = wout8.reshape(D, D)
    oatt, lse = _att_combine(os, lses, "att_combine")
    sgw = sg_w[0]
    sgbT = jnp.pad(sg_b[0].T, ((0, 0), (0, BLK - SG_GROUPS)))
    z = _sg_fwd(rest, sgw, sgbT, sg_ln_g, sg_ln_b, "sg_fwd")
    (ya, ys, merged), _ = _gate_merge(oatt, z, watt, wsg, rest, "gate_merge")
    x2, h3 = _mix_out(merged, wout, x1, ffn2_norm, "mix_out")
    (g3, u3, a3), ((wd2,),) = _ffn_up(h3, wg2, wu2, "ffn2_up", comm=[_Gather([sb["ffn2_w_down"]], 0.6, 0.35)])
    dx3, dyb3, d_final, loss_part = _ffn_down_loss(a3, wd2, x2, final_norm.reshape(1, D), tgt, "ffn2_down_loss")

    Fb = wg2.shape[1]
    Db = watt.shape[2]
    p_pad = -(-P // PROJ_TK) * PROJ_TK
    win_tail = _pad_rows(winT[p_pad - PROJ_TK:], PROJ_TK)
    (dg3, du3), _ = _ffn_bwd_act(dyb3, wd2, g3, u3, "ffn2_bwd_act")
    (dwd2,), _ = _ffn_dwd(a3, dyb3, "ffn2_dwd")
    (dwg2, dwu2), _ = _ffn_dwgu(h3, dg3, du3, "ffn2_dwgu")
    ffn2_parts = [dwd2, dwg2, dwu2]
    (dx2, dmixb, d_ffn2n), (ffn2_other,) = _ffn_dh(dg3, du3, wg2, wu2, x2, ffn2_norm, dx3, "ffn2_dh", comm=[_Swap(ffn2_parts)])
    ffn2_sums = [_pair_add(p, o, f"pair_ffn2_{i}") for i, (p, o) in enumerate(zip(ffn2_parts, ffn2_other))]

    dya, dys, dga, dgs = _mix_bwd_gate(dmixb, wout, ya, ys, rest, "mix_bwd_gate")
    (dwout,), _ = _mm_tn(merged, dmixb, _tile(D, 1024), _tile(D, 1024), _tile(S, 1024), False, "dw_out")
    do, dz, dvec = _att_sg_dout(dya, dys, watt, wsg, oatt, "att_sg_dout")
    (dwatt,), _ = _mm_tn(oatt, dya, GROUP_W, 2 * Db, _tile(S, 1024), Db, "dw_att")
    (dwsg,), _ = _mm_tn(z, dys, SG_W, 2 * Db, _tile(S, 1024), Db, "dw_sg")
    mix_parts = [dwout.reshape(N_DEV, D // N_DEV, D), dwatt, dwsg]
    du, dvs, d_sgw, d_sgbT, d_lng, d_lnb = _sg_bwd(rest, dz, sgw, sgbT, sg_ln_g, sg_ln_b, "sg_bwd")
    dqs, dks, dvs_att, ffn2_got = [], [], [], []
    for gi, d in enumerate(DILATIONS):
        ride = [_Ici([ffn2_sums[0]])] if gi == 2 else []
        (dq, dk, dv), got_here = _att_bwd(qkv, tabs, do, lse, dvec, gi, d, f"att_bwd{gi}", comm=ride)
        ffn2_got += [g[0] for g in got_here]
        dqs.append(dq)
        dks.append(dk)
        dvs_att.append(dv)
    dproj = jnp.concatenate([t.astype(BF) for t in dqs + dks + dvs_att] + [du, dvs, dga, dgs, jnp.zeros((S, p_pad - P), BF)], axis=1)
    (dx1, dyb1, d_mixn), (ffn2_rest, mix_other) = _dh_rms_bwd([(dproj, winT)], False, PROJ_TK, x1, mix_norm, dx2, 0.5, "proj_dh",
                                                            comm=[_Ici(ffn2_sums[1:]), _Swap(mix_parts)], tail=win_tail)
    ffn2_got += ffn2_rest
    mix_sums = [_pair_add(p, o, f"pair_mix_{i}") for i, (p, o) in enumerate(zip(mix_parts, mix_other))]
    (dwd1,), (mix_got,) = _ffn_dwd(a1, dyb1, "ffn1_dwd", comm=[_Ici(mix_sums)])
    rows = lambda a: a.reshape(-1, 128)
    pad8 = lambda a: _pad_rows(a, -(-a.shape[0] // 8) * 8)
    small = [("sg_w", rows(d_sgw), sg_w, m_sg_w, v_sg_w), ("mix_norm", rows(d_mixn), mix_norm, m_mix_norm, v_mix_norm),
             ("ffn2_norm", rows(d_ffn2n), ffn2_norm, m_ffn2_norm, v_ffn2_norm), ("final_norm", rows(d_final), final_norm, m_final_norm, v_final_norm),
             ("sg_ln_g", rows(d_lng), sg_ln_g, m_sg_ln_g, v_sg_ln_g), ("sg_ln_b", rows(d_lnb), sg_ln_b, m_sg_ln_b, v_sg_ln_b),
             ("sg_b", d_sgbT[:, :SG_GROUPS].T, sg_b, m_sg_b, v_sg_b)]
    gpack = jnp.concatenate([pad8(g) for _, g, _, _, _ in small] + [pad8(loss_part)], axis=0)
    (dwin,), ((wd1_other,), (gpacks,)) = _mm_tn(dproj, h2, 512, D, S, False, "dw_in", mrows=P,
                                              comm=[_Swap([dwd1]), _Spread([gpack])])
    dwin = dwin.reshape(N_DEV, Pb, D)
    wd1_sum = _pair_add(dwd1, wd1_other, "pair_wd1")
    (dg1, du1), ((wd1_got,), (win_other,)) = _ffn_bwd_act(dyb1, wd1, g1, u1, "ffn1_bwd_act", comm=[_Ici([wd1_sum]), _Swap([dwin])])
    win_sum = _pair_add(dwin, win_other, "pair_win")
    (dwg1, dwu1), ((win_got,),) = _ffn_dwgu(h1, dg1, du1, "ffn1_dwgu", comm=[_Ici([win_sum])])
    gu_parts = [dwg1, dwu1]
    gu_other = _comm_only(_Swap(gu_parts), "swap_gu1")
    gu_sums = [_pair_add(p, o, f"pair_gu1_{i}") for i, (p, o) in enumerate(zip(gu_parts, gu_other))]
    (dx0, _, d_ffn1n), (gu_got,) = _ffn_dh(dg1, du1, wg1, wu1, xs, ffn1_norm, dx1, "ffn1_dh", comm=[_Ici(gu_sums)])

    got = dict(ffn2_w_down=ffn2_got[0], ffn2_w_gate=ffn2_got[1], ffn2_w_up=ffn2_got[2], w_out=mix_got[0], w_att_out=mix_got[1],
               w_sg_out=mix_got[2], w_in=win_got, ffn1_w_gate=gu_got[0], ffn1_w_up=gu_got[1], ffn1_w_down=wd1_got)
    moments = dict(ffn1_w_gate=(m_ffn1_w_gate, v_ffn1_w_gate), ffn1_w_up=(m_ffn1_w_up, v_ffn1_w_up),
                   ffn1_w_down=(m_ffn1_w_down, v_ffn1_w_down), w_in=(m_w_in, v_w_in), w_att_out=(m_w_att_out, v_w_att_out),
                   w_sg_out=(m_w_sg_out, v_w_sg_out), w_out=(m_w_out, v_w_out), ffn2_w_gate=(m_ffn2_w_gate, v_ffn2_w_gate),
                   ffn2_w_up=(m_ffn2_w_up, v_ffn2_w_up), ffn2_w_down=(m_ffn2_w_down, v_ffn2_w_down))
    res = {}
    for n in sharded:
        mm, vv = moments[n]
        outs = _adamw(got[n], wloc[n], local(n, mm), local(n, vv), "adamw_" + n)
        res[n] = [back(n, o) for o in outs]

    zero8 = jnp.zeros((8, 128), F32)
    wpack = jnp.concatenate([pad8(rows(w)) for _, _, w, _, _ in small] + [zero8], axis=0)
    mpack = jnp.concatenate([pad8(rows(m)) for _, _, _, m, _ in small] + [zero8], axis=0)
    vpack = jnp.concatenate([pad8(rows(v)) for _, _, _, _, v in small] + [zero8], axis=0)
    packs = _adamw(gpacks, wpack, mpack, vpack, "adamw_small")
    off = 0
    for n, g, w, _, _ in small:
        r = g.shape[0]
        res[n] = [p[off:off + r].reshape(w.shape) for p in packs]
        off += -(-r // 8) * 8
    loss = packs[0][off, 0]
    g_first = _small_allreduce(rows(d_ffn1n), "allreduce_ffn1_norm")
    res["ffn1_norm"] = [p.reshape(ffn1_norm.shape) for p in
                        _adamw(g_first[None], rows(ffn1_norm), rows(m_ffn1_norm), rows(v_ffn1_norm), "adamw_ffn1_norm")]

    order = ["ffn1_norm", "ffn1_w_gate", "ffn1_w_up", "ffn1_w_down", "mix_norm", "w_in", "sg_ln_g", "sg_ln_b", "sg_w", "sg_b",
             "w_att_out", "w_sg_out", "w_out", "ffn2_norm", "ffn2_w_gate", "ffn2_w_up", "ffn2_w_down", "final_norm"]
    return (loss, dx0[None], *[res[n][0] for n in order], *[res[n][1] for n in order], *[res[n][2] for n in order],
            *[res[n][3] for n in order])
```

```python
import math

import jax
import jax.numpy as jnp
from jax import lax
from jax.experimental import pallas as pl
from jax.experimental.pallas import tpu as pltpu

BF = jnp.bfloat16
F32 = jnp.float32
MESH = pl.DeviceIdType.MESH
N_DEV = 8
N_CHIP = 4

HEAD_DIM = 128
HEADS_PER_GROUP = 4
GROUP_W = HEADS_PER_GROUP * HEAD_DIM
DILATIONS = (1, 4, 16)
ATT_W = len(DILATIONS) * GROUP_W
SG_W = 1536
SG_GROUPS = 12
BLK = 128
ROPE_DIM = 32
ROPE_THETA = 500000.0
NORM_EPS = 1e-6
LN_EPS = 1e-5
Q_OFF, K_OFF, V_OFF, U_OFF, VS_OFF, GA_OFF = 0, ATT_W, 2 * ATT_W, 3 * ATT_W, 3 * ATT_W + SG_W, 3 * ATT_W + 2 * SG_W

ADAM_LR, ADAM_B1, ADAM_B2, ADAM_EPS, ADAM_WD, ADAM_STEP = 0.001, 0.9, 0.999, 1e-08, 0.01, 10

VMEM_LIMIT = 56 * 1024 * 1024
NEG = -1e30
ANY = pl.BlockSpec(memory_space=pl.ANY)
EPI_ROWS = 128
ACC_COLS = 512
FFN_PAIR = 2
FFN_ROWS = 1024
PROJ_TK = 1536
R_U, R_VS, R_GA = 0, SG_W, 2 * SG_W


def _once(shape, index_map):
    return pl.BlockSpec(shape, index_map, pipeline_mode=pl.Buffered(1))


def _tile(n, pref):
    t = min(n, pref)
    while n % t:
        t //= 2
    return t


def _nt(a, b):
    return lax.dot_general(a, b, (((1,), (1,)), ((), ())), preferred_element_type=F32)


def _tn(a, b):
    return lax.dot_general(a, b, (((0,), (0,)), ((), ())), preferred_element_type=F32)


def _nn(a, b):
    return jnp.dot(a, b, preferred_element_type=F32)


def _acc_dots(acc_ref, terms, transposed_rhs=False):
    n = acc_ref.shape[1]
    width = min(n, ACC_COLS)
    for c0 in range(0, n, width):
        cols = slice(c0, c0 + width)
        tot = None
        for lhs, rhs in terms:
            part = _nt(lhs, rhs(cols)) if transposed_rhs else _nn(lhs, rhs(cols))
            tot = part if tot is None else tot + part
        acc_ref[:, cols] += tot


def _gauss_cdf(x):
    return 0.5 * (1.0 + lax.erf(x * (2.0 ** -0.5)))


def _gelu_grad(x, cdf):
    return cdf + x * jnp.exp(-0.5 * x * x) * (1.0 / math.sqrt(2.0 * math.pi))


def _place():
    x, y, c = lax.axis_index("x"), lax.axis_index("y"), lax.axis_index("c")
    return x, y, c


def _flip(v, bit):
    return 1 - v if bit else v


class _Gather:
    def __init__(self, shards, mid_frac=1.0, relay_frac=0.5):
        self.arrays = list(shards)
        self.relay_frac = relay_frac
        self.mid_frac = mid_frac
        nw = len(shards)
        self.out_shape = [jax.ShapeDtypeStruct((N_DEV,) + s.shape, s.dtype) for s in shards]
        self.scratch = [pltpu.SemaphoreType.DMA((nw, 7)), pltpu.SemaphoreType.DMA((nw, 7)), pltpu.SemaphoreType.DMA((nw,))]

    def _parts(self, ins, outs, sems):
        x, y, c = _place()
        send, recv, loc = sems
        south = c == 0
        near = (jnp.where(south, x, 1 - x), jnp.where(south, 1 - y, y), c)
        far = (jnp.where(south, 1 - x, x), jnp.where(south, y, 1 - y), c)
        diag = (1 - x, 1 - y, c)

        def copy(k, s, block, to, src=None):
            dst = outs[k].at[4 * block[0] + 2 * block[1] + block[2]]
            return pltpu.make_async_remote_copy(src_ref=dst if src is None else src, dst_ref=dst, send_sem=send.at[k, s],
                                                recv_sem=recv.at[k, s], device_id=to, device_id_type=MESH)

        def first(k):
            me = (x, y, c)
            return [copy(k, 0, me, (x, y, 1 - c), src=ins[k]), copy(k, 1, me, (1 - x, y, c), src=ins[k]),
                    copy(k, 2, me, (x, 1 - y, c), src=ins[k])]

        def local(k):
            return pltpu.make_async_copy(ins[k], outs[k].at[4 * x + 2 * y + c], loc.at[k])

        return x, y, c, near, far, diag, copy, first, local

    def start(self, ins, outs, sems):
        *_, first, local = self._parts(ins, outs, sems)
        for k in range(len(ins)):
            local(k).start()
            for cp in first(k):
                cp.start()

    def relay(self, ins, outs, sems):
        x, y, c, near, far, _, copy, _, _ = self._parts(ins, outs, sems)
        for k in range(len(ins)):
            copy(k, 2 - c, near, (x, y, c)).wait_recv()
            copy(k, 3, near, far).start()
            copy(k, 5 - c, near, (x, y, 1 - c)).start()

    def mid(self, ins, outs, sems):
        x, y, c, _, far, diag, copy, _, _ = self._parts(ins, outs, sems)
        for k in range(len(ins)):
            copy(k, 1 + c, far, (x, y, c)).wait_recv()
            copy(k, 4 + c, far, (x, y, 1 - c)).start()
            copy(k, 3, diag, (x, y, c)).wait_recv()
            copy(k, 6, diag, (x, y, 1 - c)).start()

    def finish(self, ins, outs, sems):
        x, y, c, near, _, _, copy, first, local = self._parts(ins, outs, sems)
        sib = (x, y, 1 - c)
        for k in range(len(ins)):
            copy(k, 0, sib, (x, y, c)).wait_recv()
            copy(k, 4, (1 - x, y, 1 - c), (x, y, c)).wait_recv()
            copy(k, 5, (x, 1 - y, 1 - c), (x, y, c)).wait_recv()
            copy(k, 6, (1 - x, 1 - y, 1 - c), (x, y, c)).wait_recv()
        for k in range(len(ins)):
            for cp in first(k):
                cp.wait_send()
            for s in (3, 4, 5, 6):
                copy(k, s, near, sib).wait_send()
            local(k).wait()


class _Swap:
    def __init__(self, parts):
        self.arrays = list(parts)
        nw = len(parts)
        self.out_shape = [jax.ShapeDtypeStruct((N_CHIP,) + p.shape[1:], p.dtype) for p in parts]
        self.scratch = [pltpu.SemaphoreType.DMA((nw, N_CHIP)), pltpu.SemaphoreType.DMA((nw, N_CHIP))]

    def _copy(self, ins, outs, sems, k, q):
        x, y, c = _place()
        return pltpu.make_async_remote_copy(src_ref=ins[k].at[2 * q + 1 - c], dst_ref=outs[k].at[q], send_sem=sems[0].at[k, q],
                                            recv_sem=sems[1].at[k, q], device_id=(x, y, 1 - c), device_id_type=MESH)

    mid_frac = None

    def start(self, ins, outs, sems):
        for k in range(len(ins)):
            for q in range(N_CHIP):
                self._copy(ins, outs, sems, k, q).start()

    def finish(self, ins, outs, sems):
        for k in range(len(ins)):
            for q in range(N_CHIP):
                self._copy(ins, outs, sems, k, q).wait()


class _Ici:
    mid_frac = None

    def __init__(self, sums):
        self.arrays = list(sums)
        nw = len(sums)
        self.out_shape = [jax.ShapeDtypeStruct(s.shape, s.dtype) for s in sums]
        self.scratch = [pltpu.SemaphoreType.DMA((nw, 3)), pltpu.SemaphoreType.DMA((nw, 3)), pltpu.SemaphoreType.DMA((nw,))]

    def _copies(self, ins, outs, sems, k):
        x, y, c = _place()
        myq = 2 * x + y
        out = []
        for r in range(1, N_CHIP):
            px, py = _flip(x, r & 2), _flip(y, r & 1)
            pq = 2 * px + py
            mk = lambda dst: pltpu.make_async_remote_copy(src_ref=ins[k].at[pq], dst_ref=dst, send_sem=sems[0].at[k, r - 1],
                                                          recv_sem=sems[1].at[k, r - 1], device_id=(px, py, c), device_id_type=MESH)
            out.append((mk(outs[k].at[myq]), mk(outs[k].at[pq])))
        return out, pltpu.make_async_copy(ins[k].at[myq], outs[k].at[myq], sems[2].at[k])

    def start(self, ins, outs, sems):
        for k in range(len(ins)):
            remote, local = self._copies(ins, outs, sems, k)
            local.start()
            for snd, _ in remote:
                snd.start()

    def finish(self, ins, outs, sems):
        for k in range(len(ins)):
            remote, local = self._copies(ins, outs, sems, k)
            for snd, rcv in remote:
                rcv.wait_recv()
                snd.wait_send()
            local.wait()


class _Spread:
    mid_frac = None

    def __init__(self, arrays):
        self.arrays = list(arrays)
        nw = len(arrays)
        self.out_shape = [jax.ShapeDtypeStruct((N_DEV,) + a.shape, a.dtype) for a in arrays]
        self.scratch = [pltpu.SemaphoreType.DMA((nw, 7)), pltpu.SemaphoreType.DMA((nw, 7)), pltpu.SemaphoreType.DMA((nw,))]

    def _copies(self, ins, outs, sems, k):
        x, y, c = _place()
        me = 4 * x + 2 * y + c
        out = []
        for r in range(1, N_DEV):
            px, py, pc = _flip(x, r & 4), _flip(y, r & 2), _flip(c, r & 1)
            peer = 4 * px + 2 * py + pc
            mk = lambda dst: pltpu.make_async_remote_copy(src_ref=ins[k], dst_ref=dst, send_sem=sems[0].at[k, r - 1],
                                                          recv_sem=sems[1].at[k, r - 1], device_id=(px, py, pc), device_id_type=MESH)
            out.append((mk(outs[k].at[me]), mk(outs[k].at[peer])))
        return out, pltpu.make_async_copy(ins[k], outs[k].at[me], sems[2].at[k])

    def start(self, ins, outs, sems):
        for k in range(len(ins)):
            remote, local = self._copies(ins, outs, sems, k)
            local.start()
            for snd, _ in remote:
                snd.start()

    def finish(self, ins, outs, sems):
        for k in range(len(ins)):
            remote, local = self._copies(ins, outs, sems, k)
            for snd, rcv in remote:
                rcv.wait_recv()
                snd.wait_send()
            local.wait()


def _call(body, *, grid, in_specs, out_specs, out_shape, name, args, scratch=(), comm=()):
    comm = list(comm)
    n_in, n_out, n_scr = len(in_specs), len(out_specs), len(scratch)
    total = math.prod(grid) if grid else 1

    def wrapped(*refs):
        p = n_in
        cin = []
        for cm in comm:
            cin.append(refs[p:p + len(cm.arrays)])
            p += len(cm.arrays)
        own_out = refs[p:p + n_out]
        p += n_out
        cout = []
        for cm in comm:
            cout.append(refs[p:p + len(cm.arrays)])
            p += len(cm.arrays)
        own_scr = refs[p:p + n_scr]
        p += n_scr
        csem = []
        for cm in comm:
            csem.append(refs[p:p + len(cm.scratch)])
            p += len(cm.scratch)
        step = 0
        for axis, g in enumerate(grid):
            step = step * g + pl.program_id(axis)

        def at(when, what):
            if total == 1:
                what()
            else:
                pl.when(step == when)(what)

        def starts():
            for cm, i, o, s in zip(comm, cin, cout, csem):
                cm.start(i, o, s)

        def finishes():
            for cm, i, o, s in zip(comm, cin, cout, csem):
                cm.finish(i, o, s)

        if comm:
            at(0, starts)
        if body is not None:
            body(*refs[:n_in], *own_out, *own_scr)
        for cm, i, o, s in zip(comm, cin, cout, csem):
            if cm.mid_frac is not None:
                at(min(total - 1, int(total * cm.relay_frac)), lambda cm=cm, i=i, o=o, s=s: cm.relay(i, o, s))
                at(min(total - 1, int(total * cm.mid_frac)), lambda cm=cm, i=i, o=o, s=s: cm.mid(i, o, s))
        if comm:
            at(total - 1, finishes)

    kw = dict(grid=tuple(grid)) if grid else {}
    outs = pl.pallas_call(
        wrapped, name=name, **kw,
        in_specs=list(in_specs) + [ANY for cm in comm for _ in cm.arrays],
        out_specs=list(out_specs) + [ANY for cm in comm for _ in cm.arrays],
        out_shape=list(out_shape) + [s for cm in comm for s in cm.out_shape],
        scratch_shapes=list(scratch) + [s for cm in comm for s in cm.scratch],
        compiler_params=pltpu.CompilerParams(dimension_semantics=("arbitrary",) * len(grid), vmem_limit_bytes=VMEM_LIMIT),
    )(*args, *[a for cm in comm for a in cm.arrays])
    own, p, per = list(outs[:n_out]), n_out, []
    for cm in comm:
        per.append(list(outs[p:p + len(cm.arrays)]))
        p += len(cm.arrays)
    return own, per


def _comm_only(cm, name):
    return _call(None, grid=(), in_specs=[], out_specs=[], out_shape=[], name=name, args=[], comm=[cm])[1][0]


def _rms_fwd(x, g, name, comm=()):
    S, D = x.shape
    tm = _tile(S, 512)

    def body(x_ref, g_ref, o_ref):
        xv = x_ref[...]
        r = lax.rsqrt(jnp.mean(xv * xv, axis=-1, keepdims=True) + NORM_EPS)
        o_ref[...] = (xv * r * g_ref[...]).astype(BF)

    return _call(body, grid=(S // tm,), name=name, args=[x, g], comm=comm,
                 in_specs=[pl.BlockSpec((tm, D), lambda i: (i, 0)), pl.BlockSpec((1, D), lambda i: (0, 0))],
                 out_specs=[pl.BlockSpec((tm, D), lambda i: (i, 0))], out_shape=[jax.ShapeDtypeStruct((S, D), BF)])


def _ffn_up(h, wg, wu, name, comm=()):
    S, D = h.shape
    nb, Fb, _ = wg.shape
    tm = _tile(S, FFN_ROWS)

    def body(h_ref, wg_ref, wu_ref, g_ref, u_ref, a_ref):
        hv = h_ref[...]
        g = _nt(hv, wg_ref[0])
        u = _nt(hv, wu_ref[0])
        g_ref[0] = g.astype(BF)
        u_ref[0] = u.astype(BF)
        a_ref[0] = (g * jax.nn.sigmoid(g) * u).astype(BF)

    act = pl.BlockSpec((1, tm, Fb), lambda j, i: (j, i, 0))
    w = pl.BlockSpec((1, Fb, D), lambda j, i: (j, 0, 0))
    shp = jax.ShapeDtypeStruct((nb, S, Fb), BF)
    return _call(body, grid=(nb, S // tm), name=name, args=[h, wg, wu], comm=comm,
                 in_specs=[pl.BlockSpec((tm, D), lambda j, i: (i, 0)), w, w], out_specs=[act, act, act], out_shape=[shp, shp, shp])


def _ffn_gate(h, wg, name, comm=()):
    S, D = h.shape
    nb, Fb, _ = wg.shape
    tm = _tile(S, FFN_ROWS)

    def body(h_ref, wg_ref, g_ref):
        g_ref[0] = _nt(h_ref[...], wg_ref[0]).astype(BF)

    act = pl.BlockSpec((1, tm, Fb), lambda j, i: (j, i, 0))
    return _call(body, grid=(nb, S // tm), name=name, args=[h, wg], comm=comm,
                 in_specs=[pl.BlockSpec((tm, D), lambda j, i: (i, 0)), pl.BlockSpec((1, Fb, D), lambda j, i: (j, 0, 0))],
                 out_specs=[act], out_shape=[jax.ShapeDtypeStruct((nb, S, Fb), BF)])


def _ffn_up_act(h, wu, g, name, comm=()):
    S, D = h.shape
    nb, Fb, _ = wu.shape
    tm = _tile(S, FFN_ROWS)

    def body(h_ref, wu_ref, g_ref, u_ref, a_ref):
        u = _nt(h_ref[...], wu_ref[0])
        gv = g_ref[0].astype(F32)
        u_ref[0] = u.astype(BF)
        a_ref[0] = (gv * jax.nn.sigmoid(gv) * u).astype(BF)

    act = pl.BlockSpec((1, tm, Fb), lambda j, i: (j, i, 0))
    shp = jax.ShapeDtypeStruct((nb, S, Fb), BF)
    return _call(body, grid=(nb, S // tm), name=name, args=[h, wu, g], comm=comm,
                 in_specs=[pl.BlockSpec((tm, D), lambda j, i: (i, 0)), pl.BlockSpec((1, Fb, D), lambda j, i: (j, 0, 0)), act],
                 out_specs=[act, act], out_shape=[shp, shp])


def _ffn_down_norm(a, wd, x, gn, name, comm=()):
    nb, S, Fb = a.shape
    D = wd.shape[2]
    tm = _tile(S, 512)

    nj = nb // FFN_PAIR

    def body(a_ref, wd_ref, x_ref, gn_ref, xo_ref, hn_ref, acc_ref):
        j = pl.program_id(1)

        @pl.when(j == 0)
        def _():
            acc_ref[...] = jnp.zeros_like(acc_ref)

        _acc_dots(acc_ref, [(a_ref[b], lambda cols, b=b: wd_ref[b, :, cols]) for b in range(FFN_PAIR)])

        @pl.when(j == nj - 1)
        def _():
            def chunk(t, carry):
                rows = pl.ds(pl.multiple_of(t * EPI_ROWS, EPI_ROWS), EPI_ROWS)
                xo = x_ref[rows, :] + 0.5 * acc_ref[rows, :]
                r = lax.rsqrt(jnp.mean(xo * xo, axis=-1, keepdims=True) + NORM_EPS)
                xo_ref[rows, :] = xo
                hn_ref[rows, :] = (xo * r * gn_ref[...]).astype(BF)
                return carry

            lax.fori_loop(0, tm // EPI_ROWS, chunk, 0)

    row = pl.BlockSpec((tm, D), lambda i, j: (i, 0))
    return _call(body, grid=(S // tm, nj), name=name, args=[a, wd, x, gn], comm=comm,
                 in_specs=[pl.BlockSpec((FFN_PAIR, tm, Fb), lambda i, j: (j, i, 0)), pl.BlockSpec((FFN_PAIR, Fb, D), lambda i, j: (j, 0, 0)),
                           row, pl.BlockSpec((1, D), lambda i, j: (0, 0))],
                 out_specs=[row, row], out_shape=[jax.ShapeDtypeStruct((S, D), F32), jax.ShapeDtypeStruct((S, D), BF)],
                 scratch=[pltpu.VMEM((tm, D), F32)])


def _ffn_down_loss(a, wd, x, gf, tgt, name):
    nb, S, Fb = a.shape
    D = wd.shape[2]
    tm = _tile(S, 512)

    nj = nb // FFN_PAIR

    def body(a_ref, wd_ref, x_ref, gf_ref, t_ref, dx_ref, dxb_ref, dgf_ref, loss_ref, acc_ref):
        i, j = pl.program_id(0), pl.program_id(1)

        @pl.when(j == 0)
        def _():
            acc_ref[...] = jnp.zeros_like(acc_ref)

        _acc_dots(acc_ref, [(a_ref[b], lambda cols, b=b: wd_ref[b, :, cols]) for b in range(FFN_PAIR)])

        @pl.when((j == nj - 1) & (i == 0))
        def _():
            dgf_ref[...] = jnp.zeros_like(dgf_ref)
            loss_ref[...] = jnp.zeros_like(loss_ref)

        @pl.when(j == nj - 1)
        def _():
            def chunk(t, carry):
                rows = pl.ds(pl.multiple_of(t * EPI_ROWS, EPI_ROWS), EPI_ROWS)
                xo = x_ref[rows, :] + 0.5 * acc_ref[rows, :]
                r = lax.rsqrt(jnp.mean(xo * xo, axis=-1, keepdims=True) + NORM_EPS)
                xh = xo * r
                gf = gf_ref[...]
                e = xh * gf - t_ref[rows, :]
                loss_ref[...] += jnp.sum(jnp.mean(e * e, axis=-1, keepdims=True), axis=0, keepdims=True) * 0.5
                dy = e * (1.0 / D)
                dgf_ref[...] += jnp.sum(dy * xh, axis=0, keepdims=True)
                dxh = dy * gf
                dx = r * (dxh - xh * jnp.mean(dxh * xh, axis=-1, keepdims=True))
                dx_ref[rows, :] = dx
                dxb_ref[rows, :] = (0.5 * dx).astype(BF)
                return carry

            lax.fori_loop(0, tm // EPI_ROWS, chunk, 0)

    row = pl.BlockSpec((tm, D), lambda i, j: (i, 0))
    once = row
    vec = pl.BlockSpec((1, D), lambda i, j: (0, 0))
    return _call(body, grid=(S // tm, nj), name=name, args=[a, wd, x, gf, tgt],
                 in_specs=[pl.BlockSpec((FFN_PAIR, tm, Fb), lambda i, j: (j, i, 0)), pl.BlockSpec((FFN_PAIR, Fb, D), lambda i, j: (j, 0, 0)),
                           once, vec, once],
                 out_specs=[row, row, vec, pl.BlockSpec((1, 128), lambda i, j: (0, 0))],
                 out_shape=[jax.ShapeDtypeStruct((S, D), F32), jax.ShapeDtypeStruct((S, D), BF), jax.ShapeDtypeStruct((1, D), F32),
                            jax.ShapeDtypeStruct((1, 128), F32)],
                 scratch=[pltpu.VMEM((tm, D), F32)])[0]


def _ffn_bwd_act(dyb, wd, g, u, name, comm=()):
    S, D = dyb.shape
    nb, Fb, _ = wd.shape
    tm = _tile(S, FFN_ROWS)

    def body(dy_ref, wd_ref, g_ref, u_ref, dg_ref, du_ref):
        rows = pl.ds(pl.multiple_of(pl.program_id(1) * tm, tm), tm)
        da = _nt(dy_ref[rows, :], wd_ref[0])
        gv = g_ref[0].astype(F32)
        uv = u_ref[0].astype(F32)
        sg = jax.nn.sigmoid(gv)
        du_ref[0] = (da * gv * sg).astype(BF)
        dg_ref[0] = (da * uv * sg * (1.0 + gv * (1.0 - sg))).astype(BF)

    act = pl.BlockSpec((1, tm, Fb), lambda j, i: (j, i, 0))
    shp = jax.ShapeDtypeStruct((nb, S, Fb), BF)
    return _call(body, grid=(nb, S // tm), name=name, args=[dyb, wd, g, u], comm=comm,
                 in_specs=[pl.BlockSpec((S, D), lambda j, i: (0, 0)), pl.BlockSpec((1, Fb, D), lambda j, i: (j, 0, 0)), act, act],
                 out_specs=[act, act], out_shape=[shp, shp])


def _ffn_dwd(a, dyb, name, comm=()):
    nb, S, Fb = a.shape
    D = dyb.shape[1]
    ts = S
    ns = S // ts

    def body(a_ref, dy_ref, o_ref, acc_ref):
        s = pl.program_id(1)

        @pl.when(s == 0)
        def _():
            acc_ref[...] = jnp.zeros_like(acc_ref)

        acc_ref[...] += _tn(a_ref[0], dy_ref[...])

        @pl.when(s == ns - 1)
        def _():
            o_ref[0] = acc_ref[...].astype(BF)

    return _call(body, grid=(nb, ns), name=name, args=[a, dyb], comm=comm,
                 in_specs=[pl.BlockSpec((1, ts, Fb), lambda j, s: (j, s, 0)), pl.BlockSpec((ts, D), lambda j, s: (s, 0))],
                 out_specs=[pl.BlockSpec((1, Fb, D), lambda j, s: (j, 0, 0))], out_shape=[jax.ShapeDtypeStruct((nb, Fb, D), BF)],
                 scratch=[pltpu.VMEM((Fb, D), F32)])


def _ffn_dwgu(h, dg, du, name, comm=()):
    S, D = h.shape
    nb, _, Fb = dg.shape
    ts = _tile(S, FFN_ROWS)
    ns = S // ts

    def body(h_ref, dg_ref, du_ref, og_ref, ou_ref, accg_ref, accu_ref):
        s = pl.program_id(1)

        @pl.when(s == 0)
        def _():
            accg_ref[...] = jnp.zeros_like(accg_ref)
            accu_ref[...] = jnp.zeros_like(accu_ref)

        hv = h_ref[...]
        accg_ref[...] += _tn(dg_ref[0], hv)
        accu_ref[...] += _tn(du_ref[0], hv)

        @pl.when(s == ns - 1)
        def _():
            og_ref[0] = accg_ref[...].astype(BF)
            ou_ref[0] = accu_ref[...].astype(BF)

    act = pl.BlockSpec((1, ts, Fb), lambda j, s: (j, s, 0))
    out = pl.BlockSpec((1, Fb, D), lambda j, s: (j, 0, 0))
    shp = jax.ShapeDtypeStruct((nb, Fb, D), BF)
    return _call(body, grid=(nb, ns), name=name, args=[h, dg, du], comm=comm,
                 in_specs=[pl.BlockSpec((ts, D), lambda j, s: (s, 0)), act, act], out_specs=[out, out], out_shape=[shp, shp],
                 scratch=[pltpu.VMEM((Fb, D), F32), pltpu.VMEM((Fb, D), F32)])


def _dh_rms_bwd(pairs, blocked, tk, x, gn, dxo, out_scale, name, comm=(), tail=None):
    S, D = x.shape
    nk = pairs[0][0].shape[0] if blocked else pairs[0][0].shape[1] // tk
    tm = _tile(S, 512)
    npair = len(pairs)
    assert blocked or (npair == 1 and tail is not None and nk >= 2)
    nin = 2 * npair + (0 if blocked else 1)

    def body(*refs):
        ins = refs[:nin]
        x_ref, gn_ref, dxo_ref, dx_ref, dxb_ref, dgn_ref, acc_ref = refs[nin:]
        i, k = pl.program_id(0), pl.program_id(1)

        @pl.when(k == 0)
        def _():
            acc_ref[...] = jnp.zeros_like(acc_ref)

        if blocked:
            _acc_dots(acc_ref, [(ins[2 * p][0], lambda cols, r=ins[2 * p + 1]: r[0, :, cols]) for p in range(npair)])
        else:
            @pl.when(k < nk - 1)
            def _():
                _acc_dots(acc_ref, [(ins[0][...], lambda cols: ins[1][:, cols])])

            @pl.when(k == nk - 1)
            def _():
                _acc_dots(acc_ref, [(ins[0][...], lambda cols: ins[2][:, cols])])

        @pl.when((k == nk - 1) & (i == 0))
        def _():
            dgn_ref[...] = jnp.zeros_like(dgn_ref)

        @pl.when(k == nk - 1)
        def _():
            def chunk(t, carry):
                rows = pl.ds(pl.multiple_of(t * EPI_ROWS, EPI_ROWS), EPI_ROWS)
                xv = x_ref[rows, :]
                r = lax.rsqrt(jnp.mean(xv * xv, axis=-1, keepdims=True) + NORM_EPS)
                xh = xv * r
                dh = acc_ref[rows, :]
                dgn_ref[...] += jnp.sum(dh * xh, axis=0, keepdims=True)
                dxh = dh * gn_ref[...]
                dx = dxo_ref[rows, :] + r * (dxh - xh * jnp.mean(dxh * xh, axis=-1, keepdims=True))
                dx_ref[rows, :] = dx
                dxb_ref[rows, :] = (out_scale * dx).astype(BF)
                return carry

            lax.fori_loop(0, tm // EPI_ROWS, chunk, 0)

    if blocked:
        mats = [pl.BlockSpec((1, tm, tk), lambda i, k: (k, i, 0)), pl.BlockSpec((1, tk, D), lambda i, k: (k, 0, 0))] * npair
        flat = [t for pr in pairs for t in pr]
    else:
        mats = [pl.BlockSpec((tm, tk), lambda i, k: (i, k)), pl.BlockSpec((tk, D), lambda i, k: (jnp.minimum(k, nk - 2), 0)),
                pl.BlockSpec((tk, D), lambda i, k: (0, 0))]
        flat = [*pairs[0], tail]
    row = pl.BlockSpec((tm, D), lambda i, k: (i, 0))
    once = row
    vec = pl.BlockSpec((1, D), lambda i, k: (0, 0))
    return _call(body, grid=(S // tm, nk), name=name, args=[*flat, x, gn, dxo], comm=comm,
                 in_specs=mats + [once, vec, once], out_specs=[row, row, vec],
                 out_shape=[jax.ShapeDtypeStruct((S, D), F32), jax.ShapeDtypeStruct((S, D), BF), jax.ShapeDtypeStruct((1, D), F32)],
                 scratch=[pltpu.VMEM((tm, D), F32)])


def _ffn_dh(dg, du, wgT, wuT, x, gn, dxo, name, comm=()):
    S, D = x.shape
    nb, _, Fb = dg.shape
    tm = _tile(S, FFN_ROWS)
    ni, er = S // tm, tm // nb
    assert er % 16 == 0

    def body(dg_ref, wg_ref, du_ref, wu_ref, x_ref, gn_ref, dxo_ref, dx_ref, dxb_ref, dgn_ref, acc_ref):
        i, k = pl.program_id(0), pl.program_id(1)
        slot = i % 2

        @pl.when((i == 0) & (k == 0))
        def _():
            acc_ref[...] = jnp.zeros_like(acc_ref)
            dgn_ref[...] = jnp.zeros_like(dgn_ref)

        @pl.when((i > 0) & (k == 0))
        def _():
            acc_ref[slot] = jnp.zeros((tm, D), F32)

        def finish_rows():
            rows = pl.ds(pl.multiple_of(k * er, er), er)
            xv = x_ref[...]
            r = lax.rsqrt(jnp.mean(xv * xv, axis=-1, keepdims=True) + NORM_EPS)
            xh = xv * r
            dh = acc_ref[1 - slot, rows, :]
            dgn_ref[...] += jnp.where(i > 0, jnp.sum(dh * xh, axis=0, keepdims=True), 0.0)
            dxh = dh * gn_ref[...]
            dx = dxo_ref[...] + r * (dxh - xh * jnp.mean(dxh * xh, axis=-1, keepdims=True))
            dx_ref[...] = dx
            dxb_ref[...] = dx.astype(BF)

        @pl.when(i < ni)
        def _():
            _acc_dots(acc_ref.at[slot], [(dg_ref[0], lambda cols: wg_ref[0, :, cols]), (du_ref[0], lambda cols: wu_ref[0, :, cols])])
            finish_rows()

        @pl.when(i == ni)
        def _():
            finish_rows()

    last = lambda i, k: jnp.where(i == ni, nb - 1, k)
    act = pl.BlockSpec((1, tm, Fb), lambda i, k: (last(i, k), jnp.minimum(i, ni - 1), 0))
    w = pl.BlockSpec((1, Fb, D), lambda i, k: (last(i, k), 0, 0))
    prev = pl.BlockSpec((er, D), lambda i, k: (jnp.maximum(i - 1, 0) * nb + jnp.where(i > 0, k, 0), 0))
    vec = pl.BlockSpec((1, D), lambda i, k: (0, 0))
    return _call(body, grid=(ni + 1, nb), name=name, args=[dg, wgT, du, wuT, x, gn, dxo], comm=comm,
                 in_specs=[act, w, act, w, prev, vec, prev], out_specs=[prev, prev, vec],
                 out_shape=[jax.ShapeDtypeStruct((S, D), F32), jax.ShapeDtypeStruct((S, D), BF), jax.ShapeDtypeStruct((1, D), F32)],
                 scratch=[pltpu.VMEM((2, tm, D), F32)])


def _proj_split(a, bT, tm, tn, split, name, comm=()):
    M, K = a.shape
    N = bT.shape[0]
    n_first = split // tn

    def body(a_ref, b_ref, first_ref, rest_ref):
        n = pl.program_id(1)
        y = _nt(a_ref[...], b_ref[...])

        @pl.when(n < n_first)
        def _():
            first_ref[...] = y

        @pl.when(n >= n_first)
        def _():
            rest_ref[...] = y.astype(BF)

    return _call(body, grid=(M // tm, N // tn), name=name, args=[a, bT], comm=comm,
                 in_specs=[pl.BlockSpec((tm, K), lambda i, n: (i, 0)), pl.BlockSpec((tn, K), lambda i, n: (n, 0))],
                 out_specs=[pl.BlockSpec((tm, tn), lambda i, n: (i, jnp.minimum(n, n_first - 1))),
                            pl.BlockSpec((tm, tn), lambda i, n: (i, jnp.maximum(n - n_first, 0)))],
                 out_shape=[jax.ShapeDtypeStruct((M, split), F32), jax.ShapeDtypeStruct((M, N - split), BF)])


def _mm_tn(a, b, tm, tn, ts, blocked, name, comm=(), mrows=None):
    S, M = a.shape[0], (a.shape[1] if mrows is None else mrows)
    N = b.shape[1]
    ns = S // ts
    per_tile = tn // blocked if blocked else 0

    def body(a_ref, b_ref, o_ref, acc_ref):
        s = pl.program_id(2)

        @pl.when(s == 0)
        def _():
            acc_ref[...] = jnp.zeros_like(acc_ref)

        acc_ref[...] += _tn(a_ref[...], b_ref[...])

        @pl.when(s == ns - 1)
        def _():
            if blocked:
                for t in range(per_tile):
                    o_ref[t] = acc_ref[:, t * blocked:(t + 1) * blocked].astype(BF)
            else:
                o_ref[...] = acc_ref[...].astype(BF)

    if blocked:
        ospec = pl.BlockSpec((per_tile, tm, blocked), lambda i, n, s: (n, i, 0))
        oshape = jax.ShapeDtypeStruct((N // blocked, M, blocked), BF)
    else:
        ospec = pl.BlockSpec((tm, tn), lambda i, n, s: (i, n))
        oshape = jax.ShapeDtypeStruct((M, N), BF)
    return _call(body, grid=(M // tm, N // tn, ns), name=name, args=[a, b], comm=comm,
                 in_specs=[pl.BlockSpec((ts, tm), lambda i, n, s: (s, i)), pl.BlockSpec((ts, tn), lambda i, n, s: (s, n))],
                 out_specs=[ospec], out_shape=[oshape], scratch=[pltpu.VMEM((tm, tn), F32)])


def _rope_tables(S):
    half = ROPE_DIM // 2
    inv_freq = ROPE_THETA ** (-jnp.arange(0, ROPE_DIM, 2, dtype=F32) / ROPE_DIM)
    ang = jnp.arange(S, dtype=F32)[:, None] * inv_freq[None, :]
    cos, sin = jnp.cos(ang), jnp.sin(ang)
    zeros = jnp.zeros((S, HEAD_DIM - ROPE_DIM), F32)
    c = jnp.concatenate([cos, cos, jnp.ones((S, HEAD_DIM - ROPE_DIM), F32)], axis=1)
    sm = jnp.concatenate([-sin, jnp.zeros((S, half), F32), zeros], axis=1)
    sp = jnp.concatenate([jnp.zeros((S, half), F32), sin, zeros], axis=1)
    return c, sm, sp


def _rope(t, c, sm, sp):
    return t * c + pltpu.roll(t, HEAD_DIM - ROPE_DIM // 2, 1) * sm + pltpu.roll(t, ROPE_DIM // 2, 1) * sp


def _rope_t(dy, c, sm, sp):
    return dy * c + pltpu.roll(dy * sm, ROPE_DIM // 2, 1) + pltpu.roll(dy * sp, HEAD_DIM - ROPE_DIM // 2, 1)


def _att_mask(i):
    qi = lax.broadcasted_iota(jnp.int32, (BLK, 2 * BLK), 0)
    kj = lax.broadcasted_iota(jnp.int32, (BLK, 2 * BLK), 1)
    diff = qi + BLK - kj
    first_key = jnp.where(i > 0, 0, BLK)
    return (diff >= 0) & (diff <= BLK) & (kj >= first_key)


def _res_rows(r, i, n, d):
    if d == 1:
        return pl.ds(pl.multiple_of(i * n, n), n)
    return pl.ds(r + i * (n * d), n, stride=d)


def _att_specs(S, gi):
    def sect(off):
        base = (off + gi * GROUP_W) // HEAD_DIM
        return _once((S, HEAD_DIM), lambda hh: (0, base + hh))

    tab = pl.BlockSpec((S, HEAD_DIM), lambda hh: (0, 0))
    head = pl.BlockSpec((S, HEAD_DIM), lambda hh: (0, hh))
    return sect, tab, head


def _each_residue(d, fn):
    if d == 1:
        fn(0)
    else:
        lax.fori_loop(0, d, lambda r, carry: (fn(r), carry)[1], 0)


def _att_fwd(qkv, tabs, gi, d, name, comm=()):
    S = qkv.shape[0]
    L = S // d
    sect, tab, head = _att_specs(S, gi)
    nblk = L // BLK
    scale = HEAD_DIM ** -0.5

    def body(q_ref, k_ref, v_ref, c_ref, sm_ref, sp_ref, o_ref, lse_ref, qr, kp, vp):
        kp[pl.ds(0, BLK), :] = jnp.zeros((BLK, HEAD_DIM), BF)
        vp[pl.ds(0, BLK), :] = jnp.zeros((BLK, HEAD_DIM), BF)

        def residue(r):
            res = _res_rows(r, 0, L, d)
            c, sm, sp = c_ref[res, :], sm_ref[res, :], sp_ref[res, :]
            qr[...] = _rope(q_ref[res, :], c, sm, sp).astype(BF)
            kp[pl.ds(BLK, L), :] = _rope(k_ref[res, :], c, sm, sp).astype(BF)
            vp[pl.ds(BLK, L), :] = v_ref[res, :].astype(BF)

            def blk(i, carry):
                r0 = pl.multiple_of(i * BLK, BLK)
                s = _nt(qr[pl.ds(r0, BLK), :], kp[pl.ds(r0, 2 * BLK), :]) * scale
                s = jnp.where(_att_mask(i), s, NEG)
                m = jnp.max(s, axis=-1, keepdims=True)
                p = jnp.exp(s - m)
                l = jnp.sum(p, axis=-1, keepdims=True)
                out = _res_rows(r, i, BLK, d)
                o_ref[out, :] = _nn(p.astype(BF), vp[pl.ds(r0, 2 * BLK), :]) / l
                lse_ref[out, :] = jnp.broadcast_to(m + jnp.log(l), (BLK, HEAD_DIM))
                return carry

            lax.fori_loop(0, nblk, blk, 0, unroll=min(4, nblk))

        _each_residue(d, residue)

    shp = jax.ShapeDtypeStruct((S, GROUP_W), F32)
    return _call(body, grid=(HEADS_PER_GROUP,), name=name, args=[qkv, qkv, qkv, *tabs], comm=comm,
                 in_specs=[sect(Q_OFF), sect(K_OFF), sect(V_OFF), tab, tab, tab], out_specs=[head, head], out_shape=[shp, shp],
                 scratch=[pltpu.VMEM((L, HEAD_DIM), BF), pltpu.VMEM((L + BLK, HEAD_DIM), BF), pltpu.VMEM((L + BLK, HEAD_DIM), BF)])


def _att_combine(os, lses, name):
    S = os[0].shape[0]
    tm = _tile(S, 512)

    def body(o0, o1, o2, l0, l1, l2, oa_ref, lse_ref):
        a, b, c = l0[...], l1[...], l2[...]
        mx = jnp.maximum(jnp.maximum(a, b), c)
        wa, wb, wc = jnp.exp(a - mx), jnp.exp(b - mx), jnp.exp(c - mx)
        den = wa + wb + wc
        oa_ref[...] = ((wa * o0[...] + wb * o1[...] + wc * o2[...]) / den).astype(BF)
        lse_ref[...] = mx + jnp.log(den)

    row = pl.BlockSpec((tm, GROUP_W), lambda i: (i, 0))
    return _call(body, grid=(S // tm,), name=name, args=[*os, *lses], in_specs=[row] * 6, out_specs=[row, row],
                 out_shape=[jax.ShapeDtypeStruct((S, GROUP_W), BF), jax.ShapeDtypeStruct((S, GROUP_W), F32)])[0]


def _att_bwd(qkv, tabs, do, lse, dvec, gi, d, name, comm=()):
    S = qkv.shape[0]
    L = S // d
    sect, tab, head = _att_specs(S, gi)
    stat = _once((S, HEAD_DIM), lambda hh: (0, hh))
    nblk = L // BLK
    scale = HEAD_DIM ** -0.5

    def body(q_ref, k_ref, v_ref, c_ref, sm_ref, sp_ref, do_ref, lse_ref, dv_ref, dq_out, dk_out, dv_out, qr, kp, vp, dkp, dvp):
        kp[pl.ds(0, BLK), :] = jnp.zeros((BLK, HEAD_DIM), BF)
        vp[pl.ds(0, BLK), :] = jnp.zeros((BLK, HEAD_DIM), BF)

        def residue(r):
            res = _res_rows(r, 0, L, d)
            c, sm, sp = c_ref[res, :], sm_ref[res, :], sp_ref[res, :]
            qr[...] = _rope(q_ref[res, :], c, sm, sp).astype(BF)
            kp[pl.ds(BLK, L), :] = _rope(k_ref[res, :], c, sm, sp).astype(BF)
            vp[pl.ds(BLK, L), :] = v_ref[res, :].astype(BF)
            dkp[...] = jnp.zeros_like(dkp)
            dvp[...] = jnp.zeros_like(dvp)

            def blk(i, carry):
                r0 = pl.multiple_of(i * BLK, BLK)
                rows, win, pos = pl.ds(r0, BLK), pl.ds(r0, 2 * BLK), _res_rows(r, i, BLK, d)
                q, kw, vw, dob = qr[rows, :], kp[win, :], vp[win, :], do_ref[pos, :].astype(BF)
                s = jnp.where(_att_mask(i), _nt(q, kw) * scale, NEG)
                p = jnp.exp(s - lse_ref[pos, :][:, :1])
                ds = p * (_nt(dob, vw) - dv_ref[pos, :][:, :1]) * scale
                dsb = ds.astype(BF)
                dq_out[pos, :] = _rope_t(_nn(dsb, kw), c_ref[pos, :], sm_ref[pos, :], sp_ref[pos, :])
                dkp[win, :] += _tn(dsb, q)
                dvp[win, :] += _tn(p.astype(BF), dob)
                return carry

            lax.fori_loop(0, nblk, blk, 0, unroll=min(4, nblk))
            dk_out[res, :] = _rope_t(dkp[pl.ds(BLK, L), :], c, sm, sp)
            dv_out[res, :] = dvp[pl.ds(BLK, L), :]

        _each_residue(d, residue)

    shp = jax.ShapeDtypeStruct((S, GROUP_W), F32)
    return _call(body, grid=(HEADS_PER_GROUP,), name=name, args=[qkv, qkv, qkv, *tabs, do, lse, dvec], comm=comm,
                 in_specs=[sect(Q_OFF), sect(K_OFF), sect(V_OFF), tab, tab, tab, stat, stat, stat],
                 out_specs=[head, head, head], out_shape=[shp, shp, shp],
                 scratch=[pltpu.VMEM((L, HEAD_DIM), BF), pltpu.VMEM((L + BLK, HEAD_DIM), BF), pltpu.VMEM((L + BLK, HEAD_DIM), BF),
                          pltpu.VMEM((L + BLK, HEAD_DIM), F32), pltpu.VMEM((L + BLK, HEAD_DIM), F32)])


def _sg_parts(u_ref, vs_ref, g_ref, b_ref):
    uv = u_ref[...].astype(F32)
    vv = vs_ref[...].astype(F32)
    cv = _gauss_cdf(vv)
    vg = vv * cv
    mu = jnp.mean(vg, axis=-1, keepdims=True)
    vc = vg - mu
    rs = lax.rsqrt(jnp.mean(vc * vc, axis=-1, keepdims=True) + LN_EPS)
    y = vc * rs
    return uv, vv, cv, rs, y, y * g_ref[...] + b_ref[...]


def _sg_wmask():
    t = lax.broadcasted_iota(jnp.int32, (BLK, BLK), 0)
    s = lax.broadcasted_iota(jnp.int32, (BLK, BLK), 1)
    return s <= t


def _sg_fwd(proj, sgw, sgbT, lng, lnb, name):
    S, P = proj.shape

    def body(u_ref, vs_ref, w_ref, bt_ref, g_ref, b_ref, z_ref):
        uv, _, _, _, _, vln = _sg_parts(u_ref, vs_ref, g_ref, b_ref)
        ug = uv * _gauss_cdf(uv)
        vb = vln.astype(BF)
        mask = _sg_wmask()
        bt = bt_ref[...]
        for g in range(SG_GROUPS):
            cols = slice(g * BLK, (g + 1) * BLK)
            w = jnp.where(mask, w_ref[g], 0.0).astype(BF)
            sp = _nn(w, vb[:, cols]) + bt[:, g:g + 1]
            z_ref[:, cols] = (ug[:, cols] * sp).astype(BF)

    tile = lambda off: pl.BlockSpec((BLK, SG_W), lambda i: (i, off // SG_W))
    full = lambda shape: pl.BlockSpec(shape, lambda i: (0,) * len(shape))
    return _call(body, grid=(S // BLK,), name=name, args=[proj, proj, sgw, sgbT, lng, lnb],
                 in_specs=[tile(R_U), tile(R_VS), full((SG_GROUPS, BLK, BLK)), full((BLK, BLK)), full((1, SG_W)), full((1, SG_W))],
                 out_specs=[pl.BlockSpec((BLK, SG_W), lambda i: (i, 0))], out_shape=[jax.ShapeDtypeStruct((S, SG_W), BF)])[0][0]


def _sg_bwd(proj, dz, sgw, sgbT, lng, lnb, name):
    S, P = proj.shape

    def body(u_ref, vs_ref, dz_ref, w_ref, bt_ref, g_ref, b_ref, du_ref, dvs_ref, dw_ref, dbt_ref, dg_ref, db_ref, dvln):
        @pl.when(pl.program_id(0) == 0)
        def _():
            dw_ref[...] = jnp.zeros_like(dw_ref)
            dbt_ref[...] = jnp.zeros_like(dbt_ref)
            dg_ref[...] = jnp.zeros_like(dg_ref)
            db_ref[...] = jnp.zeros_like(db_ref)

        uv, vv, cv, rs, y, vln = _sg_parts(u_ref, vs_ref, g_ref, b_ref)
        cu = _gauss_cdf(uv)
        ug = uv * cu
        dug = _gelu_grad(uv, cu)
        vb = vln.astype(BF)
        dzv = dz_ref[...].astype(F32)
        dsp = dzv * ug
        dspb = dsp.astype(BF)
        mask = _sg_wmask()
        bt = bt_ref[...]
        lane = lax.broadcasted_iota(jnp.int32, (BLK, BLK), 1)
        dbt = jnp.zeros((BLK, BLK), F32)
        for g in range(SG_GROUPS):
            cols = slice(g * BLK, (g + 1) * BLK)
            w = jnp.where(mask, w_ref[g], 0.0).astype(BF)
            sp = _nn(w, vb[:, cols]) + bt[:, g:g + 1]
            du_ref[:, cols] = (dzv[:, cols] * sp * dug[:, cols]).astype(BF)
            dw_ref[g] += jnp.where(mask, _nt(dspb[:, cols], vb[:, cols]), 0.0)
            dbt = dbt + jnp.where(lane == g, jnp.sum(dsp[:, cols], axis=-1, keepdims=True), 0.0)
            dvln[:, cols] = _tn(w, dspb[:, cols])
        dbt_ref[...] += dbt
        dvl = dvln[...]
        dg_ref[...] += jnp.sum(dvl * y, axis=0, keepdims=True)
        db_ref[...] += jnp.sum(dvl, axis=0, keepdims=True)
        dy = dvl * g_ref[...]
        dvg = rs * (dy - jnp.mean(dy, axis=-1, keepdims=True) - y * jnp.mean(dy * y, axis=-1, keepdims=True))
        dvs_ref[...] = (dvg * _gelu_grad(vv, cv)).astype(BF)

    tile = lambda off: pl.BlockSpec((BLK, SG_W), lambda i: (i, off // SG_W))
    full = lambda shape: pl.BlockSpec(shape, lambda i: (0,) * len(shape))
    row = pl.BlockSpec((BLK, SG_W), lambda i: (i, 0))
    return _call(body, grid=(S // BLK,), name=name, args=[proj, proj, dz, sgw, sgbT, lng, lnb],
                 in_specs=[tile(R_U), tile(R_VS), row, full((SG_GROUPS, BLK, BLK)), full((BLK, BLK)), full((1, SG_W)), full((1, SG_W))],
                 out_specs=[row, row, full((SG_GROUPS, BLK, BLK)), full((BLK, BLK)), full((1, SG_W)), full((1, SG_W))],
                 out_shape=[jax.ShapeDtypeStruct((S, SG_W), BF), jax.ShapeDtypeStruct((S, SG_W), BF),
                            jax.ShapeDtypeStruct((SG_GROUPS, BLK, BLK), F32), jax.ShapeDtypeStruct((BLK, BLK), F32),
                            jax.ShapeDtypeStruct((1, SG_W), F32), jax.ShapeDtypeStruct((1, SG_W), F32)],
                 scratch=[pltpu.VMEM((BLK, SG_W), F32)])[0]


def _gate_merge(oatt, z, watt, wsg, proj, name, comm=()):
    S = oatt.shape[0]
    nb, _, Db = watt.shape
    D = nb * Db
    tm = _tile(S, 512)
    half = D // 2
    ga, gs = R_GA // half, (R_GA + D) // half

    def body(oa_ref, z_ref, wa_ref, ws_ref, ga0, ga1, gs0, gs1, ya_ref, ys_ref, mg_ref):
        oa, zv = oa_ref[...], z_ref[...]
        for j in range(nb):
            cols = slice(j * Db, (j + 1) * Db)
            g_a, g_s = (ga0, gs0) if j < nb // 2 else (ga1, gs1)
            gcols = slice((j % (nb // 2)) * Db, (j % (nb // 2) + 1) * Db)
            ya = _nn(oa, wa_ref[j])
            ys = _nn(zv, ws_ref[j])
            ya_ref[:, cols] = ya.astype(BF)
            ys_ref[:, cols] = ys.astype(BF)
            mg_ref[:, cols] = (jax.nn.sigmoid(g_a[:, gcols].astype(F32)) * ya + jax.nn.sigmoid(g_s[:, gcols].astype(F32)) * ys).astype(BF)

    out = pl.BlockSpec((tm, D), lambda i: (i, 0))
    gate = lambda b: pl.BlockSpec((tm, half), lambda i: (i, b))
    shp = jax.ShapeDtypeStruct((S, D), BF)
    return _call(body, grid=(S // tm,), name=name, args=[oatt, z, watt, wsg, proj, proj, proj, proj], comm=comm,
                 in_specs=[pl.BlockSpec((tm, GROUP_W), lambda i: (i, 0)), pl.BlockSpec((tm, SG_W), lambda i: (i, 0)),
                           pl.BlockSpec((nb, GROUP_W, Db), lambda i: (0, 0, 0)), pl.BlockSpec((nb, SG_W, Db), lambda i: (0, 0, 0)),
                           gate(ga), gate(ga + 1), gate(gs), gate(gs + 1)],
                 out_specs=[out, out, out], out_shape=[shp, shp, shp])


def _mix_out(merged, wout, x, gn, name):
    S, D = x.shape
    tm = _tile(S, 512)

    def body(m_ref, w_ref, x_ref, gn_ref, xo_ref, hn_ref):
        xo = x_ref[...] + _nn(m_ref[...], w_ref[...])
        r = lax.rsqrt(jnp.mean(xo * xo, axis=-1, keepdims=True) + NORM_EPS)
        xo_ref[...] = xo
        hn_ref[...] = (xo * r * gn_ref[...]).astype(BF)

    row = pl.BlockSpec((tm, D), lambda i: (i, 0))
    return _call(body, grid=(S // tm,), name=name, args=[merged, wout, x, gn],
                 in_specs=[row, pl.BlockSpec((D, D), lambda i: (0, 0)), row, pl.BlockSpec((1, D), lambda i: (0, 0))],
                 out_specs=[row, row], out_shape=[jax.ShapeDtypeStruct((S, D), F32), jax.ShapeDtypeStruct((S, D), BF)])[0]


def _mix_bwd_gate(dmix, wout, ya, ys, proj, name):
    S, D = dmix.shape
    tm, tn = _tile(S, 512), min(512, D // 2)
    half = D // 2
    ga, gs = R_GA // half, (R_GA + D) // half

    def body(dm_ref, w_ref, ya_ref, ys_ref, ga0, ga1, gs0, gs1, dya_ref, dys_ref, dga_ref, dgs_ref):
        dmv = dm_ref[...]
        for c0 in range(0, D, tn):
            cols = slice(c0, c0 + tn)
            g_a, g_s = (ga0, gs0) if c0 < half else (ga1, gs1)
            gcols = slice(c0 % half, c0 % half + tn)
            dm = _nt(dmv, w_ref[cols, :])
            sa = jax.nn.sigmoid(g_a[:, gcols].astype(F32))
            ss = jax.nn.sigmoid(g_s[:, gcols].astype(F32))
            dya_ref[:, cols] = (dm * sa).astype(BF)
            dys_ref[:, cols] = (dm * ss).astype(BF)
            dga_ref[:, cols] = (dm * ya_ref[:, cols].astype(F32) * sa * (1.0 - sa)).astype(BF)
            dgs_ref[:, cols] = (dm * ys_ref[:, cols].astype(F32) * ss * (1.0 - ss)).astype(BF)

    row = pl.BlockSpec((tm, D), lambda i: (i, 0))
    gate = lambda b: pl.BlockSpec((tm, half), lambda i: (i, b))
    shp = jax.ShapeDtypeStruct((S, D), BF)
    return _call(body, grid=(S // tm,), name=name, args=[dmix, wout, ya, ys, proj, proj, proj, proj],
                 in_specs=[row, pl.BlockSpec((D, D), lambda i: (0, 0)), row, row, gate(ga), gate(ga + 1), gate(gs), gate(gs + 1)],
                 out_specs=[row] * 4, out_shape=[shp] * 4)[0]


def _att_sg_dout(dya, dys, watt, wsg, oatt, name, comm=()):
    S, D = dya.shape
    nb, _, Db = watt.shape
    tm = _tile(S, 512)

    def body(dya_ref, dys_ref, wa_ref, ws_ref, oa_ref, do_ref, dz_ref, dvec_ref):
        def back(dy_ref, w_ref, rows):
            tot = None
            for j in range(nb):
                part = _nt(dy_ref[:, j * Db:(j + 1) * Db], w_ref[j, rows, :])
                tot = part if tot is None else tot + part
            return tot

        dov = back(dya_ref, wa_ref, slice(0, GROUP_W))
        do_ref[...] = dov
        for c0 in range(0, SG_W, GROUP_W):
            dz_ref[:, c0:c0 + GROUP_W] = back(dys_ref, ws_ref, slice(c0, c0 + GROUP_W)).astype(BF)
        prod = dov * oa_ref[...].astype(F32)
        for hh in range(HEADS_PER_GROUP):
            cols = slice(hh * HEAD_DIM, (hh + 1) * HEAD_DIM)
            dvec_ref[:, cols] = jnp.broadcast_to(jnp.sum(prod[:, cols], axis=-1, keepdims=True), (tm, HEAD_DIM))

    row = pl.BlockSpec((tm, D), lambda i: (i, 0))
    att = pl.BlockSpec((tm, GROUP_W), lambda i: (i, 0))
    return _call(body, grid=(S // tm,), name=name, args=[dya, dys, watt, wsg, oatt], comm=comm,
                 in_specs=[row, row, pl.BlockSpec((nb, GROUP_W, Db), lambda i: (0, 0, 0)), pl.BlockSpec((nb, SG_W, Db), lambda i: (0, 0, 0)), att],
                 out_specs=[att, pl.BlockSpec((tm, SG_W), lambda i: (i, 0)), att],
                 out_shape=[jax.ShapeDtypeStruct((S, GROUP_W), F32), jax.ShapeDtypeStruct((S, SG_W), BF), jax.ShapeDtypeStruct((S, GROUP_W), F32)])[0]


def _small_allreduce(pack, name):
    R = pack.shape[0]

    def body(p_ref, o_ref, gath, send, recv):
        x, y, c = _place()
        me = 4 * x + 2 * y + c
        gath[me] = p_ref[...]
        copies = []
        for r in range(1, N_DEV):
            px, py, pc = _flip(x, r & 4), _flip(y, r & 2), _flip(c, r & 1)
            peer = 4 * px + 2 * py + pc
            mk = lambda dst: pltpu.make_async_remote_copy(src_ref=p_ref, dst_ref=dst, send_sem=send.at[r - 1], recv_sem=recv.at[r - 1],
                                                          device_id=(px, py, pc), device_id_type=MESH)
            snd = mk(gath.at[me])
            snd.start()
            copies.append((snd, mk(gath.at[peer])))
        for snd, rcv in copies:
            rcv.wait_recv()
            snd.wait_send()
        acc = gath[0]
        for s in range(1, N_DEV):
            acc = acc + gath[s]
        o_ref[...] = acc

    vm = pl.BlockSpec(memory_space=pltpu.VMEM)
    return pl.pallas_call(
        body, name=name, in_specs=[vm], out_specs=vm, out_shape=jax.ShapeDtypeStruct(pack.shape, F32),
        scratch_shapes=[pltpu.VMEM((N_DEV, R, 128), F32), pltpu.SemaphoreType.DMA((7,)), pltpu.SemaphoreType.DMA((7,))],
        compiler_params=pltpu.CompilerParams(vmem_limit_bytes=VMEM_LIMIT),
    )(pack)


def _row_tile(R, C, elems=262144):
    fits = [t for t in range(16, R + 1, 16) if R % t == 0 and t * C <= elems]
    return max(fits) if fits else R


def _pair_add(parts, other, name):
    _, R, C = parts.shape
    tr = _row_tile(R, C, 1048576)

    def body(c_ref, p_ref, o_ref, s_ref):
        s_ref[0] = (p_ref[0].astype(F32) + o_ref[0].astype(F32)).astype(BF)

    core = lax.axis_index("c").astype(jnp.int32).reshape(1)
    return pl.pallas_call(
        body, name=name,
        grid_spec=pltpu.PrefetchScalarGridSpec(
            num_scalar_prefetch=1, grid=(N_CHIP, R // tr),
            in_specs=[pl.BlockSpec((1, tr, C), lambda q, i, c: (2 * q + c[0], i, 0)), pl.BlockSpec((1, tr, C), lambda q, i, c: (q, i, 0))],
            out_specs=pl.BlockSpec((1, tr, C), lambda q, i, c: (q, i, 0))),
        out_shape=jax.ShapeDtypeStruct((N_CHIP, R, C), BF),
        compiler_params=pltpu.CompilerParams(dimension_semantics=("arbitrary", "arbitrary"), vmem_limit_bytes=VMEM_LIMIT),
    )(core, parts, other)


def _adamw(parts, w, m, v, name):
    ns, R, C = parts.shape
    tr = _row_tile(R, C, 524288)
    c1 = 1.0 - ADAM_B1 ** ADAM_STEP
    c2 = 1.0 - ADAM_B2 ** ADAM_STEP

    def body(p_ref, w_ref, m_ref, v_ref, g_ref, d_ref, nm_ref, nv_ref):
        g = p_ref[0].astype(F32)
        for s in range(1, ns):
            g = g + p_ref[s].astype(F32)
        mn = ADAM_B1 * m_ref[...] + (1.0 - ADAM_B1) * g
        vn = ADAM_B2 * v_ref[...] + (1.0 - ADAM_B2) * (g * g)
        g_ref[...] = g
        nm_ref[...] = mn
        nv_ref[...] = vn
        d_ref[...] = -ADAM_LR * ((mn / c1) / (jnp.sqrt(vn / c2) + ADAM_EPS) + ADAM_WD * w_ref[...])

    row = pl.BlockSpec((tr, C), lambda i: (i, 0))
    shp = jax.ShapeDtypeStruct((R, C), F32)
    return _call(body, grid=(R // tr,), name=name, args=[parts, w, m, v],
                 in_specs=[pl.BlockSpec((ns, tr, C), lambda i: (0, i, 0)), row, row, row], out_specs=[row] * 4, out_shape=[shp] * 4)[0]


def _pad_rows(a, rows):
    return jnp.pad(a, ((0, rows - a.shape[0]), (0, 0)))


def kernel(x, ffn1_norm, ffn1_w_gate, ffn1_w_up, ffn1_w_down, mix_norm, w_in, sg_ln_g, sg_ln_b, sg_w, sg_b, w_att_out, w_sg_out, w_out, ffn2_norm, ffn2_w_gate, ffn2_w_up, ffn2_w_down, final_norm, loss_target, m_ffn1_norm, m_ffn1_w_gate, m_ffn1_w_up, m_ffn1_w_down, m_mix_norm, m_w_in, m_sg_ln_g, m_sg_ln_b, m_sg_w, m_sg_b, m_w_att_out, m_w_sg_out, m_w_out, m_ffn2_norm, m_ffn2_w_gate, m_ffn2_w_up, m_ffn2_w_down, m_final_norm, v_ffn1_norm, v_ffn1_w_gate, v_ffn1_w_up, v_ffn1_w_down, v_mix_norm, v_w_in, v_sg_ln_g, v_sg_ln_b, v_sg_w, v_sg_b, v_w_att_out, v_w_sg_out, v_w_out, v_ffn2_norm, v_ffn2_w_gate, v_ffn2_w_up, v_ffn2_w_down, v_final_norm):
    S, D = x.shape[1], x.shape[2]
    Pb = w_in.shape[2]
    P = N_DEV * Pb
    assert P == GA_OFF + 2 * D and D % (N_DEV * 128) == 0 and S % (BLK * DILATIONS[-1]) == 0
    xs, tgt = x[0], loss_target[0]

    sharded = dict(ffn1_w_gate=ffn1_w_gate, ffn1_w_up=ffn1_w_up, ffn1_w_down=ffn1_w_down, w_in=w_in, w_att_out=w_att_out,
                   w_sg_out=w_sg_out, w_out=w_out, ffn2_w_gate=ffn2_w_gate, ffn2_w_up=ffn2_w_up, ffn2_w_down=ffn2_w_down)
    cols = ("ffn1_w_gate", "ffn1_w_up", "w_in", "ffn2_w_gate", "ffn2_w_up")
    local = lambda n, a: a[0].T if n in cols else a[0]
    back = lambda n, a: a.T[None] if n in cols else a[None]
    wloc = {n: local(n, w) for n, w in sharded.items()}
    sb = {n: w.astype(BF) for n, w in wloc.items()}

    (h1,), ((wg1,),) = _rms_fwd(xs, ffn1_norm, "rms1", comm=[_Gather([sb["ffn1_w_gate"]], 1.0, 1.0)])
    (g1,), ((wu1,),) = _ffn_gate(h1, wg1, "ffn1_gate", comm=[_Gather([sb["ffn1_w_up"]], 0.9, 0.55)])
    (u1, a1), ((wd1,),) = _ffn_up_act(h1, wu1, g1, "ffn1_up_act", comm=[_Gather([sb["ffn1_w_down"]], 0.9, 0.55)])
    (x1, h2), ((winT8,),) = _ffn_down_norm(a1, wd1, xs, mix_norm, "ffn1_down", comm=[_Gather([sb["w_in"]], 1.0, 0.7)])
    winT = winT8.reshape(P, D)
    (qkv, rest), ((wg2, wu2),) = _proj_split(h2, winT, _tile(S, 2048), 512, U_OFF, "proj",
                                           comm=[_Gather([sb["ffn2_w_gate"], sb["ffn2_w_up"]], 0.85, 0.5)])
    tabs = _rope_tables(S)
    rides = [[_Gather([sb["w_att_out"], sb["w_sg_out"]], 0.9, 0.5)], [_Gather([sb["w_out"]], 0.85, 0.45)], []]
    os, lses, late = [], [], []
    for gi, d in enumerate(DILATIONS):
        (o, l), got_here = _att_fwd(qkv, tabs, gi, d, f"att_fwd{gi}", comm=rides[gi])
        late += [w for g in got_here for w in g]
        os.append(o)
        lses.append(l)
    watt, wsg, wout8 = late
    wout = wout8.reshape(D, D)
    oatt, lse = _att_combine(os, lses, "att_combine")
    sgw = sg_w[0]
    sgbT = jnp.pad(sg_b[0].T, ((0, 0), (0, BLK - SG_GROUPS)))
    z = _sg_fwd(rest, sgw, sgbT, sg_ln_g, sg_ln_b, "sg_fwd")
    (ya, ys, merged), _ = _gate_merge(oatt, z, watt, wsg, rest, "gate_merge")
    x2, h3 = _mix_out(merged, wout, x1, ffn2_norm, "mix_out")
    (g3, u3, a3), ((wd2,),) = _ffn_up(h3, wg2, wu2, "ffn2_up", comm=[_Gather([sb["ffn2_w_down"]], 0.6, 0.35)])
    dx3, dyb3, d_final, loss_part = _ffn_down_loss(a3, wd2, x2, final_norm.reshape(1, D), tgt, "ffn2_down_loss")

    Fb = wg2.shape[1]
    Db = watt.shape[2]
    p_pad = -(-P // PROJ_TK) * PROJ_TK
    win_tail = _pad_rows(winT[p_pad - PROJ_TK:], PROJ_TK)
    (dg3, du3), _ = _ffn_bwd_act(dyb3, wd2, g3, u3, "ffn2_bwd_act")
    (dwd2,), _ = _ffn_dwd(a3, dyb3, "ffn2_dwd")
    (dwg2, dwu2), _ = _ffn_dwgu(h3, dg3, du3, "ffn2_dwgu")
    ffn2_parts = [dwd2, dwg2, dwu2]
    (dx2, dmixb, d_ffn2n), (ffn2_other,) = _ffn_dh(dg3, du3, wg2, wu2, x2, ffn2_norm, dx3, "ffn2_dh", comm=[_Swap(ffn2_parts)])
    ffn2_sums = [_pair_add(p, o, f"pair_ffn2_{i}") for i, (p, o) in enumerate(zip(ffn2_parts, ffn2_other))]

    dya, dys, dga, dgs = _mix_bwd_gate(dmixb, wout, ya, ys, rest, "mix_bwd_gate")
    (dwout,), _ = _mm_tn(merged, dmixb, _tile(D, 1024), _tile(D, 1024), _tile(S, 1024), False, "dw_out")
    do, dz, dvec = _att_sg_dout(dya, dys, watt, wsg, oatt, "att_sg_dout")
    (dwatt,), _ = _mm_tn(oatt, dya, GROUP_W, 2 * Db, _tile(S, 1024), Db, "dw_att")
    (dwsg,), _ = _mm_tn(z, dys, SG_W, 2 * Db, _tile(S, 1024), Db, "dw_sg")
    mix_parts = [dwout.reshape(N_DEV, D // N_DEV, D), dwatt, dwsg]
    du, dvs, d_sgw, d_sgbT, d_lng, d_lnb = _sg_bwd(rest, dz, sgw, sgbT, sg_ln_g, sg_ln_b, "sg_bwd")
    dqs, dks, dvs_att, ffn2_got = [], [], [], []
    for gi, d in enumerate(DILATIONS):
        ride = [_Ici([ffn2_sums[0]])] if gi == 2 else []
        (dq, dk, dv), got_here = _att_bwd(qkv, tabs, do, lse, dvec, gi, d, f"att_bwd{gi}", comm=ride)
        ffn2_got += [g[0] for g in got_here]
        dqs.append(dq)
        dks.append(dk)
        dvs_att.append(dv)
    dproj = jnp.concatenate([t.astype(BF) for t in dqs + dks + dvs_att] + [du, dvs, dga, dgs, jnp.zeros((S, p_pad - P), BF)], axis=1)
    (dx1, dyb1, d_mixn), (ffn2_rest, mix_other) = _dh_rms_bwd([(dproj, winT)], False, PROJ_TK, x1, mix_norm, dx2, 0.5, "proj_dh",
                                                            comm=[_Ici(ffn2_sums[1:]), _Swap(mix_parts)], tail=win_tail)
    ffn2_got += ffn2_rest
    mix_sums = [_pair_add(p, o, f"pair_mix_{i}") for i, (p, o) in enumerate(zip(mix_parts, mix_other))]
    (dwd1,), (mix_got,) = _ffn_dwd(a1, dyb1, "ffn1_dwd", comm=[_Ici(mix_sums)])
    rows = lambda a: a.reshape(-1, 128)
    pad8 = lambda a: _pad_rows(a, -(-a.shape[0] // 8) * 8)
    small = [("sg_w", rows(d_sgw), sg_w, m_sg_w, v_sg_w), ("mix_norm", rows(d_mixn), mix_norm, m_mix_norm, v_mix_norm),
             ("ffn2_norm", rows(d_ffn2n), ffn2_norm, m_ffn2_norm, v_ffn2_norm), ("final_norm", rows(d_final), final_norm, m_final_norm, v_final_norm),
             ("sg_ln_g", rows(d_lng), sg_ln_g, m_sg_ln_g, v_sg_ln_g), ("sg_ln_b", rows(d_lnb), sg_ln_b, m_sg_ln_b, v_sg_ln_b),
             ("sg_b", d_sgbT[:, :SG_GROUPS].T, sg_b, m_sg_b, v_sg_b)]
    gpack = jnp.concatenate([pad8(g) for _, g, _, _, _ in small] + [pad8(loss_part)], axis=0)
    (dwin,), ((wd1_other,), (gpacks,)) = _mm_tn(dproj, h2, 512, D, S, False, "dw_in", mrows=P,
                                              comm=[_Swap([dwd1]), _Spread([gpack])])
    dwin = dwin.reshape(N_DEV, Pb, D)
    wd1_sum = _pair_add(dwd1, wd1_other, "pair_wd1")
    (dg1, du1), ((wd1_got,), (win_other,)) = _ffn_bwd_act(dyb1, wd1, g1, u1, "ffn1_bwd_act", comm=[_Ici([wd1_sum]), _Swap([dwin])])
    win_sum = _pair_add(dwin, win_other, "pair_win")
    (dwg1, dwu1), ((win_got,),) = _ffn_dwgu(h1, dg1, du1, "ffn1_dwgu", comm=[_Ici([win_sum])])
    gu_parts = [dwg1, dwu1]
    gu_other = _comm_only(_Swap(gu_parts), "swap_gu1")
    gu_sums = [_pair_add(p, o, f"pair_gu1_{i}") for i, (p, o) in enumerate(zip(gu_parts, gu_other))]
    (dx0, _, d_ffn1n), (gu_got,) = _ffn_dh(dg1, du1, wg1, wu1, xs, ffn1_norm, dx1, "ffn1_dh", comm=[_Ici(gu_sums)])

    got = dict(ffn2_w_down=ffn2_got[0], ffn2_w_gate=ffn2_got[1], ffn2_w_up=ffn2_got[2], w_out=mix_got[0], w_att_out=mix_got[1],
               w_sg_out=mix_got[2], w_in=win_got, ffn1_w_gate=gu_got[0], ffn1_w_up=gu_got[1], ffn1_w_down=wd1_got)
    moments = dict(ffn1_w_gate=(m_ffn1_w_gate, v_ffn1_w_gate), ffn1_w_up=(m_ffn1_w_up, v_ffn1_w_up),
                   ffn1_w_down=(m_ffn1_w_down, v_ffn1_w_down), w_in=(m_w_in, v_w_in), w_att_out=(m_w_att_out, v_w_att_out),
                   w_sg_out=(m_w_sg_out, v_w_sg_out), w_out=(m_w_out, v_w_out), ffn2_w_gate=(m_ffn2_w_gate, v_ffn2_w_gate),
                   ffn2_w_up=(m_ffn2_w_up, v_ffn2_w_up), ffn2_w_down=(m_ffn2_w_down, v_ffn2_w_down))
    res = {}
    for n in sharded:
        mm, vv = moments[n]
        outs = _adamw(got[n], wloc[n], local(n, mm), local(n, vv), "adamw_" + n)
        res[n] = [back(n, o) for o in outs]

    zero8 = jnp.zeros((8, 128), F32)
    wpack = jnp.concatenate([pad8(rows(w)) for _, _, w, _, _ in small] + [zero8], axis=0)
    mpack = jnp.concatenate([pad8(rows(m)) for _, _, _, m, _ in small] + [zero8], axis=0)
    vpack = jnp.concatenate([pad8(rows(v)) for _, _, _, _, v in small] + [zero8], axis=0)
    packs = _adamw(gpacks, wpack, mpack, vpack, "adamw_small")
    off = 0
    for n, g, w, _, _ in small:
        r = g.shape[0]
        res[n] = [p[off:off + r].reshape(w.shape) for p in packs]
        off += -(-r // 8) * 8
    loss = packs[0][off, 0]
    g_first = _small_allreduce(rows(d_ffn1n), "allreduce_ffn1_norm")
    res["ffn1_norm"] = [p.reshape(ffn1_norm.shape) for p in
                        _adamw(g_first[None], rows(ffn1_norm), rows(m_ffn1_norm), rows(v_ffn1_norm), "adamw_ffn1_norm")]

    order = ["ffn1_norm", "ffn1_w_gate", "ffn1_w_up", "ffn1_w_down", "mix_norm", "w_in", "sg_ln_g", "sg_ln_b", "sg_w", "sg_b",
             "w_att_out", "w_sg_out", "w_out", "ffn2_norm", "ffn2_w_gate", "ffn2_w_up", "ffn2_w_down", "final_norm"]
    return (loss, dx0[None], *[res[n][0] for n in order], *[res[n][1] for n in order], *[res[n][2] for n in order],
            *[res[n][3] for n in order])
```
